```python
import math
import jax, jax.numpy as jnp
from jax import lax
import numpy as np

D_MODEL = 1024
BATCH = 8
SEQ = 4096
DEPTH = 4

D_MIX = D_MODEL
D_FF = 2816
NORM_EPS = 1e-6
CONV_W = 4
A_WIDTH = 384
A_HEADS = 6
A_HEAD_DIM = A_WIDTH // A_HEADS
LRU_C = 8.0
B_WIDTH = 384
B_HEADS = 6
B_HEAD_DIM = B_WIDTH // B_HEADS
B_GROUPS = 2
B_STATE = 128
B_CHUNK = 128
B_CONV_DIM = B_WIDTH + 2 * B_GROUPS * B_STATE
B_IN = B_WIDTH + B_CONV_DIM + B_HEADS
C_WIDTH = 256
C_GROUPS = 4
C_GROUP_DIM = C_WIDTH // C_GROUPS
C_CHUNK = 128
IN_COLS = 2 * A_WIDTH + B_IN + 2 * C_WIDTH

kernel_name = "hybrid_lru_ssd_gmlp_macaron"


def rms_norm(x, g):
    xf = x.astype(jnp.float32)
    y = xf * lax.rsqrt(jnp.mean(xf * xf, axis=-1, keepdims=True) + NORM_EPS)
    return (y * g.astype(jnp.float32)).astype(x.dtype)


def layer_norm(x, g, b):
    xf = x.astype(jnp.float32)
    mu = jnp.mean(xf, axis=-1, keepdims=True)
    xc = xf - mu
    y = xc * lax.rsqrt(jnp.mean(xc * xc, axis=-1, keepdims=True) + NORM_EPS)
    return (y * g.astype(jnp.float32) + b.astype(jnp.float32)).astype(x.dtype)


def swiglu(x, w_gu, w_down):
    g, u = jnp.split(x @ w_gu, 2, axis=-1)
    return (jax.nn.silu(g) * u) @ w_down


def causal_dwconv(x, w, b):
    c = x.shape[-1]
    y = lax.conv_general_dilated(
        x, w[:, None, :].astype(x.dtype), window_strides=(1,),
        padding=((CONV_W - 1, 0),), dimension_numbers=("NWC", "WIO", "NWC"),
        feature_group_count=c)
    return y + b


def rg_lru(x, w_r, b_r, w_i, b_i, lam):
    bsz, s, _ = x.shape
    xh = x.reshape(bsz, s, A_HEADS, A_HEAD_DIM)
    r = jax.nn.sigmoid(jnp.einsum("bshi,hij->bshj", xh, w_r).reshape(bsz, s, A_WIDTH) + b_r)
    i = jax.nn.sigmoid(jnp.einsum("bshi,hij->bshj", xh, w_i).reshape(bsz, s, A_WIDTH) + b_i)
    log_a = -LRU_C * r.astype(jnp.float32) * jax.nn.softplus(-lam.astype(jnp.float32))
    a = jnp.exp(log_a)
    mult = jnp.sqrt(-jnp.expm1(2.0 * log_a))
    u = mult * (i * x).astype(jnp.float32)

    def combine(left, right):
        a1, b1 = left
        a2, b2 = right
        return a1 * a2, a2 * b1 + b2

    _, h = lax.associative_scan(combine, (a, u), axis=1)
    return h.astype(x.dtype)


def segsum(a):
    t = a.shape[-1]
    cs = jnp.cumsum(a, axis=-1)
    diff = cs[..., :, None] - cs[..., None, :]
    mask = jnp.tril(jnp.ones((t, t), dtype=bool))
    return jnp.where(mask, diff, -jnp.inf)


def ssd_mixer(zxbcdt, conv_w, conv_b, dt_bias, a_log, d_skip, norm_g):
    bsz, s, _ = zxbcdt.shape
    f32 = jnp.float32
    nc = s // B_CHUNK
    hpg = B_HEADS // B_GROUPS
    z, xbc, dt = jnp.split(zxbcdt, [B_WIDTH, B_WIDTH + B_CONV_DIM], axis=-1)
    xbc = jax.nn.silu(causal_dwconv(xbc, conv_w, conv_b))
    xs, bm, cm = jnp.split(xbc, [B_WIDTH, B_WIDTH + B_GROUPS * B_STATE], axis=-1)
    dt = jax.nn.softplus(dt.astype(f32) + dt_bias.astype(f32))
    a = -jnp.exp(a_log.astype(f32))
    x_c = xs.astype(f32).reshape(bsz, nc, B_CHUNK, B_GROUPS, hpg, B_HEAD_DIM)
    dt_c = dt.reshape(bsz, nc, B_CHUNK, B_GROUPS, hpg)
    xdt = x_c * dt_c[..., None]
    ad = jnp.moveaxis((dt * a).reshape(bsz, nc, B_CHUNK, B_GROUPS, hpg), 2, -1)
    b_c = bm.astype(f32).reshape(bsz, nc, B_CHUNK, B_GROUPS, B_STATE)
    c_c = cm.astype(f32).reshape(bsz, nc, B_CHUNK, B_GROUPS, B_STATE)
    a_cs = jnp.cumsum(ad, axis=-1)
    lmat = jnp.exp(segsum(ad))
    cb = jnp.einsum("bclgn,bcsgn->bcgls", c_c, b_c)
    y_diag = jnp.einsum("bcgjls,bcsgjp->bclgjp", cb[:, :, :, None] * lmat, xdt)
    decay_states = jnp.exp(a_cs[..., -1:] - a_cs)
    chunk_states = jnp.einsum("bclgn,bcgjl,bclgjp->bcgjpn", b_c, decay_states, xdt)
    chunk_decay = jnp.exp(a_cs[..., -1])

    def step(h, inp):
        st, dc = inp
        return dc[..., None, None] * h + st, h

    h0 = jnp.zeros((bsz, B_GROUPS, hpg, B_HEAD_DIM, B_STATE), f32)
    _, prev = lax.scan(step, h0, (jnp.moveaxis(chunk_states, 1, 0), jnp.moveaxis(chunk_decay, 1, 0)))
    prev = jnp.moveaxis(prev, 0, 1)
    y_off = jnp.einsum("bclgn,bcgjpn,bcgjl->bclgjp", c_c, prev, jnp.exp(a_cs))
    y = y_diag + y_off + x_c * d_skip.astype(f32).reshape(B_GROUPS, hpg)[:, :, None]
    y = y.reshape(bsz, s, B_WIDTH)
    y = rms_norm(y * jax.nn.silu(z.astype(f32)), norm_g)
    return y.astype(zxbcdt.dtype)


def chunk_sgu(uv, ln_g, ln_b, w_s, b_s):
    bsz, s, _ = uv.shape
    nc = s // C_CHUNK
    u, v = jnp.split(jax.nn.gelu(uv), 2, axis=-1)
    v = layer_norm(v, ln_g, ln_b)
    vc = v.reshape(bsz, nc, C_CHUNK, C_GROUPS, C_GROUP_DIM)
    mask = jnp.tril(jnp.ones((C_CHUNK, C_CHUNK), dtype=bool))
    w = jnp.where(mask, w_s, jnp.zeros_like(w_s))
    mixed = jnp.einsum("gts,bcsgd->bctgd", w, vc) + jnp.swapaxes(b_s, 0, 1)[:, :, None]
    return u * mixed.reshape(bsz, s, C_WIDTH)


def hybrid_mixer(h, w_in, w_out, lru_conv_w, lru_conv_b, lru_w_r, lru_b_r, lru_w_i, lru_b_i, lru_lambda,
                 ssd_conv_w, ssd_conv_b, ssd_dt_bias, ssd_a_log, ssd_d, ssd_norm_g,
                 sgu_ln_g, sgu_ln_b, sgu_w_s, sgu_b_s):
    proj = h @ w_in
    pa, pb, pc = jnp.split(proj, [2 * A_WIDTH, 2 * A_WIDTH + B_IN], axis=-1)
    gate_a, rec_a = jnp.split(pa, 2, axis=-1)
    rec = causal_dwconv(rec_a, lru_conv_w, lru_conv_b)
    ya = rg_lru(rec, lru_w_r, lru_b_r, lru_w_i, lru_b_i, lru_lambda) * jax.nn.gelu(gate_a)
    yb = ssd_mixer(pb, ssd_conv_w, ssd_conv_b, ssd_dt_bias, ssd_a_log, ssd_d, ssd_norm_g)
    yc = chunk_sgu(pc, sgu_ln_g, sgu_ln_b, sgu_w_s, sgu_b_s)
    return jnp.concatenate([ya, yb, yc], axis=-1) @ w_out


def _fwd_setup_inputs(seed: int = 0) -> dict:
    key = jax.random.key(seed)
    ks = iter(jax.random.split(key, 40))
    f32 = jnp.float32
    L = DEPTH

    def nrm(shape, scale):
        return jax.random.normal(next(ks), shape, f32) * scale

    def gain(shape):
        return 1.0 + 0.05 * jax.random.normal(next(ks), shape, f32)

    x = jax.random.normal(next(ks), (BATCH, SEQ, D_MODEL), f32)
    ffn1_pre_g = gain((L, D_MODEL))
    ffn1_post_g = gain((L, D_MODEL))
    ffn1_w_gu = nrm((L, D_MODEL, 2 * D_FF), D_MODEL ** -0.5)
    ffn1_w_down = nrm((L, D_FF, D_MODEL), D_FF ** -0.5)
    mix_pre_g = gain((L, D_MODEL))
    mix_post_g = gain((L, D_MODEL))
    mix_w_in = nrm((L, D_MODEL, IN_COLS), D_MODEL ** -0.5)
    mix_w_out = nrm((L, D_MIX, D_MODEL), D_MIX ** -0.5)
    lru_conv_w = nrm((L, CONV_W, A_WIDTH), CONV_W ** -0.5)
    lru_conv_b = nrm((L, A_WIDTH), 0.02)
    lru_w_r = nrm((L, A_HEADS, A_HEAD_DIM, A_HEAD_DIM), A_HEAD_DIM ** -0.5)
    lru_b_r = nrm((L, A_WIDTH), 0.02)
    lru_w_i = nrm((L, A_HEADS, A_HEAD_DIM, A_HEAD_DIM), A_HEAD_DIM ** -0.5)
    lru_b_i = nrm((L, A_WIDTH), 0.02)
    a_c = jax.random.uniform(next(ks), (L, A_WIDTH), f32, 0.9, 0.999)
    a0 = a_c ** (1.0 / LRU_C)
    lru_lambda = jnp.log(a0) - jnp.log1p(-a0)
    ssd_conv_w = nrm((L, CONV_W, B_CONV_DIM), CONV_W ** -0.5)
    ssd_conv_b = nrm((L, B_CONV_DIM), 0.02)
    dt0 = jnp.exp(jax.random.uniform(next(ks), (L, B_HEADS), f32, math.log(1e-3), math.log(1e-1)))
    ssd_dt_bias = dt0 + jnp.log(-jnp.expm1(-dt0))
    ssd_a_log = jnp.log(jax.random.uniform(next(ks), (L, B_HEADS), f32, 1.0, 16.0))
    ssd_d = 1.0 + 0.1 * jax.random.normal(next(ks), (L, B_HEADS), f32)
    ssd_norm_g = gain((L, B_WIDTH))
    sgu_ln_g = gain((L, C_WIDTH))
    sgu_ln_b = nrm((L, C_WIDTH), 0.02)
    sgu_w_s = nrm((L, C_GROUPS, C_CHUNK, C_CHUNK), C_CHUNK ** -0.5)
    sgu_b_s = 1.0 + 0.1 * jax.random.normal(next(ks), (L, C_GROUPS, C_CHUNK), f32)
    ffn2_pre_g = gain((L, D_MODEL))
    ffn2_post_g = gain((L, D_MODEL))
    ffn2_w_gu = nrm((L, D_MODEL, 2 * D_FF), D_MODEL ** -0.5)
    ffn2_w_down = nrm((L, D_FF, D_MODEL), D_FF ** -0.5)
    return {
        "x": x,
        "ffn1_pre_g": ffn1_pre_g, "ffn1_post_g": ffn1_post_g,
        "ffn1_w_gu": ffn1_w_gu, "ffn1_w_down": ffn1_w_down,
        "mix_pre_g": mix_pre_g, "mix_post_g": mix_post_g,
        "mix_w_in": mix_w_in, "mix_w_out": mix_w_out,
        "lru_conv_w": lru_conv_w, "lru_conv_b": lru_conv_b,
        "lru_w_r": lru_w_r, "lru_b_r": lru_b_r,
        "lru_w_i": lru_w_i, "lru_b_i": lru_b_i, "lru_lambda": lru_lambda,
        "ssd_conv_w": ssd_conv_w, "ssd_conv_b": ssd_conv_b,
        "ssd_dt_bias": ssd_dt_bias, "ssd_a_log": ssd_a_log,
        "ssd_d": ssd_d, "ssd_norm_g": ssd_norm_g,
        "sgu_ln_g": sgu_ln_g, "sgu_ln_b": sgu_ln_b,
        "sgu_w_s": sgu_w_s, "sgu_b_s": sgu_b_s,
        "ffn2_pre_g": ffn2_pre_g, "ffn2_post_g": ffn2_post_g,
        "ffn2_w_gu": ffn2_w_gu, "ffn2_w_down": ffn2_w_down,
    }


def _fwd_reference(x, ffn1_pre_g, ffn1_post_g, ffn1_w_gu, ffn1_w_down,
              mix_pre_g, mix_post_g, mix_w_in, mix_w_out,
              lru_conv_w, lru_conv_b, lru_w_r, lru_b_r, lru_w_i, lru_b_i, lru_lambda,
              ssd_conv_w, ssd_conv_b, ssd_dt_bias, ssd_a_log, ssd_d, ssd_norm_g,
              sgu_ln_g, sgu_ln_b, sgu_w_s, sgu_b_s,
              ffn2_pre_g, ffn2_post_g, ffn2_w_gu, ffn2_w_down):
    for l in range(DEPTH):
        f = swiglu(rms_norm(x, ffn1_pre_g[l]), ffn1_w_gu[l], ffn1_w_down[l])
        x = x + 0.5 * rms_norm(f, ffn1_post_g[l])
        m = hybrid_mixer(rms_norm(x, mix_pre_g[l]), mix_w_in[l], mix_w_out[l],
                         lru_conv_w[l], lru_conv_b[l], lru_w_r[l], lru_b_r[l],
                         lru_w_i[l], lru_b_i[l], lru_lambda[l],
                         ssd_conv_w[l], ssd_conv_b[l], ssd_dt_bias[l], ssd_a_log[l],
                         ssd_d[l], ssd_norm_g[l],
                         sgu_ln_g[l], sgu_ln_b[l], sgu_w_s[l], sgu_b_s[l])
        x = x + rms_norm(m, mix_post_g[l])
        f = swiglu(rms_norm(x, ffn2_pre_g[l]), ffn2_w_gu[l], ffn2_w_down[l])
        x = x + 0.5 * rms_norm(f, ffn2_post_g[l])
    return x


import jax as _jax
import jax.numpy as _jnp

TWIN_FORMAT = 'train_step'
FWD_PARAMS = ['x', 'ffn1_pre_g', 'ffn1_post_g', 'ffn1_w_gu', 'ffn1_w_down', 'mix_pre_g', 'mix_post_g', 'mix_w_in', 'mix_w_out', 'lru_conv_w', 'lru_conv_b', 'lru_w_r', 'lru_b_r', 'lru_w_i', 'lru_b_i', 'lru_lambda', 'ssd_conv_w', 'ssd_conv_b', 'ssd_dt_bias', 'ssd_a_log', 'ssd_d', 'ssd_norm_g', 'sgu_ln_g', 'sgu_ln_b', 'sgu_w_s', 'sgu_b_s', 'ffn2_pre_g', 'ffn2_post_g', 'ffn2_w_gu', 'ffn2_w_down']
TWIN_WEIGHTS = ['ffn1_pre_g', 'ffn1_post_g', 'ffn1_w_gu', 'ffn1_w_down', 'mix_pre_g', 'mix_post_g', 'mix_w_in', 'mix_w_out', 'lru_conv_w', 'lru_conv_b', 'lru_w_r', 'lru_b_r', 'lru_w_i', 'lru_b_i', 'lru_lambda', 'ssd_conv_w', 'ssd_conv_b', 'ssd_dt_bias', 'ssd_a_log', 'ssd_d', 'ssd_norm_g', 'sgu_ln_g', 'sgu_ln_b', 'sgu_w_s', 'sgu_b_s', 'ffn2_pre_g', 'ffn2_post_g', 'ffn2_w_gu', 'ffn2_w_down']
TWIN_DIFF_INPUT = 'x'
TWIN_INPUTS = ['x', 'ffn1_pre_g', 'ffn1_post_g', 'ffn1_w_gu', 'ffn1_w_down', 'mix_pre_g', 'mix_post_g', 'mix_w_in', 'mix_w_out', 'lru_conv_w', 'lru_conv_b', 'lru_w_r', 'lru_b_r', 'lru_w_i', 'lru_b_i', 'lru_lambda', 'ssd_conv_w', 'ssd_conv_b', 'ssd_dt_bias', 'ssd_a_log', 'ssd_d', 'ssd_norm_g', 'sgu_ln_g', 'sgu_ln_b', 'sgu_w_s', 'sgu_b_s', 'ffn2_pre_g', 'ffn2_post_g', 'ffn2_w_gu', 'ffn2_w_down', 'loss_target', 'm_ffn1_pre_g', 'm_ffn1_post_g', 'm_ffn1_w_gu', 'm_ffn1_w_down', 'm_mix_pre_g', 'm_mix_post_g', 'm_mix_w_in', 'm_mix_w_out', 'm_lru_conv_w', 'm_lru_conv_b', 'm_lru_w_r', 'm_lru_b_r', 'm_lru_w_i', 'm_lru_b_i', 'm_lru_lambda', 'm_ssd_conv_w', 'm_ssd_conv_b', 'm_ssd_dt_bias', 'm_ssd_a_log', 'm_ssd_d', 'm_ssd_norm_g', 'm_sgu_ln_g', 'm_sgu_ln_b', 'm_sgu_w_s', 'm_sgu_b_s', 'm_ffn2_pre_g', 'm_ffn2_post_g', 'm_ffn2_w_gu', 'm_ffn2_w_down', 'v_ffn1_pre_g', 'v_ffn1_post_g', 'v_ffn1_w_gu', 'v_ffn1_w_down', 'v_mix_pre_g', 'v_mix_post_g', 'v_mix_w_in', 'v_mix_w_out', 'v_lru_conv_w', 'v_lru_conv_b', 'v_lru_w_r', 'v_lru_b_r', 'v_lru_w_i', 'v_lru_b_i', 'v_lru_lambda', 'v_ssd_conv_w', 'v_ssd_conv_b', 'v_ssd_dt_bias', 'v_ssd_a_log', 'v_ssd_d', 'v_ssd_norm_g', 'v_sgu_ln_g', 'v_sgu_ln_b', 'v_sgu_w_s', 'v_sgu_b_s', 'v_ffn2_pre_g', 'v_ffn2_post_g', 'v_ffn2_w_gu', 'v_ffn2_w_down']
TWIN_OUTPUTS = ['loss', 'grad_x', 'grad_ffn1_pre_g', 'grad_ffn1_post_g', 'grad_ffn1_w_gu', 'grad_ffn1_w_down', 'grad_mix_pre_g', 'grad_mix_post_g', 'grad_mix_w_in', 'grad_mix_w_out', 'grad_lru_conv_w', 'grad_lru_conv_b', 'grad_lru_w_r', 'grad_lru_b_r', 'grad_lru_w_i', 'grad_lru_b_i', 'grad_lru_lambda', 'grad_ssd_conv_w', 'grad_ssd_conv_b', 'grad_ssd_dt_bias', 'grad_ssd_a_log', 'grad_ssd_d', 'grad_ssd_norm_g', 'grad_sgu_ln_g', 'grad_sgu_ln_b', 'grad_sgu_w_s', 'grad_sgu_b_s', 'grad_ffn2_pre_g', 'grad_ffn2_post_g', 'grad_ffn2_w_gu', 'grad_ffn2_w_down', 'delta_ffn1_pre_g', 'delta_ffn1_post_g', 'delta_ffn1_w_gu', 'delta_ffn1_w_down', 'delta_mix_pre_g', 'delta_mix_post_g', 'delta_mix_w_in', 'delta_mix_w_out', 'delta_lru_conv_w', 'delta_lru_conv_b', 'delta_lru_w_r', 'delta_lru_b_r', 'delta_lru_w_i', 'delta_lru_b_i', 'delta_lru_lambda', 'delta_ssd_conv_w', 'delta_ssd_conv_b', 'delta_ssd_dt_bias', 'delta_ssd_a_log', 'delta_ssd_d', 'delta_ssd_norm_g', 'delta_sgu_ln_g', 'delta_sgu_ln_b', 'delta_sgu_w_s', 'delta_sgu_b_s', 'delta_ffn2_pre_g', 'delta_ffn2_post_g', 'delta_ffn2_w_gu', 'delta_ffn2_w_down', 'new_m_ffn1_pre_g', 'new_m_ffn1_post_g', 'new_m_ffn1_w_gu', 'new_m_ffn1_w_down', 'new_m_mix_pre_g', 'new_m_mix_post_g', 'new_m_mix_w_in', 'new_m_mix_w_out', 'new_m_lru_conv_w', 'new_m_lru_conv_b', 'new_m_lru_w_r', 'new_m_lru_b_r', 'new_m_lru_w_i', 'new_m_lru_b_i', 'new_m_lru_lambda', 'new_m_ssd_conv_w', 'new_m_ssd_conv_b', 'new_m_ssd_dt_bias', 'new_m_ssd_a_log', 'new_m_ssd_d', 'new_m_ssd_norm_g', 'new_m_sgu_ln_g', 'new_m_sgu_ln_b', 'new_m_sgu_w_s', 'new_m_sgu_b_s', 'new_m_ffn2_pre_g', 'new_m_ffn2_post_g', 'new_m_ffn2_w_gu', 'new_m_ffn2_w_down', 'new_v_ffn1_pre_g', 'new_v_ffn1_post_g', 'new_v_ffn1_w_gu', 'new_v_ffn1_w_down', 'new_v_mix_pre_g', 'new_v_mix_post_g', 'new_v_mix_w_in', 'new_v_mix_w_out', 'new_v_lru_conv_w', 'new_v_lru_conv_b', 'new_v_lru_w_r', 'new_v_lru_b_r', 'new_v_lru_w_i', 'new_v_lru_b_i', 'new_v_lru_lambda', 'new_v_ssd_conv_w', 'new_v_ssd_conv_b', 'new_v_ssd_dt_bias', 'new_v_ssd_a_log', 'new_v_ssd_d', 'new_v_ssd_norm_g', 'new_v_sgu_ln_g', 'new_v_sgu_ln_b', 'new_v_sgu_w_s', 'new_v_sgu_b_s', 'new_v_ffn2_pre_g', 'new_v_ffn2_post_g', 'new_v_ffn2_w_gu', 'new_v_ffn2_w_down']
TWIN_LEAF_KINDS = {'loss': 'loss', 'grad_x': 'grad_x', 'grad_ffn1_pre_g': 'grad_w', 'grad_ffn1_post_g': 'grad_w', 'grad_ffn1_w_gu': 'grad_w', 'grad_ffn1_w_down': 'grad_w', 'grad_mix_pre_g': 'grad_w', 'grad_mix_post_g': 'grad_w', 'grad_mix_w_in': 'grad_w', 'grad_mix_w_out': 'grad_w', 'grad_lru_conv_w': 'grad_w', 'grad_lru_conv_b': 'grad_w', 'grad_lru_w_r': 'grad_w', 'grad_lru_b_r': 'grad_w', 'grad_lru_w_i': 'grad_w', 'grad_lru_b_i': 'grad_w', 'grad_lru_lambda': 'grad_w', 'grad_ssd_conv_w': 'grad_w', 'grad_ssd_conv_b': 'grad_w', 'grad_ssd_dt_bias': 'grad_w', 'grad_ssd_a_log': 'grad_w', 'grad_ssd_d': 'grad_w', 'grad_ssd_norm_g': 'grad_w', 'grad_sgu_ln_g': 'grad_w', 'grad_sgu_ln_b': 'grad_w', 'grad_sgu_w_s': 'grad_w', 'grad_sgu_b_s': 'grad_w', 'grad_ffn2_pre_g': 'grad_w', 'grad_ffn2_post_g': 'grad_w', 'grad_ffn2_w_gu': 'grad_w', 'grad_ffn2_w_down': 'grad_w', 'delta_ffn1_pre_g': 'delta_w', 'delta_ffn1_post_g': 'delta_w', 'delta_ffn1_w_gu': 'delta_w', 'delta_ffn1_w_down': 'delta_w', 'delta_mix_pre_g': 'delta_w', 'delta_mix_post_g': 'delta_w', 'delta_mix_w_in': 'delta_w', 'delta_mix_w_out': 'delta_w', 'delta_lru_conv_w': 'delta_w', 'delta_lru_conv_b': 'delta_w', 'delta_lru_w_r': 'delta_w', 'delta_lru_b_r': 'delta_w', 'delta_lru_w_i': 'delta_w', 'delta_lru_b_i': 'delta_w', 'delta_lru_lambda': 'delta_w', 'delta_ssd_conv_w': 'delta_w', 'delta_ssd_conv_b': 'delta_w', 'delta_ssd_dt_bias': 'delta_w', 'delta_ssd_a_log': 'delta_w', 'delta_ssd_d': 'delta_w', 'delta_ssd_norm_g': 'delta_w', 'delta_sgu_ln_g': 'delta_w', 'delta_sgu_ln_b': 'delta_w', 'delta_sgu_w_s': 'delta_w', 'delta_sgu_b_s': 'delta_w', 'delta_ffn2_pre_g': 'delta_w', 'delta_ffn2_post_g': 'delta_w', 'delta_ffn2_w_gu': 'delta_w', 'delta_ffn2_w_down': 'delta_w', 'new_m_ffn1_pre_g': 'new_m', 'new_m_ffn1_post_g': 'new_m', 'new_m_ffn1_w_gu': 'new_m', 'new_m_ffn1_w_down': 'new_m', 'new_m_mix_pre_g': 'new_m', 'new_m_mix_post_g': 'new_m', 'new_m_mix_w_in': 'new_m', 'new_m_mix_w_out': 'new_m', 'new_m_lru_conv_w': 'new_m', 'new_m_lru_conv_b': 'new_m', 'new_m_lru_w_r': 'new_m', 'new_m_lru_b_r': 'new_m', 'new_m_lru_w_i': 'new_m', 'new_m_lru_b_i': 'new_m', 'new_m_lru_lambda': 'new_m', 'new_m_ssd_conv_w': 'new_m', 'new_m_ssd_conv_b': 'new_m', 'new_m_ssd_dt_bias': 'new_m', 'new_m_ssd_a_log': 'new_m', 'new_m_ssd_d': 'new_m', 'new_m_ssd_norm_g': 'new_m', 'new_m_sgu_ln_g': 'new_m', 'new_m_sgu_ln_b': 'new_m', 'new_m_sgu_w_s': 'new_m', 'new_m_sgu_b_s': 'new_m', 'new_m_ffn2_pre_g': 'new_m', 'new_m_ffn2_post_g': 'new_m', 'new_m_ffn2_w_gu': 'new_m', 'new_m_ffn2_w_down': 'new_m', 'new_v_ffn1_pre_g': 'new_v', 'new_v_ffn1_post_g': 'new_v', 'new_v_ffn1_w_gu': 'new_v', 'new_v_ffn1_w_down': 'new_v', 'new_v_mix_pre_g': 'new_v', 'new_v_mix_post_g': 'new_v', 'new_v_mix_w_in': 'new_v', 'new_v_mix_w_out': 'new_v', 'new_v_lru_conv_w': 'new_v', 'new_v_lru_conv_b': 'new_v', 'new_v_lru_w_r': 'new_v', 'new_v_lru_b_r': 'new_v', 'new_v_lru_w_i': 'new_v', 'new_v_lru_b_i': 'new_v', 'new_v_lru_lambda': 'new_v', 'new_v_ssd_conv_w': 'new_v', 'new_v_ssd_conv_b': 'new_v', 'new_v_ssd_dt_bias': 'new_v', 'new_v_ssd_a_log': 'new_v', 'new_v_ssd_d': 'new_v', 'new_v_ssd_norm_g': 'new_v', 'new_v_sgu_ln_g': 'new_v', 'new_v_sgu_ln_b': 'new_v', 'new_v_sgu_w_s': 'new_v', 'new_v_sgu_b_s': 'new_v', 'new_v_ffn2_pre_g': 'new_v', 'new_v_ffn2_post_g': 'new_v', 'new_v_ffn2_w_gu': 'new_v', 'new_v_ffn2_w_down': 'new_v'}


def _forward(args):
    return _fwd_reference(*[args[k] for k in FWD_PARAMS])


def _output_shape():
    out = _jax.eval_shape(lambda: _forward(_fwd_setup_inputs(0)))
    return out.shape, out.dtype

N_MICROBATCH = 1
ADAM_LR = 0.001
ADAM_B1 = 0.9
ADAM_B2 = 0.999
ADAM_EPS = 1e-08
ADAM_WD = 0.01
ADAM_STEP = 10
PER_EXAMPLE_BATCH_AXIS = {'x': 0, 'loss_target': 0}
SHARED_INPUTS = []
_WEIGHT_DTYPES = {'ffn1_pre_g': _jnp.float32, 'ffn1_post_g': _jnp.float32, 'ffn1_w_gu': _jnp.float32, 'ffn1_w_down': _jnp.float32, 'mix_pre_g': _jnp.float32, 'mix_post_g': _jnp.float32, 'mix_w_in': _jnp.float32, 'mix_w_out': _jnp.float32, 'lru_conv_w': _jnp.float32, 'lru_conv_b': _jnp.float32, 'lru_w_r': _jnp.float32, 'lru_b_r': _jnp.float32, 'lru_w_i': _jnp.float32, 'lru_b_i': _jnp.float32, 'lru_lambda': _jnp.float32, 'ssd_conv_w': _jnp.float32, 'ssd_conv_b': _jnp.float32, 'ssd_dt_bias': _jnp.float32, 'ssd_a_log': _jnp.float32, 'ssd_d': _jnp.float32, 'ssd_norm_g': _jnp.float32, 'sgu_ln_g': _jnp.float32, 'sgu_ln_b': _jnp.float32, 'sgu_w_s': _jnp.float32, 'sgu_b_s': _jnp.float32, 'ffn2_pre_g': _jnp.float32, 'ffn2_post_g': _jnp.float32, 'ffn2_w_gu': _jnp.float32, 'ffn2_w_down': _jnp.float32}
MOMENT_SCALE = {'ffn1_pre_g': 1.145791e+00, 'ffn1_post_g': 7.376306e+00, 'ffn1_w_gu': 4.902132e-01, 'ffn1_w_down': 8.660477e-01, 'mix_pre_g': 2.890087e+00, 'mix_post_g': 3.260501e+01, 'mix_w_in': 1.763980e+00, 'mix_w_out': 5.535031e+00, 'lru_conv_w': 4.903027e+00, 'lru_conv_b': 3.007965e+01, 'lru_w_r': 1.145456e+00, 'lru_b_r': 7.980653e-01, 'lru_w_i': 2.126069e+00, 'lru_b_i': 1.808547e+00, 'lru_lambda': 1.543115e+00, 'ssd_conv_w': 1.949830e+00, 'ssd_conv_b': 6.829537e+00, 'ssd_dt_bias': 4.751861e+00, 'ssd_a_log': 1.247848e+01, 'ssd_d': 1.210403e+01, 'ssd_norm_g': 4.352901e+00, 'sgu_ln_g': 5.166062e-01, 'sgu_ln_b': 5.290372e-01, 'sgu_w_s': 3.647771e-01, 'sgu_b_s': 5.309916e-01, 'ffn2_pre_g': 9.489127e-01, 'ffn2_post_g': 7.924131e+00, 'ffn2_w_gu': 4.014705e-01, 'ffn2_w_down': 7.913386e-01}


def _to_microbatches(a, axis):
    t = _jnp.moveaxis(a, axis, 0)
    t = t.reshape((N_MICROBATCH, t.shape[0] // N_MICROBATCH) + t.shape[1:])
    return _jnp.moveaxis(t, 1, axis + 1)


def setup_inputs(seed: int = 0) -> dict:
    inp = _fwd_setup_inputs(seed)
    key = _jax.random.fold_in(_jax.random.key(seed), 7919)
    shape, _ = _output_shape()
    out = dict(inp)
    out["loss_target"] = _jax.random.normal(_jax.random.fold_in(key, 0), shape, _jnp.float32)
    for i, name in enumerate(TWIN_WEIGHTS):
        w = inp[name].astype(_jnp.float32)
        if MOMENT_SCALE is None:
            s = _jnp.sqrt(_jnp.mean(_jnp.square(w)) + 1e-30)
        else:
            s = MOMENT_SCALE[name]
        km, kv = _jax.random.split(_jax.random.fold_in(key, i + 1))
        out[name] = w
        out["m_" + name] = s * _jax.random.normal(km, w.shape, _jnp.float32)
        out["v_" + name] = (s * s) * _jax.random.uniform(kv, w.shape, _jnp.float32, 0.5, 1.5)
    if N_MICROBATCH > 1:
        for name, axis in PER_EXAMPLE_BATCH_AXIS.items():
            out[name] = _to_microbatches(out[name], axis)
    return {'x': out['x'], 'ffn1_pre_g': out['ffn1_pre_g'], 'ffn1_post_g': out['ffn1_post_g'], 'ffn1_w_gu': out['ffn1_w_gu'], 'ffn1_w_down': out['ffn1_w_down'], 'mix_pre_g': out['mix_pre_g'], 'mix_post_g': out['mix_post_g'], 'mix_w_in': out['mix_w_in'], 'mix_w_out': out['mix_w_out'], 'lru_conv_w': out['lru_conv_w'], 'lru_conv_b': out['lru_conv_b'], 'lru_w_r': out['lru_w_r'], 'lru_b_r': out['lru_b_r'], 'lru_w_i': out['lru_w_i'], 'lru_b_i': out['lru_b_i'], 'lru_lambda': out['lru_lambda'], 'ssd_conv_w': out['ssd_conv_w'], 'ssd_conv_b': out['ssd_conv_b'], 'ssd_dt_bias': out['ssd_dt_bias'], 'ssd_a_log': out['ssd_a_log'], 'ssd_d': out['ssd_d'], 'ssd_norm_g': out['ssd_norm_g'], 'sgu_ln_g': out['sgu_ln_g'], 'sgu_ln_b': out['sgu_ln_b'], 'sgu_w_s': out['sgu_w_s'], 'sgu_b_s': out['sgu_b_s'], 'ffn2_pre_g': out['ffn2_pre_g'], 'ffn2_post_g': out['ffn2_post_g'], 'ffn2_w_gu': out['ffn2_w_gu'], 'ffn2_w_down': out['ffn2_w_down'], 'loss_target': out['loss_target'], 'm_ffn1_pre_g': out['m_ffn1_pre_g'], 'm_ffn1_post_g': out['m_ffn1_post_g'], 'm_ffn1_w_gu': out['m_ffn1_w_gu'], 'm_ffn1_w_down': out['m_ffn1_w_down'], 'm_mix_pre_g': out['m_mix_pre_g'], 'm_mix_post_g': out['m_mix_post_g'], 'm_mix_w_in': out['m_mix_w_in'], 'm_mix_w_out': out['m_mix_w_out'], 'm_lru_conv_w': out['m_lru_conv_w'], 'm_lru_conv_b': out['m_lru_conv_b'], 'm_lru_w_r': out['m_lru_w_r'], 'm_lru_b_r': out['m_lru_b_r'], 'm_lru_w_i': out['m_lru_w_i'], 'm_lru_b_i': out['m_lru_b_i'], 'm_lru_lambda': out['m_lru_lambda'], 'm_ssd_conv_w': out['m_ssd_conv_w'], 'm_ssd_conv_b': out['m_ssd_conv_b'], 'm_ssd_dt_bias': out['m_ssd_dt_bias'], 'm_ssd_a_log': out['m_ssd_a_log'], 'm_ssd_d': out['m_ssd_d'], 'm_ssd_norm_g': out['m_ssd_norm_g'], 'm_sgu_ln_g': out['m_sgu_ln_g'], 'm_sgu_ln_b': out['m_sgu_ln_b'], 'm_sgu_w_s': out['m_sgu_w_s'], 'm_sgu_b_s': out['m_sgu_b_s'], 'm_ffn2_pre_g': out['m_ffn2_pre_g'], 'm_ffn2_post_g': out['m_ffn2_post_g'], 'm_ffn2_w_gu': out['m_ffn2_w_gu'], 'm_ffn2_w_down': out['m_ffn2_w_down'], 'v_ffn1_pre_g': out['v_ffn1_pre_g'], 'v_ffn1_post_g': out['v_ffn1_post_g'], 'v_ffn1_w_gu': out['v_ffn1_w_gu'], 'v_ffn1_w_down': out['v_ffn1_w_down'], 'v_mix_pre_g': out['v_mix_pre_g'], 'v_mix_post_g': out['v_mix_post_g'], 'v_mix_w_in': out['v_mix_w_in'], 'v_mix_w_out': out['v_mix_w_out'], 'v_lru_conv_w': out['v_lru_conv_w'], 'v_lru_conv_b': out['v_lru_conv_b'], 'v_lru_w_r': out['v_lru_w_r'], 'v_lru_b_r': out['v_lru_b_r'], 'v_lru_w_i': out['v_lru_w_i'], 'v_lru_b_i': out['v_lru_b_i'], 'v_lru_lambda': out['v_lru_lambda'], 'v_ssd_conv_w': out['v_ssd_conv_w'], 'v_ssd_conv_b': out['v_ssd_conv_b'], 'v_ssd_dt_bias': out['v_ssd_dt_bias'], 'v_ssd_a_log': out['v_ssd_a_log'], 'v_ssd_d': out['v_ssd_d'], 'v_ssd_norm_g': out['v_ssd_norm_g'], 'v_sgu_ln_g': out['v_sgu_ln_g'], 'v_sgu_ln_b': out['v_sgu_ln_b'], 'v_sgu_w_s': out['v_sgu_w_s'], 'v_sgu_b_s': out['v_sgu_b_s'], 'v_ffn2_pre_g': out['v_ffn2_pre_g'], 'v_ffn2_post_g': out['v_ffn2_post_g'], 'v_ffn2_w_gu': out['v_ffn2_w_gu'], 'v_ffn2_w_down': out['v_ffn2_w_down']}


def _loss(weights, diff, rest, loss_target):
    with _jax.named_scope("forward"):
        args = {**rest, TWIN_DIFF_INPUT: diff, **{k: w.astype(_WEIGHT_DTYPES[k]) for k, w in weights.items()}}
        y = _forward(args)
    with _jax.named_scope("loss_head"):
        err = _jnp.square(y.astype(_jnp.float32) - loss_target)
        return 0.5 * _jnp.sum(_jnp.mean(err, axis=-1)) if err.ndim else 0.5 * err


def _adamw(w, g, m, v):
    m = ADAM_B1 * m + (1.0 - ADAM_B1) * g
    v = ADAM_B2 * v + (1.0 - ADAM_B2) * _jnp.square(g)
    m_hat = m / (1.0 - ADAM_B1 ** ADAM_STEP)
    v_hat = v / (1.0 - ADAM_B2 ** ADAM_STEP)
    delta = -ADAM_LR * (m_hat / (_jnp.sqrt(v_hat) + ADAM_EPS) + ADAM_WD * w)
    return delta, m, v


def reference(x, ffn1_pre_g, ffn1_post_g, ffn1_w_gu, ffn1_w_down, mix_pre_g, mix_post_g, mix_w_in, mix_w_out, lru_conv_w, lru_conv_b, lru_w_r, lru_b_r, lru_w_i, lru_b_i, lru_lambda, ssd_conv_w, ssd_conv_b, ssd_dt_bias, ssd_a_log, ssd_d, ssd_norm_g, sgu_ln_g, sgu_ln_b, sgu_w_s, sgu_b_s, ffn2_pre_g, ffn2_post_g, ffn2_w_gu, ffn2_w_down, loss_target, m_ffn1_pre_g, m_ffn1_post_g, m_ffn1_w_gu, m_ffn1_w_down, m_mix_pre_g, m_mix_post_g, m_mix_w_in, m_mix_w_out, m_lru_conv_w, m_lru_conv_b, m_lru_w_r, m_lru_b_r, m_lru_w_i, m_lru_b_i, m_lru_lambda, m_ssd_conv_w, m_ssd_conv_b, m_ssd_dt_bias, m_ssd_a_log, m_ssd_d, m_ssd_norm_g, m_sgu_ln_g, m_sgu_ln_b, m_sgu_w_s, m_sgu_b_s, m_ffn2_pre_g, m_ffn2_post_g, m_ffn2_w_gu, m_ffn2_w_down, v_ffn1_pre_g, v_ffn1_post_g, v_ffn1_w_gu, v_ffn1_w_down, v_mix_pre_g, v_mix_post_g, v_mix_w_in, v_mix_w_out, v_lru_conv_w, v_lru_conv_b, v_lru_w_r, v_lru_b_r, v_lru_w_i, v_lru_b_i, v_lru_lambda, v_ssd_conv_w, v_ssd_conv_b, v_ssd_dt_bias, v_ssd_a_log, v_ssd_d, v_ssd_norm_g, v_sgu_ln_g, v_sgu_ln_b, v_sgu_w_s, v_sgu_b_s, v_ffn2_pre_g, v_ffn2_post_g, v_ffn2_w_gu, v_ffn2_w_down):
    given = dict(x=x, ffn1_pre_g=ffn1_pre_g, ffn1_post_g=ffn1_post_g, ffn1_w_gu=ffn1_w_gu, ffn1_w_down=ffn1_w_down, mix_pre_g=mix_pre_g, mix_post_g=mix_post_g, mix_w_in=mix_w_in, mix_w_out=mix_w_out, lru_conv_w=lru_conv_w, lru_conv_b=lru_conv_b, lru_w_r=lru_w_r, lru_b_r=lru_b_r, lru_w_i=lru_w_i, lru_b_i=lru_b_i, lru_lambda=lru_lambda, ssd_conv_w=ssd_conv_w, ssd_conv_b=ssd_conv_b, ssd_dt_bias=ssd_dt_bias, ssd_a_log=ssd_a_log, ssd_d=ssd_d, ssd_norm_g=ssd_norm_g, sgu_ln_g=sgu_ln_g, sgu_ln_b=sgu_ln_b, sgu_w_s=sgu_w_s, sgu_b_s=sgu_b_s, ffn2_pre_g=ffn2_pre_g, ffn2_post_g=ffn2_post_g, ffn2_w_gu=ffn2_w_gu, ffn2_w_down=ffn2_w_down, loss_target=loss_target, m_ffn1_pre_g=m_ffn1_pre_g, m_ffn1_post_g=m_ffn1_post_g, m_ffn1_w_gu=m_ffn1_w_gu, m_ffn1_w_down=m_ffn1_w_down, m_mix_pre_g=m_mix_pre_g, m_mix_post_g=m_mix_post_g, m_mix_w_in=m_mix_w_in, m_mix_w_out=m_mix_w_out, m_lru_conv_w=m_lru_conv_w, m_lru_conv_b=m_lru_conv_b, m_lru_w_r=m_lru_w_r, m_lru_b_r=m_lru_b_r, m_lru_w_i=m_lru_w_i, m_lru_b_i=m_lru_b_i, m_lru_lambda=m_lru_lambda, m_ssd_conv_w=m_ssd_conv_w, m_ssd_conv_b=m_ssd_conv_b, m_ssd_dt_bias=m_ssd_dt_bias, m_ssd_a_log=m_ssd_a_log, m_ssd_d=m_ssd_d, m_ssd_norm_g=m_ssd_norm_g, m_sgu_ln_g=m_sgu_ln_g, m_sgu_ln_b=m_sgu_ln_b, m_sgu_w_s=m_sgu_w_s, m_sgu_b_s=m_sgu_b_s, m_ffn2_pre_g=m_ffn2_pre_g, m_ffn2_post_g=m_ffn2_post_g, m_ffn2_w_gu=m_ffn2_w_gu, m_ffn2_w_down=m_ffn2_w_down, v_ffn1_pre_g=v_ffn1_pre_g, v_ffn1_post_g=v_ffn1_post_g, v_ffn1_w_gu=v_ffn1_w_gu, v_ffn1_w_down=v_ffn1_w_down, v_mix_pre_g=v_mix_pre_g, v_mix_post_g=v_mix_post_g, v_mix_w_in=v_mix_w_in, v_mix_w_out=v_mix_w_out, v_lru_conv_w=v_lru_conv_w, v_lru_conv_b=v_lru_conv_b, v_lru_w_r=v_lru_w_r, v_lru_b_r=v_lru_b_r, v_lru_w_i=v_lru_w_i, v_lru_b_i=v_lru_b_i, v_lru_lambda=v_lru_lambda, v_ssd_conv_w=v_ssd_conv_w, v_ssd_conv_b=v_ssd_conv_b, v_ssd_dt_bias=v_ssd_dt_bias, v_ssd_a_log=v_ssd_a_log, v_ssd_d=v_ssd_d, v_ssd_norm_g=v_ssd_norm_g, v_sgu_ln_g=v_sgu_ln_g, v_sgu_ln_b=v_sgu_ln_b, v_sgu_w_s=v_sgu_w_s, v_sgu_b_s=v_sgu_b_s, v_ffn2_pre_g=v_ffn2_pre_g, v_ffn2_post_g=v_ffn2_post_g, v_ffn2_w_gu=v_ffn2_w_gu, v_ffn2_w_down=v_ffn2_w_down)
    weights = {n: given[n] for n in TWIN_WEIGHTS}
    shared = {n: given[n] for n in SHARED_INPUTS}
    per_example = {n: given[n] for n in ['x']}
    grad_fn = _jax.value_and_grad(_loss, argnums=(0, 1))

    def one_microbatch(ex, loss_target):
        ex = dict(ex)
        diff = ex.pop(TWIN_DIFF_INPUT)
        return grad_fn(weights, diff, {**shared, **ex}, loss_target)

    if N_MICROBATCH == 1:
        loss, (grad_w, grad_x) = one_microbatch(per_example, given["loss_target"])
    else:
        def body(carry, xs):
            loss_sum, grad_sum = carry
            l_k, (gw_k, gx_k) = one_microbatch(xs[0], xs[1])
            with _jax.named_scope("update"):
                return (loss_sum + l_k, _jax.tree.map(_jnp.add, grad_sum, gw_k)), gx_k

        init = (_jnp.zeros((), _jnp.float32), _jax.tree.map(_jnp.zeros_like, weights))
        (loss, grad_w), grad_x = _jax.lax.scan(body, init, (per_example, given["loss_target"]))
    with _jax.named_scope("update"):
        delta_w, new_m, new_v = {}, {}, {}
        for n in TWIN_WEIGHTS:
            delta_w[n], new_m[n], new_v[n] = _adamw(weights[n], grad_w[n], given["m_" + n], given["v_" + n])
    return (loss, grad_x, *[grad_w[n] for n in TWIN_WEIGHTS], *[delta_w[n] for n in TWIN_WEIGHTS],
            *[new_m[n] for n in TWIN_WEIGHTS], *[new_v[n] for n in TWIN_WEIGHTS])
```

```python
import functools

import jax
import jax.numpy as jnp
from jax import lax
from jax.experimental import pallas as pl
from jax.experimental.pallas import tpu as pltpu

f32, bf16 = jnp.float32, jnp.bfloat16
MESH = pl.DeviceIdType.MESH
ANY = pl.BlockSpec(memory_space=pl.ANY)

N_DEV = 8
NORM_EPS = 1e-6
LRU_C = 8.0
CHUNK = 128
HEAD = 64
A_W, B_W, C_W = 384, 384, 256
B_STATE = 128
XBC_W = B_W + 4 * B_STATE
PA_W, PB_W, PC_W = 2 * A_W, B_W + XBC_W + B_W, 2 * C_W
IN_PAD = PA_W + PB_W + PC_W
ADAM_LR, ADAM_B1, ADAM_B2, ADAM_EPS, ADAM_WD, ADAM_STEP = 0.001, 0.9, 0.999, 1e-08, 0.01, 10
VMEM_LIMIT_BYTES = 56 * 1024 * 1024
NEG_BIG = -1e30


def _params(sem=None):
    return pltpu.CompilerParams(dimension_semantics=sem, vmem_limit_bytes=VMEM_LIMIT_BYTES)


def _nn(a, b):
    return jnp.dot(a, b, preferred_element_type=f32)


def _nt(a, b):
    return lax.dot_general(a, b, (((1,), (1,)), ((), ())), preferred_element_type=f32)


def _tn(a, b):
    return lax.dot_general(a, b, (((0,), (0,)), ((), ())), preferred_element_type=f32)


def _sigmoid(x):
    return 1.0 / (1.0 + jnp.exp(-x))


def _softplus(x):
    return jnp.maximum(x, 0.0) + jnp.log(1.0 + jnp.exp(-jnp.abs(x)))


_GELU_C0, _GELU_C1 = 0.7978845608028654, 0.044715


def _gelu(x):
    t = jnp.tanh(_GELU_C0 * (x + _GELU_C1 * x * x * x))
    return 0.5 * x * (1.0 + t)


def _gelu_grad(x):
    t = jnp.tanh(_GELU_C0 * (x + _GELU_C1 * x * x * x))
    return 0.5 * (1.0 + t) + 0.5 * x * (1.0 - t * t) * _GELU_C0 * (1.0 + 3.0 * _GELU_C1 * x * x)


def _silu_grad(x, s):
    return s * (1.0 + x * (1.0 - s))


def _rms_fwd(x, g):
    r = lax.rsqrt(jnp.mean(x * x, axis=-1, keepdims=True) + NORM_EPS)
    return x * r * g


def _rms_bwd(x, g, dy):
    r = lax.rsqrt(jnp.mean(x * x, axis=-1, keepdims=True) + NORM_EPS)
    xh = x * r
    dxh = dy * g
    dx = r * (dxh - xh * jnp.mean(dxh * xh, axis=-1, keepdims=True))
    return dx, jnp.sum(dy * xh, axis=0, keepdims=True)


def _one_minus_exp(x):
    series = -x * (1.0 + x * (0.5 + x * (1.0 / 6.0 + x * (1.0 / 24.0))))
    return jnp.where(x > -0.01, series, 1.0 - jnp.exp(x))


def _cumsum_rows(x):
    row = lax.broadcasted_iota(jnp.int32, x.shape, 0)
    d = 1
    while d < x.shape[0]:
        x = x + jnp.where(row >= d, pltpu.roll(x, d, 0), 0.0)
        d *= 2
    return x


def _tile(t, cap):
    tm = min(cap, t)
    assert t % tm == 0
    return tm


def _lspec(a, l):
    return pl.BlockSpec((None,) + a.shape[1:], lambda *_: (l,) + (0,) * (a.ndim - 1))


def _wd_rows(wd_ref):
    return wd_ref[:, 0].reshape(2 * wd_ref.shape[2], wd_ref.shape[3])


def _ffn_fwd(x, pre_g, post_g, wgu, wd, l):
    t, d = x.shape
    nb, _, _, h = wgu.shape
    nj = nb // 2
    tm = _tile(t, 512)

    def body(x_ref, pg_ref, qg_ref, wg_ref, wu_ref, wd_ref, y_ref, hb_ref, g_ref, u_ref, f_ref, acc_ref):
        j = pl.program_id(1)

        @pl.when(j == 0)
        def _():
            hb_ref[...] = _rms_fwd(x_ref[...], pg_ref[...]).astype(bf16)

        hb = hb_ref[...]
        g = _nn(hb, wg_ref[0, 0])
        u = _nn(hb, wu_ref[0, 0])
        g_ref[0] = g.astype(bf16)
        u_ref[0] = u.astype(bf16)
        a = (g * _sigmoid(g) * u).astype(bf16)
        part = _nn(a, _wd_rows(wd_ref))

        @pl.when(j == 0)
        def _():
            acc_ref[...] = part

        @pl.when(j > 0)
        def _():
            acc_ref[...] += part

        @pl.when(j == nj - 1)
        def _():
            f = acc_ref[...]
            f_ref[...] = f
            y_ref[...] = x_ref[...] + 0.5 * _rms_fwd(f, qg_ref[...])

    row = pl.BlockSpec((tm, d), lambda i, j: (i, 0))
    vec = pl.BlockSpec((1, d), lambda i, j: (0, 0))
    act = pl.BlockSpec((1, tm, h), lambda i, j: (j, i, 0))
    return pl.pallas_call(
        body, name="ffn_fwd", grid=(t // tm, nj),
        in_specs=[row, _lspec(pre_g, l), _lspec(post_g, l),
                  pl.BlockSpec((1, 1, d, h), lambda i, j: (j, l, 0, 0)),
                  pl.BlockSpec((1, 1, d, h), lambda i, j: (j + nj, l, 0, 0)),
                  pl.BlockSpec((2, 1, h // 2, d), lambda i, j: (j, l, 0, 0))],
        out_specs=[row, row, act, act, row],
        out_shape=[jax.ShapeDtypeStruct((t, d), f32), jax.ShapeDtypeStruct((t, d), bf16),
                   jax.ShapeDtypeStruct((nj, t, h), bf16), jax.ShapeDtypeStruct((nj, t, h), bf16),
                   jax.ShapeDtypeStruct((t, d), f32)],
        scratch_shapes=[pltpu.VMEM((tm, d), f32)],
        compiler_params=_params(("arbitrary", "arbitrary")),
    )(x, pre_g, post_g, wgu, wgu, wd)


def _ffn_bwd(x, dy, f, pre_g, post_g, g, u, wgu, wd, l):
    t, d = x.shape
    nj, _, h = g.shape
    tm = _tile(t, 512)

    def body(x_ref, dy_ref, f_ref, pg_ref, qg_ref, g_ref, u_ref, wg_ref, wu_ref, wd_ref,
             dx_ref, dfb_ref, a_ref, dg_ref, du_ref, dpg_ref, dqg_ref, dh_ref):
        i, j = pl.program_id(0), pl.program_id(1)

        @pl.when((i == 0) & (j == 0))
        def _():
            dpg_ref[...] = jnp.zeros_like(dpg_ref)
            dqg_ref[...] = jnp.zeros_like(dqg_ref)

        @pl.when(j == 0)
        def _():
            df, dq = _rms_bwd(f_ref[...], qg_ref[...], 0.5 * dy_ref[...])
            dfb_ref[...] = df.astype(bf16)
            dqg_ref[...] += dq

        da = _nt(dfb_ref[...], _wd_rows(wd_ref))
        gv = g_ref[0].astype(f32)
        uv = u_ref[0].astype(f32)
        s = _sigmoid(gv)
        sg = gv * s
        a_ref[0] = (sg * uv).astype(bf16)
        dg = (da * uv * _silu_grad(gv, s)).astype(bf16)
        du = (da * sg).astype(bf16)
        dg_ref[0] = dg
        du_ref[0] = du
        part = _nt(dg, wg_ref[0, 0]) + _nt(du, wu_ref[0, 0])

        @pl.when(j == 0)
        def _():
            dh_ref[...] = part

        @pl.when(j > 0)
        def _():
            dh_ref[...] += part

        @pl.when(j == nj - 1)
        def _():
            dxn, dp = _rms_bwd(x_ref[...], pg_ref[...], dh_ref[...])
            dx_ref[...] = dy_ref[...] + dxn
            dpg_ref[...] += dp

    row = pl.BlockSpec((tm, d), lambda i, j: (i, 0))
    vec = pl.BlockSpec((1, d), lambda i, j: (0, 0))
    act = pl.BlockSpec((1, tm, h), lambda i, j: (j, i, 0))
    act_shape = jax.ShapeDtypeStruct((nj, t, h), bf16)
    return pl.pallas_call(
        body, name="ffn_bwd", grid=(t // tm, nj),
        in_specs=[row, row, row, _lspec(pre_g, l), _lspec(post_g, l), act, act,
                  pl.BlockSpec((1, 1, d, h), lambda i, j: (j, l, 0, 0)),
                  pl.BlockSpec((1, 1, d, h), lambda i, j: (j + nj, l, 0, 0)),
                  pl.BlockSpec((2, 1, h // 2, d), lambda i, j: (j, l, 0, 0))],
        out_specs=[row, row, act, act, act, vec, vec],
        out_shape=[jax.ShapeDtypeStruct((t, d), f32), jax.ShapeDtypeStruct((t, d), bf16),
                   act_shape, act_shape, act_shape,
                   jax.ShapeDtypeStruct((1, d), f32), jax.ShapeDtypeStruct((1, d), f32)],
        scratch_shapes=[pltpu.VMEM((tm, d), f32)],
        compiler_params=_params(("arbitrary", "arbitrary")),
    )(x, dy, f, pre_g, post_g, g, u, wgu, wgu, wd)


def _wgrad_cols(x, dy, buf, l, slot0):
    (t, k), (nj, _, n) = x.shape, dy.shape

    def body(x_ref, dy_ref, buf_ref, o_ref):
        o_ref[0, 0] = _tn(x_ref[...], dy_ref[0]).astype(bf16)

    return pl.pallas_call(
        body, name="wgrad_cols", grid=(nj,),
        in_specs=[pl.BlockSpec((t, k), lambda b: (0, 0)), pl.BlockSpec((1, t, n), lambda b: (b, 0, 0)), ANY],
        out_specs=pl.BlockSpec((1, 1, k, n), lambda b: (b + slot0, l, 0, 0)),
        out_shape=jax.ShapeDtypeStruct(buf.shape, bf16), input_output_aliases={2: 0},
        compiler_params=_params(("arbitrary",)),
    )(x, dy, buf)


def _wgrad_rows(x, dy, buf, l):
    (nj, t, k), (_, n) = x.shape, dy.shape

    def body(x_ref, dy_ref, buf_ref, o_ref):
        o_ref[:, 0] = _tn(x_ref[0], dy_ref[...]).astype(bf16).reshape(2, k // 2, n)

    return pl.pallas_call(
        body, name="wgrad_rows", grid=(nj,),
        in_specs=[pl.BlockSpec((1, t, k), lambda b: (b, 0, 0)), pl.BlockSpec((t, n), lambda b: (0, 0)), ANY],
        out_specs=pl.BlockSpec((2, 1, k // 2, n), lambda b: (b, l, 0, 0)),
        out_shape=jax.ShapeDtypeStruct(buf.shape, bf16), input_output_aliases={2: 0},
        compiler_params=_params(("arbitrary",)),
    )(x, dy, buf)


def _wgrad_kblocks(x, dys, buf, l):
    t, k = x.shape
    kb = k // N_DEV
    widths = [dy.shape[1] for dy in dys]
    n = sum(widths)
    nd = len(dys)

    def body(x_ref, *refs):
        dy_hbm, o_ref, dy_vmem = refs[:nd], refs[nd + 1], refs[nd + 2:]

        @pl.when(pl.program_id(0) == 0)
        def _():
            for src, dst in zip(dy_hbm, dy_vmem):
                pltpu.sync_copy(src, dst)

        off = 0
        for dst, w in zip(dy_vmem, widths):
            o_ref[0, 0, :, off:off + w] = _tn(x_ref[...], dst[...]).astype(bf16)
            off += w

    return pl.pallas_call(
        body, name="wgrad_kblocks", grid=(N_DEV,),
        in_specs=[pl.BlockSpec((t, kb), lambda s: (0, s))] + [ANY] * (nd + 1),
        out_specs=pl.BlockSpec((1, 1, kb, n), lambda s: (s, l, 0, 0)),
        out_shape=jax.ShapeDtypeStruct(buf.shape, bf16), input_output_aliases={nd + 1: 0},
        scratch_shapes=[pltpu.VMEM((t, w), bf16) for w in widths],
        compiler_params=_params(("arbitrary",)),
    )(x, *dys, buf)


def _gathered_rows(w_ref, lo, hi):
    return w_ref[:, 0, :, lo:hi].reshape(N_DEV * w_ref.shape[2], hi - lo)


def _gathered_spec(w, l):
    return pl.BlockSpec((N_DEV, 1) + w.shape[2:], lambda i: (0, l, 0, 0))


def _mix_in_fwd(x, pre_g, w_in, l):
    t, d = x.shape
    tm = _tile(t, 512)

    def body(x_ref, g_ref, w_ref, hb_ref, pa_ref, pb_ref, pc_ref):
        hb = _rms_fwd(x_ref[...], g_ref[...]).astype(bf16)
        hb_ref[...] = hb
        pa_ref[...] = _nn(hb, _gathered_rows(w_ref, 0, PA_W))
        pb_ref[...] = _nn(hb, _gathered_rows(w_ref, PA_W, PA_W + PB_W))
        pc_ref[...] = _nn(hb, _gathered_rows(w_ref, PA_W + PB_W, IN_PAD))

    def row(w):
        return pl.BlockSpec((tm, w), lambda i: (i, 0))

    return pl.pallas_call(
        body, name="mix_in_fwd", grid=(t // tm,),
        in_specs=[row(d), _lspec(pre_g, l), _gathered_spec(w_in, l)],
        out_specs=[row(d), row(PA_W), row(PB_W), row(PC_W)],
        out_shape=[jax.ShapeDtypeStruct((t, d), bf16), jax.ShapeDtypeStruct((t, PA_W), f32),
                   jax.ShapeDtypeStruct((t, PB_W), f32), jax.ShapeDtypeStruct((t, PC_W), f32)],
        compiler_params=_params(("arbitrary",)),
    )(x, pre_g, w_in)


def _mix_in_bwd(x, dy, pre_g, dpa, dpb, dpc, w_in, l):
    t, d = x.shape
    tm = _tile(t, 512)

    def body(x_ref, dy_ref, g_ref, dpa_ref, dpb_ref, dpc_ref, w_ref, dx_ref, dg_ref):
        @pl.when(pl.program_id(0) == 0)
        def _():
            dg_ref[...] = jnp.zeros_like(dg_ref)

        dh = (_nt(dpa_ref[...], _gathered_rows(w_ref, 0, PA_W))
              + _nt(dpb_ref[...], _gathered_rows(w_ref, PA_W, PA_W + PB_W))
              + _nt(dpc_ref[...], _gathered_rows(w_ref, PA_W + PB_W, IN_PAD)))
        dxn, dg = _rms_bwd(x_ref[...], g_ref[...], dh)
        dx_ref[...] = dy_ref[...] + dxn
        dg_ref[...] += dg

    def row(w):
        return pl.BlockSpec((tm, w), lambda i: (i, 0))

    vec = pl.BlockSpec((1, d), lambda i: (0, 0))
    return pl.pallas_call(
        body, name="mix_in_bwd", grid=(t // tm,),
        in_specs=[row(d), row(d), _lspec(pre_g, l), row(PA_W), row(PB_W), row(PC_W), _gathered_spec(w_in, l)],
        out_specs=[row(d), vec],
        out_shape=[jax.ShapeDtypeStruct((t, d), f32), jax.ShapeDtypeStruct((1, d), f32)],
        compiler_params=_params(("arbitrary",)),
    )(x, dy, pre_g, dpa, dpb, dpc, w_in)


def _mix_out_fwd(x, ya, yb, yc, post_g, w_out, l):
    t, d = x.shape
    tm = _tile(t, 512)

    def body(x_ref, ya_ref, yb_ref, yc_ref, g_ref, w_ref, y_ref, cat_ref, m_ref):
        cat_ref[:, 0:A_W] = ya_ref[...].astype(bf16)
        cat_ref[:, A_W:A_W + B_W] = yb_ref[...].astype(bf16)
        cat_ref[:, A_W + B_W:d] = yc_ref[...].astype(bf16)
        m = _nn(cat_ref[...], _gathered_rows(w_ref, 0, d))
        m_ref[...] = m
        y_ref[...] = x_ref[...] + _rms_fwd(m, g_ref[...])

    def row(w):
        return pl.BlockSpec((tm, w), lambda i: (i, 0))

    return pl.pallas_call(
        body, name="mix_out_fwd", grid=(t // tm,),
        in_specs=[row(d), row(A_W), row(B_W), row(C_W), _lspec(post_g, l), _gathered_spec(w_out, l)],
        out_specs=[row(d), row(d), row(d)],
        out_shape=[jax.ShapeDtypeStruct((t, d), f32), jax.ShapeDtypeStruct((t, d), bf16), jax.ShapeDtypeStruct((t, d), f32)],
        compiler_params=_params(("arbitrary",)),
    )(x, ya, yb, yc, post_g, w_out)


def _mix_out_bwd(dy, m, post_g, w_out, l):
    t, d = m.shape
    tm = _tile(t, 512)

    def body(dy_ref, m_ref, g_ref, w_ref, dm_ref, dya_ref, dyb_ref, dyc_ref, dg_ref):
        @pl.when(pl.program_id(0) == 0)
        def _():
            dg_ref[...] = jnp.zeros_like(dg_ref)

        dm, dg = _rms_bwd(m_ref[...], g_ref[...], dy_ref[...])
        dmb = dm.astype(bf16)
        dm_ref[...] = dmb
        dg_ref[...] += dg
        dcat = _nt(dmb, _gathered_rows(w_ref, 0, d))
        dya_ref[...] = dcat[:, 0:A_W]
        dyb_ref[...] = dcat[:, A_W:A_W + B_W]
        dyc_ref[...] = dcat[:, A_W + B_W:d]

    def row(w):
        return pl.BlockSpec((tm, w), lambda i: (i, 0))

    vec = pl.BlockSpec((1, d), lambda i: (0, 0))
    return pl.pallas_call(
        body, name="mix_out_bwd", grid=(t // tm,),
        in_specs=[row(d), row(d), _lspec(post_g, l), _gathered_spec(w_out, l)],
        out_specs=[row(d), row(A_W), row(B_W), row(C_W), vec],
        out_shape=[jax.ShapeDtypeStruct((t, d), bf16), jax.ShapeDtypeStruct((t, A_W), f32),
                   jax.ShapeDtypeStruct((t, B_W), f32), jax.ShapeDtypeStruct((t, C_W), f32),
                   jax.ShapeDtypeStruct((1, d), f32)],
        compiler_params=_params(("arbitrary",)),
    )(dy, m, post_g, w_out)


def _conv_fwd(buf_ref, halo, x, w, b, n):
    buf_ref[0:8, :] = halo
    buf_ref[8:8 + n, :] = x
    out = b + w[3:4, :] * x
    for k in range(3):
        out = out + w[k:k + 1, :] * buf_ref[pl.ds(5 + k, n), :]
    return out


def _conv_bwd(buf_ref, dbuf_ref, dout, dnext, w, n):
    dbuf_ref[0:n, :] = dout
    dbuf_ref[n:n + 8, :] = dnext
    dx = w[3:4, :] * dout
    dws = []
    for k in range(3):
        dx = dx + w[k:k + 1, :] * dbuf_ref[pl.ds(3 - k, n), :]
        dws.append(jnp.sum(dout * buf_ref[pl.ds(5 + k, n), :], axis=0, keepdims=True))
    dws.append(jnp.sum(dout * buf_ref[pl.ds(8, n), :], axis=0, keepdims=True))
    return dx, jnp.concatenate(dws, axis=0), jnp.sum(dout, axis=0, keepdims=True)


def _lru_gates(rec, wr, wi, br, bi, lam):
    rb = rec.astype(bf16)
    r = _sigmoid(_nn(rb, wr) + br)
    ig = _sigmoid(_nn(rb, wi) + bi)
    sp = _softplus(-lam)
    la = -LRU_C * r * sp
    a = jnp.exp(la)
    mult = jnp.sqrt(_one_minus_exp(2.0 * la))
    return rb, r, ig, sp, a, mult


def _lru_fwd(pa, conv_w, conv_b, wr, wi, br, bi, lam, l):
    t = pa.shape[0]
    tc = _tile(t, 512)

    def body(pa_ref, halo_ref, cw_ref, cb_ref, wr_ref, wi_ref, br_ref, bi_ref, lam_ref,
             ya_ref, h_ref, buf_ref, a_ref, u_ref, carry_ref):
        i = pl.program_id(0)

        @pl.when(i == 0)
        def _():
            carry_ref[...] = jnp.zeros_like(carry_ref)

        halo = jnp.where(i > 0, halo_ref[:, A_W:PA_W], 0.0)
        rec = _conv_fwd(buf_ref, halo, pa_ref[:, A_W:PA_W], cw_ref[...], cb_ref[...], tc)
        _, _, ig, _, a, mult = _lru_gates(rec, wr_ref[...], wi_ref[...], br_ref[...], bi_ref[...], lam_ref[...])
        a_ref[...] = a
        u_ref[...] = mult * (ig * rec)

        def step(s, h):
            h = a_ref[pl.ds(s, 1), :] * h + u_ref[pl.ds(s, 1), :]
            h_ref[pl.ds(s, 1), :] = h
            return h

        carry_ref[...] = lax.fori_loop(0, tc, step, carry_ref[...], unroll=8)
        ya_ref[...] = h_ref[...] * _gelu(pa_ref[:, 0:A_W])

    vec = pl.BlockSpec((1, A_W), lambda i: (0, 0))
    mat = pl.BlockSpec((A_W, A_W), lambda i: (0, 0))
    row = pl.BlockSpec((tc, A_W), lambda i: (i, 0))
    return pl.pallas_call(
        body, name="lru_fwd", grid=(t // tc,),
        in_specs=[pl.BlockSpec((tc, PA_W), lambda i: (i, 0)),
                  pl.BlockSpec((8, PA_W), lambda i: (jnp.maximum(i * (tc // 8) - 1, 0), 0)),
                  *[_lspec(a, l) for a in (conv_w, conv_b, wr, wi, br, bi, lam)]],
        out_specs=[row, row],
        out_shape=[jax.ShapeDtypeStruct((t, A_W), f32), jax.ShapeDtypeStruct((t, A_W), f32)],
        scratch_shapes=[pltpu.VMEM((8 + tc, A_W), f32), pltpu.VMEM((tc, A_W), f32), pltpu.VMEM((tc, A_W), f32),
                        pltpu.VMEM((1, A_W), f32)],
        compiler_params=_params(("arbitrary",)),
    )(pa, pa, conv_w, conv_b, wr, wi, br, bi, lam)


def _lru_bwd(pa, h, dya, conv_w, conv_b, wr, wi, br, bi, lam, l):
    t = pa.shape[0]
    tc = _tile(t, 512)
    nc = t // tc

    def body(pa_ref, halo_ref, h_ref, hhalo_ref, dya_ref, cw_ref, cb_ref, wr_ref, wi_ref, br_ref, bi_ref, lam_ref,
             dpa_ref, dcw_ref, dcb_ref, dwr_ref, dwi_ref, dbr_ref, dbi_ref, dlam_ref,
             buf_ref, dbuf_ref, hbuf_ref, a_ref, g_ref, dh_ref, carry_ref, dnext_ref):
        i = pl.program_id(0)
        c = nc - 1 - i

        @pl.when(i == 0)
        def _():
            carry_ref[...] = jnp.zeros_like(carry_ref)
            dnext_ref[...] = jnp.zeros_like(dnext_ref)
            for ref in (dcw_ref, dcb_ref, dwr_ref, dwi_ref, dbr_ref, dbi_ref, dlam_ref):
                ref[...] = jnp.zeros_like(ref)

        halo = jnp.where(c > 0, halo_ref[:, A_W:PA_W], 0.0)
        cw = cw_ref[...]
        rec = _conv_fwd(buf_ref, halo, pa_ref[:, A_W:PA_W], cw, cb_ref[...], tc)
        lam = lam_ref[...]
        rb, r, ig, sp, a, mult = _lru_gates(rec, wr_ref[...], wi_ref[...], br_ref[...], bi_ref[...], lam)
        hbuf_ref[0:8, :] = jnp.where(c > 0, hhalo_ref[...], 0.0)
        hbuf_ref[8:8 + tc, :] = h_ref[...]
        h_prev = hbuf_ref[pl.ds(7, tc), :]
        gate = pa_ref[:, 0:A_W]
        dya = dya_ref[...]
        dpa_ref[:, 0:A_W] = (dya * h_ref[...] * _gelu_grad(gate)).astype(bf16)
        a_ref[...] = a
        g_ref[...] = dya * _gelu(gate)

        def step(s, carry):
            row = tc - 1 - s
            dh = g_ref[pl.ds(row, 1), :] + carry
            dh_ref[pl.ds(row, 1), :] = dh
            return a_ref[pl.ds(row, 1), :] * dh

        carry_ref[...] = lax.fori_loop(0, tc, step, carry_ref[...], unroll=8)
        dh = dh_ref[...]
        da = dh * h_prev
        dmult = dh * ig * rec
        dig = dh * mult * rec
        drec = dh * mult * ig
        dla = da * a - dmult * (a * a) / mult
        dr = dla * (-LRU_C * sp)
        dsp = jnp.sum(dla * (-LRU_C * r), axis=0, keepdims=True)
        dlam_ref[...] += dsp * (-_sigmoid(-lam))
        dpr = (dr * r * (1.0 - r))
        dpi = (dig * ig * (1.0 - ig))
        dprb, dpib = dpr.astype(bf16), dpi.astype(bf16)
        drec = drec + _nt(dprb, wr_ref[...]) + _nt(dpib, wi_ref[...])
        dwr_ref[...] += _tn(rb, dprb)
        dwi_ref[...] += _tn(rb, dpib)
        dbr_ref[...] += jnp.sum(dpr, axis=0, keepdims=True)
        dbi_ref[...] += jnp.sum(dpi, axis=0, keepdims=True)
        dx, dw, db = _conv_bwd(buf_ref, dbuf_ref, drec, dnext_ref[...], cw, tc)
        dnext_ref[...] = drec[0:8, :]
        dcw_ref[...] += dw
        dcb_ref[...] += db
        dpa_ref[:, A_W:PA_W] = dx.astype(bf16)

    vec = pl.BlockSpec((1, A_W), lambda i: (0, 0))
    mat = pl.BlockSpec((A_W, A_W), lambda i: (0, 0))
    cwspec = pl.BlockSpec((4, A_W), lambda i: (0, 0))

    def rev(w):
        return pl.BlockSpec((tc, w), lambda i: (nc - 1 - i, 0))

    def halo(w):
        return pl.BlockSpec((8, w), lambda i: (jnp.maximum((nc - 1 - i) * (tc // 8) - 1, 0), 0))

    chunk = pltpu.VMEM((tc, A_W), f32)
    return pl.pallas_call(
        body, name="lru_bwd", grid=(nc,),
        in_specs=[rev(PA_W), halo(PA_W), rev(A_W), halo(A_W), rev(A_W),
                  *[_lspec(a, l) for a in (conv_w, conv_b, wr, wi, br, bi, lam)]],
        out_specs=[rev(PA_W), cwspec, vec, mat, mat, vec, vec, vec],
        out_shape=[jax.ShapeDtypeStruct((t, PA_W), bf16), jax.ShapeDtypeStruct((4, A_W), f32),
                   jax.ShapeDtypeStruct((1, A_W), f32), jax.ShapeDtypeStruct((A_W, A_W), f32),
                   jax.ShapeDtypeStruct((A_W, A_W), f32), jax.ShapeDtypeStruct((1, A_W), f32),
                   jax.ShapeDtypeStruct((1, A_W), f32), jax.ShapeDtypeStruct((1, A_W), f32)],
        scratch_shapes=[pltpu.VMEM((8 + tc, A_W), f32), pltpu.VMEM((tc + 8, A_W), f32), pltpu.VMEM((8 + tc, A_W), f32),
                        chunk, chunk, chunk, pltpu.VMEM((1, A_W), f32), pltpu.VMEM((8, A_W), f32)],
        compiler_params=_params(("arbitrary",)),
    )(pa, pa, h, h, dya, conv_w, conv_b, wr, wi, br, bi, lam)


def _sgu_norm(v, g, b):
    mu = jnp.mean(v, axis=-1, keepdims=True)
    vc = v - mu
    rstd = lax.rsqrt(jnp.mean(vc * vc, axis=-1, keepdims=True) + NORM_EPS)
    vh = vc * rstd
    return vh, rstd, vh * g + b


def _sgu_mix(w_ref, vb, bias):
    grp = lax.broadcasted_iota(jnp.int32, (CHUNK, C_W), 1) // HEAD
    out = bias
    for gi in range(C_W // HEAD):
        out = out + jnp.where(grp == gi, _nn(w_ref[gi], vb), 0.0)
    return out


def _sgu_fwd(pc, ln_g, ln_b, wm, bias, l):
    t = pc.shape[0]
    tm = _tile(t, 512)

    def body(pc_ref, g_ref, b_ref, w_ref, bias_ref, yc_ref):
        for ci in range(tm // CHUNK):
            rows = pl.ds(ci * CHUNK, CHUNK)
            ge = _gelu(pc_ref[rows, :])
            _, _, vn = _sgu_norm(ge[:, C_W:PC_W], g_ref[...], b_ref[...])
            yc_ref[rows, :] = ge[:, 0:C_W] * _sgu_mix(w_ref, vn.astype(bf16), bias_ref[...])

    vec = pl.BlockSpec((1, C_W), lambda i: (0, 0))
    return pl.pallas_call(
        body, name="sgu_fwd", grid=(t // tm,),
        in_specs=[pl.BlockSpec((tm, PC_W), lambda i: (i, 0)), *[_lspec(a, l) for a in (ln_g, ln_b, wm, bias)]],
        out_specs=pl.BlockSpec((tm, C_W), lambda i: (i, 0)),
        out_shape=jax.ShapeDtypeStruct((t, C_W), f32),
        compiler_params=_params(("arbitrary",)),
    )(pc, ln_g, ln_b, wm, bias)


def _sgu_bwd(pc, dyc, ln_g, ln_b, wm, wmt, bias, l):
    t = pc.shape[0]
    tm = _tile(t, 512)

    def body(pc_ref, dyc_ref, g_ref, b_ref, w_ref, wt_ref, bias_ref, dpc_ref, dw_ref, dbias_ref, dg_ref, db_ref):
        @pl.when(pl.program_id(0) == 0)
        def _():
            for ref in (dw_ref, dbias_ref, dg_ref, db_ref):
                ref[...] = jnp.zeros_like(ref)

        grp = lax.broadcasted_iota(jnp.int32, (CHUNK, C_W), 1) // HEAD
        for ci in range(tm // CHUNK):
            rows = pl.ds(ci * CHUNK, CHUNK)
            x = pc_ref[rows, :]
            ge = _gelu(x)
            gv = g_ref[...]
            vh, rstd, vn = _sgu_norm(ge[:, C_W:PC_W], gv, b_ref[...])
            vb = vn.astype(bf16)
            mixed = _sgu_mix(w_ref, vb, bias_ref[...])
            dyc = dyc_ref[rows, :]
            du = dyc * mixed
            dmix = dyc * ge[:, 0:C_W]
            dmb = dmix.astype(bf16)
            dvn = jnp.zeros((CHUNK, C_W), f32)
            for gi in range(C_W // HEAD):
                dvn = dvn + jnp.where(grp == gi, _nn(wt_ref[gi], dmb), 0.0)
                dw_ref[gi] += _nt(jnp.where(grp == gi, dmix, 0.0).astype(bf16), vb)
            dbias_ref[...] += dmix
            dg_ref[...] += jnp.sum(dvn * vh, axis=0, keepdims=True)
            db_ref[...] += jnp.sum(dvn, axis=0, keepdims=True)
            dvh = dvn * gv
            dv = rstd * (dvh - jnp.mean(dvh, axis=-1, keepdims=True) - vh * jnp.mean(dvh * vh, axis=-1, keepdims=True))
            gg = _gelu_grad(x)
            dpc_ref[rows, 0:C_W] = (du * gg[:, 0:C_W]).astype(bf16)
            dpc_ref[rows, C_W:PC_W] = (dv * gg[:, C_W:PC_W]).astype(bf16)

    vec = pl.BlockSpec((1, C_W), lambda i: (0, 0))
    wspec = pl.BlockSpec((4, CHUNK, CHUNK), lambda i: (0, 0, 0))
    bspec = pl.BlockSpec((CHUNK, C_W), lambda i: (0, 0))
    return pl.pallas_call(
        body, name="sgu_bwd", grid=(t // tm,),
        in_specs=[pl.BlockSpec((tm, PC_W), lambda i: (i, 0)), pl.BlockSpec((tm, C_W), lambda i: (i, 0)),
                  *[_lspec(a, l) for a in (ln_g, ln_b, wm, wmt, bias)]],
        out_specs=[pl.BlockSpec((tm, PC_W), lambda i: (i, 0)), wspec, bspec, vec, vec],
        out_shape=[jax.ShapeDtypeStruct((t, PC_W), bf16), jax.ShapeDtypeStruct((4, CHUNK, CHUNK), f32),
                   jax.ShapeDtypeStruct((CHUNK, C_W), f32), jax.ShapeDtypeStruct((1, C_W), f32),
                   jax.ShapeDtypeStruct((1, C_W), f32)],
        compiler_params=_params(("arbitrary",)),
    )(pc, dyc, ln_g, ln_b, wm, wmt, bias)


N_PAIR = B_W // 128
HEADS_PER_GROUP = 3


def _pair_groups(p):
    return (2 * p) // HEADS_PER_GROUP, (2 * p + 1) // HEADS_PER_GROUP


def _ssd_chunk(pb_ref, halo, buf_ref, cw, cb, dtb, alog):
    z = pb_ref[:, 0:B_W]
    pre = _conv_fwd(buf_ref, halo, pb_ref[:, B_W:B_W + XBC_W], cw, cb, CHUNK)
    sg = _sigmoid(pre)
    xbc = pre * sg
    xs = xbc[:, 0:B_W]
    bm = [xbc[:, B_W + k * B_STATE:B_W + (k + 1) * B_STATE] for k in range(2)]
    cm = [xbc[:, B_W + (2 + k) * B_STATE:B_W + (3 + k) * B_STATE] for k in range(2)]
    dtin = pb_ref[:, B_W + XBC_W:PB_W] + dtb
    dt = _softplus(dtin)
    a = -jnp.exp(alog)
    cs = _cumsum_rows(dt * a)
    return dict(z=z, pre=pre, sg=sg, xs=xs, bm=bm, cm=cm, dtin=dtin, dt=dt, a=a, cs=cs,
                ecs=jnp.exp(cs), ds=jnp.exp(cs[CHUNK - 1:CHUNK, :] - cs), xdt=xs * dt,
                bmb=[v.astype(bf16) for v in bm], cmb=[v.astype(bf16) for v in cm])


def _ssd_decay(cs_pair, half):
    cst = cs_pair.T
    lane0 = HEAD * half
    csc = jnp.broadcast_to(cs_pair[:, lane0:lane0 + 1], (CHUNK, CHUNK))
    csr = cst[lane0:lane0 + 1, :]
    tri = lax.broadcasted_iota(jnp.int32, (CHUNK, CHUNK), 0) >= lax.broadcasted_iota(jnp.int32, (CHUNK, CHUNK), 1)
    return jnp.exp(jnp.where(tri, csc - csr, NEG_BIG)), cst


def _ssd_fwd(pb, conv_w, conv_b, dtb, alog, dskip, norm_g, l):
    t = pb.shape[0]
    nc = t // CHUNK

    def body(pb_ref, halo_ref, cw_ref, cb_ref, dtb_ref, alog_ref, d_ref, ng_ref, yb_ref, yp_ref, sp_ref, buf_ref, s_ref):
        i = pl.program_id(0)

        @pl.when(i == 0)
        def _():
            s_ref[...] = jnp.zeros_like(s_ref)

        halo = jnp.where(i > 0, halo_ref[:, B_W:B_W + XBC_W], 0.0)
        q = _ssd_chunk(pb_ref, halo, buf_ref, cw_ref[...], cb_ref[...], dtb_ref[...], alog_ref[...])
        sp_ref[0] = s_ref[...]
        lane = lax.broadcasted_iota(jnp.int32, (CHUNK, 128), 1)
        rowi = lax.broadcasted_iota(jnp.int32, (128, B_STATE), 0)
        cb_mat = [_nt(q["cmb"][k], q["bmb"][k]) for k in range(2)]
        xd = q["xdt"] * q["ds"]
        for p in range(N_PAIR):
            cols = slice(128 * p, 128 * (p + 1))
            g_lo, g_hi = _pair_groups(p)
            cs_p, xdt_p = q["cs"][:, cols], q["xdt"][:, cols]
            s_p = s_ref[cols, :]
            s_pb = s_p.astype(bf16)
            y_p = jnp.zeros((CHUNK, 128), f32)
            for half, grp in ((0, g_lo), (1, g_hi)):
                lm, cst = _ssd_decay(cs_p, half)
                mb = (cb_mat[grp] * lm).astype(bf16)
                sel = (lane < HEAD) if half == 0 else (lane >= HEAD)
                y_p = y_p + _nn(mb, jnp.where(sel, xdt_p, 0.0).astype(bf16))
            off_lo = _nt(q["cmb"][g_lo], s_pb)
            off = off_lo if g_lo == g_hi else jnp.where(lane < HEAD, off_lo, _nt(q["cmb"][g_hi], s_pb))
            y_p = y_p + off * q["ecs"][:, cols] + q["xs"][:, cols] * d_ref[:, cols]
            yp_ref[:, cols] = y_p
            xd_pb = xd[:, cols].astype(bf16)
            upd_lo = _tn(xd_pb, q["bmb"][g_lo])
            upd = upd_lo if g_lo == g_hi else jnp.where(rowi < HEAD, upd_lo, _tn(xd_pb, q["bmb"][g_hi]))
            cd = jnp.exp(jnp.broadcast_to(cst[:, CHUNK - 1:CHUNK], (128, B_STATE)))
            s_ref[cols, :] = cd * s_p + upd
        z = q["z"]
        yg = yp_ref[...] * (z * _sigmoid(z))
        yb_ref[...] = _rms_fwd(yg, ng_ref[...])

    vec = pl.BlockSpec((1, B_W), lambda i: (0, 0))
    row = pl.BlockSpec((CHUNK, B_W), lambda i: (i, 0))
    return pl.pallas_call(
        body, name="ssd_fwd", grid=(nc,),
        in_specs=[pl.BlockSpec((CHUNK, PB_W), lambda i: (i, 0)),
                  pl.BlockSpec((8, PB_W), lambda i: (jnp.maximum(i * (CHUNK // 8) - 1, 0), 0)),
                  *[_lspec(a, l) for a in (conv_w, conv_b, dtb, alog, dskip, norm_g)]],
        out_specs=[row, row, pl.BlockSpec((1, B_W, B_STATE), lambda i: (i, 0, 0))],
        out_shape=[jax.ShapeDtypeStruct((t, B_W), f32), jax.ShapeDtypeStruct((t, B_W), f32),
                   jax.ShapeDtypeStruct((nc, B_W, B_STATE), f32)],
        scratch_shapes=[pltpu.VMEM((8 + CHUNK, XBC_W), f32), pltpu.VMEM((B_W, B_STATE), f32)],
        compiler_params=_params(("arbitrary",)),
    )(pb, pb, conv_w, conv_b, dtb, alog, dskip, norm_g)


def _ssd_bwd(pb, yp, sprev, dyb, conv_w, conv_b, dtb, alog, dskip, norm_g, l):
    t = pb.shape[0]
    nc = t // CHUNK

    def body(pb_ref, halo_ref, yp_ref, sp_ref, dyb_ref, cw_ref, cb_ref, dtb_ref, alog_ref, d_ref, ng_ref,
             dpb_ref, dcw_ref, dcb_ref, ddtb_ref, dalog_ref, dd_ref, dng_ref,
             buf_ref, dbuf_ref, ds_ref, dnext_ref, dxbc_ref, dcs_ref, dxdt_ref):
        i = pl.program_id(0)
        c = nc - 1 - i

        @pl.when(i == 0)
        def _():
            ds_ref[...] = jnp.zeros_like(ds_ref)
            dnext_ref[...] = jnp.zeros_like(dnext_ref)
            for ref in (dcw_ref, dcb_ref, ddtb_ref, dalog_ref, dd_ref, dng_ref):
                ref[...] = jnp.zeros_like(ref)

        halo = jnp.where(c > 0, halo_ref[:, B_W:B_W + XBC_W], 0.0)
        cw = cw_ref[...]
        q = _ssd_chunk(pb_ref, halo, buf_ref, cw, cb_ref[...], dtb_ref[...], alog_ref[...])
        z, xs, dt, a, ecs, dsd, xdt = q["z"], q["xs"], q["dt"], q["a"], q["ecs"], q["ds"], q["xdt"]
        sz = _sigmoid(z)
        siluz = z * sz
        yp = yp_ref[...]
        dyg, dng = _rms_bwd(yp * siluz, ng_ref[...], dyb_ref[...])
        dng_ref[...] += dng
        dy = dyg * siluz
        dpb_ref[:, 0:B_W] = (dyg * yp * _silu_grad(z, sz)).astype(bf16)
        dd_ref[...] += jnp.sum(dy * xs, axis=0, keepdims=True)
        g1 = dy * ecs
        lane = lax.broadcasted_iota(jnp.int32, (CHUNK, 128), 1)
        rowi = lax.broadcasted_iota(jnp.int32, (128, B_STATE), 0)
        rowc = lax.broadcasted_iota(jnp.int32, (CHUNK, 128), 0)
        cb_mat = [_nt(q["cmb"][k], q["bmb"][k]) for k in range(2)]
        d_cb = [jnp.zeros((CHUNK, CHUNK), f32) for _ in range(2)]
        d_b = [jnp.zeros((CHUNK, B_STATE), f32) for _ in range(2)]
        d_c = [jnp.zeros((CHUNK, B_STATE), f32) for _ in range(2)]
        for p in range(N_PAIR):
            cols = slice(128 * p, 128 * (p + 1))
            g_lo, g_hi = _pair_groups(p)
            lo, hi = lane < HEAD, lane >= HEAD
            cs_p, xdt_p, dy_p, ds_p, g1_p = q["cs"][:, cols], xdt[:, cols], dy[:, cols], dsd[:, cols], g1[:, cols]
            s_p = sp_ref[0, cols, :]
            s_pb = s_p.astype(bf16)
            dsn = ds_ref[cols, :]
            dsnb = dsn.astype(bf16)
            g1b = g1_p.astype(bf16)
            off_lo = _nt(q["cmb"][g_lo], s_pb)
            off = off_lo if g_lo == g_hi else jnp.where(lo, off_lo, _nt(q["cmb"][g_hi], s_pb))
            dcs_p = dy_p * off * ecs[:, cols]
            dsp_lo = _tn(g1b, q["cmb"][g_lo])
            dsp = dsp_lo if g_lo == g_hi else jnp.where(rowi < HEAD, dsp_lo, _tn(g1b, q["cmb"][g_hi]))
            dx_lo = _nt(q["bmb"][g_lo], dsnb)
            dxd = dx_lo if g_lo == g_hi else jnp.where(lo, dx_lo, _nt(q["bmb"][g_hi], dsnb))
            xd_p = xdt_p * ds_p
            if g_lo == g_hi:
                d_c[g_lo] = d_c[g_lo] + _nn(g1b, s_pb)
                d_b[g_lo] = d_b[g_lo] + _nn(xd_p.astype(bf16), dsnb)
            else:
                d_c[g_lo] = d_c[g_lo] + _nn(jnp.where(lo, g1_p, 0.0).astype(bf16), s_pb)
                d_c[g_hi] = d_c[g_hi] + _nn(jnp.where(hi, g1_p, 0.0).astype(bf16), s_pb)
                d_b[g_lo] = d_b[g_lo] + _nn(jnp.where(lo, xd_p, 0.0).astype(bf16), dsnb)
                d_b[g_hi] = d_b[g_hi] + _nn(jnp.where(hi, xd_p, 0.0).astype(bf16), dsnb)
            dxdt_p = dxd * ds_p
            t2 = dxd * xdt_p * ds_p
            dcs_p = dcs_p - t2
            dlast = jnp.sum(t2, axis=0, keepdims=True)
            cst = None
            for half, grp in ((0, g_lo), (1, g_hi)):
                sel = lo if half == 0 else hi
                lm, cst = _ssd_decay(cs_p, half)
                m = cb_mat[grp] * lm
                dyh = jnp.where(sel, dy_p, 0.0).astype(bf16)
                xdh = jnp.where(sel, xdt_p, 0.0).astype(bf16)
                dm = _nt(dyh, xdh)
                pm = dm * m
                col = jnp.sum(pm, axis=1, keepdims=True) - jnp.sum(pm.T, axis=1, keepdims=True)
                dcs_p = dcs_p + jnp.where(lane == HEAD * half, col, 0.0)
                d_cb[grp] = d_cb[grp] + dm * lm
                dxdt_p = dxdt_p + _tn(m.astype(bf16), dyh)
            cdcol = jnp.exp(jnp.broadcast_to(cst[:, CHUNK - 1:CHUNK], (128, B_STATE)))
            ds_ref[cols, :] = cdcol * dsn + dsp
            dcd_row = jnp.sum((dsn * s_p).T, axis=0, keepdims=True)
            dlast = dlast + dcd_row * ecs[CHUNK - 1:CHUNK, cols]
            dcs_ref[:, cols] = dcs_p + jnp.where(rowc == CHUNK - 1, dlast, 0.0)
            dxdt_ref[:, cols] = dxdt_p
        for k in range(2):
            dcbb = d_cb[k].astype(bf16)
            d_c[k] = d_c[k] + _nn(dcbb, q["bmb"][k])
            d_b[k] = d_b[k] + _tn(dcbb, q["cmb"][k])
            dxbc_ref[:, B_W + k * B_STATE:B_W + (k + 1) * B_STATE] = d_b[k]
            dxbc_ref[:, B_W + (2 + k) * B_STATE:B_W + (3 + k) * B_STATE] = d_c[k]
        dxdt = dxdt_ref[...]
        dxbc_ref[:, 0:B_W] = dy * d_ref[...] + dxdt * dt
        dcs = dcs_ref[...]
        dad = jnp.sum(dcs, axis=0, keepdims=True) - _cumsum_rows(dcs) + dcs
        ddt = dxdt * xs + dad * a
        dalog_ref[...] += jnp.sum(dad * dt, axis=0, keepdims=True) * a
        dtraw = ddt * _sigmoid(q["dtin"])
        ddtb_ref[...] += jnp.sum(dtraw, axis=0, keepdims=True)
        dpb_ref[:, B_W + XBC_W:PB_W] = dtraw.astype(bf16)
        dpre = dxbc_ref[...] * _silu_grad(q["pre"], q["sg"])
        dx, dw, db = _conv_bwd(buf_ref, dbuf_ref, dpre, dnext_ref[...], cw, CHUNK)
        dnext_ref[...] = dpre[0:8, :]
        dcw_ref[...] += dw
        dcb_ref[...] += db
        dpb_ref[:, B_W:B_W + XBC_W] = dx.astype(bf16)

    vec = pl.BlockSpec((1, B_W), lambda i: (0, 0))
    cwspec = pl.BlockSpec((4, XBC_W), lambda i: (0, 0))
    cbspec = pl.BlockSpec((1, XBC_W), lambda i: (0, 0))

    def rev(w):
        return pl.BlockSpec((CHUNK, w), lambda i: (nc - 1 - i, 0))

    vshape = jax.ShapeDtypeStruct((1, B_W), f32)
    return pl.pallas_call(
        body, name="ssd_bwd", grid=(nc,),
        in_specs=[rev(PB_W), pl.BlockSpec((8, PB_W), lambda i: (jnp.maximum((nc - 1 - i) * (CHUNK // 8) - 1, 0), 0)),
                  rev(B_W), pl.BlockSpec((1, B_W, B_STATE), lambda i: (nc - 1 - i, 0, 0)), rev(B_W),
                  *[_lspec(a, l) for a in (conv_w, conv_b, dtb, alog, dskip, norm_g)]],
        out_specs=[rev(PB_W), cwspec, cbspec, vec, vec, vec, vec],
        out_shape=[jax.ShapeDtypeStruct((t, PB_W), bf16), jax.ShapeDtypeStruct((4, XBC_W), f32),
                   jax.ShapeDtypeStruct((1, XBC_W), f32), vshape, vshape, vshape, vshape],
        scratch_shapes=[pltpu.VMEM((8 + CHUNK, XBC_W), f32), pltpu.VMEM((CHUNK + 8, XBC_W), f32),
                        pltpu.VMEM((B_W, B_STATE), f32), pltpu.VMEM((8, XBC_W), f32),
                        pltpu.VMEM((CHUNK, XBC_W), f32), pltpu.VMEM((CHUNK, B_W), f32), pltpu.VMEM((CHUNK, B_W), f32)],
        compiler_params=_params(("arbitrary",)),
    )(pb, pb, yp, sprev, dyb, conv_w, conv_b, dtb, alog, dskip, norm_g)


def _loss_fwd(y, target):
    t, d = y.shape
    tm = _tile(t, 512)

    def body(y_ref, t_ref, dy_ref, loss_ref):
        @pl.when(pl.program_id(0) == 0)
        def _():
            loss_ref[...] = jnp.zeros_like(loss_ref)

        e = y_ref[...] - t_ref[...]
        dy_ref[...] = e * (1.0 / d)
        per_tok = jnp.mean(e * e, axis=-1, keepdims=True)
        loss_ref[...] += 0.5 * jnp.sum(per_tok, axis=0, keepdims=True)

    row = pl.BlockSpec((tm, d), lambda i: (i, 0))
    return pl.pallas_call(
        body, name="loss_fwd", grid=(t // tm,), in_specs=[row, row],
        out_specs=[row, pl.BlockSpec((1, 128), lambda i: (0, 0))],
        out_shape=[jax.ShapeDtypeStruct((t, d), f32), jax.ShapeDtypeStruct((1, 128), f32)],
        compiler_params=_params(("arbitrary",)),
    )(y, target)


def _row_tile(r):
    return 512 if r % 512 == 0 else r


def _pair_add(g, r, c_dev):
    _, nl, rows, cols = g.shape
    tr = _row_tile(rows)

    def body(c_ref, g_ref, r_ref, o_ref):
        o_ref[...] = (g_ref[...].astype(f32) + r_ref[...].astype(f32)).astype(bf16)

    blk = (None, None, tr, cols)
    return pl.pallas_call(
        body, name="pair_add",
        grid_spec=pltpu.PrefetchScalarGridSpec(
            num_scalar_prefetch=1, grid=(4, nl, rows // tr),
            in_specs=[pl.BlockSpec(blk, lambda b, l, i, c: (2 * b + c[0], l, i, 0)),
                      pl.BlockSpec(blk, lambda b, l, i, c: (b, l, i, 0))],
            out_specs=pl.BlockSpec(blk, lambda b, l, i, c: (b, l, i, 0))),
        out_shape=jax.ShapeDtypeStruct(r.shape, bf16),
        compiler_params=_params(("arbitrary", "arbitrary", "arbitrary")),
    )(c_dev, g, r)


def _grad_sum(s, q, b_dev):
    _, nl, rows, cols = s.shape
    tr = _row_tile(rows)

    def body(b_ref, s_ref, q0_ref, q1_ref, q2_ref, o_ref):
        o_ref[...] = ((s_ref[...].astype(f32) + q0_ref[...].astype(f32)) + q1_ref[...].astype(f32)) + q2_ref[...].astype(f32)

    blk = (None, None, tr, cols)

    def qspec(k):
        return pl.BlockSpec(blk, lambda l, i, b: (k, l, i, 0))

    return pl.pallas_call(
        body, name="grad_sum",
        grid_spec=pltpu.PrefetchScalarGridSpec(
            num_scalar_prefetch=1, grid=(nl, rows // tr),
            in_specs=[pl.BlockSpec(blk, lambda l, i, b: (b[0], l, i, 0)), qspec(0), qspec(1), qspec(2)],
            out_specs=pl.BlockSpec((None, tr, cols), lambda l, i, b: (l, i, 0))),
        out_shape=jax.ShapeDtypeStruct(s.shape[1:], f32),
        compiler_params=_params(("arbitrary", "arbitrary")),
    )(b_dev, s, q, q, q)


def _sum_devices(parts):
    n, rows, cols = parts.shape
    tr = _row_tile(rows)

    def body(p_ref, o_ref):
        acc = p_ref[0]
        for k in range(1, n):
            acc = acc + p_ref[k]
        o_ref[...] = acc

    return pl.pallas_call(
        body, name="sum_devices", grid=(rows // tr,),
        in_specs=[pl.BlockSpec((n, tr, cols), lambda i: (0, i, 0))],
        out_specs=pl.BlockSpec((tr, cols), lambda i: (i, 0)),
        out_shape=jax.ShapeDtypeStruct((rows, cols), f32),
        compiler_params=_params(("arbitrary",)),
    )(parts)


def _adamw(w, m, v, g):
    nl, rows, cols = w.shape
    tr = _row_tile(rows)

    def body(w_ref, m_ref, v_ref, g_ref, d_ref, nm_ref, nv_ref):
        gv = g_ref[...]
        nm = ADAM_B1 * m_ref[...] + (1.0 - ADAM_B1) * gv
        nv = ADAM_B2 * v_ref[...] + (1.0 - ADAM_B2) * (gv * gv)
        nm_ref[...] = nm
        nv_ref[...] = nv
        m_hat = nm / (1.0 - ADAM_B1 ** ADAM_STEP)
        v_hat = nv / (1.0 - ADAM_B2 ** ADAM_STEP)
        d_ref[...] = -ADAM_LR * (m_hat / (jnp.sqrt(v_hat) + ADAM_EPS) + ADAM_WD * w_ref[...])

    blk = pl.BlockSpec((None, tr, cols), lambda l, i: (l, i, 0))
    shape = jax.ShapeDtypeStruct(w.shape, f32)
    return pl.pallas_call(
        body, name="adamw", grid=(nl, rows // tr), in_specs=[blk] * 4, out_specs=[blk] * 3,
        out_shape=[shape] * 3, compiler_params=_params(("arbitrary", "arbitrary")),
    )(w, m, v, g)


def _place():
    return lax.axis_index("x"), lax.axis_index("y"), lax.axis_index("c")


def _all_gather(shards):
    n = len(shards)

    def body(*refs):
        src, dst = refs[:n], refs[n:2 * n]
        send_sems, recv_sems, local_sems = refs[2 * n:]
        x, y, c = _place()
        me, sibling = (x, y, c), (x, y, 1 - c)
        chips = [(1 - x, y), (x, 1 - y), (1 - x, 1 - y)]

        def copy(a, k, block, to, from_shard=False):
            px, py, pc = block
            rows = dst[a].at[4 * px + 2 * py + pc]
            return pltpu.make_async_remote_copy(
                src_ref=src[a] if from_shard else rows, dst_ref=rows,
                send_sem=send_sems.at[a, k], recv_sem=recv_sems.at[a, k], device_id=to, device_id_type=MESH)

        mine = [pltpu.make_async_copy(src[a], dst[a].at[4 * x + 2 * y + c], local_sems.at[a]) for a in range(n)]
        for cp in mine:
            cp.start()
        first = []
        for a in range(n):
            first.append(copy(a, 0, me, sibling, True))
            first += [copy(a, 1 + j, me, (*chip, c), True) for j, chip in enumerate(chips)]
        for cp in first:
            cp.start()
        passed = []
        for j, chip in enumerate(chips):
            for a in range(n):
                copy(a, 1 + j, (*chip, c), me).wait_recv()
                fwd = copy(a, 4 + j, (*chip, c), sibling)
                fwd.start()
                passed.append(fwd)
        for a in range(n):
            copy(a, 0, sibling, me).wait_recv()
            for j, chip in enumerate(chips):
                copy(a, 4 + j, (*chip, 1 - c), me).wait_recv()
        for cp in first + passed:
            cp.wait_send()
        for cp in mine:
            cp.wait()

    return pl.pallas_call(
        body, name="all_gather", in_specs=[ANY] * n, out_specs=[ANY] * n,
        out_shape=[jax.ShapeDtypeStruct((N_DEV,) + s.shape, s.dtype) for s in shards],
        scratch_shapes=[pltpu.SemaphoreType.DMA((n, 7)), pltpu.SemaphoreType.DMA((n, 7)), pltpu.SemaphoreType.DMA((n,))],
    )(*shards)


def _exchange_pair(grads):
    n = len(grads)

    def body(*refs):
        src, dst = refs[:n], refs[n:2 * n]
        send_sems, recv_sems = refs[2 * n:]
        x, y, c = _place()
        copies = [pltpu.make_async_remote_copy(
            src_ref=src[a].at[2 * b + (1 - c)], dst_ref=dst[a].at[b], send_sem=send_sems.at[a, b],
            recv_sem=recv_sems.at[a, b], device_id=(x, y, 1 - c), device_id_type=MESH)
            for a in range(n) for b in range(4)]
        for cp in copies:
            cp.start()
        for cp in copies:
            cp.wait()

    return pl.pallas_call(
        body, name="exchange_pair", in_specs=[ANY] * n, out_specs=[ANY] * n,
        out_shape=[jax.ShapeDtypeStruct((4,) + g.shape[1:], g.dtype) for g in grads],
        scratch_shapes=[pltpu.SemaphoreType.DMA((n, 4)), pltpu.SemaphoreType.DMA((n, 4))],
    )(*grads)


def _exchange_chips(sums):
    n = len(sums)

    def body(*refs):
        src, dst = refs[:n], refs[n:2 * n]
        send_sems, recv_sems = refs[2 * n:]
        x, y, c = _place()
        chips = [(1 - x, y), (x, 1 - y), (1 - x, 1 - y)]
        copies = [pltpu.make_async_remote_copy(
            src_ref=src[a].at[2 * px + py], dst_ref=dst[a].at[j], send_sem=send_sems.at[a, j],
            recv_sem=recv_sems.at[a, j], device_id=(px, py, c), device_id_type=MESH)
            for a in range(n) for j, (px, py) in enumerate(chips)]
        for cp in copies:
            cp.start()
        for cp in copies:
            cp.wait()

    return pl.pallas_call(
        body, name="exchange_chips", in_specs=[ANY] * n, out_specs=[ANY] * n,
        out_shape=[jax.ShapeDtypeStruct((3,) + s.shape[1:], s.dtype) for s in sums],
        scratch_shapes=[pltpu.SemaphoreType.DMA((n, 3)), pltpu.SemaphoreType.DMA((n, 3))],
    )(*sums)


BIG = ("ffn1_w_gu", "ffn1_w_down", "mix_w_in", "mix_w_out", "ffn2_w_gu", "ffn2_w_down")
SHARDED_CONV = ("lru_conv_w", "ssd_conv_w")
REPLICATED = ("ffn1_pre_g", "ffn1_post_g", "mix_pre_g", "mix_post_g", "lru_conv_b", "lru_w_r", "lru_b_r", "lru_w_i",
              "lru_b_i", "lru_lambda", "ssd_conv_b", "ssd_dt_bias", "ssd_a_log", "ssd_d", "ssd_norm_g", "sgu_ln_g",
              "sgu_ln_b", "sgu_w_s", "sgu_b_s", "ffn2_pre_g", "ffn2_post_g")
WEIGHTS = ("ffn1_pre_g", "ffn1_post_g", "ffn1_w_gu", "ffn1_w_down", "mix_pre_g", "mix_post_g", "mix_w_in", "mix_w_out",
           "lru_conv_w", "lru_conv_b", "lru_w_r", "lru_b_r", "lru_w_i", "lru_b_i", "lru_lambda", "ssd_conv_w",
           "ssd_conv_b", "ssd_dt_bias", "ssd_a_log", "ssd_d", "ssd_norm_g", "sgu_ln_g", "sgu_ln_b", "sgu_w_s", "sgu_b_s",
           "ffn2_pre_g", "ffn2_post_g", "ffn2_w_gu", "ffn2_w_down")
DT_LO = PA_W + B_W + XBC_W
N_HEADS = B_W // HEAD
PACK_COLS = 1024


def _pack(arrays):
    flat = jnp.concatenate([a.reshape(-1) for a in arrays])
    rows = -(-flat.shape[0] // (8 * PACK_COLS)) * 8
    return jnp.pad(flat, (0, rows * PACK_COLS - flat.shape[0])).reshape(rows, PACK_COLS)


def _unpack(packed, shapes):
    flat = packed.reshape(-1)
    out, off = [], 0
    for s in shapes:
        size = 1
        for dim in s:
            size *= dim
        out.append(flat[off:off + size].reshape(s))
        off += size
    return out


def _widen_w_in(w):
    return jnp.concatenate([w[..., :DT_LO], jnp.repeat(w[..., DT_LO:DT_LO + N_HEADS], HEAD, axis=-1),
                            w[..., DT_LO + N_HEADS:]], axis=-1)


def _narrow_w_in_grad(g):
    dt = g[..., DT_LO:DT_LO + B_W]
    dt = dt.reshape(dt.shape[:-1] + (N_HEADS, HEAD)).sum(-1)
    return jnp.concatenate([g[..., :DT_LO], dt, g[..., DT_LO + B_W:]], axis=-1)


def _per_head(a):
    return a.reshape(a.shape[:-1] + (N_HEADS, HEAD)).sum(-1)


def kernel(x, ffn1_pre_g, ffn1_post_g, ffn1_w_gu, ffn1_w_down, mix_pre_g, mix_post_g, mix_w_in, mix_w_out, lru_conv_w, lru_conv_b, lru_w_r, lru_b_r, lru_w_i, lru_b_i, lru_lambda, ssd_conv_w, ssd_conv_b, ssd_dt_bias, ssd_a_log, ssd_d, ssd_norm_g, sgu_ln_g, sgu_ln_b, sgu_w_s, sgu_b_s, ffn2_pre_g, ffn2_post_g, ffn2_w_gu, ffn2_w_down, loss_target, m_ffn1_pre_g, m_ffn1_post_g, m_ffn1_w_gu, m_ffn1_w_down, m_mix_pre_g, m_mix_post_g, m_mix_w_in, m_mix_w_out, m_lru_conv_w, m_lru_conv_b, m_lru_w_r, m_lru_b_r, m_lru_w_i, m_lru_b_i, m_lru_lambda, m_ssd_conv_w, m_ssd_conv_b, m_ssd_dt_bias, m_ssd_a_log, m_ssd_d, m_ssd_norm_g, m_sgu_ln_g, m_sgu_ln_b, m_sgu_w_s, m_sgu_b_s, m_ffn2_pre_g, m_ffn2_post_g, m_ffn2_w_gu, m_ffn2_w_down, v_ffn1_pre_g, v_ffn1_post_g, v_ffn1_w_gu, v_ffn1_w_down, v_mix_pre_g, v_mix_post_g, v_mix_w_in, v_mix_w_out, v_lru_conv_w, v_lru_conv_b, v_lru_w_r, v_lru_b_r, v_lru_w_i, v_lru_b_i, v_lru_lambda, v_ssd_conv_w, v_ssd_conv_b, v_ssd_dt_bias, v_ssd_a_log, v_ssd_d, v_ssd_norm_g, v_sgu_ln_g, v_sgu_ln_b, v_sgu_w_s, v_sgu_b_s, v_ffn2_pre_g, v_ffn2_post_g, v_ffn2_w_gu, v_ffn2_w_down):
    given = dict(locals())
    w = {n: given[n] for n in WEIGHTS}
    mom = {n: given["m_" + n] for n in WEIGHTS}
    var = {n: given["v_" + n] for n in WEIGHTS}
    nl = ffn1_pre_g.shape[0]
    _, t, d = x.shape
    xi, yi, ci = _place()
    dev = 4 * xi + 2 * yi + ci
    c_dev = jnp.reshape(ci, (1,)).astype(jnp.int32)
    b_dev = jnp.reshape(2 * xi + yi, (1,)).astype(jnp.int32)

    conv_shapes = [lru_conv_w.shape, ssd_conv_w.shape]
    shards = [ffn1_w_gu.astype(bf16), ffn1_w_down.astype(bf16), _widen_w_in(mix_w_in).astype(bf16),
              mix_w_out.astype(bf16), ffn2_w_gu.astype(bf16), ffn2_w_down.astype(bf16),
              _pack([lru_conv_w, ssd_conv_w])]
    wgu1, wd1, win, wout, wgu2, wd2, conv_all = _all_gather(shards)
    conv_full = []
    for k, shape in enumerate(conv_shapes):
        per_dev = jnp.stack([_unpack(conv_all[s], conv_shapes)[k] for s in range(N_DEV)], axis=2)
        conv_full.append(per_dev.reshape(shape[0], shape[1], N_DEV * shape[2]))
    lru_cw, ssd_cw = conv_full

    def vec(a):
        return a.reshape(nl, 1, -1)

    def per_channel(a):
        return jnp.repeat(a, HEAD, axis=-1).reshape(nl, 1, B_W)

    eye = jnp.eye(A_W // HEAD, dtype=f32)

    def block_diag(a):
        return jnp.einsum("lhij,hg->lhigj", a, eye).reshape(nl, A_W, A_W).astype(bf16)

    causal = jnp.tril(jnp.ones((CHUNK, CHUNK), dtype=bool))
    p = dict(
        ffn1_pre=vec(ffn1_pre_g), ffn1_post=vec(ffn1_post_g), mix_pre=vec(mix_pre_g), mix_post=vec(mix_post_g),
        ffn2_pre=vec(ffn2_pre_g), ffn2_post=vec(ffn2_post_g),
        lru=(lru_cw, vec(lru_conv_b), block_diag(lru_w_r), block_diag(lru_w_i), vec(lru_b_r), vec(lru_b_i), vec(lru_lambda)),
        ssd=(ssd_cw, vec(ssd_conv_b), per_channel(ssd_dt_bias), per_channel(ssd_a_log), per_channel(ssd_d), vec(ssd_norm_g)),
    )
    wm = jnp.where(causal, sgu_w_s, 0.0).astype(bf16)
    sgu_bias = jnp.repeat(jnp.swapaxes(sgu_b_s, 1, 2), HEAD, axis=2)
    sgu_f = (vec(sgu_ln_g), vec(sgu_ln_b), wm, sgu_bias)
    sgu_b = (vec(sgu_ln_g), vec(sgu_ln_b), wm, jnp.swapaxes(wm, 2, 3), sgu_bias)

    xs = x.reshape(t, d)
    saved = []
    for l in range(nl):
        x0 = xs
        x1, hb1, g1, u1, f1 = _ffn_fwd(x0, p["ffn1_pre"], p["ffn1_post"], wgu1, wd1, l)
        hbm, pa, pb, pc = _mix_in_fwd(x1, p["mix_pre"], win, l)
        ya, h = _lru_fwd(pa, *p["lru"], l)
        yb, yp, sp = _ssd_fwd(pb, *p["ssd"], l)
        yc = _sgu_fwd(pc, *sgu_f, l)
        x2, cat, m = _mix_out_fwd(x1, ya, yb, yc, p["mix_post"], wout, l)
        xs, hb2, g2, u2, f2 = _ffn_fwd(x2, p["ffn2_pre"], p["ffn2_post"], wgu2, wd2, l)
        saved.append((x0, hb1, g1, u1, f1, x1, hbm, pa, pb, pc, h, yp, sp, cat, m, x2, hb2, g2, u2, f2))
    dy, loss_part = _loss_fwd(xs, loss_target.reshape(t, d))
    loss = lax.psum(loss_part[0, 0], ("x", "y", "c"))

    gbuf = {n: lax.empty((N_DEV,) + s.shape, bf16) for n, s in zip(BIG, shards[:6])}
    small = {n: [None] * nl for n in REPLICATED + SHARDED_CONV}
    for l in reversed(range(nl)):
        x0, hb1, g1, u1, f1, x1, hbm, pa, pb, pc, h, yp, sp, cat, m, x2, hb2, g2, u2, f2 = saved[l]
        dx2, dfb, act, dg, du, dpre, dpost = _ffn_bwd(x2, dy, f2, p["ffn2_pre"], p["ffn2_post"], g2, u2, wgu2, wd2, l)
        small["ffn2_pre_g"][l], small["ffn2_post_g"][l] = dpre[0], dpost[0]
        gbuf["ffn2_w_gu"] = _wgrad_cols(hb2, dg, gbuf["ffn2_w_gu"], l, 0)
        gbuf["ffn2_w_gu"] = _wgrad_cols(hb2, du, gbuf["ffn2_w_gu"], l, dg.shape[0])
        gbuf["ffn2_w_down"] = _wgrad_rows(act, dfb, gbuf["ffn2_w_down"], l)

        dm, dya, dyb, dyc, dpost = _mix_out_bwd(dx2, m, p["mix_post"], wout, l)
        small["mix_post_g"][l] = dpost[0]
        gbuf["mix_w_out"] = _wgrad_kblocks(cat, [dm], gbuf["mix_w_out"], l)
        dpc, dws, dbias, dlg, dlb = _sgu_bwd(pc, dyc, *sgu_b, l)
        small["sgu_w_s"][l] = jnp.where(causal, dws, 0.0)
        small["sgu_b_s"][l] = dbias.reshape(CHUNK, C_W // HEAD, HEAD).sum(-1).T
        small["sgu_ln_g"][l], small["sgu_ln_b"][l] = dlg[0], dlb[0]
        dpb, dcw, dcb, ddtb, dalog, ddsk, dng = _ssd_bwd(pb, yp, sp, dyb, *p["ssd"], l)
        small["ssd_conv_w"][l], small["ssd_conv_b"][l], small["ssd_norm_g"][l] = dcw, dcb[0], dng[0]
        small["ssd_dt_bias"][l], small["ssd_a_log"][l], small["ssd_d"][l] = _per_head(ddtb[0]), _per_head(dalog[0]), _per_head(ddsk[0])
        dpa, dcw, dcb, dwr, dwi, dbr, dbi, dlam = _lru_bwd(pa, h, dya, *p["lru"], l)
        small["lru_conv_w"][l], small["lru_conv_b"][l], small["lru_lambda"][l] = dcw, dcb[0], dlam[0]
        small["lru_b_r"][l], small["lru_b_i"][l] = dbr[0], dbi[0]
        heads = range(A_W // HEAD)
        small["lru_w_r"][l] = jnp.stack([dwr[HEAD * i:HEAD * (i + 1), HEAD * i:HEAD * (i + 1)] for i in heads])
        small["lru_w_i"][l] = jnp.stack([dwi[HEAD * i:HEAD * (i + 1), HEAD * i:HEAD * (i + 1)] for i in heads])
        dx1, dpre = _mix_in_bwd(x1, dx2, p["mix_pre"], dpa, dpb, dpc, win, l)
        small["mix_pre_g"][l] = dpre[0]
        gbuf["mix_w_in"] = _wgrad_kblocks(hbm, [dpa, dpb, dpc], gbuf["mix_w_in"], l)

        dy, dfb, act, dg, du, dpre, dpost = _ffn_bwd(x0, dx1, f1, p["ffn1_pre"], p["ffn1_post"], g1, u1, wgu1, wd1, l)
        small["ffn1_pre_g"][l], small["ffn1_post_g"][l] = dpre[0], dpost[0]
        gbuf["ffn1_w_gu"] = _wgrad_cols(hb1, dg, gbuf["ffn1_w_gu"], l, 0)
        gbuf["ffn1_w_gu"] = _wgrad_cols(hb1, du, gbuf["ffn1_w_gu"], l, dg.shape[0])
        gbuf["ffn1_w_down"] = _wgrad_rows(act, dfb, gbuf["ffn1_w_down"], l)
    grad_x = dy.reshape(x.shape)

    grads, delta, new_m, new_v = {}, {}, {}, {}
    recv = _exchange_pair([gbuf[n] for n in BIG])
    sums = [_pair_add(gbuf[n], r, c_dev) for n, r in zip(BIG, recv)]
    others = _exchange_chips(sums)
    for n, s, q in zip(BIG, sums, others):
        g = _grad_sum(s, q, b_dev)
        if n == "mix_w_in":
            g = _narrow_w_in_grad(g)
        grads[n] = g.reshape(w[n].shape)
        delta[n], new_m[n], new_v[n] = _adamw(w[n], mom[n], var[n], grads[n])

    names = REPLICATED + SHARDED_CONV
    stacked = [jnp.stack(small[n]) for n in names]
    total = _sum_devices(_all_gather([_pack(stacked)])[0])
    full = dict(zip(names, _unpack(total, [a.shape for a in stacked])))
    for n in REPLICATED:
        grads[n] = full[n]
    for n in SHARDED_CONV:
        cols = w[n].shape[2]
        grads[n] = lax.dynamic_slice_in_dim(full[n], dev * cols, cols, axis=2)
    shapes = [w[n].shape for n in names]
    packs = [_pack([src[n] for n in names])[None] for src in (w, mom, var, grads)]
    for dst, packed in zip((delta, new_m, new_v), _adamw(*packs)):
        dst.update(zip(names, _unpack(packed[0], shapes)))

    return (loss, grad_x, *[grads[n] for n in WEIGHTS], *[delta[n] for n in WEIGHTS],
            *[new_m[n] for n in WEIGHTS], *[new_v[n] for n in WEIGHTS])
```

```python
import functools

import jax
import jax.numpy as jnp
from jax import lax
from jax.experimental import pallas as pl
from jax.experimental.pallas import tpu as pltpu

f32, bf16 = jnp.float32, jnp.bfloat16
MESH = pl.DeviceIdType.MESH
ANY = pl.BlockSpec(memory_space=pl.ANY)

N_DEV = 8
NORM_EPS = 1e-6
LRU_C = 8.0
CHUNK = 128
HEAD = 64
A_W, B_W, C_W = 384, 384, 256
B_STATE = 128
XBC_W = B_W + 4 * B_STATE
PA_W, PB_W, PC_W = 2 * A_W, B_W + XBC_W + B_W, 2 * C_W
IN_PAD = PA_W + PB_W + PC_W
ADAM_LR, ADAM_B1, ADAM_B2, ADAM_EPS, ADAM_WD, ADAM_STEP = 0.001, 0.9, 0.999, 1e-08, 0.01, 10
VMEM_LIMIT_BYTES = 56 * 1024 * 1024
NEG_BIG = -1e30


def _params(sem=None):
    return pltpu.CompilerParams(dimension_semantics=sem, vmem_limit_bytes=VMEM_LIMIT_BYTES)


def _nn(a, b):
    return jnp.dot(a, b, preferred_element_type=f32)


def _nt(a, b):
    return lax.dot_general(a, b, (((1,), (1,)), ((), ())), preferred_element_type=f32)


def _tn(a, b):
    return lax.dot_general(a, b, (((0,), (0,)), ((), ())), preferred_element_type=f32)


def _sigmoid(x):
    return 1.0 / (1.0 + jnp.exp(-x))


def _softplus(x):
    return jnp.maximum(x, 0.0) + jnp.log(1.0 + jnp.exp(-jnp.abs(x)))


_GELU_C0, _GELU_C1 = 0.7978845608028654, 0.044715


def _gelu(x):
    t = jnp.tanh(_GELU_C0 * (x + _GELU_C1 * x * x * x))
    return 0.5 * x * (1.0 + t)


def _gelu_grad(x):
    t = jnp.tanh(_GELU_C0 * (x + _GELU_C1 * x * x * x))
    return 0.5 * (1.0 + t) + 0.5 * x * (1.0 - t * t) * _GELU_C0 * (1.0 + 3.0 * _GELU_C1 * x * x)


def _silu_grad(x, s):
    return s * (1.0 + x * (1.0 - s))


def _rms_fwd(x, g):
    r = lax.rsqrt(jnp.mean(x * x, axis=-1, keepdims=True) + NORM_EPS)
    return x * r * g


def _rms_bwd(x, g, dy):
    r = lax.rsqrt(jnp.mean(x * x, axis=-1, keepdims=True) + NORM_EPS)
    xh = x * r
    dxh = dy * g
    dx = r * (dxh - xh * jnp.mean(dxh * xh, axis=-1, keepdims=True))
    return dx, jnp.sum(dy * xh, axis=0, keepdims=True)


def _one_minus_exp(x):
    series = -x * (1.0 + x * (0.5 + x * (1.0 / 6.0 + x * (1.0 / 24.0))))
    return jnp.where(x > -0.01, series, 1.0 - jnp.exp(x))


def _cumsum_rows(x):
    row = lax.broadcasted_iota(jnp.int32, x.shape, 0)
    d = 1
    while d < x.shape[0]:
        x = x + jnp.where(row >= d, pltpu.roll(x, d, 0), 0.0)
        d *= 2
    return x


def _tile(t, cap):
    tm = min(cap, t)
    assert t % tm == 0
    return tm


def _after(body, n_in, deps):
    def wrapped(*refs):
        return body(*refs[:n_in], *refs[n_in + len(deps):])
    return wrapped


def _lspec(a, l):
    return pl.BlockSpec((None,) + a.shape[1:], lambda *_: (l,) + (0,) * (a.ndim - 1))


def _wd_rows(wd_ref):
    return wd_ref[:, 0].reshape(2 * wd_ref.shape[2], wd_ref.shape[3])


def _ffn_fwd(x, pre_g, post_g, wgu, wd, l, deps=()):
    t, d = x.shape
    nb, _, _, h = wgu.shape
    nj = nb // 2
    tm = _tile(t, 512)

    def body(x_ref, pg_ref, qg_ref, wg_ref, wu_ref, wd_ref, y_ref, hb_ref, g_ref, u_ref, f_ref, acc_ref):
        j = pl.program_id(1)

        @pl.when(j == 0)
        def _():
            hb_ref[...] = _rms_fwd(x_ref[...], pg_ref[...]).astype(bf16)

        hb = hb_ref[...]
        g = _nn(hb, wg_ref[0, 0])
        u = _nn(hb, wu_ref[0, 0])
        g_ref[0] = g.astype(bf16)
        u_ref[0] = u.astype(bf16)
        a = (g * _sigmoid(g) * u).astype(bf16)
        part = _nn(a, _wd_rows(wd_ref))

        @pl.when(j == 0)
        def _():
            acc_ref[...] = part

        @pl.when(j > 0)
        def _():
            acc_ref[...] += part

        @pl.when(j == nj - 1)
        def _():
            f = acc_ref[...]
            f_ref[...] = f
            y_ref[...] = x_ref[...] + 0.5 * _rms_fwd(f, qg_ref[...])

    row = pl.BlockSpec((tm, d), lambda i, j: (i, 0))
    vec = pl.BlockSpec((1, d), lambda i, j: (0, 0))
    act = pl.BlockSpec((1, tm, h), lambda i, j: (j, i, 0))
    return pl.pallas_call(
        _after(body, 6, deps), name="ffn_fwd", grid=(t // tm, nj),
        in_specs=[row, _lspec(pre_g, l), _lspec(post_g, l),
                  pl.BlockSpec((1, 1, d, h), lambda i, j: (j, 0, 0, 0)),
                  pl.BlockSpec((1, 1, d, h), lambda i, j: (j + nj, 0, 0, 0)),
                  pl.BlockSpec((2, 1, h // 2, d), lambda i, j: (j, 0, 0, 0))] + [ANY] * len(deps),
        out_specs=[row, row, act, act, row],
        out_shape=[jax.ShapeDtypeStruct((t, d), f32), jax.ShapeDtypeStruct((t, d), bf16),
                   jax.ShapeDtypeStruct((nj, t, h), bf16), jax.ShapeDtypeStruct((nj, t, h), bf16),
                   jax.ShapeDtypeStruct((t, d), f32)],
        scratch_shapes=[pltpu.VMEM((tm, d), f32)],
        compiler_params=_params(("arbitrary", "arbitrary")),
    )(x, pre_g, post_g, wgu, wgu, wd, *deps)


def _ffn_bwd(x, dy, f, pre_g, post_g, g, u, wgu, wd, l, deps=()):
    t, d = x.shape
    nj, _, h = g.shape
    tm = _tile(t, 512)

    def body(x_ref, dy_ref, f_ref, pg_ref, qg_ref, g_ref, u_ref, wg_ref, wu_ref, wd_ref,
             dx_ref, dfb_ref, a_ref, dg_ref, du_ref, dpg_ref, dqg_ref, dh_ref):
        i, j = pl.program_id(0), pl.program_id(1)

        @pl.when((i == 0) & (j == 0))
        def _():
            dpg_ref[...] = jnp.zeros_like(dpg_ref)
            dqg_ref[...] = jnp.zeros_like(dqg_ref)

        @pl.when(j == 0)
        def _():
            df, dq = _rms_bwd(f_ref[...], qg_ref[...], 0.5 * dy_ref[...])
            dfb_ref[...] = df.astype(bf16)
            dqg_ref[...] += dq

        da = _nt(dfb_ref[...], _wd_rows(wd_ref))
        gv = g_ref[0].astype(f32)
        uv = u_ref[0].astype(f32)
        s = _sigmoid(gv)
        sg = gv * s
        a_ref[0] = (sg * uv).astype(bf16)
        dg = (da * uv * _silu_grad(gv, s)).astype(bf16)
        du = (da * sg).astype(bf16)
        dg_ref[0] = dg
        du_ref[0] = du
        part = _nt(dg, wg_ref[0, 0]) + _nt(du, wu_ref[0, 0])

        @pl.when(j == 0)
        def _():
            dh_ref[...] = part

        @pl.when(j > 0)
        def _():
            dh_ref[...] += part

        @pl.when(j == nj - 1)
        def _():
            dxn, dp = _rms_bwd(x_ref[...], pg_ref[...], dh_ref[...])
            dx_ref[...] = dy_ref[...] + dxn
            dpg_ref[...] += dp

    row = pl.BlockSpec((tm, d), lambda i, j: (i, 0))
    vec = pl.BlockSpec((1, d), lambda i, j: (0, 0))
    act = pl.BlockSpec((1, tm, h), lambda i, j: (j, i, 0))
    act_shape = jax.ShapeDtypeStruct((nj, t, h), bf16)
    return pl.pallas_call(
        _after(body, 10, deps), name="ffn_bwd", grid=(t // tm, nj),
        in_specs=[row, row, row, _lspec(pre_g, l), _lspec(post_g, l), act, act,
                  pl.BlockSpec((1, 1, d, h), lambda i, j: (j, 0, 0, 0)),
                  pl.BlockSpec((1, 1, d, h), lambda i, j: (j + nj, 0, 0, 0)),
                  pl.BlockSpec((2, 1, h // 2, d), lambda i, j: (j, 0, 0, 0))] + [ANY] * len(deps),
        out_specs=[row, row, act, act, act, vec, vec],
        out_shape=[jax.ShapeDtypeStruct((t, d), f32), jax.ShapeDtypeStruct((t, d), bf16),
                   act_shape, act_shape, act_shape,
                   jax.ShapeDtypeStruct((1, d), f32), jax.ShapeDtypeStruct((1, d), f32)],
        scratch_shapes=[pltpu.VMEM((tm, d), f32)],
        compiler_params=_params(("arbitrary", "arbitrary")),
    )(x, dy, f, pre_g, post_g, g, u, wgu, wgu, wd, *deps)


def _wgrad_cols(x, dy, buf, l, slot0):
    (t, k), (nj, _, n) = x.shape, dy.shape

    def body(x_ref, dy_ref, buf_ref, o_ref):
        o_ref[0, 0] = _tn(dy_ref[0], x_ref[...]).astype(bf16)

    return pl.pallas_call(
        body, name="wgrad_cols", grid=(nj,),
        in_specs=[pl.BlockSpec((t, k), lambda b: (0, 0)), pl.BlockSpec((1, t, n), lambda b: (b, 0, 0)), ANY],
        out_specs=pl.BlockSpec((1, 1, n, k), lambda b: (b + slot0, l, 0, 0)),
        out_shape=jax.ShapeDtypeStruct(buf.shape, bf16), input_output_aliases={2: 0},
        compiler_params=_params(("arbitrary",)),
    )(x, dy, buf)


def _wgrad_rows(x, dy, buf, l):
    (nj, t, k), (_, n) = x.shape, dy.shape

    def body(x_ref, dy_ref, buf_ref, o_ref):
        o_ref[:, 0] = _tn(x_ref[0], dy_ref[...]).astype(bf16).reshape(2, k // 2, n)

    return pl.pallas_call(
        body, name="wgrad_rows", grid=(nj,),
        in_specs=[pl.BlockSpec((1, t, k), lambda b: (b, 0, 0)), pl.BlockSpec((t, n), lambda b: (0, 0)), ANY],
        out_specs=pl.BlockSpec((2, 1, k // 2, n), lambda b: (b, l, 0, 0)),
        out_shape=jax.ShapeDtypeStruct(buf.shape, bf16), input_output_aliases={2: 0},
        compiler_params=_params(("arbitrary",)),
    )(x, dy, buf)


def _wgrad_kblocks(x, dys, buf, l):
    t, k = x.shape
    kb = k // N_DEV
    widths = [dy.shape[1] for dy in dys]
    n = sum(widths)
    nd = len(dys)

    def body(x_ref, *refs):
        dy_hbm, o_ref, dy_vmem = refs[:nd], refs[nd + 1], refs[nd + 2:]

        @pl.when(pl.program_id(0) == 0)
        def _():
            for src, dst in zip(dy_hbm, dy_vmem):
                pltpu.sync_copy(src, dst)

        off = 0
        for dst, w in zip(dy_vmem, widths):
            o_ref[0, 0, :, off:off + w] = _tn(x_ref[...], dst[...]).astype(bf16)
            off += w

    return pl.pallas_call(
        body, name="wgrad_kblocks", grid=(N_DEV,),
        in_specs=[pl.BlockSpec((t, kb), lambda s: (0, s))] + [ANY] * (nd + 1),
        out_specs=pl.BlockSpec((1, 1, kb, n), lambda s: (s, l, 0, 0)),
        out_shape=jax.ShapeDtypeStruct(buf.shape, bf16), input_output_aliases={nd + 1: 0},
        scratch_shapes=[pltpu.VMEM((t, w), bf16) for w in widths],
        compiler_params=_params(("arbitrary",)),
    )(x, *dys, buf)


def _gathered_rows(w_ref, lo, hi):
    return w_ref[:, 0, :, lo:hi].reshape(N_DEV * w_ref.shape[2], hi - lo)


def _gathered_spec(w):
    return pl.BlockSpec((N_DEV, 1) + w.shape[2:], lambda i: (0, 0, 0, 0))


def _mix_in_fwd(x, pre_g, w_in, l):
    t, d = x.shape
    tm = _tile(t, 512)

    def body(x_ref, g_ref, w_ref, hb_ref, pa_ref, pb_ref, pc_ref):
        hb = _rms_fwd(x_ref[...], g_ref[...]).astype(bf16)
        hb_ref[...] = hb
        pa_ref[...] = _nn(hb, _gathered_rows(w_ref, 0, PA_W))
        pb_ref[...] = _nn(hb, _gathered_rows(w_ref, PA_W, PA_W + PB_W))
        pc_ref[...] = _nn(hb, _gathered_rows(w_ref, PA_W + PB_W, IN_PAD))

    def row(w):
        return pl.BlockSpec((tm, w), lambda i: (i, 0))

    return pl.pallas_call(
        body, name="mix_in_fwd", grid=(t // tm,),
        in_specs=[row(d), _lspec(pre_g, l), _gathered_spec(w_in)],
        out_specs=[row(d), row(PA_W), row(PB_W), row(PC_W)],
        out_shape=[jax.ShapeDtypeStruct((t, d), bf16), jax.ShapeDtypeStruct((t, PA_W), f32),
                   jax.ShapeDtypeStruct((t, PB_W), f32), jax.ShapeDtypeStruct((t, PC_W), f32)],
        compiler_params=_params(("arbitrary",)),
    )(x, pre_g, w_in)


def _mix_in_bwd(x, dy, pre_g, dpa, dpb, dpc, w_in, l):
    t, d = x.shape
    tm = _tile(t, 512)

    def body(x_ref, dy_ref, g_ref, dpa_ref, dpb_ref, dpc_ref, w_ref, dx_ref, dg_ref):
        @pl.when(pl.program_id(0) == 0)
        def _():
            dg_ref[...] = jnp.zeros_like(dg_ref)

        dh = (_nt(dpa_ref[...], _gathered_rows(w_ref, 0, PA_W))
              + _nt(dpb_ref[...], _gathered_rows(w_ref, PA_W, PA_W + PB_W))
              + _nt(dpc_ref[...], _gathered_rows(w_ref, PA_W + PB_W, IN_PAD)))
        dxn, dg = _rms_bwd(x_ref[...], g_ref[...], dh)
        dx_ref[...] = dy_ref[...] + dxn
        dg_ref[...] += dg

    def row(w):
        return pl.BlockSpec((tm, w), lambda i: (i, 0))

    vec = pl.BlockSpec((1, d), lambda i: (0, 0))
    return pl.pallas_call(
        body, name="mix_in_bwd", grid=(t // tm,),
        in_specs=[row(d), row(d), _lspec(pre_g, l), row(PA_W), row(PB_W), row(PC_W), _gathered_spec(w_in)],
        out_specs=[row(d), vec],
        out_shape=[jax.ShapeDtypeStruct((t, d), f32), jax.ShapeDtypeStruct((1, d), f32)],
        compiler_params=_params(("arbitrary",)),
    )(x, dy, pre_g, dpa, dpb, dpc, w_in)


def _mix_out_fwd(x, ya, yb, yc, post_g, w_out, l):
    t, d = x.shape
    tm = _tile(t, 512)

    def body(x_ref, ya_ref, yb_ref, yc_ref, g_ref, w_ref, y_ref, cat_ref, m_ref):
        cat_ref[:, 0:A_W] = ya_ref[...].astype(bf16)
        cat_ref[:, A_W:A_W + B_W] = yb_ref[...].astype(bf16)
        cat_ref[:, A_W + B_W:d] = yc_ref[...].astype(bf16)
        m = _nn(cat_ref[...], _gathered_rows(w_ref, 0, d))
        m_ref[...] = m
        y_ref[...] = x_ref[...] + _rms_fwd(m, g_ref[...])

    def row(w):
        return pl.BlockSpec((tm, w), lambda i: (i, 0))

    return pl.pallas_call(
        body, name="mix_out_fwd", grid=(t // tm,),
        in_specs=[row(d), row(A_W), row(B_W), row(C_W), _lspec(post_g, l), _gathered_spec(w_out)],
        out_specs=[row(d), row(d), row(d)],
        out_shape=[jax.ShapeDtypeStruct((t, d), f32), jax.ShapeDtypeStruct((t, d), bf16), jax.ShapeDtypeStruct((t, d), f32)],
        compiler_params=_params(("arbitrary",)),
    )(x, ya, yb, yc, post_g, w_out)


def _mix_out_bwd(dy, m, post_g, w_out, l, deps=()):
    t, d = m.shape
    tm = _tile(t, 512)

    def body(dy_ref, m_ref, g_ref, w_ref, dm_ref, dya_ref, dyb_ref, dyc_ref, dg_ref):
        @pl.when(pl.program_id(0) == 0)
        def _():
            dg_ref[...] = jnp.zeros_like(dg_ref)

        dm, dg = _rms_bwd(m_ref[...], g_ref[...], dy_ref[...])
        dmb = dm.astype(bf16)
        dm_ref[...] = dmb
        dg_ref[...] += dg
        dcat = _nt(dmb, _gathered_rows(w_ref, 0, d))
        dya_ref[...] = dcat[:, 0:A_W]
        dyb_ref[...] = dcat[:, A_W:A_W + B_W]
        dyc_ref[...] = dcat[:, A_W + B_W:d]

    def row(w):
        return pl.BlockSpec((tm, w), lambda i: (i, 0))

    vec = pl.BlockSpec((1, d), lambda i: (0, 0))
    return pl.pallas_call(
        _after(body, 4, deps), name="mix_out_bwd", grid=(t // tm,),
        in_specs=[row(d), row(d), _lspec(post_g, l), _gathered_spec(w_out)] + [ANY] * len(deps),
        out_specs=[row(d), row(A_W), row(B_W), row(C_W), vec],
        out_shape=[jax.ShapeDtypeStruct((t, d), bf16), jax.ShapeDtypeStruct((t, A_W), f32),
                   jax.ShapeDtypeStruct((t, B_W), f32), jax.ShapeDtypeStruct((t, C_W), f32),
                   jax.ShapeDtypeStruct((1, d), f32)],
        compiler_params=_params(("arbitrary",)),
    )(dy, m, post_g, w_out, *deps)


def _conv_fwd(buf_ref, halo, x, w, b, n):
    buf_ref[0:8, :] = halo
    buf_ref[8:8 + n, :] = x
    out = b + w[3:4, :] * x
    for k in range(3):
        out = out + w[k:k + 1, :] * buf_ref[pl.ds(5 + k, n), :]
    return out


def _conv_bwd(buf_ref, dbuf_ref, dout, dnext, w, n):
    dbuf_ref[0:n, :] = dout
    dbuf_ref[n:n + 8, :] = dnext
    dx = w[3:4, :] * dout
    dws = []
    for k in range(3):
        dx = dx + w[k:k + 1, :] * dbuf_ref[pl.ds(3 - k, n), :]
        dws.append(jnp.sum(dout * buf_ref[pl.ds(5 + k, n), :], axis=0, keepdims=True))
    dws.append(jnp.sum(dout * buf_ref[pl.ds(8, n), :], axis=0, keepdims=True))
    return dx, jnp.concatenate(dws, axis=0), jnp.sum(dout, axis=0, keepdims=True)


def _lru_gates(rec, wr, wi, br, bi, lam):
    rb = rec.astype(bf16)
    r = _sigmoid(_nn(rb, wr) + br)
    ig = _sigmoid(_nn(rb, wi) + bi)
    sp = _softplus(-lam)
    la = -LRU_C * r * sp
    a = jnp.exp(la)
    mult = jnp.sqrt(_one_minus_exp(2.0 * la))
    return rb, r, ig, sp, a, mult


def _lru_fwd(pa, conv_w, conv_b, wr, wi, br, bi, lam, l):
    t = pa.shape[0]
    tc = _tile(t, 512)

    def body(pa_ref, halo_ref, cw_ref, cb_ref, wr_ref, wi_ref, br_ref, bi_ref, lam_ref,
             ya_ref, h_ref, buf_ref, a_ref, u_ref, carry_ref):
        i = pl.program_id(0)

        @pl.when(i == 0)
        def _():
            carry_ref[...] = jnp.zeros_like(carry_ref)

        halo = jnp.where(i > 0, halo_ref[:, A_W:PA_W], 0.0)
        rec = _conv_fwd(buf_ref, halo, pa_ref[:, A_W:PA_W], cw_ref[...], cb_ref[...], tc)
        _, _, ig, _, a, mult = _lru_gates(rec, wr_ref[...], wi_ref[...], br_ref[...], bi_ref[...], lam_ref[...])
        a_ref[...] = a
        u_ref[...] = mult * (ig * rec)

        def step(s, h):
            h = a_ref[pl.ds(s, 1), :] * h + u_ref[pl.ds(s, 1), :]
            h_ref[pl.ds(s, 1), :] = h
            return h

        carry_ref[...] = lax.fori_loop(0, tc, step, carry_ref[...], unroll=8)
        ya_ref[...] = h_ref[...] * _gelu(pa_ref[:, 0:A_W])

    vec = pl.BlockSpec((1, A_W), lambda i: (0, 0))
    mat = pl.BlockSpec((A_W, A_W), lambda i: (0, 0))
    row = pl.BlockSpec((tc, A_W), lambda i: (i, 0))
    return pl.pallas_call(
        body, name="lru_fwd", grid=(t // tc,),
        in_specs=[pl.BlockSpec((tc, PA_W), lambda i: (i, 0)),
                  pl.BlockSpec((8, PA_W), lambda i: (jnp.maximum(i * (tc // 8) - 1, 0), 0)),
                  *[_lspec(a, l) for a in (conv_w, conv_b, wr, wi, br, bi, lam)]],
        out_specs=[row, row],
        out_shape=[jax.ShapeDtypeStruct((t, A_W), f32), jax.ShapeDtypeStruct((t, A_W), f32)],
        scratch_shapes=[pltpu.VMEM((8 + tc, A_W), f32), pltpu.VMEM((tc, A_W), f32), pltpu.VMEM((tc, A_W), f32),
                        pltpu.VMEM((1, A_W), f32)],
        compiler_params=_params(("arbitrary",)),
    )(pa, pa, conv_w, conv_b, wr, wi, br, bi, lam)


def _lru_bwd(pa, h, dya, conv_w, conv_b, wr, wi, br, bi, lam, l):
    t = pa.shape[0]
    tc = _tile(t, 512)
    nc = t // tc

    def body(pa_ref, halo_ref, h_ref, hhalo_ref, dya_ref, cw_ref, cb_ref, wr_ref, wi_ref, br_ref, bi_ref, lam_ref,
             dpa_ref, dcw_ref, dcb_ref, dwr_ref, dwi_ref, dbr_ref, dbi_ref, dlam_ref,
             buf_ref, dbuf_ref, hbuf_ref, a_ref, g_ref, dh_ref, carry_ref, dnext_ref):
        i = pl.program_id(0)
        c = nc - 1 - i

        @pl.when(i == 0)
        def _():
            carry_ref[...] = jnp.zeros_like(carry_ref)
            dnext_ref[...] = jnp.zeros_like(dnext_ref)
            for ref in (dcw_ref, dcb_ref, dwr_ref, dwi_ref, dbr_ref, dbi_ref, dlam_ref):
                ref[...] = jnp.zeros_like(ref)

        halo = jnp.where(c > 0, halo_ref[:, A_W:PA_W], 0.0)
        cw = cw_ref[...]
        rec = _conv_fwd(buf_ref, halo, pa_ref[:, A_W:PA_W], cw, cb_ref[...], tc)
        lam = lam_ref[...]
        rb, r, ig, sp, a, mult = _lru_gates(rec, wr_ref[...], wi_ref[...], br_ref[...], bi_ref[...], lam)
        hbuf_ref[0:8, :] = jnp.where(c > 0, hhalo_ref[...], 0.0)
        hbuf_ref[8:8 + tc, :] = h_ref[...]
        h_prev = hbuf_ref[pl.ds(7, tc), :]
        gate = pa_ref[:, 0:A_W]
        dya = dya_ref[...]
        dpa_ref[:, 0:A_W] = (dya * h_ref[...] * _gelu_grad(gate)).astype(bf16)
        a_ref[...] = a
        g_ref[...] = dya * _gelu(gate)

        def step(s, carry):
            row = tc - 1 - s
            dh = g_ref[pl.ds(row, 1), :] + carry
            dh_ref[pl.ds(row, 1), :] = dh
            return a_ref[pl.ds(row, 1), :] * dh

        carry_ref[...] = lax.fori_loop(0, tc, step, carry_ref[...], unroll=8)
        dh = dh_ref[...]
        da = dh * h_prev
        dmult = dh * ig * rec
        dig = dh * mult * rec
        drec = dh * mult * ig
        dla = da * a - dmult * (a * a) / mult
        dr = dla * (-LRU_C * sp)
        dsp = jnp.sum(dla * (-LRU_C * r), axis=0, keepdims=True)
        dlam_ref[...] += dsp * (-_sigmoid(-lam))
        dpr = (dr * r * (1.0 - r))
        dpi = (dig * ig * (1.0 - ig))
        dprb, dpib = dpr.astype(bf16), dpi.astype(bf16)
        drec = drec + _nt(dprb, wr_ref[...]) + _nt(dpib, wi_ref[...])
        dwr_ref[...] += _tn(rb, dprb)
        dwi_ref[...] += _tn(rb, dpib)
        dbr_ref[...] += jnp.sum(dpr, axis=0, keepdims=True)
        dbi_ref[...] += jnp.sum(dpi, axis=0, keepdims=True)
        dx, dw, db = _conv_bwd(buf_ref, dbuf_ref, drec, dnext_ref[...], cw, tc)
        dnext_ref[...] = drec[0:8, :]
        dcw_ref[...] += dw
        dcb_ref[...] += db
        dpa_ref[:, A_W:PA_W] = dx.astype(bf16)

    vec = pl.BlockSpec((1, A_W), lambda i: (0, 0))
    mat = pl.BlockSpec((A_W, A_W), lambda i: (0, 0))
    cwspec = pl.BlockSpec((4, A_W), lambda i: (0, 0))

    def rev(w):
        return pl.BlockSpec((tc, w), lambda i: (nc - 1 - i, 0))

    def halo(w):
        return pl.BlockSpec((8, w), lambda i: (jnp.maximum((nc - 1 - i) * (tc // 8) - 1, 0), 0))

    chunk = pltpu.VMEM((tc, A_W), f32)
    return pl.pallas_call(
        body, name="lru_bwd", grid=(nc,),
        in_specs=[rev(PA_W), halo(PA_W), rev(A_W), halo(A_W), rev(A_W),
                  *[_lspec(a, l) for a in (conv_w, conv_b, wr, wi, br, bi, lam)]],
        out_specs=[rev(PA_W), cwspec, vec, mat, mat, vec, vec, vec],
        out_shape=[jax.ShapeDtypeStruct((t, PA_W), bf16), jax.ShapeDtypeStruct((4, A_W), f32),
                   jax.ShapeDtypeStruct((1, A_W), f32), jax.ShapeDtypeStruct((A_W, A_W), f32),
                   jax.ShapeDtypeStruct((A_W, A_W), f32), jax.ShapeDtypeStruct((1, A_W), f32),
                   jax.ShapeDtypeStruct((1, A_W), f32), jax.ShapeDtypeStruct((1, A_W), f32)],
        scratch_shapes=[pltpu.VMEM((8 + tc, A_W), f32), pltpu.VMEM((tc + 8, A_W), f32), pltpu.VMEM((8 + tc, A_W), f32),
                        chunk, chunk, chunk, pltpu.VMEM((1, A_W), f32), pltpu.VMEM((8, A_W), f32)],
        compiler_params=_params(("arbitrary",)),
    )(pa, pa, h, h, dya, conv_w, conv_b, wr, wi, br, bi, lam)


def _sgu_norm(v, g, b):
    mu = jnp.mean(v, axis=-1, keepdims=True)
    vc = v - mu
    rstd = lax.rsqrt(jnp.mean(vc * vc, axis=-1, keepdims=True) + NORM_EPS)
    vh = vc * rstd
    return vh, rstd, vh * g + b


def _sgu_mix(w_ref, vb, bias):
    grp = lax.broadcasted_iota(jnp.int32, (CHUNK, C_W), 1) // HEAD
    out = bias
    for gi in range(C_W // HEAD):
        out = out + jnp.where(grp == gi, _nn(w_ref[gi], vb), 0.0)
    return out


def _sgu_fwd(pc, ln_g, ln_b, wm, bias, l):
    t = pc.shape[0]
    tm = _tile(t, 512)

    def body(pc_ref, g_ref, b_ref, w_ref, bias_ref, yc_ref):
        for ci in range(tm // CHUNK):
            rows = pl.ds(ci * CHUNK, CHUNK)
            ge = _gelu(pc_ref[rows, :])
            _, _, vn = _sgu_norm(ge[:, C_W:PC_W], g_ref[...], b_ref[...])
            yc_ref[rows, :] = ge[:, 0:C_W] * _sgu_mix(w_ref, vn.astype(bf16), bias_ref[...])

    vec = pl.BlockSpec((1, C_W), lambda i: (0, 0))
    return pl.pallas_call(
        body, name="sgu_fwd", grid=(t // tm,),
        in_specs=[pl.BlockSpec((tm, PC_W), lambda i: (i, 0)), *[_lspec(a, l) for a in (ln_g, ln_b, wm, bias)]],
        out_specs=pl.BlockSpec((tm, C_W), lambda i: (i, 0)),
        out_shape=jax.ShapeDtypeStruct((t, C_W), f32),
        compiler_params=_params(("arbitrary",)),
    )(pc, ln_g, ln_b, wm, bias)


def _sgu_bwd(pc, dyc, ln_g, ln_b, wm, wmt, bias, l):
    t = pc.shape[0]
    tm = _tile(t, 512)

    def body(pc_ref, dyc_ref, g_ref, b_ref, w_ref, wt_ref, bias_ref, dpc_ref, dw_ref, dbias_ref, dg_ref, db_ref):
        @pl.when(pl.program_id(0) == 0)
        def _():
            for ref in (dw_ref, dbias_ref, dg_ref, db_ref):
                ref[...] = jnp.zeros_like(ref)

        grp = lax.broadcasted_iota(jnp.int32, (CHUNK, C_W), 1) // HEAD
        for ci in range(tm // CHUNK):
            rows = pl.ds(ci * CHUNK, CHUNK)
            x = pc_ref[rows, :]
            ge = _gelu(x)
            gv = g_ref[...]
            vh, rstd, vn = _sgu_norm(ge[:, C_W:PC_W], gv, b_ref[...])
            vb = vn.astype(bf16)
            mixed = _sgu_mix(w_ref, vb, bias_ref[...])
            dyc = dyc_ref[rows, :]
            du = dyc * mixed
            dmix = dyc * ge[:, 0:C_W]
            dmb = dmix.astype(bf16)
            dvn = jnp.zeros((CHUNK, C_W), f32)
            for gi in range(C_W // HEAD):
                dvn = dvn + jnp.where(grp == gi, _nn(wt_ref[gi], dmb), 0.0)
                dw_ref[gi] += _nt(jnp.where(grp == gi, dmix, 0.0).astype(bf16), vb)
            dbias_ref[...] += dmix
            dg_ref[...] += jnp.sum(dvn * vh, axis=0, keepdims=True)
            db_ref[...] += jnp.sum(dvn, axis=0, keepdims=True)
            dvh = dvn * gv
            dv = rstd * (dvh - jnp.mean(dvh, axis=-1, keepdims=True) - vh * jnp.mean(dvh * vh, axis=-1, keepdims=True))
            gg = _gelu_grad(x)
            dpc_ref[rows, 0:C_W] = (du * gg[:, 0:C_W]).astype(bf16)
            dpc_ref[rows, C_W:PC_W] = (dv * gg[:, C_W:PC_W]).astype(bf16)

    vec = pl.BlockSpec((1, C_W), lambda i: (0, 0))
    wspec = pl.BlockSpec((4, CHUNK, CHUNK), lambda i: (0, 0, 0))
    bspec = pl.BlockSpec((CHUNK, C_W), lambda i: (0, 0))
    return pl.pallas_call(
        body, name="sgu_bwd", grid=(t // tm,),
        in_specs=[pl.BlockSpec((tm, PC_W), lambda i: (i, 0)), pl.BlockSpec((tm, C_W), lambda i: (i, 0)),
                  *[_lspec(a, l) for a in (ln_g, ln_b, wm, wmt, bias)]],
        out_specs=[pl.BlockSpec((tm, PC_W), lambda i: (i, 0)), wspec, bspec, vec, vec],
        out_shape=[jax.ShapeDtypeStruct((t, PC_W), bf16), jax.ShapeDtypeStruct((4, CHUNK, CHUNK), f32),
                   jax.ShapeDtypeStruct((CHUNK, C_W), f32), jax.ShapeDtypeStruct((1, C_W), f32),
                   jax.ShapeDtypeStruct((1, C_W), f32)],
        compiler_params=_params(("arbitrary",)),
    )(pc, dyc, ln_g, ln_b, wm, wmt, bias)


N_PAIR = B_W // 128
HEADS_PER_GROUP = 3


def _pair_groups(p):
    return (2 * p) // HEADS_PER_GROUP, (2 * p + 1) // HEADS_PER_GROUP


def _ssd_chunk(pb_ref, halo, buf_ref, cw, cb, dtb, alog):
    z = pb_ref[:, 0:B_W]
    pre = _conv_fwd(buf_ref, halo, pb_ref[:, B_W:B_W + XBC_W], cw, cb, CHUNK)
    sg = _sigmoid(pre)
    xbc = pre * sg
    xs = xbc[:, 0:B_W]
    bm = [xbc[:, B_W + k * B_STATE:B_W + (k + 1) * B_STATE] for k in range(2)]
    cm = [xbc[:, B_W + (2 + k) * B_STATE:B_W + (3 + k) * B_STATE] for k in range(2)]
    dtin = pb_ref[:, B_W + XBC_W:PB_W] + dtb
    dt = _softplus(dtin)
    a = -jnp.exp(alog)
    cs = _cumsum_rows(dt * a)
    return dict(z=z, pre=pre, sg=sg, xs=xs, bm=bm, cm=cm, dtin=dtin, dt=dt, a=a, cs=cs,
                ecs=jnp.exp(cs), ds=jnp.exp(cs[CHUNK - 1:CHUNK, :] - cs), xdt=xs * dt,
                bmb=[v.astype(bf16) for v in bm], cmb=[v.astype(bf16) for v in cm])


def _ssd_decay(cs_pair, half):
    cst = cs_pair.T
    lane0 = HEAD * half
    csc = jnp.broadcast_to(cs_pair[:, lane0:lane0 + 1], (CHUNK, CHUNK))
    csr = cst[lane0:lane0 + 1, :]
    tri = lax.broadcasted_iota(jnp.int32, (CHUNK, CHUNK), 0) >= lax.broadcasted_iota(jnp.int32, (CHUNK, CHUNK), 1)
    return jnp.exp(jnp.where(tri, csc - csr, NEG_BIG)), cst


def _ssd_fwd(pb, conv_w, conv_b, dtb, alog, dskip, norm_g, l):
    t = pb.shape[0]
    nc = t // CHUNK

    def body(pb_ref, halo_ref, cw_ref, cb_ref, dtb_ref, alog_ref, d_ref, ng_ref, yb_ref, yp_ref, sp_ref, buf_ref, s_ref):
        i = pl.program_id(0)

        @pl.when(i == 0)
        def _():
            s_ref[...] = jnp.zeros_like(s_ref)

        halo = jnp.where(i > 0, halo_ref[:, B_W:B_W + XBC_W], 0.0)
        q = _ssd_chunk(pb_ref, halo, buf_ref, cw_ref[...], cb_ref[...], dtb_ref[...], alog_ref[...])
        sp_ref[0] = s_ref[...]
        lane = lax.broadcasted_iota(jnp.int32, (CHUNK, 128), 1)
        rowi = lax.broadcasted_iota(jnp.int32, (128, B_STATE), 0)
        cb_mat = [_nt(q["cmb"][k], q["bmb"][k]) for k in range(2)]
        xd = q["xdt"] * q["ds"]
        for p in range(N_PAIR):
            cols = slice(128 * p, 128 * (p + 1))
            g_lo, g_hi = _pair_groups(p)
            cs_p, xdt_p = q["cs"][:, cols], q["xdt"][:, cols]
            s_p = s_ref[cols, :]
            s_pb = s_p.astype(bf16)
            y_p = jnp.zeros((CHUNK, 128), f32)
            for half, grp in ((0, g_lo), (1, g_hi)):
                lm, cst = _ssd_decay(cs_p, half)
                mb = (cb_mat[grp] * lm).astype(bf16)
                sel = (lane < HEAD) if half == 0 else (lane >= HEAD)
                y_p = y_p + _nn(mb, jnp.where(sel, xdt_p, 0.0).astype(bf16))
            off_lo = _nt(q["cmb"][g_lo], s_pb)
            off = off_lo if g_lo == g_hi else jnp.where(lane < HEAD, off_lo, _nt(q["cmb"][g_hi], s_pb))
            y_p = y_p + off * q["ecs"][:, cols] + q["xs"][:, cols] * d_ref[:, cols]
            yp_ref[:, cols] = y_p
            xd_pb = xd[:, cols].astype(bf16)
            upd_lo = _tn(xd_pb, q["bmb"][g_lo])
            upd = upd_lo if g_lo == g_hi else jnp.where(rowi < HEAD, upd_lo, _tn(xd_pb, q["bmb"][g_hi]))
            cd = jnp.exp(jnp.broadcast_to(cst[:, CHUNK - 1:CHUNK], (128, B_STATE)))
            s_ref[cols, :] = cd * s_p + upd
        z = q["z"]
        yg = yp_ref[...] * (z * _sigmoid(z))
        yb_ref[...] = _rms_fwd(yg, ng_ref[...])

    vec = pl.BlockSpec((1, B_W), lambda i: (0, 0))
    row = pl.BlockSpec((CHUNK, B_W), lambda i: (i, 0))
    return pl.pallas_call(
        body, name="ssd_fwd", grid=(nc,),
        in_specs=[pl.BlockSpec((CHUNK, PB_W), lambda i: (i, 0)),
                  pl.BlockSpec((8, PB_W), lambda i: (jnp.maximum(i * (CHUNK // 8) - 1, 0), 0)),
                  *[_lspec(a, l) for a in (conv_w, conv_b, dtb, alog, dskip, norm_g)]],
        out_specs=[row, row, pl.BlockSpec((1, B_W, B_STATE), lambda i: (i, 0, 0))],
        out_shape=[jax.ShapeDtypeStruct((t, B_W), f32), jax.ShapeDtypeStruct((t, B_W), f32),
                   jax.ShapeDtypeStruct((nc, B_W, B_STATE), f32)],
        scratch_shapes=[pltpu.VMEM((8 + CHUNK, XBC_W), f32), pltpu.VMEM((B_W, B_STATE), f32)],
        compiler_params=_params(("arbitrary",)),
    )(pb, pb, conv_w, conv_b, dtb, alog, dskip, norm_g)


def _ssd_bwd(pb, yp, sprev, dyb, conv_w, conv_b, dtb, alog, dskip, norm_g, l):
    t = pb.shape[0]
    nc = t // CHUNK

    def body(pb_ref, halo_ref, yp_ref, sp_ref, dyb_ref, cw_ref, cb_ref, dtb_ref, alog_ref, d_ref, ng_ref,
             dpb_ref, dcw_ref, dcb_ref, ddtb_ref, dalog_ref, dd_ref, dng_ref,
             buf_ref, dbuf_ref, ds_ref, dnext_ref, dxbc_ref, dcs_ref, dxdt_ref):
        i = pl.program_id(0)
        c = nc - 1 - i

        @pl.when(i == 0)
        def _():
            ds_ref[...] = jnp.zeros_like(ds_ref)
            dnext_ref[...] = jnp.zeros_like(dnext_ref)
            for ref in (dcw_ref, dcb_ref, ddtb_ref, dalog_ref, dd_ref, dng_ref):
                ref[...] = jnp.zeros_like(ref)

        halo = jnp.where(c > 0, halo_ref[:, B_W:B_W + XBC_W], 0.0)
        cw = cw_ref[...]
        q = _ssd_chunk(pb_ref, halo, buf_ref, cw, cb_ref[...], dtb_ref[...], alog_ref[...])
        z, xs, dt, a, ecs, dsd, xdt = q["z"], q["xs"], q["dt"], q["a"], q["ecs"], q["ds"], q["xdt"]
        sz = _sigmoid(z)
        siluz = z * sz
        yp = yp_ref[...]
        dyg, dng = _rms_bwd(yp * siluz, ng_ref[...], dyb_ref[...])
        dng_ref[...] += dng
        dy = dyg * siluz
        dpb_ref[:, 0:B_W] = (dyg * yp * _silu_grad(z, sz)).astype(bf16)
        dd_ref[...] += jnp.sum(dy * xs, axis=0, keepdims=True)
        g1 = dy * ecs
        lane = lax.broadcasted_iota(jnp.int32, (CHUNK, 128), 1)
        rowi = lax.broadcasted_iota(jnp.int32, (128, B_STATE), 0)
        rowc = lax.broadcasted_iota(jnp.int32, (CHUNK, 128), 0)
        cb_mat = [_nt(q["cmb"][k], q["bmb"][k]) for k in range(2)]
        d_cb = [jnp.zeros((CHUNK, CHUNK), f32) for _ in range(2)]
        d_b = [jnp.zeros((CHUNK, B_STATE), f32) for _ in range(2)]
        d_c = [jnp.zeros((CHUNK, B_STATE), f32) for _ in range(2)]
        for p in range(N_PAIR):
            cols = slice(128 * p, 128 * (p + 1))
            g_lo, g_hi = _pair_groups(p)
            lo, hi = lane < HEAD, lane >= HEAD
            cs_p, xdt_p, dy_p, ds_p, g1_p = q["cs"][:, cols], xdt[:, cols], dy[:, cols], dsd[:, cols], g1[:, cols]
            s_p = sp_ref[0, cols, :]
            s_pb = s_p.astype(bf16)
            dsn = ds_ref[cols, :]
            dsnb = dsn.astype(bf16)
            g1b = g1_p.astype(bf16)
            off_lo = _nt(q["cmb"][g_lo], s_pb)
            off = off_lo if g_lo == g_hi else jnp.where(lo, off_lo, _nt(q["cmb"][g_hi], s_pb))
            dcs_p = dy_p * off * ecs[:, cols]
            dsp_lo = _tn(g1b, q["cmb"][g_lo])
            dsp = dsp_lo if g_lo == g_hi else jnp.where(rowi < HEAD, dsp_lo, _tn(g1b, q["cmb"][g_hi]))
            dx_lo = _nt(q["bmb"][g_lo], dsnb)
            dxd = dx_lo if g_lo == g_hi else jnp.where(lo, dx_lo, _nt(q["bmb"][g_hi], dsnb))
            xd_p = xdt_p * ds_p
            if g_lo == g_hi:
                d_c[g_lo] = d_c[g_lo] + _nn(g1b, s_pb)
                d_b[g_lo] = d_b[g_lo] + _nn(xd_p.astype(bf16), dsnb)
            else:
                d_c[g_lo] = d_c[g_lo] + _nn(jnp.where(lo, g1_p, 0.0).astype(bf16), s_pb)
                d_c[g_hi] = d_c[g_hi] + _nn(jnp.where(hi, g1_p, 0.0).astype(bf16), s_pb)
                d_b[g_lo] = d_b[g_lo] + _nn(jnp.where(lo, xd_p, 0.0).astype(bf16), dsnb)
                d_b[g_hi] = d_b[g_hi] + _nn(jnp.where(hi, xd_p, 0.0).astype(bf16), dsnb)
            dxdt_p = dxd * ds_p
            t2 = dxd * xdt_p * ds_p
            dcs_p = dcs_p - t2
            dlast = jnp.sum(t2, axis=0, keepdims=True)
            cst = None
            for half, grp in ((0, g_lo), (1, g_hi)):
                sel = lo if half == 0 else hi
                lm, cst = _ssd_decay(cs_p, half)
                m = cb_mat[grp] * lm
                dyh = jnp.where(sel, dy_p, 0.0).astype(bf16)
                xdh = jnp.where(sel, xdt_p, 0.0).astype(bf16)
                dm = _nt(dyh, xdh)
                pm = dm * m
                col = jnp.sum(pm, axis=1, keepdims=True) - jnp.sum(pm.T, axis=1, keepdims=True)
                dcs_p = dcs_p + jnp.where(lane == HEAD * half, col, 0.0)
                d_cb[grp] = d_cb[grp] + dm * lm
                dxdt_p = dxdt_p + _tn(m.astype(bf16), dyh)
            cdcol = jnp.exp(jnp.broadcast_to(cst[:, CHUNK - 1:CHUNK], (128, B_STATE)))
            ds_ref[cols, :] = cdcol * dsn + dsp
            dcd_row = jnp.sum((dsn * s_p).T, axis=0, keepdims=True)
            dlast = dlast + dcd_row * ecs[CHUNK - 1:CHUNK, cols]
            dcs_ref[:, cols] = dcs_p + jnp.where(rowc == CHUNK - 1, dlast, 0.0)
            dxdt_ref[:, cols] = dxdt_p
        for k in range(2):
            dcbb = d_cb[k].astype(bf16)
            d_c[k] = d_c[k] + _nn(dcbb, q["bmb"][k])
            d_b[k] = d_b[k] + _tn(dcbb, q["cmb"][k])
            dxbc_ref[:, B_W + k * B_STATE:B_W + (k + 1) * B_STATE] = d_b[k]
            dxbc_ref[:, B_W + (2 + k) * B_STATE:B_W + (3 + k) * B_STATE] = d_c[k]
        dxdt = dxdt_ref[...]
        dxbc_ref[:, 0:B_W] = dy * d_ref[...] + dxdt * dt
        dcs = dcs_ref[...]
        dad = jnp.sum(dcs, axis=0, keepdims=True) - _cumsum_rows(dcs) + dcs
        ddt = dxdt * xs + dad * a
        dalog_ref[...] += jnp.sum(dad * dt, axis=0, keepdims=True) * a
        dtraw = ddt * _sigmoid(q["dtin"])
        ddtb_ref[...] += jnp.sum(dtraw, axis=0, keepdims=True)
        dpb_ref[:, B_W + XBC_W:PB_W] = dtraw.astype(bf16)
        dpre = dxbc_ref[...] * _silu_grad(q["pre"], q["sg"])
        dx, dw, db = _conv_bwd(buf_ref, dbuf_ref, dpre, dnext_ref[...], cw, CHUNK)
        dnext_ref[...] = dpre[0:8, :]
        dcw_ref[...] += dw
        dcb_ref[...] += db
        dpb_ref[:, B_W:B_W + XBC_W] = dx.astype(bf16)

    vec = pl.BlockSpec((1, B_W), lambda i: (0, 0))
    cwspec = pl.BlockSpec((4, XBC_W), lambda i: (0, 0))
    cbspec = pl.BlockSpec((1, XBC_W), lambda i: (0, 0))

    def rev(w):
        return pl.BlockSpec((CHUNK, w), lambda i: (nc - 1 - i, 0))

    vshape = jax.ShapeDtypeStruct((1, B_W), f32)
    return pl.pallas_call(
        body, name="ssd_bwd", grid=(nc,),
        in_specs=[rev(PB_W), pl.BlockSpec((8, PB_W), lambda i: (jnp.maximum((nc - 1 - i) * (CHUNK // 8) - 1, 0), 0)),
                  rev(B_W), pl.BlockSpec((1, B_W, B_STATE), lambda i: (nc - 1 - i, 0, 0)), rev(B_W),
                  *[_lspec(a, l) for a in (conv_w, conv_b, dtb, alog, dskip, norm_g)]],
        out_specs=[rev(PB_W), cwspec, cbspec, vec, vec, vec, vec],
        out_shape=[jax.ShapeDtypeStruct((t, PB_W), bf16), jax.ShapeDtypeStruct((4, XBC_W), f32),
                   jax.ShapeDtypeStruct((1, XBC_W), f32), vshape, vshape, vshape, vshape],
        scratch_shapes=[pltpu.VMEM((8 + CHUNK, XBC_W), f32), pltpu.VMEM((CHUNK + 8, XBC_W), f32),
                        pltpu.VMEM((B_W, B_STATE), f32), pltpu.VMEM((8, XBC_W), f32),
                        pltpu.VMEM((CHUNK, XBC_W), f32), pltpu.VMEM((CHUNK, B_W), f32), pltpu.VMEM((CHUNK, B_W), f32)],
        compiler_params=_params(("arbitrary",)),
    )(pb, pb, yp, sprev, dyb, conv_w, conv_b, dtb, alog, dskip, norm_g)


def _loss_fwd(y, target):
    t, d = y.shape
    tm = _tile(t, 512)

    def body(y_ref, t_ref, dy_ref, loss_ref):
        @pl.when(pl.program_id(0) == 0)
        def _():
            loss_ref[...] = jnp.zeros_like(loss_ref)

        e = y_ref[...] - t_ref[...]
        dy_ref[...] = e * (1.0 / d)
        per_tok = jnp.mean(e * e, axis=-1, keepdims=True)
        loss_ref[...] += 0.5 * jnp.sum(per_tok, axis=0, keepdims=True)

    row = pl.BlockSpec((tm, d), lambda i: (i, 0))
    return pl.pallas_call(
        body, name="loss_fwd", grid=(t // tm,), in_specs=[row, row],
        out_specs=[row, pl.BlockSpec((1, 128), lambda i: (0, 0))],
        out_shape=[jax.ShapeDtypeStruct((t, d), f32), jax.ShapeDtypeStruct((1, 128), f32)],
        compiler_params=_params(("arbitrary",)),
    )(y, target)


def _row_tile(r):
    return 512 if r % 512 == 0 else r


def _pair_add(g, r, c_dev):
    _, nl, rows, cols = g.shape
    tr = _row_tile(rows)

    def body(c_ref, g_ref, r_ref, o_ref):
        o_ref[...] = (g_ref[...].astype(f32) + r_ref[...].astype(f32)).astype(bf16)

    blk = (None, None, tr, cols)
    return pl.pallas_call(
        body, name="pair_add",
        grid_spec=pltpu.PrefetchScalarGridSpec(
            num_scalar_prefetch=1, grid=(4, nl, rows // tr),
            in_specs=[pl.BlockSpec(blk, lambda b, l, i, c: (2 * b + c[0], l, i, 0)),
                      pl.BlockSpec(blk, lambda b, l, i, c: (b, l, i, 0))],
            out_specs=pl.BlockSpec(blk, lambda b, l, i, c: (b, l, i, 0))),
        out_shape=jax.ShapeDtypeStruct(r.shape, bf16),
        compiler_params=_params(("arbitrary", "arbitrary", "arbitrary")),
    )(c_dev, g, r)


def _grad_sum(s, q, b_dev):
    _, nl, rows, cols = s.shape
    tr = _row_tile(rows)

    def body(b_ref, s_ref, q0_ref, q1_ref, q2_ref, o_ref):
        o_ref[...] = ((s_ref[...].astype(f32) + q0_ref[...].astype(f32)) + q1_ref[...].astype(f32)) + q2_ref[...].astype(f32)

    blk = (None, None, tr, cols)

    def qspec(k):
        return pl.BlockSpec(blk, lambda l, i, b: (k, l, i, 0))

    return pl.pallas_call(
        body, name="grad_sum",
        grid_spec=pltpu.PrefetchScalarGridSpec(
            num_scalar_prefetch=1, grid=(nl, rows // tr),
            in_specs=[pl.BlockSpec(blk, lambda l, i, b: (b[0], l, i, 0)), qspec(0), qspec(1), qspec(2)],
            out_specs=pl.BlockSpec((None, tr, cols), lambda l, i, b: (l, i, 0))),
        out_shape=jax.ShapeDtypeStruct(s.shape[1:], f32),
        compiler_params=_params(("arbitrary", "arbitrary")),
    )(b_dev, s, q, q, q)


def _sum_devices(parts):
    n, rows, cols = parts.shape
    tr = _row_tile(rows)

    def body(p_ref, o_ref):
        acc = p_ref[0]
        for k in range(1, n):
            acc = acc + p_ref[k]
        o_ref[...] = acc

    return pl.pallas_call(
        body, name="sum_devices", grid=(rows // tr,),
        in_specs=[pl.BlockSpec((n, tr, cols), lambda i: (0, i, 0))],
        out_specs=pl.BlockSpec((tr, cols), lambda i: (i, 0)),
        out_shape=jax.ShapeDtypeStruct((rows, cols), f32),
        compiler_params=_params(("arbitrary",)),
    )(parts)


def _adamw(w, m, v, g):
    nl, rows, cols = w.shape
    tr = _row_tile(rows)

    def body(w_ref, m_ref, v_ref, g_ref, d_ref, nm_ref, nv_ref):
        d_ref[...], nm_ref[...], nv_ref[...] = _adamw_math(w_ref[...], m_ref[...], v_ref[...], g_ref[...])

    blk = pl.BlockSpec((None, tr, cols), lambda l, i: (l, i, 0))
    shape = jax.ShapeDtypeStruct(w.shape, f32)
    return pl.pallas_call(
        body, name="adamw", grid=(nl, rows // tr), in_specs=[blk] * 4, out_specs=[blk] * 3,
        out_shape=[shape] * 3, compiler_params=_params(("arbitrary", "arbitrary")),
    )(w, m, v, g)


def _adamw_math(w, m, v, g):
    nm = ADAM_B1 * m + (1.0 - ADAM_B1) * g
    nv = ADAM_B2 * v + (1.0 - ADAM_B2) * (g * g)
    m_hat = nm / (1.0 - ADAM_B1 ** ADAM_STEP)
    v_hat = nv / (1.0 - ADAM_B2 ** ADAM_STEP)
    return -ADAM_LR * (m_hat / (jnp.sqrt(v_hat) + ADAM_EPS) + ADAM_WD * w), nm, nv


def _adamw_layer(w, m, v, s, q, b_dev, outs, l):
    _, rows, cols = w.shape
    tr = _row_tile(rows)

    def body(b_ref, w_ref, m_ref, v_ref, s_ref, q0_ref, q1_ref, q2_ref, o0, o1, o2, o3, g_ref, d_ref, nm_ref, nv_ref):
        g = ((s_ref[...].astype(f32) + q0_ref[...].astype(f32)) + q1_ref[...].astype(f32)) + q2_ref[...].astype(f32)
        g_ref[...] = g
        d_ref[...], nm_ref[...], nv_ref[...] = _adamw_math(w_ref[...], m_ref[...], v_ref[...], g)

    wspec = pl.BlockSpec((None, tr, cols), lambda i, b: (l, i, 0))
    blk = (None, None, tr, cols)

    def qspec(k):
        return pl.BlockSpec(blk, lambda i, b: (k, 0, i, 0))

    shape = jax.ShapeDtypeStruct(w.shape, f32)
    return pl.pallas_call(
        body, name="adamw_layer",
        grid_spec=pltpu.PrefetchScalarGridSpec(
            num_scalar_prefetch=1, grid=(rows // tr,),
            in_specs=[wspec] * 3 + [pl.BlockSpec(blk, lambda i, b: (b[0], 0, i, 0)), qspec(0), qspec(1), qspec(2)] + [ANY] * 4,
            out_specs=[wspec] * 4),
        out_shape=[shape] * 4, input_output_aliases={8 + k: k for k in range(4)},
        compiler_params=_params(("arbitrary",)),
    )(b_dev, w, m, v, s, q, q, q, *outs)


def _place():
    return lax.axis_index("x"), lax.axis_index("y"), lax.axis_index("c")


def _all_gather(shards):
    n = len(shards)

    def body(*refs):
        src, dst = refs[:n], refs[n:2 * n]
        send_sems, recv_sems, local_sems = refs[2 * n:]
        x, y, c = _place()
        me, sibling = (x, y, c), (x, y, 1 - c)
        chips = [(1 - x, y), (x, 1 - y), (1 - x, 1 - y)]

        def copy(a, k, block, to, from_shard=False):
            px, py, pc = block
            rows = dst[a].at[4 * px + 2 * py + pc]
            return pltpu.make_async_remote_copy(
                src_ref=src[a] if from_shard else rows, dst_ref=rows,
                send_sem=send_sems.at[a, k], recv_sem=recv_sems.at[a, k], device_id=to, device_id_type=MESH)

        mine = [pltpu.make_async_copy(src[a], dst[a].at[4 * x + 2 * y + c], local_sems.at[a]) for a in range(n)]
        for cp in mine:
            cp.start()
        first = []
        for a in range(n):
            first.append(copy(a, 0, me, sibling, True))
            first += [copy(a, 1 + j, me, (*chip, c), True) for j, chip in enumerate(chips)]
        for cp in first:
            cp.start()
        passed = []
        for j, chip in enumerate(chips):
            for a in range(n):
                copy(a, 1 + j, (*chip, c), me).wait_recv()
                fwd = copy(a, 4 + j, (*chip, c), sibling)
                fwd.start()
                passed.append(fwd)
        for a in range(n):
            copy(a, 0, sibling, me).wait_recv()
            for j, chip in enumerate(chips):
                copy(a, 4 + j, (*chip, 1 - c), me).wait_recv()
        for cp in first + passed:
            cp.wait_send()
        for cp in mine:
            cp.wait()

    return pl.pallas_call(
        body, name="all_gather", in_specs=[ANY] * n, out_specs=[ANY] * n,
        out_shape=[jax.ShapeDtypeStruct((N_DEV,) + s.shape, s.dtype) for s in shards],
        scratch_shapes=[pltpu.SemaphoreType.DMA((n, 7)), pltpu.SemaphoreType.DMA((n, 7)), pltpu.SemaphoreType.DMA((n,))],
    )(*shards)


HBM = pl.BlockSpec(memory_space=pltpu.HBM)
SEM = pl.BlockSpec(memory_space=pltpu.SEMAPHORE)
_EFFECT = pltpu.SideEffectType.DATAFLOW_SIDE_EFFECTING


def _split_start(name, srcs, dsts, sem_shape, plan):
    ns, nb = len(srcs), len(srcs) + len(dsts)

    def body(*refs):
        send_sems, recv_sems = refs[nb], refs[nb + 1]
        for cp in plan(refs[:ns], refs[ns:nb], send_sems, recv_sems):
            cp.start()
        refs[-1][...] = jnp.zeros_like(refs[-1])

    bufs = list(srcs) + list(dsts)
    return pl.pallas_call(
        body, name=name,
        out_shape=(pltpu.SemaphoreType.DMA(sem_shape), pltpu.SemaphoreType.DMA(sem_shape),
                   *[pltpu.HBM(a.shape, a.dtype) for a in bufs], jax.ShapeDtypeStruct((8, 128), f32)),
        in_specs=[HBM] * nb, out_specs=(SEM, SEM, *[HBM] * nb, pl.BlockSpec(memory_space=pltpu.VMEM)),
        input_output_aliases={i: 2 + i for i in range(nb)},
        compiler_params=pltpu.CompilerParams(has_side_effects=_EFFECT),
    )(*[pltpu.with_memory_space_constraint(a, pltpu.HBM) for a in bufs])


def _split_wait(name, started, ns, plan, after):
    send_sems, recv_sems = started[0], started[1]
    bufs = list(started[2:-1])
    nb = len(bufs)

    def body(*refs):
        for cp in plan(refs[:ns], refs[ns:nb], refs[nb], refs[nb + 1]):
            cp.wait_send()
            cp.wait_recv()

    return pl.pallas_call(
        body, name=name, out_shape=tuple(pltpu.HBM(a.shape, a.dtype) for a in bufs),
        in_specs=[HBM] * nb + [SEM, SEM, ANY], out_specs=tuple([HBM] * nb),
        input_output_aliases={i: i for i in range(nb)},
        compiler_params=pltpu.CompilerParams(has_side_effects=_EFFECT),
    )(*bufs, send_sems, recv_sems, after)


def _remote(src, dst, send_sem, recv_sem, to):
    return pltpu.make_async_remote_copy(src_ref=src, dst_ref=dst, send_sem=send_sem, recv_sem=recv_sem,
                                        device_id=to, device_id_type=MESH)


def _gather_plan(src, dst, send_sems, recv_sems):
    x, y, c = _place()
    peers = [(x, y, 1 - c), (1 - x, y, c), (x, 1 - y, c), (1 - x, 1 - y, c)]
    return [_remote(src[a], dst[a].at[4 * x + 2 * y + c], send_sems.at[4 * a + k], recv_sems.at[4 * a + k], peer)
            for a in range(len(src)) for k, peer in enumerate(peers)]


def _pair_plan(src, dst, send_sems, recv_sems):
    x, y, c = _place()
    return [_remote(src[a].at[2 * b + (1 - c)], dst[a].at[b], send_sems.at[4 * a + b], recv_sems.at[4 * a + b], (x, y, 1 - c))
            for a in range(len(src)) for b in range(4)]


def _chips_plan(src, dst, send_sems, recv_sems):
    x, y, c = _place()
    chips = [(1 - x, y), (x, 1 - y), (1 - x, 1 - y)]
    return [_remote(src[a].at[2 * px + py], dst[a].at[j], send_sems.at[3 * a + j], recv_sems.at[3 * a + j], (px, py, c))
            for a in range(len(src)) for j, (px, py) in enumerate(chips)]


def _gather_finish(shards, bufs):
    n = len(shards)

    def body(*refs):
        src, dst = refs[:n], refs[2 * n:3 * n]
        send_sems, recv_sems, local_sems = refs[3 * n:]
        x, y, c = _place()
        mine = [pltpu.make_async_copy(src[a], dst[a].at[4 * x + 2 * y + c], local_sems.at[a]) for a in range(n)]
        passed = []
        for a in range(n):
            for j, (px, py) in enumerate([(1 - x, y), (x, 1 - y), (1 - x, 1 - y)]):
                rows = dst[a].at[4 * px + 2 * py + c]
                passed.append(_remote(rows, rows, send_sems.at[a, j], recv_sems.at[a, j], (x, y, 1 - c)))
        for cp in mine + passed:
            cp.start()
        for cp in passed:
            cp.wait_send()
        for a in range(n):
            for j, (px, py) in enumerate([(1 - x, y), (x, 1 - y), (1 - x, 1 - y)]):
                rows = dst[a].at[4 * px + 2 * py + (1 - c)]
                _remote(rows, rows, send_sems.at[a, j], recv_sems.at[a, j], (x, y, 1 - c)).wait_recv()
        for cp in mine:
            cp.wait()

    return pl.pallas_call(
        body, name="gather_finish", in_specs=[ANY] * (2 * n), out_specs=[ANY] * n,
        out_shape=[jax.ShapeDtypeStruct(b.shape, b.dtype) for b in bufs],
        input_output_aliases={n + a: a for a in range(n)},
        scratch_shapes=[pltpu.SemaphoreType.DMA((n, 3)), pltpu.SemaphoreType.DMA((n, 3)), pltpu.SemaphoreType.DMA((n,))],
    )(*shards, *bufs)


BIG = ("ffn1_w_gu", "ffn1_w_down", "mix_w_in", "mix_w_out", "ffn2_w_gu", "ffn2_w_down")
SHARDED_CONV = ("lru_conv_w", "ssd_conv_w")
REPLICATED = ("ffn1_pre_g", "ffn1_post_g", "mix_pre_g", "mix_post_g", "lru_conv_b", "lru_w_r", "lru_b_r", "lru_w_i",
              "lru_b_i", "lru_lambda", "ssd_conv_b", "ssd_dt_bias", "ssd_a_log", "ssd_d", "ssd_norm_g", "sgu_ln_g",
              "sgu_ln_b", "sgu_w_s", "sgu_b_s", "ffn2_pre_g", "ffn2_post_g")
WEIGHTS = ("ffn1_pre_g", "ffn1_post_g", "ffn1_w_gu", "ffn1_w_down", "mix_pre_g", "mix_post_g", "mix_w_in", "mix_w_out",
           "lru_conv_w", "lru_conv_b", "lru_w_r", "lru_b_r", "lru_w_i", "lru_b_i", "lru_lambda", "ssd_conv_w",
           "ssd_conv_b", "ssd_dt_bias", "ssd_a_log", "ssd_d", "ssd_norm_g", "sgu_ln_g", "sgu_ln_b", "sgu_w_s", "sgu_b_s",
           "ffn2_pre_g", "ffn2_post_g", "ffn2_w_gu", "ffn2_w_down")
DT_LO = PA_W + B_W + XBC_W
N_HEADS = B_W // HEAD
PACK_COLS = 1024


def _pack(arrays):
    flat = jnp.concatenate([a.reshape(-1) for a in arrays])
    rows = -(-flat.shape[0] // (8 * PACK_COLS)) * 8
    return jnp.pad(flat, (0, rows * PACK_COLS - flat.shape[0])).reshape(rows, PACK_COLS)


def _unpack(packed, shapes):
    flat = packed.reshape(-1)
    out, off = [], 0
    for s in shapes:
        size = 1
        for dim in s:
            size *= dim
        out.append(flat[off:off + size].reshape(s))
        off += size
    return out


def _widen_w_in(w):
    return jnp.concatenate([w[..., :DT_LO], jnp.repeat(w[..., DT_LO:DT_LO + N_HEADS], HEAD, axis=-1),
                            w[..., DT_LO + N_HEADS:]], axis=-1)


def _narrow_w_in_grad(g):
    dt = g[..., DT_LO:DT_LO + B_W]
    dt = dt.reshape(dt.shape[:-1] + (N_HEADS, HEAD)).sum(-1)
    return jnp.concatenate([g[..., :DT_LO], dt, g[..., DT_LO + B_W:]], axis=-1)


def _per_head(a):
    return a.reshape(a.shape[:-1] + (N_HEADS, HEAD)).sum(-1)


def kernel(x, ffn1_pre_g, ffn1_post_g, ffn1_w_gu, ffn1_w_down, mix_pre_g, mix_post_g, mix_w_in, mix_w_out, lru_conv_w, lru_conv_b, lru_w_r, lru_b_r, lru_w_i, lru_b_i, lru_lambda, ssd_conv_w, ssd_conv_b, ssd_dt_bias, ssd_a_log, ssd_d, ssd_norm_g, sgu_ln_g, sgu_ln_b, sgu_w_s, sgu_b_s, ffn2_pre_g, ffn2_post_g, ffn2_w_gu, ffn2_w_down, loss_target, m_ffn1_pre_g, m_ffn1_post_g, m_ffn1_w_gu, m_ffn1_w_down, m_mix_pre_g, m_mix_post_g, m_mix_w_in, m_mix_w_out, m_lru_conv_w, m_lru_conv_b, m_lru_w_r, m_lru_b_r, m_lru_w_i, m_lru_b_i, m_lru_lambda, m_ssd_conv_w, m_ssd_conv_b, m_ssd_dt_bias, m_ssd_a_log, m_ssd_d, m_ssd_norm_g, m_sgu_ln_g, m_sgu_ln_b, m_sgu_w_s, m_sgu_b_s, m_ffn2_pre_g, m_ffn2_post_g, m_ffn2_w_gu, m_ffn2_w_down, v_ffn1_pre_g, v_ffn1_post_g, v_ffn1_w_gu, v_ffn1_w_down, v_mix_pre_g, v_mix_post_g, v_mix_w_in, v_mix_w_out, v_lru_conv_w, v_lru_conv_b, v_lru_w_r, v_lru_b_r, v_lru_w_i, v_lru_b_i, v_lru_lambda, v_ssd_conv_w, v_ssd_conv_b, v_ssd_dt_bias, v_ssd_a_log, v_ssd_d, v_ssd_norm_g, v_sgu_ln_g, v_sgu_ln_b, v_sgu_w_s, v_sgu_b_s, v_ffn2_pre_g, v_ffn2_post_g, v_ffn2_w_gu, v_ffn2_w_down):
    given = dict(locals())
    w = {n: given[n] for n in WEIGHTS}
    mom = {n: given["m_" + n] for n in WEIGHTS}
    var = {n: given["v_" + n] for n in WEIGHTS}
    nl = ffn1_pre_g.shape[0]
    _, t, d = x.shape
    xi, yi, ci = _place()
    dev = 4 * xi + 2 * yi + ci
    c_dev = jnp.reshape(ci, (1,)).astype(jnp.int32)
    b_dev = jnp.reshape(2 * xi + yi, (1,)).astype(jnp.int32)

    conv_shapes = [lru_conv_w.shape, ssd_conv_w.shape]
    shards = [ffn1_w_gu.astype(bf16), ffn1_w_down.astype(bf16), _widen_w_in(mix_w_in).astype(bf16),
              mix_w_out.astype(bf16), ffn2_w_gu.astype(bf16), ffn2_w_down.astype(bf16)]
    nbig = len(shards)
    layer_shards = [[s[l:l + 1] for s in shards] for l in range(nl)]
    gather_started = [
        _split_start(f"gather_start_{l}", layer_shards[l], [lax.empty((N_DEV,) + s.shape, bf16) for s in layer_shards[l]],
                     (4 * nbig,), _gather_plan)
        for l in range(nl)]
    conv_all, = _all_gather([_pack([lru_conv_w, ssd_conv_w])])
    conv_full = []
    for k, shape in enumerate(conv_shapes):
        per_dev = jnp.stack([_unpack(conv_all[s], conv_shapes)[k] for s in range(N_DEV)], axis=2)
        conv_full.append(per_dev.reshape(shape[0], shape[1], N_DEV * shape[2]))
    lru_cw, ssd_cw = conv_full

    def vec(a):
        return a.reshape(nl, 1, -1)

    def per_channel(a):
        return jnp.repeat(a, HEAD, axis=-1).reshape(nl, 1, B_W)

    eye = jnp.eye(A_W // HEAD, dtype=f32)

    def block_diag(a):
        return jnp.einsum("lhij,hg->lhigj", a, eye).reshape(nl, A_W, A_W).astype(bf16)

    causal = jnp.tril(jnp.ones((CHUNK, CHUNK), dtype=bool))
    p = dict(
        ffn1_pre=vec(ffn1_pre_g), ffn1_post=vec(ffn1_post_g), mix_pre=vec(mix_pre_g), mix_post=vec(mix_post_g),
        ffn2_pre=vec(ffn2_pre_g), ffn2_post=vec(ffn2_post_g),
        lru=(lru_cw, vec(lru_conv_b), block_diag(lru_w_r), block_diag(lru_w_i), vec(lru_b_r), vec(lru_b_i), vec(lru_lambda)),
        ssd=(ssd_cw, vec(ssd_conv_b), per_channel(ssd_dt_bias), per_channel(ssd_a_log), per_channel(ssd_d), vec(ssd_norm_g)),
    )
    wm = jnp.where(causal, sgu_w_s, 0.0).astype(bf16)
    sgu_bias = jnp.repeat(jnp.swapaxes(sgu_b_s, 1, 2), HEAD, axis=2)
    sgu_f = (vec(sgu_ln_g), vec(sgu_ln_b), wm, sgu_bias)
    sgu_b = (vec(sgu_ln_g), vec(sgu_ln_b), wm, jnp.swapaxes(wm, 2, 3), sgu_bias)

    xs = x.reshape(t, d)
    saved, gathered = [], []
    for l in range(nl):
        waited = _split_wait(f"gather_wait_{l}", gather_started[l], nbig, _gather_plan, xs)
        gathered.append(_gather_finish(list(waited[:nbig]), list(waited[nbig:])))
        wgu1, wd1, win, wout, wgu2, wd2 = gathered[l]
        x0 = xs
        deps = tuple(started[-1] for started in gather_started[1:]) if l == 0 else ()
        x1, hb1, g1, u1, f1 = _ffn_fwd(x0, p["ffn1_pre"], p["ffn1_post"], wgu1, wd1, l, deps)
        hbm, pa, pb, pc = _mix_in_fwd(x1, p["mix_pre"], win, l)
        ya, h = _lru_fwd(pa, *p["lru"], l)
        yb, yp, sp = _ssd_fwd(pb, *p["ssd"], l)
        yc = _sgu_fwd(pc, *sgu_f, l)
        x2, cat, m = _mix_out_fwd(x1, ya, yb, yc, p["mix_post"], wout, l)
        xs, hb2, g2, u2, f2 = _ffn_fwd(x2, p["ffn2_pre"], p["ffn2_post"], wgu2, wd2, l)
        saved.append((x0, hb1, g1, u1, f1, x1, hbm, pa, pb, pc, h, yp, sp, cat, m, x2, hb2, g2, u2, f2))
    dy, loss_part = _loss_fwd(xs, loss_target.reshape(t, d))
    loss = lax.psum(loss_part[0, 0], ("x", "y", "c"))

    small = {n: [None] * nl for n in REPLICATED + SHARDED_CONV}
    grads, delta, new_m, new_v = {}, {}, {}, {}
    fused = [n for n in BIG if n != "mix_w_in"]

    def oriented(a, n):
        return jnp.swapaxes(a, 1, 2) if n.endswith("w_gu") else a

    opt_in = {n: tuple(oriented(src[n], n) for src in (w, mom, var)) for n in fused}
    opt_out = {n: tuple(lax.empty(opt_in[n][0].shape, f32) for _ in range(4)) for n in fused}
    w_in_grads = [None] * nl
    grad_shapes = {n: (s.shape[2], s.shape[1]) if n.endswith("w_gu") else s.shape[1:] for n, s in zip(BIG, shards)}

    def finish_pair(pending, after):
        lp, started = pending
        done = _split_wait(f"pair_wait_{lp}", started, nbig, _pair_plan, after)
        sums = [_pair_add(g, r, c_dev) for g, r in zip(done[:nbig], done[nbig:])]
        landing = [lax.empty((3,) + s.shape[1:], bf16) for s in sums]
        return lp, _split_start(f"chips_start_{lp}", sums, landing, (3 * nbig,), _chips_plan)

    def finish_chips(pending, after):
        lp, started = pending
        done = _split_wait(f"chips_wait_{lp}", started, nbig, _chips_plan, after)
        for n, s, q in zip(BIG, done[:nbig], done[nbig:]):
            if n == "mix_w_in":
                w_in_grads[lp] = _grad_sum(s, q, b_dev)
            else:
                opt_out[n] = tuple(_adamw_layer(*opt_in[n], s, q, b_dev, opt_out[n], lp))

    pending_pair = pending_chips = None
    for l in reversed(range(nl)):
        x0, hb1, g1, u1, f1, x1, hbm, pa, pb, pc, h, yp, sp, cat, m, x2, hb2, g2, u2, f2 = saved[l]
        wgu1, wd1, win, wout, wgu2, wd2 = gathered[l]
        gbuf = {n: lax.empty((N_DEV, 1) + grad_shapes[n], bf16) for n in BIG}
        deps = () if pending_pair is None else (pending_pair[1][-1],)
        dx2, dfb, act, dg, du, dpre, dpost = _ffn_bwd(x2, dy, f2, p["ffn2_pre"], p["ffn2_post"], g2, u2, wgu2, wd2, l, deps)
        small["ffn2_pre_g"][l], small["ffn2_post_g"][l] = dpre[0], dpost[0]
        gbuf["ffn2_w_gu"] = _wgrad_cols(hb2, dg, gbuf["ffn2_w_gu"], 0, 0)
        gbuf["ffn2_w_gu"] = _wgrad_cols(hb2, du, gbuf["ffn2_w_gu"], 0, dg.shape[0])
        gbuf["ffn2_w_down"] = _wgrad_rows(act, dfb, gbuf["ffn2_w_down"], 0)
        deps = ()
        if pending_pair is not None:
            pending_chips = finish_pair(pending_pair, dx2)
            deps = (pending_chips[1][-1],)

        dm, dya, dyb, dyc, dpost = _mix_out_bwd(dx2, m, p["mix_post"], wout, l, deps)
        small["mix_post_g"][l] = dpost[0]
        gbuf["mix_w_out"] = _wgrad_kblocks(cat, [dm], gbuf["mix_w_out"], 0)
        dpc, dws, dbias, dlg, dlb = _sgu_bwd(pc, dyc, *sgu_b, l)
        small["sgu_w_s"][l] = jnp.where(causal, dws, 0.0)
        small["sgu_b_s"][l] = dbias.reshape(CHUNK, C_W // HEAD, HEAD).sum(-1).T
        small["sgu_ln_g"][l], small["sgu_ln_b"][l] = dlg[0], dlb[0]
        dpb, dcw, dcb, ddtb, dalog, ddsk, dng = _ssd_bwd(pb, yp, sp, dyb, *p["ssd"], l)
        small["ssd_conv_w"][l], small["ssd_conv_b"][l], small["ssd_norm_g"][l] = dcw, dcb[0], dng[0]
        small["ssd_dt_bias"][l], small["ssd_a_log"][l], small["ssd_d"][l] = _per_head(ddtb[0]), _per_head(dalog[0]), _per_head(ddsk[0])
        dpa, dcw, dcb, dwr, dwi, dbr, dbi, dlam = _lru_bwd(pa, h, dya, *p["lru"], l)
        small["lru_conv_w"][l], small["lru_conv_b"][l], small["lru_lambda"][l] = dcw, dcb[0], dlam[0]
        small["lru_b_r"][l], small["lru_b_i"][l] = dbr[0], dbi[0]
        heads = range(A_W // HEAD)
        small["lru_w_r"][l] = jnp.stack([dwr[HEAD * i:HEAD * (i + 1), HEAD * i:HEAD * (i + 1)] for i in heads])
        small["lru_w_i"][l] = jnp.stack([dwi[HEAD * i:HEAD * (i + 1), HEAD * i:HEAD * (i + 1)] for i in heads])
        dx1, dpre = _mix_in_bwd(x1, dx2, p["mix_pre"], dpa, dpb, dpc, win, l)
        small["mix_pre_g"][l] = dpre[0]
        gbuf["mix_w_in"] = _wgrad_kblocks(hbm, [dpa, dpb, dpc], gbuf["mix_w_in"], 0)

        dy, dfb, act, dg, du, dpre, dpost = _ffn_bwd(x0, dx1, f1, p["ffn1_pre"], p["ffn1_post"], g1, u1, wgu1, wd1, l)
        small["ffn1_pre_g"][l], small["ffn1_post_g"][l] = dpre[0], dpost[0]
        gbuf["ffn1_w_gu"] = _wgrad_cols(hb1, dg, gbuf["ffn1_w_gu"], 0, 0)
        gbuf["ffn1_w_gu"] = _wgrad_cols(hb1, du, gbuf["ffn1_w_gu"], 0, dg.shape[0])
        gbuf["ffn1_w_down"] = _wgrad_rows(act, dfb, gbuf["ffn1_w_down"], 0)
        if pending_chips is not None:
            finish_chips(pending_chips, dy)
        landing = [lax.empty((4, 1) + grad_shapes[n], bf16) for n in BIG]
        pending_pair = (l, _split_start(f"pair_start_{l}", [gbuf[n] for n in BIG], landing, (4 * nbig,), _pair_plan))
    grad_x = dy.reshape(x.shape)
    finish_chips(finish_pair(pending_pair, dy), dy)

    for n in fused:
        grads[n], delta[n], new_m[n], new_v[n] = (oriented(a, n) for a in opt_out[n])
    grads["mix_w_in"] = _narrow_w_in_grad(jnp.concatenate(w_in_grads, axis=0))
    delta["mix_w_in"], new_m["mix_w_in"], new_v["mix_w_in"] = _adamw(
        w["mix_w_in"], mom["mix_w_in"], var["mix_w_in"], grads["mix_w_in"])

    names = REPLICATED + SHARDED_CONV
    stacked = [jnp.stack(small[n]) for n in names]
    total = _sum_devices(_all_gather([_pack(stacked)])[0])
    full = dict(zip(names, _unpack(total, [a.shape for a in stacked])))
    for n in REPLICATED:
        grads[n] = full[n]
    for n in SHARDED_CONV:
        cols = w[n].shape[2]
        grads[n] = lax.dynamic_slice_in_dim(full[n], dev * cols, cols, axis=2)
    shapes = [w[n].shape for n in names]
    packs = [_pack([src[n] for n in names])[None] for src in (w, mom, var, grads)]
    for dst, packed in zip((delta, new_m, new_v), _adamw(*packs)):
        dst.update(zip(names, _unpack(packed[0], shapes)))

    return (loss, grad_x, *[grads[n] for n in WEIGHTS], *[delta[n] for n in WEIGHTS],
            *[new_m[n] for n in WEIGHTS], *[new_v[n] for n in WEIGHTS])
```

```python
import functools

import jax
import jax.numpy as jnp
from jax import lax
from jax.experimental import pallas as pl
from jax.experimental.pallas import tpu as pltpu

f32, bf16 = jnp.float32, jnp.bfloat16
MESH = pl.DeviceIdType.MESH
ANY = pl.BlockSpec(memory_space=pl.ANY)

N_DEV = 8
NORM_EPS = 1e-6
LRU_C = 8.0
CHUNK = 128
HEAD = 64
A_W, B_W, C_W = 384, 384, 256
B_STATE = 128
XBC_W = B_W + 4 * B_STATE
PA_W, PB_W, PC_W = 2 * A_W, B_W + XBC_W + B_W, 2 * C_W
IN_PAD = PA_W + PB_W + PC_W
ADAM_LR, ADAM_B1, ADAM_B2, ADAM_EPS, ADAM_WD, ADAM_STEP = 0.001, 0.9, 0.999, 1e-08, 0.01, 10
VMEM_LIMIT_BYTES = 56 * 1024 * 1024
NEG_BIG = -1e30


def _params(sem=None):
    return pltpu.CompilerParams(dimension_semantics=sem, vmem_limit_bytes=VMEM_LIMIT_BYTES)


def _nn(a, b):
    return jnp.dot(a, b, preferred_element_type=f32)


def _nt(a, b):
    return lax.dot_general(a, b, (((1,), (1,)), ((), ())), preferred_element_type=f32)


def _tn(a, b):
    return lax.dot_general(a, b, (((0,), (0,)), ((), ())), preferred_element_type=f32)


def _sigmoid(x):
    return 1.0 / (1.0 + jnp.exp(-x))


def _softplus(x):
    return jnp.maximum(x, 0.0) + jnp.log(1.0 + jnp.exp(-jnp.abs(x)))


_GELU_C0, _GELU_C1 = 0.7978845608028654, 0.044715


def _gelu(x):
    t = jnp.tanh(_GELU_C0 * (x + _GELU_C1 * x * x * x))
    return 0.5 * x * (1.0 + t)


def _gelu_grad(x):
    t = jnp.tanh(_GELU_C0 * (x + _GELU_C1 * x * x * x))
    return 0.5 * (1.0 + t) + 0.5 * x * (1.0 - t * t) * _GELU_C0 * (1.0 + 3.0 * _GELU_C1 * x * x)


def _silu_grad(x, s):
    return s * (1.0 + x * (1.0 - s))


def _rms_fwd(x, g):
    r = lax.rsqrt(jnp.mean(x * x, axis=-1, keepdims=True) + NORM_EPS)
    return x * r * g


def _rms_bwd(x, g, dy):
    r = lax.rsqrt(jnp.mean(x * x, axis=-1, keepdims=True) + NORM_EPS)
    xh = x * r
    dxh = dy * g
    dx = r * (dxh - xh * jnp.mean(dxh * xh, axis=-1, keepdims=True))
    return dx, jnp.sum(dy * xh, axis=0, keepdims=True)


def _one_minus_exp(x):
    series = -x * (1.0 + x * (0.5 + x * (1.0 / 6.0 + x * (1.0 / 24.0))))
    return jnp.where(x > -0.01, series, 1.0 - jnp.exp(x))


def _cumsum_rows(x):
    row = lax.broadcasted_iota(jnp.int32, x.shape, 0)
    d = 1
    while d < x.shape[0]:
        x = x + jnp.where(row >= d, pltpu.roll(x, d, 0), 0.0)
        d *= 2
    return x


def _tile(t, cap):
    tm = min(cap, t)
    assert t % tm == 0
    return tm


def _after(body, n_in, deps):
    def wrapped(*refs):
        return body(*refs[:n_in], *refs[n_in + len(deps):])
    return wrapped


def _lspec(a, l):
    return pl.BlockSpec((None,) + a.shape[1:], lambda *_: (l,) + (0,) * (a.ndim - 1))


def _wd_rows(wd_ref):
    return wd_ref[:, 0].reshape(2 * wd_ref.shape[2], wd_ref.shape[3])


def _ffn_fwd(x, pre_g, post_g, wgu, wd, l, deps=()):
    t, d = x.shape
    nb, _, _, h = wgu.shape
    nj = nb // 2
    tm = _tile(t, 512)

    def body(x_ref, pg_ref, qg_ref, wg_ref, wu_ref, wd_ref, y_ref, hb_ref, g_ref, u_ref, f_ref, acc_ref):
        j = pl.program_id(1)

        @pl.when(j == 0)
        def _():
            hb_ref[...] = _rms_fwd(x_ref[...], pg_ref[...]).astype(bf16)

        hb = hb_ref[...]
        g = _nn(hb, wg_ref[0, 0])
        u = _nn(hb, wu_ref[0, 0])
        g_ref[0] = g.astype(bf16)
        u_ref[0] = u.astype(bf16)
        a = (g * _sigmoid(g) * u).astype(bf16)
        part = _nn(a, _wd_rows(wd_ref))

        @pl.when(j == 0)
        def _():
            acc_ref[...] = part

        @pl.when(j > 0)
        def _():
            acc_ref[...] += part

        @pl.when(j == nj - 1)
        def _():
            f = acc_ref[...]
            f_ref[...] = f
            y_ref[...] = x_ref[...] + 0.5 * _rms_fwd(f, qg_ref[...])

    row = pl.BlockSpec((tm, d), lambda i, j: (i, 0))
    vec = pl.BlockSpec((1, d), lambda i, j: (0, 0))
    act = pl.BlockSpec((1, tm, h), lambda i, j: (j, i, 0))
    return pl.pallas_call(
        _after(body, 6, deps), name="ffn_fwd", grid=(t // tm, nj),
        in_specs=[row, _lspec(pre_g, l), _lspec(post_g, l),
                  pl.BlockSpec((1, 1, d, h), lambda i, j: (j, 0, 0, 0)),
                  pl.BlockSpec((1, 1, d, h), lambda i, j: (j + nj, 0, 0, 0)),
                  pl.BlockSpec((2, 1, h // 2, d), lambda i, j: (j, 0, 0, 0))] + [ANY] * len(deps),
        out_specs=[row, row, act, act, row],
        out_shape=[jax.ShapeDtypeStruct((t, d), f32), jax.ShapeDtypeStruct((t, d), bf16),
                   jax.ShapeDtypeStruct((nj, t, h), bf16), jax.ShapeDtypeStruct((nj, t, h), bf16),
                   jax.ShapeDtypeStruct((t, d), f32)],
        scratch_shapes=[pltpu.VMEM((tm, d), f32)],
        compiler_params=_params(("arbitrary", "arbitrary")),
    )(x, pre_g, post_g, wgu, wgu, wd, *deps)


def _ffn_bwd(x, dy, f, pre_g, post_g, g, u, wgu, wd, l, deps=()):
    t, d = x.shape
    nj, _, h = g.shape
    tm = _tile(t, 512)

    def body(x_ref, dy_ref, f_ref, pg_ref, qg_ref, g_ref, u_ref, wg_ref, wu_ref, wd_ref,
             dx_ref, dfb_ref, a_ref, dg_ref, du_ref, dpg_ref, dqg_ref, dh_ref):
        i, j = pl.program_id(0), pl.program_id(1)

        @pl.when((i == 0) & (j == 0))
        def _():
            dpg_ref[...] = jnp.zeros_like(dpg_ref)
            dqg_ref[...] = jnp.zeros_like(dqg_ref)

        @pl.when(j == 0)
        def _():
            df, dq = _rms_bwd(f_ref[...], qg_ref[...], 0.5 * dy_ref[...])
            dfb_ref[...] = df.astype(bf16)
            dqg_ref[...] += dq

        da = _nt(dfb_ref[...], _wd_rows(wd_ref))
        gv = g_ref[0].astype(f32)
        uv = u_ref[0].astype(f32)
        s = _sigmoid(gv)
        sg = gv * s
        a_ref[0] = (sg * uv).astype(bf16)
        dg = (da * uv * _silu_grad(gv, s)).astype(bf16)
        du = (da * sg).astype(bf16)
        dg_ref[0] = dg
        du_ref[0] = du
        part = _nt(dg, wg_ref[0, 0]) + _nt(du, wu_ref[0, 0])

        @pl.when(j == 0)
        def _():
            dh_ref[...] = part

        @pl.when(j > 0)
        def _():
            dh_ref[...] += part

        @pl.when(j == nj - 1)
        def _():
            dxn, dp = _rms_bwd(x_ref[...], pg_ref[...], dh_ref[...])
            dx_ref[...] = dy_ref[...] + dxn
            dpg_ref[...] += dp

    row = pl.BlockSpec((tm, d), lambda i, j: (i, 0))
    vec = pl.BlockSpec((1, d), lambda i, j: (0, 0))
    act = pl.BlockSpec((1, tm, h), lambda i, j: (j, i, 0))
    act_shape = jax.ShapeDtypeStruct((nj, t, h), bf16)
    return pl.pallas_call(
        _after(body, 10, deps), name="ffn_bwd", grid=(t // tm, nj),
        in_specs=[row, row, row, _lspec(pre_g, l), _lspec(post_g, l), act, act,
                  pl.BlockSpec((1, 1, d, h), lambda i, j: (j, 0, 0, 0)),
                  pl.BlockSpec((1, 1, d, h), lambda i, j: (j + nj, 0, 0, 0)),
                  pl.BlockSpec((2, 1, h // 2, d), lambda i, j: (j, 0, 0, 0))] + [ANY] * len(deps),
        out_specs=[row, row, act, act, act, vec, vec],
        out_shape=[jax.ShapeDtypeStruct((t, d), f32), jax.ShapeDtypeStruct((t, d), bf16),
                   act_shape, act_shape, act_shape,
                   jax.ShapeDtypeStruct((1, d), f32), jax.ShapeDtypeStruct((1, d), f32)],
        scratch_shapes=[pltpu.VMEM((tm, d), f32)],
        compiler_params=_params(("arbitrary", "arbitrary")),
    )(x, dy, f, pre_g, post_g, g, u, wgu, wgu, wd, *deps)


def _wgrad_cols(x, dy, buf, l, slot0):
    (t, k), (nj, _, n) = x.shape, dy.shape

    def body(x_ref, dy_ref, buf_ref, o_ref):
        o_ref[0, 0] = _tn(dy_ref[0], x_ref[...]).astype(bf16)

    return pl.pallas_call(
        body, name="wgrad_cols", grid=(nj,),
        in_specs=[pl.BlockSpec((t, k), lambda b: (0, 0)), pl.BlockSpec((1, t, n), lambda b: (b, 0, 0)), ANY],
        out_specs=pl.BlockSpec((1, 1, n, k), lambda b: (b + slot0, l, 0, 0)),
        out_shape=jax.ShapeDtypeStruct(buf.shape, bf16), input_output_aliases={2: 0},
        compiler_params=_params(("arbitrary",)),
    )(x, dy, buf)


def _wgrad_rows(x, dy, buf, l):
    (nj, t, k), (_, n) = x.shape, dy.shape

    def body(x_ref, dy_ref, buf_ref, o_ref):
        o_ref[:, 0] = _tn(x_ref[0], dy_ref[...]).astype(bf16).reshape(2, k // 2, n)

    return pl.pallas_call(
        body, name="wgrad_rows", grid=(nj,),
        in_specs=[pl.BlockSpec((1, t, k), lambda b: (b, 0, 0)), pl.BlockSpec((t, n), lambda b: (0, 0)), ANY],
        out_specs=pl.BlockSpec((2, 1, k // 2, n), lambda b: (b, l, 0, 0)),
        out_shape=jax.ShapeDtypeStruct(buf.shape, bf16), input_output_aliases={2: 0},
        compiler_params=_params(("arbitrary",)),
    )(x, dy, buf)


def _wgrad_kblocks(x, dys, buf, l):
    t, k = x.shape
    kb = k // N_DEV
    widths = [dy.shape[1] for dy in dys]
    n = sum(widths)
    nd = len(dys)

    def body(x_ref, *refs):
        dy_hbm, o_ref, dy_vmem = refs[:nd], refs[nd + 1], refs[nd + 2:]

        @pl.when(pl.program_id(0) == 0)
        def _():
            for src, dst in zip(dy_hbm, dy_vmem):
                pltpu.sync_copy(src, dst)

        off = 0
        for dst, w in zip(dy_vmem, widths):
            o_ref[0, 0, :, off:off + w] = _tn(x_ref[...], dst[...]).astype(bf16)
            off += w

    return pl.pallas_call(
        body, name="wgrad_kblocks", grid=(N_DEV,),
        in_specs=[pl.BlockSpec((t, kb), lambda s: (0, s))] + [ANY] * (nd + 1),
        out_specs=pl.BlockSpec((1, 1, kb, n), lambda s: (s, l, 0, 0)),
        out_shape=jax.ShapeDtypeStruct(buf.shape, bf16), input_output_aliases={nd + 1: 0},
        scratch_shapes=[pltpu.VMEM((t, w), bf16) for w in widths],
        compiler_params=_params(("arbitrary",)),
    )(x, *dys, buf)


def _gathered_rows(w_ref, lo, hi):
    return w_ref[:, 0, :, lo:hi].reshape(N_DEV * w_ref.shape[2], hi - lo)


def _gathered_spec(w):
    return pl.BlockSpec((N_DEV, 1) + w.shape[2:], lambda i: (0, 0, 0, 0))


def _mix_in_fwd(x, pre_g, w_in, l):
    t, d = x.shape
    tm = _tile(t, 512)

    def body(x_ref, g_ref, w_ref, hb_ref, pa_ref, pb_ref, pc_ref):
        hb = _rms_fwd(x_ref[...], g_ref[...]).astype(bf16)
        hb_ref[...] = hb
        pa_ref[...] = _nn(hb, _gathered_rows(w_ref, 0, PA_W))
        pb_ref[...] = _nn(hb, _gathered_rows(w_ref, PA_W, PA_W + PB_W))
        pc_ref[...] = _nn(hb, _gathered_rows(w_ref, PA_W + PB_W, IN_PAD))

    def row(w):
        return pl.BlockSpec((tm, w), lambda i: (i, 0))

    return pl.pallas_call(
        body, name="mix_in_fwd", grid=(t // tm,),
        in_specs=[row(d), _lspec(pre_g, l), _gathered_spec(w_in)],
        out_specs=[row(d), row(PA_W), row(PB_W), row(PC_W)],
        out_shape=[jax.ShapeDtypeStruct((t, d), bf16), jax.ShapeDtypeStruct((t, PA_W), f32),
                   jax.ShapeDtypeStruct((t, PB_W), f32), jax.ShapeDtypeStruct((t, PC_W), f32)],
        compiler_params=_params(("arbitrary",)),
    )(x, pre_g, w_in)


def _mix_in_bwd(x, dy, pre_g, dpa, dpb, dpc, w_in, l):
    t, d = x.shape
    tm = _tile(t, 512)

    def body(x_ref, dy_ref, g_ref, dpa_ref, dpb_ref, dpc_ref, w_ref, dx_ref, dg_ref):
        @pl.when(pl.program_id(0) == 0)
        def _():
            dg_ref[...] = jnp.zeros_like(dg_ref)

        dh = (_nt(dpa_ref[...], _gathered_rows(w_ref, 0, PA_W))
              + _nt(dpb_ref[...], _gathered_rows(w_ref, PA_W, PA_W + PB_W))
              + _nt(dpc_ref[...], _gathered_rows(w_ref, PA_W + PB_W, IN_PAD)))
        dxn, dg = _rms_bwd(x_ref[...], g_ref[...], dh)
        dx_ref[...] = dy_ref[...] + dxn
        dg_ref[...] += dg

    def row(w):
        return pl.BlockSpec((tm, w), lambda i: (i, 0))

    vec = pl.BlockSpec((1, d), lambda i: (0, 0))
    return pl.pallas_call(
        body, name="mix_in_bwd", grid=(t // tm,),
        in_specs=[row(d), row(d), _lspec(pre_g, l), row(PA_W), row(PB_W), row(PC_W), _gathered_spec(w_in)],
        out_specs=[row(d), vec],
        out_shape=[jax.ShapeDtypeStruct((t, d), f32), jax.ShapeDtypeStruct((1, d), f32)],
        compiler_params=_params(("arbitrary",)),
    )(x, dy, pre_g, dpa, dpb, dpc, w_in)


def _mix_out_fwd(x, ya, yb, yc, post_g, w_out, l):
    t, d = x.shape
    tm = _tile(t, 512)

    def body(x_ref, ya_ref, yb_ref, yc_ref, g_ref, w_ref, y_ref, cat_ref, m_ref):
        cat_ref[:, 0:A_W] = ya_ref[...].astype(bf16)
        cat_ref[:, A_W:A_W + B_W] = yb_ref[...].astype(bf16)
        cat_ref[:, A_W + B_W:d] = yc_ref[...].astype(bf16)
        m = _nn(cat_ref[...], _gathered_rows(w_ref, 0, d))
        m_ref[...] = m
        y_ref[...] = x_ref[...] + _rms_fwd(m, g_ref[...])

    def row(w):
        return pl.BlockSpec((tm, w), lambda i: (i, 0))

    return pl.pallas_call(
        body, name="mix_out_fwd", grid=(t // tm,),
        in_specs=[row(d), row(A_W), row(B_W), row(C_W), _lspec(post_g, l), _gathered_spec(w_out)],
        out_specs=[row(d), row(d), row(d)],
        out_shape=[jax.ShapeDtypeStruct((t, d), f32), jax.ShapeDtypeStruct((t, d), bf16), jax.ShapeDtypeStruct((t, d), f32)],
        compiler_params=_params(("arbitrary",)),
    )(x, ya, yb, yc, post_g, w_out)


def _mix_out_bwd(dy, m, post_g, w_out, l, deps=()):
    t, d = m.shape
    tm = _tile(t, 512)

    def body(dy_ref, m_ref, g_ref, w_ref, dm_ref, dya_ref, dyb_ref, dyc_ref, dg_ref):
        @pl.when(pl.program_id(0) == 0)
        def _():
            dg_ref[...] = jnp.zeros_like(dg_ref)

        dm, dg = _rms_bwd(m_ref[...], g_ref[...], dy_ref[...])
        dmb = dm.astype(bf16)
        dm_ref[...] = dmb
        dg_ref[...] += dg
        dcat = _nt(dmb, _gathered_rows(w_ref, 0, d))
        dya_ref[...] = dcat[:, 0:A_W]
        dyb_ref[...] = dcat[:, A_W:A_W + B_W]
        dyc_ref[...] = dcat[:, A_W + B_W:d]

    def row(w):
        return pl.BlockSpec((tm, w), lambda i: (i, 0))

    vec = pl.BlockSpec((1, d), lambda i: (0, 0))
    return pl.pallas_call(
        _after(body, 4, deps), name="mix_out_bwd", grid=(t // tm,),
        in_specs=[row(d), row(d), _lspec(post_g, l), _gathered_spec(w_out)] + [ANY] * len(deps),
        out_specs=[row(d), row(A_W), row(B_W), row(C_W), vec],
        out_shape=[jax.ShapeDtypeStruct((t, d), bf16), jax.ShapeDtypeStruct((t, A_W), f32),
                   jax.ShapeDtypeStruct((t, B_W), f32), jax.ShapeDtypeStruct((t, C_W), f32),
                   jax.ShapeDtypeStruct((1, d), f32)],
        compiler_params=_params(("arbitrary",)),
    )(dy, m, post_g, w_out, *deps)


def _conv_fwd(buf_ref, halo, x, w, b, n):
    buf_ref[0:8, :] = halo
    buf_ref[8:8 + n, :] = x
    out = b + w[3:4, :] * x
    for k in range(3):
        out = out + w[k:k + 1, :] * buf_ref[pl.ds(5 + k, n), :]
    return out


def _conv_bwd(buf_ref, dbuf_ref, dout, dnext, w, n):
    dbuf_ref[0:n, :] = dout
    dbuf_ref[n:n + 8, :] = dnext
    dx = w[3:4, :] * dout
    dws = []
    for k in range(3):
        dx = dx + w[k:k + 1, :] * dbuf_ref[pl.ds(3 - k, n), :]
        dws.append(jnp.sum(dout * buf_ref[pl.ds(5 + k, n), :], axis=0, keepdims=True))
    dws.append(jnp.sum(dout * buf_ref[pl.ds(8, n), :], axis=0, keepdims=True))
    return dx, jnp.concatenate(dws, axis=0), jnp.sum(dout, axis=0, keepdims=True)


def _lru_gates(rec, wr, wi, br, bi, lam):
    rb = rec.astype(bf16)
    r = _sigmoid(_nn(rb, wr) + br)
    ig = _sigmoid(_nn(rb, wi) + bi)
    sp = _softplus(-lam)
    la = -LRU_C * r * sp
    a = jnp.exp(la)
    mult = jnp.sqrt(_one_minus_exp(2.0 * la))
    return rb, r, ig, sp, a, mult


def _lru_fwd(pa, conv_w, conv_b, wr, wi, br, bi, lam, l):
    t = pa.shape[0]
    tc = _tile(t, 512)

    def body(pa_ref, halo_ref, cw_ref, cb_ref, wr_ref, wi_ref, br_ref, bi_ref, lam_ref,
             ya_ref, h_ref, buf_ref, a_ref, u_ref, carry_ref):
        i = pl.program_id(0)

        @pl.when(i == 0)
        def _():
            carry_ref[...] = jnp.zeros_like(carry_ref)

        halo = jnp.where(i > 0, halo_ref[:, A_W:PA_W], 0.0)
        rec = _conv_fwd(buf_ref, halo, pa_ref[:, A_W:PA_W], cw_ref[...], cb_ref[...], tc)
        _, _, ig, _, a, mult = _lru_gates(rec, wr_ref[...], wi_ref[...], br_ref[...], bi_ref[...], lam_ref[...])
        a_ref[...] = a
        u_ref[...] = mult * (ig * rec)

        def step(s, h):
            h = a_ref[pl.ds(s, 1), :] * h + u_ref[pl.ds(s, 1), :]
            h_ref[pl.ds(s, 1), :] = h
            return h

        carry_ref[...] = lax.fori_loop(0, tc, step, carry_ref[...], unroll=8)
        ya_ref[...] = h_ref[...] * _gelu(pa_ref[:, 0:A_W])

    vec = pl.BlockSpec((1, A_W), lambda i: (0, 0))
    mat = pl.BlockSpec((A_W, A_W), lambda i: (0, 0))
    row = pl.BlockSpec((tc, A_W), lambda i: (i, 0))
    return pl.pallas_call(
        body, name="lru_fwd", grid=(t // tc,),
        in_specs=[pl.BlockSpec((tc, PA_W), lambda i: (i, 0)),
                  pl.BlockSpec((8, PA_W), lambda i: (jnp.maximum(i * (tc // 8) - 1, 0), 0)),
                  *[_lspec(a, l) for a in (conv_w, conv_b, wr, wi, br, bi, lam)]],
        out_specs=[row, row],
        out_shape=[jax.ShapeDtypeStruct((t, A_W), f32), jax.ShapeDtypeStruct((t, A_W), f32)],
        scratch_shapes=[pltpu.VMEM((8 + tc, A_W), f32), pltpu.VMEM((tc, A_W), f32), pltpu.VMEM((tc, A_W), f32),
                        pltpu.VMEM((1, A_W), f32)],
        compiler_params=_params(("arbitrary",)),
    )(pa, pa, conv_w, conv_b, wr, wi, br, bi, lam)


def _lru_bwd(pa, h, dya, conv_w, conv_b, wr, wi, br, bi, lam, l, deps=()):
    t = pa.shape[0]
    tc = _tile(t, 512)
    nc = t // tc

    def body(pa_ref, halo_ref, h_ref, hhalo_ref, dya_ref, cw_ref, cb_ref, wr_ref, wi_ref, br_ref, bi_ref, lam_ref,
             dpa_ref, dcw_ref, dcb_ref, dwr_ref, dwi_ref, dbr_ref, dbi_ref, dlam_ref,
             buf_ref, dbuf_ref, hbuf_ref, a_ref, g_ref, dh_ref, carry_ref, dnext_ref):
        i = pl.program_id(0)
        c = nc - 1 - i

        @pl.when(i == 0)
        def _():
            carry_ref[...] = jnp.zeros_like(carry_ref)
            dnext_ref[...] = jnp.zeros_like(dnext_ref)
            for ref in (dcw_ref, dcb_ref, dwr_ref, dwi_ref, dbr_ref, dbi_ref, dlam_ref):
                ref[...] = jnp.zeros_like(ref)

        halo = jnp.where(c > 0, halo_ref[:, A_W:PA_W], 0.0)
        cw = cw_ref[...]
        rec = _conv_fwd(buf_ref, halo, pa_ref[:, A_W:PA_W], cw, cb_ref[...], tc)
        lam = lam_ref[...]
        rb, r, ig, sp, a, mult = _lru_gates(rec, wr_ref[...], wi_ref[...], br_ref[...], bi_ref[...], lam)
        hbuf_ref[0:8, :] = jnp.where(c > 0, hhalo_ref[...], 0.0)
        hbuf_ref[8:8 + tc, :] = h_ref[...]
        h_prev = hbuf_ref[pl.ds(7, tc), :]
        gate = pa_ref[:, 0:A_W]
        dya = dya_ref[...]
        dpa_ref[:, 0:A_W] = (dya * h_ref[...] * _gelu_grad(gate)).astype(bf16)
        a_ref[...] = a
        g_ref[...] = dya * _gelu(gate)

        def step(s, carry):
            row = tc - 1 - s
            dh = g_ref[pl.ds(row, 1), :] + carry
            dh_ref[pl.ds(row, 1), :] = dh
            return a_ref[pl.ds(row, 1), :] * dh

        carry_ref[...] = lax.fori_loop(0, tc, step, carry_ref[...], unroll=8)
        dh = dh_ref[...]
        da = dh * h_prev
        dmult = dh * ig * rec
        dig = dh * mult * rec
        drec = dh * mult * ig
        dla = da * a - dmult * (a * a) / mult
        dr = dla * (-LRU_C * sp)
        dsp = jnp.sum(dla * (-LRU_C * r), axis=0, keepdims=True)
        dlam_ref[...] += dsp * (-_sigmoid(-lam))
        dpr = (dr * r * (1.0 - r))
        dpi = (dig * ig * (1.0 - ig))
        dprb, dpib = dpr.astype(bf16), dpi.astype(bf16)
        drec = drec + _nt(dprb, wr_ref[...]) + _nt(dpib, wi_ref[...])
        dwr_ref[...] += _tn(rb, dprb)
        dwi_ref[...] += _tn(rb, dpib)
        dbr_ref[...] += jnp.sum(dpr, axis=0, keepdims=True)
        dbi_ref[...] += jnp.sum(dpi, axis=0, keepdims=True)
        dx, dw, db = _conv_bwd(buf_ref, dbuf_ref, drec, dnext_ref[...], cw, tc)
        dnext_ref[...] = drec[0:8, :]
        dcw_ref[...] += dw
        dcb_ref[...] += db
        dpa_ref[:, A_W:PA_W] = dx.astype(bf16)

    vec = pl.BlockSpec((1, A_W), lambda i: (0, 0))
    mat = pl.BlockSpec((A_W, A_W), lambda i: (0, 0))
    cwspec = pl.BlockSpec((4, A_W), lambda i: (0, 0))

    def rev(w):
        return pl.BlockSpec((tc, w), lambda i: (nc - 1 - i, 0))

    def halo(w):
        return pl.BlockSpec((8, w), lambda i: (jnp.maximum((nc - 1 - i) * (tc // 8) - 1, 0), 0))

    chunk = pltpu.VMEM((tc, A_W), f32)
    return pl.pallas_call(
        _after(body, 12, deps), name="lru_bwd", grid=(nc,),
        in_specs=[rev(PA_W), halo(PA_W), rev(A_W), halo(A_W), rev(A_W),
                  *[_lspec(a, l) for a in (conv_w, conv_b, wr, wi, br, bi, lam)]] + [ANY] * len(deps),
        out_specs=[rev(PA_W), cwspec, vec, mat, mat, vec, vec, vec],
        out_shape=[jax.ShapeDtypeStruct((t, PA_W), bf16), jax.ShapeDtypeStruct((4, A_W), f32),
                   jax.ShapeDtypeStruct((1, A_W), f32), jax.ShapeDtypeStruct((A_W, A_W), f32),
                   jax.ShapeDtypeStruct((A_W, A_W), f32), jax.ShapeDtypeStruct((1, A_W), f32),
                   jax.ShapeDtypeStruct((1, A_W), f32), jax.ShapeDtypeStruct((1, A_W), f32)],
        scratch_shapes=[pltpu.VMEM((8 + tc, A_W), f32), pltpu.VMEM((tc + 8, A_W), f32), pltpu.VMEM((8 + tc, A_W), f32),
                        chunk, chunk, chunk, pltpu.VMEM((1, A_W), f32), pltpu.VMEM((8, A_W), f32)],
        compiler_params=_params(("arbitrary",)),
    )(pa, pa, h, h, dya, conv_w, conv_b, wr, wi, br, bi, lam, *deps)


def _sgu_norm(v, g, b):
    mu = jnp.mean(v, axis=-1, keepdims=True)
    vc = v - mu
    rstd = lax.rsqrt(jnp.mean(vc * vc, axis=-1, keepdims=True) + NORM_EPS)
    vh = vc * rstd
    return vh, rstd, vh * g + b


def _sgu_mix(w_ref, vb, bias):
    grp = lax.broadcasted_iota(jnp.int32, (CHUNK, C_W), 1) // HEAD
    out = bias
    for gi in range(C_W // HEAD):
        out = out + jnp.where(grp == gi, _nn(w_ref[gi], vb), 0.0)
    return out


def _sgu_fwd(pc, ln_g, ln_b, wm, bias, l):
    t = pc.shape[0]
    tm = _tile(t, 512)

    def body(pc_ref, g_ref, b_ref, w_ref, bias_ref, yc_ref):
        for ci in range(tm // CHUNK):
            rows = pl.ds(ci * CHUNK, CHUNK)
            ge = _gelu(pc_ref[rows, :])
            _, _, vn = _sgu_norm(ge[:, C_W:PC_W], g_ref[...], b_ref[...])
            yc_ref[rows, :] = ge[:, 0:C_W] * _sgu_mix(w_ref, vn.astype(bf16), bias_ref[...])

    vec = pl.BlockSpec((1, C_W), lambda i: (0, 0))
    return pl.pallas_call(
        body, name="sgu_fwd", grid=(t // tm,),
        in_specs=[pl.BlockSpec((tm, PC_W), lambda i: (i, 0)), *[_lspec(a, l) for a in (ln_g, ln_b, wm, bias)]],
        out_specs=pl.BlockSpec((tm, C_W), lambda i: (i, 0)),
        out_shape=jax.ShapeDtypeStruct((t, C_W), f32),
        compiler_params=_params(("arbitrary",)),
    )(pc, ln_g, ln_b, wm, bias)


def _sgu_bwd(pc, dyc, ln_g, ln_b, wm, wmt, bias, l, deps=()):
    t = pc.shape[0]
    tm = _tile(t, 512)

    def body(pc_ref, dyc_ref, g_ref, b_ref, w_ref, wt_ref, bias_ref, dpc_ref, dw_ref, dbias_ref, dg_ref, db_ref):
        @pl.when(pl.program_id(0) == 0)
        def _():
            for ref in (dw_ref, dbias_ref, dg_ref, db_ref):
                ref[...] = jnp.zeros_like(ref)

        grp = lax.broadcasted_iota(jnp.int32, (CHUNK, C_W), 1) // HEAD
        for ci in range(tm // CHUNK):
            rows = pl.ds(ci * CHUNK, CHUNK)
            x = pc_ref[rows, :]
            ge = _gelu(x)
            gv = g_ref[...]
            vh, rstd, vn = _sgu_norm(ge[:, C_W:PC_W], gv, b_ref[...])
            vb = vn.astype(bf16)
            mixed = _sgu_mix(w_ref, vb, bias_ref[...])
            dyc = dyc_ref[rows, :]
            du = dyc * mixed
            dmix = dyc * ge[:, 0:C_W]
            dmb = dmix.astype(bf16)
            dvn = jnp.zeros((CHUNK, C_W), f32)
            for gi in range(C_W // HEAD):
                dvn = dvn + jnp.where(grp == gi, _nn(wt_ref[gi], dmb), 0.0)
                dw_ref[gi] += _nt(jnp.where(grp == gi, dmix, 0.0).astype(bf16), vb)
            dbias_ref[...] += dmix
            dg_ref[...] += jnp.sum(dvn * vh, axis=0, keepdims=True)
            db_ref[...] += jnp.sum(dvn, axis=0, keepdims=True)
            dvh = dvn * gv
            dv = rstd * (dvh - jnp.mean(dvh, axis=-1, keepdims=True) - vh * jnp.mean(dvh * vh, axis=-1, keepdims=True))
            gg = _gelu_grad(x)
            dpc_ref[rows, 0:C_W] = (du * gg[:, 0:C_W]).astype(bf16)
            dpc_ref[rows, C_W:PC_W] = (dv * gg[:, C_W:PC_W]).astype(bf16)

    vec = pl.BlockSpec((1, C_W), lambda i: (0, 0))
    wspec = pl.BlockSpec((4, CHUNK, CHUNK), lambda i: (0, 0, 0))
    bspec = pl.BlockSpec((CHUNK, C_W), lambda i: (0, 0))
    return pl.pallas_call(
        _after(body, 7, deps), name="sgu_bwd", grid=(t // tm,),
        in_specs=[pl.BlockSpec((tm, PC_W), lambda i: (i, 0)), pl.BlockSpec((tm, C_W), lambda i: (i, 0)),
                  *[_lspec(a, l) for a in (ln_g, ln_b, wm, wmt, bias)]] + [ANY] * len(deps),
        out_specs=[pl.BlockSpec((tm, PC_W), lambda i: (i, 0)), wspec, bspec, vec, vec],
        out_shape=[jax.ShapeDtypeStruct((t, PC_W), bf16), jax.ShapeDtypeStruct((4, CHUNK, CHUNK), f32),
                   jax.ShapeDtypeStruct((CHUNK, C_W), f32), jax.ShapeDtypeStruct((1, C_W), f32),
                   jax.ShapeDtypeStruct((1, C_W), f32)],
        compiler_params=_params(("arbitrary",)),
    )(pc, dyc, ln_g, ln_b, wm, wmt, bias, *deps)


N_PAIR = B_W // 128
HEADS_PER_GROUP = 3


def _pair_groups(p):
    return (2 * p) // HEADS_PER_GROUP, (2 * p + 1) // HEADS_PER_GROUP


def _ssd_chunk(pb_ref, halo, buf_ref, cw, cb, dtb, alog):
    z = pb_ref[:, 0:B_W]
    pre = _conv_fwd(buf_ref, halo, pb_ref[:, B_W:B_W + XBC_W], cw, cb, CHUNK)
    sg = _sigmoid(pre)
    xbc = pre * sg
    xs = xbc[:, 0:B_W]
    bm = [xbc[:, B_W + k * B_STATE:B_W + (k + 1) * B_STATE] for k in range(2)]
    cm = [xbc[:, B_W + (2 + k) * B_STATE:B_W + (3 + k) * B_STATE] for k in range(2)]
    dtin = pb_ref[:, B_W + XBC_W:PB_W] + dtb
    dt = _softplus(dtin)
    a = -jnp.exp(alog)
    cs = _cumsum_rows(dt * a)
    return dict(z=z, pre=pre, sg=sg, xs=xs, bm=bm, cm=cm, dtin=dtin, dt=dt, a=a, cs=cs,
                ecs=jnp.exp(cs), ds=jnp.exp(cs[CHUNK - 1:CHUNK, :] - cs), xdt=xs * dt,
                bmb=[v.astype(bf16) for v in bm], cmb=[v.astype(bf16) for v in cm])


def _ssd_decay(cs_pair, half):
    cst = cs_pair.T
    lane0 = HEAD * half
    csc = jnp.broadcast_to(cs_pair[:, lane0:lane0 + 1], (CHUNK, CHUNK))
    csr = cst[lane0:lane0 + 1, :]
    tri = lax.broadcasted_iota(jnp.int32, (CHUNK, CHUNK), 0) >= lax.broadcasted_iota(jnp.int32, (CHUNK, CHUNK), 1)
    return jnp.exp(jnp.where(tri, csc - csr, NEG_BIG)), cst


def _ssd_fwd(pb, conv_w, conv_b, dtb, alog, dskip, norm_g, l):
    t = pb.shape[0]
    nc = t // CHUNK

    def body(pb_ref, halo_ref, cw_ref, cb_ref, dtb_ref, alog_ref, d_ref, ng_ref, yb_ref, yp_ref, sp_ref, buf_ref, s_ref):
        i = pl.program_id(0)

        @pl.when(i == 0)
        def _():
            s_ref[...] = jnp.zeros_like(s_ref)

        halo = jnp.where(i > 0, halo_ref[:, B_W:B_W + XBC_W], 0.0)
        q = _ssd_chunk(pb_ref, halo, buf_ref, cw_ref[...], cb_ref[...], dtb_ref[...], alog_ref[...])
        sp_ref[0] = s_ref[...]
        lane = lax.broadcasted_iota(jnp.int32, (CHUNK, 128), 1)
        rowi = lax.broadcasted_iota(jnp.int32, (128, B_STATE), 0)
        cb_mat = [_nt(q["cmb"][k], q["bmb"][k]) for k in range(2)]
        xd = q["xdt"] * q["ds"]
        for p in range(N_PAIR):
            cols = slice(128 * p, 128 * (p + 1))
            g_lo, g_hi = _pair_groups(p)
            cs_p, xdt_p = q["cs"][:, cols], q["xdt"][:, cols]
            s_p = s_ref[cols, :]
            s_pb = s_p.astype(bf16)
            y_p = jnp.zeros((CHUNK, 128), f32)
            for half, grp in ((0, g_lo), (1, g_hi)):
                lm, cst = _ssd_decay(cs_p, half)
                mb = (cb_mat[grp] * lm).astype(bf16)
                sel = (lane < HEAD) if half == 0 else (lane >= HEAD)
                y_p = y_p + _nn(mb, jnp.where(sel, xdt_p, 0.0).astype(bf16))
            off_lo = _nt(q["cmb"][g_lo], s_pb)
            off = off_lo if g_lo == g_hi else jnp.where(lane < HEAD, off_lo, _nt(q["cmb"][g_hi], s_pb))
            y_p = y_p + off * q["ecs"][:, cols] + q["xs"][:, cols] * d_ref[:, cols]
            yp_ref[:, cols] = y_p
            xd_pb = xd[:, cols].astype(bf16)
            upd_lo = _tn(xd_pb, q["bmb"][g_lo])
            upd = upd_lo if g_lo == g_hi else jnp.where(rowi < HEAD, upd_lo, _tn(xd_pb, q["bmb"][g_hi]))
            cd = jnp.exp(jnp.broadcast_to(cst[:, CHUNK - 1:CHUNK], (128, B_STATE)))
            s_ref[cols, :] = cd * s_p + upd
        z = q["z"]
        yg = yp_ref[...] * (z * _sigmoid(z))
        yb_ref[...] = _rms_fwd(yg, ng_ref[...])

    vec = pl.BlockSpec((1, B_W), lambda i: (0, 0))
    row = pl.BlockSpec((CHUNK, B_W), lambda i: (i, 0))
    return pl.pallas_call(
        body, name="ssd_fwd", grid=(nc,),
        in_specs=[pl.BlockSpec((CHUNK, PB_W), lambda i: (i, 0)),
                  pl.BlockSpec((8, PB_W), lambda i: (jnp.maximum(i * (CHUNK // 8) - 1, 0), 0)),
                  *[_lspec(a, l) for a in (conv_w, conv_b, dtb, alog, dskip, norm_g)]],
        out_specs=[row, row, pl.BlockSpec((1, B_W, B_STATE), lambda i: (i, 0, 0))],
        out_shape=[jax.ShapeDtypeStruct((t, B_W), f32), jax.ShapeDtypeStruct((t, B_W), f32),
                   jax.ShapeDtypeStruct((nc, B_W, B_STATE), f32)],
        scratch_shapes=[pltpu.VMEM((8 + CHUNK, XBC_W), f32), pltpu.VMEM((B_W, B_STATE), f32)],
        compiler_params=_params(("arbitrary",)),
    )(pb, pb, conv_w, conv_b, dtb, alog, dskip, norm_g)


def _ssd_bwd(pb, yp, sprev, dyb, conv_w, conv_b, dtb, alog, dskip, norm_g, l):
    t = pb.shape[0]
    nc = t // CHUNK

    def body(pb_ref, halo_ref, yp_ref, sp_ref, dyb_ref, cw_ref, cb_ref, dtb_ref, alog_ref, d_ref, ng_ref,
             dpb_ref, dcw_ref, dcb_ref, ddtb_ref, dalog_ref, dd_ref, dng_ref,
             buf_ref, dbuf_ref, ds_ref, dnext_ref, dxbc_ref, dcs_ref, dxdt_ref):
        i = pl.program_id(0)
        c = nc - 1 - i

        @pl.when(i == 0)
        def _():
            ds_ref[...] = jnp.zeros_like(ds_ref)
            dnext_ref[...] = jnp.zeros_like(dnext_ref)
            for ref in (dcw_ref, dcb_ref, ddtb_ref, dalog_ref, dd_ref, dng_ref):
                ref[...] = jnp.zeros_like(ref)

        halo = jnp.where(c > 0, halo_ref[:, B_W:B_W + XBC_W], 0.0)
        cw = cw_ref[...]
        q = _ssd_chunk(pb_ref, halo, buf_ref, cw, cb_ref[...], dtb_ref[...], alog_ref[...])
        z, xs, dt, a, ecs, dsd, xdt = q["z"], q["xs"], q["dt"], q["a"], q["ecs"], q["ds"], q["xdt"]
        sz = _sigmoid(z)
        siluz = z * sz
        yp = yp_ref[...]
        dyg, dng = _rms_bwd(yp * siluz, ng_ref[...], dyb_ref[...])
        dng_ref[...] += dng
        dy = dyg * siluz
        dpb_ref[:, 0:B_W] = (dyg * yp * _silu_grad(z, sz)).astype(bf16)
        dd_ref[...] += jnp.sum(dy * xs, axis=0, keepdims=True)
        g1 = dy * ecs
        lane = lax.broadcasted_iota(jnp.int32, (CHUNK, 128), 1)
        rowi = lax.broadcasted_iota(jnp.int32, (128, B_STATE), 0)
        rowc = lax.broadcasted_iota(jnp.int32, (CHUNK, 128), 0)
        cb_mat = [_nt(q["cmb"][k], q["bmb"][k]) for k in range(2)]
        d_cb = [jnp.zeros((CHUNK, CHUNK), f32) for _ in range(2)]
        d_b = [jnp.zeros((CHUNK, B_STATE), f32) for _ in range(2)]
        d_c = [jnp.zeros((CHUNK, B_STATE), f32) for _ in range(2)]
        for p in range(N_PAIR):
            cols = slice(128 * p, 128 * (p + 1))
            g_lo, g_hi = _pair_groups(p)
            lo, hi = lane < HEAD, lane >= HEAD
            cs_p, xdt_p, dy_p, ds_p, g1_p = q["cs"][:, cols], xdt[:, cols], dy[:, cols], dsd[:, cols], g1[:, cols]
            s_p = sp_ref[0, cols, :]
            s_pb = s_p.astype(bf16)
            dsn = ds_ref[cols, :]
            dsnb = dsn.astype(bf16)
            g1b = g1_p.astype(bf16)
            off_lo = _nt(q["cmb"][g_lo], s_pb)
            off = off_lo if g_lo == g_hi else jnp.where(lo, off_lo, _nt(q["cmb"][g_hi], s_pb))
            dcs_p = dy_p * off * ecs[:, cols]
            dsp_lo = _tn(g1b, q["cmb"][g_lo])
            dsp = dsp_lo if g_lo == g_hi else jnp.where(rowi < HEAD, dsp_lo, _tn(g1b, q["cmb"][g_hi]))
            dx_lo = _nt(q["bmb"][g_lo], dsnb)
            dxd = dx_lo if g_lo == g_hi else jnp.where(lo, dx_lo, _nt(q["bmb"][g_hi], dsnb))
            xd_p = xdt_p * ds_p
            if g_lo == g_hi:
                d_c[g_lo] = d_c[g_lo] + _nn(g1b, s_pb)
                d_b[g_lo] = d_b[g_lo] + _nn(xd_p.astype(bf16), dsnb)
            else:
                d_c[g_lo] = d_c[g_lo] + _nn(jnp.where(lo, g1_p, 0.0).astype(bf16), s_pb)
                d_c[g_hi] = d_c[g_hi] + _nn(jnp.where(hi, g1_p, 0.0).astype(bf16), s_pb)
                d_b[g_lo] = d_b[g_lo] + _nn(jnp.where(lo, xd_p, 0.0).astype(bf16), dsnb)
                d_b[g_hi] = d_b[g_hi] + _nn(jnp.where(hi, xd_p, 0.0).astype(bf16), dsnb)
            dxdt_p = dxd * ds_p
            t2 = dxd * xdt_p * ds_p
            dcs_p = dcs_p - t2
            dlast = jnp.sum(t2, axis=0, keepdims=True)
            cst = None
            for half, grp in ((0, g_lo), (1, g_hi)):
                sel = lo if half == 0 else hi
                lm, cst = _ssd_decay(cs_p, half)
                m = cb_mat[grp] * lm
                dyh = jnp.where(sel, dy_p, 0.0).astype(bf16)
                xdh = jnp.where(sel, xdt_p, 0.0).astype(bf16)
                dm = _nt(dyh, xdh)
                pm = dm * m
                col = jnp.sum(pm, axis=1, keepdims=True) - jnp.sum(pm.T, axis=1, keepdims=True)
                dcs_p = dcs_p + jnp.where(lane == HEAD * half, col, 0.0)
                d_cb[grp] = d_cb[grp] + dm * lm
                dxdt_p = dxdt_p + _tn(m.astype(bf16), dyh)
            cdcol = jnp.exp(jnp.broadcast_to(cst[:, CHUNK - 1:CHUNK], (128, B_STATE)))
            ds_ref[cols, :] = cdcol * dsn + dsp
            dcd_row = jnp.sum((dsn * s_p).T, axis=0, keepdims=True)
            dlast = dlast + dcd_row * ecs[CHUNK - 1:CHUNK, cols]
            dcs_ref[:, cols] = dcs_p + jnp.where(rowc == CHUNK - 1, dlast, 0.0)
            dxdt_ref[:, cols] = dxdt_p
        for k in range(2):
            dcbb = d_cb[k].astype(bf16)
            d_c[k] = d_c[k] + _nn(dcbb, q["bmb"][k])
            d_b[k] = d_b[k] + _tn(dcbb, q["cmb"][k])
            dxbc_ref[:, B_W + k * B_STATE:B_W + (k + 1) * B_STATE] = d_b[k]
            dxbc_ref[:, B_W + (2 + k) * B_STATE:B_W + (3 + k) * B_STATE] = d_c[k]
        dxdt = dxdt_ref[...]
        dxbc_ref[:, 0:B_W] = dy * d_ref[...] + dxdt * dt
        dcs = dcs_ref[...]
        dad = jnp.sum(dcs, axis=0, keepdims=True) - _cumsum_rows(dcs) + dcs
        ddt = dxdt * xs + dad * a
        dalog_ref[...] += jnp.sum(dad * dt, axis=0, keepdims=True) * a
        dtraw = ddt * _sigmoid(q["dtin"])
        ddtb_ref[...] += jnp.sum(dtraw, axis=0, keepdims=True)
        dpb_ref[:, B_W + XBC_W:PB_W] = dtraw.astype(bf16)
        dpre = dxbc_ref[...] * _silu_grad(q["pre"], q["sg"])
        dx, dw, db = _conv_bwd(buf_ref, dbuf_ref, dpre, dnext_ref[...], cw, CHUNK)
        dnext_ref[...] = dpre[0:8, :]
        dcw_ref[...] += dw
        dcb_ref[...] += db
        dpb_ref[:, B_W:B_W + XBC_W] = dx.astype(bf16)

    vec = pl.BlockSpec((1, B_W), lambda i: (0, 0))
    cwspec = pl.BlockSpec((4, XBC_W), lambda i: (0, 0))
    cbspec = pl.BlockSpec((1, XBC_W), lambda i: (0, 0))

    def rev(w):
        return pl.BlockSpec((CHUNK, w), lambda i: (nc - 1 - i, 0))

    vshape = jax.ShapeDtypeStruct((1, B_W), f32)
    return pl.pallas_call(
        body, name="ssd_bwd", grid=(nc,),
        in_specs=[rev(PB_W), pl.BlockSpec((8, PB_W), lambda i: (jnp.maximum((nc - 1 - i) * (CHUNK // 8) - 1, 0), 0)),
                  rev(B_W), pl.BlockSpec((1, B_W, B_STATE), lambda i: (nc - 1 - i, 0, 0)), rev(B_W),
                  *[_lspec(a, l) for a in (conv_w, conv_b, dtb, alog, dskip, norm_g)]],
        out_specs=[rev(PB_W), cwspec, cbspec, vec, vec, vec, vec],
        out_shape=[jax.ShapeDtypeStruct((t, PB_W), bf16), jax.ShapeDtypeStruct((4, XBC_W), f32),
                   jax.ShapeDtypeStruct((1, XBC_W), f32), vshape, vshape, vshape, vshape],
        scratch_shapes=[pltpu.VMEM((8 + CHUNK, XBC_W), f32), pltpu.VMEM((CHUNK + 8, XBC_W), f32),
                        pltpu.VMEM((B_W, B_STATE), f32), pltpu.VMEM((8, XBC_W), f32),
                        pltpu.VMEM((CHUNK, XBC_W), f32), pltpu.VMEM((CHUNK, B_W), f32), pltpu.VMEM((CHUNK, B_W), f32)],
        compiler_params=_params(("arbitrary",)),
    )(pb, pb, yp, sprev, dyb, conv_w, conv_b, dtb, alog, dskip, norm_g)


def _loss_fwd(y, target):
    t, d = y.shape
    tm = _tile(t, 512)

    def body(y_ref, t_ref, dy_ref, loss_ref):
        @pl.when(pl.program_id(0) == 0)
        def _():
            loss_ref[...] = jnp.zeros_like(loss_ref)

        e = y_ref[...] - t_ref[...]
        dy_ref[...] = e * (1.0 / d)
        per_tok = jnp.mean(e * e, axis=-1, keepdims=True)
        loss_ref[...] += 0.5 * jnp.sum(per_tok, axis=0, keepdims=True)

    row = pl.BlockSpec((tm, d), lambda i: (i, 0))
    return pl.pallas_call(
        body, name="loss_fwd", grid=(t // tm,), in_specs=[row, row],
        out_specs=[row, pl.BlockSpec((1, 128), lambda i: (0, 0))],
        out_shape=[jax.ShapeDtypeStruct((t, d), f32), jax.ShapeDtypeStruct((1, 128), f32)],
        compiler_params=_params(("arbitrary",)),
    )(y, target)


def _row_tile(r):
    return 512 if r % 512 == 0 else r


def _pair_add(g, r, c_dev):
    _, nl, rows, cols = g.shape
    tr = _row_tile(rows)

    def body(c_ref, g_ref, r_ref, o_ref):
        o_ref[...] = (g_ref[...].astype(f32) + r_ref[...].astype(f32)).astype(bf16)

    blk = (None, None, tr, cols)
    return pl.pallas_call(
        body, name="pair_add",
        grid_spec=pltpu.PrefetchScalarGridSpec(
            num_scalar_prefetch=1, grid=(4, nl, rows // tr),
            in_specs=[pl.BlockSpec(blk, lambda b, l, i, c: (2 * b + c[0], l, i, 0)),
                      pl.BlockSpec(blk, lambda b, l, i, c: (b, l, i, 0))],
            out_specs=pl.BlockSpec(blk, lambda b, l, i, c: (b, l, i, 0))),
        out_shape=jax.ShapeDtypeStruct(r.shape, bf16),
        compiler_params=_params(("arbitrary", "arbitrary", "arbitrary")),
    )(c_dev, g, r)


def _grad_sum(s, q, b_dev):
    _, nl, rows, cols = s.shape
    tr = _row_tile(rows)

    def body(b_ref, s_ref, q0_ref, q1_ref, q2_ref, o_ref):
        o_ref[...] = ((s_ref[...].astype(f32) + q0_ref[...].astype(f32)) + q1_ref[...].astype(f32)) + q2_ref[...].astype(f32)

    blk = (None, None, tr, cols)

    def qspec(k):
        return pl.BlockSpec(blk, lambda l, i, b: (k, l, i, 0))

    return pl.pallas_call(
        body, name="grad_sum",
        grid_spec=pltpu.PrefetchScalarGridSpec(
            num_scalar_prefetch=1, grid=(nl, rows // tr),
            in_specs=[pl.BlockSpec(blk, lambda l, i, b: (b[0], l, i, 0)), qspec(0), qspec(1), qspec(2)],
            out_specs=pl.BlockSpec((None, tr, cols), lambda l, i, b: (l, i, 0))),
        out_shape=jax.ShapeDtypeStruct(s.shape[1:], f32),
        compiler_params=_params(("arbitrary", "arbitrary")),
    )(b_dev, s, q, q, q)


def _sum_devices(parts):
    n, rows, cols = parts.shape
    tr = _row_tile(rows)

    def body(p_ref, o_ref):
        acc = p_ref[0]
        for k in range(1, n):
            acc = acc + p_ref[k]
        o_ref[...] = acc

    return pl.pallas_call(
        body, name="sum_devices", grid=(rows // tr,),
        in_specs=[pl.BlockSpec((n, tr, cols), lambda i: (0, i, 0))],
        out_specs=pl.BlockSpec((tr, cols), lambda i: (i, 0)),
        out_shape=jax.ShapeDtypeStruct((rows, cols), f32),
        compiler_params=_params(("arbitrary",)),
    )(parts)


def _adamw(w, m, v, g):
    nl, rows, cols = w.shape
    tr = _row_tile(rows)

    def body(w_ref, m_ref, v_ref, g_ref, d_ref, nm_ref, nv_ref):
        d_ref[...], nm_ref[...], nv_ref[...] = _adamw_math(w_ref[...], m_ref[...], v_ref[...], g_ref[...])

    blk = pl.BlockSpec((None, tr, cols), lambda l, i: (l, i, 0))
    shape = jax.ShapeDtypeStruct(w.shape, f32)
    return pl.pallas_call(
        body, name="adamw", grid=(nl, rows // tr), in_specs=[blk] * 4, out_specs=[blk] * 3,
        out_shape=[shape] * 3, compiler_params=_params(("arbitrary", "arbitrary")),
    )(w, m, v, g)


def _adamw_math(w, m, v, g):
    nm = ADAM_B1 * m + (1.0 - ADAM_B1) * g
    nv = ADAM_B2 * v + (1.0 - ADAM_B2) * (g * g)
    m_hat = nm / (1.0 - ADAM_B1 ** ADAM_STEP)
    v_hat = nv / (1.0 - ADAM_B2 ** ADAM_STEP)
    return -ADAM_LR * (m_hat / (jnp.sqrt(v_hat) + ADAM_EPS) + ADAM_WD * w), nm, nv


def _adamw_layer(w, m, v, s, q, b_dev, outs, l):
    _, rows, cols = w.shape
    tr = _row_tile(rows)

    def body(b_ref, w_ref, m_ref, v_ref, s_ref, q0_ref, q1_ref, q2_ref, o0, o1, o2, o3, g_ref, d_ref, nm_ref, nv_ref):
        g = ((s_ref[...].astype(f32) + q0_ref[...].astype(f32)) + q1_ref[...].astype(f32)) + q2_ref[...].astype(f32)
        g_ref[...] = g
        d_ref[...], nm_ref[...], nv_ref[...] = _adamw_math(w_ref[...], m_ref[...], v_ref[...], g)

    wspec = pl.BlockSpec((None, tr, cols), lambda i, b: (l, i, 0))
    blk = (None, None, tr, cols)

    def qspec(k):
        return pl.BlockSpec(blk, lambda i, b: (k, 0, i, 0))

    shape = jax.ShapeDtypeStruct(w.shape, f32)
    return pl.pallas_call(
        body, name="adamw_layer",
        grid_spec=pltpu.PrefetchScalarGridSpec(
            num_scalar_prefetch=1, grid=(rows // tr,),
            in_specs=[wspec] * 3 + [pl.BlockSpec(blk, lambda i, b: (b[0], 0, i, 0)), qspec(0), qspec(1), qspec(2)] + [ANY] * 4,
            out_specs=[wspec] * 4),
        out_shape=[shape] * 4, input_output_aliases={8 + k: k for k in range(4)},
        compiler_params=_params(("arbitrary",)),
    )(b_dev, w, m, v, s, q, q, q, *outs)


def _place():
    return lax.axis_index("x"), lax.axis_index("y"), lax.axis_index("c")


def _all_gather(shards, deps=()):
    n = len(shards)
    nd = len(deps)

    def body(*refs):
        src, dst = refs[:n], refs[n:2 * n]
        send_sems, recv_sems, local_sems = refs[2 * n:]
        x, y, c = _place()
        me, sibling = (x, y, c), (x, y, 1 - c)
        chips = [(1 - x, y), (x, 1 - y), (1 - x, 1 - y)]

        def copy(a, k, block, to, from_shard=False):
            px, py, pc = block
            rows = dst[a].at[4 * px + 2 * py + pc]
            return pltpu.make_async_remote_copy(
                src_ref=src[a] if from_shard else rows, dst_ref=rows,
                send_sem=send_sems.at[a, k], recv_sem=recv_sems.at[a, k], device_id=to, device_id_type=MESH)

        mine = [pltpu.make_async_copy(src[a], dst[a].at[4 * x + 2 * y + c], local_sems.at[a]) for a in range(n)]
        for cp in mine:
            cp.start()
        first = []
        for a in range(n):
            first.append(copy(a, 0, me, sibling, True))
            first += [copy(a, 1 + j, me, (*chip, c), True) for j, chip in enumerate(chips)]
        for cp in first:
            cp.start()
        passed = []
        for j, chip in enumerate(chips):
            for a in range(n):
                copy(a, 1 + j, (*chip, c), me).wait_recv()
                fwd = copy(a, 4 + j, (*chip, c), sibling)
                fwd.start()
                passed.append(fwd)
        for a in range(n):
            copy(a, 0, sibling, me).wait_recv()
            for j, chip in enumerate(chips):
                copy(a, 4 + j, (*chip, 1 - c), me).wait_recv()
        for cp in first + passed:
            cp.wait_send()
        for cp in mine:
            cp.wait()

    return pl.pallas_call(
        _after(body, n, deps), name="all_gather", in_specs=[ANY] * (n + nd), out_specs=[ANY] * n,
        out_shape=[jax.ShapeDtypeStruct((N_DEV,) + s.shape, s.dtype) for s in shards],
        scratch_shapes=[pltpu.SemaphoreType.DMA((n, 7)), pltpu.SemaphoreType.DMA((n, 7)), pltpu.SemaphoreType.DMA((n,))],
    )(*shards, *deps)


HBM = pl.BlockSpec(memory_space=pltpu.HBM)
SEM = pl.BlockSpec(memory_space=pltpu.SEMAPHORE)
_EFFECT = pltpu.SideEffectType.DATAFLOW_SIDE_EFFECTING


def _split_start(name, srcs, dsts, sem_shape, plan):
    ns, nb = len(srcs), len(srcs) + len(dsts)

    def body(*refs):
        send_sems, recv_sems = refs[nb], refs[nb + 1]
        for cp in plan(refs[:ns], refs[ns:nb], send_sems, recv_sems):
            cp.start()
        refs[-1][...] = jnp.zeros_like(refs[-1])

    bufs = list(srcs) + list(dsts)
    return pl.pallas_call(
        body, name=name,
        out_shape=(pltpu.SemaphoreType.DMA(sem_shape), pltpu.SemaphoreType.DMA(sem_shape),
                   *[pltpu.HBM(a.shape, a.dtype) for a in bufs], jax.ShapeDtypeStruct((8, 128), f32)),
        in_specs=[HBM] * nb, out_specs=(SEM, SEM, *[HBM] * nb, pl.BlockSpec(memory_space=pltpu.VMEM)),
        input_output_aliases={i: 2 + i for i in range(nb)},
        compiler_params=pltpu.CompilerParams(has_side_effects=_EFFECT),
    )(*[pltpu.with_memory_space_constraint(a, pltpu.HBM) for a in bufs])


def _split_wait(name, started, ns, plan, after):
    send_sems, recv_sems = started[0], started[1]
    bufs = list(started[2:-1])
    nb = len(bufs)

    def body(*refs):
        for cp in plan(refs[:ns], refs[ns:nb], refs[nb], refs[nb + 1]):
            cp.wait_send()
            cp.wait_recv()

    return pl.pallas_call(
        body, name=name, out_shape=tuple(pltpu.HBM(a.shape, a.dtype) for a in bufs),
        in_specs=[HBM] * nb + [SEM, SEM, ANY], out_specs=tuple([HBM] * nb),
        input_output_aliases={i: i for i in range(nb)},
        compiler_params=pltpu.CompilerParams(has_side_effects=_EFFECT),
    )(*bufs, send_sems, recv_sems, after)


def _remote(src, dst, send_sem, recv_sem, to):
    return pltpu.make_async_remote_copy(src_ref=src, dst_ref=dst, send_sem=send_sem, recv_sem=recv_sem,
                                        device_id=to, device_id_type=MESH)


def _gather_plan(src, dst, send_sems, recv_sems):
    x, y, c = _place()
    peers = [(x, y, 1 - c), (1 - x, y, c), (x, 1 - y, c), (1 - x, 1 - y, c)]
    copies = []
    for a in range(len(dst)):
        rows = dst[a].at[4 * x + 2 * y + c]
        copies += [_remote(rows, rows, send_sems.at[4 * a + k], recv_sems.at[4 * a + k], peer) for k, peer in enumerate(peers)]
    return copies


def _pair_plan(src, dst, send_sems, recv_sems):
    x, y, c = _place()
    return [_remote(src[a].at[2 * b + (1 - c)], dst[a].at[b], send_sems.at[4 * a + b], recv_sems.at[4 * a + b], (x, y, 1 - c))
            for a in range(len(src)) for b in range(4)]


def _chips_plan(src, dst, send_sems, recv_sems):
    x, y, c = _place()
    chips = [(1 - x, y), (x, 1 - y), (1 - x, 1 - y)]
    return [_remote(src[a].at[2 * px + py], dst[a].at[j], send_sems.at[3 * a + j], recv_sems.at[3 * a + j], (px, py, c))
            for a in range(len(src)) for j, (px, py) in enumerate(chips)]


def _gather_finish(bufs):
    n = len(bufs)

    def body(*refs):
        dst = refs[n:2 * n]
        send_sems, recv_sems = refs[2 * n:]
        x, y, c = _place()
        chips = [(1 - x, y), (x, 1 - y), (1 - x, 1 - y)]
        passed = []
        for a in range(n):
            for j, (px, py) in enumerate(chips):
                rows = dst[a].at[4 * px + 2 * py + c]
                passed.append(_remote(rows, rows, send_sems.at[a, j], recv_sems.at[a, j], (x, y, 1 - c)))
        for cp in passed:
            cp.start()
        for cp in passed:
            cp.wait_send()
        for a in range(n):
            for j, (px, py) in enumerate(chips):
                rows = dst[a].at[4 * px + 2 * py + (1 - c)]
                _remote(rows, rows, send_sems.at[a, j], recv_sems.at[a, j], (x, y, 1 - c)).wait_recv()

    return pl.pallas_call(
        body, name="gather_finish", in_specs=[ANY] * n, out_specs=[ANY] * n,
        out_shape=[jax.ShapeDtypeStruct(b.shape, b.dtype) for b in bufs],
        input_output_aliases={a: a for a in range(n)},
        scratch_shapes=[pltpu.SemaphoreType.DMA((n, 3)), pltpu.SemaphoreType.DMA((n, 3))],
    )(*bufs)


def _place_shards(mats, l, dev):
    n = len(mats)

    def body(dev_ref, *refs):
        for a in range(n):
            refs[n + a][...] = refs[a][...].astype(bf16)

    return pl.pallas_call(
        body, name="place_shards",
        grid_spec=pltpu.PrefetchScalarGridSpec(
            num_scalar_prefetch=1, grid=(1,),
            in_specs=[pl.BlockSpec((None,) + m.shape[1:], lambda i, dv: (l, 0, 0)) for m in mats],
            out_specs=[pl.BlockSpec((None, None) + m.shape[1:], lambda i, dv: (dv[0], 0, 0, 0)) for m in mats]),
        out_shape=[jax.ShapeDtypeStruct((N_DEV, 1) + m.shape[1:], bf16) for m in mats],
        compiler_params=_params(("arbitrary",)),
    )(dev, *mats)


BIG = ("ffn1_w_gu", "ffn1_w_down", "mix_w_in", "mix_w_out", "ffn2_w_gu", "ffn2_w_down")
SHARDED_CONV = ("lru_conv_w", "ssd_conv_w")
REPLICATED = ("ffn1_pre_g", "ffn1_post_g", "mix_pre_g", "mix_post_g", "lru_conv_b", "lru_w_r", "lru_b_r", "lru_w_i",
              "lru_b_i", "lru_lambda", "ssd_conv_b", "ssd_dt_bias", "ssd_a_log", "ssd_d", "ssd_norm_g", "sgu_ln_g",
              "sgu_ln_b", "sgu_w_s", "sgu_b_s", "ffn2_pre_g", "ffn2_post_g")
WEIGHTS = ("ffn1_pre_g", "ffn1_post_g", "ffn1_w_gu", "ffn1_w_down", "mix_pre_g", "mix_post_g", "mix_w_in", "mix_w_out",
           "lru_conv_w", "lru_conv_b", "lru_w_r", "lru_b_r", "lru_w_i", "lru_b_i", "lru_lambda", "ssd_conv_w",
           "ssd_conv_b", "ssd_dt_bias", "ssd_a_log", "ssd_d", "ssd_norm_g", "sgu_ln_g", "sgu_ln_b", "sgu_w_s", "sgu_b_s",
           "ffn2_pre_g", "ffn2_post_g", "ffn2_w_gu", "ffn2_w_down")
DT_LO = PA_W + B_W + XBC_W
N_HEADS = B_W // HEAD
PACK_COLS = 1024


def _pack(arrays):
    flat = jnp.concatenate([a.reshape(-1) for a in arrays])
    rows = -(-flat.shape[0] // (8 * PACK_COLS)) * 8
    return jnp.pad(flat, (0, rows * PACK_COLS - flat.shape[0])).reshape(rows, PACK_COLS)


def _unpack(packed, shapes):
    flat = packed.reshape(-1)
    out, off = [], 0
    for s in shapes:
        size = 1
        for dim in s:
            size *= dim
        out.append(flat[off:off + size].reshape(s))
        off += size
    return out


def _widen_w_in(w):
    return jnp.concatenate([w[..., :DT_LO], jnp.repeat(w[..., DT_LO:DT_LO + N_HEADS], HEAD, axis=-1),
                            w[..., DT_LO + N_HEADS:]], axis=-1)


def _narrow_w_in_grad(g):
    dt = g[..., DT_LO:DT_LO + B_W]
    dt = dt.reshape(dt.shape[:-1] + (N_HEADS, HEAD)).sum(-1)
    return jnp.concatenate([g[..., :DT_LO], dt, g[..., DT_LO + B_W:]], axis=-1)


def _per_head(a):
    return a.reshape(a.shape[:-1] + (N_HEADS, HEAD)).sum(-1)


def kernel(x, ffn1_pre_g, ffn1_post_g, ffn1_w_gu, ffn1_w_down, mix_pre_g, mix_post_g, mix_w_in, mix_w_out, lru_conv_w, lru_conv_b, lru_w_r, lru_b_r, lru_w_i, lru_b_i, lru_lambda, ssd_conv_w, ssd_conv_b, ssd_dt_bias, ssd_a_log, ssd_d, ssd_norm_g, sgu_ln_g, sgu_ln_b, sgu_w_s, sgu_b_s, ffn2_pre_g, ffn2_post_g, ffn2_w_gu, ffn2_w_down, loss_target, m_ffn1_pre_g, m_ffn1_post_g, m_ffn1_w_gu, m_ffn1_w_down, m_mix_pre_g, m_mix_post_g, m_mix_w_in, m_mix_w_out, m_lru_conv_w, m_lru_conv_b, m_lru_w_r, m_lru_b_r, m_lru_w_i, m_lru_b_i, m_lru_lambda, m_ssd_conv_w, m_ssd_conv_b, m_ssd_dt_bias, m_ssd_a_log, m_ssd_d, m_ssd_norm_g, m_sgu_ln_g, m_sgu_ln_b, m_sgu_w_s, m_sgu_b_s, m_ffn2_pre_g, m_ffn2_post_g, m_ffn2_w_gu, m_ffn2_w_down, v_ffn1_pre_g, v_ffn1_post_g, v_ffn1_w_gu, v_ffn1_w_down, v_mix_pre_g, v_mix_post_g, v_mix_w_in, v_mix_w_out, v_lru_conv_w, v_lru_conv_b, v_lru_w_r, v_lru_b_r, v_lru_w_i, v_lru_b_i, v_lru_lambda, v_ssd_conv_w, v_ssd_conv_b, v_ssd_dt_bias, v_ssd_a_log, v_ssd_d, v_ssd_norm_g, v_sgu_ln_g, v_sgu_ln_b, v_sgu_w_s, v_sgu_b_s, v_ffn2_pre_g, v_ffn2_post_g, v_ffn2_w_gu, v_ffn2_w_down):
    given = dict(locals())
    w = {n: given[n] for n in WEIGHTS}
    mom = {n: given["m_" + n] for n in WEIGHTS}
    var = {n: given["v_" + n] for n in WEIGHTS}
    nl = ffn1_pre_g.shape[0]
    _, t, d = x.shape
    xi, yi, ci = _place()
    dev = 4 * xi + 2 * yi + ci
    c_dev = jnp.reshape(ci, (1,)).astype(jnp.int32)
    b_dev = jnp.reshape(2 * xi + yi, (1,)).astype(jnp.int32)

    conv_shapes = [lru_conv_w.shape, ssd_conv_w.shape]
    shards = [ffn1_w_gu, ffn1_w_down, _widen_w_in(mix_w_in), mix_w_out, ffn2_w_gu, ffn2_w_down]
    nbig = len(shards)
    dev_arr = jnp.reshape(dev, (1,)).astype(jnp.int32)
    conv_pack = _pack([lru_conv_w, ssd_conv_w])
    conv_buf = lax.dynamic_update_slice_in_dim(jnp.zeros((N_DEV,) + conv_pack.shape, f32), conv_pack[None], dev, axis=0)
    def gather_groups(l):
        return [(0, 1), (2, 3), (4, 5)] if l == 0 else [tuple(range(nbig))]

    gather_started = {}
    for l in range(nl):
        for gi, idx in enumerate(gather_groups(l)):
            bufs = list(_place_shards([shards[i] for i in idx], l, dev_arr)) + ([conv_buf] if (l, gi) == (0, 1) else [])
            gather_started[l, gi] = _split_start(f"gather_start_{l}_{gi}", [], bufs, (4 * len(bufs),), _gather_plan)

    def finish_gather(l, gi, after):
        waited = _split_wait(f"gather_wait_{l}_{gi}", gather_started[l, gi], 0, _gather_plan, after)
        return _gather_finish(list(waited))

    def conv_taps(conv_all):
        full = []
        for k, shape in enumerate(conv_shapes):
            per_dev = jnp.stack([_unpack(conv_all[s], conv_shapes)[k] for s in range(N_DEV)], axis=2)
            full.append(per_dev.reshape(shape[0], shape[1], N_DEV * shape[2]))
        return full

    def vec(a):
        return a.reshape(nl, 1, -1)

    def per_channel(a):
        return jnp.repeat(a, HEAD, axis=-1).reshape(nl, 1, B_W)

    eye = jnp.eye(A_W // HEAD, dtype=f32)

    def block_diag(a):
        return jnp.einsum("lhij,hg->lhigj", a, eye).reshape(nl, A_W, A_W).astype(bf16)

    causal = jnp.tril(jnp.ones((CHUNK, CHUNK), dtype=bool))
    p = dict(
        ffn1_pre=vec(ffn1_pre_g), ffn1_post=vec(ffn1_post_g), mix_pre=vec(mix_pre_g), mix_post=vec(mix_post_g),
        ffn2_pre=vec(ffn2_pre_g), ffn2_post=vec(ffn2_post_g),
        lru=(vec(lru_conv_b), block_diag(lru_w_r), block_diag(lru_w_i), vec(lru_b_r), vec(lru_b_i), vec(lru_lambda)),
        ssd=(vec(ssd_conv_b), per_channel(ssd_dt_bias), per_channel(ssd_a_log), per_channel(ssd_d), vec(ssd_norm_g)),
    )
    wm = jnp.where(causal, sgu_w_s, 0.0).astype(bf16)
    sgu_bias = jnp.repeat(jnp.swapaxes(sgu_b_s, 1, 2), HEAD, axis=2)
    sgu_f = (vec(sgu_ln_g), vec(sgu_ln_b), wm, sgu_bias)
    sgu_b = (vec(sgu_ln_g), vec(sgu_ln_b), wm, jnp.swapaxes(wm, 2, 3), sgu_bias)

    xs = x.reshape(t, d)
    saved, gathered = [], []
    for l in range(nl):
        x0 = xs
        if l == 0:
            wgu1, wd1 = finish_gather(0, 0, x0)
            deps = tuple(started[-1] for key, started in gather_started.items() if key != (0, 0))
        else:
            wgu1, wd1, win, wout, wgu2, wd2 = finish_gather(l, 0, x0)
            deps = ()
        x1, hb1, g1, u1, f1 = _ffn_fwd(x0, p["ffn1_pre"], p["ffn1_post"], wgu1, wd1, l, deps)
        if l == 0:
            win, wout, conv_all = finish_gather(0, 1, x1)
            lru_cw, ssd_cw = conv_taps(conv_all)
            p["lru"], p["ssd"] = (lru_cw,) + p["lru"], (ssd_cw,) + p["ssd"]
        hbm, pa, pb, pc = _mix_in_fwd(x1, p["mix_pre"], win, l)
        ya, h = _lru_fwd(pa, *p["lru"], l)
        yb, yp, sp = _ssd_fwd(pb, *p["ssd"], l)
        yc = _sgu_fwd(pc, *sgu_f, l)
        x2, cat, m = _mix_out_fwd(x1, ya, yb, yc, p["mix_post"], wout, l)
        if l == 0:
            wgu2, wd2 = finish_gather(0, 2, x2)
        xs, hb2, g2, u2, f2 = _ffn_fwd(x2, p["ffn2_pre"], p["ffn2_post"], wgu2, wd2, l)
        gathered.append((wgu1, wd1, win, wout, wgu2, wd2))
        saved.append((x0, hb1, g1, u1, f1, x1, hbm, pa, pb, pc, h, yp, sp, cat, m, x2, hb2, g2, u2, f2))
    dy, loss_part = _loss_fwd(xs, loss_target.reshape(t, d))
    loss = lax.psum(loss_part[0, 0], ("x", "y", "c"))

    small = {n: [None] * nl for n in REPLICATED + SHARDED_CONV}
    grads, delta, new_m, new_v = {}, {}, {}, {}
    fused = [n for n in BIG if n != "mix_w_in"]

    def oriented(a, n):
        return jnp.swapaxes(a, 1, 2) if n.endswith("w_gu") else a

    opt_in = {n: tuple(oriented(src[n], n) for src in (w, mom, var)) for n in fused}
    opt_out = {n: tuple(lax.empty(opt_in[n][0].shape, f32) for _ in range(4)) for n in fused}
    w_in_grads = [None] * nl
    grad_shapes = {n: (s.shape[2], s.shape[1]) if n.endswith("w_gu") else s.shape[1:] for n, s in zip(BIG, shards)}

    def start_pair(tag, lp, names, gbuf):
        landing = [lax.empty((4, 1) + grad_shapes[n], bf16) for n in names]
        started = _split_start(f"pair_start_{tag}", [gbuf[n] for n in names], landing, (4 * len(names),), _pair_plan)
        return tag, lp, names, started

    def finish_pair(pending, after):
        tag, lp, names, started = pending
        k = len(names)
        done = _split_wait(f"pair_wait_{tag}", started, k, _pair_plan, after)
        sums = [_pair_add(g, r, c_dev) for g, r in zip(done[:k], done[k:])]
        landing = [lax.empty((3,) + s.shape[1:], bf16) for s in sums]
        return tag, lp, names, _split_start(f"chips_start_{tag}", sums, landing, (3 * k,), _chips_plan)

    def finish_chips(pending, after):
        tag, lp, names, started = pending
        k = len(names)
        done = _split_wait(f"chips_wait_{tag}", started, k, _chips_plan, after)
        for n, s, q in zip(names, done[:k], done[k:]):
            if n == "mix_w_in":
                w_in_grads[lp] = _grad_sum(s, q, b_dev)
            else:
                opt_out[n] = tuple(_adamw_layer(*opt_in[n], s, q, b_dev, opt_out[n], lp))

    early = ("ffn2_w_gu", "ffn2_w_down", "mix_w_out")
    late = ("mix_w_in", "ffn1_w_gu", "ffn1_w_down")
    pending_pair = pending_chips = early_pair = early_chips = None
    for l in reversed(range(nl)):
        x0, hb1, g1, u1, f1, x1, hbm, pa, pb, pc, h, yp, sp, cat, m, x2, hb2, g2, u2, f2 = saved[l]
        wgu1, wd1, win, wout, wgu2, wd2 = gathered[l][:nbig]
        gbuf ={n: lax.empty((N_DEV, 1) + grad_shapes[n], bf16) for n in BIG}
        deps = () if pending_pair is None else (pending_pair[3][-1],)
        dx2, dfb, act, dg, du, dpre, dpost = _ffn_bwd(x2, dy, f2, p["ffn2_pre"], p["ffn2_post"], g2, u2, wgu2, wd2, l, deps)
        small["ffn2_pre_g"][l], small["ffn2_post_g"][l] = dpre[0], dpost[0]
        gbuf["ffn2_w_gu"] = _wgrad_cols(hb2, dg, gbuf["ffn2_w_gu"], 0, 0)
        gbuf["ffn2_w_gu"] = _wgrad_cols(hb2, du, gbuf["ffn2_w_gu"], 0, dg.shape[0])
        gbuf["ffn2_w_down"] = _wgrad_rows(act, dfb, gbuf["ffn2_w_down"], 0)
        deps = ()
        if pending_pair is not None:
            pending_chips = finish_pair(pending_pair, dx2)
            deps = (pending_chips[3][-1],)

        dm, dya, dyb, dyc, dpost = _mix_out_bwd(dx2, m, p["mix_post"], wout, l, deps)
        small["mix_post_g"][l] = dpost[0]
        gbuf["mix_w_out"] = _wgrad_kblocks(cat, [dm], gbuf["mix_w_out"], 0)
        deps = ()
        if l == 0:
            early_pair = start_pair("0a", 0, early, gbuf)
            deps = (early_pair[3][-1],)
        dpc, dws, dbias, dlg, dlb = _sgu_bwd(pc, dyc, *sgu_b, l, deps)
        small["sgu_w_s"][l] = jnp.where(causal, dws, 0.0)
        small["sgu_b_s"][l] = dbias.reshape(CHUNK, C_W // HEAD, HEAD).sum(-1).T
        small["sgu_ln_g"][l], small["sgu_ln_b"][l] = dlg[0], dlb[0]
        dpb, dcw, dcb, ddtb, dalog, ddsk, dng = _ssd_bwd(pb, yp, sp, dyb, *p["ssd"], l)
        small["ssd_conv_w"][l], small["ssd_conv_b"][l], small["ssd_norm_g"][l] = dcw, dcb[0], dng[0]
        small["ssd_dt_bias"][l], small["ssd_a_log"][l], small["ssd_d"][l] = _per_head(ddtb[0]), _per_head(dalog[0]), _per_head(ddsk[0])
        deps = ()
        if l == 0:
            early_chips = finish_pair(early_pair, dpb)
            deps = (early_chips[3][-1],)
        dpa, dcw, dcb, dwr, dwi, dbr, dbi, dlam = _lru_bwd(pa, h, dya, *p["lru"], l, deps)
        small["lru_conv_w"][l], small["lru_conv_b"][l], small["lru_lambda"][l] = dcw, dcb[0], dlam[0]
        small["lru_b_r"][l], small["lru_b_i"][l] = dbr[0], dbi[0]
        heads = range(A_W // HEAD)
        small["lru_w_r"][l] = jnp.stack([dwr[HEAD * i:HEAD * (i + 1), HEAD * i:HEAD * (i + 1)] for i in heads])
        small["lru_w_i"][l] = jnp.stack([dwi[HEAD * i:HEAD * (i + 1), HEAD * i:HEAD * (i + 1)] for i in heads])
        dx1, dpre = _mix_in_bwd(x1, dx2, p["mix_pre"], dpa, dpb, dpc, win, l)
        small["mix_pre_g"][l] = dpre[0]
        gbuf["mix_w_in"] = _wgrad_kblocks(hbm, [dpa, dpb, dpc], gbuf["mix_w_in"], 0)

        dy, dfb, act, dg, du, dpre, dpost = _ffn_bwd(x0, dx1, f1, p["ffn1_pre"], p["ffn1_post"], g1, u1, wgu1, wd1, l)
        small["ffn1_pre_g"][l], small["ffn1_post_g"][l] = dpre[0], dpost[0]
        gbuf["ffn1_w_gu"] = _wgrad_cols(hb1, dg, gbuf["ffn1_w_gu"], 0, 0)
        gbuf["ffn1_w_gu"] = _wgrad_cols(hb1, du, gbuf["ffn1_w_gu"], 0, dg.shape[0])
        gbuf["ffn1_w_down"] = _wgrad_rows(act, dfb, gbuf["ffn1_w_down"], 0)
        if pending_chips is not None:
            finish_chips(pending_chips, dy)
        pending_pair = start_pair(f"{l}", l, late if l == 0 else BIG, gbuf)
    grad_x = dy.reshape(x.shape)

    names = REPLICATED + SHARDED_CONV
    stacked = [jnp.stack(small[n]) for n in names]
    total = _sum_devices(_all_gather([_pack(stacked)], (pending_pair[3][-1],))[0])
    finish_chips(early_chips, total)
    finish_chips(finish_pair(pending_pair, total), total)

    for n in fused:
        grads[n], delta[n], new_m[n], new_v[n] = (oriented(a, n) for a in opt_out[n])
    grads["mix_w_in"] = _narrow_w_in_grad(jnp.concatenate(w_in_grads, axis=0))
    delta["mix_w_in"], new_m["mix_w_in"], new_v["mix_w_in"] = _adamw(
        w["mix_w_in"], mom["mix_w_in"], var["mix_w_in"], grads["mix_w_in"])
    full = dict(zip(names, _unpack(total, [a.shape for a in stacked])))
    for n in REPLICATED:
        grads[n] = full[n]
    for n in SHARDED_CONV:
        cols = w[n].shape[2]
        grads[n] = lax.dynamic_slice_in_dim(full[n], dev * cols, cols, axis=2)
    shapes = [w[n].shape for n in names]
    packs = [_pack([src[n] for n in names])[None] for src in (w, mom, var, grads)]
    for dst, packed in zip((delta, new_m, new_v), _adamw(*packs)):
        dst.update(zip(names, _unpack(packed[0], shapes)))

    return (loss, grad_x, *[grads[n] for n in WEIGHTS], *[delta[n] for n in WEIGHTS],
            *[new_m[n] for n in WEIGHTS], *[new_v[n] for n in WEIGHTS])
```

```python
import functools

import jax
import jax.numpy as jnp
from jax import lax
from jax.experimental import pallas as pl
from jax.experimental.pallas import tpu as pltpu

f32, bf16 = jnp.float32, jnp.bfloat16
MESH = pl.DeviceIdType.MESH
ANY = pl.BlockSpec(memory_space=pl.ANY)

N_DEV = 8
NORM_EPS = 1e-6
LRU_C = 8.0
CHUNK = 128
HEAD = 64
A_W, B_W, C_W = 384, 384, 256
B_STATE = 128
XBC_W = B_W + 4 * B_STATE
PA_W, PB_W, PC_W = 2 * A_W, B_W + XBC_W + B_W, 2 * C_W
IN_PAD = PA_W + PB_W + PC_W
ADAM_LR, ADAM_B1, ADAM_B2, ADAM_EPS, ADAM_WD, ADAM_STEP = 0.001, 0.9, 0.999, 1e-08, 0.01, 10
VMEM_LIMIT_BYTES = 56 * 1024 * 1024
FFN_BWD_SPLIT = 2
NEG_BIG = -1e30


def _params(sem=None):
    return pltpu.CompilerParams(dimension_semantics=sem, vmem_limit_bytes=VMEM_LIMIT_BYTES)


def _nn(a, b):
    return jnp.dot(a, b, preferred_element_type=f32)


def _nt(a, b):
    return lax.dot_general(a, b, (((1,), (1,)), ((), ())), preferred_element_type=f32)


def _tn(a, b):
    return lax.dot_general(a, b, (((0,), (0,)), ((), ())), preferred_element_type=f32)


def _sigmoid(x):
    return 0.5 * jnp.tanh(0.5 * x) + 0.5


def _softplus(x):
    return jnp.maximum(x, 0.0) + jnp.log(1.0 + jnp.exp(-jnp.abs(x)))


_GELU_C0, _GELU_C1 = 0.7978845608028654, 0.044715


def _gelu(x):
    t = jnp.tanh(_GELU_C0 * (x + _GELU_C1 * x * x * x))
    return 0.5 * x * (1.0 + t)


def _gelu_grad(x):
    t = jnp.tanh(_GELU_C0 * (x + _GELU_C1 * x * x * x))
    return 0.5 * (1.0 + t) + 0.5 * x * (1.0 - t * t) * _GELU_C0 * (1.0 + 3.0 * _GELU_C1 * x * x)


def _silu_grad(x, s):
    return s * (1.0 + x * (1.0 - s))


def _rms_fwd(x, g):
    r = lax.rsqrt(jnp.mean(x * x, axis=-1, keepdims=True) + NORM_EPS)
    return x * r * g


def _rms_bwd(x, g, dy):
    r = lax.rsqrt(jnp.mean(x * x, axis=-1, keepdims=True) + NORM_EPS)
    xh = x * r
    dxh = dy * g
    dx = r * (dxh - xh * jnp.mean(dxh * xh, axis=-1, keepdims=True))
    return dx, jnp.sum(dy * xh, axis=0, keepdims=True)


def _one_minus_exp(x):
    series = -x * (1.0 + x * (0.5 + x * (1.0 / 6.0 + x * (1.0 / 24.0))))
    return jnp.where(x > -0.01, series, 1.0 - jnp.exp(x))


def _cumsum_rows(x):
    row = lax.broadcasted_iota(jnp.int32, x.shape, 0)
    d = 1
    while d < x.shape[0]:
        x = x + jnp.where(row >= d, pltpu.roll(x, d, 0), 0.0)
        d *= 2
    return x


def _tile(t, cap):
    tm = min(cap, t)
    assert t % tm == 0
    return tm


def _after(body, n_in, deps):
    def wrapped(*refs):
        return body(*refs[:n_in], *refs[n_in + len(deps):])
    return wrapped


def _lspec(a, l):
    return pl.BlockSpec((None,) + a.shape[1:], lambda *_: (l,) + (0,) * (a.ndim - 1))


def _wd_rows(wd_ref):
    return wd_ref[:, 0].reshape(2 * wd_ref.shape[2], wd_ref.shape[3])


def _ffn_fwd(x, pre_g, post_g, wgu, wd, l, deps=()):
    t, d = x.shape
    nb, _, _, h = wgu.shape
    nj = nb // 2
    tm = _tile(t, 512)

    def body(x_ref, pg_ref, qg_ref, wg_ref, wu_ref, wd_ref, y_ref, hb_ref, g_ref, u_ref, f_ref, acc_ref):
        j = pl.program_id(1)

        @pl.when(j == 0)
        def _():
            hb_ref[...] = _rms_fwd(x_ref[...], pg_ref[...]).astype(bf16)

        hb = hb_ref[...]
        g = _nn(hb, wg_ref[0, 0])
        u = _nn(hb, wu_ref[0, 0])
        g_ref[0] = g.astype(bf16)
        u_ref[0] = u.astype(bf16)
        a = (g * _sigmoid(g) * u).astype(bf16)
        part = _nn(a, _wd_rows(wd_ref))

        @pl.when(j == 0)
        def _():
            acc_ref[...] = part

        @pl.when(j > 0)
        def _():
            acc_ref[...] += part

        @pl.when(j == nj - 1)
        def _():
            f = acc_ref[...]
            f_ref[...] = f
            y_ref[...] = x_ref[...] + 0.5 * _rms_fwd(f, qg_ref[...])

    row = pl.BlockSpec((tm, d), lambda i, j: (i, 0))
    vec = pl.BlockSpec((1, d), lambda i, j: (0, 0))
    act = pl.BlockSpec((1, tm, h), lambda i, j: (j, i, 0))
    return pl.pallas_call(
        _after(body, 6, deps), name="ffn_fwd", grid=(t // tm, nj),
        in_specs=[row, _lspec(pre_g, l), _lspec(post_g, l),
                  pl.BlockSpec((1, 1, d, h), lambda i, j: (j, 0, 0, 0)),
                  pl.BlockSpec((1, 1, d, h), lambda i, j: (j + nj, 0, 0, 0)),
                  pl.BlockSpec((2, 1, h // 2, d), lambda i, j: (j, 0, 0, 0))] + [ANY] * len(deps),
        out_specs=[row, row, act, act, row],
        out_shape=[jax.ShapeDtypeStruct((t, d), f32), jax.ShapeDtypeStruct((t, d), bf16),
                   jax.ShapeDtypeStruct((nj, t, h), bf16), jax.ShapeDtypeStruct((nj, t, h), bf16),
                   jax.ShapeDtypeStruct((t, d), f32)],
        scratch_shapes=[pltpu.VMEM((tm, d), f32)],
        compiler_params=_params(("arbitrary", "arbitrary")),
    )(x, pre_g, post_g, wgu, wgu, wd, *deps)


def _ffn_bwd(x, dy, f, pre_g, post_g, g, u, wgu, wd, l, deps=()):
    t, d = x.shape
    nj, _, h = g.shape
    tm = _tile(t, 512)

    def body(x_ref, dy_ref, f_ref, pg_ref, qg_ref, g_ref, u_ref, wg_ref, wu_ref, wd_ref,
             dx_ref, dfb_ref, a_ref, dg_ref, du_ref, dpg_ref, dqg_ref, dh_ref):
        i, j = pl.program_id(0), pl.program_id(1)

        @pl.when((i == 0) & (j == 0))
        def _():
            dpg_ref[...] = jnp.zeros_like(dpg_ref)
            dqg_ref[...] = jnp.zeros_like(dqg_ref)

        @pl.when(j == 0)
        def _():
            df, dq = _rms_bwd(f_ref[...], qg_ref[...], 0.5 * dy_ref[...])
            dfb_ref[...] = df.astype(bf16)
            dqg_ref[...] += dq
            dh_ref[...] = jnp.zeros_like(dh_ref)

        wdm, wg, wu = _wd_rows(wd_ref), wg_ref[0, 0], wu_ref[0, 0]
        for half in range(FFN_BWD_SPLIT):
            rows = pl.ds(half * (tm // FFN_BWD_SPLIT), tm // FFN_BWD_SPLIT)
            da = _nt(dfb_ref[rows, :], wdm)
            gv = g_ref[0, rows, :].astype(f32)
            uv = u_ref[0, rows, :].astype(f32)
            s = _sigmoid(gv)
            sg = gv * s
            a_ref[0, rows, :] = (sg * uv).astype(bf16)
            dg = (da * uv * _silu_grad(gv, s)).astype(bf16)
            du = (da * sg).astype(bf16)
            dg_ref[0, rows, :] = dg
            du_ref[0, rows, :] = du
            dh_ref[rows, :] += _nt(dg, wg) + _nt(du, wu)

        @pl.when(j == nj - 1)
        def _():
            dxn, dp = _rms_bwd(x_ref[...], pg_ref[...], dh_ref[...])
            dx_ref[...] = dy_ref[...] + dxn
            dpg_ref[...] += dp

    row = pl.BlockSpec((tm, d), lambda i, j: (i, 0))
    vec = pl.BlockSpec((1, d), lambda i, j: (0, 0))
    act = pl.BlockSpec((1, tm, h), lambda i, j: (j, i, 0))
    act_shape = jax.ShapeDtypeStruct((nj, t, h), bf16)
    return pl.pallas_call(
        _after(body, 10, deps), name="ffn_bwd", grid=(t // tm, nj),
        in_specs=[row, row, row, _lspec(pre_g, l), _lspec(post_g, l), act, act,
                  pl.BlockSpec((1, 1, d, h), lambda i, j: (j, 0, 0, 0)),
                  pl.BlockSpec((1, 1, d, h), lambda i, j: (j + nj, 0, 0, 0)),
                  pl.BlockSpec((2, 1, h // 2, d), lambda i, j: (j, 0, 0, 0))] + [ANY] * len(deps),
        out_specs=[row, row, act, act, act, vec, vec],
        out_shape=[jax.ShapeDtypeStruct((t, d), f32), jax.ShapeDtypeStruct((t, d), bf16),
                   act_shape, act_shape, act_shape,
                   jax.ShapeDtypeStruct((1, d), f32), jax.ShapeDtypeStruct((1, d), f32)],
        scratch_shapes=[pltpu.VMEM((tm, d), f32)],
        compiler_params=_params(("arbitrary", "arbitrary")),
    )(x, dy, f, pre_g, post_g, g, u, wgu, wgu, wd, *deps)


def _wgrad_cols(x, dy, buf, l, slot0):
    (t, k), (nj, _, n) = x.shape, dy.shape

    def body(x_ref, dy_ref, buf_ref, o_ref):
        o_ref[0, 0] = _tn(dy_ref[0], x_ref[...]).astype(bf16)

    return pl.pallas_call(
        body, name="wgrad_cols", grid=(nj,),
        in_specs=[pl.BlockSpec((t, k), lambda b: (0, 0)), pl.BlockSpec((1, t, n), lambda b: (b, 0, 0)), ANY],
        out_specs=pl.BlockSpec((1, 1, n, k), lambda b: (b + slot0, l, 0, 0)),
        out_shape=jax.ShapeDtypeStruct(buf.shape, bf16), input_output_aliases={2: 0},
        compiler_params=_params(("arbitrary",)),
    )(x, dy, buf)


def _wgrad_rows(x, dy, buf, l):
    (nj, t, k), (_, n) = x.shape, dy.shape

    def body(x_ref, dy_ref, buf_ref, o_ref):
        o_ref[:, 0] = _tn(x_ref[0], dy_ref[...]).astype(bf16).reshape(2, k // 2, n)

    return pl.pallas_call(
        body, name="wgrad_rows", grid=(nj,),
        in_specs=[pl.BlockSpec((1, t, k), lambda b: (b, 0, 0)), pl.BlockSpec((t, n), lambda b: (0, 0)), ANY],
        out_specs=pl.BlockSpec((2, 1, k // 2, n), lambda b: (b, l, 0, 0)),
        out_shape=jax.ShapeDtypeStruct(buf.shape, bf16), input_output_aliases={2: 0},
        compiler_params=_params(("arbitrary",)),
    )(x, dy, buf)


def _wgrad_kblocks(x, dys, buf, l):
    t, k = x.shape
    kb = k // N_DEV
    widths = [dy.shape[1] for dy in dys]
    n = sum(widths)
    nd = len(dys)

    def body(x_ref, *refs):
        dy_hbm, o_ref, dy_vmem = refs[:nd], refs[nd + 1], refs[nd + 2:]

        @pl.when(pl.program_id(0) == 0)
        def _():
            for src, dst in zip(dy_hbm, dy_vmem):
                pltpu.sync_copy(src, dst)

        off = 0
        for dst, w in zip(dy_vmem, widths):
            o_ref[0, 0, :, off:off + w] = _tn(x_ref[...], dst[...]).astype(bf16)
            off += w

    return pl.pallas_call(
        body, name="wgrad_kblocks", grid=(N_DEV,),
        in_specs=[pl.BlockSpec((t, kb), lambda s: (0, s))] + [ANY] * (nd + 1),
        out_specs=pl.BlockSpec((1, 1, kb, n), lambda s: (s, l, 0, 0)),
        out_shape=jax.ShapeDtypeStruct(buf.shape, bf16), input_output_aliases={nd + 1: 0},
        scratch_shapes=[pltpu.VMEM((t, w), bf16) for w in widths],
        compiler_params=_params(("arbitrary",)),
    )(x, *dys, buf)


def _gathered_rows(w_ref, lo, hi):
    return w_ref[:, 0, :, lo:hi].reshape(N_DEV * w_ref.shape[2], hi - lo)


def _gathered_spec(w):
    return pl.BlockSpec((N_DEV, 1) + w.shape[2:], lambda i: (0, 0, 0, 0))


def _mix_in_fwd(x, pre_g, w_in, l):
    t, d = x.shape
    tm = _tile(t, 512)

    def body(x_ref, g_ref, w_ref, hb_ref, pa_ref, pb_ref, pc_ref):
        hb = _rms_fwd(x_ref[...], g_ref[...]).astype(bf16)
        hb_ref[...] = hb
        pa_ref[...] = _nn(hb, _gathered_rows(w_ref, 0, PA_W))
        pb_ref[...] = _nn(hb, _gathered_rows(w_ref, PA_W, PA_W + PB_W))
        pc_ref[...] = _nn(hb, _gathered_rows(w_ref, PA_W + PB_W, IN_PAD))

    def row(w):
        return pl.BlockSpec((tm, w), lambda i: (i, 0))

    return pl.pallas_call(
        body, name="mix_in_fwd", grid=(t // tm,),
        in_specs=[row(d), _lspec(pre_g, l), _gathered_spec(w_in)],
        out_specs=[row(d), row(PA_W), row(PB_W), row(PC_W)],
        out_shape=[jax.ShapeDtypeStruct((t, d), bf16), jax.ShapeDtypeStruct((t, PA_W), f32),
                   jax.ShapeDtypeStruct((t, PB_W), f32), jax.ShapeDtypeStruct((t, PC_W), f32)],
        compiler_params=_params(("arbitrary",)),
    )(x, pre_g, w_in)


def _mix_in_bwd(x, dy, pre_g, dpa, dpb, dpc, w_in, l):
    t, d = x.shape
    tm = _tile(t, 512)

    def body(x_ref, dy_ref, g_ref, dpa_ref, dpb_ref, dpc_ref, w_ref, dx_ref, dg_ref):
        @pl.when(pl.program_id(0) == 0)
        def _():
            dg_ref[...] = jnp.zeros_like(dg_ref)

        dh = (_nt(dpa_ref[...], _gathered_rows(w_ref, 0, PA_W))
              + _nt(dpb_ref[...], _gathered_rows(w_ref, PA_W, PA_W + PB_W))
              + _nt(dpc_ref[...], _gathered_rows(w_ref, PA_W + PB_W, IN_PAD)))
        dxn, dg = _rms_bwd(x_ref[...], g_ref[...], dh)
        dx_ref[...] = dy_ref[...] + dxn
        dg_ref[...] += dg

    def row(w):
        return pl.BlockSpec((tm, w), lambda i: (i, 0))

    vec = pl.BlockSpec((1, d), lambda i: (0, 0))
    return pl.pallas_call(
        body, name="mix_in_bwd", grid=(t // tm,),
        in_specs=[row(d), row(d), _lspec(pre_g, l), row(PA_W), row(PB_W), row(PC_W), _gathered_spec(w_in)],
        out_specs=[row(d), vec],
        out_shape=[jax.ShapeDtypeStruct((t, d), f32), jax.ShapeDtypeStruct((1, d), f32)],
        compiler_params=_params(("arbitrary",)),
    )(x, dy, pre_g, dpa, dpb, dpc, w_in)


def _mix_out_fwd(x, ya, yb, yc, post_g, w_out, l):
    t, d = x.shape
    tm = _tile(t, 512)

    def body(x_ref, ya_ref, yb_ref, yc_ref, g_ref, w_ref, y_ref, cat_ref, m_ref):
        cat_ref[:, 0:A_W] = ya_ref[...].astype(bf16)
        cat_ref[:, A_W:A_W + B_W] = yb_ref[...].astype(bf16)
        cat_ref[:, A_W + B_W:d] = yc_ref[...].astype(bf16)
        m = _nn(cat_ref[...], _gathered_rows(w_ref, 0, d))
        m_ref[...] = m
        y_ref[...] = x_ref[...] + _rms_fwd(m, g_ref[...])

    def row(w):
        return pl.BlockSpec((tm, w), lambda i: (i, 0))

    return pl.pallas_call(
        body, name="mix_out_fwd", grid=(t // tm,),
        in_specs=[row(d), row(A_W), row(B_W), row(C_W), _lspec(post_g, l), _gathered_spec(w_out)],
        out_specs=[row(d), row(d), row(d)],
        out_shape=[jax.ShapeDtypeStruct((t, d), f32), jax.ShapeDtypeStruct((t, d), bf16), jax.ShapeDtypeStruct((t, d), f32)],
        compiler_params=_params(("arbitrary",)),
    )(x, ya, yb, yc, post_g, w_out)


def _mix_out_bwd(dy, m, post_g, w_out, l, deps=()):
    t, d = m.shape
    tm = _tile(t, 512)

    def body(dy_ref, m_ref, g_ref, w_ref, dm_ref, dya_ref, dyb_ref, dyc_ref, dg_ref):
        @pl.when(pl.program_id(0) == 0)
        def _():
            dg_ref[...] = jnp.zeros_like(dg_ref)

        dm, dg = _rms_bwd(m_ref[...], g_ref[...], dy_ref[...])
        dmb = dm.astype(bf16)
        dm_ref[...] = dmb
        dg_ref[...] += dg
        dcat = _nt(dmb, _gathered_rows(w_ref, 0, d))
        dya_ref[...] = dcat[:, 0:A_W]
        dyb_ref[...] = dcat[:, A_W:A_W + B_W]
        dyc_ref[...] = dcat[:, A_W + B_W:d]

    def row(w):
        return pl.BlockSpec((tm, w), lambda i: (i, 0))

    vec = pl.BlockSpec((1, d), lambda i: (0, 0))
    return pl.pallas_call(
        _after(body, 4, deps), name="mix_out_bwd", grid=(t // tm,),
        in_specs=[row(d), row(d), _lspec(post_g, l), _gathered_spec(w_out)] + [ANY] * len(deps),
        out_specs=[row(d), row(A_W), row(B_W), row(C_W), vec],
        out_shape=[jax.ShapeDtypeStruct((t, d), bf16), jax.ShapeDtypeStruct((t, A_W), f32),
                   jax.ShapeDtypeStruct((t, B_W), f32), jax.ShapeDtypeStruct((t, C_W), f32),
                   jax.ShapeDtypeStruct((1, d), f32)],
        compiler_params=_params(("arbitrary",)),
    )(dy, m, post_g, w_out, *deps)


def _conv_fwd(buf_ref, halo, x, w, b, n):
    buf_ref[0:8, :] = halo
    buf_ref[8:8 + n, :] = x
    out = b + w[3:4, :] * x
    for k in range(3):
        out = out + w[k:k + 1, :] * buf_ref[pl.ds(5 + k, n), :]
    return out


def _conv_bwd(buf_ref, dbuf_ref, dout, dnext, w, n):
    dbuf_ref[0:n, :] = dout
    dbuf_ref[n:n + 8, :] = dnext
    dx = w[3:4, :] * dout
    dws = []
    for k in range(3):
        dx = dx + w[k:k + 1, :] * dbuf_ref[pl.ds(3 - k, n), :]
        dws.append(jnp.sum(dout * buf_ref[pl.ds(5 + k, n), :], axis=0, keepdims=True))
    dws.append(jnp.sum(dout * buf_ref[pl.ds(8, n), :], axis=0, keepdims=True))
    return dx, jnp.concatenate(dws, axis=0), jnp.sum(dout, axis=0, keepdims=True)


def _lru_gates(rec, wr, wi, br, bi, lam):
    rb = rec.astype(bf16)
    r = _sigmoid(_nn(rb, wr) + br)
    ig = _sigmoid(_nn(rb, wi) + bi)
    sp = _softplus(-lam)
    la = -LRU_C * r * sp
    a = jnp.exp(la)
    mult = jnp.sqrt(_one_minus_exp(2.0 * la))
    return rb, r, ig, sp, a, mult


def _scan_rows(a_ref, b_ref, o_ref, carry, n, reverse):
    row = lax.broadcasted_iota(jnp.int32, (8, a_ref.shape[1]), 0)
    nb = n // 8

    def step(k, carry):
        blk = (nb - 1 - k) if reverse else k
        rows = pl.ds(pl.multiple_of(blk * 8, 8), 8)
        a, b = a_ref[rows, :], b_ref[rows, :]
        for d in (1, 2, 4):
            shift = 8 - d if reverse else d
            keep = (row < 8 - d) if reverse else (row >= d)
            b = a * jnp.where(keep, pltpu.roll(b, shift, 0), 0.0) + b
            a = a * jnp.where(keep, pltpu.roll(a, shift, 0), 1.0)
        o = a * carry + b
        o_ref[rows, :] = o
        return o[0:1, :] if reverse else o[7:8, :]

    return lax.fori_loop(0, nb, step, carry, unroll=2)


def _lru_fwd(pa, conv_w, conv_b, wr, wi, br, bi, lam, l):
    t = pa.shape[0]
    tc = _tile(t, 512)

    def body(pa_ref, halo_ref, cw_ref, cb_ref, wr_ref, wi_ref, br_ref, bi_ref, lam_ref,
             ya_ref, h_ref, buf_ref, a_ref, u_ref, carry_ref):
        i = pl.program_id(0)

        @pl.when(i == 0)
        def _():
            carry_ref[...] = jnp.zeros_like(carry_ref)

        halo = jnp.where(i > 0, halo_ref[:, A_W:PA_W], 0.0)
        rec = _conv_fwd(buf_ref, halo, pa_ref[:, A_W:PA_W], cw_ref[...], cb_ref[...], tc)
        _, _, ig, _, a, mult = _lru_gates(rec, wr_ref[...], wi_ref[...], br_ref[...], bi_ref[...], lam_ref[...])
        a_ref[...] = a
        u_ref[...] = mult * (ig * rec)

        carry_ref[...] = _scan_rows(a_ref, u_ref, h_ref, carry_ref[...], tc, reverse=False)
        ya_ref[...] = h_ref[...] * _gelu(pa_ref[:, 0:A_W])

    vec = pl.BlockSpec((1, A_W), lambda i: (0, 0))
    mat = pl.BlockSpec((A_W, A_W), lambda i: (0, 0))
    row = pl.BlockSpec((tc, A_W), lambda i: (i, 0))
    return pl.pallas_call(
        body, name="lru_fwd", grid=(t // tc,),
        in_specs=[pl.BlockSpec((tc, PA_W), lambda i: (i, 0)),
                  pl.BlockSpec((8, PA_W), lambda i: (jnp.maximum(i * (tc // 8) - 1, 0), 0)),
                  *[_lspec(a, l) for a in (conv_w, conv_b, wr, wi, br, bi, lam)]],
        out_specs=[row, row],
        out_shape=[jax.ShapeDtypeStruct((t, A_W), f32), jax.ShapeDtypeStruct((t, A_W), f32)],
        scratch_shapes=[pltpu.VMEM((8 + tc, A_W), f32), pltpu.VMEM((tc, A_W), f32), pltpu.VMEM((tc, A_W), f32),
                        pltpu.VMEM((1, A_W), f32)],
        compiler_params=_params(("arbitrary",)),
    )(pa, pa, conv_w, conv_b, wr, wi, br, bi, lam)


def _lru_bwd(pa, h, dya, conv_w, conv_b, wr, wi, br, bi, lam, l, deps=()):
    t = pa.shape[0]
    tc = _tile(t, 512)
    nc = t // tc

    def body(pa_ref, halo_ref, h_ref, hhalo_ref, dya_ref, cw_ref, cb_ref, wr_ref, wi_ref, br_ref, bi_ref, lam_ref,
             dpa_ref, dcw_ref, dcb_ref, dwr_ref, dwi_ref, dbr_ref, dbi_ref, dlam_ref,
             buf_ref, dbuf_ref, hbuf_ref, a_ref, g_ref, dh_ref, carry_ref, dnext_ref, dhbuf_ref):
        i = pl.program_id(0)
        c = nc - 1 - i

        @pl.when(i == 0)
        def _():
            carry_ref[...] = jnp.zeros_like(carry_ref)
            dnext_ref[...] = jnp.zeros_like(dnext_ref)
            for ref in (dcw_ref, dcb_ref, dwr_ref, dwi_ref, dbr_ref, dbi_ref, dlam_ref):
                ref[...] = jnp.zeros_like(ref)

        halo = jnp.where(c > 0, halo_ref[:, A_W:PA_W], 0.0)
        cw = cw_ref[...]
        rec = _conv_fwd(buf_ref, halo, pa_ref[:, A_W:PA_W], cw, cb_ref[...], tc)
        lam = lam_ref[...]
        rb, r, ig, sp, a, mult = _lru_gates(rec, wr_ref[...], wi_ref[...], br_ref[...], bi_ref[...], lam)
        hbuf_ref[0:8, :] = jnp.where(c > 0, hhalo_ref[...], 0.0)
        hbuf_ref[8:8 + tc, :] = h_ref[...]
        h_prev = hbuf_ref[pl.ds(7, tc), :]
        gate = pa_ref[:, 0:A_W]
        dya = dya_ref[...]
        dpa_ref[:, 0:A_W] = (dya * h_ref[...] * _gelu_grad(gate)).astype(bf16)
        a_ref[...] = a
        gg = dya * _gelu(gate)
        g_ref[...] = a * gg
        carry_in = carry_ref[...]
        carry_ref[...] = _scan_rows(a_ref, g_ref, dh_ref, carry_in, tc, reverse=True)
        dhbuf_ref[0:tc, :] = dh_ref[...]
        dhbuf_ref[tc:tc + 8, :] = jnp.broadcast_to(carry_in, (8, A_W))
        dh = gg + dhbuf_ref[pl.ds(1, tc), :]
        da = dh * h_prev
        dmult = dh * ig * rec
        dig = dh * mult * rec
        drec = dh * mult * ig
        dla = da * a - dmult * (a * a) / mult
        dr = dla * (-LRU_C * sp)
        dsp = jnp.sum(dla * (-LRU_C * r), axis=0, keepdims=True)
        dlam_ref[...] += dsp * (-_sigmoid(-lam))
        dpr = (dr * r * (1.0 - r))
        dpi = (dig * ig * (1.0 - ig))
        dprb, dpib = dpr.astype(bf16), dpi.astype(bf16)
        drec = drec + _nt(dprb, wr_ref[...]) + _nt(dpib, wi_ref[...])
        dwr_ref[...] += _tn(rb, dprb)
        dwi_ref[...] += _tn(rb, dpib)
        dbr_ref[...] += jnp.sum(dpr, axis=0, keepdims=True)
        dbi_ref[...] += jnp.sum(dpi, axis=0, keepdims=True)
        dx, dw, db = _conv_bwd(buf_ref, dbuf_ref, drec, dnext_ref[...], cw, tc)
        dnext_ref[...] = drec[0:8, :]
        dcw_ref[...] += dw
        dcb_ref[...] += db
        dpa_ref[:, A_W:PA_W] = dx.astype(bf16)

    vec = pl.BlockSpec((1, A_W), lambda i: (0, 0))
    mat = pl.BlockSpec((A_W, A_W), lambda i: (0, 0))
    cwspec = pl.BlockSpec((4, A_W), lambda i: (0, 0))

    def rev(w):
        return pl.BlockSpec((tc, w), lambda i: (nc - 1 - i, 0))

    def halo(w):
        return pl.BlockSpec((8, w), lambda i: (jnp.maximum((nc - 1 - i) * (tc // 8) - 1, 0), 0))

    chunk = pltpu.VMEM((tc, A_W), f32)
    return pl.pallas_call(
        _after(body, 12, deps), name="lru_bwd", grid=(nc,),
        in_specs=[rev(PA_W), halo(PA_W), rev(A_W), halo(A_W), rev(A_W),
                  *[_lspec(a, l) for a in (conv_w, conv_b, wr, wi, br, bi, lam)]] + [ANY] * len(deps),
        out_specs=[rev(PA_W), cwspec, vec, mat, mat, vec, vec, vec],
        out_shape=[jax.ShapeDtypeStruct((t, PA_W), bf16), jax.ShapeDtypeStruct((4, A_W), f32),
                   jax.ShapeDtypeStruct((1, A_W), f32), jax.ShapeDtypeStruct((A_W, A_W), f32),
                   jax.ShapeDtypeStruct((A_W, A_W), f32), jax.ShapeDtypeStruct((1, A_W), f32),
                   jax.ShapeDtypeStruct((1, A_W), f32), jax.ShapeDtypeStruct((1, A_W), f32)],
        scratch_shapes=[pltpu.VMEM((8 + tc, A_W), f32), pltpu.VMEM((tc + 8, A_W), f32), pltpu.VMEM((8 + tc, A_W), f32),
                        chunk, chunk, chunk, pltpu.VMEM((1, A_W), f32), pltpu.VMEM((8, A_W), f32),
                        pltpu.VMEM((tc + 8, A_W), f32)],
        compiler_params=_params(("arbitrary",)),
    )(pa, pa, h, h, dya, conv_w, conv_b, wr, wi, br, bi, lam, *deps)


def _sgu_norm(v, g, b):
    mu = jnp.mean(v, axis=-1, keepdims=True)
    vc = v - mu
    rstd = lax.rsqrt(jnp.mean(vc * vc, axis=-1, keepdims=True) + NORM_EPS)
    vh = vc * rstd
    return vh, rstd, vh * g + b


def _sgu_mix(w_ref, vb, bias):
    grp = lax.broadcasted_iota(jnp.int32, (CHUNK, C_W), 1) // HEAD
    out = bias
    for gi in range(C_W // HEAD):
        out = out + jnp.where(grp == gi, _nn(w_ref[gi], vb), 0.0)
    return out


def _sgu_fwd(pc, ln_g, ln_b, wm, bias, l):
    t = pc.shape[0]
    tm = _tile(t, 512)

    def body(pc_ref, g_ref, b_ref, w_ref, bias_ref, yc_ref):
        for ci in range(tm // CHUNK):
            rows = pl.ds(ci * CHUNK, CHUNK)
            ge = _gelu(pc_ref[rows, :])
            _, _, vn = _sgu_norm(ge[:, C_W:PC_W], g_ref[...], b_ref[...])
            yc_ref[rows, :] = ge[:, 0:C_W] * _sgu_mix(w_ref, vn.astype(bf16), bias_ref[...])

    vec = pl.BlockSpec((1, C_W), lambda i: (0, 0))
    return pl.pallas_call(
        body, name="sgu_fwd", grid=(t // tm,),
        in_specs=[pl.BlockSpec((tm, PC_W), lambda i: (i, 0)), *[_lspec(a, l) for a in (ln_g, ln_b, wm, bias)]],
        out_specs=pl.BlockSpec((tm, C_W), lambda i: (i, 0)),
        out_shape=jax.ShapeDtypeStruct((t, C_W), f32),
        compiler_params=_params(("arbitrary",)),
    )(pc, ln_g, ln_b, wm, bias)


def _sgu_bwd(pc, dyc, ln_g, ln_b, wm, wmt, bias, l, deps=()):
    t = pc.shape[0]
    tm = _tile(t, 512)

    def body(pc_ref, dyc_ref, g_ref, b_ref, w_ref, wt_ref, bias_ref, dpc_ref, dw_ref, dbias_ref, dg_ref, db_ref):
        @pl.when(pl.program_id(0) == 0)
        def _():
            for ref in (dw_ref, dbias_ref, dg_ref, db_ref):
                ref[...] = jnp.zeros_like(ref)

        grp = lax.broadcasted_iota(jnp.int32, (CHUNK, C_W), 1) // HEAD
        for ci in range(tm // CHUNK):
            rows = pl.ds(ci * CHUNK, CHUNK)
            x = pc_ref[rows, :]
            ge = _gelu(x)
            gv = g_ref[...]
            vh, rstd, vn = _sgu_norm(ge[:, C_W:PC_W], gv, b_ref[...])
            vb = vn.astype(bf16)
            mixed = _sgu_mix(w_ref, vb, bias_ref[...])
            dyc = dyc_ref[rows, :]
            du = dyc * mixed
            dmix = dyc * ge[:, 0:C_W]
            dmb = dmix.astype(bf16)
            dvn = jnp.zeros((CHUNK, C_W), f32)
            for gi in range(C_W // HEAD):
                dvn = dvn + jnp.where(grp == gi, _nn(wt_ref[gi], dmb), 0.0)
                dw_ref[gi] += _nt(jnp.where(grp == gi, dmix, 0.0).astype(bf16), vb)
            dbias_ref[...] += dmix
            dg_ref[...] += jnp.sum(dvn * vh, axis=0, keepdims=True)
            db_ref[...] += jnp.sum(dvn, axis=0, keepdims=True)
            dvh = dvn * gv
            dv = rstd * (dvh - jnp.mean(dvh, axis=-1, keepdims=True) - vh * jnp.mean(dvh * vh, axis=-1, keepdims=True))
            gg = _gelu_grad(x)
            dpc_ref[rows, 0:C_W] = (du * gg[:, 0:C_W]).astype(bf16)
            dpc_ref[rows, C_W:PC_W] = (dv * gg[:, C_W:PC_W]).astype(bf16)

    vec = pl.BlockSpec((1, C_W), lambda i: (0, 0))
    wspec = pl.BlockSpec((4, CHUNK, CHUNK), lambda i: (0, 0, 0))
    bspec = pl.BlockSpec((CHUNK, C_W), lambda i: (0, 0))
    return pl.pallas_call(
        _after(body, 7, deps), name="sgu_bwd", grid=(t // tm,),
        in_specs=[pl.BlockSpec((tm, PC_W), lambda i: (i, 0)), pl.BlockSpec((tm, C_W), lambda i: (i, 0)),
                  *[_lspec(a, l) for a in (ln_g, ln_b, wm, wmt, bias)]] + [ANY] * len(deps),
        out_specs=[pl.BlockSpec((tm, PC_W), lambda i: (i, 0)), wspec, bspec, vec, vec],
        out_shape=[jax.ShapeDtypeStruct((t, PC_W), bf16), jax.ShapeDtypeStruct((4, CHUNK, CHUNK), f32),
                   jax.ShapeDtypeStruct((CHUNK, C_W), f32), jax.ShapeDtypeStruct((1, C_W), f32),
                   jax.ShapeDtypeStruct((1, C_W), f32)],
        compiler_params=_params(("arbitrary",)),
    )(pc, dyc, ln_g, ln_b, wm, wmt, bias, *deps)


N_PAIR = B_W // 128
HEADS_PER_GROUP = 3


def _pair_groups(p):
    return (2 * p) // HEADS_PER_GROUP, (2 * p + 1) // HEADS_PER_GROUP


def _ssd_chunk(pb_ref, halo, buf_ref, cw, cb, dtb, alog):
    z = pb_ref[:, 0:B_W]
    pre = _conv_fwd(buf_ref, halo, pb_ref[:, B_W:B_W + XBC_W], cw, cb, CHUNK)
    sg = _sigmoid(pre)
    xbc = pre * sg
    xs = xbc[:, 0:B_W]
    bm = [xbc[:, B_W + k * B_STATE:B_W + (k + 1) * B_STATE] for k in range(2)]
    cm = [xbc[:, B_W + (2 + k) * B_STATE:B_W + (3 + k) * B_STATE] for k in range(2)]
    dtin = pb_ref[:, B_W + XBC_W:PB_W] + dtb
    dt = _softplus(dtin)
    a = -jnp.exp(alog)
    cs = _cumsum_rows(dt * a)
    return dict(z=z, pre=pre, sg=sg, xs=xs, bm=bm, cm=cm, dtin=dtin, dt=dt, a=a, cs=cs,
                ecs=jnp.exp(cs), ds=jnp.exp(cs[CHUNK - 1:CHUNK, :] - cs), xdt=xs * dt,
                bmb=[v.astype(bf16) for v in bm], cmb=[v.astype(bf16) for v in cm])


def _ssd_decay(cs_pair, half):
    cst = cs_pair.T
    lane0 = HEAD * half
    csc = jnp.broadcast_to(cs_pair[:, lane0:lane0 + 1], (CHUNK, CHUNK))
    csr = cst[lane0:lane0 + 1, :]
    tri = lax.broadcasted_iota(jnp.int32, (CHUNK, CHUNK), 0) >= lax.broadcasted_iota(jnp.int32, (CHUNK, CHUNK), 1)
    return jnp.exp(jnp.where(tri, csc - csr, NEG_BIG)), cst


def _ssd_fwd(pb, conv_w, conv_b, dtb, alog, dskip, norm_g, l):
    t = pb.shape[0]
    nc = t // CHUNK

    def body(pb_ref, halo_ref, cw_ref, cb_ref, dtb_ref, alog_ref, d_ref, ng_ref, yb_ref, yp_ref, sp_ref, buf_ref, s_ref):
        i = pl.program_id(0)

        @pl.when(i == 0)
        def _():
            s_ref[...] = jnp.zeros_like(s_ref)

        halo = jnp.where(i > 0, halo_ref[:, B_W:B_W + XBC_W], 0.0)
        q = _ssd_chunk(pb_ref, halo, buf_ref, cw_ref[...], cb_ref[...], dtb_ref[...], alog_ref[...])
        sp_ref[0] = s_ref[...]
        lane = lax.broadcasted_iota(jnp.int32, (CHUNK, 128), 1)
        rowi = lax.broadcasted_iota(jnp.int32, (128, B_STATE), 0)
        cb_mat = [_nt(q["cmb"][k], q["bmb"][k]) for k in range(2)]
        xd = q["xdt"] * q["ds"]
        for p in range(N_PAIR):
            cols = slice(128 * p, 128 * (p + 1))
            g_lo, g_hi = _pair_groups(p)
            cs_p, xdt_p = q["cs"][:, cols], q["xdt"][:, cols]
            s_p = s_ref[cols, :]
            s_pb = s_p.astype(bf16)
            y_p = jnp.zeros((CHUNK, 128), f32)
            for half, grp in ((0, g_lo), (1, g_hi)):
                lm, cst = _ssd_decay(cs_p, half)
                mb = (cb_mat[grp] * lm).astype(bf16)
                sel = (lane < HEAD) if half == 0 else (lane >= HEAD)
                y_p = y_p + _nn(mb, jnp.where(sel, xdt_p, 0.0).astype(bf16))
            off_lo = _nt(q["cmb"][g_lo], s_pb)
            off = off_lo if g_lo == g_hi else jnp.where(lane < HEAD, off_lo, _nt(q["cmb"][g_hi], s_pb))
            y_p = y_p + off * q["ecs"][:, cols] + q["xs"][:, cols] * d_ref[:, cols]
            yp_ref[:, cols] = y_p
            xd_pb = xd[:, cols].astype(bf16)
            upd_lo = _tn(xd_pb, q["bmb"][g_lo])
            upd = upd_lo if g_lo == g_hi else jnp.where(rowi < HEAD, upd_lo, _tn(xd_pb, q["bmb"][g_hi]))
            cd = jnp.exp(jnp.broadcast_to(cst[:, CHUNK - 1:CHUNK], (128, B_STATE)))
            s_ref[cols, :] = cd * s_p + upd
        z = q["z"]
        yg = yp_ref[...] * (z * _sigmoid(z))
        yb_ref[...] = _rms_fwd(yg, ng_ref[...])

    vec = pl.BlockSpec((1, B_W), lambda i: (0, 0))
    row = pl.BlockSpec((CHUNK, B_W), lambda i: (i, 0))
    return pl.pallas_call(
        body, name="ssd_fwd", grid=(nc,),
        in_specs=[pl.BlockSpec((CHUNK, PB_W), lambda i: (i, 0)),
                  pl.BlockSpec((8, PB_W), lambda i: (jnp.maximum(i * (CHUNK // 8) - 1, 0), 0)),
                  *[_lspec(a, l) for a in (conv_w, conv_b, dtb, alog, dskip, norm_g)]],
        out_specs=[row, row, pl.BlockSpec((1, B_W, B_STATE), lambda i: (i, 0, 0))],
        out_shape=[jax.ShapeDtypeStruct((t, B_W), f32), jax.ShapeDtypeStruct((t, B_W), f32),
                   jax.ShapeDtypeStruct((nc, B_W, B_STATE), f32)],
        scratch_shapes=[pltpu.VMEM((8 + CHUNK, XBC_W), f32), pltpu.VMEM((B_W, B_STATE), f32)],
        compiler_params=_params(("arbitrary",)),
    )(pb, pb, conv_w, conv_b, dtb, alog, dskip, norm_g)


def _ssd_bwd(pb, yp, sprev, dyb, conv_w, conv_b, dtb, alog, dskip, norm_g, l):
    t = pb.shape[0]
    nc = t // CHUNK

    def body(pb_ref, halo_ref, yp_ref, sp_ref, dyb_ref, cw_ref, cb_ref, dtb_ref, alog_ref, d_ref, ng_ref,
             dpb_ref, dcw_ref, dcb_ref, ddtb_ref, dalog_ref, dd_ref, dng_ref,
             buf_ref, dbuf_ref, ds_ref, dnext_ref, dxbc_ref, dcs_ref, dxdt_ref):
        i = pl.program_id(0)
        c = nc - 1 - i

        @pl.when(i == 0)
        def _():
            ds_ref[...] = jnp.zeros_like(ds_ref)
            dnext_ref[...] = jnp.zeros_like(dnext_ref)
            for ref in (dcw_ref, dcb_ref, ddtb_ref, dalog_ref, dd_ref, dng_ref):
                ref[...] = jnp.zeros_like(ref)

        halo = jnp.where(c > 0, halo_ref[:, B_W:B_W + XBC_W], 0.0)
        cw = cw_ref[...]
        q = _ssd_chunk(pb_ref, halo, buf_ref, cw, cb_ref[...], dtb_ref[...], alog_ref[...])
        z, xs, dt, a, ecs, dsd, xdt = q["z"], q["xs"], q["dt"], q["a"], q["ecs"], q["ds"], q["xdt"]
        sz = _sigmoid(z)
        siluz = z * sz
        yp = yp_ref[...]
        dyg, dng = _rms_bwd(yp * siluz, ng_ref[...], dyb_ref[...])
        dng_ref[...] += dng
        dy = dyg * siluz
        dpb_ref[:, 0:B_W] = (dyg * yp * _silu_grad(z, sz)).astype(bf16)
        dd_ref[...] += jnp.sum(dy * xs, axis=0, keepdims=True)
        g1 = dy * ecs
        lane = lax.broadcasted_iota(jnp.int32, (CHUNK, 128), 1)
        rowi = lax.broadcasted_iota(jnp.int32, (128, B_STATE), 0)
        rowc = lax.broadcasted_iota(jnp.int32, (CHUNK, 128), 0)
        cb_mat = [_nt(q["cmb"][k], q["bmb"][k]) for k in range(2)]
        d_cb = [jnp.zeros((CHUNK, CHUNK), f32) for _ in range(2)]
        d_b = [jnp.zeros((CHUNK, B_STATE), f32) for _ in range(2)]
        d_c = [jnp.zeros((CHUNK, B_STATE), f32) for _ in range(2)]
        for p in range(N_PAIR):
            cols = slice(128 * p, 128 * (p + 1))
            g_lo, g_hi = _pair_groups(p)
            lo, hi = lane < HEAD, lane >= HEAD
            cs_p, xdt_p, dy_p, ds_p, g1_p = q["cs"][:, cols], xdt[:, cols], dy[:, cols], dsd[:, cols], g1[:, cols]
            s_p = sp_ref[0, cols, :]
            s_pb = s_p.astype(bf16)
            dsn = ds_ref[cols, :]
            dsnb = dsn.astype(bf16)
            g1b = g1_p.astype(bf16)
            off_lo = _nt(q["cmb"][g_lo], s_pb)
            off = off_lo if g_lo == g_hi else jnp.where(lo, off_lo, _nt(q["cmb"][g_hi], s_pb))
            dcs_p = dy_p * off * ecs[:, cols]
            dsp_lo = _tn(g1b, q["cmb"][g_lo])
            dsp = dsp_lo if g_lo == g_hi else jnp.where(rowi < HEAD, dsp_lo, _tn(g1b, q["cmb"][g_hi]))
            dx_lo = _nt(q["bmb"][g_lo], dsnb)
            dxd = dx_lo if g_lo == g_hi else jnp.where(lo, dx_lo, _nt(q["bmb"][g_hi], dsnb))
            xd_p = xdt_p * ds_p
            if g_lo == g_hi:
                d_c[g_lo] = d_c[g_lo] + _nn(g1b, s_pb)
                d_b[g_lo] = d_b[g_lo] + _nn(xd_p.astype(bf16), dsnb)
            else:
                d_c[g_lo] = d_c[g_lo] + _nn(jnp.where(lo, g1_p, 0.0).astype(bf16), s_pb)
                d_c[g_hi] = d_c[g_hi] + _nn(jnp.where(hi, g1_p, 0.0).astype(bf16), s_pb)
                d_b[g_lo] = d_b[g_lo] + _nn(jnp.where(lo, xd_p, 0.0).astype(bf16), dsnb)
                d_b[g_hi] = d_b[g_hi] + _nn(jnp.where(hi, xd_p, 0.0).astype(bf16), dsnb)
            dxdt_p = dxd * ds_p
            t2 = dxd * xdt_p * ds_p
            dcs_p = dcs_p - t2
            dlast = jnp.sum(t2, axis=0, keepdims=True)
            cst = None
            for half, grp in ((0, g_lo), (1, g_hi)):
                sel = lo if half == 0 else hi
                lm, cst = _ssd_decay(cs_p, half)
                m = cb_mat[grp] * lm
                dyh = jnp.where(sel, dy_p, 0.0).astype(bf16)
                xdh = jnp.where(sel, xdt_p, 0.0).astype(bf16)
                dm = _nt(dyh, xdh)
                pm = dm * m
                col = jnp.sum(pm, axis=1, keepdims=True) - jnp.sum(pm.T, axis=1, keepdims=True)
                dcs_p = dcs_p + jnp.where(lane == HEAD * half, col, 0.0)
                d_cb[grp] = d_cb[grp] + dm * lm
                dxdt_p = dxdt_p + _tn(m.astype(bf16), dyh)
            cdcol = jnp.exp(jnp.broadcast_to(cst[:, CHUNK - 1:CHUNK], (128, B_STATE)))
            ds_ref[cols, :] = cdcol * dsn + dsp
            dcd_row = jnp.sum((dsn * s_p).T, axis=0, keepdims=True)
            dlast = dlast + dcd_row * ecs[CHUNK - 1:CHUNK, cols]
            dcs_ref[:, cols] = dcs_p + jnp.where(rowc == CHUNK - 1, dlast, 0.0)
            dxdt_ref[:, cols] = dxdt_p
        for k in range(2):
            dcbb = d_cb[k].astype(bf16)
            d_c[k] = d_c[k] + _nn(dcbb, q["bmb"][k])
            d_b[k] = d_b[k] + _tn(dcbb, q["cmb"][k])
            dxbc_ref[:, B_W + k * B_STATE:B_W + (k + 1) * B_STATE] = d_b[k]
            dxbc_ref[:, B_W + (2 + k) * B_STATE:B_W + (3 + k) * B_STATE] = d_c[k]
        dxdt = dxdt_ref[...]
        dxbc_ref[:, 0:B_W] = dy * d_ref[...] + dxdt * dt
        dcs = dcs_ref[...]
        dad = jnp.sum(dcs, axis=0, keepdims=True) - _cumsum_rows(dcs) + dcs
        ddt = dxdt * xs + dad * a
        dalog_ref[...] += jnp.sum(dad * dt, axis=0, keepdims=True) * a
        dtraw = ddt * _sigmoid(q["dtin"])
        ddtb_ref[...] += jnp.sum(dtraw, axis=0, keepdims=True)
        dpb_ref[:, B_W + XBC_W:PB_W] = dtraw.astype(bf16)
        dpre = dxbc_ref[...] * _silu_grad(q["pre"], q["sg"])
        dx, dw, db = _conv_bwd(buf_ref, dbuf_ref, dpre, dnext_ref[...], cw, CHUNK)
        dnext_ref[...] = dpre[0:8, :]
        dcw_ref[...] += dw
        dcb_ref[...] += db
        dpb_ref[:, B_W:B_W + XBC_W] = dx.astype(bf16)

    vec = pl.BlockSpec((1, B_W), lambda i: (0, 0))
    cwspec = pl.BlockSpec((4, XBC_W), lambda i: (0, 0))
    cbspec = pl.BlockSpec((1, XBC_W), lambda i: (0, 0))

    def rev(w):
        return pl.BlockSpec((CHUNK, w), lambda i: (nc - 1 - i, 0))

    vshape = jax.ShapeDtypeStruct((1, B_W), f32)
    return pl.pallas_call(
        body, name="ssd_bwd", grid=(nc,),
        in_specs=[rev(PB_W), pl.BlockSpec((8, PB_W), lambda i: (jnp.maximum((nc - 1 - i) * (CHUNK // 8) - 1, 0), 0)),
                  rev(B_W), pl.BlockSpec((1, B_W, B_STATE), lambda i: (nc - 1 - i, 0, 0)), rev(B_W),
                  *[_lspec(a, l) for a in (conv_w, conv_b, dtb, alog, dskip, norm_g)]],
        out_specs=[rev(PB_W), cwspec, cbspec, vec, vec, vec, vec],
        out_shape=[jax.ShapeDtypeStruct((t, PB_W), bf16), jax.ShapeDtypeStruct((4, XBC_W), f32),
                   jax.ShapeDtypeStruct((1, XBC_W), f32), vshape, vshape, vshape, vshape],
        scratch_shapes=[pltpu.VMEM((8 + CHUNK, XBC_W), f32), pltpu.VMEM((CHUNK + 8, XBC_W), f32),
                        pltpu.VMEM((B_W, B_STATE), f32), pltpu.VMEM((8, XBC_W), f32),
                        pltpu.VMEM((CHUNK, XBC_W), f32), pltpu.VMEM((CHUNK, B_W), f32), pltpu.VMEM((CHUNK, B_W), f32)],
        compiler_params=_params(("arbitrary",)),
    )(pb, pb, yp, sprev, dyb, conv_w, conv_b, dtb, alog, dskip, norm_g)


def _loss_fwd(y, target):
    t, d = y.shape
    tm = _tile(t, 512)

    def body(y_ref, t_ref, dy_ref, loss_ref):
        @pl.when(pl.program_id(0) == 0)
        def _():
            loss_ref[...] = jnp.zeros_like(loss_ref)

        e = y_ref[...] - t_ref[...]
        dy_ref[...] = e * (1.0 / d)
        per_tok = jnp.mean(e * e, axis=-1, keepdims=True)
        loss_ref[...] += 0.5 * jnp.sum(per_tok, axis=0, keepdims=True)

    row = pl.BlockSpec((tm, d), lambda i: (i, 0))
    return pl.pallas_call(
        body, name="loss_fwd", grid=(t // tm,), in_specs=[row, row],
        out_specs=[row, pl.BlockSpec((1, 128), lambda i: (0, 0))],
        out_shape=[jax.ShapeDtypeStruct((t, d), f32), jax.ShapeDtypeStruct((1, 128), f32)],
        compiler_params=_params(("arbitrary",)),
    )(y, target)


def _row_tile(r):
    return 512 if r % 512 == 0 else r


def _pair_add(g, r, c_dev):
    _, nl, rows, cols = g.shape
    tr = _row_tile(rows)

    def body(c_ref, g_ref, r_ref, o_ref):
        o_ref[...] = (g_ref[...].astype(f32) + r_ref[...].astype(f32)).astype(bf16)

    blk = (None, None, tr, cols)
    return pl.pallas_call(
        body, name="pair_add",
        grid_spec=pltpu.PrefetchScalarGridSpec(
            num_scalar_prefetch=1, grid=(4, nl, rows // tr),
            in_specs=[pl.BlockSpec(blk, lambda b, l, i, c: (2 * b + c[0], l, i, 0)),
                      pl.BlockSpec(blk, lambda b, l, i, c: (b, l, i, 0))],
            out_specs=pl.BlockSpec(blk, lambda b, l, i, c: (b, l, i, 0))),
        out_shape=jax.ShapeDtypeStruct(r.shape, bf16),
        compiler_params=_params(("arbitrary", "arbitrary", "arbitrary")),
    )(c_dev, g, r)


def _grad_sum(s, q, b_dev):
    _, nl, rows, cols = s.shape
    tr = _row_tile(rows)

    def body(b_ref, s_ref, q0_ref, q1_ref, q2_ref, o_ref):
        o_ref[...] = ((s_ref[...].astype(f32) + q0_ref[...].astype(f32)) + q1_ref[...].astype(f32)) + q2_ref[...].astype(f32)

    blk = (None, None, tr, cols)

    def qspec(k):
        return pl.BlockSpec(blk, lambda l, i, b: (k, l, i, 0))

    return pl.pallas_call(
        body, name="grad_sum",
        grid_spec=pltpu.PrefetchScalarGridSpec(
            num_scalar_prefetch=1, grid=(nl, rows // tr),
            in_specs=[pl.BlockSpec(blk, lambda l, i, b: (b[0], l, i, 0)), qspec(0), qspec(1), qspec(2)],
            out_specs=pl.BlockSpec((None, tr, cols), lambda l, i, b: (l, i, 0))),
        out_shape=jax.ShapeDtypeStruct(s.shape[1:], f32),
        compiler_params=_params(("arbitrary", "arbitrary")),
    )(b_dev, s, q, q, q)


def _sum_devices(parts):
    n, rows, cols = parts.shape
    tr = _row_tile(rows)

    def body(p_ref, o_ref):
        acc = p_ref[0]
        for k in range(1, n):
            acc = acc + p_ref[k]
        o_ref[...] = acc

    return pl.pallas_call(
        body, name="sum_devices", grid=(rows // tr,),
        in_specs=[pl.BlockSpec((n, tr, cols), lambda i: (0, i, 0))],
        out_specs=pl.BlockSpec((tr, cols), lambda i: (i, 0)),
        out_shape=jax.ShapeDtypeStruct((rows, cols), f32),
        compiler_params=_params(("arbitrary",)),
    )(parts)


def _adamw(w, m, v, g):
    nl, rows, cols = w.shape
    tr = _row_tile(rows)

    def body(w_ref, m_ref, v_ref, g_ref, d_ref, nm_ref, nv_ref):
        d_ref[...], nm_ref[...], nv_ref[...] = _adamw_math(w_ref[...], m_ref[...], v_ref[...], g_ref[...])

    blk = pl.BlockSpec((None, tr, cols), lambda l, i: (l, i, 0))
    shape = jax.ShapeDtypeStruct(w.shape, f32)
    return pl.pallas_call(
        body, name="adamw", grid=(nl, rows // tr), in_specs=[blk] * 4, out_specs=[blk] * 3,
        out_shape=[shape] * 3, compiler_params=_params(("arbitrary", "arbitrary")),
    )(w, m, v, g)


def _adamw_math(w, m, v, g):
    nm = ADAM_B1 * m + (1.0 - ADAM_B1) * g
    nv = ADAM_B2 * v + (1.0 - ADAM_B2) * (g * g)
    m_hat = nm / (1.0 - ADAM_B1 ** ADAM_STEP)
    v_hat = nv / (1.0 - ADAM_B2 ** ADAM_STEP)
    return -ADAM_LR * (m_hat / (jnp.sqrt(v_hat) + ADAM_EPS) + ADAM_WD * w), nm, nv


def _adamw_layer(w, m, v, s, q, b_dev, outs, l):
    _, rows, cols = w.shape
    tr = _row_tile(rows)

    def body(b_ref, w_ref, m_ref, v_ref, s_ref, q0_ref, q1_ref, q2_ref, o0, o1, o2, o3, g_ref, d_ref, nm_ref, nv_ref):
        g = ((s_ref[...].astype(f32) + q0_ref[...].astype(f32)) + q1_ref[...].astype(f32)) + q2_ref[...].astype(f32)
        g_ref[...] = g
        d_ref[...], nm_ref[...], nv_ref[...] = _adamw_math(w_ref[...], m_ref[...], v_ref[...], g)

    wspec = pl.BlockSpec((None, tr, cols), lambda i, b: (l, i, 0))
    blk = (None, None, tr, cols)

    def qspec(k):
        return pl.BlockSpec(blk, lambda i, b: (k, 0, i, 0))

    shape = jax.ShapeDtypeStruct(w.shape, f32)
    return pl.pallas_call(
        body, name="adamw_layer",
        grid_spec=pltpu.PrefetchScalarGridSpec(
            num_scalar_prefetch=1, grid=(rows // tr,),
            in_specs=[wspec] * 3 + [pl.BlockSpec(blk, lambda i, b: (b[0], 0, i, 0)), qspec(0), qspec(1), qspec(2)] + [ANY] * 4,
            out_specs=[wspec] * 4),
        out_shape=[shape] * 4, input_output_aliases={8 + k: k for k in range(4)},
        compiler_params=_params(("arbitrary",)),
    )(b_dev, w, m, v, s, q, q, q, *outs)


def _place():
    return lax.axis_index("x"), lax.axis_index("y"), lax.axis_index("c")


def _all_gather(shards, deps=()):
    n = len(shards)
    nd = len(deps)

    def body(*refs):
        src, dst = refs[:n], refs[n:2 * n]
        send_sems, recv_sems, local_sems = refs[2 * n:]
        x, y, c = _place()
        me, sibling = (x, y, c), (x, y, 1 - c)
        chips = [(1 - x, y), (x, 1 - y), (1 - x, 1 - y)]

        def copy(a, k, block, to, from_shard=False):
            px, py, pc = block
            rows = dst[a].at[4 * px + 2 * py + pc]
            return pltpu.make_async_remote_copy(
                src_ref=src[a] if from_shard else rows, dst_ref=rows,
                send_sem=send_sems.at[a, k], recv_sem=recv_sems.at[a, k], device_id=to, device_id_type=MESH)

        mine = [pltpu.make_async_copy(src[a], dst[a].at[4 * x + 2 * y + c], local_sems.at[a]) for a in range(n)]
        for cp in mine:
            cp.start()
        first = []
        for a in range(n):
            first.append(copy(a, 0, me, sibling, True))
            first += [copy(a, 1 + j, me, (*chip, c), True) for j, chip in enumerate(chips)]
        for cp in first:
            cp.start()
        passed = []
        for j, chip in enumerate(chips):
            for a in range(n):
                copy(a, 1 + j, (*chip, c), me).wait_recv()
                fwd = copy(a, 4 + j, (*chip, c), sibling)
                fwd.start()
                passed.append(fwd)
        for a in range(n):
            copy(a, 0, sibling, me).wait_recv()
            for j, chip in enumerate(chips):
                copy(a, 4 + j, (*chip, 1 - c), me).wait_recv()
        for cp in first + passed:
            cp.wait_send()
        for cp in mine:
            cp.wait()

    return pl.pallas_call(
        _after(body, n, deps), name="all_gather", in_specs=[ANY] * (n + nd), out_specs=[ANY] * n,
        out_shape=[jax.ShapeDtypeStruct((N_DEV,) + s.shape, s.dtype) for s in shards],
        scratch_shapes=[pltpu.SemaphoreType.DMA((n, 7)), pltpu.SemaphoreType.DMA((n, 7)), pltpu.SemaphoreType.DMA((n,))],
    )(*shards, *deps)


HBM = pl.BlockSpec(memory_space=pltpu.HBM)
SEM = pl.BlockSpec(memory_space=pltpu.SEMAPHORE)
_EFFECT = pltpu.SideEffectType.DATAFLOW_SIDE_EFFECTING


def _split_start(name, srcs, dsts, sem_shape, plan):
    ns, nb = len(srcs), len(srcs) + len(dsts)

    def body(*refs):
        send_sems, recv_sems = refs[nb], refs[nb + 1]
        for cp in plan(refs[:ns], refs[ns:nb], send_sems, recv_sems):
            cp.start()
        refs[-1][...] = jnp.zeros_like(refs[-1])

    bufs = list(srcs) + list(dsts)
    return pl.pallas_call(
        body, name=name,
        out_shape=(pltpu.SemaphoreType.DMA(sem_shape), pltpu.SemaphoreType.DMA(sem_shape),
                   *[pltpu.HBM(a.shape, a.dtype) for a in bufs], jax.ShapeDtypeStruct((8, 128), f32)),
        in_specs=[HBM] * nb, out_specs=(SEM, SEM, *[HBM] * nb, pl.BlockSpec(memory_space=pltpu.VMEM)),
        input_output_aliases={i: 2 + i for i in range(nb)},
        compiler_params=pltpu.CompilerParams(has_side_effects=_EFFECT),
    )(*[pltpu.with_memory_space_constraint(a, pltpu.HBM) for a in bufs])


def _split_wait(name, started, ns, plan, after):
    send_sems, recv_sems = started[0], started[1]
    bufs = list(started[2:-1])
    nb = len(bufs)

    def body(*refs):
        for cp in plan(refs[:ns], refs[ns:nb], refs[nb], refs[nb + 1]):
            cp.wait_send()
            cp.wait_recv()

    return pl.pallas_call(
        body, name=name, out_shape=tuple(pltpu.HBM(a.shape, a.dtype) for a in bufs),
        in_specs=[HBM] * nb + [SEM, SEM, ANY], out_specs=tuple([HBM] * nb),
        input_output_aliases={i: i for i in range(nb)},
        compiler_params=pltpu.CompilerParams(has_side_effects=_EFFECT),
    )(*bufs, send_sems, recv_sems, after)


def _remote(src, dst, send_sem, recv_sem, to):
    return pltpu.make_async_remote_copy(src_ref=src, dst_ref=dst, send_sem=send_sem, recv_sem=recv_sem,
                                        device_id=to, device_id_type=MESH)


def _gather_plan(src, dst, send_sems, recv_sems):
    x, y, c = _place()
    peers = [(x, y, 1 - c), (1 - x, y, c), (x, 1 - y, c), (1 - x, 1 - y, c)]
    copies = []
    for a in range(len(dst)):
        rows = dst[a].at[4 * x + 2 * y + c]
        copies += [_remote(rows, rows, send_sems.at[4 * a + k], recv_sems.at[4 * a + k], peer) for k, peer in enumerate(peers)]
    return copies


def _pair_plan(src, dst, send_sems, recv_sems):
    x, y, c = _place()
    return [_remote(src[a].at[2 * b + (1 - c)], dst[a].at[b], send_sems.at[4 * a + b], recv_sems.at[4 * a + b], (x, y, 1 - c))
            for a in range(len(src)) for b in range(4)]


def _chips_plan(src, dst, send_sems, recv_sems):
    x, y, c = _place()
    chips = [(1 - x, y), (x, 1 - y), (1 - x, 1 - y)]
    return [_remote(src[a].at[2 * px + py], dst[a].at[j], send_sems.at[3 * a + j], recv_sems.at[3 * a + j], (px, py, c))
            for a in range(len(src)) for j, (px, py) in enumerate(chips)]


def _gather_finish(bufs):
    n = len(bufs)

    def body(*refs):
        dst = refs[n:2 * n]
        send_sems, recv_sems = refs[2 * n:]
        x, y, c = _place()
        chips = [(1 - x, y), (x, 1 - y), (1 - x, 1 - y)]
        passed = []
        for a in range(n):
            for j, (px, py) in enumerate(chips):
                rows = dst[a].at[4 * px + 2 * py + c]
                passed.append(_remote(rows, rows, send_sems.at[a, j], recv_sems.at[a, j], (x, y, 1 - c)))
        for cp in passed:
            cp.start()
        for cp in passed:
            cp.wait_send()
        for a in range(n):
            for j, (px, py) in enumerate(chips):
                rows = dst[a].at[4 * px + 2 * py + (1 - c)]
                _remote(rows, rows, send_sems.at[a, j], recv_sems.at[a, j], (x, y, 1 - c)).wait_recv()

    return pl.pallas_call(
        body, name="gather_finish", in_specs=[ANY] * n, out_specs=[ANY] * n,
        out_shape=[jax.ShapeDtypeStruct(b.shape, b.dtype) for b in bufs],
        input_output_aliases={a: a for a in range(n)},
        scratch_shapes=[pltpu.SemaphoreType.DMA((n, 3)), pltpu.SemaphoreType.DMA((n, 3))],
    )(*bufs)


def _place_shards(mats, l, dev):
    n = len(mats)

    def body(dev_ref, *refs):
        for a in range(n):
            refs[n + a][...] = refs[a][...].astype(bf16)

    return pl.pallas_call(
        body, name="place_shards",
        grid_spec=pltpu.PrefetchScalarGridSpec(
            num_scalar_prefetch=1, grid=(1,),
            in_specs=[pl.BlockSpec((None,) + m.shape[1:], lambda i, dv: (l, 0, 0)) for m in mats],
            out_specs=[pl.BlockSpec((None, None) + m.shape[1:], lambda i, dv: (dv[0], 0, 0, 0)) for m in mats]),
        out_shape=[jax.ShapeDtypeStruct((N_DEV, 1) + m.shape[1:], bf16) for m in mats],
        compiler_params=_params(("arbitrary",)),
    )(dev, *mats)


BIG = ("ffn1_w_gu", "ffn1_w_down", "mix_w_in", "mix_w_out", "ffn2_w_gu", "ffn2_w_down")
SHARDED_CONV = ("lru_conv_w", "ssd_conv_w")
REPLICATED = ("ffn1_pre_g", "ffn1_post_g", "mix_pre_g", "mix_post_g", "lru_conv_b", "lru_w_r", "lru_b_r", "lru_w_i",
              "lru_b_i", "lru_lambda", "ssd_conv_b", "ssd_dt_bias", "ssd_a_log", "ssd_d", "ssd_norm_g", "sgu_ln_g",
              "sgu_ln_b", "sgu_w_s", "sgu_b_s", "ffn2_pre_g", "ffn2_post_g")
WEIGHTS = ("ffn1_pre_g", "ffn1_post_g", "ffn1_w_gu", "ffn1_w_down", "mix_pre_g", "mix_post_g", "mix_w_in", "mix_w_out",
           "lru_conv_w", "lru_conv_b", "lru_w_r", "lru_b_r", "lru_w_i", "lru_b_i", "lru_lambda", "ssd_conv_w",
           "ssd_conv_b", "ssd_dt_bias", "ssd_a_log", "ssd_d", "ssd_norm_g", "sgu_ln_g", "sgu_ln_b", "sgu_w_s", "sgu_b_s",
           "ffn2_pre_g", "ffn2_post_g", "ffn2_w_gu", "ffn2_w_down")
DT_LO = PA_W + B_W + XBC_W
N_HEADS = B_W // HEAD
PACK_COLS = 1024


def _pack(arrays):
    flat = jnp.concatenate([a.reshape(-1) for a in arrays])
    rows = -(-flat.shape[0] // (8 * PACK_COLS)) * 8
    return jnp.pad(flat, (0, rows * PACK_COLS - flat.shape[0])).reshape(rows, PACK_COLS)


def _unpack(packed, shapes):
    flat = packed.reshape(-1)
    out, off = [], 0
    for s in shapes:
        size = 1
        for dim in s:
            size *= dim
        out.append(flat[off:off + size].reshape(s))
        off += size
    return out


def _widen_w_in(w):
    return jnp.concatenate([w[..., :DT_LO], jnp.repeat(w[..., DT_LO:DT_LO + N_HEADS], HEAD, axis=-1),
                            w[..., DT_LO + N_HEADS:]], axis=-1)


def _narrow_w_in_grad(g):
    dt = g[..., DT_LO:DT_LO + B_W]
    dt = dt.reshape(dt.shape[:-1] + (N_HEADS, HEAD)).sum(-1)
    return jnp.concatenate([g[..., :DT_LO], dt, g[..., DT_LO + B_W:]], axis=-1)


def _per_head(a):
    return a.reshape(a.shape[:-1] + (N_HEADS, HEAD)).sum(-1)


def kernel(x, ffn1_pre_g, ffn1_post_g, ffn1_w_gu, ffn1_w_down, mix_pre_g, mix_post_g, mix_w_in, mix_w_out, lru_conv_w, lru_conv_b, lru_w_r, lru_b_r, lru_w_i, lru_b_i, lru_lambda, ssd_conv_w, ssd_conv_b, ssd_dt_bias, ssd_a_log, ssd_d, ssd_norm_g, sgu_ln_g, sgu_ln_b, sgu_w_s, sgu_b_s, ffn2_pre_g, ffn2_post_g, ffn2_w_gu, ffn2_w_down, loss_target, m_ffn1_pre_g, m_ffn1_post_g, m_ffn1_w_gu, m_ffn1_w_down, m_mix_pre_g, m_mix_post_g, m_mix_w_in, m_mix_w_out, m_lru_conv_w, m_lru_conv_b, m_lru_w_r, m_lru_b_r, m_lru_w_i, m_lru_b_i, m_lru_lambda, m_ssd_conv_w, m_ssd_conv_b, m_ssd_dt_bias, m_ssd_a_log, m_ssd_d, m_ssd_norm_g, m_sgu_ln_g, m_sgu_ln_b, m_sgu_w_s, m_sgu_b_s, m_ffn2_pre_g, m_ffn2_post_g, m_ffn2_w_gu, m_ffn2_w_down, v_ffn1_pre_g, v_ffn1_post_g, v_ffn1_w_gu, v_ffn1_w_down, v_mix_pre_g, v_mix_post_g, v_mix_w_in, v_mix_w_out, v_lru_conv_w, v_lru_conv_b, v_lru_w_r, v_lru_b_r, v_lru_w_i, v_lru_b_i, v_lru_lambda, v_ssd_conv_w, v_ssd_conv_b, v_ssd_dt_bias, v_ssd_a_log, v_ssd_d, v_ssd_norm_g, v_sgu_ln_g, v_sgu_ln_b, v_sgu_w_s, v_sgu_b_s, v_ffn2_pre_g, v_ffn2_post_g, v_ffn2_w_gu, v_ffn2_w_down):
    given = dict(locals())
    w = {n: given[n] for n in WEIGHTS}
    mom = {n: given["m_" + n] for n in WEIGHTS}
    var = {n: given["v_" + n] for n in WEIGHTS}
    nl = ffn1_pre_g.shape[0]
    _, t, d = x.shape
    xi, yi, ci = _place()
    dev = 4 * xi + 2 * yi + ci
    c_dev = jnp.reshape(ci, (1,)).astype(jnp.int32)
    b_dev = jnp.reshape(2 * xi + yi, (1,)).astype(jnp.int32)

    conv_shapes = [lru_conv_w.shape, ssd_conv_w.shape]
    shards = [ffn1_w_gu, ffn1_w_down, _widen_w_in(mix_w_in), mix_w_out, ffn2_w_gu, ffn2_w_down]
    nbig = len(shards)
    dev_arr = jnp.reshape(dev, (1,)).astype(jnp.int32)
    conv_pack = _pack([lru_conv_w, ssd_conv_w])
    conv_buf = lax.dynamic_update_slice_in_dim(jnp.zeros((N_DEV,) + conv_pack.shape, f32), conv_pack[None], dev, axis=0)
    def gather_groups(l):
        return [(0, 1), (2, 3), (4, 5)] if l == 0 else [tuple(range(nbig))]

    gather_started = {}
    for l in range(nl):
        for gi, idx in enumerate(gather_groups(l)):
            bufs = list(_place_shards([shards[i] for i in idx], l, dev_arr)) + ([conv_buf] if (l, gi) == (0, 1) else [])
            gather_started[l, gi] = _split_start(f"gather_start_{l}_{gi}", [], bufs, (4 * len(bufs),), _gather_plan)

    def finish_gather(l, gi, after):
        waited = _split_wait(f"gather_wait_{l}_{gi}", gather_started[l, gi], 0, _gather_plan, after)
        return _gather_finish(list(waited))

    def conv_taps(conv_all):
        full = []
        for k, shape in enumerate(conv_shapes):
            per_dev = jnp.stack([_unpack(conv_all[s], conv_shapes)[k] for s in range(N_DEV)], axis=2)
            full.append(per_dev.reshape(shape[0], shape[1], N_DEV * shape[2]))
        return full

    def vec(a):
        return a.reshape(nl, 1, -1)

    def per_channel(a):
        return jnp.repeat(a, HEAD, axis=-1).reshape(nl, 1, B_W)

    eye = jnp.eye(A_W // HEAD, dtype=f32)

    def block_diag(a):
        return jnp.einsum("lhij,hg->lhigj", a, eye).reshape(nl, A_W, A_W).astype(bf16)

    causal = jnp.tril(jnp.ones((CHUNK, CHUNK), dtype=bool))
    p = dict(
        ffn1_pre=vec(ffn1_pre_g), ffn1_post=vec(ffn1_post_g), mix_pre=vec(mix_pre_g), mix_post=vec(mix_post_g),
        ffn2_pre=vec(ffn2_pre_g), ffn2_post=vec(ffn2_post_g),
        lru=(vec(lru_conv_b), block_diag(lru_w_r), block_diag(lru_w_i), vec(lru_b_r), vec(lru_b_i), vec(lru_lambda)),
        ssd=(vec(ssd_conv_b), per_channel(ssd_dt_bias), per_channel(ssd_a_log), per_channel(ssd_d), vec(ssd_norm_g)),
    )
    wm = jnp.where(causal, sgu_w_s, 0.0).astype(bf16)
    sgu_bias = jnp.repeat(jnp.swapaxes(sgu_b_s, 1, 2), HEAD, axis=2)
    sgu_f = (vec(sgu_ln_g), vec(sgu_ln_b), wm, sgu_bias)
    sgu_b = (vec(sgu_ln_g), vec(sgu_ln_b), wm, jnp.swapaxes(wm, 2, 3), sgu_bias)

    xs = x.reshape(t, d)
    saved, gathered = [], []
    for l in range(nl):
        x0 = xs
        if l == 0:
            wgu1, wd1 = finish_gather(0, 0, x0)
            deps = tuple(started[-1] for key, started in gather_started.items() if key != (0, 0))
        else:
            wgu1, wd1, win, wout, wgu2, wd2 = finish_gather(l, 0, x0)
            deps = ()
        x1, hb1, g1, u1, f1 = _ffn_fwd(x0, p["ffn1_pre"], p["ffn1_post"], wgu1, wd1, l, deps)
        if l == 0:
            win, wout, conv_all = finish_gather(0, 1, x1)
            lru_cw, ssd_cw = conv_taps(conv_all)
            p["lru"], p["ssd"] = (lru_cw,) + p["lru"], (ssd_cw,) + p["ssd"]
        hbm, pa, pb, pc = _mix_in_fwd(x1, p["mix_pre"], win, l)
        ya, h = _lru_fwd(pa, *p["lru"], l)
        yb, yp, sp = _ssd_fwd(pb, *p["ssd"], l)
        yc = _sgu_fwd(pc, *sgu_f, l)
        x2, cat, m = _mix_out_fwd(x1, ya, yb, yc, p["mix_post"], wout, l)
        if l == 0:
            wgu2, wd2 = finish_gather(0, 2, x2)
        xs, hb2, g2, u2, f2 = _ffn_fwd(x2, p["ffn2_pre"], p["ffn2_post"], wgu2, wd2, l)
        gathered.append((wgu1, wd1, win, wout, wgu2, wd2))
        saved.append((x0, hb1, g1, u1, f1, x1, hbm, pa, pb, pc, h, yp, sp, cat, m, x2, hb2, g2, u2, f2))
    dy, loss_part = _loss_fwd(xs, loss_target.reshape(t, d))
    loss = lax.psum(loss_part[0, 0], ("x", "y", "c"))

    small = {n: [None] * nl for n in REPLICATED + SHARDED_CONV}
    grads, delta, new_m, new_v = {}, {}, {}, {}
    fused = [n for n in BIG if n != "mix_w_in"]

    def oriented(a, n):
        return jnp.swapaxes(a, 1, 2) if n.endswith("w_gu") else a

    opt_in = {n: tuple(oriented(src[n], n) for src in (w, mom, var)) for n in fused}
    opt_out = {n: tuple(lax.empty(opt_in[n][0].shape, f32) for _ in range(4)) for n in fused}
    w_in_grads = [None] * nl
    grad_shapes = {n: (s.shape[2], s.shape[1]) if n.endswith("w_gu") else s.shape[1:] for n, s in zip(BIG, shards)}

    def start_pair(tag, lp, names, gbuf):
        landing = [lax.empty((4, 1) + grad_shapes[n], bf16) for n in names]
        started = _split_start(f"pair_start_{tag}", [gbuf[n] for n in names], landing, (4 * len(names),), _pair_plan)
        return tag, lp, names, started

    def finish_pair(pending, after):
        tag, lp, names, started = pending
        k = len(names)
        done = _split_wait(f"pair_wait_{tag}", started, k, _pair_plan, after)
        sums = [_pair_add(g, r, c_dev) for g, r in zip(done[:k], done[k:])]
        landing = [lax.empty((3,) + s.shape[1:], bf16) for s in sums]
        return tag, lp, names, _split_start(f"chips_start_{tag}", sums, landing, (3 * k,), _chips_plan)

    def finish_chips(pending, after):
        tag, lp, names, started = pending
        k = len(names)
        done = _split_wait(f"chips_wait_{tag}", started, k, _chips_plan, after)
        for n, s, q in zip(names, done[:k], done[k:]):
            if n == "mix_w_in":
                w_in_grads[lp] = _grad_sum(s, q, b_dev)
            else:
                opt_out[n] = tuple(_adamw_layer(*opt_in[n], s, q, b_dev, opt_out[n], lp))

    early = ("ffn2_w_gu", "ffn2_w_down", "mix_w_out")
    late = ("mix_w_in", "ffn1_w_gu", "ffn1_w_down")
    pending_pair = pending_chips = early_pair = early_chips = None
    for l in reversed(range(nl)):
        x0, hb1, g1, u1, f1, x1, hbm, pa, pb, pc, h, yp, sp, cat, m, x2, hb2, g2, u2, f2 = saved[l]
        wgu1, wd1, win, wout, wgu2, wd2 = gathered[l][:nbig]
        gbuf ={n: lax.empty((N_DEV, 1) + grad_shapes[n], bf16) for n in BIG}
        deps = () if pending_pair is None else (pending_pair[3][-1],)
        dx2, dfb, act, dg, du, dpre, dpost = _ffn_bwd(x2, dy, f2, p["ffn2_pre"], p["ffn2_post"], g2, u2, wgu2, wd2, l, deps)
        small["ffn2_pre_g"][l], small["ffn2_post_g"][l] = dpre[0], dpost[0]
        gbuf["ffn2_w_gu"] = _wgrad_cols(hb2, dg, gbuf["ffn2_w_gu"], 0, 0)
        gbuf["ffn2_w_gu"] = _wgrad_cols(hb2, du, gbuf["ffn2_w_gu"], 0, dg.shape[0])
        gbuf["ffn2_w_down"] = _wgrad_rows(act, dfb, gbuf["ffn2_w_down"], 0)
        deps = ()
        if pending_pair is not None:
            pending_chips = finish_pair(pending_pair, dx2)
            deps = (pending_chips[3][-1],)

        dm, dya, dyb, dyc, dpost = _mix_out_bwd(dx2, m, p["mix_post"], wout, l, deps)
        small["mix_post_g"][l] = dpost[0]
        gbuf["mix_w_out"] = _wgrad_kblocks(cat, [dm], gbuf["mix_w_out"], 0)
        deps = ()
        if l == 0:
            early_pair = start_pair("0a", 0, early, gbuf)
            deps = (early_pair[3][-1],)
        dpc, dws, dbias, dlg, dlb = _sgu_bwd(pc, dyc, *sgu_b, l, deps)
        small["sgu_w_s"][l] = jnp.where(causal, dws, 0.0)
        small["sgu_b_s"][l] = dbias.reshape(CHUNK, C_W // HEAD, HEAD).sum(-1).T
        small["sgu_ln_g"][l], small["sgu_ln_b"][l] = dlg[0], dlb[0]
        dpb, dcw, dcb, ddtb, dalog, ddsk, dng = _ssd_bwd(pb, yp, sp, dyb, *p["ssd"], l)
        small["ssd_conv_w"][l], small["ssd_conv_b"][l], small["ssd_norm_g"][l] = dcw, dcb[0], dng[0]
        small["ssd_dt_bias"][l], small["ssd_a_log"][l], small["ssd_d"][l] = _per_head(ddtb[0]), _per_head(dalog[0]), _per_head(ddsk[0])
        deps = ()
        if l == 0:
            early_chips = finish_pair(early_pair, dpb)
            deps = (early_chips[3][-1],)
        dpa, dcw, dcb, dwr, dwi, dbr, dbi, dlam = _lru_bwd(pa, h, dya, *p["lru"], l, deps)
        small["lru_conv_w"][l], small["lru_conv_b"][l], small["lru_lambda"][l] = dcw, dcb[0], dlam[0]
        small["lru_b_r"][l], small["lru_b_i"][l] = dbr[0], dbi[0]
        heads = range(A_W // HEAD)
        small["lru_w_r"][l] = jnp.stack([dwr[HEAD * i:HEAD * (i + 1), HEAD * i:HEAD * (i + 1)] for i in heads])
        small["lru_w_i"][l] = jnp.stack([dwi[HEAD * i:HEAD * (i + 1), HEAD * i:HEAD * (i + 1)] for i in heads])
        dx1, dpre = _mix_in_bwd(x1, dx2, p["mix_pre"], dpa, dpb, dpc, win, l)
        small["mix_pre_g"][l] = dpre[0]
        gbuf["mix_w_in"] = _wgrad_kblocks(hbm, [dpa, dpb, dpc], gbuf["mix_w_in"], 0)

        dy, dfb, act, dg, du, dpre, dpost = _ffn_bwd(x0, dx1, f1, p["ffn1_pre"], p["ffn1_post"], g1, u1, wgu1, wd1, l)
        small["ffn1_pre_g"][l], small["ffn1_post_g"][l] = dpre[0], dpost[0]
        gbuf["ffn1_w_gu"] = _wgrad_cols(hb1, dg, gbuf["ffn1_w_gu"], 0, 0)
        gbuf["ffn1_w_gu"] = _wgrad_cols(hb1, du, gbuf["ffn1_w_gu"], 0, dg.shape[0])
        gbuf["ffn1_w_down"] = _wgrad_rows(act, dfb, gbuf["ffn1_w_down"], 0)
        if pending_chips is not None:
            finish_chips(pending_chips, dy)
        pending_pair = start_pair(f"{l}", l, late if l == 0 else BIG, gbuf)
    grad_x = dy.reshape(x.shape)

    names = REPLICATED + SHARDED_CONV
    stacked = [jnp.stack(small[n]) for n in names]
    total = _sum_devices(_all_gather([_pack(stacked)], (pending_pair[3][-1],))[0])
    finish_chips(early_chips, total)
    finish_chips(finish_pair(pending_pair, total), total)

    for n in fused:
        grads[n], delta[n], new_m[n], new_v[n] = (oriented(a, n) for a in opt_out[n])
    grads["mix_w_in"] = _narrow_w_in_grad(jnp.concatenate(w_in_grads, axis=0))
    delta["mix_w_in"], new_m["mix_w_in"], new_v["mix_w_in"] = _adamw(
        w["mix_w_in"], mom["mix_w_in"], var["mix_w_in"], grads["mix_w_in"])
    full = dict(zip(names, _unpack(total, [a.shape for a in stacked])))
    for n in REPLICATED:
        grads[n] = full[n]
    for n in SHARDED_CONV:
        cols = w[n].shape[2]
        grads[n] = lax.dynamic_slice_in_dim(full[n], dev * cols, cols, axis=2)
    shapes = [w[n].shape for n in names]
    packs = [_pack([src[n] for n in names])[None] for src in (w, mom, var, grads)]
    for dst, packed in zip((delta, new_m, new_v), _adamw(*packs)):
        dst.update(zip(names, _unpack(packed[0], shapes)))

    return (loss, grad_x, *[grads[n] for n in WEIGHTS], *[delta[n] for n in WEIGHTS],
            *[new_m[n] for n in WEIGHTS], *[new_v[n] for n in WEIGHTS])
```

```python
import functools

import jax
import jax.numpy as jnp
from jax import lax
from jax.experimental import pallas as pl
from jax.experimental.pallas import tpu as pltpu

f32, bf16 = jnp.float32, jnp.bfloat16
MESH = pl.DeviceIdType.MESH
ANY = pl.BlockSpec(memory_space=pl.ANY)

N_DEV = 8
NORM_EPS = 1e-6
LRU_C = 8.0
CHUNK = 128
HEAD = 64
A_W, B_W, C_W = 384, 384, 256
B_STATE = 128
XBC_W = B_W + 4 * B_STATE
PA_W, PB_W, PC_W = 2 * A_W, B_W + XBC_W + B_W, 2 * C_W
IN_PAD = PA_W + PB_W + PC_W
ADAM_LR, ADAM_B1, ADAM_B2, ADAM_EPS, ADAM_WD, ADAM_STEP = 0.001, 0.9, 0.999, 1e-08, 0.01, 10
VMEM_LIMIT_BYTES = 56 * 1024 * 1024
FFN_BWD_SPLIT = 2
NEG_BIG = -1e30


def _params(sem=None):
    return pltpu.CompilerParams(dimension_semantics=sem, vmem_limit_bytes=VMEM_LIMIT_BYTES)


def _nn(a, b):
    return jnp.dot(a, b, preferred_element_type=f32)


def _nt(a, b):
    return lax.dot_general(a, b, (((1,), (1,)), ((), ())), preferred_element_type=f32)


def _tn(a, b):
    return lax.dot_general(a, b, (((0,), (0,)), ((), ())), preferred_element_type=f32)


def _sigmoid(x):
    return 0.5 * jnp.tanh(0.5 * x) + 0.5


def _softplus(x):
    return jnp.maximum(x, 0.0) + jnp.log(1.0 + jnp.exp(-jnp.abs(x)))


_GELU_C0, _GELU_C1 = 0.7978845608028654, 0.044715


def _gelu(x):
    t = jnp.tanh(_GELU_C0 * (x + _GELU_C1 * x * x * x))
    return 0.5 * x * (1.0 + t)


def _gelu_grad(x):
    t = jnp.tanh(_GELU_C0 * (x + _GELU_C1 * x * x * x))
    return 0.5 * (1.0 + t) + 0.5 * x * (1.0 - t * t) * _GELU_C0 * (1.0 + 3.0 * _GELU_C1 * x * x)


def _silu_grad(x, s):
    return s * (1.0 + x * (1.0 - s))


def _rms_fwd(x, g):
    r = lax.rsqrt(jnp.mean(x * x, axis=-1, keepdims=True) + NORM_EPS)
    return x * r * g


def _rms_bwd(x, g, dy):
    r = lax.rsqrt(jnp.mean(x * x, axis=-1, keepdims=True) + NORM_EPS)
    xh = x * r
    dxh = dy * g
    dx = r * (dxh - xh * jnp.mean(dxh * xh, axis=-1, keepdims=True))
    return dx, jnp.sum(dy * xh, axis=0, keepdims=True)


def _one_minus_exp(x):
    series = -x * (1.0 + x * (0.5 + x * (1.0 / 6.0 + x * (1.0 / 24.0))))
    return jnp.where(x > -0.01, series, 1.0 - jnp.exp(x))


def _cumsum_rows(x):
    row = lax.broadcasted_iota(jnp.int32, x.shape, 0)
    d = 1
    while d < x.shape[0]:
        x = x + jnp.where(row >= d, pltpu.roll(x, d, 0), 0.0)
        d *= 2
    return x


def _tile(t, cap):
    tm = min(cap, t)
    assert t % tm == 0
    return tm


def _after(body, n_in, deps):
    def wrapped(*refs):
        return body(*refs[:n_in], *refs[n_in + len(deps):])
    return wrapped


def _lspec(a, l):
    return pl.BlockSpec((None,) + a.shape[1:], lambda *_: (l,) + (0,) * (a.ndim - 1))


def _wd_rows(wd_ref):
    return wd_ref[:, 0].reshape(2 * wd_ref.shape[2], wd_ref.shape[3])


def _ffn_fwd(x, pre_g, post_g, wgu, wd, l, deps=()):
    t, d = x.shape
    nb, _, _, h = wgu.shape
    nj = nb // 2
    tm = _tile(t, 512)

    def body(x_ref, pg_ref, qg_ref, wg_ref, wu_ref, wd_ref, y_ref, hb_ref, g_ref, u_ref, f_ref, acc_ref):
        j = pl.program_id(1)

        @pl.when(j == 0)
        def _():
            hb_ref[...] = _rms_fwd(x_ref[...], pg_ref[...]).astype(bf16)

        hb = hb_ref[...]
        g = _nn(hb, wg_ref[0, 0])
        u = _nn(hb, wu_ref[0, 0])
        g_ref[0] = g.astype(bf16)
        u_ref[0] = u.astype(bf16)
        a = (g * _sigmoid(g) * u).astype(bf16)
        part = _nn(a, _wd_rows(wd_ref))

        @pl.when(j == 0)
        def _():
            acc_ref[...] = part

        @pl.when(j > 0)
        def _():
            acc_ref[...] += part

        @pl.when(j == nj - 1)
        def _():
            f = acc_ref[...]
            f_ref[...] = f
            y_ref[...] = x_ref[...] + 0.5 * _rms_fwd(f, qg_ref[...])

    row = pl.BlockSpec((tm, d), lambda i, j: (i, 0))
    vec = pl.BlockSpec((1, d), lambda i, j: (0, 0))
    act = pl.BlockSpec((1, tm, h), lambda i, j: (j, i, 0))
    return pl.pallas_call(
        _after(body, 6, deps), name="ffn_fwd", grid=(t // tm, nj),
        in_specs=[row, _lspec(pre_g, l), _lspec(post_g, l),
                  pl.BlockSpec((1, 1, d, h), lambda i, j: (j, 0, 0, 0)),
                  pl.BlockSpec((1, 1, d, h), lambda i, j: (j + nj, 0, 0, 0)),
                  pl.BlockSpec((2, 1, h // 2, d), lambda i, j: (j, 0, 0, 0))] + [ANY] * len(deps),
        out_specs=[row, row, act, act, row],
        out_shape=[jax.ShapeDtypeStruct((t, d), f32), jax.ShapeDtypeStruct((t, d), bf16),
                   jax.ShapeDtypeStruct((nj, t, h), bf16), jax.ShapeDtypeStruct((nj, t, h), bf16),
                   jax.ShapeDtypeStruct((t, d), f32)],
        scratch_shapes=[pltpu.VMEM((tm, d), f32)],
        compiler_params=_params(("arbitrary", "arbitrary")),
    )(x, pre_g, post_g, wgu, wgu, wd, *deps)


def _ffn_bwd(x, dy, f, pre_g, post_g, g, u, wgu, wd, l, deps=()):
    t, d = x.shape
    nj, _, h = g.shape
    tm = _tile(t, 512)

    def body(x_ref, dy_ref, f_ref, pg_ref, qg_ref, g_ref, u_ref, wg_ref, wu_ref, wd_ref,
             dx_ref, dfb_ref, a_ref, dg_ref, du_ref, dpg_ref, dqg_ref, dh_ref):
        i, j = pl.program_id(0), pl.program_id(1)

        @pl.when((i == 0) & (j == 0))
        def _():
            dpg_ref[...] = jnp.zeros_like(dpg_ref)
            dqg_ref[...] = jnp.zeros_like(dqg_ref)

        @pl.when(j == 0)
        def _():
            df, dq = _rms_bwd(f_ref[...], qg_ref[...], 0.5 * dy_ref[...])
            dfb_ref[...] = df.astype(bf16)
            dqg_ref[...] += dq
            dh_ref[...] = jnp.zeros_like(dh_ref)

        wdm, wg, wu = _wd_rows(wd_ref), wg_ref[0, 0], wu_ref[0, 0]
        for half in range(FFN_BWD_SPLIT):
            rows = pl.ds(half * (tm // FFN_BWD_SPLIT), tm // FFN_BWD_SPLIT)
            da = _nt(dfb_ref[rows, :], wdm)
            gv = g_ref[0, rows, :].astype(f32)
            uv = u_ref[0, rows, :].astype(f32)
            s = _sigmoid(gv)
            sg = gv * s
            a_ref[0, rows, :] = (sg * uv).astype(bf16)
            dg = (da * uv * _silu_grad(gv, s)).astype(bf16)
            du = (da * sg).astype(bf16)
            dg_ref[0, rows, :] = dg
            du_ref[0, rows, :] = du
            dh_ref[rows, :] += _nt(dg, wg) + _nt(du, wu)

        @pl.when(j == nj - 1)
        def _():
            dxn, dp = _rms_bwd(x_ref[...], pg_ref[...], dh_ref[...])
            dx_ref[...] = dy_ref[...] + dxn
            dpg_ref[...] += dp

    row = pl.BlockSpec((tm, d), lambda i, j: (i, 0))
    vec = pl.BlockSpec((1, d), lambda i, j: (0, 0))
    act = pl.BlockSpec((1, tm, h), lambda i, j: (j, i, 0))
    act_shape = jax.ShapeDtypeStruct((nj, t, h), bf16)
    return pl.pallas_call(
        _after(body, 10, deps), name="ffn_bwd", grid=(t // tm, nj),
        in_specs=[row, row, row, _lspec(pre_g, l), _lspec(post_g, l), act, act,
                  pl.BlockSpec((1, 1, d, h), lambda i, j: (j, 0, 0, 0)),
                  pl.BlockSpec((1, 1, d, h), lambda i, j: (j + nj, 0, 0, 0)),
                  pl.BlockSpec((2, 1, h // 2, d), lambda i, j: (j, 0, 0, 0))] + [ANY] * len(deps),
        out_specs=[row, row, act, act, act, vec, vec],
        out_shape=[jax.ShapeDtypeStruct((t, d), f32), jax.ShapeDtypeStruct((t, d), bf16),
                   act_shape, act_shape, act_shape,
                   jax.ShapeDtypeStruct((1, d), f32), jax.ShapeDtypeStruct((1, d), f32)],
        scratch_shapes=[pltpu.VMEM((tm, d), f32)],
        compiler_params=_params(("arbitrary", "arbitrary")),
    )(x, dy, f, pre_g, post_g, g, u, wgu, wgu, wd, *deps)


def _wgrad_cols(x, dy, buf, l, slot0):
    (t, k), (nj, _, n) = x.shape, dy.shape

    def body(x_ref, dy_ref, buf_ref, o_ref):
        o_ref[0, 0] = _tn(dy_ref[0], x_ref[...]).astype(bf16)

    return pl.pallas_call(
        body, name="wgrad_cols", grid=(nj,),
        in_specs=[pl.BlockSpec((t, k), lambda b: (0, 0)), pl.BlockSpec((1, t, n), lambda b: (b, 0, 0)), ANY],
        out_specs=pl.BlockSpec((1, 1, n, k), lambda b: (b + slot0, l, 0, 0)),
        out_shape=jax.ShapeDtypeStruct(buf.shape, bf16), input_output_aliases={2: 0},
        compiler_params=_params(("arbitrary",)),
    )(x, dy, buf)


def _wgrad_rows(x, dy, buf, l):
    (nj, t, k), (_, n) = x.shape, dy.shape

    def body(x_ref, dy_ref, buf_ref, o_ref):
        o_ref[:, 0] = _tn(x_ref[0], dy_ref[...]).astype(bf16).reshape(2, k // 2, n)

    return pl.pallas_call(
        body, name="wgrad_rows", grid=(nj,),
        in_specs=[pl.BlockSpec((1, t, k), lambda b: (b, 0, 0)), pl.BlockSpec((t, n), lambda b: (0, 0)), ANY],
        out_specs=pl.BlockSpec((2, 1, k // 2, n), lambda b: (b, l, 0, 0)),
        out_shape=jax.ShapeDtypeStruct(buf.shape, bf16), input_output_aliases={2: 0},
        compiler_params=_params(("arbitrary",)),
    )(x, dy, buf)


def _wgrad_kblocks(x, dys, buf, l):
    t, k = x.shape
    kb = k // N_DEV
    widths = [dy.shape[1] for dy in dys]
    n = sum(widths)
    nd = len(dys)

    def body(x_ref, *refs):
        dy_hbm, o_ref, dy_vmem = refs[:nd], refs[nd + 1], refs[nd + 2:]

        @pl.when(pl.program_id(0) == 0)
        def _():
            for src, dst in zip(dy_hbm, dy_vmem):
                pltpu.sync_copy(src, dst)

        off = 0
        for dst, w in zip(dy_vmem, widths):
            o_ref[0, 0, :, off:off + w] = _tn(x_ref[...], dst[...]).astype(bf16)
            off += w

    return pl.pallas_call(
        body, name="wgrad_kblocks", grid=(N_DEV,),
        in_specs=[pl.BlockSpec((t, kb), lambda s: (0, s))] + [ANY] * (nd + 1),
        out_specs=pl.BlockSpec((1, 1, kb, n), lambda s: (s, l, 0, 0)),
        out_shape=jax.ShapeDtypeStruct(buf.shape, bf16), input_output_aliases={nd + 1: 0},
        scratch_shapes=[pltpu.VMEM((t, w), bf16) for w in widths],
        compiler_params=_params(("arbitrary",)),
    )(x, *dys, buf)


def _gathered_rows(w_ref, lo, hi):
    return w_ref[:, 0, :, lo:hi].reshape(N_DEV * w_ref.shape[2], hi - lo)


def _gathered_spec(w):
    return pl.BlockSpec((N_DEV, 1) + w.shape[2:], lambda i: (0, 0, 0, 0))


def _mix_in_fwd(x, pre_g, w_in, l):
    t, d = x.shape
    tm = _tile(t, 512)

    def body(x_ref, g_ref, w_ref, hb_ref, pa_ref, pb_ref, pc_ref):
        hb = _rms_fwd(x_ref[...], g_ref[...]).astype(bf16)
        hb_ref[...] = hb
        pa_ref[...] = _nn(hb, _gathered_rows(w_ref, 0, PA_W))
        pb_ref[...] = _nn(hb, _gathered_rows(w_ref, PA_W, PA_W + PB_W))
        pc_ref[...] = _nn(hb, _gathered_rows(w_ref, PA_W + PB_W, IN_PAD))

    def row(w):
        return pl.BlockSpec((tm, w), lambda i: (i, 0))

    return pl.pallas_call(
        body, name="mix_in_fwd", grid=(t // tm,),
        in_specs=[row(d), _lspec(pre_g, l), _gathered_spec(w_in)],
        out_specs=[row(d), row(PA_W), row(PB_W), row(PC_W)],
        out_shape=[jax.ShapeDtypeStruct((t, d), bf16), jax.ShapeDtypeStruct((t, PA_W), f32),
                   jax.ShapeDtypeStruct((t, PB_W), f32), jax.ShapeDtypeStruct((t, PC_W), f32)],
        compiler_params=_params(("arbitrary",)),
    )(x, pre_g, w_in)


def _mix_in_bwd(x, dy, pre_g, dpa, dpb, dpc, w_in, l):
    t, d = x.shape
    tm = _tile(t, 512)

    def body(x_ref, dy_ref, g_ref, dpa_ref, dpb_ref, dpc_ref, w_ref, dx_ref, dg_ref):
        @pl.when(pl.program_id(0) == 0)
        def _():
            dg_ref[...] = jnp.zeros_like(dg_ref)

        dh = (_nt(dpa_ref[...], _gathered_rows(w_ref, 0, PA_W))
              + _nt(dpb_ref[...], _gathered_rows(w_ref, PA_W, PA_W + PB_W))
              + _nt(dpc_ref[...], _gathered_rows(w_ref, PA_W + PB_W, IN_PAD)))
        dxn, dg = _rms_bwd(x_ref[...], g_ref[...], dh)
        dx_ref[...] = dy_ref[...] + dxn
        dg_ref[...] += dg

    def row(w):
        return pl.BlockSpec((tm, w), lambda i: (i, 0))

    vec = pl.BlockSpec((1, d), lambda i: (0, 0))
    return pl.pallas_call(
        body, name="mix_in_bwd", grid=(t // tm,),
        in_specs=[row(d), row(d), _lspec(pre_g, l), row(PA_W), row(PB_W), row(PC_W), _gathered_spec(w_in)],
        out_specs=[row(d), vec],
        out_shape=[jax.ShapeDtypeStruct((t, d), f32), jax.ShapeDtypeStruct((1, d), f32)],
        compiler_params=_params(("arbitrary",)),
    )(x, dy, pre_g, dpa, dpb, dpc, w_in)


def _mix_out_fwd(x, ya, yb, yc, post_g, w_out, l):
    t, d = x.shape
    tm = _tile(t, 512)

    def body(x_ref, ya_ref, yb_ref, yc_ref, g_ref, w_ref, y_ref, cat_ref, m_ref):
        cat_ref[:, 0:A_W] = ya_ref[...].astype(bf16)
        cat_ref[:, A_W:A_W + B_W] = yb_ref[...].astype(bf16)
        cat_ref[:, A_W + B_W:d] = yc_ref[...].astype(bf16)
        m = _nn(cat_ref[...], _gathered_rows(w_ref, 0, d))
        m_ref[...] = m
        y_ref[...] = x_ref[...] + _rms_fwd(m, g_ref[...])

    def row(w):
        return pl.BlockSpec((tm, w), lambda i: (i, 0))

    return pl.pallas_call(
        body, name="mix_out_fwd", grid=(t // tm,),
        in_specs=[row(d), row(A_W), row(B_W), row(C_W), _lspec(post_g, l), _gathered_spec(w_out)],
        out_specs=[row(d), row(d), row(d)],
        out_shape=[jax.ShapeDtypeStruct((t, d), f32), jax.ShapeDtypeStruct((t, d), bf16), jax.ShapeDtypeStruct((t, d), f32)],
        compiler_params=_params(("arbitrary",)),
    )(x, ya, yb, yc, post_g, w_out)


def _mix_out_bwd(dy, m, post_g, w_out, l, deps=()):
    t, d = m.shape
    tm = _tile(t, 512)

    def body(dy_ref, m_ref, g_ref, w_ref, dm_ref, dya_ref, dyb_ref, dyc_ref, dg_ref):
        @pl.when(pl.program_id(0) == 0)
        def _():
            dg_ref[...] = jnp.zeros_like(dg_ref)

        dm, dg = _rms_bwd(m_ref[...], g_ref[...], dy_ref[...])
        dmb = dm.astype(bf16)
        dm_ref[...] = dmb
        dg_ref[...] += dg
        dcat = _nt(dmb, _gathered_rows(w_ref, 0, d))
        dya_ref[...] = dcat[:, 0:A_W]
        dyb_ref[...] = dcat[:, A_W:A_W + B_W]
        dyc_ref[...] = dcat[:, A_W + B_W:d]

    def row(w):
        return pl.BlockSpec((tm, w), lambda i: (i, 0))

    vec = pl.BlockSpec((1, d), lambda i: (0, 0))
    return pl.pallas_call(
        _after(body, 4, deps), name="mix_out_bwd", grid=(t // tm,),
        in_specs=[row(d), row(d), _lspec(post_g, l), _gathered_spec(w_out)] + [ANY] * len(deps),
        out_specs=[row(d), row(A_W), row(B_W), row(C_W), vec],
        out_shape=[jax.ShapeDtypeStruct((t, d), bf16), jax.ShapeDtypeStruct((t, A_W), f32),
                   jax.ShapeDtypeStruct((t, B_W), f32), jax.ShapeDtypeStruct((t, C_W), f32),
                   jax.ShapeDtypeStruct((1, d), f32)],
        compiler_params=_params(("arbitrary",)),
    )(dy, m, post_g, w_out, *deps)


def _conv_fwd(buf_ref, halo, x, w, b, n):
    buf_ref[0:8, :] = halo
    buf_ref[8:8 + n, :] = x
    out = b + w[3:4, :] * x
    for k in range(3):
        out = out + w[k:k + 1, :] * buf_ref[pl.ds(5 + k, n), :]
    return out


def _conv_bwd(buf_ref, dbuf_ref, dout, dnext, w, n):
    dbuf_ref[0:n, :] = dout
    dbuf_ref[n:n + 8, :] = dnext
    dx = w[3:4, :] * dout
    dws = []
    for k in range(3):
        dx = dx + w[k:k + 1, :] * dbuf_ref[pl.ds(3 - k, n), :]
        dws.append(jnp.sum(dout * buf_ref[pl.ds(5 + k, n), :], axis=0, keepdims=True))
    dws.append(jnp.sum(dout * buf_ref[pl.ds(8, n), :], axis=0, keepdims=True))
    return dx, jnp.concatenate(dws, axis=0), jnp.sum(dout, axis=0, keepdims=True)


def _lru_gates(rec, wr, wi, br, bi, lam):
    rb = rec.astype(bf16)
    r = _sigmoid(_nn(rb, wr) + br)
    ig = _sigmoid(_nn(rb, wi) + bi)
    sp = _softplus(-lam)
    la = -LRU_C * r * sp
    a = jnp.exp(la)
    mult = jnp.sqrt(_one_minus_exp(2.0 * la))
    return rb, r, ig, sp, a, mult


def _scan_rows(a_ref, b_ref, o_ref, carry, n, reverse):
    row = lax.broadcasted_iota(jnp.int32, (8, a_ref.shape[1]), 0)
    nb = n // 8

    def step(k, carry):
        blk = (nb - 1 - k) if reverse else k
        rows = pl.ds(pl.multiple_of(blk * 8, 8), 8)
        a, b = a_ref[rows, :], b_ref[rows, :]
        for d in (1, 2, 4):
            shift = 8 - d if reverse else d
            keep = (row < 8 - d) if reverse else (row >= d)
            b = a * jnp.where(keep, pltpu.roll(b, shift, 0), 0.0) + b
            a = a * jnp.where(keep, pltpu.roll(a, shift, 0), 1.0)
        o = a * carry + b
        o_ref[rows, :] = o
        return o[0:1, :] if reverse else o[7:8, :]

    return lax.fori_loop(0, nb, step, carry, unroll=2)


def _lru_fwd(pa, conv_w, conv_b, wr, wi, br, bi, lam, l):
    t = pa.shape[0]
    tc = _tile(t, 512)

    def body(pa_ref, halo_ref, cw_ref, cb_ref, wr_ref, wi_ref, br_ref, bi_ref, lam_ref,
             ya_ref, h_ref, buf_ref, a_ref, u_ref, carry_ref):
        i = pl.program_id(0)

        @pl.when(i == 0)
        def _():
            carry_ref[...] = jnp.zeros_like(carry_ref)

        halo = jnp.where(i > 0, halo_ref[:, A_W:PA_W], 0.0)
        rec = _conv_fwd(buf_ref, halo, pa_ref[:, A_W:PA_W], cw_ref[...], cb_ref[...], tc)
        _, _, ig, _, a, mult = _lru_gates(rec, wr_ref[...], wi_ref[...], br_ref[...], bi_ref[...], lam_ref[...])
        a_ref[...] = a
        u_ref[...] = mult * (ig * rec)

        carry_ref[...] = _scan_rows(a_ref, u_ref, h_ref, carry_ref[...], tc, reverse=False)
        ya_ref[...] = h_ref[...] * _gelu(pa_ref[:, 0:A_W])

    vec = pl.BlockSpec((1, A_W), lambda i: (0, 0))
    mat = pl.BlockSpec((A_W, A_W), lambda i: (0, 0))
    row = pl.BlockSpec((tc, A_W), lambda i: (i, 0))
    return pl.pallas_call(
        body, name="lru_fwd", grid=(t // tc,),
        in_specs=[pl.BlockSpec((tc, PA_W), lambda i: (i, 0)),
                  pl.BlockSpec((8, PA_W), lambda i: (jnp.maximum(i * (tc // 8) - 1, 0), 0)),
                  *[_lspec(a, l) for a in (conv_w, conv_b, wr, wi, br, bi, lam)]],
        out_specs=[row, row],
        out_shape=[jax.ShapeDtypeStruct((t, A_W), f32), jax.ShapeDtypeStruct((t, A_W), f32)],
        scratch_shapes=[pltpu.VMEM((8 + tc, A_W), f32), pltpu.VMEM((tc, A_W), f32), pltpu.VMEM((tc, A_W), f32),
                        pltpu.VMEM((1, A_W), f32)],
        compiler_params=_params(("arbitrary",)),
    )(pa, pa, conv_w, conv_b, wr, wi, br, bi, lam)


def _lru_bwd(pa, h, dya, conv_w, conv_b, wr, wi, br, bi, lam, l, deps=()):
    t = pa.shape[0]
    tc = _tile(t, 512)
    nc = t // tc

    def body(pa_ref, halo_ref, h_ref, hhalo_ref, dya_ref, cw_ref, cb_ref, wr_ref, wi_ref, br_ref, bi_ref, lam_ref,
             dpa_ref, dcw_ref, dcb_ref, dwr_ref, dwi_ref, dbr_ref, dbi_ref, dlam_ref,
             buf_ref, dbuf_ref, hbuf_ref, a_ref, g_ref, dh_ref, carry_ref, dnext_ref, dhbuf_ref):
        i = pl.program_id(0)
        c = nc - 1 - i

        @pl.when(i == 0)
        def _():
            carry_ref[...] = jnp.zeros_like(carry_ref)
            dnext_ref[...] = jnp.zeros_like(dnext_ref)
            for ref in (dcw_ref, dcb_ref, dwr_ref, dwi_ref, dbr_ref, dbi_ref, dlam_ref):
                ref[...] = jnp.zeros_like(ref)

        halo = jnp.where(c > 0, halo_ref[:, A_W:PA_W], 0.0)
        cw = cw_ref[...]
        rec = _conv_fwd(buf_ref, halo, pa_ref[:, A_W:PA_W], cw, cb_ref[...], tc)
        lam = lam_ref[...]
        rb, r, ig, sp, a, mult = _lru_gates(rec, wr_ref[...], wi_ref[...], br_ref[...], bi_ref[...], lam)
        hbuf_ref[0:8, :] = jnp.where(c > 0, hhalo_ref[...], 0.0)
        hbuf_ref[8:8 + tc, :] = h_ref[...]
        h_prev = hbuf_ref[pl.ds(7, tc), :]
        gate = pa_ref[:, 0:A_W]
        dya = dya_ref[...]
        dpa_ref[:, 0:A_W] = (dya * h_ref[...] * _gelu_grad(gate)).astype(bf16)
        a_ref[...] = a
        gg = dya * _gelu(gate)
        g_ref[...] = a * gg
        carry_in = carry_ref[...]
        carry_ref[...] = _scan_rows(a_ref, g_ref, dh_ref, carry_in, tc, reverse=True)
        dhbuf_ref[0:tc, :] = dh_ref[...]
        dhbuf_ref[tc:tc + 8, :] = jnp.broadcast_to(carry_in, (8, A_W))
        dh = gg + dhbuf_ref[pl.ds(1, tc), :]
        da = dh * h_prev
        dmult = dh * ig * rec
        dig = dh * mult * rec
        drec = dh * mult * ig
        dla = da * a - dmult * (a * a) / mult
        dr = dla * (-LRU_C * sp)
        dsp = jnp.sum(dla * (-LRU_C * r), axis=0, keepdims=True)
        dlam_ref[...] += dsp * (-_sigmoid(-lam))
        dpr = (dr * r * (1.0 - r))
        dpi = (dig * ig * (1.0 - ig))
        dprb, dpib = dpr.astype(bf16), dpi.astype(bf16)
        drec = drec + _nt(dprb, wr_ref[...]) + _nt(dpib, wi_ref[...])
        dwr_ref[...] += _tn(rb, dprb)
        dwi_ref[...] += _tn(rb, dpib)
        dbr_ref[...] += jnp.sum(dpr, axis=0, keepdims=True)
        dbi_ref[...] += jnp.sum(dpi, axis=0, keepdims=True)
        dx, dw, db = _conv_bwd(buf_ref, dbuf_ref, drec, dnext_ref[...], cw, tc)
        dnext_ref[...] = drec[0:8, :]
        dcw_ref[...] += dw
        dcb_ref[...] += db
        dpa_ref[:, A_W:PA_W] = dx.astype(bf16)

    vec = pl.BlockSpec((1, A_W), lambda i: (0, 0))
    mat = pl.BlockSpec((A_W, A_W), lambda i: (0, 0))
    cwspec = pl.BlockSpec((4, A_W), lambda i: (0, 0))

    def rev(w):
        return pl.BlockSpec((tc, w), lambda i: (nc - 1 - i, 0))

    def halo(w):
        return pl.BlockSpec((8, w), lambda i: (jnp.maximum((nc - 1 - i) * (tc // 8) - 1, 0), 0))

    chunk = pltpu.VMEM((tc, A_W), f32)
    return pl.pallas_call(
        _after(body, 12, deps), name="lru_bwd", grid=(nc,),
        in_specs=[rev(PA_W), halo(PA_W), rev(A_W), halo(A_W), rev(A_W),
                  *[_lspec(a, l) for a in (conv_w, conv_b, wr, wi, br, bi, lam)]] + [ANY] * len(deps),
        out_specs=[rev(PA_W), cwspec, vec, mat, mat, vec, vec, vec],
        out_shape=[jax.ShapeDtypeStruct((t, PA_W), bf16), jax.ShapeDtypeStruct((4, A_W), f32),
                   jax.ShapeDtypeStruct((1, A_W), f32), jax.ShapeDtypeStruct((A_W, A_W), f32),
                   jax.ShapeDtypeStruct((A_W, A_W), f32), jax.ShapeDtypeStruct((1, A_W), f32),
                   jax.ShapeDtypeStruct((1, A_W), f32), jax.ShapeDtypeStruct((1, A_W), f32)],
        scratch_shapes=[pltpu.VMEM((8 + tc, A_W), f32), pltpu.VMEM((tc + 8, A_W), f32), pltpu.VMEM((8 + tc, A_W), f32),
                        chunk, chunk, chunk, pltpu.VMEM((1, A_W), f32), pltpu.VMEM((8, A_W), f32),
                        pltpu.VMEM((tc + 8, A_W), f32)],
        compiler_params=_params(("arbitrary",)),
    )(pa, pa, h, h, dya, conv_w, conv_b, wr, wi, br, bi, lam, *deps)


def _sgu_norm(v, g, b):
    mu = jnp.mean(v, axis=-1, keepdims=True)
    vc = v - mu
    rstd = lax.rsqrt(jnp.mean(vc * vc, axis=-1, keepdims=True) + NORM_EPS)
    vh = vc * rstd
    return vh, rstd, vh * g + b


def _sgu_mix(w_ref, vb, bias):
    grp = lax.broadcasted_iota(jnp.int32, (CHUNK, C_W), 1) // HEAD
    out = bias
    for gi in range(C_W // HEAD):
        out = out + jnp.where(grp == gi, _nn(w_ref[gi], vb), 0.0)
    return out


def _sgu_fwd(pc, ln_g, ln_b, wm, bias, l):
    t = pc.shape[0]
    tm = _tile(t, 512)

    def body(pc_ref, g_ref, b_ref, w_ref, bias_ref, yc_ref):
        for ci in range(tm // CHUNK):
            rows = pl.ds(ci * CHUNK, CHUNK)
            ge = _gelu(pc_ref[rows, :])
            _, _, vn = _sgu_norm(ge[:, C_W:PC_W], g_ref[...], b_ref[...])
            yc_ref[rows, :] = ge[:, 0:C_W] * _sgu_mix(w_ref, vn.astype(bf16), bias_ref[...])

    vec = pl.BlockSpec((1, C_W), lambda i: (0, 0))
    return pl.pallas_call(
        body, name="sgu_fwd", grid=(t // tm,),
        in_specs=[pl.BlockSpec((tm, PC_W), lambda i: (i, 0)), *[_lspec(a, l) for a in (ln_g, ln_b, wm, bias)]],
        out_specs=pl.BlockSpec((tm, C_W), lambda i: (i, 0)),
        out_shape=jax.ShapeDtypeStruct((t, C_W), f32),
        compiler_params=_params(("arbitrary",)),
    )(pc, ln_g, ln_b, wm, bias)


def _sgu_bwd(pc, dyc, ln_g, ln_b, wm, wmt, bias, l, deps=()):
    t = pc.shape[0]
    tm = _tile(t, 512)

    def body(pc_ref, dyc_ref, g_ref, b_ref, w_ref, wt_ref, bias_ref, dpc_ref, dw_ref, dbias_ref, dg_ref, db_ref):
        @pl.when(pl.program_id(0) == 0)
        def _():
            for ref in (dw_ref, dbias_ref, dg_ref, db_ref):
                ref[...] = jnp.zeros_like(ref)

        grp = lax.broadcasted_iota(jnp.int32, (CHUNK, C_W), 1) // HEAD
        for ci in range(tm // CHUNK):
            rows = pl.ds(ci * CHUNK, CHUNK)
            x = pc_ref[rows, :]
            ge = _gelu(x)
            gv = g_ref[...]
            vh, rstd, vn = _sgu_norm(ge[:, C_W:PC_W], gv, b_ref[...])
            vb = vn.astype(bf16)
            mixed = _sgu_mix(w_ref, vb, bias_ref[...])
            dyc = dyc_ref[rows, :]
            du = dyc * mixed
            dmix = dyc * ge[:, 0:C_W]
            dmb = dmix.astype(bf16)
            dvn = jnp.zeros((CHUNK, C_W), f32)
            for gi in range(C_W // HEAD):
                dvn = dvn + jnp.where(grp == gi, _nn(wt_ref[gi], dmb), 0.0)
                dw_ref[gi] += _nt(jnp.where(grp == gi, dmix, 0.0).astype(bf16), vb)
            dbias_ref[...] += dmix
            dg_ref[...] += jnp.sum(dvn * vh, axis=0, keepdims=True)
            db_ref[...] += jnp.sum(dvn, axis=0, keepdims=True)
            dvh = dvn * gv
            dv = rstd * (dvh - jnp.mean(dvh, axis=-1, keepdims=True) - vh * jnp.mean(dvh * vh, axis=-1, keepdims=True))
            gg = _gelu_grad(x)
            dpc_ref[rows, 0:C_W] = (du * gg[:, 0:C_W]).astype(bf16)
            dpc_ref[rows, C_W:PC_W] = (dv * gg[:, C_W:PC_W]).astype(bf16)

    vec = pl.BlockSpec((1, C_W), lambda i: (0, 0))
    wspec = pl.BlockSpec((4, CHUNK, CHUNK), lambda i: (0, 0, 0))
    bspec = pl.BlockSpec((CHUNK, C_W), lambda i: (0, 0))
    return pl.pallas_call(
        _after(body, 7, deps), name="sgu_bwd", grid=(t // tm,),
        in_specs=[pl.BlockSpec((tm, PC_W), lambda i: (i, 0)), pl.BlockSpec((tm, C_W), lambda i: (i, 0)),
                  *[_lspec(a, l) for a in (ln_g, ln_b, wm, wmt, bias)]] + [ANY] * len(deps),
        out_specs=[pl.BlockSpec((tm, PC_W), lambda i: (i, 0)), wspec, bspec, vec, vec],
        out_shape=[jax.ShapeDtypeStruct((t, PC_W), bf16), jax.ShapeDtypeStruct((4, CHUNK, CHUNK), f32),
                   jax.ShapeDtypeStruct((CHUNK, C_W), f32), jax.ShapeDtypeStruct((1, C_W), f32),
                   jax.ShapeDtypeStruct((1, C_W), f32)],
        compiler_params=_params(("arbitrary",)),
    )(pc, dyc, ln_g, ln_b, wm, wmt, bias, *deps)


N_PAIR = B_W // 128
HEADS_PER_GROUP = 3


def _pair_groups(p):
    return (2 * p) // HEADS_PER_GROUP, (2 * p + 1) // HEADS_PER_GROUP


def _ssd_chunk(pb_ref, halo, buf_ref, cw, cb, dtb, alog):
    z = pb_ref[:, 0:B_W]
    pre = _conv_fwd(buf_ref, halo, pb_ref[:, B_W:B_W + XBC_W], cw, cb, CHUNK)
    sg = _sigmoid(pre)
    xbc = pre * sg
    xs = xbc[:, 0:B_W]
    bm = [xbc[:, B_W + k * B_STATE:B_W + (k + 1) * B_STATE] for k in range(2)]
    cm = [xbc[:, B_W + (2 + k) * B_STATE:B_W + (3 + k) * B_STATE] for k in range(2)]
    dtin = pb_ref[:, B_W + XBC_W:PB_W] + dtb
    dt = _softplus(dtin)
    a = -jnp.exp(alog)
    cs = _cumsum_rows(dt * a)
    return dict(z=z, pre=pre, sg=sg, xs=xs, bm=bm, cm=cm, dtin=dtin, dt=dt, a=a, cs=cs,
                ecs=jnp.exp(cs), ds=jnp.exp(cs[CHUNK - 1:CHUNK, :] - cs), xdt=xs * dt,
                bmb=[v.astype(bf16) for v in bm], cmb=[v.astype(bf16) for v in cm])


def _ssd_decay(cs_pair, half):
    cst = cs_pair.T
    lane0 = HEAD * half
    csc = jnp.broadcast_to(cs_pair[:, lane0:lane0 + 1], (CHUNK, CHUNK))
    csr = cst[lane0:lane0 + 1, :]
    tri = lax.broadcasted_iota(jnp.int32, (CHUNK, CHUNK), 0) >= lax.broadcasted_iota(jnp.int32, (CHUNK, CHUNK), 1)
    return jnp.exp(jnp.where(tri, csc - csr, NEG_BIG)), cst


def _ssd_fwd(pb, conv_w, conv_b, dtb, alog, dskip, norm_g, l):
    t = pb.shape[0]
    nc = t // CHUNK

    def body(pb_ref, halo_ref, cw_ref, cb_ref, dtb_ref, alog_ref, d_ref, ng_ref, yb_ref, yp_ref, sp_ref, buf_ref, s_ref):
        i = pl.program_id(0)

        @pl.when(i == 0)
        def _():
            s_ref[...] = jnp.zeros_like(s_ref)

        halo = jnp.where(i > 0, halo_ref[:, B_W:B_W + XBC_W], 0.0)
        q = _ssd_chunk(pb_ref, halo, buf_ref, cw_ref[...], cb_ref[...], dtb_ref[...], alog_ref[...])
        sp_ref[0] = s_ref[...]
        lane = lax.broadcasted_iota(jnp.int32, (CHUNK, 128), 1)
        rowi = lax.broadcasted_iota(jnp.int32, (128, B_STATE), 0)
        cb_mat = [_nt(q["cmb"][k], q["bmb"][k]) for k in range(2)]
        xd = q["xdt"] * q["ds"]
        for p in range(N_PAIR):
            cols = slice(128 * p, 128 * (p + 1))
            g_lo, g_hi = _pair_groups(p)
            cs_p, xdt_p = q["cs"][:, cols], q["xdt"][:, cols]
            s_p = s_ref[cols, :]
            s_pb = s_p.astype(bf16)
            y_p = jnp.zeros((CHUNK, 128), f32)
            for half, grp in ((0, g_lo), (1, g_hi)):
                lm, cst = _ssd_decay(cs_p, half)
                mb = (cb_mat[grp] * lm).astype(bf16)
                sel = (lane < HEAD) if half == 0 else (lane >= HEAD)
                y_p = y_p + _nn(mb, jnp.where(sel, xdt_p, 0.0).astype(bf16))
            off_lo = _nt(q["cmb"][g_lo], s_pb)
            off = off_lo if g_lo == g_hi else jnp.where(lane < HEAD, off_lo, _nt(q["cmb"][g_hi], s_pb))
            y_p = y_p + off * q["ecs"][:, cols] + q["xs"][:, cols] * d_ref[:, cols]
            yp_ref[:, cols] = y_p
            xd_pb = xd[:, cols].astype(bf16)
            upd_lo = _tn(xd_pb, q["bmb"][g_lo])
            upd = upd_lo if g_lo == g_hi else jnp.where(rowi < HEAD, upd_lo, _tn(xd_pb, q["bmb"][g_hi]))
            cd = jnp.exp(jnp.broadcast_to(cst[:, CHUNK - 1:CHUNK], (128, B_STATE)))
            s_ref[cols, :] = cd * s_p + upd
        z = q["z"]
        yg = yp_ref[...] * (z * _sigmoid(z))
        yb_ref[...] = _rms_fwd(yg, ng_ref[...])

    vec = pl.BlockSpec((1, B_W), lambda i: (0, 0))
    row = pl.BlockSpec((CHUNK, B_W), lambda i: (i, 0))
    return pl.pallas_call(
        body, name="ssd_fwd", grid=(nc,),
        in_specs=[pl.BlockSpec((CHUNK, PB_W), lambda i: (i, 0)),
                  pl.BlockSpec((8, PB_W), lambda i: (jnp.maximum(i * (CHUNK // 8) - 1, 0), 0)),
                  *[_lspec(a, l) for a in (conv_w, conv_b, dtb, alog, dskip, norm_g)]],
        out_specs=[row, row, pl.BlockSpec((1, B_W, B_STATE), lambda i: (i, 0, 0))],
        out_shape=[jax.ShapeDtypeStruct((t, B_W), f32), jax.ShapeDtypeStruct((t, B_W), f32),
                   jax.ShapeDtypeStruct((nc, B_W, B_STATE), f32)],
        scratch_shapes=[pltpu.VMEM((8 + CHUNK, XBC_W), f32), pltpu.VMEM((B_W, B_STATE), f32)],
        compiler_params=_params(("arbitrary",)),
    )(pb, pb, conv_w, conv_b, dtb, alog, dskip, norm_g)


def _ssd_bwd(pb, yp, sprev, dyb, conv_w, conv_b, dtb, alog, dskip, norm_g, l):
    t = pb.shape[0]
    nc = t // CHUNK

    def body(pb_ref, halo_ref, yp_ref, sp_ref, dyb_ref, cw_ref, cb_ref, dtb_ref, alog_ref, d_ref, ng_ref,
             dpb_ref, dcw_ref, dcb_ref, ddtb_ref, dalog_ref, dd_ref, dng_ref,
             buf_ref, dbuf_ref, ds_ref, dnext_ref, dxbc_ref, dcs_ref, dxdt_ref):
        i = pl.program_id(0)
        c = nc - 1 - i

        @pl.when(i == 0)
        def _():
            ds_ref[...] = jnp.zeros_like(ds_ref)
            dnext_ref[...] = jnp.zeros_like(dnext_ref)
            for ref in (dcw_ref, dcb_ref, ddtb_ref, dalog_ref, dd_ref, dng_ref):
                ref[...] = jnp.zeros_like(ref)

        halo = jnp.where(c > 0, halo_ref[:, B_W:B_W + XBC_W], 0.0)
        cw = cw_ref[...]
        q = _ssd_chunk(pb_ref, halo, buf_ref, cw, cb_ref[...], dtb_ref[...], alog_ref[...])
        z, xs, dt, a, ecs, dsd, xdt = q["z"], q["xs"], q["dt"], q["a"], q["ecs"], q["ds"], q["xdt"]
        sz = _sigmoid(z)
        siluz = z * sz
        yp = yp_ref[...]
        dyg, dng = _rms_bwd(yp * siluz, ng_ref[...], dyb_ref[...])
        dng_ref[...] += dng
        dy = dyg * siluz
        dpb_ref[:, 0:B_W] = (dyg * yp * _silu_grad(z, sz)).astype(bf16)
        dd_ref[...] += jnp.sum(dy * xs, axis=0, keepdims=True)
        g1 = dy * ecs
        lane = lax.broadcasted_iota(jnp.int32, (CHUNK, 128), 1)
        rowi = lax.broadcasted_iota(jnp.int32, (128, B_STATE), 0)
        rowc = lax.broadcasted_iota(jnp.int32, (CHUNK, 128), 0)
        cb_mat = [_nt(q["cmb"][k], q["bmb"][k]) for k in range(2)]
        d_cb = [jnp.zeros((CHUNK, CHUNK), f32) for _ in range(2)]
        d_b = [jnp.zeros((CHUNK, B_STATE), f32) for _ in range(2)]
        d_c = [jnp.zeros((CHUNK, B_STATE), f32) for _ in range(2)]
        for p in range(N_PAIR):
            cols = slice(128 * p, 128 * (p + 1))
            g_lo, g_hi = _pair_groups(p)
            lo, hi = lane < HEAD, lane >= HEAD
            cs_p, xdt_p, dy_p, ds_p, g1_p = q["cs"][:, cols], xdt[:, cols], dy[:, cols], dsd[:, cols], g1[:, cols]
            s_p = sp_ref[0, cols, :]
            s_pb = s_p.astype(bf16)
            dsn = ds_ref[cols, :]
            dsnb = dsn.astype(bf16)
            g1b = g1_p.astype(bf16)
            off_lo = _nt(q["cmb"][g_lo], s_pb)
            off = off_lo if g_lo == g_hi else jnp.where(lo, off_lo, _nt(q["cmb"][g_hi], s_pb))
            dcs_p = dy_p * off * ecs[:, cols]
            dsp_lo = _tn(g1b, q["cmb"][g_lo])
            dsp = dsp_lo if g_lo == g_hi else jnp.where(rowi < HEAD, dsp_lo, _tn(g1b, q["cmb"][g_hi]))
            dx_lo = _nt(q["bmb"][g_lo], dsnb)
            dxd = dx_lo if g_lo == g_hi else jnp.where(lo, dx_lo, _nt(q["bmb"][g_hi], dsnb))
            xd_p = xdt_p * ds_p
            if g_lo == g_hi:
                d_c[g_lo] = d_c[g_lo] + _nn(g1b, s_pb)
                d_b[g_lo] = d_b[g_lo] + _nn(xd_p.astype(bf16), dsnb)
            else:
                d_c[g_lo] = d_c[g_lo] + _nn(jnp.where(lo, g1_p, 0.0).astype(bf16), s_pb)
                d_c[g_hi] = d_c[g_hi] + _nn(jnp.where(hi, g1_p, 0.0).astype(bf16), s_pb)
                d_b[g_lo] = d_b[g_lo] + _nn(jnp.where(lo, xd_p, 0.0).astype(bf16), dsnb)
                d_b[g_hi] = d_b[g_hi] + _nn(jnp.where(hi, xd_p, 0.0).astype(bf16), dsnb)
            dxdt_p = dxd * ds_p
            t2 = dxd * xdt_p * ds_p
            dcs_p = dcs_p - t2
            dlast = jnp.sum(t2, axis=0, keepdims=True)
            cst = None
            for half, grp in ((0, g_lo), (1, g_hi)):
                sel = lo if half == 0 else hi
                lm, cst = _ssd_decay(cs_p, half)
                m = cb_mat[grp] * lm
                dyh = jnp.where(sel, dy_p, 0.0).astype(bf16)
                xdh = jnp.where(sel, xdt_p, 0.0).astype(bf16)
                dm = _nt(dyh, xdh)
                pm = dm * m
                col = jnp.sum(pm, axis=1, keepdims=True) - jnp.sum(pm.T, axis=1, keepdims=True)
                dcs_p = dcs_p + jnp.where(lane == HEAD * half, col, 0.0)
                d_cb[grp] = d_cb[grp] + dm * lm
                dxdt_p = dxdt_p + _tn(m.astype(bf16), dyh)
            cdcol = jnp.exp(jnp.broadcast_to(cst[:, CHUNK - 1:CHUNK], (128, B_STATE)))
            ds_ref[cols, :] = cdcol * dsn + dsp
            dcd_row = jnp.sum((dsn * s_p).T, axis=0, keepdims=True)
            dlast = dlast + dcd_row * ecs[CHUNK - 1:CHUNK, cols]
            dcs_ref[:, cols] = dcs_p + jnp.where(rowc == CHUNK - 1, dlast, 0.0)
            dxdt_ref[:, cols] = dxdt_p
        for k in range(2):
            dcbb = d_cb[k].astype(bf16)
            d_c[k] = d_c[k] + _nn(dcbb, q["bmb"][k])
            d_b[k] = d_b[k] + _tn(dcbb, q["cmb"][k])
            dxbc_ref[:, B_W + k * B_STATE:B_W + (k + 1) * B_STATE] = d_b[k]
            dxbc_ref[:, B_W + (2 + k) * B_STATE:B_W + (3 + k) * B_STATE] = d_c[k]
        dxdt = dxdt_ref[...]
        dxbc_ref[:, 0:B_W] = dy * d_ref[...] + dxdt * dt
        dcs = dcs_ref[...]
        dad = jnp.sum(dcs, axis=0, keepdims=True) - _cumsum_rows(dcs) + dcs
        ddt = dxdt * xs + dad * a
        dalog_ref[...] += jnp.sum(dad * dt, axis=0, keepdims=True) * a
        dtraw = ddt * _sigmoid(q["dtin"])
        ddtb_ref[...] += jnp.sum(dtraw, axis=0, keepdims=True)
        dpb_ref[:, B_W + XBC_W:PB_W] = dtraw.astype(bf16)
        dpre = dxbc_ref[...] * _silu_grad(q["pre"], q["sg"])
        dx, dw, db = _conv_bwd(buf_ref, dbuf_ref, dpre, dnext_ref[...], cw, CHUNK)
        dnext_ref[...] = dpre[0:8, :]
        dcw_ref[...] += dw
        dcb_ref[...] += db
        dpb_ref[:, B_W:B_W + XBC_W] = dx.astype(bf16)

    vec = pl.BlockSpec((1, B_W), lambda i: (0, 0))
    cwspec = pl.BlockSpec((4, XBC_W), lambda i: (0, 0))
    cbspec = pl.BlockSpec((1, XBC_W), lambda i: (0, 0))

    def rev(w):
        return pl.BlockSpec((CHUNK, w), lambda i: (nc - 1 - i, 0))

    vshape = jax.ShapeDtypeStruct((1, B_W), f32)
    return pl.pallas_call(
        body, name="ssd_bwd", grid=(nc,),
        in_specs=[rev(PB_W), pl.BlockSpec((8, PB_W), lambda i: (jnp.maximum((nc - 1 - i) * (CHUNK // 8) - 1, 0), 0)),
                  rev(B_W), pl.BlockSpec((1, B_W, B_STATE), lambda i: (nc - 1 - i, 0, 0)), rev(B_W),
                  *[_lspec(a, l) for a in (conv_w, conv_b, dtb, alog, dskip, norm_g)]],
        out_specs=[rev(PB_W), cwspec, cbspec, vec, vec, vec, vec],
        out_shape=[jax.ShapeDtypeStruct((t, PB_W), bf16), jax.ShapeDtypeStruct((4, XBC_W), f32),
                   jax.ShapeDtypeStruct((1, XBC_W), f32), vshape, vshape, vshape, vshape],
        scratch_shapes=[pltpu.VMEM((8 + CHUNK, XBC_W), f32), pltpu.VMEM((CHUNK + 8, XBC_W), f32),
                        pltpu.VMEM((B_W, B_STATE), f32), pltpu.VMEM((8, XBC_W), f32),
                        pltpu.VMEM((CHUNK, XBC_W), f32), pltpu.VMEM((CHUNK, B_W), f32), pltpu.VMEM((CHUNK, B_W), f32)],
        compiler_params=_params(("arbitrary",)),
    )(pb, pb, yp, sprev, dyb, conv_w, conv_b, dtb, alog, dskip, norm_g)


def _loss_fwd(y, target):
    t, d = y.shape
    tm = _tile(t, 512)

    def body(y_ref, t_ref, dy_ref, loss_ref):
        @pl.when(pl.program_id(0) == 0)
        def _():
            loss_ref[...] = jnp.zeros_like(loss_ref)

        e = y_ref[...] - t_ref[...]
        dy_ref[...] = e * (1.0 / d)
        per_tok = jnp.mean(e * e, axis=-1, keepdims=True)
        loss_ref[...] += 0.5 * jnp.sum(per_tok, axis=0, keepdims=True)

    row = pl.BlockSpec((tm, d), lambda i: (i, 0))
    return pl.pallas_call(
        body, name="loss_fwd", grid=(t // tm,), in_specs=[row, row],
        out_specs=[row, pl.BlockSpec((1, 128), lambda i: (0, 0))],
        out_shape=[jax.ShapeDtypeStruct((t, d), f32), jax.ShapeDtypeStruct((1, 128), f32)],
        compiler_params=_params(("arbitrary",)),
    )(y, target)


def _row_tile(r):
    return 512 if r % 512 == 0 else r


def _pair_add(g, r, c_dev):
    _, nl, rows, cols = g.shape
    tr = _row_tile(rows)

    def body(c_ref, g_ref, r_ref, o_ref):
        o_ref[...] = (g_ref[...].astype(f32) + r_ref[...].astype(f32)).astype(bf16)

    blk = (None, None, tr, cols)
    return pl.pallas_call(
        body, name="pair_add",
        grid_spec=pltpu.PrefetchScalarGridSpec(
            num_scalar_prefetch=1, grid=(4, nl, rows // tr),
            in_specs=[pl.BlockSpec(blk, lambda b, l, i, c: (2 * b + c[0], l, i, 0)),
                      pl.BlockSpec(blk, lambda b, l, i, c: (b, l, i, 0))],
            out_specs=pl.BlockSpec(blk, lambda b, l, i, c: (b, l, i, 0))),
        out_shape=jax.ShapeDtypeStruct(r.shape, bf16),
        compiler_params=_params(("arbitrary", "arbitrary", "arbitrary")),
    )(c_dev, g, r)


def _grad_sum(s, q, b_dev):
    _, nl, rows, cols = s.shape
    tr = _row_tile(rows)

    def body(b_ref, s_ref, q0_ref, q1_ref, q2_ref, o_ref):
        o_ref[...] = ((s_ref[...].astype(f32) + q0_ref[...].astype(f32)) + q1_ref[...].astype(f32)) + q2_ref[...].astype(f32)

    blk = (None, None, tr, cols)

    def qspec(k):
        return pl.BlockSpec(blk, lambda l, i, b: (k, l, i, 0))

    return pl.pallas_call(
        body, name="grad_sum",
        grid_spec=pltpu.PrefetchScalarGridSpec(
            num_scalar_prefetch=1, grid=(nl, rows // tr),
            in_specs=[pl.BlockSpec(blk, lambda l, i, b: (b[0], l, i, 0)), qspec(0), qspec(1), qspec(2)],
            out_specs=pl.BlockSpec((None, tr, cols), lambda l, i, b: (l, i, 0))),
        out_shape=jax.ShapeDtypeStruct(s.shape[1:], f32),
        compiler_params=_params(("arbitrary", "arbitrary")),
    )(b_dev, s, q, q, q)


def _sum_devices(parts):
    n, rows, cols = parts.shape
    tr = _row_tile(rows)

    def body(p_ref, o_ref):
        acc = p_ref[0]
        for k in range(1, n):
            acc = acc + p_ref[k]
        o_ref[...] = acc

    return pl.pallas_call(
        body, name="sum_devices", grid=(rows // tr,),
        in_specs=[pl.BlockSpec((n, tr, cols), lambda i: (0, i, 0))],
        out_specs=pl.BlockSpec((tr, cols), lambda i: (i, 0)),
        out_shape=jax.ShapeDtypeStruct((rows, cols), f32),
        compiler_params=_params(("arbitrary",)),
    )(parts)


def _adamw(w, m, v, g):
    nl, rows, cols = w.shape
    tr = _row_tile(rows)

    def body(w_ref, m_ref, v_ref, g_ref, d_ref, nm_ref, nv_ref):
        d_ref[...], nm_ref[...], nv_ref[...] = _adamw_math(w_ref[...], m_ref[...], v_ref[...], g_ref[...])

    blk = pl.BlockSpec((None, tr, cols), lambda l, i: (l, i, 0))
    shape = jax.ShapeDtypeStruct(w.shape, f32)
    return pl.pallas_call(
        body, name="adamw", grid=(nl, rows // tr), in_specs=[blk] * 4, out_specs=[blk] * 3,
        out_shape=[shape] * 3, compiler_params=_params(("arbitrary", "arbitrary")),
    )(w, m, v, g)


def _adamw_math(w, m, v, g):
    nm = ADAM_B1 * m + (1.0 - ADAM_B1) * g
    nv = ADAM_B2 * v + (1.0 - ADAM_B2) * (g * g)
    m_hat = nm / (1.0 - ADAM_B1 ** ADAM_STEP)
    v_hat = nv / (1.0 - ADAM_B2 ** ADAM_STEP)
    return -ADAM_LR * (m_hat / (jnp.sqrt(v_hat) + ADAM_EPS) + ADAM_WD * w), nm, nv


def _adamw_layer(w, m, v, s, q, b_dev, outs, l, deps=()):
    _, rows, cols = w.shape
    tr = _row_tile(rows)

    def body(b_ref, w_ref, m_ref, v_ref, s_ref, q0_ref, q1_ref, q2_ref, o0, o1, o2, o3, g_ref, d_ref, nm_ref, nv_ref):
        g = ((s_ref[...].astype(f32) + q0_ref[...].astype(f32)) + q1_ref[...].astype(f32)) + q2_ref[...].astype(f32)
        g_ref[...] = g
        d_ref[...], nm_ref[...], nv_ref[...] = _adamw_math(w_ref[...], m_ref[...], v_ref[...], g)

    wspec = pl.BlockSpec((None, tr, cols), lambda i, b: (l, i, 0))
    blk = (None, None, tr, cols)

    def qspec(k):
        return pl.BlockSpec(blk, lambda i, b: (k, 0, i, 0))

    shape = jax.ShapeDtypeStruct(w.shape, f32)
    return pl.pallas_call(
        _after(body, 12, deps), name="adamw_layer",
        grid_spec=pltpu.PrefetchScalarGridSpec(
            num_scalar_prefetch=1, grid=(rows // tr,),
            in_specs=[wspec] * 3 + [pl.BlockSpec(blk, lambda i, b: (b[0], 0, i, 0)), qspec(0), qspec(1), qspec(2)]
            + [ANY] * (4 + len(deps)),
            out_specs=[wspec] * 4),
        out_shape=[shape] * 4, input_output_aliases={8 + k: k for k in range(4)},
        compiler_params=_params(("arbitrary",)),
    )(b_dev, w, m, v, s, q, q, q, *outs, *deps)


def _place():
    return lax.axis_index("x"), lax.axis_index("y"), lax.axis_index("c")


def _all_gather(shards, deps=()):
    n = len(shards)
    nd = len(deps)

    def body(*refs):
        src, dst = refs[:n], refs[n:2 * n]
        send_sems, recv_sems, local_sems = refs[2 * n:]
        x, y, c = _place()
        me, sibling = (x, y, c), (x, y, 1 - c)
        chips = [(1 - x, y), (x, 1 - y), (1 - x, 1 - y)]

        def copy(a, k, block, to, from_shard=False):
            px, py, pc = block
            rows = dst[a].at[4 * px + 2 * py + pc]
            return pltpu.make_async_remote_copy(
                src_ref=src[a] if from_shard else rows, dst_ref=rows,
                send_sem=send_sems.at[a, k], recv_sem=recv_sems.at[a, k], device_id=to, device_id_type=MESH)

        mine = [pltpu.make_async_copy(src[a], dst[a].at[4 * x + 2 * y + c], local_sems.at[a]) for a in range(n)]
        for cp in mine:
            cp.start()
        first = []
        for a in range(n):
            first.append(copy(a, 0, me, sibling, True))
            first += [copy(a, 1 + j, me, (*chip, c), True) for j, chip in enumerate(chips)]
        for cp in first:
            cp.start()
        passed = []
        for j, chip in enumerate(chips):
            for a in range(n):
                copy(a, 1 + j, (*chip, c), me).wait_recv()
                fwd = copy(a, 4 + j, (*chip, c), sibling)
                fwd.start()
                passed.append(fwd)
        for a in range(n):
            copy(a, 0, sibling, me).wait_recv()
            for j, chip in enumerate(chips):
                copy(a, 4 + j, (*chip, 1 - c), me).wait_recv()
        for cp in first + passed:
            cp.wait_send()
        for cp in mine:
            cp.wait()

    return pl.pallas_call(
        _after(body, n, deps), name="all_gather", in_specs=[ANY] * (n + nd), out_specs=[ANY] * n,
        out_shape=[jax.ShapeDtypeStruct((N_DEV,) + s.shape, s.dtype) for s in shards],
        scratch_shapes=[pltpu.SemaphoreType.DMA((n, 7)), pltpu.SemaphoreType.DMA((n, 7)), pltpu.SemaphoreType.DMA((n,))],
    )(*shards, *deps)


HBM = pl.BlockSpec(memory_space=pltpu.HBM)
SEM = pl.BlockSpec(memory_space=pltpu.SEMAPHORE)
_EFFECT = pltpu.SideEffectType.DATAFLOW_SIDE_EFFECTING


def _split_start(name, srcs, dsts, sem_shape, plan):
    ns, nb = len(srcs), len(srcs) + len(dsts)

    def body(*refs):
        send_sems, recv_sems = refs[nb], refs[nb + 1]
        for cp in plan(refs[:ns], refs[ns:nb], send_sems, recv_sems):
            cp.start()
        refs[-1][...] = jnp.zeros_like(refs[-1])

    bufs = list(srcs) + list(dsts)
    return pl.pallas_call(
        body, name=name,
        out_shape=(pltpu.SemaphoreType.DMA(sem_shape), pltpu.SemaphoreType.DMA(sem_shape),
                   *[pltpu.HBM(a.shape, a.dtype) for a in bufs], jax.ShapeDtypeStruct((8, 128), f32)),
        in_specs=[HBM] * nb, out_specs=(SEM, SEM, *[HBM] * nb, pl.BlockSpec(memory_space=pltpu.VMEM)),
        input_output_aliases={i: 2 + i for i in range(nb)},
        compiler_params=pltpu.CompilerParams(has_side_effects=_EFFECT),
    )(*[pltpu.with_memory_space_constraint(a, pltpu.HBM) for a in bufs])


def _split_wait(name, started, ns, plan, after):
    send_sems, recv_sems = started[0], started[1]
    bufs = list(started[2:-1])
    nb = len(bufs)
    after = list(after) if isinstance(after, (list, tuple)) else [after]

    def body(*refs):
        for cp in plan(refs[:ns], refs[ns:nb], refs[nb], refs[nb + 1]):
            cp.wait_send()
            cp.wait_recv()

    return pl.pallas_call(
        body, name=name, out_shape=tuple(pltpu.HBM(a.shape, a.dtype) for a in bufs),
        in_specs=[HBM] * nb + [SEM, SEM] + [ANY] * len(after), out_specs=tuple([HBM] * nb),
        input_output_aliases={i: i for i in range(nb)},
        compiler_params=pltpu.CompilerParams(has_side_effects=_EFFECT),
    )(*bufs, send_sems, recv_sems, *after)


def _remote(src, dst, send_sem, recv_sem, to):
    return pltpu.make_async_remote_copy(src_ref=src, dst_ref=dst, send_sem=send_sem, recv_sem=recv_sem,
                                        device_id=to, device_id_type=MESH)


def _gather_plan(src, dst, send_sems, recv_sems):
    x, y, c = _place()
    peers = [(x, y, 1 - c), (1 - x, y, c), (x, 1 - y, c), (1 - x, 1 - y, c)]
    copies = []
    for a in range(len(dst)):
        rows = dst[a].at[4 * x + 2 * y + c]
        copies += [_remote(rows, rows, send_sems.at[4 * a + k], recv_sems.at[4 * a + k], peer) for k, peer in enumerate(peers)]
    return copies


def _pair_plan(src, dst, send_sems, recv_sems):
    x, y, c = _place()
    return [_remote(src[a].at[2 * b + (1 - c)], dst[a].at[b], send_sems.at[4 * a + b], recv_sems.at[4 * a + b], (x, y, 1 - c))
            for a in range(len(src)) for b in range(4)]


def _chips_plan(src, dst, send_sems, recv_sems):
    x, y, c = _place()
    chips = [(1 - x, y), (x, 1 - y), (1 - x, 1 - y)]
    return [_remote(src[a].at[2 * px + py], dst[a].at[j], send_sems.at[3 * a + j], recv_sems.at[3 * a + j], (px, py, c))
            for a in range(len(src)) for j, (px, py) in enumerate(chips)]


def _gather_finish(bufs):
    n = len(bufs)

    def body(*refs):
        dst = refs[n:2 * n]
        send_sems, recv_sems = refs[2 * n:]
        x, y, c = _place()
        chips = [(1 - x, y), (x, 1 - y), (1 - x, 1 - y)]
        passed = []
        for a in range(n):
            for j, (px, py) in enumerate(chips):
                rows = dst[a].at[4 * px + 2 * py + c]
                passed.append(_remote(rows, rows, send_sems.at[a, j], recv_sems.at[a, j], (x, y, 1 - c)))
        for cp in passed:
            cp.start()
        for cp in passed:
            cp.wait_send()
        for a in range(n):
            for j, (px, py) in enumerate(chips):
                rows = dst[a].at[4 * px + 2 * py + (1 - c)]
                _remote(rows, rows, send_sems.at[a, j], recv_sems.at[a, j], (x, y, 1 - c)).wait_recv()

    return pl.pallas_call(
        body, name="gather_finish", in_specs=[ANY] * n, out_specs=[ANY] * n,
        out_shape=[jax.ShapeDtypeStruct(b.shape, b.dtype) for b in bufs],
        input_output_aliases={a: a for a in range(n)},
        scratch_shapes=[pltpu.SemaphoreType.DMA((n, 3)), pltpu.SemaphoreType.DMA((n, 3))],
    )(*bufs)


def _place_shards(mats, l, dev):
    n = len(mats)

    def body(dev_ref, *refs):
        for a in range(n):
            refs[n + a][...] = refs[a][...].astype(bf16)

    return pl.pallas_call(
        body, name="place_shards",
        grid_spec=pltpu.PrefetchScalarGridSpec(
            num_scalar_prefetch=1, grid=(1,),
            in_specs=[pl.BlockSpec((None,) + m.shape[1:], lambda i, dv: (l, 0, 0)) for m in mats],
            out_specs=[pl.BlockSpec((None, None) + m.shape[1:], lambda i, dv: (dv[0], 0, 0, 0)) for m in mats]),
        out_shape=[jax.ShapeDtypeStruct((N_DEV, 1) + m.shape[1:], bf16) for m in mats],
        compiler_params=_params(("arbitrary",)),
    )(dev, *mats)


BIG = ("ffn1_w_gu", "ffn1_w_down", "mix_w_in", "mix_w_out", "ffn2_w_gu", "ffn2_w_down")
SHARDED_CONV = ("lru_conv_w", "ssd_conv_w")
REPLICATED = ("ffn1_pre_g", "ffn1_post_g", "mix_pre_g", "mix_post_g", "lru_conv_b", "lru_w_r", "lru_b_r", "lru_w_i",
              "lru_b_i", "lru_lambda", "ssd_conv_b", "ssd_dt_bias", "ssd_a_log", "ssd_d", "ssd_norm_g", "sgu_ln_g",
              "sgu_ln_b", "sgu_w_s", "sgu_b_s", "ffn2_pre_g", "ffn2_post_g")
WEIGHTS = ("ffn1_pre_g", "ffn1_post_g", "ffn1_w_gu", "ffn1_w_down", "mix_pre_g", "mix_post_g", "mix_w_in", "mix_w_out",
           "lru_conv_w", "lru_conv_b", "lru_w_r", "lru_b_r", "lru_w_i", "lru_b_i", "lru_lambda", "ssd_conv_w",
           "ssd_conv_b", "ssd_dt_bias", "ssd_a_log", "ssd_d", "ssd_norm_g", "sgu_ln_g", "sgu_ln_b", "sgu_w_s", "sgu_b_s",
           "ffn2_pre_g", "ffn2_post_g", "ffn2_w_gu", "ffn2_w_down")
DT_LO = PA_W + B_W + XBC_W
N_HEADS = B_W // HEAD
PACK_COLS = 1024


def _pack(arrays):
    flat = jnp.concatenate([a.reshape(-1) for a in arrays])
    rows = -(-flat.shape[0] // (8 * PACK_COLS)) * 8
    return jnp.pad(flat, (0, rows * PACK_COLS - flat.shape[0])).reshape(rows, PACK_COLS)


def _unpack(packed, shapes):
    flat = packed.reshape(-1)
    out, off = [], 0
    for s in shapes:
        size = 1
        for dim in s:
            size *= dim
        out.append(flat[off:off + size].reshape(s))
        off += size
    return out


def _widen_w_in(w):
    return jnp.concatenate([w[..., :DT_LO], jnp.repeat(w[..., DT_LO:DT_LO + N_HEADS], HEAD, axis=-1),
                            w[..., DT_LO + N_HEADS:]], axis=-1)


def _narrow_w_in_grad(g):
    dt = g[..., DT_LO:DT_LO + B_W]
    dt = dt.reshape(dt.shape[:-1] + (N_HEADS, HEAD)).sum(-1)
    return jnp.concatenate([g[..., :DT_LO], dt, g[..., DT_LO + B_W:]], axis=-1)


def _per_head(a):
    return a.reshape(a.shape[:-1] + (N_HEADS, HEAD)).sum(-1)


def kernel(x, ffn1_pre_g, ffn1_post_g, ffn1_w_gu, ffn1_w_down, mix_pre_g, mix_post_g, mix_w_in, mix_w_out, lru_conv_w, lru_conv_b, lru_w_r, lru_b_r, lru_w_i, lru_b_i, lru_lambda, ssd_conv_w, ssd_conv_b, ssd_dt_bias, ssd_a_log, ssd_d, ssd_norm_g, sgu_ln_g, sgu_ln_b, sgu_w_s, sgu_b_s, ffn2_pre_g, ffn2_post_g, ffn2_w_gu, ffn2_w_down, loss_target, m_ffn1_pre_g, m_ffn1_post_g, m_ffn1_w_gu, m_ffn1_w_down, m_mix_pre_g, m_mix_post_g, m_mix_w_in, m_mix_w_out, m_lru_conv_w, m_lru_conv_b, m_lru_w_r, m_lru_b_r, m_lru_w_i, m_lru_b_i, m_lru_lambda, m_ssd_conv_w, m_ssd_conv_b, m_ssd_dt_bias, m_ssd_a_log, m_ssd_d, m_ssd_norm_g, m_sgu_ln_g, m_sgu_ln_b, m_sgu_w_s, m_sgu_b_s, m_ffn2_pre_g, m_ffn2_post_g, m_ffn2_w_gu, m_ffn2_w_down, v_ffn1_pre_g, v_ffn1_post_g, v_ffn1_w_gu, v_ffn1_w_down, v_mix_pre_g, v_mix_post_g, v_mix_w_in, v_mix_w_out, v_lru_conv_w, v_lru_conv_b, v_lru_w_r, v_lru_b_r, v_lru_w_i, v_lru_b_i, v_lru_lambda, v_ssd_conv_w, v_ssd_conv_b, v_ssd_dt_bias, v_ssd_a_log, v_ssd_d, v_ssd_norm_g, v_sgu_ln_g, v_sgu_ln_b, v_sgu_w_s, v_sgu_b_s, v_ffn2_pre_g, v_ffn2_post_g, v_ffn2_w_gu, v_ffn2_w_down):
    given = dict(locals())
    w = {n: given[n] for n in WEIGHTS}
    mom = {n: given["m_" + n] for n in WEIGHTS}
    var = {n: given["v_" + n] for n in WEIGHTS}
    nl = ffn1_pre_g.shape[0]
    _, t, d = x.shape
    xi, yi, ci = _place()
    dev = 4 * xi + 2 * yi + ci
    c_dev = jnp.reshape(ci, (1,)).astype(jnp.int32)
    b_dev = jnp.reshape(2 * xi + yi, (1,)).astype(jnp.int32)

    conv_shapes = [lru_conv_w.shape, ssd_conv_w.shape]
    shards = [ffn1_w_gu, ffn1_w_down, _widen_w_in(mix_w_in), mix_w_out, ffn2_w_gu, ffn2_w_down]
    nbig = len(shards)
    dev_arr = jnp.reshape(dev, (1,)).astype(jnp.int32)
    conv_pack = _pack([lru_conv_w, ssd_conv_w])
    conv_buf = lax.dynamic_update_slice_in_dim(jnp.zeros((N_DEV,) + conv_pack.shape, f32), conv_pack[None], dev, axis=0)
    def gather_groups(l):
        return [(0, 1), (2, 3), (4, 5)] if l == 0 else [tuple(range(nbig))]

    gather_started = {}
    for l in range(nl):
        for gi, idx in enumerate(gather_groups(l)):
            bufs = list(_place_shards([shards[i] for i in idx], l, dev_arr)) + ([conv_buf] if (l, gi) == (0, 1) else [])
            gather_started[l, gi] = _split_start(f"gather_start_{l}_{gi}", [], bufs, (4 * len(bufs),), _gather_plan)

    def finish_gather(l, gi, after):
        waited = _split_wait(f"gather_wait_{l}_{gi}", gather_started[l, gi], 0, _gather_plan, after)
        return _gather_finish(list(waited))

    def conv_taps(conv_all):
        full = []
        for k, shape in enumerate(conv_shapes):
            per_dev = jnp.stack([_unpack(conv_all[s], conv_shapes)[k] for s in range(N_DEV)], axis=2)
            full.append(per_dev.reshape(shape[0], shape[1], N_DEV * shape[2]))
        return full

    def vec(a):
        return a.reshape(nl, 1, -1)

    def per_channel(a):
        return jnp.repeat(a, HEAD, axis=-1).reshape(nl, 1, B_W)

    eye = jnp.eye(A_W // HEAD, dtype=f32)

    def block_diag(a):
        return jnp.einsum("lhij,hg->lhigj", a, eye).reshape(nl, A_W, A_W).astype(bf16)

    causal = jnp.tril(jnp.ones((CHUNK, CHUNK), dtype=bool))
    p = dict(
        ffn1_pre=vec(ffn1_pre_g), ffn1_post=vec(ffn1_post_g), mix_pre=vec(mix_pre_g), mix_post=vec(mix_post_g),
        ffn2_pre=vec(ffn2_pre_g), ffn2_post=vec(ffn2_post_g),
        lru=(vec(lru_conv_b), block_diag(lru_w_r), block_diag(lru_w_i), vec(lru_b_r), vec(lru_b_i), vec(lru_lambda)),
        ssd=(vec(ssd_conv_b), per_channel(ssd_dt_bias), per_channel(ssd_a_log), per_channel(ssd_d), vec(ssd_norm_g)),
    )
    wm = jnp.where(causal, sgu_w_s, 0.0).astype(bf16)
    sgu_bias = jnp.repeat(jnp.swapaxes(sgu_b_s, 1, 2), HEAD, axis=2)
    sgu_f = (vec(sgu_ln_g), vec(sgu_ln_b), wm, sgu_bias)
    sgu_b = (vec(sgu_ln_g), vec(sgu_ln_b), wm, jnp.swapaxes(wm, 2, 3), sgu_bias)

    xs = x.reshape(t, d)
    saved, gathered = [], []
    for l in range(nl):
        x0 = xs
        if l == 0:
            wgu1, wd1 = finish_gather(0, 0, x0)
            deps = tuple(started[-1] for key, started in gather_started.items() if key != (0, 0))
        else:
            wgu1, wd1, win, wout, wgu2, wd2 = finish_gather(l, 0, x0)
            deps = ()
        x1, hb1, g1, u1, f1 = _ffn_fwd(x0, p["ffn1_pre"], p["ffn1_post"], wgu1, wd1, l, deps)
        if l == 0:
            win, wout, conv_all = finish_gather(0, 1, x1)
            lru_cw, ssd_cw = conv_taps(conv_all)
            p["lru"], p["ssd"] = (lru_cw,) + p["lru"], (ssd_cw,) + p["ssd"]
        hbm, pa, pb, pc = _mix_in_fwd(x1, p["mix_pre"], win, l)
        ya, h = _lru_fwd(pa, *p["lru"], l)
        yb, yp, sp = _ssd_fwd(pb, *p["ssd"], l)
        yc = _sgu_fwd(pc, *sgu_f, l)
        x2, cat, m = _mix_out_fwd(x1, ya, yb, yc, p["mix_post"], wout, l)
        if l == 0:
            wgu2, wd2 = finish_gather(0, 2, x2)
        xs, hb2, g2, u2, f2 = _ffn_fwd(x2, p["ffn2_pre"], p["ffn2_post"], wgu2, wd2, l)
        gathered.append((wgu1, wd1, win, wout, wgu2, wd2))
        saved.append((x0, hb1, g1, u1, f1, x1, hbm, pa, pb, pc, h, yp, sp, cat, m, x2, hb2, g2, u2, f2))
    dy, loss_part = _loss_fwd(xs, loss_target.reshape(t, d))
    loss = lax.psum(loss_part[0, 0], ("x", "y", "c"))

    small = {n: [None] * nl for n in REPLICATED + SHARDED_CONV}
    grads, delta, new_m, new_v = {}, {}, {}, {}
    fused = [n for n in BIG if n != "mix_w_in"]

    def oriented(a, n):
        return jnp.swapaxes(a, 1, 2) if n.endswith("w_gu") else a

    opt_in = {n: tuple(oriented(src[n], n) for src in (w, mom, var)) for n in fused}
    opt_out = {n: tuple(lax.empty(opt_in[n][0].shape, f32) for _ in range(4)) for n in fused}
    w_in_grads = [None] * nl
    grad_shapes = {n: (s.shape[2], s.shape[1]) if n.endswith("w_gu") else s.shape[1:] for n, s in zip(BIG, shards)}

    def start_pair(tag, lp, names, gbuf):
        landing = [lax.empty((4, 1) + grad_shapes[n], bf16) for n in names]
        started = _split_start(f"pair_start_{tag}", [gbuf[n] for n in names], landing, (4 * len(names),), _pair_plan)
        return tag, lp, names, started

    def finish_pair(pending, after):
        tag, lp, names, started = pending
        k = len(names)
        done = _split_wait(f"pair_wait_{tag}", started, k, _pair_plan, after)
        sums = [_pair_add(g, r, c_dev) for g, r in zip(done[:k], done[k:])]
        landing = [lax.empty((3,) + s.shape[1:], bf16) for s in sums]
        return tag, lp, names, _split_start(f"chips_start_{tag}", sums, landing, (3 * k,), _chips_plan)

    def finish_chips(pending, after, deps=()):
        tag, lp, names, started = pending
        k = len(names)
        done = _split_wait(f"chips_wait_{tag}", started, k, _chips_plan, after)
        last = None
        for n, s, q in zip(names, done[:k], done[k:]):
            if n == "mix_w_in":
                w_in_grads[lp] = last = _grad_sum(s, q, b_dev)
            else:
                opt_out[n] = tuple(_adamw_layer(*opt_in[n], s, q, b_dev, opt_out[n], lp, deps))
                last = opt_out[n][0]
        return last

    early = ("ffn2_w_gu", "ffn2_w_down", "mix_w_out")
    late = ("mix_w_in", "ffn1_w_gu", "ffn1_w_down")
    pending_pair = pending_chips = early_pair = early_chips = upper_started = None
    deferred = []
    names = REPLICATED + SHARDED_CONV
    assert nl > 1
    for l in reversed(range(nl)):
        x0, hb1, g1, u1, f1, x1, hbm, pa, pb, pc, h, yp, sp, cat, m, x2, hb2, g2, u2, f2 = saved[l]
        wgu1, wd1, win, wout, wgu2, wd2 = gathered[l][:nbig]
        gbuf ={n: lax.empty((N_DEV, 1) + grad_shapes[n], bf16) for n in BIG}
        deps = () if pending_pair is None else (pending_pair[3][-1],)
        if l == 0:
            deps += (upper_started[-1],)
        dx2, dfb, act, dg, du, dpre, dpost = _ffn_bwd(x2, dy, f2, p["ffn2_pre"], p["ffn2_post"], g2, u2, wgu2, wd2, l, deps)
        small["ffn2_pre_g"][l], small["ffn2_post_g"][l] = dpre[0], dpost[0]
        gbuf["ffn2_w_gu"] = _wgrad_cols(hb2, dg, gbuf["ffn2_w_gu"], 0, 0)
        gbuf["ffn2_w_gu"] = _wgrad_cols(hb2, du, gbuf["ffn2_w_gu"], 0, dg.shape[0])
        gbuf["ffn2_w_down"] = _wgrad_rows(act, dfb, gbuf["ffn2_w_down"], 0)
        deps = ()
        if pending_pair is not None:
            pending_chips = finish_pair(pending_pair, dx2)
            deps = (pending_chips[3][-1],)

        dm, dya, dyb, dyc, dpost = _mix_out_bwd(dx2, m, p["mix_post"], wout, l, deps)
        small["mix_post_g"][l] = dpost[0]
        gbuf["mix_w_out"] = _wgrad_kblocks(cat, [dm], gbuf["mix_w_out"], 0)
        deps = ()
        if l == 0:
            early_pair = start_pair("0a", 0, early, gbuf)
            deps = (early_pair[3][-1],)
        dpc, dws, dbias, dlg, dlb = _sgu_bwd(pc, dyc, *sgu_b, l, deps)
        small["sgu_w_s"][l] = jnp.where(causal, dws, 0.0)
        small["sgu_b_s"][l] = dbias.reshape(CHUNK, C_W // HEAD, HEAD).sum(-1).T
        small["sgu_ln_g"][l], small["sgu_ln_b"][l] = dlg[0], dlb[0]
        dpb, dcw, dcb, ddtb, dalog, ddsk, dng = _ssd_bwd(pb, yp, sp, dyb, *p["ssd"], l)
        small["ssd_conv_w"][l], small["ssd_conv_b"][l], small["ssd_norm_g"][l] = dcw, dcb[0], dng[0]
        small["ssd_dt_bias"][l], small["ssd_a_log"][l], small["ssd_d"][l] = _per_head(ddtb[0]), _per_head(dalog[0]), _per_head(ddsk[0])
        deps = ()
        if l == 0:
            early_chips = finish_pair(early_pair, dpb)
            deps = (early_chips[3][-1],)
        dpa, dcw, dcb, dwr, dwi, dbr, dbi, dlam = _lru_bwd(pa, h, dya, *p["lru"], l, deps)
        small["lru_conv_w"][l], small["lru_conv_b"][l], small["lru_lambda"][l] = dcw, dcb[0], dlam[0]
        small["lru_b_r"][l], small["lru_b_i"][l] = dbr[0], dbi[0]
        heads = range(A_W // HEAD)
        small["lru_w_r"][l] = jnp.stack([dwr[HEAD * i:HEAD * (i + 1), HEAD * i:HEAD * (i + 1)] for i in heads])
        small["lru_w_i"][l] = jnp.stack([dwi[HEAD * i:HEAD * (i + 1), HEAD * i:HEAD * (i + 1)] for i in heads])
        dx1, dpre = _mix_in_bwd(x1, dx2, p["mix_pre"], dpa, dpb, dpc, win, l)
        small["mix_pre_g"][l] = dpre[0]
        gbuf["mix_w_in"] = _wgrad_kblocks(hbm, [dpa, dpb, dpc], gbuf["mix_w_in"], 0)

        dy, dfb, act, dg, du, dpre, dpost = _ffn_bwd(x0, dx1, f1, p["ffn1_pre"], p["ffn1_post"], g1, u1, wgu1, wd1, l)
        small["ffn1_pre_g"][l], small["ffn1_post_g"][l] = dpre[0], dpost[0]
        gbuf["ffn1_w_gu"] = _wgrad_cols(hb1, dg, gbuf["ffn1_w_gu"], 0, 0)
        gbuf["ffn1_w_gu"] = _wgrad_cols(hb1, du, gbuf["ffn1_w_gu"], 0, dg.shape[0])
        gbuf["ffn1_w_down"] = _wgrad_rows(act, dfb, gbuf["ffn1_w_down"], 0)
        if pending_chips is not None:
            deferred.append(pending_chips)
            pending_chips = None
        pending_pair = start_pair(f"{l}", l, late if l == 0 else BIG, gbuf)
        if l == 1:
            upper = [jnp.stack(small[n][1:]) for n in names]
            upper_pack = _pack(upper)
            upper_buf = lax.dynamic_update_slice_in_dim(
                jnp.zeros((N_DEV,) + upper_pack.shape, f32), upper_pack[None], dev, axis=0)
            upper_started = _split_start("small_start", [], [upper_buf], (4,), _gather_plan)
    grad_x = dy.reshape(x.shape)

    lower = [jnp.stack(small[n][:1]) for n in names]
    lower_total = _sum_devices(_all_gather([_pack(lower)], (pending_pair[3][-1],))[0])
    late_chips = finish_pair(pending_pair, lower_total)
    order = lower_total
    for pending in deferred + [early_chips]:
        order = finish_chips(pending, order, (late_chips[3][-1],))
    upper_all = _gather_finish(list(_split_wait("small_wait", upper_started, 0, _gather_plan, order)))[0]
    upper_total = _sum_devices(upper_all)
    finish_chips(late_chips, upper_total)
    full = {n: jnp.concatenate([lo, up], axis=0) for n, lo, up in zip(
        names, _unpack(lower_total, [a.shape for a in lower]), _unpack(upper_total, [a.shape for a in upper]))}

    for n in fused:
        grads[n], delta[n], new_m[n], new_v[n] = (oriented(a, n) for a in opt_out[n])
    grads["mix_w_in"] = _narrow_w_in_grad(jnp.concatenate(w_in_grads, axis=0))
    delta["mix_w_in"], new_m["mix_w_in"], new_v["mix_w_in"] = _adamw(
        w["mix_w_in"], mom["mix_w_in"], var["mix_w_in"], grads["mix_w_in"])
    for n in REPLICATED:
        grads[n] = full[n]
    for n in SHARDED_CONV:
        cols = w[n].shape[2]
        grads[n] = lax.dynamic_slice_in_dim(full[n], dev * cols, cols, axis=2)
    shapes = [w[n].shape for n in names]
    packs = [_pack([src[n] for n in names])[None] for src in (w, mom, var, grads)]
    for dst, packed in zip((delta, new_m, new_v), _adamw(*packs)):
        dst.update(zip(names, _unpack(packed[0], shapes)))

    return (loss, grad_x, *[grads[n] for n in WEIGHTS], *[delta[n] for n in WEIGHTS],
            *[new_m[n] for n in WEIGHTS], *[new_v[n] for n in WEIGHTS])
```

```python
import functools

import jax
import jax.numpy as jnp
from jax import lax
from jax.experimental import pallas as pl
from jax.experimental.pallas import tpu as pltpu

f32, bf16 = jnp.float32, jnp.bfloat16
MESH = pl.DeviceIdType.MESH
ANY = pl.BlockSpec(memory_space=pl.ANY)

N_DEV = 8
NORM_EPS = 1e-6
LRU_C = 8.0
CHUNK = 128
HEAD = 64
A_W, B_W, C_W = 384, 384, 256
B_STATE = 128
XBC_W = B_W + 4 * B_STATE
PA_W, PB_W, PC_W = 2 * A_W, B_W + XBC_W + B_W, 2 * C_W
IN_PAD = PA_W + PB_W + PC_W
ADAM_LR, ADAM_B1, ADAM_B2, ADAM_EPS, ADAM_WD, ADAM_STEP = 0.001, 0.9, 0.999, 1e-08, 0.01, 10
VMEM_LIMIT_BYTES = 56 * 1024 * 1024
FFN_BWD_SPLIT = 2
NEG_BIG = -1e30


def _params(sem=None):
    return pltpu.CompilerParams(dimension_semantics=sem, vmem_limit_bytes=VMEM_LIMIT_BYTES)


def _nn(a, b):
    return jnp.dot(a, b, preferred_element_type=f32)


def _nt(a, b):
    return lax.dot_general(a, b, (((1,), (1,)), ((), ())), preferred_element_type=f32)


def _tn(a, b):
    return lax.dot_general(a, b, (((0,), (0,)), ((), ())), preferred_element_type=f32)


def _sigmoid(x):
    return 0.5 * jnp.tanh(0.5 * x) + 0.5


def _softplus(x):
    return jnp.maximum(x, 0.0) + jnp.log(1.0 + jnp.exp(-jnp.abs(x)))


_GELU_C0, _GELU_C1 = 0.7978845608028654, 0.044715


def _gelu(x):
    t = jnp.tanh(_GELU_C0 * (x + _GELU_C1 * x * x * x))
    return 0.5 * x * (1.0 + t)


def _gelu_grad(x):
    t = jnp.tanh(_GELU_C0 * (x + _GELU_C1 * x * x * x))
    return 0.5 * (1.0 + t) + 0.5 * x * (1.0 - t * t) * _GELU_C0 * (1.0 + 3.0 * _GELU_C1 * x * x)


def _silu_grad(x, s):
    return s * (1.0 + x * (1.0 - s))


def _rms_fwd(x, g):
    r = lax.rsqrt(jnp.mean(x * x, axis=-1, keepdims=True) + NORM_EPS)
    return x * r * g


def _rms_bwd(x, g, dy):
    r = lax.rsqrt(jnp.mean(x * x, axis=-1, keepdims=True) + NORM_EPS)
    xh = x * r
    dxh = dy * g
    dx = r * (dxh - xh * jnp.mean(dxh * xh, axis=-1, keepdims=True))
    return dx, jnp.sum(dy * xh, axis=0, keepdims=True)


def _one_minus_exp(x):
    series = -x * (1.0 + x * (0.5 + x * (1.0 / 6.0 + x * (1.0 / 24.0))))
    return jnp.where(x > -0.01, series, 1.0 - jnp.exp(x))


def _cumsum_rows(x):
    row = lax.broadcasted_iota(jnp.int32, x.shape, 0)
    d = 1
    while d < x.shape[0]:
        x = x + jnp.where(row >= d, pltpu.roll(x, d, 0), 0.0)
        d *= 2
    return x


def _tile(t, cap):
    tm = min(cap, t)
    assert t % tm == 0
    return tm


def _after(body, n_in, deps):
    def wrapped(*refs):
        return body(*refs[:n_in], *refs[n_in + len(deps):])
    return wrapped


def _lspec(a, l):
    return pl.BlockSpec((None,) + a.shape[1:], lambda *_: (l,) + (0,) * (a.ndim - 1))


def _wd_rows(wd_ref):
    return wd_ref[:, 0].reshape(2 * wd_ref.shape[2], wd_ref.shape[3])


def _ffn_fwd(x, pre_g, post_g, wgu, wd, l, deps=()):
    t, d = x.shape
    nb, _, _, h = wgu.shape
    nj = nb // 2
    tm = _tile(t, 512)

    def body(x_ref, pg_ref, qg_ref, wg_ref, wu_ref, wd_ref, y_ref, hb_ref, g_ref, u_ref, f_ref, acc_ref):
        j = pl.program_id(1)

        @pl.when(j == 0)
        def _():
            hb_ref[...] = _rms_fwd(x_ref[...], pg_ref[...]).astype(bf16)

        hb = hb_ref[...]
        g = _nn(hb, wg_ref[0, 0])
        u = _nn(hb, wu_ref[0, 0])
        g_ref[0] = g.astype(bf16)
        u_ref[0] = u.astype(bf16)
        a = (g * _sigmoid(g) * u).astype(bf16)
        part = _nn(a, _wd_rows(wd_ref))

        @pl.when(j == 0)
        def _():
            acc_ref[...] = part

        @pl.when(j > 0)
        def _():
            acc_ref[...] += part

        @pl.when(j == nj - 1)
        def _():
            f = acc_ref[...]
            f_ref[...] = f
            y_ref[...] = x_ref[...] + 0.5 * _rms_fwd(f, qg_ref[...])

    row = pl.BlockSpec((tm, d), lambda i, j: (i, 0))
    vec = pl.BlockSpec((1, d), lambda i, j: (0, 0))
    act = pl.BlockSpec((1, tm, h), lambda i, j: (j, i, 0))
    return pl.pallas_call(
        _after(body, 6, deps), name="ffn_fwd", grid=(t // tm, nj),
        in_specs=[row, _lspec(pre_g, l), _lspec(post_g, l),
                  pl.BlockSpec((1, 1, d, h), lambda i, j: (j, 0, 0, 0)),
                  pl.BlockSpec((1, 1, d, h), lambda i, j: (j + nj, 0, 0, 0)),
                  pl.BlockSpec((2, 1, h // 2, d), lambda i, j: (j, 0, 0, 0))] + [ANY] * len(deps),
        out_specs=[row, row, act, act, row],
        out_shape=[jax.ShapeDtypeStruct((t, d), f32), jax.ShapeDtypeStruct((t, d), bf16),
                   jax.ShapeDtypeStruct((nj, t, h), bf16), jax.ShapeDtypeStruct((nj, t, h), bf16),
                   jax.ShapeDtypeStruct((t, d), f32)],
        scratch_shapes=[pltpu.VMEM((tm, d), f32)],
        compiler_params=_params(("arbitrary", "arbitrary")),
    )(x, pre_g, post_g, wgu, wgu, wd, *deps)


def _ffn_bwd(x, dy, f, pre_g, post_g, g, u, wgu, wd, l, deps=()):
    t, d = x.shape
    nj, _, h = g.shape
    tm = _tile(t, 512)

    def body(x_ref, dy_ref, f_ref, pg_ref, qg_ref, g_ref, u_ref, wg_ref, wu_ref, wd_ref,
             dx_ref, dfb_ref, a_ref, dg_ref, du_ref, dpg_ref, dqg_ref, dh_ref):
        i, j = pl.program_id(0), pl.program_id(1)

        @pl.when((i == 0) & (j == 0))
        def _():
            dpg_ref[...] = jnp.zeros_like(dpg_ref)
            dqg_ref[...] = jnp.zeros_like(dqg_ref)

        @pl.when(j == 0)
        def _():
            df, dq = _rms_bwd(f_ref[...], qg_ref[...], 0.5 * dy_ref[...])
            dfb_ref[...] = df.astype(bf16)
            dqg_ref[...] += dq
            dh_ref[...] = jnp.zeros_like(dh_ref)

        wdm, wg, wu = _wd_rows(wd_ref), wg_ref[0, 0], wu_ref[0, 0]
        sub = tm // FFN_BWD_SPLIT
        das = [_nt(dfb_ref[pl.ds(half * sub, sub), :], wdm) for half in range(FFN_BWD_SPLIT)]
        for half in range(FFN_BWD_SPLIT):
            rows = pl.ds(half * sub, sub)
            da = das[half]
            gv = g_ref[0, rows, :].astype(f32)
            uv = u_ref[0, rows, :].astype(f32)
            s = _sigmoid(gv)
            sg = gv * s
            a_ref[0, rows, :] = (sg * uv).astype(bf16)
            dg = (da * uv * _silu_grad(gv, s)).astype(bf16)
            du = (da * sg).astype(bf16)
            dg_ref[0, rows, :] = dg
            du_ref[0, rows, :] = du
            dh_ref[rows, :] += _nt(dg, wg) + _nt(du, wu)

        @pl.when(j == nj - 1)
        def _():
            dxn, dp = _rms_bwd(x_ref[...], pg_ref[...], dh_ref[...])
            dx_ref[...] = dy_ref[...] + dxn
            dpg_ref[...] += dp

    row = pl.BlockSpec((tm, d), lambda i, j: (i, 0))
    vec = pl.BlockSpec((1, d), lambda i, j: (0, 0))
    act = pl.BlockSpec((1, tm, h), lambda i, j: (j, i, 0))
    act_shape = jax.ShapeDtypeStruct((nj, t, h), bf16)
    return pl.pallas_call(
        _after(body, 10, deps), name="ffn_bwd", grid=(t // tm, nj),
        in_specs=[row, row, row, _lspec(pre_g, l), _lspec(post_g, l), act, act,
                  pl.BlockSpec((1, 1, d, h), lambda i, j: (j, 0, 0, 0)),
                  pl.BlockSpec((1, 1, d, h), lambda i, j: (j + nj, 0, 0, 0)),
                  pl.BlockSpec((2, 1, h // 2, d), lambda i, j: (j, 0, 0, 0))] + [ANY] * len(deps),
        out_specs=[row, row, act, act, act, vec, vec],
        out_shape=[jax.ShapeDtypeStruct((t, d), f32), jax.ShapeDtypeStruct((t, d), bf16),
                   act_shape, act_shape, act_shape,
                   jax.ShapeDtypeStruct((1, d), f32), jax.ShapeDtypeStruct((1, d), f32)],
        scratch_shapes=[pltpu.VMEM((tm, d), f32)],
        compiler_params=_params(("arbitrary", "arbitrary")),
    )(x, dy, f, pre_g, post_g, g, u, wgu, wgu, wd, *deps)


def _wgrad_cols(x, dy, buf, l, slot0):
    (t, k), (nj, _, n) = x.shape, dy.shape

    def body(x_ref, dy_ref, buf_ref, o_ref):
        o_ref[0, 0] = _tn(dy_ref[0], x_ref[...]).astype(bf16)

    return pl.pallas_call(
        body, name="wgrad_cols", grid=(nj,),
        in_specs=[pl.BlockSpec((t, k), lambda b: (0, 0)), pl.BlockSpec((1, t, n), lambda b: (b, 0, 0)), ANY],
        out_specs=pl.BlockSpec((1, 1, n, k), lambda b: (b + slot0, l, 0, 0)),
        out_shape=jax.ShapeDtypeStruct(buf.shape, bf16), input_output_aliases={2: 0},
        compiler_params=_params(("arbitrary",)),
    )(x, dy, buf)


def _wgrad_rows(x, dy, buf, l):
    (nj, t, k), (_, n) = x.shape, dy.shape

    def body(x_ref, dy_ref, buf_ref, o_ref):
        o_ref[:, 0] = _tn(x_ref[0], dy_ref[...]).astype(bf16).reshape(2, k // 2, n)

    return pl.pallas_call(
        body, name="wgrad_rows", grid=(nj,),
        in_specs=[pl.BlockSpec((1, t, k), lambda b: (b, 0, 0)), pl.BlockSpec((t, n), lambda b: (0, 0)), ANY],
        out_specs=pl.BlockSpec((2, 1, k // 2, n), lambda b: (b, l, 0, 0)),
        out_shape=jax.ShapeDtypeStruct(buf.shape, bf16), input_output_aliases={2: 0},
        compiler_params=_params(("arbitrary",)),
    )(x, dy, buf)


def _wgrad_kblocks(x, dys, buf, l):
    t, k = x.shape
    kb = k // N_DEV
    widths = [dy.shape[1] for dy in dys]
    n = sum(widths)
    nd = len(dys)

    def body(x_ref, *refs):
        dy_hbm, o_ref, dy_vmem = refs[:nd], refs[nd + 1], refs[nd + 2:]

        @pl.when(pl.program_id(0) == 0)
        def _():
            for src, dst in zip(dy_hbm, dy_vmem):
                pltpu.sync_copy(src, dst)

        off = 0
        for dst, w in zip(dy_vmem, widths):
            o_ref[0, 0, :, off:off + w] = _tn(x_ref[...], dst[...]).astype(bf16)
            off += w

    return pl.pallas_call(
        body, name="wgrad_kblocks", grid=(N_DEV,),
        in_specs=[pl.BlockSpec((t, kb), lambda s: (0, s))] + [ANY] * (nd + 1),
        out_specs=pl.BlockSpec((1, 1, kb, n), lambda s: (s, l, 0, 0)),
        out_shape=jax.ShapeDtypeStruct(buf.shape, bf16), input_output_aliases={nd + 1: 0},
        scratch_shapes=[pltpu.VMEM((t, w), bf16) for w in widths],
        compiler_params=_params(("arbitrary",)),
    )(x, *dys, buf)


def _gathered_rows(w_ref, lo, hi):
    return w_ref[:, 0, :, lo:hi].reshape(N_DEV * w_ref.shape[2], hi - lo)


def _gathered_spec(w):
    return pl.BlockSpec((N_DEV, 1) + w.shape[2:], lambda i: (0, 0, 0, 0))


def _mix_in_fwd(x, pre_g, w_in, l):
    t, d = x.shape
    tm = _tile(t, 512)

    def body(x_ref, g_ref, w_ref, hb_ref, pa_ref, pb_ref, pc_ref):
        hb = _rms_fwd(x_ref[...], g_ref[...]).astype(bf16)
        hb_ref[...] = hb
        pa_ref[...] = _nn(hb, _gathered_rows(w_ref, 0, PA_W))
        pb_ref[...] = _nn(hb, _gathered_rows(w_ref, PA_W, PA_W + PB_W))
        pc_ref[...] = _nn(hb, _gathered_rows(w_ref, PA_W + PB_W, IN_PAD))

    def row(w):
        return pl.BlockSpec((tm, w), lambda i: (i, 0))

    return pl.pallas_call(
        body, name="mix_in_fwd", grid=(t // tm,),
        in_specs=[row(d), _lspec(pre_g, l), _gathered_spec(w_in)],
        out_specs=[row(d), row(PA_W), row(PB_W), row(PC_W)],
        out_shape=[jax.ShapeDtypeStruct((t, d), bf16), jax.ShapeDtypeStruct((t, PA_W), f32),
                   jax.ShapeDtypeStruct((t, PB_W), f32), jax.ShapeDtypeStruct((t, PC_W), f32)],
        compiler_params=_params(("arbitrary",)),
    )(x, pre_g, w_in)


def _mix_in_bwd(x, dy, pre_g, dpa, dpb, dpc, w_in, l):
    t, d = x.shape
    tm = _tile(t, 512)

    def body(x_ref, dy_ref, g_ref, dpa_ref, dpb_ref, dpc_ref, w_ref, dx_ref, dg_ref):
        @pl.when(pl.program_id(0) == 0)
        def _():
            dg_ref[...] = jnp.zeros_like(dg_ref)

        wa, wb, wc = (_gathered_rows(w_ref, 0, PA_W), _gathered_rows(w_ref, PA_W, PA_W + PB_W),
                      _gathered_rows(w_ref, PA_W + PB_W, IN_PAD))
        halves = [pl.ds(k * (tm // 2), tm // 2) for k in range(2)]
        dhs = [_nt(dpa_ref[rows, :], wa) + _nt(dpb_ref[rows, :], wb) + _nt(dpc_ref[rows, :], wc) for rows in halves]
        for rows, dh in zip(halves, dhs):
            dxn, dg = _rms_bwd(x_ref[rows, :], g_ref[...], dh)
            dx_ref[rows, :] = dy_ref[rows, :] + dxn
            dg_ref[...] += dg

    def row(w):
        return pl.BlockSpec((tm, w), lambda i: (i, 0))

    vec = pl.BlockSpec((1, d), lambda i: (0, 0))
    return pl.pallas_call(
        body, name="mix_in_bwd", grid=(t // tm,),
        in_specs=[row(d), row(d), _lspec(pre_g, l), row(PA_W), row(PB_W), row(PC_W), _gathered_spec(w_in)],
        out_specs=[row(d), vec],
        out_shape=[jax.ShapeDtypeStruct((t, d), f32), jax.ShapeDtypeStruct((1, d), f32)],
        compiler_params=_params(("arbitrary",)),
    )(x, dy, pre_g, dpa, dpb, dpc, w_in)


def _mix_out_fwd(x, ya, yb, yc, post_g, w_out, l):
    t, d = x.shape
    tm = _tile(t, 512)

    def body(x_ref, ya_ref, yb_ref, yc_ref, g_ref, w_ref, y_ref, cat_ref, m_ref):
        cat_ref[:, 0:A_W] = ya_ref[...].astype(bf16)
        cat_ref[:, A_W:A_W + B_W] = yb_ref[...].astype(bf16)
        cat_ref[:, A_W + B_W:d] = yc_ref[...].astype(bf16)
        m = _nn(cat_ref[...], _gathered_rows(w_ref, 0, d))
        m_ref[...] = m
        y_ref[...] = x_ref[...] + _rms_fwd(m, g_ref[...])

    def row(w):
        return pl.BlockSpec((tm, w), lambda i: (i, 0))

    return pl.pallas_call(
        body, name="mix_out_fwd", grid=(t // tm,),
        in_specs=[row(d), row(A_W), row(B_W), row(C_W), _lspec(post_g, l), _gathered_spec(w_out)],
        out_specs=[row(d), row(d), row(d)],
        out_shape=[jax.ShapeDtypeStruct((t, d), f32), jax.ShapeDtypeStruct((t, d), bf16), jax.ShapeDtypeStruct((t, d), f32)],
        compiler_params=_params(("arbitrary",)),
    )(x, ya, yb, yc, post_g, w_out)


def _mix_out_bwd(dy, m, post_g, w_out, l, deps=()):
    t, d = m.shape
    tm = _tile(t, 512)

    def body(dy_ref, m_ref, g_ref, w_ref, dm_ref, dya_ref, dyb_ref, dyc_ref, dg_ref):
        @pl.when(pl.program_id(0) == 0)
        def _():
            dg_ref[...] = jnp.zeros_like(dg_ref)

        dm, dg = _rms_bwd(m_ref[...], g_ref[...], dy_ref[...])
        dmb = dm.astype(bf16)
        dm_ref[...] = dmb
        dg_ref[...] += dg
        dcat = _nt(dmb, _gathered_rows(w_ref, 0, d))
        dya_ref[...] = dcat[:, 0:A_W]
        dyb_ref[...] = dcat[:, A_W:A_W + B_W]
        dyc_ref[...] = dcat[:, A_W + B_W:d]

    def row(w):
        return pl.BlockSpec((tm, w), lambda i: (i, 0))

    vec = pl.BlockSpec((1, d), lambda i: (0, 0))
    return pl.pallas_call(
        _after(body, 4, deps), name="mix_out_bwd", grid=(t // tm,),
        in_specs=[row(d), row(d), _lspec(post_g, l), _gathered_spec(w_out)] + [ANY] * len(deps),
        out_specs=[row(d), row(A_W), row(B_W), row(C_W), vec],
        out_shape=[jax.ShapeDtypeStruct((t, d), bf16), jax.ShapeDtypeStruct((t, A_W), f32),
                   jax.ShapeDtypeStruct((t, B_W), f32), jax.ShapeDtypeStruct((t, C_W), f32),
                   jax.ShapeDtypeStruct((1, d), f32)],
        compiler_params=_params(("arbitrary",)),
    )(dy, m, post_g, w_out, *deps)


def _conv_fwd(buf_ref, halo, x, w, b, n):
    buf_ref[0:8, :] = halo
    buf_ref[8:8 + n, :] = x
    out = b + w[3:4, :] * x
    for k in range(3):
        out = out + w[k:k + 1, :] * buf_ref[pl.ds(5 + k, n), :]
    return out


def _conv_bwd(buf_ref, dbuf_ref, dout, dnext, w, n):
    dbuf_ref[0:n, :] = dout
    dbuf_ref[n:n + 8, :] = dnext
    dx = w[3:4, :] * dout
    dws = []
    for k in range(3):
        dx = dx + w[k:k + 1, :] * dbuf_ref[pl.ds(3 - k, n), :]
        dws.append(jnp.sum(dout * buf_ref[pl.ds(5 + k, n), :], axis=0, keepdims=True))
    dws.append(jnp.sum(dout * buf_ref[pl.ds(8, n), :], axis=0, keepdims=True))
    return dx, jnp.concatenate(dws, axis=0), jnp.sum(dout, axis=0, keepdims=True)


def _lru_gates(rec, wr, wi, br, bi, lam):
    rb = rec.astype(bf16)
    r = _sigmoid(_nn(rb, wr) + br)
    ig = _sigmoid(_nn(rb, wi) + bi)
    sp = _softplus(-lam)
    la = -LRU_C * r * sp
    a = jnp.exp(la)
    mult = jnp.sqrt(_one_minus_exp(2.0 * la))
    return rb, r, ig, sp, a, mult


def _scan_rows(a_ref, b_ref, o_ref, carry, n, reverse):
    row = lax.broadcasted_iota(jnp.int32, (8, a_ref.shape[1]), 0)
    nb = n // 8

    def step(k, carry):
        blk = (nb - 1 - k) if reverse else k
        rows = pl.ds(pl.multiple_of(blk * 8, 8), 8)
        a, b = a_ref[rows, :], b_ref[rows, :]
        for d in (1, 2, 4):
            shift = 8 - d if reverse else d
            keep = (row < 8 - d) if reverse else (row >= d)
            b = a * jnp.where(keep, pltpu.roll(b, shift, 0), 0.0) + b
            a = a * jnp.where(keep, pltpu.roll(a, shift, 0), 1.0)
        o = a * carry + b
        o_ref[rows, :] = o
        return o[0:1, :] if reverse else o[7:8, :]

    return lax.fori_loop(0, nb, step, carry, unroll=2)


def _lru_fwd(pa, conv_w, conv_b, wr, wi, br, bi, lam, l):
    t = pa.shape[0]
    tc = _tile(t, 512)

    def body(pa_ref, halo_ref, cw_ref, cb_ref, wr_ref, wi_ref, br_ref, bi_ref, lam_ref,
             ya_ref, h_ref, buf_ref, a_ref, u_ref, carry_ref):
        i = pl.program_id(0)

        @pl.when(i == 0)
        def _():
            carry_ref[...] = jnp.zeros_like(carry_ref)

        halo = jnp.where(i > 0, halo_ref[:, A_W:PA_W], 0.0)
        rec = _conv_fwd(buf_ref, halo, pa_ref[:, A_W:PA_W], cw_ref[...], cb_ref[...], tc)
        _, _, ig, _, a, mult = _lru_gates(rec, wr_ref[...], wi_ref[...], br_ref[...], bi_ref[...], lam_ref[...])
        a_ref[...] = a
        u_ref[...] = mult * (ig * rec)

        carry_ref[...] = _scan_rows(a_ref, u_ref, h_ref, carry_ref[...], tc, reverse=False)
        ya_ref[...] = h_ref[...] * _gelu(pa_ref[:, 0:A_W])

    vec = pl.BlockSpec((1, A_W), lambda i: (0, 0))
    mat = pl.BlockSpec((A_W, A_W), lambda i: (0, 0))
    row = pl.BlockSpec((tc, A_W), lambda i: (i, 0))
    return pl.pallas_call(
        body, name="lru_fwd", grid=(t // tc,),
        in_specs=[pl.BlockSpec((tc, PA_W), lambda i: (i, 0)),
                  pl.BlockSpec((8, PA_W), lambda i: (jnp.maximum(i * (tc // 8) - 1, 0), 0)),
                  *[_lspec(a, l) for a in (conv_w, conv_b, wr, wi, br, bi, lam)]],
        out_specs=[row, row],
        out_shape=[jax.ShapeDtypeStruct((t, A_W), f32), jax.ShapeDtypeStruct((t, A_W), f32)],
        scratch_shapes=[pltpu.VMEM((8 + tc, A_W), f32), pltpu.VMEM((tc, A_W), f32), pltpu.VMEM((tc, A_W), f32),
                        pltpu.VMEM((1, A_W), f32)],
        compiler_params=_params(("arbitrary",)),
    )(pa, pa, conv_w, conv_b, wr, wi, br, bi, lam)


def _lru_bwd(pa, h, dya, conv_w, conv_b, wr, wi, br, bi, lam, l, deps=()):
    t = pa.shape[0]
    tc = _tile(t, 512)
    nc = t // tc

    def body(pa_ref, halo_ref, h_ref, hhalo_ref, dya_ref, cw_ref, cb_ref, wr_ref, wi_ref, br_ref, bi_ref, lam_ref,
             dpa_ref, dcw_ref, dcb_ref, dwr_ref, dwi_ref, dbr_ref, dbi_ref, dlam_ref,
             buf_ref, dbuf_ref, hbuf_ref, a_ref, g_ref, dh_ref, carry_ref, dnext_ref, dhbuf_ref):
        i = pl.program_id(0)
        c = nc - 1 - i

        @pl.when(i == 0)
        def _():
            carry_ref[...] = jnp.zeros_like(carry_ref)
            dnext_ref[...] = jnp.zeros_like(dnext_ref)
            for ref in (dcw_ref, dcb_ref, dwr_ref, dwi_ref, dbr_ref, dbi_ref, dlam_ref):
                ref[...] = jnp.zeros_like(ref)

        halo = jnp.where(c > 0, halo_ref[:, A_W:PA_W], 0.0)
        cw = cw_ref[...]
        rec = _conv_fwd(buf_ref, halo, pa_ref[:, A_W:PA_W], cw, cb_ref[...], tc)
        lam = lam_ref[...]
        rb, r, ig, sp, a, mult = _lru_gates(rec, wr_ref[...], wi_ref[...], br_ref[...], bi_ref[...], lam)
        hbuf_ref[0:8, :] = jnp.where(c > 0, hhalo_ref[...], 0.0)
        hbuf_ref[8:8 + tc, :] = h_ref[...]
        h_prev = hbuf_ref[pl.ds(7, tc), :]
        gate = pa_ref[:, 0:A_W]
        dya = dya_ref[...]
        dpa_ref[:, 0:A_W] = (dya * h_ref[...] * _gelu_grad(gate)).astype(bf16)
        a_ref[...] = a
        gg = dya * _gelu(gate)
        g_ref[...] = a * gg
        carry_in = carry_ref[...]
        carry_ref[...] = _scan_rows(a_ref, g_ref, dh_ref, carry_in, tc, reverse=True)
        dhbuf_ref[0:tc, :] = dh_ref[...]
        dhbuf_ref[tc:tc + 8, :] = jnp.broadcast_to(carry_in, (8, A_W))
        dh = gg + dhbuf_ref[pl.ds(1, tc), :]
        da = dh * h_prev
        dmult = dh * ig * rec
        dig = dh * mult * rec
        drec = dh * mult * ig
        dla = da * a - dmult * (a * a) / mult
        dr = dla * (-LRU_C * sp)
        dsp = jnp.sum(dla * (-LRU_C * r), axis=0, keepdims=True)
        dlam_ref[...] += dsp * (-_sigmoid(-lam))
        dpr = (dr * r * (1.0 - r))
        dpi = (dig * ig * (1.0 - ig))
        dprb, dpib = dpr.astype(bf16), dpi.astype(bf16)
        drec = drec + _nt(dprb, wr_ref[...]) + _nt(dpib, wi_ref[...])
        dwr_ref[...] += _tn(rb, dprb)
        dwi_ref[...] += _tn(rb, dpib)
        dbr_ref[...] += jnp.sum(dpr, axis=0, keepdims=True)
        dbi_ref[...] += jnp.sum(dpi, axis=0, keepdims=True)
        dx, dw, db = _conv_bwd(buf_ref, dbuf_ref, drec, dnext_ref[...], cw, tc)
        dnext_ref[...] = drec[0:8, :]
        dcw_ref[...] += dw
        dcb_ref[...] += db
        dpa_ref[:, A_W:PA_W] = dx.astype(bf16)

    vec = pl.BlockSpec((1, A_W), lambda i: (0, 0))
    mat = pl.BlockSpec((A_W, A_W), lambda i: (0, 0))
    cwspec = pl.BlockSpec((4, A_W), lambda i: (0, 0))

    def rev(w):
        return pl.BlockSpec((tc, w), lambda i: (nc - 1 - i, 0))

    def halo(w):
        return pl.BlockSpec((8, w), lambda i: (jnp.maximum((nc - 1 - i) * (tc // 8) - 1, 0), 0))

    chunk = pltpu.VMEM((tc, A_W), f32)
    return pl.pallas_call(
        _after(body, 12, deps), name="lru_bwd", grid=(nc,),
        in_specs=[rev(PA_W), halo(PA_W), rev(A_W), halo(A_W), rev(A_W),
                  *[_lspec(a, l) for a in (conv_w, conv_b, wr, wi, br, bi, lam)]] + [ANY] * len(deps),
        out_specs=[rev(PA_W), cwspec, vec, mat, mat, vec, vec, vec],
        out_shape=[jax.ShapeDtypeStruct((t, PA_W), bf16), jax.ShapeDtypeStruct((4, A_W), f32),
                   jax.ShapeDtypeStruct((1, A_W), f32), jax.ShapeDtypeStruct((A_W, A_W), f32),
                   jax.ShapeDtypeStruct((A_W, A_W), f32), jax.ShapeDtypeStruct((1, A_W), f32),
                   jax.ShapeDtypeStruct((1, A_W), f32), jax.ShapeDtypeStruct((1, A_W), f32)],
        scratch_shapes=[pltpu.VMEM((8 + tc, A_W), f32), pltpu.VMEM((tc + 8, A_W), f32), pltpu.VMEM((8 + tc, A_W), f32),
                        chunk, chunk, chunk, pltpu.VMEM((1, A_W), f32), pltpu.VMEM((8, A_W), f32),
                        pltpu.VMEM((tc + 8, A_W), f32)],
        compiler_params=_params(("arbitrary",)),
    )(pa, pa, h, h, dya, conv_w, conv_b, wr, wi, br, bi, lam, *deps)


def _sgu_norm(v, g, b):
    mu = jnp.mean(v, axis=-1, keepdims=True)
    vc = v - mu
    rstd = lax.rsqrt(jnp.mean(vc * vc, axis=-1, keepdims=True) + NORM_EPS)
    vh = vc * rstd
    return vh, rstd, vh * g + b


def _sgu_mix(w_ref, vb, bias):
    grp = lax.broadcasted_iota(jnp.int32, (CHUNK, C_W), 1) // HEAD
    out = bias
    for gi in range(C_W // HEAD):
        out = out + jnp.where(grp == gi, _nn(w_ref[gi], vb), 0.0)
    return out


def _sgu_fwd(pc, ln_g, ln_b, wm, bias, l):
    t = pc.shape[0]
    tm = _tile(t, 512)

    def body(pc_ref, g_ref, b_ref, w_ref, bias_ref, yc_ref):
        for ci in range(tm // CHUNK):
            rows = pl.ds(ci * CHUNK, CHUNK)
            ge = _gelu(pc_ref[rows, :])
            _, _, vn = _sgu_norm(ge[:, C_W:PC_W], g_ref[...], b_ref[...])
            yc_ref[rows, :] = ge[:, 0:C_W] * _sgu_mix(w_ref, vn.astype(bf16), bias_ref[...])

    vec = pl.BlockSpec((1, C_W), lambda i: (0, 0))
    return pl.pallas_call(
        body, name="sgu_fwd", grid=(t // tm,),
        in_specs=[pl.BlockSpec((tm, PC_W), lambda i: (i, 0)), *[_lspec(a, l) for a in (ln_g, ln_b, wm, bias)]],
        out_specs=pl.BlockSpec((tm, C_W), lambda i: (i, 0)),
        out_shape=jax.ShapeDtypeStruct((t, C_W), f32),
        compiler_params=_params(("arbitrary",)),
    )(pc, ln_g, ln_b, wm, bias)


def _sgu_bwd(pc, dyc, ln_g, ln_b, wm, wmt, bias, l, deps=()):
    t = pc.shape[0]
    tm = _tile(t, 512)

    def body(pc_ref, dyc_ref, g_ref, b_ref, w_ref, wt_ref, bias_ref, dpc_ref, dw_ref, dbias_ref, dg_ref, db_ref):
        @pl.when(pl.program_id(0) == 0)
        def _():
            for ref in (dw_ref, dbias_ref, dg_ref, db_ref):
                ref[...] = jnp.zeros_like(ref)

        grp = lax.broadcasted_iota(jnp.int32, (CHUNK, C_W), 1) // HEAD
        for ci in range(tm // CHUNK):
            rows = pl.ds(ci * CHUNK, CHUNK)
            x = pc_ref[rows, :]
            ge = _gelu(x)
            gv = g_ref[...]
            vh, rstd, vn = _sgu_norm(ge[:, C_W:PC_W], gv, b_ref[...])
            vb = vn.astype(bf16)
            mixed = _sgu_mix(w_ref, vb, bias_ref[...])
            dyc = dyc_ref[rows, :]
            du = dyc * mixed
            dmix = dyc * ge[:, 0:C_W]
            dmb = dmix.astype(bf16)
            dvn = jnp.zeros((CHUNK, C_W), f32)
            for gi in range(C_W // HEAD):
                dvn = dvn + jnp.where(grp == gi, _nn(wt_ref[gi], dmb), 0.0)
                dw_ref[gi] += _nt(jnp.where(grp == gi, dmix, 0.0).astype(bf16), vb)
            dbias_ref[...] += dmix
            dg_ref[...] += jnp.sum(dvn * vh, axis=0, keepdims=True)
            db_ref[...] += jnp.sum(dvn, axis=0, keepdims=True)
            dvh = dvn * gv
            dv = rstd * (dvh - jnp.mean(dvh, axis=-1, keepdims=True) - vh * jnp.mean(dvh * vh, axis=-1, keepdims=True))
            gg = _gelu_grad(x)
            dpc_ref[rows, 0:C_W] = (du * gg[:, 0:C_W]).astype(bf16)
            dpc_ref[rows, C_W:PC_W] = (dv * gg[:, C_W:PC_W]).astype(bf16)

    vec = pl.BlockSpec((1, C_W), lambda i: (0, 0))
    wspec = pl.BlockSpec((4, CHUNK, CHUNK), lambda i: (0, 0, 0))
    bspec = pl.BlockSpec((CHUNK, C_W), lambda i: (0, 0))
    return pl.pallas_call(
        _after(body, 7, deps), name="sgu_bwd", grid=(t // tm,),
        in_specs=[pl.BlockSpec((tm, PC_W), lambda i: (i, 0)), pl.BlockSpec((tm, C_W), lambda i: (i, 0)),
                  *[_lspec(a, l) for a in (ln_g, ln_b, wm, wmt, bias)]] + [ANY] * len(deps),
        out_specs=[pl.BlockSpec((tm, PC_W), lambda i: (i, 0)), wspec, bspec, vec, vec],
        out_shape=[jax.ShapeDtypeStruct((t, PC_W), bf16), jax.ShapeDtypeStruct((4, CHUNK, CHUNK), f32),
                   jax.ShapeDtypeStruct((CHUNK, C_W), f32), jax.ShapeDtypeStruct((1, C_W), f32),
                   jax.ShapeDtypeStruct((1, C_W), f32)],
        compiler_params=_params(("arbitrary",)),
    )(pc, dyc, ln_g, ln_b, wm, wmt, bias, *deps)


N_PAIR = B_W // 128
HEADS_PER_GROUP = 3


def _pair_groups(p):
    return (2 * p) // HEADS_PER_GROUP, (2 * p + 1) // HEADS_PER_GROUP


def _ssd_chunk(pb_ref, halo, buf_ref, cw, cb, dtb, alog):
    z = pb_ref[:, 0:B_W]
    pre = _conv_fwd(buf_ref, halo, pb_ref[:, B_W:B_W + XBC_W], cw, cb, CHUNK)
    sg = _sigmoid(pre)
    xbc = pre * sg
    xs = xbc[:, 0:B_W]
    bm = [xbc[:, B_W + k * B_STATE:B_W + (k + 1) * B_STATE] for k in range(2)]
    cm = [xbc[:, B_W + (2 + k) * B_STATE:B_W + (3 + k) * B_STATE] for k in range(2)]
    dtin = pb_ref[:, B_W + XBC_W:PB_W] + dtb
    dt = _softplus(dtin)
    a = -jnp.exp(alog)
    cs = _cumsum_rows(dt * a)
    return dict(z=z, pre=pre, sg=sg, xs=xs, bm=bm, cm=cm, dtin=dtin, dt=dt, a=a, cs=cs,
                ecs=jnp.exp(cs), ds=jnp.exp(cs[CHUNK - 1:CHUNK, :] - cs), xdt=xs * dt,
                bmb=[v.astype(bf16) for v in bm], cmb=[v.astype(bf16) for v in cm])


def _ssd_decay(cs_pair, half):
    cst = cs_pair.T
    lane0 = HEAD * half
    csc = jnp.broadcast_to(cs_pair[:, lane0:lane0 + 1], (CHUNK, CHUNK))
    csr = cst[lane0:lane0 + 1, :]
    tri = lax.broadcasted_iota(jnp.int32, (CHUNK, CHUNK), 0) >= lax.broadcasted_iota(jnp.int32, (CHUNK, CHUNK), 1)
    return jnp.exp(jnp.where(tri, csc - csr, NEG_BIG)), cst


def _ssd_fwd(pb, conv_w, conv_b, dtb, alog, dskip, norm_g, l):
    t = pb.shape[0]
    nc = t // CHUNK

    def body(pb_ref, halo_ref, cw_ref, cb_ref, dtb_ref, alog_ref, d_ref, ng_ref, yb_ref, yp_ref, sp_ref, buf_ref, s_ref):
        i = pl.program_id(0)

        @pl.when(i == 0)
        def _():
            s_ref[...] = jnp.zeros_like(s_ref)

        halo = jnp.where(i > 0, halo_ref[:, B_W:B_W + XBC_W], 0.0)
        q = _ssd_chunk(pb_ref, halo, buf_ref, cw_ref[...], cb_ref[...], dtb_ref[...], alog_ref[...])
        sp_ref[0] = s_ref[...]
        lane = lax.broadcasted_iota(jnp.int32, (CHUNK, 128), 1)
        rowi = lax.broadcasted_iota(jnp.int32, (128, B_STATE), 0)
        cb_mat = [_nt(q["cmb"][k], q["bmb"][k]) for k in range(2)]
        xd = q["xdt"] * q["ds"]
        for p in range(N_PAIR):
            cols = slice(128 * p, 128 * (p + 1))
            g_lo, g_hi = _pair_groups(p)
            cs_p, xdt_p = q["cs"][:, cols], q["xdt"][:, cols]
            s_p = s_ref[cols, :]
            s_pb = s_p.astype(bf16)
            y_p = jnp.zeros((CHUNK, 128), f32)
            for half, grp in ((0, g_lo), (1, g_hi)):
                lm, cst = _ssd_decay(cs_p, half)
                mb = (cb_mat[grp] * lm).astype(bf16)
                sel = (lane < HEAD) if half == 0 else (lane >= HEAD)
                y_p = y_p + _nn(mb, jnp.where(sel, xdt_p, 0.0).astype(bf16))
            off_lo = _nt(q["cmb"][g_lo], s_pb)
            off = off_lo if g_lo == g_hi else jnp.where(lane < HEAD, off_lo, _nt(q["cmb"][g_hi], s_pb))
            y_p = y_p + off * q["ecs"][:, cols] + q["xs"][:, cols] * d_ref[:, cols]
            yp_ref[:, cols] = y_p
            xd_pb = xd[:, cols].astype(bf16)
            upd_lo = _tn(xd_pb, q["bmb"][g_lo])
            upd = upd_lo if g_lo == g_hi else jnp.where(rowi < HEAD, upd_lo, _tn(xd_pb, q["bmb"][g_hi]))
            cd = jnp.exp(jnp.broadcast_to(cst[:, CHUNK - 1:CHUNK], (128, B_STATE)))
            s_ref[cols, :] = cd * s_p + upd
        z = q["z"]
        yg = yp_ref[...] * (z * _sigmoid(z))
        yb_ref[...] = _rms_fwd(yg, ng_ref[...])

    vec = pl.BlockSpec((1, B_W), lambda i: (0, 0))
    row = pl.BlockSpec((CHUNK, B_W), lambda i: (i, 0))
    return pl.pallas_call(
        body, name="ssd_fwd", grid=(nc,),
        in_specs=[pl.BlockSpec((CHUNK, PB_W), lambda i: (i, 0)),
                  pl.BlockSpec((8, PB_W), lambda i: (jnp.maximum(i * (CHUNK // 8) - 1, 0), 0)),
                  *[_lspec(a, l) for a in (conv_w, conv_b, dtb, alog, dskip, norm_g)]],
        out_specs=[row, row, pl.BlockSpec((1, B_W, B_STATE), lambda i: (i, 0, 0))],
        out_shape=[jax.ShapeDtypeStruct((t, B_W), f32), jax.ShapeDtypeStruct((t, B_W), f32),
                   jax.ShapeDtypeStruct((nc, B_W, B_STATE), f32)],
        scratch_shapes=[pltpu.VMEM((8 + CHUNK, XBC_W), f32), pltpu.VMEM((B_W, B_STATE), f32)],
        compiler_params=_params(("arbitrary",)),
    )(pb, pb, conv_w, conv_b, dtb, alog, dskip, norm_g)


def _ssd_bwd(pb, yp, sprev, dyb, conv_w, conv_b, dtb, alog, dskip, norm_g, l):
    t = pb.shape[0]
    nc = t // CHUNK

    def body(pb_ref, halo_ref, yp_ref, sp_ref, dyb_ref, cw_ref, cb_ref, dtb_ref, alog_ref, d_ref, ng_ref,
             dpb_ref, dcw_ref, dcb_ref, ddtb_ref, dalog_ref, dd_ref, dng_ref,
             buf_ref, dbuf_ref, ds_ref, dnext_ref, dxbc_ref, dcs_ref, dxdt_ref):
        i = pl.program_id(0)
        c = nc - 1 - i

        @pl.when(i == 0)
        def _():
            ds_ref[...] = jnp.zeros_like(ds_ref)
            dnext_ref[...] = jnp.zeros_like(dnext_ref)
            for ref in (dcw_ref, dcb_ref, ddtb_ref, dalog_ref, dd_ref, dng_ref):
                ref[...] = jnp.zeros_like(ref)

        halo = jnp.where(c > 0, halo_ref[:, B_W:B_W + XBC_W], 0.0)
        cw = cw_ref[...]
        q = _ssd_chunk(pb_ref, halo, buf_ref, cw, cb_ref[...], dtb_ref[...], alog_ref[...])
        z, xs, dt, a, ecs, dsd, xdt = q["z"], q["xs"], q["dt"], q["a"], q["ecs"], q["ds"], q["xdt"]
        sz = _sigmoid(z)
        siluz = z * sz
        yp = yp_ref[...]
        dyg, dng = _rms_bwd(yp * siluz, ng_ref[...], dyb_ref[...])
        dng_ref[...] += dng
        dy = dyg * siluz
        dpb_ref[:, 0:B_W] = (dyg * yp * _silu_grad(z, sz)).astype(bf16)
        dd_ref[...] += jnp.sum(dy * xs, axis=0, keepdims=True)
        g1 = dy * ecs
        lane = lax.broadcasted_iota(jnp.int32, (CHUNK, 128), 1)
        rowi = lax.broadcasted_iota(jnp.int32, (128, B_STATE), 0)
        rowc = lax.broadcasted_iota(jnp.int32, (CHUNK, 128), 0)
        cb_mat = [_nt(q["cmb"][k], q["bmb"][k]) for k in range(2)]
        d_cb = [jnp.zeros((CHUNK, CHUNK), f32) for _ in range(2)]
        d_b = [jnp.zeros((CHUNK, B_STATE), f32) for _ in range(2)]
        d_c = [jnp.zeros((CHUNK, B_STATE), f32) for _ in range(2)]
        for p in range(N_PAIR):
            cols = slice(128 * p, 128 * (p + 1))
            g_lo, g_hi = _pair_groups(p)
            lo, hi = lane < HEAD, lane >= HEAD
            cs_p, xdt_p, dy_p, ds_p, g1_p = q["cs"][:, cols], xdt[:, cols], dy[:, cols], dsd[:, cols], g1[:, cols]
            s_p = sp_ref[0, cols, :]
            s_pb = s_p.astype(bf16)
            dsn = ds_ref[cols, :]
            dsnb = dsn.astype(bf16)
            g1b = g1_p.astype(bf16)
            off_lo = _nt(q["cmb"][g_lo], s_pb)
            off = off_lo if g_lo == g_hi else jnp.where(lo, off_lo, _nt(q["cmb"][g_hi], s_pb))
            dcs_p = dy_p * off * ecs[:, cols]
            dsp_lo = _tn(g1b, q["cmb"][g_lo])
            dsp = dsp_lo if g_lo == g_hi else jnp.where(rowi < HEAD, dsp_lo, _tn(g1b, q["cmb"][g_hi]))
            dx_lo = _nt(q["bmb"][g_lo], dsnb)
            dxd = dx_lo if g_lo == g_hi else jnp.where(lo, dx_lo, _nt(q["bmb"][g_hi], dsnb))
            xd_p = xdt_p * ds_p
            if g_lo == g_hi:
                d_c[g_lo] = d_c[g_lo] + _nn(g1b, s_pb)
                d_b[g_lo] = d_b[g_lo] + _nn(xd_p.astype(bf16), dsnb)
            else:
                d_c[g_lo] = d_c[g_lo] + _nn(jnp.where(lo, g1_p, 0.0).astype(bf16), s_pb)
                d_c[g_hi] = d_c[g_hi] + _nn(jnp.where(hi, g1_p, 0.0).astype(bf16), s_pb)
                d_b[g_lo] = d_b[g_lo] + _nn(jnp.where(lo, xd_p, 0.0).astype(bf16), dsnb)
                d_b[g_hi] = d_b[g_hi] + _nn(jnp.where(hi, xd_p, 0.0).astype(bf16), dsnb)
            dxdt_p = dxd * ds_p
            t2 = dxd * xdt_p * ds_p
            dcs_p = dcs_p - t2
            dlast = jnp.sum(t2, axis=0, keepdims=True)
            cst = None
            for half, grp in ((0, g_lo), (1, g_hi)):
                sel = lo if half == 0 else hi
                lm, cst = _ssd_decay(cs_p, half)
                m = cb_mat[grp] * lm
                dyh = jnp.where(sel, dy_p, 0.0).astype(bf16)
                xdh = jnp.where(sel, xdt_p, 0.0).astype(bf16)
                dm = _nt(dyh, xdh)
                pm = dm * m
                col = jnp.sum(pm, axis=1, keepdims=True) - jnp.sum(pm.T, axis=1, keepdims=True)
                dcs_p = dcs_p + jnp.where(lane == HEAD * half, col, 0.0)
                d_cb[grp] = d_cb[grp] + dm * lm
                dxdt_p = dxdt_p + _tn(m.astype(bf16), dyh)
            cdcol = jnp.exp(jnp.broadcast_to(cst[:, CHUNK - 1:CHUNK], (128, B_STATE)))
            ds_ref[cols, :] = cdcol * dsn + dsp
            dcd_row = jnp.sum((dsn * s_p).T, axis=0, keepdims=True)
            dlast = dlast + dcd_row * ecs[CHUNK - 1:CHUNK, cols]
            dcs_ref[:, cols] = dcs_p + jnp.where(rowc == CHUNK - 1, dlast, 0.0)
            dxdt_ref[:, cols] = dxdt_p
        for k in range(2):
            dcbb = d_cb[k].astype(bf16)
            d_c[k] = d_c[k] + _nn(dcbb, q["bmb"][k])
            d_b[k] = d_b[k] + _tn(dcbb, q["cmb"][k])
            dxbc_ref[:, B_W + k * B_STATE:B_W + (k + 1) * B_STATE] = d_b[k]
            dxbc_ref[:, B_W + (2 + k) * B_STATE:B_W + (3 + k) * B_STATE] = d_c[k]
        dxdt = dxdt_ref[...]
        dxbc_ref[:, 0:B_W] = dy * d_ref[...] + dxdt * dt
        dcs = dcs_ref[...]
        dad = jnp.sum(dcs, axis=0, keepdims=True) - _cumsum_rows(dcs) + dcs
        ddt = dxdt * xs + dad * a
        dalog_ref[...] += jnp.sum(dad * dt, axis=0, keepdims=True) * a
        dtraw = ddt * _sigmoid(q["dtin"])
        ddtb_ref[...] += jnp.sum(dtraw, axis=0, keepdims=True)
        dpb_ref[:, B_W + XBC_W:PB_W] = dtraw.astype(bf16)
        dpre = dxbc_ref[...] * _silu_grad(q["pre"], q["sg"])
        dx, dw, db = _conv_bwd(buf_ref, dbuf_ref, dpre, dnext_ref[...], cw, CHUNK)
        dnext_ref[...] = dpre[0:8, :]
        dcw_ref[...] += dw
        dcb_ref[...] += db
        dpb_ref[:, B_W:B_W + XBC_W] = dx.astype(bf16)

    vec = pl.BlockSpec((1, B_W), lambda i: (0, 0))
    cwspec = pl.BlockSpec((4, XBC_W), lambda i: (0, 0))
    cbspec = pl.BlockSpec((1, XBC_W), lambda i: (0, 0))

    def rev(w):
        return pl.BlockSpec((CHUNK, w), lambda i: (nc - 1 - i, 0))

    vshape = jax.ShapeDtypeStruct((1, B_W), f32)
    return pl.pallas_call(
        body, name="ssd_bwd", grid=(nc,),
        in_specs=[rev(PB_W), pl.BlockSpec((8, PB_W), lambda i: (jnp.maximum((nc - 1 - i) * (CHUNK // 8) - 1, 0), 0)),
                  rev(B_W), pl.BlockSpec((1, B_W, B_STATE), lambda i: (nc - 1 - i, 0, 0)), rev(B_W),
                  *[_lspec(a, l) for a in (conv_w, conv_b, dtb, alog, dskip, norm_g)]],
        out_specs=[rev(PB_W), cwspec, cbspec, vec, vec, vec, vec],
        out_shape=[jax.ShapeDtypeStruct((t, PB_W), bf16), jax.ShapeDtypeStruct((4, XBC_W), f32),
                   jax.ShapeDtypeStruct((1, XBC_W), f32), vshape, vshape, vshape, vshape],
        scratch_shapes=[pltpu.VMEM((8 + CHUNK, XBC_W), f32), pltpu.VMEM((CHUNK + 8, XBC_W), f32),
                        pltpu.VMEM((B_W, B_STATE), f32), pltpu.VMEM((8, XBC_W), f32),
                        pltpu.VMEM((CHUNK, XBC_W), f32), pltpu.VMEM((CHUNK, B_W), f32), pltpu.VMEM((CHUNK, B_W), f32)],
        compiler_params=_params(("arbitrary",)),
    )(pb, pb, yp, sprev, dyb, conv_w, conv_b, dtb, alog, dskip, norm_g)


def _loss_fwd(y, target):
    t, d = y.shape
    tm = _tile(t, 512)

    def body(y_ref, t_ref, dy_ref, loss_ref):
        @pl.when(pl.program_id(0) == 0)
        def _():
            loss_ref[...] = jnp.zeros_like(loss_ref)

        e = y_ref[...] - t_ref[...]
        dy_ref[...] = e * (1.0 / d)
        per_tok = jnp.mean(e * e, axis=-1, keepdims=True)
        loss_ref[...] += 0.5 * jnp.sum(per_tok, axis=0, keepdims=True)

    row = pl.BlockSpec((tm, d), lambda i: (i, 0))
    return pl.pallas_call(
        body, name="loss_fwd", grid=(t // tm,), in_specs=[row, row],
        out_specs=[row, pl.BlockSpec((1, 128), lambda i: (0, 0))],
        out_shape=[jax.ShapeDtypeStruct((t, d), f32), jax.ShapeDtypeStruct((1, 128), f32)],
        compiler_params=_params(("arbitrary",)),
    )(y, target)


def _row_tile(r):
    return 512 if r % 512 == 0 else r


def _pair_add(g, r, c_dev):
    _, nl, rows, cols = g.shape
    tr = _row_tile(rows)

    def body(c_ref, g_ref, r_ref, o_ref):
        o_ref[...] = (g_ref[...].astype(f32) + r_ref[...].astype(f32)).astype(bf16)

    blk = (None, None, tr, cols)
    return pl.pallas_call(
        body, name="pair_add",
        grid_spec=pltpu.PrefetchScalarGridSpec(
            num_scalar_prefetch=1, grid=(4, nl, rows // tr),
            in_specs=[pl.BlockSpec(blk, lambda b, l, i, c: (2 * b + c[0], l, i, 0)),
                      pl.BlockSpec(blk, lambda b, l, i, c: (b, l, i, 0))],
            out_specs=pl.BlockSpec(blk, lambda b, l, i, c: (b, l, i, 0))),
        out_shape=jax.ShapeDtypeStruct(r.shape, bf16),
        compiler_params=_params(("arbitrary", "arbitrary", "arbitrary")),
    )(c_dev, g, r)


def _grad_sum(s, q, b_dev):
    _, nl, rows, cols = s.shape
    tr = _row_tile(rows)

    def body(b_ref, s_ref, q0_ref, q1_ref, q2_ref, o_ref):
        o_ref[...] = ((s_ref[...].astype(f32) + q0_ref[...].astype(f32)) + q1_ref[...].astype(f32)) + q2_ref[...].astype(f32)

    blk = (None, None, tr, cols)

    def qspec(k):
        return pl.BlockSpec(blk, lambda l, i, b: (k, l, i, 0))

    return pl.pallas_call(
        body, name="grad_sum",
        grid_spec=pltpu.PrefetchScalarGridSpec(
            num_scalar_prefetch=1, grid=(nl, rows // tr),
            in_specs=[pl.BlockSpec(blk, lambda l, i, b: (b[0], l, i, 0)), qspec(0), qspec(1), qspec(2)],
            out_specs=pl.BlockSpec((None, tr, cols), lambda l, i, b: (l, i, 0))),
        out_shape=jax.ShapeDtypeStruct(s.shape[1:], f32),
        compiler_params=_params(("arbitrary", "arbitrary")),
    )(b_dev, s, q, q, q)


def _sum_devices(parts):
    n, rows, cols = parts.shape
    tr = _row_tile(rows)

    def body(p_ref, o_ref):
        acc = p_ref[0]
        for k in range(1, n):
            acc = acc + p_ref[k]
        o_ref[...] = acc

    return pl.pallas_call(
        body, name="sum_devices", grid=(rows // tr,),
        in_specs=[pl.BlockSpec((n, tr, cols), lambda i: (0, i, 0))],
        out_specs=pl.BlockSpec((tr, cols), lambda i: (i, 0)),
        out_shape=jax.ShapeDtypeStruct((rows, cols), f32),
        compiler_params=_params(("arbitrary",)),
    )(parts)


def _adamw(w, m, v, g):
    nl, rows, cols = w.shape
    tr = _row_tile(rows)

    def body(w_ref, m_ref, v_ref, g_ref, d_ref, nm_ref, nv_ref):
        d_ref[...], nm_ref[...], nv_ref[...] = _adamw_math(w_ref[...], m_ref[...], v_ref[...], g_ref[...])

    blk = pl.BlockSpec((None, tr, cols), lambda l, i: (l, i, 0))
    shape = jax.ShapeDtypeStruct(w.shape, f32)
    return pl.pallas_call(
        body, name="adamw", grid=(nl, rows // tr), in_specs=[blk] * 4, out_specs=[blk] * 3,
        out_shape=[shape] * 3, compiler_params=_params(("arbitrary", "arbitrary")),
    )(w, m, v, g)


def _adamw_math(w, m, v, g):
    nm = ADAM_B1 * m + (1.0 - ADAM_B1) * g
    nv = ADAM_B2 * v + (1.0 - ADAM_B2) * (g * g)
    m_hat = nm / (1.0 - ADAM_B1 ** ADAM_STEP)
    v_hat = nv / (1.0 - ADAM_B2 ** ADAM_STEP)
    return -ADAM_LR * (m_hat / (jnp.sqrt(v_hat) + ADAM_EPS) + ADAM_WD * w), nm, nv


def _adamw_layer(w, m, v, s, q, b_dev, outs, l, deps=()):
    _, rows, cols = w.shape
    tr = _row_tile(rows)

    def body(b_ref, w_ref, m_ref, v_ref, s_ref, q0_ref, q1_ref, q2_ref, o0, o1, o2, o3, g_ref, d_ref, nm_ref, nv_ref):
        g = ((s_ref[...].astype(f32) + q0_ref[...].astype(f32)) + q1_ref[...].astype(f32)) + q2_ref[...].astype(f32)
        g_ref[...] = g
        d_ref[...], nm_ref[...], nv_ref[...] = _adamw_math(w_ref[...], m_ref[...], v_ref[...], g)

    wspec = pl.BlockSpec((None, tr, cols), lambda i, b: (l, i, 0))
    blk = (None, None, tr, cols)

    def qspec(k):
        return pl.BlockSpec(blk, lambda i, b: (k, 0, i, 0))

    shape = jax.ShapeDtypeStruct(w.shape, f32)
    return pl.pallas_call(
        _after(body, 12, deps), name="adamw_layer",
        grid_spec=pltpu.PrefetchScalarGridSpec(
            num_scalar_prefetch=1, grid=(rows // tr,),
            in_specs=[wspec] * 3 + [pl.BlockSpec(blk, lambda i, b: (b[0], 0, i, 0)), qspec(0), qspec(1), qspec(2)]
            + [ANY] * (4 + len(deps)),
            out_specs=[wspec] * 4),
        out_shape=[shape] * 4, input_output_aliases={8 + k: k for k in range(4)},
        compiler_params=_params(("arbitrary",)),
    )(b_dev, w, m, v, s, q, q, q, *outs, *deps)


def _place():
    return lax.axis_index("x"), lax.axis_index("y"), lax.axis_index("c")


def _all_gather(shards, deps=()):
    n = len(shards)
    nd = len(deps)

    def body(*refs):
        src, dst = refs[:n], refs[n:2 * n]
        send_sems, recv_sems, local_sems = refs[2 * n:]
        x, y, c = _place()
        me, sibling = (x, y, c), (x, y, 1 - c)
        chips = [(1 - x, y), (x, 1 - y), (1 - x, 1 - y)]

        def copy(a, k, block, to, from_shard=False):
            px, py, pc = block
            rows = dst[a].at[4 * px + 2 * py + pc]
            return pltpu.make_async_remote_copy(
                src_ref=src[a] if from_shard else rows, dst_ref=rows,
                send_sem=send_sems.at[a, k], recv_sem=recv_sems.at[a, k], device_id=to, device_id_type=MESH)

        mine = [pltpu.make_async_copy(src[a], dst[a].at[4 * x + 2 * y + c], local_sems.at[a]) for a in range(n)]
        for cp in mine:
            cp.start()
        first = []
        for a in range(n):
            first.append(copy(a, 0, me, sibling, True))
            first += [copy(a, 1 + j, me, (*chip, c), True) for j, chip in enumerate(chips)]
        for cp in first:
            cp.start()
        passed = []
        for j, chip in enumerate(chips):
            for a in range(n):
                copy(a, 1 + j, (*chip, c), me).wait_recv()
                fwd = copy(a, 4 + j, (*chip, c), sibling)
                fwd.start()
                passed.append(fwd)
        for a in range(n):
            copy(a, 0, sibling, me).wait_recv()
            for j, chip in enumerate(chips):
                copy(a, 4 + j, (*chip, 1 - c), me).wait_recv()
        for cp in first + passed:
            cp.wait_send()
        for cp in mine:
            cp.wait()

    return pl.pallas_call(
        _after(body, n, deps), name="all_gather", in_specs=[ANY] * (n + nd), out_specs=[ANY] * n,
        out_shape=[jax.ShapeDtypeStruct((N_DEV,) + s.shape, s.dtype) for s in shards],
        scratch_shapes=[pltpu.SemaphoreType.DMA((n, 7)), pltpu.SemaphoreType.DMA((n, 7)), pltpu.SemaphoreType.DMA((n,))],
    )(*shards, *deps)


HBM = pl.BlockSpec(memory_space=pltpu.HBM)
SEM = pl.BlockSpec(memory_space=pltpu.SEMAPHORE)
_EFFECT = pltpu.SideEffectType.DATAFLOW_SIDE_EFFECTING


def _split_start(name, srcs, dsts, sem_shape, plan):
    ns, nb = len(srcs), len(srcs) + len(dsts)

    def body(*refs):
        send_sems, recv_sems = refs[nb], refs[nb + 1]
        for cp in plan(refs[:ns], refs[ns:nb], send_sems, recv_sems):
            cp.start()
        refs[-1][...] = jnp.zeros_like(refs[-1])

    bufs = list(srcs) + list(dsts)
    return pl.pallas_call(
        body, name=name,
        out_shape=(pltpu.SemaphoreType.DMA(sem_shape), pltpu.SemaphoreType.DMA(sem_shape),
                   *[pltpu.HBM(a.shape, a.dtype) for a in bufs], jax.ShapeDtypeStruct((8, 128), f32)),
        in_specs=[HBM] * nb, out_specs=(SEM, SEM, *[HBM] * nb, pl.BlockSpec(memory_space=pltpu.VMEM)),
        input_output_aliases={i: 2 + i for i in range(nb)},
        compiler_params=pltpu.CompilerParams(has_side_effects=_EFFECT),
    )(*[pltpu.with_memory_space_constraint(a, pltpu.HBM) for a in bufs])


def _split_wait(name, started, ns, plan, after):
    send_sems, recv_sems = started[0], started[1]
    bufs = list(started[2:-1])
    nb = len(bufs)
    after = list(after) if isinstance(after, (list, tuple)) else [after]

    def body(*refs):
        for cp in plan(refs[:ns], refs[ns:nb], refs[nb], refs[nb + 1]):
            cp.wait_send()
            cp.wait_recv()

    return pl.pallas_call(
        body, name=name, out_shape=tuple(pltpu.HBM(a.shape, a.dtype) for a in bufs),
        in_specs=[HBM] * nb + [SEM, SEM] + [ANY] * len(after), out_specs=tuple([HBM] * nb),
        input_output_aliases={i: i for i in range(nb)},
        compiler_params=pltpu.CompilerParams(has_side_effects=_EFFECT),
    )(*bufs, send_sems, recv_sems, *after)


def _remote(src, dst, send_sem, recv_sem, to):
    return pltpu.make_async_remote_copy(src_ref=src, dst_ref=dst, send_sem=send_sem, recv_sem=recv_sem,
                                        device_id=to, device_id_type=MESH)


def _gather_plan(src, dst, send_sems, recv_sems):
    x, y, c = _place()
    peers = [(x, y, 1 - c), (1 - x, y, c), (x, 1 - y, c), (1 - x, 1 - y, c)]
    copies = []
    for a in range(len(dst)):
        rows = dst[a].at[4 * x + 2 * y + c]
        copies += [_remote(rows, rows, send_sems.at[4 * a + k], recv_sems.at[4 * a + k], peer) for k, peer in enumerate(peers)]
    return copies


def _pair_plan(src, dst, send_sems, recv_sems):
    x, y, c = _place()
    return [_remote(src[a].at[2 * b + (1 - c)], dst[a].at[b], send_sems.at[4 * a + b], recv_sems.at[4 * a + b], (x, y, 1 - c))
            for a in range(len(src)) for b in range(4)]


def _chips_plan(src, dst, send_sems, recv_sems):
    x, y, c = _place()
    chips = [(1 - x, y), (x, 1 - y), (1 - x, 1 - y)]
    return [_remote(src[a].at[2 * px + py], dst[a].at[j], send_sems.at[3 * a + j], recv_sems.at[3 * a + j], (px, py, c))
            for a in range(len(src)) for j, (px, py) in enumerate(chips)]


def _gather_finish(bufs):
    n = len(bufs)

    def body(*refs):
        dst = refs[n:2 * n]
        send_sems, recv_sems = refs[2 * n:]
        x, y, c = _place()
        chips = [(1 - x, y), (x, 1 - y), (1 - x, 1 - y)]
        passed = []
        for a in range(n):
            for j, (px, py) in enumerate(chips):
                rows = dst[a].at[4 * px + 2 * py + c]
                passed.append(_remote(rows, rows, send_sems.at[a, j], recv_sems.at[a, j], (x, y, 1 - c)))
        for cp in passed:
            cp.start()
        for cp in passed:
            cp.wait_send()
        for a in range(n):
            for j, (px, py) in enumerate(chips):
                rows = dst[a].at[4 * px + 2 * py + (1 - c)]
                _remote(rows, rows, send_sems.at[a, j], recv_sems.at[a, j], (x, y, 1 - c)).wait_recv()

    return pl.pallas_call(
        body, name="gather_finish", in_specs=[ANY] * n, out_specs=[ANY] * n,
        out_shape=[jax.ShapeDtypeStruct(b.shape, b.dtype) for b in bufs],
        input_output_aliases={a: a for a in range(n)},
        scratch_shapes=[pltpu.SemaphoreType.DMA((n, 3)), pltpu.SemaphoreType.DMA((n, 3))],
    )(*bufs)


def _place_shards(mats, l, dev):
    n = len(mats)

    def body(dev_ref, *refs):
        for a in range(n):
            refs[n + a][...] = refs[a][...].astype(bf16)

    return pl.pallas_call(
        body, name="place_shards",
        grid_spec=pltpu.PrefetchScalarGridSpec(
            num_scalar_prefetch=1, grid=(1,),
            in_specs=[pl.BlockSpec((None,) + m.shape[1:], lambda i, dv: (l, 0, 0)) for m in mats],
            out_specs=[pl.BlockSpec((None, None) + m.shape[1:], lambda i, dv: (dv[0], 0, 0, 0)) for m in mats]),
        out_shape=[jax.ShapeDtypeStruct((N_DEV, 1) + m.shape[1:], bf16) for m in mats],
        compiler_params=_params(("arbitrary",)),
    )(dev, *mats)


BIG = ("ffn1_w_gu", "ffn1_w_down", "mix_w_in", "mix_w_out", "ffn2_w_gu", "ffn2_w_down")
SHARDED_CONV = ("lru_conv_w", "ssd_conv_w")
REPLICATED = ("ffn1_pre_g", "ffn1_post_g", "mix_pre_g", "mix_post_g", "lru_conv_b", "lru_w_r", "lru_b_r", "lru_w_i",
              "lru_b_i", "lru_lambda", "ssd_conv_b", "ssd_dt_bias", "ssd_a_log", "ssd_d", "ssd_norm_g", "sgu_ln_g",
              "sgu_ln_b", "sgu_w_s", "sgu_b_s", "ffn2_pre_g", "ffn2_post_g")
WEIGHTS = ("ffn1_pre_g", "ffn1_post_g", "ffn1_w_gu", "ffn1_w_down", "mix_pre_g", "mix_post_g", "mix_w_in", "mix_w_out",
           "lru_conv_w", "lru_conv_b", "lru_w_r", "lru_b_r", "lru_w_i", "lru_b_i", "lru_lambda", "ssd_conv_w",
           "ssd_conv_b", "ssd_dt_bias", "ssd_a_log", "ssd_d", "ssd_norm_g", "sgu_ln_g", "sgu_ln_b", "sgu_w_s", "sgu_b_s",
           "ffn2_pre_g", "ffn2_post_g", "ffn2_w_gu", "ffn2_w_down")
DT_LO = PA_W + B_W + XBC_W
N_HEADS = B_W // HEAD
PACK_COLS = 1024


def _pack(arrays):
    flat = jnp.concatenate([a.reshape(-1) for a in arrays])
    rows = -(-flat.shape[0] // (8 * PACK_COLS)) * 8
    return jnp.pad(flat, (0, rows * PACK_COLS - flat.shape[0])).reshape(rows, PACK_COLS)


def _unpack(packed, shapes):
    flat = packed.reshape(-1)
    out, off = [], 0
    for s in shapes:
        size = 1
        for dim in s:
            size *= dim
        out.append(flat[off:off + size].reshape(s))
        off += size
    return out


def _widen_w_in(w):
    return jnp.concatenate([w[..., :DT_LO], jnp.repeat(w[..., DT_LO:DT_LO + N_HEADS], HEAD, axis=-1),
                            w[..., DT_LO + N_HEADS:]], axis=-1)


def _narrow_w_in_grad(g):
    dt = g[..., DT_LO:DT_LO + B_W]
    dt = dt.reshape(dt.shape[:-1] + (N_HEADS, HEAD)).sum(-1)
    return jnp.concatenate([g[..., :DT_LO], dt, g[..., DT_LO + B_W:]], axis=-1)


def _per_head(a):
    return a.reshape(a.shape[:-1] + (N_HEADS, HEAD)).sum(-1)


def kernel(x, ffn1_pre_g, ffn1_post_g, ffn1_w_gu, ffn1_w_down, mix_pre_g, mix_post_g, mix_w_in, mix_w_out, lru_conv_w, lru_conv_b, lru_w_r, lru_b_r, lru_w_i, lru_b_i, lru_lambda, ssd_conv_w, ssd_conv_b, ssd_dt_bias, ssd_a_log, ssd_d, ssd_norm_g, sgu_ln_g, sgu_ln_b, sgu_w_s, sgu_b_s, ffn2_pre_g, ffn2_post_g, ffn2_w_gu, ffn2_w_down, loss_target, m_ffn1_pre_g, m_ffn1_post_g, m_ffn1_w_gu, m_ffn1_w_down, m_mix_pre_g, m_mix_post_g, m_mix_w_in, m_mix_w_out, m_lru_conv_w, m_lru_conv_b, m_lru_w_r, m_lru_b_r, m_lru_w_i, m_lru_b_i, m_lru_lambda, m_ssd_conv_w, m_ssd_conv_b, m_ssd_dt_bias, m_ssd_a_log, m_ssd_d, m_ssd_norm_g, m_sgu_ln_g, m_sgu_ln_b, m_sgu_w_s, m_sgu_b_s, m_ffn2_pre_g, m_ffn2_post_g, m_ffn2_w_gu, m_ffn2_w_down, v_ffn1_pre_g, v_ffn1_post_g, v_ffn1_w_gu, v_ffn1_w_down, v_mix_pre_g, v_mix_post_g, v_mix_w_in, v_mix_w_out, v_lru_conv_w, v_lru_conv_b, v_lru_w_r, v_lru_b_r, v_lru_w_i, v_lru_b_i, v_lru_lambda, v_ssd_conv_w, v_ssd_conv_b, v_ssd_dt_bias, v_ssd_a_log, v_ssd_d, v_ssd_norm_g, v_sgu_ln_g, v_sgu_ln_b, v_sgu_w_s, v_sgu_b_s, v_ffn2_pre_g, v_ffn2_post_g, v_ffn2_w_gu, v_ffn2_w_down):
    given = dict(locals())
    w = {n: given[n] for n in WEIGHTS}
    mom = {n: given["m_" + n] for n in WEIGHTS}
    var = {n: given["v_" + n] for n in WEIGHTS}
    nl = ffn1_pre_g.shape[0]
    _, t, d = x.shape
    xi, yi, ci = _place()
    dev = 4 * xi + 2 * yi + ci
    c_dev = jnp.reshape(ci, (1,)).astype(jnp.int32)
    b_dev = jnp.reshape(2 * xi + yi, (1,)).astype(jnp.int32)

    conv_shapes = [lru_conv_w.shape, ssd_conv_w.shape]
    shards = [ffn1_w_gu, ffn1_w_down, _widen_w_in(mix_w_in), mix_w_out, ffn2_w_gu, ffn2_w_down]
    nbig = len(shards)
    dev_arr = jnp.reshape(dev, (1,)).astype(jnp.int32)
    conv_pack = _pack([lru_conv_w, ssd_conv_w])
    conv_buf = lax.dynamic_update_slice_in_dim(jnp.zeros((N_DEV,) + conv_pack.shape, f32), conv_pack[None], dev, axis=0)
    def gather_groups(l):
        return [(0, 1), (2, 3), (4, 5)] if l == 0 else [tuple(range(nbig))]

    gather_started = {}
    for l in range(nl):
        for gi, idx in enumerate(gather_groups(l)):
            bufs = list(_place_shards([shards[i] for i in idx], l, dev_arr)) + ([conv_buf] if (l, gi) == (0, 1) else [])
            gather_started[l, gi] = _split_start(f"gather_start_{l}_{gi}", [], bufs, (4 * len(bufs),), _gather_plan)

    def finish_gather(l, gi, after):
        waited = _split_wait(f"gather_wait_{l}_{gi}", gather_started[l, gi], 0, _gather_plan, after)
        return _gather_finish(list(waited))

    def conv_taps(conv_all):
        full = []
        for k, shape in enumerate(conv_shapes):
            per_dev = jnp.stack([_unpack(conv_all[s], conv_shapes)[k] for s in range(N_DEV)], axis=2)
            full.append(per_dev.reshape(shape[0], shape[1], N_DEV * shape[2]))
        return full

    def vec(a):
        return a.reshape(nl, 1, -1)

    def per_channel(a):
        return jnp.repeat(a, HEAD, axis=-1).reshape(nl, 1, B_W)

    eye = jnp.eye(A_W // HEAD, dtype=f32)

    def block_diag(a):
        return jnp.einsum("lhij,hg->lhigj", a, eye).reshape(nl, A_W, A_W).astype(bf16)

    causal = jnp.tril(jnp.ones((CHUNK, CHUNK), dtype=bool))
    p = dict(
        ffn1_pre=vec(ffn1_pre_g), ffn1_post=vec(ffn1_post_g), mix_pre=vec(mix_pre_g), mix_post=vec(mix_post_g),
        ffn2_pre=vec(ffn2_pre_g), ffn2_post=vec(ffn2_post_g),
        lru=(vec(lru_conv_b), block_diag(lru_w_r), block_diag(lru_w_i), vec(lru_b_r), vec(lru_b_i), vec(lru_lambda)),
        ssd=(vec(ssd_conv_b), per_channel(ssd_dt_bias), per_channel(ssd_a_log), per_channel(ssd_d), vec(ssd_norm_g)),
    )
    wm = jnp.where(causal, sgu_w_s, 0.0).astype(bf16)
    sgu_bias = jnp.repeat(jnp.swapaxes(sgu_b_s, 1, 2), HEAD, axis=2)
    sgu_f = (vec(sgu_ln_g), vec(sgu_ln_b), wm, sgu_bias)
    sgu_b = (vec(sgu_ln_g), vec(sgu_ln_b), wm, jnp.swapaxes(wm, 2, 3), sgu_bias)

    xs = x.reshape(t, d)
    saved, gathered = [], []
    for l in range(nl):
        x0 = xs
        if l == 0:
            wgu1, wd1 = finish_gather(0, 0, x0)
            deps = tuple(started[-1] for key, started in gather_started.items() if key != (0, 0))
        else:
            wgu1, wd1, win, wout, wgu2, wd2 = finish_gather(l, 0, x0)
            deps = ()
        x1, hb1, g1, u1, f1 = _ffn_fwd(x0, p["ffn1_pre"], p["ffn1_post"], wgu1, wd1, l, deps)
        if l == 0:
            win, wout, conv_all = finish_gather(0, 1, x1)
            lru_cw, ssd_cw = conv_taps(conv_all)
            p["lru"], p["ssd"] = (lru_cw,) + p["lru"], (ssd_cw,) + p["ssd"]
        hbm, pa, pb, pc = _mix_in_fwd(x1, p["mix_pre"], win, l)
        ya, h = _lru_fwd(pa, *p["lru"], l)
        yb, yp, sp = _ssd_fwd(pb, *p["ssd"], l)
        yc = _sgu_fwd(pc, *sgu_f, l)
        x2, cat, m = _mix_out_fwd(x1, ya, yb, yc, p["mix_post"], wout, l)
        if l == 0:
            wgu2, wd2 = finish_gather(0, 2, x2)
        xs, hb2, g2, u2, f2 = _ffn_fwd(x2, p["ffn2_pre"], p["ffn2_post"], wgu2, wd2, l)
        gathered.append((wgu1, wd1, win, wout, wgu2, wd2))
        saved.append((x0, hb1, g1, u1, f1, x1, hbm, pa, pb, pc, h, yp, sp, cat, m, x2, hb2, g2, u2, f2))
    dy, loss_part = _loss_fwd(xs, loss_target.reshape(t, d))
    loss = lax.psum(loss_part[0, 0], ("x", "y", "c"))

    small = {n: [None] * nl for n in REPLICATED + SHARDED_CONV}
    grads, delta, new_m, new_v = {}, {}, {}, {}
    fused = [n for n in BIG if n != "mix_w_in"]

    def oriented(a, n):
        return jnp.swapaxes(a, 1, 2) if n.endswith("w_gu") else a

    opt_in = {n: tuple(oriented(src[n], n) for src in (w, mom, var)) for n in fused}
    opt_out = {n: tuple(lax.empty(opt_in[n][0].shape, f32) for _ in range(4)) for n in fused}
    w_in_grads = [None] * nl
    grad_shapes = {n: (s.shape[2], s.shape[1]) if n.endswith("w_gu") else s.shape[1:] for n, s in zip(BIG, shards)}

    def start_pair(tag, lp, names, gbuf):
        landing = [lax.empty((4, 1) + grad_shapes[n], bf16) for n in names]
        started = _split_start(f"pair_start_{tag}", [gbuf[n] for n in names], landing, (4 * len(names),), _pair_plan)
        return tag, lp, names, started

    def finish_pair(pending, after):
        tag, lp, names, started = pending
        k = len(names)
        done = _split_wait(f"pair_wait_{tag}", started, k, _pair_plan, after)
        sums = [_pair_add(g, r, c_dev) for g, r in zip(done[:k], done[k:])]
        landing = [lax.empty((3,) + s.shape[1:], bf16) for s in sums]
        return tag, lp, names, _split_start(f"chips_start_{tag}", sums, landing, (3 * k,), _chips_plan)

    def finish_chips(pending, after, deps=()):
        tag, lp, names, started = pending
        k = len(names)
        done = _split_wait(f"chips_wait_{tag}", started, k, _chips_plan, after)
        last = None
        for n, s, q in zip(names, done[:k], done[k:]):
            if n == "mix_w_in":
                w_in_grads[lp] = last = _grad_sum(s, q, b_dev)
            else:
                opt_out[n] = tuple(_adamw_layer(*opt_in[n], s, q, b_dev, opt_out[n], lp, deps))
                last = opt_out[n][0]
        return last

    early = ("ffn2_w_gu", "ffn2_w_down", "mix_w_out")
    late = ("mix_w_in", "ffn1_w_gu", "ffn1_w_down")
    pending_pair = pending_chips = early_pair = early_chips = upper_started = None
    deferred = []
    names = REPLICATED + SHARDED_CONV
    assert nl > 1
    for l in reversed(range(nl)):
        x0, hb1, g1, u1, f1, x1, hbm, pa, pb, pc, h, yp, sp, cat, m, x2, hb2, g2, u2, f2 = saved[l]
        wgu1, wd1, win, wout, wgu2, wd2 = gathered[l][:nbig]
        gbuf ={n: lax.empty((N_DEV, 1) + grad_shapes[n], bf16) for n in BIG}
        deps = () if pending_pair is None else (pending_pair[3][-1],)
        if l == 0:
            deps += (upper_started[-1],)
        dx2, dfb, act, dg, du, dpre, dpost = _ffn_bwd(x2, dy, f2, p["ffn2_pre"], p["ffn2_post"], g2, u2, wgu2, wd2, l, deps)
        small["ffn2_pre_g"][l], small["ffn2_post_g"][l] = dpre[0], dpost[0]
        gbuf["ffn2_w_gu"] = _wgrad_cols(hb2, dg, gbuf["ffn2_w_gu"], 0, 0)
        gbuf["ffn2_w_gu"] = _wgrad_cols(hb2, du, gbuf["ffn2_w_gu"], 0, dg.shape[0])
        gbuf["ffn2_w_down"] = _wgrad_rows(act, dfb, gbuf["ffn2_w_down"], 0)
        deps = ()
        if pending_pair is not None:
            pending_chips = finish_pair(pending_pair, dx2)
            deps = (pending_chips[3][-1],)

        dm, dya, dyb, dyc, dpost = _mix_out_bwd(dx2, m, p["mix_post"], wout, l, deps)
        small["mix_post_g"][l] = dpost[0]
        gbuf["mix_w_out"] = _wgrad_kblocks(cat, [dm], gbuf["mix_w_out"], 0)
        deps = ()
        if l == 0:
            early_pair = start_pair("0a", 0, early, gbuf)
            deps = (early_pair[3][-1],)
        dpc, dws, dbias, dlg, dlb = _sgu_bwd(pc, dyc, *sgu_b, l, deps)
        small["sgu_w_s"][l] = jnp.where(causal, dws, 0.0)
        small["sgu_b_s"][l] = dbias.reshape(CHUNK, C_W // HEAD, HEAD).sum(-1).T
        small["sgu_ln_g"][l], small["sgu_ln_b"][l] = dlg[0], dlb[0]
        dpb, dcw, dcb, ddtb, dalog, ddsk, dng = _ssd_bwd(pb, yp, sp, dyb, *p["ssd"], l)
        small["ssd_conv_w"][l], small["ssd_conv_b"][l], small["ssd_norm_g"][l] = dcw, dcb[0], dng[0]
        small["ssd_dt_bias"][l], small["ssd_a_log"][l], small["ssd_d"][l] = _per_head(ddtb[0]), _per_head(dalog[0]), _per_head(ddsk[0])
        deps = ()
        if l == 0:
            early_chips = finish_pair(early_pair, dpb)
            deps = (early_chips[3][-1],)
        dpa, dcw, dcb, dwr, dwi, dbr, dbi, dlam = _lru_bwd(pa, h, dya, *p["lru"], l, deps)
        small["lru_conv_w"][l], small["lru_conv_b"][l], small["lru_lambda"][l] = dcw, dcb[0], dlam[0]
        small["lru_b_r"][l], small["lru_b_i"][l] = dbr[0], dbi[0]
        heads = range(A_W // HEAD)
        small["lru_w_r"][l] = jnp.stack([dwr[HEAD * i:HEAD * (i + 1), HEAD * i:HEAD * (i + 1)] for i in heads])
        small["lru_w_i"][l] = jnp.stack([dwi[HEAD * i:HEAD * (i + 1), HEAD * i:HEAD * (i + 1)] for i in heads])
        dx1, dpre = _mix_in_bwd(x1, dx2, p["mix_pre"], dpa, dpb, dpc, win, l)
        small["mix_pre_g"][l] = dpre[0]
        gbuf["mix_w_in"] = _wgrad_kblocks(hbm, [dpa, dpb, dpc], gbuf["mix_w_in"], 0)

        dy, dfb, act, dg, du, dpre, dpost = _ffn_bwd(x0, dx1, f1, p["ffn1_pre"], p["ffn1_post"], g1, u1, wgu1, wd1, l)
        small["ffn1_pre_g"][l], small["ffn1_post_g"][l] = dpre[0], dpost[0]
        gbuf["ffn1_w_gu"] = _wgrad_cols(hb1, dg, gbuf["ffn1_w_gu"], 0, 0)
        gbuf["ffn1_w_gu"] = _wgrad_cols(hb1, du, gbuf["ffn1_w_gu"], 0, dg.shape[0])
        gbuf["ffn1_w_down"] = _wgrad_rows(act, dfb, gbuf["ffn1_w_down"], 0)
        if pending_chips is not None:
            deferred.append(pending_chips)
            pending_chips = None
        pending_pair = start_pair(f"{l}", l, late if l == 0 else BIG, gbuf)
        if l == 1:
            upper = [jnp.stack(small[n][1:]) for n in names]
            upper_pack = _pack(upper)
            upper_buf = lax.dynamic_update_slice_in_dim(
                jnp.zeros((N_DEV,) + upper_pack.shape, f32), upper_pack[None], dev, axis=0)
            upper_started = _split_start("small_start", [], [upper_buf], (4,), _gather_plan)
    grad_x = dy.reshape(x.shape)

    lower = [jnp.stack(small[n][:1]) for n in names]
    lower_total = _sum_devices(_all_gather([_pack(lower)], (pending_pair[3][-1],))[0])
    late_chips = finish_pair(pending_pair, lower_total)
    order = lower_total
    for pending in deferred + [early_chips]:
        order = finish_chips(pending, order, (late_chips[3][-1],))
    upper_all = _gather_finish(list(_split_wait("small_wait", upper_started, 0, _gather_plan, order)))[0]
    upper_total = _sum_devices(upper_all)
    finish_chips(late_chips, [upper_total] + [opt_out[n][0] for n in fused] + [g for g in w_in_grads if g is not None])
    full = {n: jnp.concatenate([lo, up], axis=0) for n, lo, up in zip(
        names, _unpack(lower_total, [a.shape for a in lower]), _unpack(upper_total, [a.shape for a in upper]))}

    for n in fused:
        grads[n], delta[n], new_m[n], new_v[n] = (oriented(a, n) for a in opt_out[n])
    grads["mix_w_in"] = _narrow_w_in_grad(jnp.concatenate(w_in_grads, axis=0))
    delta["mix_w_in"], new_m["mix_w_in"], new_v["mix_w_in"] = _adamw(
        w["mix_w_in"], mom["mix_w_in"], var["mix_w_in"], grads["mix_w_in"])
    for n in REPLICATED:
        grads[n] = full[n]
    for n in SHARDED_CONV:
        cols = w[n].shape[2]
        grads[n] = lax.dynamic_slice_in_dim(full[n], dev * cols, cols, axis=2)
    shapes = [w[n].shape for n in names]
    packs = [_pack([src[n] for n in names])[None] for src in (w, mom, var, grads)]
    for dst, packed in zip((delta, new_m, new_v), _adamw(*packs)):
        dst.update(zip(names, _unpack(packed[0], shapes)))

    return (loss, grad_x, *[grads[n] for n in WEIGHTS], *[delta[n] for n in WEIGHTS],
            *[new_m[n] for n in WEIGHTS], *[new_v[n] for n in WEIGHTS])
```

```python
import functools

import jax
import jax.numpy as jnp
from jax import lax
from jax.experimental import pallas as pl
from jax.experimental.pallas import tpu as pltpu

f32, bf16 = jnp.float32, jnp.bfloat16
MESH = pl.DeviceIdType.MESH
ANY = pl.BlockSpec(memory_space=pl.ANY)

N_DEV = 8
NORM_EPS = 1e-6
LRU_C = 8.0
CHUNK = 128
HEAD = 64
A_W, B_W, C_W = 384, 384, 256
B_STATE = 128
XBC_W = B_W + 4 * B_STATE
PA_W, PB_W, PC_W = 2 * A_W, B_W + XBC_W + B_W, 2 * C_W
IN_PAD = PA_W + PB_W + PC_W
ADAM_LR, ADAM_B1, ADAM_B2, ADAM_EPS, ADAM_WD, ADAM_STEP = 0.001, 0.9, 0.999, 1e-08, 0.01, 10
VMEM_LIMIT_BYTES = 56 * 1024 * 1024
FFN_BWD_SPLIT = 2
NEG_BIG = -1e30


def _params(sem=None):
    return pltpu.CompilerParams(dimension_semantics=sem, vmem_limit_bytes=VMEM_LIMIT_BYTES)


def _nn(a, b):
    return jnp.dot(a, b, preferred_element_type=f32)


def _nt(a, b):
    return lax.dot_general(a, b, (((1,), (1,)), ((), ())), preferred_element_type=f32)


def _tn(a, b):
    return lax.dot_general(a, b, (((0,), (0,)), ((), ())), preferred_element_type=f32)


def _sigmoid(x):
    return 0.5 * jnp.tanh(0.5 * x) + 0.5


def _softplus(x):
    return jnp.maximum(x, 0.0) + jnp.log(1.0 + jnp.exp(-jnp.abs(x)))


_GELU_C0, _GELU_C1 = 0.7978845608028654, 0.044715


def _gelu(x):
    t = jnp.tanh(_GELU_C0 * (x + _GELU_C1 * x * x * x))
    return 0.5 * x * (1.0 + t)


def _gelu_grad(x):
    t = jnp.tanh(_GELU_C0 * (x + _GELU_C1 * x * x * x))
    return 0.5 * (1.0 + t) + 0.5 * x * (1.0 - t * t) * _GELU_C0 * (1.0 + 3.0 * _GELU_C1 * x * x)


def _silu_grad(x, s):
    return s * (1.0 + x * (1.0 - s))


def _rms_fwd(x, g):
    r = lax.rsqrt(jnp.mean(x * x, axis=-1, keepdims=True) + NORM_EPS)
    return x * r * g


def _rms_bwd(x, g, dy):
    r = lax.rsqrt(jnp.mean(x * x, axis=-1, keepdims=True) + NORM_EPS)
    xh = x * r
    dxh = dy * g
    dx = r * (dxh - xh * jnp.mean(dxh * xh, axis=-1, keepdims=True))
    return dx, jnp.sum(dy * xh, axis=0, keepdims=True)


def _one_minus_exp(x):
    series = -x * (1.0 + x * (0.5 + x * (1.0 / 6.0 + x * (1.0 / 24.0))))
    return jnp.where(x > -0.01, series, 1.0 - jnp.exp(x))


def _cumsum_rows(x):
    row = lax.broadcasted_iota(jnp.int32, x.shape, 0)
    d = 1
    while d < x.shape[0]:
        x = x + jnp.where(row >= d, pltpu.roll(x, d, 0), 0.0)
        d *= 2
    return x


def _tile(t, cap):
    tm = min(cap, t)
    assert t % tm == 0
    return tm


def _after(body, n_in, deps):
    def wrapped(*refs):
        return body(*refs[:n_in], *refs[n_in + len(deps):])
    return wrapped


def _lspec(a, l):
    return pl.BlockSpec((None,) + a.shape[1:], lambda *_: (l,) + (0,) * (a.ndim - 1))


def _wd_rows(wd_ref):
    return wd_ref[:, 0].reshape(2 * wd_ref.shape[2], wd_ref.shape[3])


def _ffn_fwd(x, pre_g, post_g, wgu, wd, l, deps=()):
    t, d = x.shape
    nb, _, _, h = wgu.shape
    nj = nb // 2
    tm = _tile(t, 512)

    def body(x_ref, pg_ref, qg_ref, wg_ref, wu_ref, wd_ref, y_ref, hb_ref, g_ref, u_ref, f_ref, acc_ref):
        j = pl.program_id(1)

        @pl.when(j == 0)
        def _():
            hb_ref[...] = _rms_fwd(x_ref[...], pg_ref[...]).astype(bf16)

        hb = hb_ref[...]
        g = _nn(hb, wg_ref[0, 0])
        u = _nn(hb, wu_ref[0, 0])
        g_ref[0] = g.astype(bf16)
        u_ref[0] = u.astype(bf16)
        a = (g * _sigmoid(g) * u).astype(bf16)
        part = _nn(a, _wd_rows(wd_ref))

        @pl.when(j == 0)
        def _():
            acc_ref[...] = part

        @pl.when(j > 0)
        def _():
            acc_ref[...] += part

        @pl.when(j == nj - 1)
        def _():
            f = acc_ref[...]
            f_ref[...] = f
            y_ref[...] = x_ref[...] + 0.5 * _rms_fwd(f, qg_ref[...])

    row = pl.BlockSpec((tm, d), lambda i, j: (i, 0))
    vec = pl.BlockSpec((1, d), lambda i, j: (0, 0))
    act = pl.BlockSpec((1, tm, h), lambda i, j: (j, i, 0))
    return pl.pallas_call(
        _after(body, 6, deps), name="ffn_fwd", grid=(t // tm, nj),
        in_specs=[row, _lspec(pre_g, l), _lspec(post_g, l),
                  pl.BlockSpec((1, 1, d, h), lambda i, j: (j, 0, 0, 0)),
                  pl.BlockSpec((1, 1, d, h), lambda i, j: (j + nj, 0, 0, 0)),
                  pl.BlockSpec((2, 1, h // 2, d), lambda i, j: (j, 0, 0, 0))] + [ANY] * len(deps),
        out_specs=[row, row, act, act, row],
        out_shape=[jax.ShapeDtypeStruct((t, d), f32), jax.ShapeDtypeStruct((t, d), bf16),
                   jax.ShapeDtypeStruct((nj, t, h), bf16), jax.ShapeDtypeStruct((nj, t, h), bf16),
                   jax.ShapeDtypeStruct((t, d), f32)],
        scratch_shapes=[pltpu.VMEM((tm, d), f32)],
        compiler_params=_params(("arbitrary", "arbitrary")),
    )(x, pre_g, post_g, wgu, wgu, wd, *deps)


def _ffn_bwd(x, dy, f, pre_g, post_g, g, u, wgu, wd, l, deps=()):
    t, d = x.shape
    nj, _, h = g.shape
    tm = _tile(t, 512)

    def body(x_ref, dy_ref, f_ref, pg_ref, qg_ref, g_ref, u_ref, wg_ref, wu_ref, wd_ref,
             dx_ref, dfb_ref, a_ref, dg_ref, du_ref, dpg_ref, dqg_ref, dh_ref):
        i, j = pl.program_id(0), pl.program_id(1)

        @pl.when((i == 0) & (j == 0))
        def _():
            dpg_ref[...] = jnp.zeros_like(dpg_ref)
            dqg_ref[...] = jnp.zeros_like(dqg_ref)

        @pl.when(j == 0)
        def _():
            df, dq = _rms_bwd(f_ref[...], qg_ref[...], 0.5 * dy_ref[...])
            dfb_ref[...] = df.astype(bf16)
            dqg_ref[...] += dq
            dh_ref[...] = jnp.zeros_like(dh_ref)

        wdm, wg, wu = _wd_rows(wd_ref), wg_ref[0, 0], wu_ref[0, 0]
        sub = tm // FFN_BWD_SPLIT
        das = [_nt(dfb_ref[pl.ds(half * sub, sub), :], wdm) for half in range(FFN_BWD_SPLIT)]
        for half in range(FFN_BWD_SPLIT):
            rows = pl.ds(half * sub, sub)
            da = das[half]
            gv = g_ref[0, rows, :].astype(f32)
            uv = u_ref[0, rows, :].astype(f32)
            s = _sigmoid(gv)
            sg = gv * s
            a_ref[0, rows, :] = (sg * uv).astype(bf16)
            dg = (da * uv * _silu_grad(gv, s)).astype(bf16)
            du = (da * sg).astype(bf16)
            dg_ref[0, rows, :] = dg
            du_ref[0, rows, :] = du
            dh_ref[rows, :] += _nt(dg, wg) + _nt(du, wu)

        @pl.when(j == nj - 1)
        def _():
            dxn, dp = _rms_bwd(x_ref[...], pg_ref[...], dh_ref[...])
            dx_ref[...] = dy_ref[...] + dxn
            dpg_ref[...] += dp

    row = pl.BlockSpec((tm, d), lambda i, j: (i, 0))
    vec = pl.BlockSpec((1, d), lambda i, j: (0, 0))
    act = pl.BlockSpec((1, tm, h), lambda i, j: (j, i, 0))
    act_shape = jax.ShapeDtypeStruct((nj, t, h), bf16)
    return pl.pallas_call(
        _after(body, 10, deps), name="ffn_bwd", grid=(t // tm, nj),
        in_specs=[row, row, row, _lspec(pre_g, l), _lspec(post_g, l), act, act,
                  pl.BlockSpec((1, 1, d, h), lambda i, j: (j, 0, 0, 0)),
                  pl.BlockSpec((1, 1, d, h), lambda i, j: (j + nj, 0, 0, 0)),
                  pl.BlockSpec((2, 1, h // 2, d), lambda i, j: (j, 0, 0, 0))] + [ANY] * len(deps),
        out_specs=[row, row, act, act, act, vec, vec],
        out_shape=[jax.ShapeDtypeStruct((t, d), f32), jax.ShapeDtypeStruct((t, d), bf16),
                   act_shape, act_shape, act_shape,
                   jax.ShapeDtypeStruct((1, d), f32), jax.ShapeDtypeStruct((1, d), f32)],
        scratch_shapes=[pltpu.VMEM((tm, d), f32)],
        compiler_params=_params(("arbitrary", "arbitrary")),
    )(x, dy, f, pre_g, post_g, g, u, wgu, wgu, wd, *deps)


def _wgrad_cols(x, dy, buf, l, slot0):
    (t, k), (nj, _, n) = x.shape, dy.shape

    def body(x_ref, dy_ref, buf_ref, o_ref):
        o_ref[0, 0] = _tn(dy_ref[0], x_ref[...]).astype(bf16)

    return pl.pallas_call(
        body, name="wgrad_cols", grid=(nj,),
        in_specs=[pl.BlockSpec((t, k), lambda b: (0, 0)), pl.BlockSpec((1, t, n), lambda b: (b, 0, 0)), ANY],
        out_specs=pl.BlockSpec((1, 1, n, k), lambda b: (b + slot0, l, 0, 0)),
        out_shape=jax.ShapeDtypeStruct(buf.shape, bf16), input_output_aliases={2: 0},
        compiler_params=_params(("arbitrary",)),
    )(x, dy, buf)


def _wgrad_rows(x, dy, buf, l):
    (nj, t, k), (_, n) = x.shape, dy.shape

    def body(x_ref, dy_ref, buf_ref, o_ref):
        o_ref[:, 0] = _tn(x_ref[0], dy_ref[...]).astype(bf16).reshape(2, k // 2, n)

    return pl.pallas_call(
        body, name="wgrad_rows", grid=(nj,),
        in_specs=[pl.BlockSpec((1, t, k), lambda b: (b, 0, 0)), pl.BlockSpec((t, n), lambda b: (0, 0)), ANY],
        out_specs=pl.BlockSpec((2, 1, k // 2, n), lambda b: (b, l, 0, 0)),
        out_shape=jax.ShapeDtypeStruct(buf.shape, bf16), input_output_aliases={2: 0},
        compiler_params=_params(("arbitrary",)),
    )(x, dy, buf)


def _wgrad_kblocks(x, dys, buf, l):
    t, k = x.shape
    kb = k // N_DEV
    widths = [dy.shape[1] for dy in dys]
    n = sum(widths)
    nd = len(dys)

    def body(x_ref, *refs):
        dy_hbm, o_ref, dy_vmem = refs[:nd], refs[nd + 1], refs[nd + 2:]

        @pl.when(pl.program_id(0) == 0)
        def _():
            for src, dst in zip(dy_hbm, dy_vmem):
                pltpu.sync_copy(src, dst)

        off = 0
        for dst, w in zip(dy_vmem, widths):
            o_ref[0, 0, :, off:off + w] = _tn(x_ref[...], dst[...]).astype(bf16)
            off += w

    return pl.pallas_call(
        body, name="wgrad_kblocks", grid=(N_DEV,),
        in_specs=[pl.BlockSpec((t, kb), lambda s: (0, s))] + [ANY] * (nd + 1),
        out_specs=pl.BlockSpec((1, 1, kb, n), lambda s: (s, l, 0, 0)),
        out_shape=jax.ShapeDtypeStruct(buf.shape, bf16), input_output_aliases={nd + 1: 0},
        scratch_shapes=[pltpu.VMEM((t, w), bf16) for w in widths],
        compiler_params=_params(("arbitrary",)),
    )(x, *dys, buf)


def _gathered_rows(w_ref, lo, hi):
    return w_ref[:, 0, :, lo:hi].reshape(N_DEV * w_ref.shape[2], hi - lo)


def _gathered_spec(w):
    return pl.BlockSpec((N_DEV, 1) + w.shape[2:], lambda i: (0, 0, 0, 0))


def _mix_in_fwd(x, pre_g, w_in, l):
    t, d = x.shape
    tm = _tile(t, 512)

    def body(x_ref, g_ref, w_ref, hb_ref, pa_ref, pb_ref, pc_ref):
        hb = _rms_fwd(x_ref[...], g_ref[...]).astype(bf16)
        hb_ref[...] = hb
        pa_ref[...] = _nn(hb, _gathered_rows(w_ref, 0, PA_W))
        pb_ref[...] = _nn(hb, _gathered_rows(w_ref, PA_W, PA_W + PB_W))
        pc_ref[...] = _nn(hb, _gathered_rows(w_ref, PA_W + PB_W, IN_PAD))

    def row(w):
        return pl.BlockSpec((tm, w), lambda i: (i, 0))

    return pl.pallas_call(
        body, name="mix_in_fwd", grid=(t // tm,),
        in_specs=[row(d), _lspec(pre_g, l), _gathered_spec(w_in)],
        out_specs=[row(d), row(PA_W), row(PB_W), row(PC_W)],
        out_shape=[jax.ShapeDtypeStruct((t, d), bf16), jax.ShapeDtypeStruct((t, PA_W), f32),
                   jax.ShapeDtypeStruct((t, PB_W), f32), jax.ShapeDtypeStruct((t, PC_W), f32)],
        compiler_params=_params(("arbitrary",)),
    )(x, pre_g, w_in)


def _mix_in_bwd(x, dy, pre_g, dpa, dpb, dpc, w_in, l):
    t, d = x.shape
    tm = _tile(t, 512)

    def body(x_ref, dy_ref, g_ref, dpa_ref, dpb_ref, dpc_ref, w_ref, dx_ref, dg_ref):
        @pl.when(pl.program_id(0) == 0)
        def _():
            dg_ref[...] = jnp.zeros_like(dg_ref)

        wa, wb, wc = (_gathered_rows(w_ref, 0, PA_W), _gathered_rows(w_ref, PA_W, PA_W + PB_W),
                      _gathered_rows(w_ref, PA_W + PB_W, IN_PAD))
        halves = [pl.ds(k * (tm // 2), tm // 2) for k in range(2)]
        dhs = [_nt(dpa_ref[rows, :], wa) + _nt(dpb_ref[rows, :], wb) + _nt(dpc_ref[rows, :], wc) for rows in halves]
        for rows, dh in zip(halves, dhs):
            dxn, dg = _rms_bwd(x_ref[rows, :], g_ref[...], dh)
            dx_ref[rows, :] = dy_ref[rows, :] + dxn
            dg_ref[...] += dg

    def row(w):
        return pl.BlockSpec((tm, w), lambda i: (i, 0))

    vec = pl.BlockSpec((1, d), lambda i: (0, 0))
    return pl.pallas_call(
        body, name="mix_in_bwd", grid=(t // tm,),
        in_specs=[row(d), row(d), _lspec(pre_g, l), row(PA_W), row(PB_W), row(PC_W), _gathered_spec(w_in)],
        out_specs=[row(d), vec],
        out_shape=[jax.ShapeDtypeStruct((t, d), f32), jax.ShapeDtypeStruct((1, d), f32)],
        compiler_params=_params(("arbitrary",)),
    )(x, dy, pre_g, dpa, dpb, dpc, w_in)


def _mix_out_fwd(x, ya, yb, yc, post_g, w_out, l):
    t, d = x.shape
    tm = _tile(t, 512)

    def body(x_ref, ya_ref, yb_ref, yc_ref, g_ref, w_ref, y_ref, cat_ref, m_ref):
        cat_ref[:, 0:A_W] = ya_ref[...].astype(bf16)
        cat_ref[:, A_W:A_W + B_W] = yb_ref[...].astype(bf16)
        cat_ref[:, A_W + B_W:d] = yc_ref[...].astype(bf16)
        m = _nn(cat_ref[...], _gathered_rows(w_ref, 0, d))
        m_ref[...] = m
        y_ref[...] = x_ref[...] + _rms_fwd(m, g_ref[...])

    def row(w):
        return pl.BlockSpec((tm, w), lambda i: (i, 0))

    return pl.pallas_call(
        body, name="mix_out_fwd", grid=(t // tm,),
        in_specs=[row(d), row(A_W), row(B_W), row(C_W), _lspec(post_g, l), _gathered_spec(w_out)],
        out_specs=[row(d), row(d), row(d)],
        out_shape=[jax.ShapeDtypeStruct((t, d), f32), jax.ShapeDtypeStruct((t, d), bf16), jax.ShapeDtypeStruct((t, d), f32)],
        compiler_params=_params(("arbitrary",)),
    )(x, ya, yb, yc, post_g, w_out)


def _mix_out_bwd(dy, m, post_g, w_out, l, deps=()):
    t, d = m.shape
    tm = _tile(t, 512)

    def body(dy_ref, m_ref, g_ref, w_ref, dm_ref, dya_ref, dyb_ref, dyc_ref, dg_ref):
        @pl.when(pl.program_id(0) == 0)
        def _():
            dg_ref[...] = jnp.zeros_like(dg_ref)

        dm, dg = _rms_bwd(m_ref[...], g_ref[...], dy_ref[...])
        dmb = dm.astype(bf16)
        dm_ref[...] = dmb
        dg_ref[...] += dg
        dcat = _nt(dmb, _gathered_rows(w_ref, 0, d))
        dya_ref[...] = dcat[:, 0:A_W]
        dyb_ref[...] = dcat[:, A_W:A_W + B_W]
        dyc_ref[...] = dcat[:, A_W + B_W:d]

    def row(w):
        return pl.BlockSpec((tm, w), lambda i: (i, 0))

    vec = pl.BlockSpec((1, d), lambda i: (0, 0))
    return pl.pallas_call(
        _after(body, 4, deps), name="mix_out_bwd", grid=(t // tm,),
        in_specs=[row(d), row(d), _lspec(post_g, l), _gathered_spec(w_out)] + [ANY] * len(deps),
        out_specs=[row(d), row(A_W), row(B_W), row(C_W), vec],
        out_shape=[jax.ShapeDtypeStruct((t, d), bf16), jax.ShapeDtypeStruct((t, A_W), f32),
                   jax.ShapeDtypeStruct((t, B_W), f32), jax.ShapeDtypeStruct((t, C_W), f32),
                   jax.ShapeDtypeStruct((1, d), f32)],
        compiler_params=_params(("arbitrary",)),
    )(dy, m, post_g, w_out, *deps)


def _conv_fwd(buf_ref, halo, x, w, b, n):
    buf_ref[0:8, :] = halo
    buf_ref[8:8 + n, :] = x
    out = b + w[3:4, :] * x
    for k in range(3):
        out = out + w[k:k + 1, :] * buf_ref[pl.ds(5 + k, n), :]
    return out


def _conv_bwd(buf_ref, dbuf_ref, dout, dnext, w, n):
    dbuf_ref[0:n, :] = dout
    dbuf_ref[n:n + 8, :] = dnext
    dx = w[3:4, :] * dout
    dws = []
    for k in range(3):
        dx = dx + w[k:k + 1, :] * dbuf_ref[pl.ds(3 - k, n), :]
        dws.append(jnp.sum(dout * buf_ref[pl.ds(5 + k, n), :], axis=0, keepdims=True))
    dws.append(jnp.sum(dout * buf_ref[pl.ds(8, n), :], axis=0, keepdims=True))
    return dx, jnp.concatenate(dws, axis=0), jnp.sum(dout, axis=0, keepdims=True)


def _lru_gates(rec, wr, wi, br, bi, lam):
    rb = rec.astype(bf16)
    r = _sigmoid(_nn(rb, wr) + br)
    ig = _sigmoid(_nn(rb, wi) + bi)
    sp = _softplus(-lam)
    la = -LRU_C * r * sp
    a = jnp.exp(la)
    mult = jnp.sqrt(_one_minus_exp(2.0 * la))
    return rb, r, ig, sp, a, mult


def _scan_rows(a_ref, b_ref, o_ref, carry, n, reverse):
    row = lax.broadcasted_iota(jnp.int32, (8, a_ref.shape[1]), 0)
    nb = n // 8

    def step(k, carry):
        blk = (nb - 1 - k) if reverse else k
        rows = pl.ds(pl.multiple_of(blk * 8, 8), 8)
        a, b = a_ref[rows, :], b_ref[rows, :]
        for d in (1, 2, 4):
            shift = 8 - d if reverse else d
            keep = (row < 8 - d) if reverse else (row >= d)
            b = a * jnp.where(keep, pltpu.roll(b, shift, 0), 0.0) + b
            a = a * jnp.where(keep, pltpu.roll(a, shift, 0), 1.0)
        o = a * carry + b
        o_ref[rows, :] = o
        return o[0:1, :] if reverse else o[7:8, :]

    return lax.fori_loop(0, nb, step, carry, unroll=2)


N_GATES = 5


def _lru_fwd(pa, conv_w, conv_b, wr, wi, br, bi, lam, l):
    t = pa.shape[0]
    tc = _tile(t, 512)

    def body(pa_ref, halo_ref, cw_ref, cb_ref, wr_ref, wi_ref, br_ref, bi_ref, lam_ref,
             ya_ref, h_ref, gates_ref, buf_ref, u_ref, carry_ref):
        i = pl.program_id(0)

        @pl.when(i == 0)
        def _():
            carry_ref[...] = jnp.zeros_like(carry_ref)

        halo = jnp.where(i > 0, halo_ref[:, A_W:PA_W], 0.0)
        rec = _conv_fwd(buf_ref, halo, pa_ref[:, A_W:PA_W], cw_ref[...], cb_ref[...], tc)
        _, r, ig, _, a, mult = _lru_gates(rec, wr_ref[...], wi_ref[...], br_ref[...], bi_ref[...], lam_ref[...])
        for k, val in enumerate((rec, r, ig, a, mult)):
            gates_ref[k] = val
        u_ref[...] = mult * (ig * rec)

        carry_ref[...] = _scan_rows(gates_ref.at[3], u_ref, h_ref, carry_ref[...], tc, reverse=False)
        ya_ref[...] = h_ref[...] * _gelu(pa_ref[:, 0:A_W])

    vec = pl.BlockSpec((1, A_W), lambda i: (0, 0))
    mat = pl.BlockSpec((A_W, A_W), lambda i: (0, 0))
    row = pl.BlockSpec((tc, A_W), lambda i: (i, 0))
    return pl.pallas_call(
        body, name="lru_fwd", grid=(t // tc,),
        in_specs=[pl.BlockSpec((tc, PA_W), lambda i: (i, 0)),
                  pl.BlockSpec((8, PA_W), lambda i: (jnp.maximum(i * (tc // 8) - 1, 0), 0)),
                  *[_lspec(a, l) for a in (conv_w, conv_b, wr, wi, br, bi, lam)]],
        out_specs=[row, row, pl.BlockSpec((N_GATES, tc, A_W), lambda i: (0, i, 0))],
        out_shape=[jax.ShapeDtypeStruct((t, A_W), f32), jax.ShapeDtypeStruct((t, A_W), f32),
                   jax.ShapeDtypeStruct((N_GATES, t, A_W), f32)],
        scratch_shapes=[pltpu.VMEM((8 + tc, A_W), f32), pltpu.VMEM((tc, A_W), f32), pltpu.VMEM((1, A_W), f32)],
        compiler_params=_params(("arbitrary",)),
    )(pa, pa, conv_w, conv_b, wr, wi, br, bi, lam)


def _lru_bwd(pa, h, gates, dya, conv_w, conv_b, wr, wi, br, bi, lam, l, deps=()):
    t = pa.shape[0]
    tc = _tile(t, 512)
    nc = t // tc

    def body(pa_ref, halo_ref, h_ref, hhalo_ref, gates_ref, dya_ref, cw_ref, cb_ref, wr_ref, wi_ref, br_ref, bi_ref,
             lam_ref, dpa_ref, dcw_ref, dcb_ref, dwr_ref, dwi_ref, dbr_ref, dbi_ref, dlam_ref,
             buf_ref, dbuf_ref, hbuf_ref, g_ref, dh_ref, carry_ref, dnext_ref, dhbuf_ref):
        i = pl.program_id(0)
        c = nc - 1 - i

        @pl.when(i == 0)
        def _():
            carry_ref[...] = jnp.zeros_like(carry_ref)
            dnext_ref[...] = jnp.zeros_like(dnext_ref)
            for ref in (dcw_ref, dcb_ref, dwr_ref, dwi_ref, dbr_ref, dbi_ref, dlam_ref):
                ref[...] = jnp.zeros_like(ref)

        halo = jnp.where(c > 0, halo_ref[:, A_W:PA_W], 0.0)
        cw = cw_ref[...]
        buf_ref[0:8, :] = halo
        buf_ref[8:8 + tc, :] = pa_ref[:, A_W:PA_W]
        rec, r, ig, a, mult = (gates_ref[k] for k in range(N_GATES))
        rb = rec.astype(bf16)
        lam = lam_ref[...]
        sp = _softplus(-lam)
        hbuf_ref[0:8, :] = jnp.where(c > 0, hhalo_ref[...], 0.0)
        hbuf_ref[8:8 + tc, :] = h_ref[...]
        h_prev = hbuf_ref[pl.ds(7, tc), :]
        gate = pa_ref[:, 0:A_W]
        dya = dya_ref[...]
        dpa_ref[:, 0:A_W] = (dya * h_ref[...] * _gelu_grad(gate)).astype(bf16)
        gg = dya * _gelu(gate)
        g_ref[...] = a * gg
        carry_in = carry_ref[...]
        carry_ref[...] = _scan_rows(gates_ref.at[3], g_ref, dh_ref, carry_in, tc, reverse=True)
        dhbuf_ref[0:tc, :] = dh_ref[...]
        dhbuf_ref[tc:tc + 8, :] = jnp.broadcast_to(carry_in, (8, A_W))
        dh = gg + dhbuf_ref[pl.ds(1, tc), :]
        da = dh * h_prev
        dmult = dh * ig * rec
        dig = dh * mult * rec
        drec = dh * mult * ig
        dla = da * a - dmult * (a * a) / mult
        dr = dla * (-LRU_C * sp)
        dsp = jnp.sum(dla * (-LRU_C * r), axis=0, keepdims=True)
        dlam_ref[...] += dsp * (-_sigmoid(-lam))
        dpr = (dr * r * (1.0 - r))
        dpi = (dig * ig * (1.0 - ig))
        dprb, dpib = dpr.astype(bf16), dpi.astype(bf16)
        drec = drec + _nt(dprb, wr_ref[...]) + _nt(dpib, wi_ref[...])
        dwr_ref[...] += _tn(rb, dprb)
        dwi_ref[...] += _tn(rb, dpib)
        dbr_ref[...] += jnp.sum(dpr, axis=0, keepdims=True)
        dbi_ref[...] += jnp.sum(dpi, axis=0, keepdims=True)
        dx, dw, db = _conv_bwd(buf_ref, dbuf_ref, drec, dnext_ref[...], cw, tc)
        dnext_ref[...] = drec[0:8, :]
        dcw_ref[...] += dw
        dcb_ref[...] += db
        dpa_ref[:, A_W:PA_W] = dx.astype(bf16)

    vec = pl.BlockSpec((1, A_W), lambda i: (0, 0))
    mat = pl.BlockSpec((A_W, A_W), lambda i: (0, 0))
    cwspec = pl.BlockSpec((4, A_W), lambda i: (0, 0))

    def rev(w):
        return pl.BlockSpec((tc, w), lambda i: (nc - 1 - i, 0))

    def halo(w):
        return pl.BlockSpec((8, w), lambda i: (jnp.maximum((nc - 1 - i) * (tc // 8) - 1, 0), 0))

    chunk = pltpu.VMEM((tc, A_W), f32)
    return pl.pallas_call(
        _after(body, 13, deps), name="lru_bwd", grid=(nc,),
        in_specs=[rev(PA_W), halo(PA_W), rev(A_W), halo(A_W),
                  pl.BlockSpec((N_GATES, tc, A_W), lambda i: (0, nc - 1 - i, 0)), rev(A_W),
                  *[_lspec(a, l) for a in (conv_w, conv_b, wr, wi, br, bi, lam)]] + [ANY] * len(deps),
        out_specs=[rev(PA_W), cwspec, vec, mat, mat, vec, vec, vec],
        out_shape=[jax.ShapeDtypeStruct((t, PA_W), bf16), jax.ShapeDtypeStruct((4, A_W), f32),
                   jax.ShapeDtypeStruct((1, A_W), f32), jax.ShapeDtypeStruct((A_W, A_W), f32),
                   jax.ShapeDtypeStruct((A_W, A_W), f32), jax.ShapeDtypeStruct((1, A_W), f32),
                   jax.ShapeDtypeStruct((1, A_W), f32), jax.ShapeDtypeStruct((1, A_W), f32)],
        scratch_shapes=[pltpu.VMEM((8 + tc, A_W), f32), pltpu.VMEM((tc + 8, A_W), f32), pltpu.VMEM((8 + tc, A_W), f32),
                        chunk, chunk, pltpu.VMEM((1, A_W), f32), pltpu.VMEM((8, A_W), f32),
                        pltpu.VMEM((tc + 8, A_W), f32)],
        compiler_params=_params(("arbitrary",)),
    )(pa, pa, h, h, gates, dya, conv_w, conv_b, wr, wi, br, bi, lam, *deps)


def _sgu_norm(v, g, b):
    mu = jnp.mean(v, axis=-1, keepdims=True)
    vc = v - mu
    rstd = lax.rsqrt(jnp.mean(vc * vc, axis=-1, keepdims=True) + NORM_EPS)
    vh = vc * rstd
    return vh, rstd, vh * g + b


def _sgu_mix(w_ref, vb, bias):
    grp = lax.broadcasted_iota(jnp.int32, (CHUNK, C_W), 1) // HEAD
    out = bias
    for gi in range(C_W // HEAD):
        out = out + jnp.where(grp == gi, _nn(w_ref[gi], vb), 0.0)
    return out


def _sgu_fwd(pc, ln_g, ln_b, wm, bias, l):
    t = pc.shape[0]
    tm = _tile(t, 512)

    def body(pc_ref, g_ref, b_ref, w_ref, bias_ref, yc_ref):
        for ci in range(tm // CHUNK):
            rows = pl.ds(ci * CHUNK, CHUNK)
            ge = _gelu(pc_ref[rows, :])
            _, _, vn = _sgu_norm(ge[:, C_W:PC_W], g_ref[...], b_ref[...])
            yc_ref[rows, :] = ge[:, 0:C_W] * _sgu_mix(w_ref, vn.astype(bf16), bias_ref[...])

    vec = pl.BlockSpec((1, C_W), lambda i: (0, 0))
    return pl.pallas_call(
        body, name="sgu_fwd", grid=(t // tm,),
        in_specs=[pl.BlockSpec((tm, PC_W), lambda i: (i, 0)), *[_lspec(a, l) for a in (ln_g, ln_b, wm, bias)]],
        out_specs=pl.BlockSpec((tm, C_W), lambda i: (i, 0)),
        out_shape=jax.ShapeDtypeStruct((t, C_W), f32),
        compiler_params=_params(("arbitrary",)),
    )(pc, ln_g, ln_b, wm, bias)


def _sgu_bwd(pc, dyc, ln_g, ln_b, wm, wmt, bias, l, deps=()):
    t = pc.shape[0]
    tm = _tile(t, 512)

    def body(pc_ref, dyc_ref, g_ref, b_ref, w_ref, wt_ref, bias_ref, dpc_ref, dw_ref, dbias_ref, dg_ref, db_ref):
        @pl.when(pl.program_id(0) == 0)
        def _():
            for ref in (dw_ref, dbias_ref, dg_ref, db_ref):
                ref[...] = jnp.zeros_like(ref)

        grp = lax.broadcasted_iota(jnp.int32, (CHUNK, C_W), 1) // HEAD
        for ci in range(tm // CHUNK):
            rows = pl.ds(ci * CHUNK, CHUNK)
            x = pc_ref[rows, :]
            ge = _gelu(x)
            gv = g_ref[...]
            vh, rstd, vn = _sgu_norm(ge[:, C_W:PC_W], gv, b_ref[...])
            vb = vn.astype(bf16)
            mixed = _sgu_mix(w_ref, vb, bias_ref[...])
            dyc = dyc_ref[rows, :]
            du = dyc * mixed
            dmix = dyc * ge[:, 0:C_W]
            dmb = dmix.astype(bf16)
            dvn = jnp.zeros((CHUNK, C_W), f32)
            for gi in range(C_W // HEAD):
                dvn = dvn + jnp.where(grp == gi, _nn(wt_ref[gi], dmb), 0.0)
                dw_ref[gi] += _nt(jnp.where(grp == gi, dmix, 0.0).astype(bf16), vb)
            dbias_ref[...] += dmix
            dg_ref[...] += jnp.sum(dvn * vh, axis=0, keepdims=True)
            db_ref[...] += jnp.sum(dvn, axis=0, keepdims=True)
            dvh = dvn * gv
            dv = rstd * (dvh - jnp.mean(dvh, axis=-1, keepdims=True) - vh * jnp.mean(dvh * vh, axis=-1, keepdims=True))
            gg = _gelu_grad(x)
            dpc_ref[rows, 0:C_W] = (du * gg[:, 0:C_W]).astype(bf16)
            dpc_ref[rows, C_W:PC_W] = (dv * gg[:, C_W:PC_W]).astype(bf16)

    vec = pl.BlockSpec((1, C_W), lambda i: (0, 0))
    wspec = pl.BlockSpec((4, CHUNK, CHUNK), lambda i: (0, 0, 0))
    bspec = pl.BlockSpec((CHUNK, C_W), lambda i: (0, 0))
    return pl.pallas_call(
        _after(body, 7, deps), name="sgu_bwd", grid=(t // tm,),
        in_specs=[pl.BlockSpec((tm, PC_W), lambda i: (i, 0)), pl.BlockSpec((tm, C_W), lambda i: (i, 0)),
                  *[_lspec(a, l) for a in (ln_g, ln_b, wm, wmt, bias)]] + [ANY] * len(deps),
        out_specs=[pl.BlockSpec((tm, PC_W), lambda i: (i, 0)), wspec, bspec, vec, vec],
        out_shape=[jax.ShapeDtypeStruct((t, PC_W), bf16), jax.ShapeDtypeStruct((4, CHUNK, CHUNK), f32),
                   jax.ShapeDtypeStruct((CHUNK, C_W), f32), jax.ShapeDtypeStruct((1, C_W), f32),
                   jax.ShapeDtypeStruct((1, C_W), f32)],
        compiler_params=_params(("arbitrary",)),
    )(pc, dyc, ln_g, ln_b, wm, wmt, bias, *deps)


N_PAIR = B_W // 128
HEADS_PER_GROUP = 3


def _pair_groups(p):
    return (2 * p) // HEADS_PER_GROUP, (2 * p + 1) // HEADS_PER_GROUP


def _ssd_chunk(pb_ref, halo, buf_ref, cw, cb, dtb, alog, pre_ref=None):
    z = pb_ref[:, 0:B_W]
    if pre_ref is None:
        pre = _conv_fwd(buf_ref, halo, pb_ref[:, B_W:B_W + XBC_W], cw, cb, CHUNK)
    else:
        buf_ref[0:8, :] = halo
        buf_ref[8:8 + CHUNK, :] = pb_ref[:, B_W:B_W + XBC_W]
        pre = pre_ref[...]
    sg = _sigmoid(pre)
    xbc = pre * sg
    xs = xbc[:, 0:B_W]
    bm = [xbc[:, B_W + k * B_STATE:B_W + (k + 1) * B_STATE] for k in range(2)]
    cm = [xbc[:, B_W + (2 + k) * B_STATE:B_W + (3 + k) * B_STATE] for k in range(2)]
    dtin = pb_ref[:, B_W + XBC_W:PB_W] + dtb
    dt = _softplus(dtin)
    a = -jnp.exp(alog)
    cs = _cumsum_rows(dt * a)
    return dict(z=z, pre=pre, sg=sg, xs=xs, bm=bm, cm=cm, dtin=dtin, dt=dt, a=a, cs=cs,
                ecs=jnp.exp(cs), ds=jnp.exp(cs[CHUNK - 1:CHUNK, :] - cs), xdt=xs * dt,
                bmb=[v.astype(bf16) for v in bm], cmb=[v.astype(bf16) for v in cm])


def _ssd_decay(cs_pair, half):
    cst = cs_pair.T
    lane0 = HEAD * half
    csc = jnp.broadcast_to(cs_pair[:, lane0:lane0 + 1], (CHUNK, CHUNK))
    csr = cst[lane0:lane0 + 1, :]
    tri = lax.broadcasted_iota(jnp.int32, (CHUNK, CHUNK), 0) >= lax.broadcasted_iota(jnp.int32, (CHUNK, CHUNK), 1)
    return jnp.exp(jnp.where(tri, csc - csr, NEG_BIG)), cst


def _ssd_fwd(pb, conv_w, conv_b, dtb, alog, dskip, norm_g, l):
    t = pb.shape[0]
    nc = t // CHUNK

    def body(pb_ref, halo_ref, cw_ref, cb_ref, dtb_ref, alog_ref, d_ref, ng_ref, yb_ref, yp_ref, sp_ref, pre_ref,
             buf_ref, s_ref):
        i = pl.program_id(0)

        @pl.when(i == 0)
        def _():
            s_ref[...] = jnp.zeros_like(s_ref)

        halo = jnp.where(i > 0, halo_ref[:, B_W:B_W + XBC_W], 0.0)
        q = _ssd_chunk(pb_ref, halo, buf_ref, cw_ref[...], cb_ref[...], dtb_ref[...], alog_ref[...])
        pre_ref[...] = q["pre"]
        sp_ref[0] = s_ref[...]
        lane = lax.broadcasted_iota(jnp.int32, (CHUNK, 128), 1)
        rowi = lax.broadcasted_iota(jnp.int32, (128, B_STATE), 0)
        cb_mat = [_nt(q["cmb"][k], q["bmb"][k]) for k in range(2)]
        xd = q["xdt"] * q["ds"]
        for p in range(N_PAIR):
            cols = slice(128 * p, 128 * (p + 1))
            g_lo, g_hi = _pair_groups(p)
            cs_p, xdt_p = q["cs"][:, cols], q["xdt"][:, cols]
            s_p = s_ref[cols, :]
            s_pb = s_p.astype(bf16)
            y_p = jnp.zeros((CHUNK, 128), f32)
            for half, grp in ((0, g_lo), (1, g_hi)):
                lm, cst = _ssd_decay(cs_p, half)
                mb = (cb_mat[grp] * lm).astype(bf16)
                sel = (lane < HEAD) if half == 0 else (lane >= HEAD)
                y_p = y_p + _nn(mb, jnp.where(sel, xdt_p, 0.0).astype(bf16))
            off_lo = _nt(q["cmb"][g_lo], s_pb)
            off = off_lo if g_lo == g_hi else jnp.where(lane < HEAD, off_lo, _nt(q["cmb"][g_hi], s_pb))
            y_p = y_p + off * q["ecs"][:, cols] + q["xs"][:, cols] * d_ref[:, cols]
            yp_ref[:, cols] = y_p
            xd_pb = xd[:, cols].astype(bf16)
            upd_lo = _tn(xd_pb, q["bmb"][g_lo])
            upd = upd_lo if g_lo == g_hi else jnp.where(rowi < HEAD, upd_lo, _tn(xd_pb, q["bmb"][g_hi]))
            cd = jnp.exp(jnp.broadcast_to(cst[:, CHUNK - 1:CHUNK], (128, B_STATE)))
            s_ref[cols, :] = cd * s_p + upd
        z = q["z"]
        yg = yp_ref[...] * (z * _sigmoid(z))
        yb_ref[...] = _rms_fwd(yg, ng_ref[...])

    vec = pl.BlockSpec((1, B_W), lambda i: (0, 0))
    row = pl.BlockSpec((CHUNK, B_W), lambda i: (i, 0))
    return pl.pallas_call(
        body, name="ssd_fwd", grid=(nc,),
        in_specs=[pl.BlockSpec((CHUNK, PB_W), lambda i: (i, 0)),
                  pl.BlockSpec((8, PB_W), lambda i: (jnp.maximum(i * (CHUNK // 8) - 1, 0), 0)),
                  *[_lspec(a, l) for a in (conv_w, conv_b, dtb, alog, dskip, norm_g)]],
        out_specs=[row, row, pl.BlockSpec((1, B_W, B_STATE), lambda i: (i, 0, 0)),
                   pl.BlockSpec((CHUNK, XBC_W), lambda i: (i, 0))],
        out_shape=[jax.ShapeDtypeStruct((t, B_W), f32), jax.ShapeDtypeStruct((t, B_W), f32),
                   jax.ShapeDtypeStruct((nc, B_W, B_STATE), f32), jax.ShapeDtypeStruct((t, XBC_W), f32)],
        scratch_shapes=[pltpu.VMEM((8 + CHUNK, XBC_W), f32), pltpu.VMEM((B_W, B_STATE), f32)],
        compiler_params=_params(("arbitrary",)),
    )(pb, pb, conv_w, conv_b, dtb, alog, dskip, norm_g)


def _ssd_bwd(pb, yp, sprev, pre, dyb, conv_w, conv_b, dtb, alog, dskip, norm_g, l):
    t = pb.shape[0]
    nc = t // CHUNK

    def body(pb_ref, halo_ref, yp_ref, sp_ref, pre_ref, dyb_ref, cw_ref, cb_ref, dtb_ref, alog_ref, d_ref, ng_ref,
             dpb_ref, dcw_ref, dcb_ref, ddtb_ref, dalog_ref, dd_ref, dng_ref,
             buf_ref, dbuf_ref, ds_ref, dnext_ref, dxbc_ref, dcs_ref, dxdt_ref):
        i = pl.program_id(0)
        c = nc - 1 - i

        @pl.when(i == 0)
        def _():
            ds_ref[...] = jnp.zeros_like(ds_ref)
            dnext_ref[...] = jnp.zeros_like(dnext_ref)
            for ref in (dcw_ref, dcb_ref, ddtb_ref, dalog_ref, dd_ref, dng_ref):
                ref[...] = jnp.zeros_like(ref)

        halo = jnp.where(c > 0, halo_ref[:, B_W:B_W + XBC_W], 0.0)
        cw = cw_ref[...]
        q = _ssd_chunk(pb_ref, halo, buf_ref, cw, cb_ref[...], dtb_ref[...], alog_ref[...], pre_ref)
        z, xs, dt, a, ecs, dsd, xdt =q["z"], q["xs"], q["dt"], q["a"], q["ecs"], q["ds"], q["xdt"]
        sz = _sigmoid(z)
        siluz = z * sz
        yp = yp_ref[...]
        dyg, dng = _rms_bwd(yp * siluz, ng_ref[...], dyb_ref[...])
        dng_ref[...] += dng
        dy = dyg * siluz
        dpb_ref[:, 0:B_W] = (dyg * yp * _silu_grad(z, sz)).astype(bf16)
        dd_ref[...] += jnp.sum(dy * xs, axis=0, keepdims=True)
        g1 = dy * ecs
        lane = lax.broadcasted_iota(jnp.int32, (CHUNK, 128), 1)
        rowi = lax.broadcasted_iota(jnp.int32, (128, B_STATE), 0)
        rowc = lax.broadcasted_iota(jnp.int32, (CHUNK, 128), 0)
        cb_mat = [_nt(q["cmb"][k], q["bmb"][k]) for k in range(2)]
        d_cb = [jnp.zeros((CHUNK, CHUNK), f32) for _ in range(2)]
        d_b = [jnp.zeros((CHUNK, B_STATE), f32) for _ in range(2)]
        d_c = [jnp.zeros((CHUNK, B_STATE), f32) for _ in range(2)]
        for p in range(N_PAIR):
            cols = slice(128 * p, 128 * (p + 1))
            g_lo, g_hi = _pair_groups(p)
            lo, hi = lane < HEAD, lane >= HEAD
            cs_p, xdt_p, dy_p, ds_p, g1_p = q["cs"][:, cols], xdt[:, cols], dy[:, cols], dsd[:, cols], g1[:, cols]
            s_p = sp_ref[0, cols, :]
            s_pb = s_p.astype(bf16)
            dsn = ds_ref[cols, :]
            dsnb = dsn.astype(bf16)
            g1b = g1_p.astype(bf16)
            off_lo = _nt(q["cmb"][g_lo], s_pb)
            off = off_lo if g_lo == g_hi else jnp.where(lo, off_lo, _nt(q["cmb"][g_hi], s_pb))
            dcs_p = dy_p * off * ecs[:, cols]
            dsp_lo = _tn(g1b, q["cmb"][g_lo])
            dsp = dsp_lo if g_lo == g_hi else jnp.where(rowi < HEAD, dsp_lo, _tn(g1b, q["cmb"][g_hi]))
            dx_lo = _nt(q["bmb"][g_lo], dsnb)
            dxd = dx_lo if g_lo == g_hi else jnp.where(lo, dx_lo, _nt(q["bmb"][g_hi], dsnb))
            xd_p = xdt_p * ds_p
            if g_lo == g_hi:
                d_c[g_lo] = d_c[g_lo] + _nn(g1b, s_pb)
                d_b[g_lo] = d_b[g_lo] + _nn(xd_p.astype(bf16), dsnb)
            else:
                d_c[g_lo] = d_c[g_lo] + _nn(jnp.where(lo, g1_p, 0.0).astype(bf16), s_pb)
                d_c[g_hi] = d_c[g_hi] + _nn(jnp.where(hi, g1_p, 0.0).astype(bf16), s_pb)
                d_b[g_lo] = d_b[g_lo] + _nn(jnp.where(lo, xd_p, 0.0).astype(bf16), dsnb)
                d_b[g_hi] = d_b[g_hi] + _nn(jnp.where(hi, xd_p, 0.0).astype(bf16), dsnb)
            dxdt_p = dxd * ds_p
            t2 = dxd * xdt_p * ds_p
            dcs_p = dcs_p - t2
            dlast = jnp.sum(t2, axis=0, keepdims=True)
            cst = None
            for half, grp in ((0, g_lo), (1, g_hi)):
                sel = lo if half == 0 else hi
                lm, cst = _ssd_decay(cs_p, half)
                m = cb_mat[grp] * lm
                dyh = jnp.where(sel, dy_p, 0.0).astype(bf16)
                xdh = jnp.where(sel, xdt_p, 0.0).astype(bf16)
                dm = _nt(dyh, xdh)
                pm = dm * m
                col = jnp.sum(pm, axis=1, keepdims=True) - jnp.sum(pm.T, axis=1, keepdims=True)
                dcs_p = dcs_p + jnp.where(lane == HEAD * half, col, 0.0)
                d_cb[grp] = d_cb[grp] + dm * lm
                dxdt_p = dxdt_p + _tn(m.astype(bf16), dyh)
            cdcol = jnp.exp(jnp.broadcast_to(cst[:, CHUNK - 1:CHUNK], (128, B_STATE)))
            ds_ref[cols, :] = cdcol * dsn + dsp
            dcd_row = jnp.sum((dsn * s_p).T, axis=0, keepdims=True)
            dlast = dlast + dcd_row * ecs[CHUNK - 1:CHUNK, cols]
            dcs_ref[:, cols] = dcs_p + jnp.where(rowc == CHUNK - 1, dlast, 0.0)
            dxdt_ref[:, cols] = dxdt_p
        for k in range(2):
            dcbb = d_cb[k].astype(bf16)
            d_c[k] = d_c[k] + _nn(dcbb, q["bmb"][k])
            d_b[k] = d_b[k] + _tn(dcbb, q["cmb"][k])
            dxbc_ref[:, B_W + k * B_STATE:B_W + (k + 1) * B_STATE] = d_b[k]
            dxbc_ref[:, B_W + (2 + k) * B_STATE:B_W + (3 + k) * B_STATE] = d_c[k]
        dxdt = dxdt_ref[...]
        dxbc_ref[:, 0:B_W] = dy * d_ref[...] + dxdt * dt
        dcs = dcs_ref[...]
        dad = jnp.sum(dcs, axis=0, keepdims=True) - _cumsum_rows(dcs) + dcs
        ddt = dxdt * xs + dad * a
        dalog_ref[...] += jnp.sum(dad * dt, axis=0, keepdims=True) * a
        dtraw = ddt * _sigmoid(q["dtin"])
        ddtb_ref[...] += jnp.sum(dtraw, axis=0, keepdims=True)
        dpb_ref[:, B_W + XBC_W:PB_W] = dtraw.astype(bf16)
        dpre = dxbc_ref[...] * _silu_grad(q["pre"], q["sg"])
        dx, dw, db = _conv_bwd(buf_ref, dbuf_ref, dpre, dnext_ref[...], cw, CHUNK)
        dnext_ref[...] = dpre[0:8, :]
        dcw_ref[...] += dw
        dcb_ref[...] += db
        dpb_ref[:, B_W:B_W + XBC_W] = dx.astype(bf16)

    vec = pl.BlockSpec((1, B_W), lambda i: (0, 0))
    cwspec = pl.BlockSpec((4, XBC_W), lambda i: (0, 0))
    cbspec = pl.BlockSpec((1, XBC_W), lambda i: (0, 0))

    def rev(w):
        return pl.BlockSpec((CHUNK, w), lambda i: (nc - 1 - i, 0))

    vshape = jax.ShapeDtypeStruct((1, B_W), f32)
    return pl.pallas_call(
        body, name="ssd_bwd", grid=(nc,),
        in_specs=[rev(PB_W), pl.BlockSpec((8, PB_W), lambda i: (jnp.maximum((nc - 1 - i) * (CHUNK // 8) - 1, 0), 0)),
                  rev(B_W), pl.BlockSpec((1, B_W, B_STATE), lambda i: (nc - 1 - i, 0, 0)), rev(XBC_W), rev(B_W),
                  *[_lspec(a, l) for a in (conv_w, conv_b, dtb, alog, dskip, norm_g)]],
        out_specs=[rev(PB_W), cwspec, cbspec, vec, vec, vec, vec],
        out_shape=[jax.ShapeDtypeStruct((t, PB_W), bf16), jax.ShapeDtypeStruct((4, XBC_W), f32),
                   jax.ShapeDtypeStruct((1, XBC_W), f32), vshape, vshape, vshape, vshape],
        scratch_shapes=[pltpu.VMEM((8 + CHUNK, XBC_W), f32), pltpu.VMEM((CHUNK + 8, XBC_W), f32),
                        pltpu.VMEM((B_W, B_STATE), f32), pltpu.VMEM((8, XBC_W), f32),
                        pltpu.VMEM((CHUNK, XBC_W), f32), pltpu.VMEM((CHUNK, B_W), f32), pltpu.VMEM((CHUNK, B_W), f32)],
        compiler_params=_params(("arbitrary",)),
    )(pb, pb, yp, sprev, pre, dyb, conv_w, conv_b, dtb, alog, dskip, norm_g)


def _loss_fwd(y, target):
    t, d = y.shape
    tm = _tile(t, 512)

    def body(y_ref, t_ref, dy_ref, loss_ref):
        @pl.when(pl.program_id(0) == 0)
        def _():
            loss_ref[...] = jnp.zeros_like(loss_ref)

        e = y_ref[...] - t_ref[...]
        dy_ref[...] = e * (1.0 / d)
        per_tok = jnp.mean(e * e, axis=-1, keepdims=True)
        loss_ref[...] += 0.5 * jnp.sum(per_tok, axis=0, keepdims=True)

    row = pl.BlockSpec((tm, d), lambda i: (i, 0))
    return pl.pallas_call(
        body, name="loss_fwd", grid=(t // tm,), in_specs=[row, row],
        out_specs=[row, pl.BlockSpec((1, 128), lambda i: (0, 0))],
        out_shape=[jax.ShapeDtypeStruct((t, d), f32), jax.ShapeDtypeStruct((1, 128), f32)],
        compiler_params=_params(("arbitrary",)),
    )(y, target)


def _row_tile(r):
    return 512 if r % 512 == 0 else r


def _pair_add(g, r, c_dev):
    _, nl, rows, cols = g.shape
    tr = _row_tile(rows)

    def body(c_ref, g_ref, r_ref, o_ref):
        o_ref[...] = (g_ref[...].astype(f32) + r_ref[...].astype(f32)).astype(bf16)

    blk = (None, None, tr, cols)
    return pl.pallas_call(
        body, name="pair_add",
        grid_spec=pltpu.PrefetchScalarGridSpec(
            num_scalar_prefetch=1, grid=(4, nl, rows // tr),
            in_specs=[pl.BlockSpec(blk, lambda b, l, i, c: (2 * b + c[0], l, i, 0)),
                      pl.BlockSpec(blk, lambda b, l, i, c: (b, l, i, 0))],
            out_specs=pl.BlockSpec(blk, lambda b, l, i, c: (b, l, i, 0))),
        out_shape=jax.ShapeDtypeStruct(r.shape, bf16),
        compiler_params=_params(("arbitrary", "arbitrary", "arbitrary")),
    )(c_dev, g, r)


def _grad_sum(s, q, b_dev):
    _, nl, rows, cols = s.shape
    tr = _row_tile(rows)

    def body(b_ref, s_ref, q0_ref, q1_ref, q2_ref, o_ref):
        o_ref[...] = ((s_ref[...].astype(f32) + q0_ref[...].astype(f32)) + q1_ref[...].astype(f32)) + q2_ref[...].astype(f32)

    blk = (None, None, tr, cols)

    def qspec(k):
        return pl.BlockSpec(blk, lambda l, i, b: (k, l, i, 0))

    return pl.pallas_call(
        body, name="grad_sum",
        grid_spec=pltpu.PrefetchScalarGridSpec(
            num_scalar_prefetch=1, grid=(nl, rows // tr),
            in_specs=[pl.BlockSpec(blk, lambda l, i, b: (b[0], l, i, 0)), qspec(0), qspec(1), qspec(2)],
            out_specs=pl.BlockSpec((None, tr, cols), lambda l, i, b: (l, i, 0))),
        out_shape=jax.ShapeDtypeStruct(s.shape[1:], f32),
        compiler_params=_params(("arbitrary", "arbitrary")),
    )(b_dev, s, q, q, q)


def _sum_devices(parts):
    n, rows, cols = parts.shape
    tr = _row_tile(rows)

    def body(p_ref, o_ref):
        acc = p_ref[0]
        for k in range(1, n):
            acc = acc + p_ref[k]
        o_ref[...] = acc

    return pl.pallas_call(
        body, name="sum_devices", grid=(rows // tr,),
        in_specs=[pl.BlockSpec((n, tr, cols), lambda i: (0, i, 0))],
        out_specs=pl.BlockSpec((tr, cols), lambda i: (i, 0)),
        out_shape=jax.ShapeDtypeStruct((rows, cols), f32),
        compiler_params=_params(("arbitrary",)),
    )(parts)


def _adamw(w, m, v, g):
    nl, rows, cols = w.shape
    tr = _row_tile(rows)

    def body(w_ref, m_ref, v_ref, g_ref, d_ref, nm_ref, nv_ref):
        d_ref[...], nm_ref[...], nv_ref[...] = _adamw_math(w_ref[...], m_ref[...], v_ref[...], g_ref[...])

    blk = pl.BlockSpec((None, tr, cols), lambda l, i: (l, i, 0))
    shape = jax.ShapeDtypeStruct(w.shape, f32)
    return pl.pallas_call(
        body, name="adamw", grid=(nl, rows // tr), in_specs=[blk] * 4, out_specs=[blk] * 3,
        out_shape=[shape] * 3, compiler_params=_params(("arbitrary", "arbitrary")),
    )(w, m, v, g)


def _adamw_math(w, m, v, g):
    nm = ADAM_B1 * m + (1.0 - ADAM_B1) * g
    nv = ADAM_B2 * v + (1.0 - ADAM_B2) * (g * g)
    m_hat = nm / (1.0 - ADAM_B1 ** ADAM_STEP)
    v_hat = nv / (1.0 - ADAM_B2 ** ADAM_STEP)
    return -ADAM_LR * (m_hat / (jnp.sqrt(v_hat) + ADAM_EPS) + ADAM_WD * w), nm, nv


def _adamw_layer(w, m, v, s, q, b_dev, outs, l, deps=()):
    _, rows, cols = w.shape
    tr = _row_tile(rows)

    def body(b_ref, w_ref, m_ref, v_ref, s_ref, q0_ref, q1_ref, q2_ref, o0, o1, o2, o3, g_ref, d_ref, nm_ref, nv_ref):
        g = ((s_ref[...].astype(f32) + q0_ref[...].astype(f32)) + q1_ref[...].astype(f32)) + q2_ref[...].astype(f32)
        g_ref[...] = g
        d_ref[...], nm_ref[...], nv_ref[...] = _adamw_math(w_ref[...], m_ref[...], v_ref[...], g)

    wspec = pl.BlockSpec((None, tr, cols), lambda i, b: (l, i, 0))
    blk = (None, None, tr, cols)

    def qspec(k):
        return pl.BlockSpec(blk, lambda i, b: (k, 0, i, 0))

    shape = jax.ShapeDtypeStruct(w.shape, f32)
    return pl.pallas_call(
        _after(body, 12, deps), name="adamw_layer",
        grid_spec=pltpu.PrefetchScalarGridSpec(
            num_scalar_prefetch=1, grid=(rows // tr,),
            in_specs=[wspec] * 3 + [pl.BlockSpec(blk, lambda i, b: (b[0], 0, i, 0)), qspec(0), qspec(1), qspec(2)]
            + [ANY] * (4 + len(deps)),
            out_specs=[wspec] * 4),
        out_shape=[shape] * 4, input_output_aliases={8 + k: k for k in range(4)},
        compiler_params=_params(("arbitrary",)),
    )(b_dev, w, m, v, s, q, q, q, *outs, *deps)


def _place():
    return lax.axis_index("x"), lax.axis_index("y"), lax.axis_index("c")


def _all_gather(shards, deps=()):
    n = len(shards)
    nd = len(deps)

    def body(*refs):
        src, dst = refs[:n], refs[n:2 * n]
        send_sems, recv_sems, local_sems = refs[2 * n:]
        x, y, c = _place()
        me, sibling = (x, y, c), (x, y, 1 - c)
        chips = [(1 - x, y), (x, 1 - y), (1 - x, 1 - y)]

        def copy(a, k, block, to, from_shard=False):
            px, py, pc = block
            rows = dst[a].at[4 * px + 2 * py + pc]
            return pltpu.make_async_remote_copy(
                src_ref=src[a] if from_shard else rows, dst_ref=rows,
                send_sem=send_sems.at[a, k], recv_sem=recv_sems.at[a, k], device_id=to, device_id_type=MESH)

        mine = [pltpu.make_async_copy(src[a], dst[a].at[4 * x + 2 * y + c], local_sems.at[a]) for a in range(n)]
        for cp in mine:
            cp.start()
        first = []
        for a in range(n):
            first.append(copy(a, 0, me, sibling, True))
            first += [copy(a, 1 + j, me, (*chip, c), True) for j, chip in enumerate(chips)]
        for cp in first:
            cp.start()
        passed = []
        for j, chip in enumerate(chips):
            for a in range(n):
                copy(a, 1 + j, (*chip, c), me).wait_recv()
                fwd = copy(a, 4 + j, (*chip, c), sibling)
                fwd.start()
                passed.append(fwd)
        for a in range(n):
            copy(a, 0, sibling, me).wait_recv()
            for j, chip in enumerate(chips):
                copy(a, 4 + j, (*chip, 1 - c), me).wait_recv()
        for cp in first + passed:
            cp.wait_send()
        for cp in mine:
            cp.wait()

    return pl.pallas_call(
        _after(body, n, deps), name="all_gather", in_specs=[ANY] * (n + nd), out_specs=[ANY] * n,
        out_shape=[jax.ShapeDtypeStruct((N_DEV,) + s.shape, s.dtype) for s in shards],
        scratch_shapes=[pltpu.SemaphoreType.DMA((n, 7)), pltpu.SemaphoreType.DMA((n, 7)), pltpu.SemaphoreType.DMA((n,))],
    )(*shards, *deps)


HBM = pl.BlockSpec(memory_space=pltpu.HBM)
SEM = pl.BlockSpec(memory_space=pltpu.SEMAPHORE)
_EFFECT = pltpu.SideEffectType.DATAFLOW_SIDE_EFFECTING


def _split_start(name, srcs, dsts, sem_shape, plan):
    ns, nb = len(srcs), len(srcs) + len(dsts)

    def body(*refs):
        send_sems, recv_sems = refs[nb], refs[nb + 1]
        for cp in plan(refs[:ns], refs[ns:nb], send_sems, recv_sems):
            cp.start()
        refs[-1][...] = jnp.zeros_like(refs[-1])

    bufs = list(srcs) + list(dsts)
    return pl.pallas_call(
        body, name=name,
        out_shape=(pltpu.SemaphoreType.DMA(sem_shape), pltpu.SemaphoreType.DMA(sem_shape),
                   *[pltpu.HBM(a.shape, a.dtype) for a in bufs], jax.ShapeDtypeStruct((8, 128), f32)),
        in_specs=[HBM] * nb, out_specs=(SEM, SEM, *[HBM] * nb, pl.BlockSpec(memory_space=pltpu.VMEM)),
        input_output_aliases={i: 2 + i for i in range(nb)},
        compiler_params=pltpu.CompilerParams(has_side_effects=_EFFECT),
    )(*[pltpu.with_memory_space_constraint(a, pltpu.HBM) for a in bufs])


def _split_wait(name, started, ns, plan, after):
    send_sems, recv_sems = started[0], started[1]
    bufs = list(started[2:-1])
    nb = len(bufs)
    after = list(after) if isinstance(after, (list, tuple)) else [after]

    def body(*refs):
        for cp in plan(refs[:ns], refs[ns:nb], refs[nb], refs[nb + 1]):
            cp.wait_send()
            cp.wait_recv()

    return pl.pallas_call(
        body, name=name, out_shape=tuple(pltpu.HBM(a.shape, a.dtype) for a in bufs),
        in_specs=[HBM] * nb + [SEM, SEM] + [ANY] * len(after), out_specs=tuple([HBM] * nb),
        input_output_aliases={i: i for i in range(nb)},
        compiler_params=pltpu.CompilerParams(has_side_effects=_EFFECT),
    )(*bufs, send_sems, recv_sems, *after)


def _remote(src, dst, send_sem, recv_sem, to):
    return pltpu.make_async_remote_copy(src_ref=src, dst_ref=dst, send_sem=send_sem, recv_sem=recv_sem,
                                        device_id=to, device_id_type=MESH)


def _gather_plan(src, dst, send_sems, recv_sems):
    x, y, c = _place()
    peers = [(x, y, 1 - c), (1 - x, y, c), (x, 1 - y, c), (1 - x, 1 - y, c)]
    copies = []
    for a in range(len(dst)):
        rows = dst[a].at[4 * x + 2 * y + c]
        copies += [_remote(rows, rows, send_sems.at[4 * a + k], recv_sems.at[4 * a + k], peer) for k, peer in enumerate(peers)]
    return copies


def _pair_plan(src, dst, send_sems, recv_sems):
    x, y, c = _place()
    return [_remote(src[a].at[2 * b + (1 - c)], dst[a].at[b], send_sems.at[4 * a + b], recv_sems.at[4 * a + b], (x, y, 1 - c))
            for a in range(len(src)) for b in range(4)]


def _chips_plan(src, dst, send_sems, recv_sems):
    x, y, c = _place()
    chips = [(1 - x, y), (x, 1 - y), (1 - x, 1 - y)]
    return [_remote(src[a].at[2 * px + py], dst[a].at[j], send_sems.at[3 * a + j], recv_sems.at[3 * a + j], (px, py, c))
            for a in range(len(src)) for j, (px, py) in enumerate(chips)]


def _gather_finish(bufs):
    n = len(bufs)

    def body(*refs):
        dst = refs[n:2 * n]
        send_sems, recv_sems = refs[2 * n:]
        x, y, c = _place()
        chips = [(1 - x, y), (x, 1 - y), (1 - x, 1 - y)]
        passed = []
        for a in range(n):
            for j, (px, py) in enumerate(chips):
                rows = dst[a].at[4 * px + 2 * py + c]
                passed.append(_remote(rows, rows, send_sems.at[a, j], recv_sems.at[a, j], (x, y, 1 - c)))
        for cp in passed:
            cp.start()
        for cp in passed:
            cp.wait_send()
        for a in range(n):
            for j, (px, py) in enumerate(chips):
                rows = dst[a].at[4 * px + 2 * py + (1 - c)]
                _remote(rows, rows, send_sems.at[a, j], recv_sems.at[a, j], (x, y, 1 - c)).wait_recv()

    return pl.pallas_call(
        body, name="gather_finish", in_specs=[ANY] * n, out_specs=[ANY] * n,
        out_shape=[jax.ShapeDtypeStruct(b.shape, b.dtype) for b in bufs],
        input_output_aliases={a: a for a in range(n)},
        scratch_shapes=[pltpu.SemaphoreType.DMA((n, 3)), pltpu.SemaphoreType.DMA((n, 3))],
    )(*bufs)


def _place_shards(mats, l, dev):
    n = len(mats)

    def body(dev_ref, *refs):
        for a in range(n):
            refs[n + a][...] = refs[a][...].astype(bf16)

    return pl.pallas_call(
        body, name="place_shards",
        grid_spec=pltpu.PrefetchScalarGridSpec(
            num_scalar_prefetch=1, grid=(1,),
            in_specs=[pl.BlockSpec((None,) + m.shape[1:], lambda i, dv: (l, 0, 0)) for m in mats],
            out_specs=[pl.BlockSpec((None, None) + m.shape[1:], lambda i, dv: (dv[0], 0, 0, 0)) for m in mats]),
        out_shape=[jax.ShapeDtypeStruct((N_DEV, 1) + m.shape[1:], bf16) for m in mats],
        compiler_params=_params(("arbitrary",)),
    )(dev, *mats)


BIG = ("ffn1_w_gu", "ffn1_w_down", "mix_w_in", "mix_w_out", "ffn2_w_gu", "ffn2_w_down")
SHARDED_CONV = ("lru_conv_w", "ssd_conv_w")
REPLICATED = ("ffn1_pre_g", "ffn1_post_g", "mix_pre_g", "mix_post_g", "lru_conv_b", "lru_w_r", "lru_b_r", "lru_w_i",
              "lru_b_i", "lru_lambda", "ssd_conv_b", "ssd_dt_bias", "ssd_a_log", "ssd_d", "ssd_norm_g", "sgu_ln_g",
              "sgu_ln_b", "sgu_w_s", "sgu_b_s", "ffn2_pre_g", "ffn2_post_g")
WEIGHTS = ("ffn1_pre_g", "ffn1_post_g", "ffn1_w_gu", "ffn1_w_down", "mix_pre_g", "mix_post_g", "mix_w_in", "mix_w_out",
           "lru_conv_w", "lru_conv_b", "lru_w_r", "lru_b_r", "lru_w_i", "lru_b_i", "lru_lambda", "ssd_conv_w",
           "ssd_conv_b", "ssd_dt_bias", "ssd_a_log", "ssd_d", "ssd_norm_g", "sgu_ln_g", "sgu_ln_b", "sgu_w_s", "sgu_b_s",
           "ffn2_pre_g", "ffn2_post_g", "ffn2_w_gu", "ffn2_w_down")
DT_LO = PA_W + B_W + XBC_W
N_HEADS = B_W // HEAD
PACK_COLS = 1024


def _pack(arrays):
    flat = jnp.concatenate([a.reshape(-1) for a in arrays])
    rows = -(-flat.shape[0] // (8 * PACK_COLS)) * 8
    return jnp.pad(flat, (0, rows * PACK_COLS - flat.shape[0])).reshape(rows, PACK_COLS)


def _unpack(packed, shapes):
    flat = packed.reshape(-1)
    out, off = [], 0
    for s in shapes:
        size = 1
        for dim in s:
            size *= dim
        out.append(flat[off:off + size].reshape(s))
        off += size
    return out


def _widen_w_in(w):
    return jnp.concatenate([w[..., :DT_LO], jnp.repeat(w[..., DT_LO:DT_LO + N_HEADS], HEAD, axis=-1),
                            w[..., DT_LO + N_HEADS:]], axis=-1)


def _narrow_w_in_grad(g):
    dt = g[..., DT_LO:DT_LO + B_W]
    dt = dt.reshape(dt.shape[:-1] + (N_HEADS, HEAD)).sum(-1)
    return jnp.concatenate([g[..., :DT_LO], dt, g[..., DT_LO + B_W:]], axis=-1)


def _per_head(a):
    return a.reshape(a.shape[:-1] + (N_HEADS, HEAD)).sum(-1)


def kernel(x, ffn1_pre_g, ffn1_post_g, ffn1_w_gu, ffn1_w_down, mix_pre_g, mix_post_g, mix_w_in, mix_w_out, lru_conv_w, lru_conv_b, lru_w_r, lru_b_r, lru_w_i, lru_b_i, lru_lambda, ssd_conv_w, ssd_conv_b, ssd_dt_bias, ssd_a_log, ssd_d, ssd_norm_g, sgu_ln_g, sgu_ln_b, sgu_w_s, sgu_b_s, ffn2_pre_g, ffn2_post_g, ffn2_w_gu, ffn2_w_down, loss_target, m_ffn1_pre_g, m_ffn1_post_g, m_ffn1_w_gu, m_ffn1_w_down, m_mix_pre_g, m_mix_post_g, m_mix_w_in, m_mix_w_out, m_lru_conv_w, m_lru_conv_b, m_lru_w_r, m_lru_b_r, m_lru_w_i, m_lru_b_i, m_lru_lambda, m_ssd_conv_w, m_ssd_conv_b, m_ssd_dt_bias, m_ssd_a_log, m_ssd_d, m_ssd_norm_g, m_sgu_ln_g, m_sgu_ln_b, m_sgu_w_s, m_sgu_b_s, m_ffn2_pre_g, m_ffn2_post_g, m_ffn2_w_gu, m_ffn2_w_down, v_ffn1_pre_g, v_ffn1_post_g, v_ffn1_w_gu, v_ffn1_w_down, v_mix_pre_g, v_mix_post_g, v_mix_w_in, v_mix_w_out, v_lru_conv_w, v_lru_conv_b, v_lru_w_r, v_lru_b_r, v_lru_w_i, v_lru_b_i, v_lru_lambda, v_ssd_conv_w, v_ssd_conv_b, v_ssd_dt_bias, v_ssd_a_log, v_ssd_d, v_ssd_norm_g, v_sgu_ln_g, v_sgu_ln_b, v_sgu_w_s, v_sgu_b_s, v_ffn2_pre_g, v_ffn2_post_g, v_ffn2_w_gu, v_ffn2_w_down):
    given = dict(locals())
    w = {n: given[n] for n in WEIGHTS}
    mom = {n: given["m_" + n] for n in WEIGHTS}
    var = {n: given["v_" + n] for n in WEIGHTS}
    nl = ffn1_pre_g.shape[0]
    _, t, d = x.shape
    xi, yi, ci = _place()
    dev = 4 * xi + 2 * yi + ci
    c_dev = jnp.reshape(ci, (1,)).astype(jnp.int32)
    b_dev = jnp.reshape(2 * xi + yi, (1,)).astype(jnp.int32)

    conv_shapes = [lru_conv_w.shape, ssd_conv_w.shape]
    shards = [ffn1_w_gu, ffn1_w_down, _widen_w_in(mix_w_in), mix_w_out, ffn2_w_gu, ffn2_w_down]
    nbig = len(shards)
    dev_arr = jnp.reshape(dev, (1,)).astype(jnp.int32)
    conv_pack = _pack([lru_conv_w, ssd_conv_w])
    conv_buf = lax.dynamic_update_slice_in_dim(jnp.zeros((N_DEV,) + conv_pack.shape, f32), conv_pack[None], dev, axis=0)
    def gather_groups(l):
        return [(0, 1), (2, 3), (4, 5)] if l == 0 else [tuple(range(nbig))]

    gather_started = {}
    for l in range(nl):
        for gi, idx in enumerate(gather_groups(l)):
            bufs = list(_place_shards([shards[i] for i in idx], l, dev_arr)) + ([conv_buf] if (l, gi) == (0, 1) else [])
            gather_started[l, gi] = _split_start(f"gather_start_{l}_{gi}", [], bufs, (4 * len(bufs),), _gather_plan)

    def finish_gather(l, gi, after):
        waited = _split_wait(f"gather_wait_{l}_{gi}", gather_started[l, gi], 0, _gather_plan, after)
        return _gather_finish(list(waited))

    def conv_taps(conv_all):
        full = []
        for k, shape in enumerate(conv_shapes):
            per_dev = jnp.stack([_unpack(conv_all[s], conv_shapes)[k] for s in range(N_DEV)], axis=2)
            full.append(per_dev.reshape(shape[0], shape[1], N_DEV * shape[2]))
        return full

    def vec(a):
        return a.reshape(nl, 1, -1)

    def per_channel(a):
        return jnp.repeat(a, HEAD, axis=-1).reshape(nl, 1, B_W)

    eye = jnp.eye(A_W // HEAD, dtype=f32)

    def block_diag(a):
        return jnp.einsum("lhij,hg->lhigj", a, eye).reshape(nl, A_W, A_W).astype(bf16)

    causal = jnp.tril(jnp.ones((CHUNK, CHUNK), dtype=bool))
    p = dict(
        ffn1_pre=vec(ffn1_pre_g), ffn1_post=vec(ffn1_post_g), mix_pre=vec(mix_pre_g), mix_post=vec(mix_post_g),
        ffn2_pre=vec(ffn2_pre_g), ffn2_post=vec(ffn2_post_g),
        lru=(vec(lru_conv_b), block_diag(lru_w_r), block_diag(lru_w_i), vec(lru_b_r), vec(lru_b_i), vec(lru_lambda)),
        ssd=(vec(ssd_conv_b), per_channel(ssd_dt_bias), per_channel(ssd_a_log), per_channel(ssd_d), vec(ssd_norm_g)),
    )
    wm = jnp.where(causal, sgu_w_s, 0.0).astype(bf16)
    sgu_bias = jnp.repeat(jnp.swapaxes(sgu_b_s, 1, 2), HEAD, axis=2)
    sgu_f = (vec(sgu_ln_g), vec(sgu_ln_b), wm, sgu_bias)
    sgu_b = (vec(sgu_ln_g), vec(sgu_ln_b), wm, jnp.swapaxes(wm, 2, 3), sgu_bias)

    xs = x.reshape(t, d)
    saved, gathered = [], []
    for l in range(nl):
        x0 = xs
        if l == 0:
            wgu1, wd1 = finish_gather(0, 0, x0)
            deps = tuple(started[-1] for key, started in gather_started.items() if key != (0, 0))
        else:
            wgu1, wd1, win, wout, wgu2, wd2 = finish_gather(l, 0, x0)
            deps = ()
        x1, hb1, g1, u1, f1 = _ffn_fwd(x0, p["ffn1_pre"], p["ffn1_post"], wgu1, wd1, l, deps)
        if l == 0:
            win, wout, conv_all = finish_gather(0, 1, x1)
            lru_cw, ssd_cw = conv_taps(conv_all)
            p["lru"], p["ssd"] = (lru_cw,) + p["lru"], (ssd_cw,) + p["ssd"]
        hbm, pa, pb, pc = _mix_in_fwd(x1, p["mix_pre"], win, l)
        ya, h, gates = _lru_fwd(pa, *p["lru"], l)
        yb, yp, sp, pre = _ssd_fwd(pb, *p["ssd"], l)
        yc = _sgu_fwd(pc, *sgu_f, l)
        x2, cat, m = _mix_out_fwd(x1, ya, yb, yc, p["mix_post"], wout, l)
        if l == 0:
            wgu2, wd2 = finish_gather(0, 2, x2)
        xs, hb2, g2, u2, f2 = _ffn_fwd(x2, p["ffn2_pre"], p["ffn2_post"], wgu2, wd2, l)
        gathered.append((wgu1, wd1, win, wout, wgu2, wd2))
        saved.append((x0, hb1, g1, u1, f1, x1, hbm, pa, pb, pc, h, gates, yp, sp, pre, cat, m, x2, hb2, g2, u2, f2))
    dy, loss_part = _loss_fwd(xs, loss_target.reshape(t, d))
    loss = lax.psum(loss_part[0, 0], ("x", "y", "c"))

    small = {n: [None] * nl for n in REPLICATED + SHARDED_CONV}
    grads, delta, new_m, new_v = {}, {}, {}, {}
    fused = [n for n in BIG if n != "mix_w_in"]

    def oriented(a, n):
        return jnp.swapaxes(a, 1, 2) if n.endswith("w_gu") else a

    opt_in = {n: tuple(oriented(src[n], n) for src in (w, mom, var)) for n in fused}
    opt_out = {n: tuple(lax.empty(opt_in[n][0].shape, f32) for _ in range(4)) for n in fused}
    w_in_grads = [None] * nl
    grad_shapes = {n: (s.shape[2], s.shape[1]) if n.endswith("w_gu") else s.shape[1:] for n, s in zip(BIG, shards)}

    def start_pair(tag, lp, names, gbuf):
        landing = [lax.empty((4, 1) + grad_shapes[n], bf16) for n in names]
        started = _split_start(f"pair_start_{tag}", [gbuf[n] for n in names], landing, (4 * len(names),), _pair_plan)
        return tag, lp, names, started

    def finish_pair(pending, after):
        tag, lp, names, started = pending
        k = len(names)
        done = _split_wait(f"pair_wait_{tag}", started, k, _pair_plan, after)
        sums = [_pair_add(g, r, c_dev) for g, r in zip(done[:k], done[k:])]
        landing = [lax.empty((3,) + s.shape[1:], bf16) for s in sums]
        return tag, lp, names, _split_start(f"chips_start_{tag}", sums, landing, (3 * k,), _chips_plan)

    def finish_chips(pending, after, deps=()):
        tag, lp, names, started = pending
        k = len(names)
        done = _split_wait(f"chips_wait_{tag}", started, k, _chips_plan, after)
        last = None
        for n, s, q in zip(names, done[:k], done[k:]):
            if n == "mix_w_in":
                w_in_grads[lp] = last = _grad_sum(s, q, b_dev)
            else:
                opt_out[n] = tuple(_adamw_layer(*opt_in[n], s, q, b_dev, opt_out[n], lp, deps))
                last = opt_out[n][0]
        return last

    early = ("ffn2_w_gu", "ffn2_w_down", "mix_w_out")
    late = ("mix_w_in", "ffn1_w_gu", "ffn1_w_down")
    pending_pair = pending_chips = early_pair = early_chips = upper_started = None
    deferred = []
    names = REPLICATED + SHARDED_CONV
    assert nl > 1
    for l in reversed(range(nl)):
        x0, hb1, g1, u1, f1, x1, hbm, pa, pb, pc, h, gates, yp, sp, pre, cat, m, x2, hb2, g2, u2, f2 = saved[l]
        wgu1, wd1, win, wout, wgu2, wd2 = gathered[l][:nbig]
        gbuf ={n: lax.empty((N_DEV, 1) + grad_shapes[n], bf16) for n in BIG}
        deps = () if pending_pair is None else (pending_pair[3][-1],)
        if l == 0:
            deps += (upper_started[-1],)
        dx2, dfb, act, dg, du, dpre, dpost = _ffn_bwd(x2, dy, f2, p["ffn2_pre"], p["ffn2_post"], g2, u2, wgu2, wd2, l, deps)
        small["ffn2_pre_g"][l], small["ffn2_post_g"][l] = dpre[0], dpost[0]
        gbuf["ffn2_w_gu"] = _wgrad_cols(hb2, dg, gbuf["ffn2_w_gu"], 0, 0)
        gbuf["ffn2_w_gu"] = _wgrad_cols(hb2, du, gbuf["ffn2_w_gu"], 0, dg.shape[0])
        gbuf["ffn2_w_down"] = _wgrad_rows(act, dfb, gbuf["ffn2_w_down"], 0)
        deps = ()
        if pending_pair is not None:
            pending_chips = finish_pair(pending_pair, dx2)
            deps = (pending_chips[3][-1],)

        dm, dya, dyb, dyc, dpost = _mix_out_bwd(dx2, m, p["mix_post"], wout, l, deps)
        small["mix_post_g"][l] = dpost[0]
        gbuf["mix_w_out"] = _wgrad_kblocks(cat, [dm], gbuf["mix_w_out"], 0)
        deps = ()
        if l == 0:
            early_pair = start_pair("0a", 0, early, gbuf)
            deps = (early_pair[3][-1],)
        dpc, dws, dbias, dlg, dlb = _sgu_bwd(pc, dyc, *sgu_b, l, deps)
        small["sgu_w_s"][l] = jnp.where(causal, dws, 0.0)
        small["sgu_b_s"][l] = dbias.reshape(CHUNK, C_W // HEAD, HEAD).sum(-1).T
        small["sgu_ln_g"][l], small["sgu_ln_b"][l] = dlg[0], dlb[0]
        dpb, dcw, dcb, ddtb, dalog, ddsk, dng = _ssd_bwd(pb, yp, sp, pre, dyb, *p["ssd"], l)
        small["ssd_conv_w"][l], small["ssd_conv_b"][l], small["ssd_norm_g"][l] = dcw, dcb[0], dng[0]
        small["ssd_dt_bias"][l], small["ssd_a_log"][l], small["ssd_d"][l] = _per_head(ddtb[0]), _per_head(dalog[0]), _per_head(ddsk[0])
        deps = ()
        if l == 0:
            early_chips = finish_pair(early_pair, dpb)
            deps = (early_chips[3][-1],)
        dpa, dcw, dcb, dwr, dwi, dbr, dbi, dlam = _lru_bwd(pa, h, gates, dya, *p["lru"], l, deps)
        small["lru_conv_w"][l], small["lru_conv_b"][l], small["lru_lambda"][l] = dcw, dcb[0], dlam[0]
        small["lru_b_r"][l], small["lru_b_i"][l] = dbr[0], dbi[0]
        heads = range(A_W // HEAD)
        small["lru_w_r"][l] = jnp.stack([dwr[HEAD * i:HEAD * (i + 1), HEAD * i:HEAD * (i + 1)] for i in heads])
        small["lru_w_i"][l] = jnp.stack([dwi[HEAD * i:HEAD * (i + 1), HEAD * i:HEAD * (i + 1)] for i in heads])
        dx1, dpre = _mix_in_bwd(x1, dx2, p["mix_pre"], dpa, dpb, dpc, win, l)
        small["mix_pre_g"][l] = dpre[0]
        gbuf["mix_w_in"] = _wgrad_kblocks(hbm, [dpa, dpb, dpc], gbuf["mix_w_in"], 0)

        dy, dfb, act, dg, du, dpre, dpost = _ffn_bwd(x0, dx1, f1, p["ffn1_pre"], p["ffn1_post"], g1, u1, wgu1, wd1, l)
        small["ffn1_pre_g"][l], small["ffn1_post_g"][l] = dpre[0], dpost[0]
        gbuf["ffn1_w_gu"] = _wgrad_cols(hb1, dg, gbuf["ffn1_w_gu"], 0, 0)
        gbuf["ffn1_w_gu"] = _wgrad_cols(hb1, du, gbuf["ffn1_w_gu"], 0, dg.shape[0])
        gbuf["ffn1_w_down"] = _wgrad_rows(act, dfb, gbuf["ffn1_w_down"], 0)
        if pending_chips is not None:
            deferred.append(pending_chips)
            pending_chips = None
        pending_pair = start_pair(f"{l}", l, late if l == 0 else BIG, gbuf)
        if l == 1:
            upper = [jnp.stack(small[n][1:]) for n in names]
            upper_pack = _pack(upper)
            upper_buf = lax.dynamic_update_slice_in_dim(
                jnp.zeros((N_DEV,) + upper_pack.shape, f32), upper_pack[None], dev, axis=0)
            upper_started = _split_start("small_start", [], [upper_buf], (4,), _gather_plan)
    grad_x = dy.reshape(x.shape)

    lower = [jnp.stack(small[n][:1]) for n in names]
    lower_total = _sum_devices(_all_gather([_pack(lower)], (pending_pair[3][-1],))[0])
    late_chips = finish_pair(pending_pair, lower_total)
    order = lower_total
    for pending in deferred + [early_chips]:
        order = finish_chips(pending, order, (late_chips[3][-1],))
    upper_all = _gather_finish(list(_split_wait("small_wait", upper_started, 0, _gather_plan, order)))[0]
    upper_total = _sum_devices(upper_all)
    finish_chips(late_chips, [upper_total] + [opt_out[n][0] for n in fused] + [g for g in w_in_grads if g is not None])
    full = {n: jnp.concatenate([lo, up], axis=0) for n, lo, up in zip(
        names, _unpack(lower_total, [a.shape for a in lower]), _unpack(upper_total, [a.shape for a in upper]))}

    for n in fused:
        grads[n], delta[n], new_m[n], new_v[n] = (oriented(a, n) for a in opt_out[n])
    grads["mix_w_in"] = _narrow_w_in_grad(jnp.concatenate(w_in_grads, axis=0))
    delta["mix_w_in"], new_m["mix_w_in"], new_v["mix_w_in"] = _adamw(
        w["mix_w_in"], mom["mix_w_in"], var["mix_w_in"], grads["mix_w_in"])
    for n in REPLICATED:
        grads[n] = full[n]
    for n in SHARDED_CONV:
        cols = w[n].shape[2]
        grads[n] = lax.dynamic_slice_in_dim(full[n], dev * cols, cols, axis=2)
    shapes = [w[n].shape for n in names]
    packs = [_pack([src[n] for n in names])[None] for src in (w, mom, var, grads)]
    for dst, packed in zip((delta, new_m, new_v), _adamw(*packs)):
        dst.update(zip(names, _unpack(packed[0], shapes)))

    return (loss, grad_x, *[grads[n] for n in WEIGHTS], *[delta[n] for n in WEIGHTS],
            *[new_m[n] for n in WEIGHTS], *[new_v[n] for n in WEIGHTS])
```

```python
import functools

import jax
import jax.numpy as jnp
from jax import lax
from jax.experimental import pallas as pl
from jax.experimental.pallas import tpu as pltpu

f32, bf16 = jnp.float32, jnp.bfloat16
MESH = pl.DeviceIdType.MESH
ANY = pl.BlockSpec(memory_space=pl.ANY)

N_DEV = 8
NORM_EPS = 1e-6
LRU_C = 8.0
CHUNK = 128
HEAD = 64
A_W, B_W, C_W = 384, 384, 256
B_STATE = 128
XBC_W = B_W + 4 * B_STATE
PA_W, PB_W, PC_W = 2 * A_W, B_W + XBC_W + B_W, 2 * C_W
IN_PAD = PA_W + PB_W + PC_W
ADAM_LR, ADAM_B1, ADAM_B2, ADAM_EPS, ADAM_WD, ADAM_STEP = 0.001, 0.9, 0.999, 1e-08, 0.01, 10
VMEM_LIMIT_BYTES = 56 * 1024 * 1024
FFN_BWD_SPLIT = 2
NEG_BIG = -1e30


def _params(sem=None):
    return pltpu.CompilerParams(dimension_semantics=sem, vmem_limit_bytes=VMEM_LIMIT_BYTES)


def _nn(a, b):
    return jnp.dot(a, b, preferred_element_type=f32)


def _nt(a, b):
    return lax.dot_general(a, b, (((1,), (1,)), ((), ())), preferred_element_type=f32)


def _tn(a, b):
    return lax.dot_general(a, b, (((0,), (0,)), ((), ())), preferred_element_type=f32)


def _sigmoid(x):
    return 0.5 * jnp.tanh(0.5 * x) + 0.5


def _softplus(x):
    return jnp.maximum(x, 0.0) + jnp.log(1.0 + jnp.exp(-jnp.abs(x)))


_GELU_C0, _GELU_C1 = 0.7978845608028654, 0.044715


def _gelu(x):
    t = jnp.tanh(_GELU_C0 * (x + _GELU_C1 * x * x * x))
    return 0.5 * x * (1.0 + t)


def _gelu_grad(x):
    t = jnp.tanh(_GELU_C0 * (x + _GELU_C1 * x * x * x))
    return 0.5 * (1.0 + t) + 0.5 * x * (1.0 - t * t) * _GELU_C0 * (1.0 + 3.0 * _GELU_C1 * x * x)


def _silu_grad(x, s):
    return s * (1.0 + x * (1.0 - s))


def _rms_fwd(x, g):
    r = lax.rsqrt(jnp.mean(x * x, axis=-1, keepdims=True) + NORM_EPS)
    return x * r * g


def _rms_bwd(x, g, dy):
    r = lax.rsqrt(jnp.mean(x * x, axis=-1, keepdims=True) + NORM_EPS)
    xh = x * r
    dxh = dy * g
    dx = r * (dxh - xh * jnp.mean(dxh * xh, axis=-1, keepdims=True))
    return dx, jnp.sum(dy * xh, axis=0, keepdims=True)


def _one_minus_exp(x):
    series = -x * (1.0 + x * (0.5 + x * (1.0 / 6.0 + x * (1.0 / 24.0))))
    return jnp.where(x > -0.01, series, 1.0 - jnp.exp(x))


def _cumsum_rows(x):
    row = lax.broadcasted_iota(jnp.int32, x.shape, 0)
    d = 1
    while d < x.shape[0]:
        x = x + jnp.where(row >= d, pltpu.roll(x, d, 0), 0.0)
        d *= 2
    return x


def _tile(t, cap):
    tm = min(cap, t)
    assert t % tm == 0
    return tm


def _after(body, n_in, deps):
    def wrapped(*refs):
        return body(*refs[:n_in], *refs[n_in + len(deps):])
    return wrapped


def _lspec(a, l):
    return pl.BlockSpec((None,) + a.shape[1:], lambda *_: (l,) + (0,) * (a.ndim - 1))


def _wd_rows(wd_ref):
    return wd_ref[:, 0].reshape(2 * wd_ref.shape[2], wd_ref.shape[3])


def _ffn_fwd(x, pre_g, post_g, wgu, wd, l, deps=()):
    t, d = x.shape
    nb, _, _, h = wgu.shape
    nj = nb // 2
    tm = _tile(t, 512)

    def body(x_ref, pg_ref, qg_ref, wg_ref, wu_ref, wd_ref, y_ref, hb_ref, g_ref, u_ref, f_ref, acc_ref):
        j = pl.program_id(1)

        @pl.when(j == 0)
        def _():
            hb_ref[...] = _rms_fwd(x_ref[...], pg_ref[...]).astype(bf16)

        hb = hb_ref[...]
        g = _nn(hb, wg_ref[0, 0])
        u = _nn(hb, wu_ref[0, 0])
        g_ref[0] = g.astype(bf16)
        u_ref[0] = u.astype(bf16)
        a = (g * _sigmoid(g) * u).astype(bf16)
        part = _nn(a, _wd_rows(wd_ref))

        @pl.when(j == 0)
        def _():
            acc_ref[...] = part

        @pl.when(j > 0)
        def _():
            acc_ref[...] += part

        @pl.when(j == nj - 1)
        def _():
            f = acc_ref[...]
            f_ref[...] = f
            y_ref[...] = x_ref[...] + 0.5 * _rms_fwd(f, qg_ref[...])

    row = pl.BlockSpec((tm, d), lambda i, j: (i, 0))
    vec = pl.BlockSpec((1, d), lambda i, j: (0, 0))
    act = pl.BlockSpec((1, tm, h), lambda i, j: (j, i, 0))
    return pl.pallas_call(
        _after(body, 6, deps), name="ffn_fwd", grid=(t // tm, nj),
        in_specs=[row, _lspec(pre_g, l), _lspec(post_g, l),
                  pl.BlockSpec((1, 1, d, h), lambda i, j: (j, 0, 0, 0)),
                  pl.BlockSpec((1, 1, d, h), lambda i, j: (j + nj, 0, 0, 0)),
                  pl.BlockSpec((2, 1, h // 2, d), lambda i, j: (j, 0, 0, 0))] + [ANY] * len(deps),
        out_specs=[row, row, act, act, row],
        out_shape=[jax.ShapeDtypeStruct((t, d), f32), jax.ShapeDtypeStruct((t, d), bf16),
                   jax.ShapeDtypeStruct((nj, t, h), bf16), jax.ShapeDtypeStruct((nj, t, h), bf16),
                   jax.ShapeDtypeStruct((t, d), f32)],
        scratch_shapes=[pltpu.VMEM((tm, d), f32)],
        compiler_params=_params(("arbitrary", "arbitrary")),
    )(x, pre_g, post_g, wgu, wgu, wd, *deps)


def _ffn_bwd(x, dy, f, pre_g, post_g, g, u, wgu, wd, l, deps=()):
    t, d = x.shape
    nj, _, h = g.shape
    tm = _tile(t, 512)

    def body(x_ref, dy_ref, f_ref, pg_ref, qg_ref, g_ref, u_ref, wg_ref, wu_ref, wd_ref,
             dx_ref, dfb_ref, a_ref, dg_ref, du_ref, dpg_ref, dqg_ref, dh_ref):
        i, j = pl.program_id(0), pl.program_id(1)

        @pl.when((i == 0) & (j == 0))
        def _():
            dpg_ref[...] = jnp.zeros_like(dpg_ref)
            dqg_ref[...] = jnp.zeros_like(dqg_ref)

        @pl.when(j == 0)
        def _():
            df, dq = _rms_bwd(f_ref[...], qg_ref[...], 0.5 * dy_ref[...])
            dfb_ref[...] = df.astype(bf16)
            dqg_ref[...] += dq
            dh_ref[...] = jnp.zeros_like(dh_ref)

        wdm, wg, wu = _wd_rows(wd_ref), wg_ref[0, 0], wu_ref[0, 0]
        sub = tm // FFN_BWD_SPLIT
        das = [_nt(dfb_ref[pl.ds(half * sub, sub), :], wdm) for half in range(FFN_BWD_SPLIT)]
        for half in range(FFN_BWD_SPLIT):
            rows = pl.ds(half * sub, sub)
            da = das[half]
            gv = g_ref[0, rows, :].astype(f32)
            uv = u_ref[0, rows, :].astype(f32)
            s = _sigmoid(gv)
            sg = gv * s
            a_ref[0, rows, :] = (sg * uv).astype(bf16)
            dg = (da * uv * _silu_grad(gv, s)).astype(bf16)
            du = (da * sg).astype(bf16)
            dg_ref[0, rows, :] = dg
            du_ref[0, rows, :] = du
            dh_ref[rows, :] += _nt(dg, wg) + _nt(du, wu)

        @pl.when(j == nj - 1)
        def _():
            dxn, dp = _rms_bwd(x_ref[...], pg_ref[...], dh_ref[...])
            dx_ref[...] = dy_ref[...] + dxn
            dpg_ref[...] += dp

    row = pl.BlockSpec((tm, d), lambda i, j: (i, 0))
    vec = pl.BlockSpec((1, d), lambda i, j: (0, 0))
    act = pl.BlockSpec((1, tm, h), lambda i, j: (j, i, 0))
    act_shape = jax.ShapeDtypeStruct((nj, t, h), bf16)
    return pl.pallas_call(
        _after(body, 10, deps), name="ffn_bwd", grid=(t // tm, nj),
        in_specs=[row, row, row, _lspec(pre_g, l), _lspec(post_g, l), act, act,
                  pl.BlockSpec((1, 1, d, h), lambda i, j: (j, 0, 0, 0)),
                  pl.BlockSpec((1, 1, d, h), lambda i, j: (j + nj, 0, 0, 0)),
                  pl.BlockSpec((2, 1, h // 2, d), lambda i, j: (j, 0, 0, 0))] + [ANY] * len(deps),
        out_specs=[row, row, act, act, act, vec, vec],
        out_shape=[jax.ShapeDtypeStruct((t, d), f32), jax.ShapeDtypeStruct((t, d), bf16),
                   act_shape, act_shape, act_shape,
                   jax.ShapeDtypeStruct((1, d), f32), jax.ShapeDtypeStruct((1, d), f32)],
        scratch_shapes=[pltpu.VMEM((tm, d), f32)],
        compiler_params=_params(("arbitrary", "arbitrary")),
    )(x, dy, f, pre_g, post_g, g, u, wgu, wgu, wd, *deps)


def _wgrad_cols(x, dy, buf, l, slot0):
    (t, k), (nj, _, n) = x.shape, dy.shape

    def body(x_ref, dy_ref, buf_ref, o_ref):
        o_ref[0, 0] = _tn(dy_ref[0], x_ref[...]).astype(bf16)

    return pl.pallas_call(
        body, name="wgrad_cols", grid=(nj,),
        in_specs=[pl.BlockSpec((t, k), lambda b: (0, 0)), pl.BlockSpec((1, t, n), lambda b: (b, 0, 0)), ANY],
        out_specs=pl.BlockSpec((1, 1, n, k), lambda b: (b + slot0, l, 0, 0)),
        out_shape=jax.ShapeDtypeStruct(buf.shape, bf16), input_output_aliases={2: 0},
        compiler_params=_params(("arbitrary",)),
    )(x, dy, buf)


def _wgrad_rows(x, dy, buf, l):
    (nj, t, k), (_, n) = x.shape, dy.shape

    def body(x_ref, dy_ref, buf_ref, o_ref):
        o_ref[:, 0] = _tn(x_ref[0], dy_ref[...]).astype(bf16).reshape(2, k // 2, n)

    return pl.pallas_call(
        body, name="wgrad_rows", grid=(nj,),
        in_specs=[pl.BlockSpec((1, t, k), lambda b: (b, 0, 0)), pl.BlockSpec((t, n), lambda b: (0, 0)), ANY],
        out_specs=pl.BlockSpec((2, 1, k // 2, n), lambda b: (b, l, 0, 0)),
        out_shape=jax.ShapeDtypeStruct(buf.shape, bf16), input_output_aliases={2: 0},
        compiler_params=_params(("arbitrary",)),
    )(x, dy, buf)


def _wgrad_kblocks(x, dys, buf, l):
    t, k = x.shape
    kb = k // N_DEV
    widths = [dy.shape[1] for dy in dys]
    n = sum(widths)
    nd = len(dys)

    def body(x_ref, *refs):
        dy_hbm, o_ref, dy_vmem = refs[:nd], refs[nd + 1], refs[nd + 2:]

        @pl.when(pl.program_id(0) == 0)
        def _():
            for src, dst in zip(dy_hbm, dy_vmem):
                pltpu.sync_copy(src, dst)

        off = 0
        for dst, w in zip(dy_vmem, widths):
            o_ref[0, 0, :, off:off + w] = _tn(x_ref[...], dst[...]).astype(bf16)
            off += w

    return pl.pallas_call(
        body, name="wgrad_kblocks", grid=(N_DEV,),
        in_specs=[pl.BlockSpec((t, kb), lambda s: (0, s))] + [ANY] * (nd + 1),
        out_specs=pl.BlockSpec((1, 1, kb, n), lambda s: (s, l, 0, 0)),
        out_shape=jax.ShapeDtypeStruct(buf.shape, bf16), input_output_aliases={nd + 1: 0},
        scratch_shapes=[pltpu.VMEM((t, w), bf16) for w in widths],
        compiler_params=_params(("arbitrary",)),
    )(x, *dys, buf)


def _gathered_rows(w_ref, lo, hi):
    return w_ref[:, 0, :, lo:hi].reshape(N_DEV * w_ref.shape[2], hi - lo)


def _gathered_spec(w):
    return pl.BlockSpec((N_DEV, 1) + w.shape[2:], lambda i: (0, 0, 0, 0))


def _mix_in_fwd(x, pre_g, w_in, l):
    t, d = x.shape
    tm = _tile(t, 512)

    def body(x_ref, g_ref, w_ref, hb_ref, pa_ref, pb_ref, pc_ref):
        hb = _rms_fwd(x_ref[...], g_ref[...]).astype(bf16)
        hb_ref[...] = hb
        pa_ref[...] = _nn(hb, _gathered_rows(w_ref, 0, PA_W))
        pb_ref[...] = _nn(hb, _gathered_rows(w_ref, PA_W, PA_W + PB_W))
        pc_ref[...] = _nn(hb, _gathered_rows(w_ref, PA_W + PB_W, IN_PAD))

    def row(w):
        return pl.BlockSpec((tm, w), lambda i: (i, 0))

    return pl.pallas_call(
        body, name="mix_in_fwd", grid=(t // tm,),
        in_specs=[row(d), _lspec(pre_g, l), _gathered_spec(w_in)],
        out_specs=[row(d), row(PA_W), row(PB_W), row(PC_W)],
        out_shape=[jax.ShapeDtypeStruct((t, d), bf16), jax.ShapeDtypeStruct((t, PA_W), f32),
                   jax.ShapeDtypeStruct((t, PB_W), f32), jax.ShapeDtypeStruct((t, PC_W), f32)],
        compiler_params=_params(("arbitrary",)),
    )(x, pre_g, w_in)


def _mix_in_bwd(x, dy, pre_g, dpa, dpb, dpc, w_in, l):
    t, d = x.shape
    tm = _tile(t, 512)

    def body(x_ref, dy_ref, g_ref, dpa_ref, dpb_ref, dpc_ref, w_ref, dx_ref, dg_ref):
        @pl.when(pl.program_id(0) == 0)
        def _():
            dg_ref[...] = jnp.zeros_like(dg_ref)

        wa, wb, wc = (_gathered_rows(w_ref, 0, PA_W), _gathered_rows(w_ref, PA_W, PA_W + PB_W),
                      _gathered_rows(w_ref, PA_W + PB_W, IN_PAD))
        halves = [pl.ds(k * (tm // 2), tm // 2) for k in range(2)]
        dhs = [_nt(dpa_ref[rows, :], wa) + _nt(dpb_ref[rows, :], wb) + _nt(dpc_ref[rows, :], wc) for rows in halves]
        for rows, dh in zip(halves, dhs):
            dxn, dg = _rms_bwd(x_ref[rows, :], g_ref[...], dh)
            dx_ref[rows, :] = dy_ref[rows, :] + dxn
            dg_ref[...] += dg

    def row(w):
        return pl.BlockSpec((tm, w), lambda i: (i, 0))

    vec = pl.BlockSpec((1, d), lambda i: (0, 0))
    return pl.pallas_call(
        body, name="mix_in_bwd", grid=(t // tm,),
        in_specs=[row(d), row(d), _lspec(pre_g, l), row(PA_W), row(PB_W), row(PC_W), _gathered_spec(w_in)],
        out_specs=[row(d), vec],
        out_shape=[jax.ShapeDtypeStruct((t, d), f32), jax.ShapeDtypeStruct((1, d), f32)],
        compiler_params=_params(("arbitrary",)),
    )(x, dy, pre_g, dpa, dpb, dpc, w_in)


def _mix_out_fwd(x, ya, yb, yc, post_g, w_out, l):
    t, d = x.shape
    tm = _tile(t, 512)

    def body(x_ref, ya_ref, yb_ref, yc_ref, g_ref, w_ref, y_ref, cat_ref, m_ref):
        cat_ref[:, 0:A_W] = ya_ref[...].astype(bf16)
        cat_ref[:, A_W:A_W + B_W] = yb_ref[...].astype(bf16)
        cat_ref[:, A_W + B_W:d] = yc_ref[...].astype(bf16)
        m = _nn(cat_ref[...], _gathered_rows(w_ref, 0, d))
        m_ref[...] = m
        y_ref[...] = x_ref[...] + _rms_fwd(m, g_ref[...])

    def row(w):
        return pl.BlockSpec((tm, w), lambda i: (i, 0))

    return pl.pallas_call(
        body, name="mix_out_fwd", grid=(t // tm,),
        in_specs=[row(d), row(A_W), row(B_W), row(C_W), _lspec(post_g, l), _gathered_spec(w_out)],
        out_specs=[row(d), row(d), row(d)],
        out_shape=[jax.ShapeDtypeStruct((t, d), f32), jax.ShapeDtypeStruct((t, d), bf16), jax.ShapeDtypeStruct((t, d), f32)],
        compiler_params=_params(("arbitrary",)),
    )(x, ya, yb, yc, post_g, w_out)


def _mix_out_bwd(dy, m, post_g, w_out, l, deps=()):
    t, d = m.shape
    tm = _tile(t, 512)

    def body(dy_ref, m_ref, g_ref, w_ref, dm_ref, dya_ref, dyb_ref, dyc_ref, dg_ref):
        @pl.when(pl.program_id(0) == 0)
        def _():
            dg_ref[...] = jnp.zeros_like(dg_ref)

        dm, dg = _rms_bwd(m_ref[...], g_ref[...], dy_ref[...])
        dmb = dm.astype(bf16)
        dm_ref[...] = dmb
        dg_ref[...] += dg
        dcat = _nt(dmb, _gathered_rows(w_ref, 0, d))
        dya_ref[...] = dcat[:, 0:A_W]
        dyb_ref[...] = dcat[:, A_W:A_W + B_W]
        dyc_ref[...] = dcat[:, A_W + B_W:d]

    def row(w):
        return pl.BlockSpec((tm, w), lambda i: (i, 0))

    vec = pl.BlockSpec((1, d), lambda i: (0, 0))
    return pl.pallas_call(
        _after(body, 4, deps), name="mix_out_bwd", grid=(t // tm,),
        in_specs=[row(d), row(d), _lspec(post_g, l), _gathered_spec(w_out)] + [ANY] * len(deps),
        out_specs=[row(d), row(A_W), row(B_W), row(C_W), vec],
        out_shape=[jax.ShapeDtypeStruct((t, d), bf16), jax.ShapeDtypeStruct((t, A_W), f32),
                   jax.ShapeDtypeStruct((t, B_W), f32), jax.ShapeDtypeStruct((t, C_W), f32),
                   jax.ShapeDtypeStruct((1, d), f32)],
        compiler_params=_params(("arbitrary",)),
    )(dy, m, post_g, w_out, *deps)


def _conv_fwd(buf_ref, halo, x, w, b, n):
    buf_ref[0:8, :] = halo
    buf_ref[8:8 + n, :] = x
    out = b + w[3:4, :] * x
    for k in range(3):
        out = out + w[k:k + 1, :] * buf_ref[pl.ds(5 + k, n), :]
    return out


def _conv_bwd(buf_ref, dbuf_ref, dout, dnext, w, n):
    dbuf_ref[0:n, :] = dout
    dbuf_ref[n:n + 8, :] = dnext
    dx = w[3:4, :] * dout
    dws = []
    for k in range(3):
        dx = dx + w[k:k + 1, :] * dbuf_ref[pl.ds(3 - k, n), :]
        dws.append(jnp.sum(dout * buf_ref[pl.ds(5 + k, n), :], axis=0, keepdims=True))
    dws.append(jnp.sum(dout * buf_ref[pl.ds(8, n), :], axis=0, keepdims=True))
    return dx, jnp.concatenate(dws, axis=0), jnp.sum(dout, axis=0, keepdims=True)


def _lru_gates(rec, wr, wi, br, bi, lam):
    rb = rec.astype(bf16)
    r = _sigmoid(_nn(rb, wr) + br)
    ig = _sigmoid(_nn(rb, wi) + bi)
    sp = _softplus(-lam)
    la = -LRU_C * r * sp
    a = jnp.exp(la)
    mult = jnp.sqrt(_one_minus_exp(2.0 * la))
    return rb, r, ig, sp, a, mult


def _scan_rows(a_ref, b_ref, o_ref, carry, n, reverse):
    row = lax.broadcasted_iota(jnp.int32, (8, a_ref.shape[1]), 0)
    nb = n // 8

    def step(k, carry):
        blk = (nb - 1 - k) if reverse else k
        rows = pl.ds(pl.multiple_of(blk * 8, 8), 8)
        a, b = a_ref[rows, :], b_ref[rows, :]
        for d in (1, 2, 4):
            shift = 8 - d if reverse else d
            keep = (row < 8 - d) if reverse else (row >= d)
            b = a * jnp.where(keep, pltpu.roll(b, shift, 0), 0.0) + b
            a = a * jnp.where(keep, pltpu.roll(a, shift, 0), 1.0)
        o = a * carry + b
        o_ref[rows, :] = o
        return o[0:1, :] if reverse else o[7:8, :]

    return lax.fori_loop(0, nb, step, carry, unroll=2)


N_GATES = 5


def _lru_fwd(pa, conv_w, conv_b, wr, wi, br, bi, lam, l):
    t = pa.shape[0]
    tc = _tile(t, 512)

    def body(pa_ref, halo_ref, cw_ref, cb_ref, wr_ref, wi_ref, br_ref, bi_ref, lam_ref,
             ya_ref, h_ref, gates_ref, buf_ref, u_ref, carry_ref):
        i = pl.program_id(0)

        @pl.when(i == 0)
        def _():
            carry_ref[...] = jnp.zeros_like(carry_ref)

        halo = jnp.where(i > 0, halo_ref[:, A_W:PA_W], 0.0)
        rec = _conv_fwd(buf_ref, halo, pa_ref[:, A_W:PA_W], cw_ref[...], cb_ref[...], tc)
        _, r, ig, _, a, mult = _lru_gates(rec, wr_ref[...], wi_ref[...], br_ref[...], bi_ref[...], lam_ref[...])
        for k, val in enumerate((rec, r, ig, a, mult)):
            gates_ref[k] = val
        u_ref[...] = mult * (ig * rec)

        carry_ref[...] = _scan_rows(gates_ref.at[3], u_ref, h_ref, carry_ref[...], tc, reverse=False)
        ya_ref[...] = h_ref[...] * _gelu(pa_ref[:, 0:A_W])

    vec = pl.BlockSpec((1, A_W), lambda i: (0, 0))
    mat = pl.BlockSpec((A_W, A_W), lambda i: (0, 0))
    row = pl.BlockSpec((tc, A_W), lambda i: (i, 0))
    return pl.pallas_call(
        body, name="lru_fwd", grid=(t // tc,),
        in_specs=[pl.BlockSpec((tc, PA_W), lambda i: (i, 0)),
                  pl.BlockSpec((8, PA_W), lambda i: (jnp.maximum(i * (tc // 8) - 1, 0), 0)),
                  *[_lspec(a, l) for a in (conv_w, conv_b, wr, wi, br, bi, lam)]],
        out_specs=[row, row, pl.BlockSpec((N_GATES, tc, A_W), lambda i: (0, i, 0))],
        out_shape=[jax.ShapeDtypeStruct((t, A_W), f32), jax.ShapeDtypeStruct((t, A_W), f32),
                   jax.ShapeDtypeStruct((N_GATES, t, A_W), f32)],
        scratch_shapes=[pltpu.VMEM((8 + tc, A_W), f32), pltpu.VMEM((tc, A_W), f32), pltpu.VMEM((1, A_W), f32)],
        compiler_params=_params(("arbitrary",)),
    )(pa, pa, conv_w, conv_b, wr, wi, br, bi, lam)


def _lru_bwd(pa, h, gates, dya, conv_w, conv_b, wr, wi, br, bi, lam, l, deps=()):
    t = pa.shape[0]
    tc = _tile(t, 512)
    nc = t // tc

    def body(pa_ref, halo_ref, h_ref, hhalo_ref, gates_ref, dya_ref, cw_ref, cb_ref, wr_ref, wi_ref, br_ref, bi_ref,
             lam_ref, dpa_ref, dcw_ref, dcb_ref, dwr_ref, dwi_ref, dbr_ref, dbi_ref, dlam_ref,
             buf_ref, dbuf_ref, hbuf_ref, g_ref, dh_ref, carry_ref, dnext_ref, dhbuf_ref):
        i = pl.program_id(0)
        c = nc - 1 - i

        @pl.when(i == 0)
        def _():
            carry_ref[...] = jnp.zeros_like(carry_ref)
            dnext_ref[...] = jnp.zeros_like(dnext_ref)
            for ref in (dcw_ref, dcb_ref, dwr_ref, dwi_ref, dbr_ref, dbi_ref, dlam_ref):
                ref[...] = jnp.zeros_like(ref)

        halo = jnp.where(c > 0, halo_ref[:, A_W:PA_W], 0.0)
        cw = cw_ref[...]
        buf_ref[0:8, :] = halo
        buf_ref[8:8 + tc, :] = pa_ref[:, A_W:PA_W]
        rec, r, ig, a, mult = (gates_ref[k] for k in range(N_GATES))
        rb = rec.astype(bf16)
        lam = lam_ref[...]
        sp = _softplus(-lam)
        hbuf_ref[0:8, :] = jnp.where(c > 0, hhalo_ref[...], 0.0)
        hbuf_ref[8:8 + tc, :] = h_ref[...]
        h_prev = hbuf_ref[pl.ds(7, tc), :]
        gate = pa_ref[:, 0:A_W]
        dya = dya_ref[...]
        dpa_ref[:, 0:A_W] = (dya * h_ref[...] * _gelu_grad(gate)).astype(bf16)
        gg = dya * _gelu(gate)
        g_ref[...] = a * gg
        carry_in = carry_ref[...]
        carry_ref[...] = _scan_rows(gates_ref.at[3], g_ref, dh_ref, carry_in, tc, reverse=True)
        dhbuf_ref[0:tc, :] = dh_ref[...]
        dhbuf_ref[tc:tc + 8, :] = jnp.broadcast_to(carry_in, (8, A_W))
        dh = gg + dhbuf_ref[pl.ds(1, tc), :]
        da = dh * h_prev
        dmult = dh * ig * rec
        dig = dh * mult * rec
        drec = dh * mult * ig
        dla = da * a - dmult * (a * a) / mult
        dr = dla * (-LRU_C * sp)
        dsp = jnp.sum(dla * (-LRU_C * r), axis=0, keepdims=True)
        dlam_ref[...] += dsp * (-_sigmoid(-lam))
        dpr = (dr * r * (1.0 - r))
        dpi = (dig * ig * (1.0 - ig))
        dprb, dpib = dpr.astype(bf16), dpi.astype(bf16)
        drec = drec + _nt(dprb, wr_ref[...]) + _nt(dpib, wi_ref[...])
        dwr_ref[...] += _tn(rb, dprb)
        dwi_ref[...] += _tn(rb, dpib)
        dbr_ref[...] += jnp.sum(dpr, axis=0, keepdims=True)
        dbi_ref[...] += jnp.sum(dpi, axis=0, keepdims=True)
        dx, dw, db = _conv_bwd(buf_ref, dbuf_ref, drec, dnext_ref[...], cw, tc)
        dnext_ref[...] = drec[0:8, :]
        dcw_ref[...] += dw
        dcb_ref[...] += db
        dpa_ref[:, A_W:PA_W] = dx.astype(bf16)

    vec = pl.BlockSpec((1, A_W), lambda i: (0, 0))
    mat = pl.BlockSpec((A_W, A_W), lambda i: (0, 0))
    cwspec = pl.BlockSpec((4, A_W), lambda i: (0, 0))

    def rev(w):
        return pl.BlockSpec((tc, w), lambda i: (nc - 1 - i, 0))

    def halo(w):
        return pl.BlockSpec((8, w), lambda i: (jnp.maximum((nc - 1 - i) * (tc // 8) - 1, 0), 0))

    chunk = pltpu.VMEM((tc, A_W), f32)
    return pl.pallas_call(
        _after(body, 13, deps), name="lru_bwd", grid=(nc,),
        in_specs=[rev(PA_W), halo(PA_W), rev(A_W), halo(A_W),
                  pl.BlockSpec((N_GATES, tc, A_W), lambda i: (0, nc - 1 - i, 0)), rev(A_W),
                  *[_lspec(a, l) for a in (conv_w, conv_b, wr, wi, br, bi, lam)]] + [ANY] * len(deps),
        out_specs=[rev(PA_W), cwspec, vec, mat, mat, vec, vec, vec],
        out_shape=[jax.ShapeDtypeStruct((t, PA_W), bf16), jax.ShapeDtypeStruct((4, A_W), f32),
                   jax.ShapeDtypeStruct((1, A_W), f32), jax.ShapeDtypeStruct((A_W, A_W), f32),
                   jax.ShapeDtypeStruct((A_W, A_W), f32), jax.ShapeDtypeStruct((1, A_W), f32),
                   jax.ShapeDtypeStruct((1, A_W), f32), jax.ShapeDtypeStruct((1, A_W), f32)],
        scratch_shapes=[pltpu.VMEM((8 + tc, A_W), f32), pltpu.VMEM((tc + 8, A_W), f32), pltpu.VMEM((8 + tc, A_W), f32),
                        chunk, chunk, pltpu.VMEM((1, A_W), f32), pltpu.VMEM((8, A_W), f32),
                        pltpu.VMEM((tc + 8, A_W), f32)],
        compiler_params=_params(("arbitrary",)),
    )(pa, pa, h, h, gates, dya, conv_w, conv_b, wr, wi, br, bi, lam, *deps)


def _sgu_norm(v, g, b):
    mu = jnp.mean(v, axis=-1, keepdims=True)
    vc = v - mu
    rstd = lax.rsqrt(jnp.mean(vc * vc, axis=-1, keepdims=True) + NORM_EPS)
    vh = vc * rstd
    return vh, rstd, vh * g + b


def _sgu_mix(w_ref, vb, bias):
    grp = lax.broadcasted_iota(jnp.int32, (CHUNK, C_W), 1) // HEAD
    out = bias
    for gi in range(C_W // HEAD):
        out = out + jnp.where(grp == gi, _nn(w_ref[gi], vb), 0.0)
    return out


def _sgu_fwd(pc, ln_g, ln_b, wm, bias, l):
    t = pc.shape[0]
    tm = _tile(t, 512)

    def body(pc_ref, g_ref, b_ref, w_ref, bias_ref, yc_ref):
        for ci in range(tm // CHUNK):
            rows = pl.ds(ci * CHUNK, CHUNK)
            ge = _gelu(pc_ref[rows, :])
            _, _, vn = _sgu_norm(ge[:, C_W:PC_W], g_ref[...], b_ref[...])
            yc_ref[rows, :] = ge[:, 0:C_W] * _sgu_mix(w_ref, vn.astype(bf16), bias_ref[...])

    vec = pl.BlockSpec((1, C_W), lambda i: (0, 0))
    return pl.pallas_call(
        body, name="sgu_fwd", grid=(t // tm,),
        in_specs=[pl.BlockSpec((tm, PC_W), lambda i: (i, 0)), *[_lspec(a, l) for a in (ln_g, ln_b, wm, bias)]],
        out_specs=pl.BlockSpec((tm, C_W), lambda i: (i, 0)),
        out_shape=jax.ShapeDtypeStruct((t, C_W), f32),
        compiler_params=_params(("arbitrary",)),
    )(pc, ln_g, ln_b, wm, bias)


def _sgu_bwd(pc, dyc, ln_g, ln_b, wm, wmt, bias, l, deps=()):
    t = pc.shape[0]
    tm = _tile(t, 512)

    def body(pc_ref, dyc_ref, g_ref, b_ref, w_ref, wt_ref, bias_ref, dpc_ref, dw_ref, dbias_ref, dg_ref, db_ref):
        @pl.when(pl.program_id(0) == 0)
        def _():
            for ref in (dw_ref, dbias_ref, dg_ref, db_ref):
                ref[...] = jnp.zeros_like(ref)

        grp = lax.broadcasted_iota(jnp.int32, (CHUNK, C_W), 1) // HEAD
        for ci in range(tm // CHUNK):
            rows = pl.ds(ci * CHUNK, CHUNK)
            x = pc_ref[rows, :]
            ge = _gelu(x)
            gv = g_ref[...]
            vh, rstd, vn = _sgu_norm(ge[:, C_W:PC_W], gv, b_ref[...])
            vb = vn.astype(bf16)
            mixed = _sgu_mix(w_ref, vb, bias_ref[...])
            dyc = dyc_ref[rows, :]
            du = dyc * mixed
            dmix = dyc * ge[:, 0:C_W]
            dmb = dmix.astype(bf16)
            dvn = jnp.zeros((CHUNK, C_W), f32)
            for gi in range(C_W // HEAD):
                dvn = dvn + jnp.where(grp == gi, _nn(wt_ref[gi], dmb), 0.0)
                dw_ref[gi] += _nt(jnp.where(grp == gi, dmix, 0.0).astype(bf16), vb)
            dbias_ref[...] += dmix
            dg_ref[...] += jnp.sum(dvn * vh, axis=0, keepdims=True)
            db_ref[...] += jnp.sum(dvn, axis=0, keepdims=True)
            dvh = dvn * gv
            dv = rstd * (dvh - jnp.mean(dvh, axis=-1, keepdims=True) - vh * jnp.mean(dvh * vh, axis=-1, keepdims=True))
            gg = _gelu_grad(x)
            dpc_ref[rows, 0:C_W] = (du * gg[:, 0:C_W]).astype(bf16)
            dpc_ref[rows, C_W:PC_W] = (dv * gg[:, C_W:PC_W]).astype(bf16)

    vec = pl.BlockSpec((1, C_W), lambda i: (0, 0))
    wspec = pl.BlockSpec((4, CHUNK, CHUNK), lambda i: (0, 0, 0))
    bspec = pl.BlockSpec((CHUNK, C_W), lambda i: (0, 0))
    return pl.pallas_call(
        _after(body, 7, deps), name="sgu_bwd", grid=(t // tm,),
        in_specs=[pl.BlockSpec((tm, PC_W), lambda i: (i, 0)), pl.BlockSpec((tm, C_W), lambda i: (i, 0)),
                  *[_lspec(a, l) for a in (ln_g, ln_b, wm, wmt, bias)]] + [ANY] * len(deps),
        out_specs=[pl.BlockSpec((tm, PC_W), lambda i: (i, 0)), wspec, bspec, vec, vec],
        out_shape=[jax.ShapeDtypeStruct((t, PC_W), bf16), jax.ShapeDtypeStruct((4, CHUNK, CHUNK), f32),
                   jax.ShapeDtypeStruct((CHUNK, C_W), f32), jax.ShapeDtypeStruct((1, C_W), f32),
                   jax.ShapeDtypeStruct((1, C_W), f32)],
        compiler_params=_params(("arbitrary",)),
    )(pc, dyc, ln_g, ln_b, wm, wmt, bias, *deps)


N_PAIR = B_W // 128
HEADS_PER_GROUP = 3


def _pair_groups(p):
    return (2 * p) // HEADS_PER_GROUP, (2 * p + 1) // HEADS_PER_GROUP


def _ssd_chunk(pb_ref, halo, buf_ref, cw, cb, dtb, alog):
    z = pb_ref[:, 0:B_W]
    pre = _conv_fwd(buf_ref, halo, pb_ref[:, B_W:B_W + XBC_W], cw, cb, CHUNK)
    sg = _sigmoid(pre)
    xbc = pre * sg
    xs = xbc[:, 0:B_W]
    bm = [xbc[:, B_W + k * B_STATE:B_W + (k + 1) * B_STATE] for k in range(2)]
    cm = [xbc[:, B_W + (2 + k) * B_STATE:B_W + (3 + k) * B_STATE] for k in range(2)]
    dtin = pb_ref[:, B_W + XBC_W:PB_W] + dtb
    dt = _softplus(dtin)
    a = -jnp.exp(alog)
    cs = _cumsum_rows(dt * a)
    return dict(z=z, pre=pre, sg=sg, xs=xs, bm=bm, cm=cm, dtin=dtin, dt=dt, a=a, cs=cs,
                ecs=jnp.exp(cs), ds=jnp.exp(cs[CHUNK - 1:CHUNK, :] - cs), xdt=xs * dt,
                bmb=[v.astype(bf16) for v in bm], cmb=[v.astype(bf16) for v in cm])


def _ssd_decay(cs_pair, half):
    cst = cs_pair.T
    lane0 = HEAD * half
    csc = jnp.broadcast_to(cs_pair[:, lane0:lane0 + 1], (CHUNK, CHUNK))
    csr = cst[lane0:lane0 + 1, :]
    tri = lax.broadcasted_iota(jnp.int32, (CHUNK, CHUNK), 0) >= lax.broadcasted_iota(jnp.int32, (CHUNK, CHUNK), 1)
    return jnp.exp(jnp.where(tri, csc - csr, NEG_BIG)), cst


def _ssd_fwd(pb, conv_w, conv_b, dtb, alog, dskip, norm_g, l):
    t = pb.shape[0]
    nc = t // CHUNK

    def body(pb_ref, halo_ref, cw_ref, cb_ref, dtb_ref, alog_ref, d_ref, ng_ref, yb_ref, yp_ref, sp_ref, buf_ref, s_ref):
        i = pl.program_id(0)

        @pl.when(i == 0)
        def _():
            s_ref[...] = jnp.zeros_like(s_ref)

        halo = jnp.where(i > 0, halo_ref[:, B_W:B_W + XBC_W], 0.0)
        q = _ssd_chunk(pb_ref, halo, buf_ref, cw_ref[...], cb_ref[...], dtb_ref[...], alog_ref[...])
        sp_ref[0] = s_ref[...]
        lane = lax.broadcasted_iota(jnp.int32, (CHUNK, 128), 1)
        rowi = lax.broadcasted_iota(jnp.int32, (128, B_STATE), 0)
        cb_mat = [_nt(q["cmb"][k], q["bmb"][k]) for k in range(2)]
        xd = q["xdt"] * q["ds"]
        for p in range(N_PAIR):
            cols = slice(128 * p, 128 * (p + 1))
            g_lo, g_hi = _pair_groups(p)
            cs_p, xdt_p = q["cs"][:, cols], q["xdt"][:, cols]
            s_p = s_ref[cols, :]
            s_pb = s_p.astype(bf16)
            y_p = jnp.zeros((CHUNK, 128), f32)
            for half, grp in ((0, g_lo), (1, g_hi)):
                lm, cst = _ssd_decay(cs_p, half)
                mb = (cb_mat[grp] * lm).astype(bf16)
                sel = (lane < HEAD) if half == 0 else (lane >= HEAD)
                y_p = y_p + _nn(mb, jnp.where(sel, xdt_p, 0.0).astype(bf16))
            off_lo = _nt(q["cmb"][g_lo], s_pb)
            off = off_lo if g_lo == g_hi else jnp.where(lane < HEAD, off_lo, _nt(q["cmb"][g_hi], s_pb))
            y_p = y_p + off * q["ecs"][:, cols] + q["xs"][:, cols] * d_ref[:, cols]
            yp_ref[:, cols] = y_p
            xd_pb = xd[:, cols].astype(bf16)
            upd_lo = _tn(xd_pb, q["bmb"][g_lo])
            upd = upd_lo if g_lo == g_hi else jnp.where(rowi < HEAD, upd_lo, _tn(xd_pb, q["bmb"][g_hi]))
            cd = jnp.exp(jnp.broadcast_to(cst[:, CHUNK - 1:CHUNK], (128, B_STATE)))
            s_ref[cols, :] = cd * s_p + upd
        z = q["z"]
        yg = yp_ref[...] * (z * _sigmoid(z))
        yb_ref[...] = _rms_fwd(yg, ng_ref[...])

    vec = pl.BlockSpec((1, B_W), lambda i: (0, 0))
    row = pl.BlockSpec((CHUNK, B_W), lambda i: (i, 0))
    return pl.pallas_call(
        body, name="ssd_fwd", grid=(nc,),
        in_specs=[pl.BlockSpec((CHUNK, PB_W), lambda i: (i, 0)),
                  pl.BlockSpec((8, PB_W), lambda i: (jnp.maximum(i * (CHUNK // 8) - 1, 0), 0)),
                  *[_lspec(a, l) for a in (conv_w, conv_b, dtb, alog, dskip, norm_g)]],
        out_specs=[row, row, pl.BlockSpec((1, B_W, B_STATE), lambda i: (i, 0, 0))],
        out_shape=[jax.ShapeDtypeStruct((t, B_W), f32), jax.ShapeDtypeStruct((t, B_W), f32),
                   jax.ShapeDtypeStruct((nc, B_W, B_STATE), f32)],
        scratch_shapes=[pltpu.VMEM((8 + CHUNK, XBC_W), f32), pltpu.VMEM((B_W, B_STATE), f32)],
        compiler_params=_params(("arbitrary",)),
    )(pb, pb, conv_w, conv_b, dtb, alog, dskip, norm_g)


def _ssd_bwd(pb, yp, sprev, dyb, conv_w, conv_b, dtb, alog, dskip, norm_g, l):
    t = pb.shape[0]
    nc = t // CHUNK

    def body(pb_ref, halo_ref, yp_ref, sp_ref, dyb_ref, cw_ref, cb_ref, dtb_ref, alog_ref, d_ref, ng_ref,
             dpb_ref, dcw_ref, dcb_ref, ddtb_ref, dalog_ref, dd_ref, dng_ref,
             buf_ref, dbuf_ref, ds_ref, dnext_ref, dxbc_ref, dcs_ref, dxdt_ref):
        i = pl.program_id(0)
        c = nc - 1 - i

        @pl.when(i == 0)
        def _():
            ds_ref[...] = jnp.zeros_like(ds_ref)
            dnext_ref[...] = jnp.zeros_like(dnext_ref)
            for ref in (dcw_ref, dcb_ref, ddtb_ref, dalog_ref, dd_ref, dng_ref):
                ref[...] = jnp.zeros_like(ref)

        halo = jnp.where(c > 0, halo_ref[:, B_W:B_W + XBC_W], 0.0)
        cw = cw_ref[...]
        q = _ssd_chunk(pb_ref, halo, buf_ref, cw, cb_ref[...], dtb_ref[...], alog_ref[...])
        z, xs, dt, a, ecs, dsd, xdt =q["z"], q["xs"], q["dt"], q["a"], q["ecs"], q["ds"], q["xdt"]
        sz = _sigmoid(z)
        siluz = z * sz
        yp = yp_ref[...]
        dyg, dng = _rms_bwd(yp * siluz, ng_ref[...], dyb_ref[...])
        dng_ref[...] += dng
        dy = dyg * siluz
        dpb_ref[:, 0:B_W] = (dyg * yp * _silu_grad(z, sz)).astype(bf16)
        dd_ref[...] += jnp.sum(dy * xs, axis=0, keepdims=True)
        g1 = dy * ecs
        lane = lax.broadcasted_iota(jnp.int32, (CHUNK, 128), 1)
        rowi = lax.broadcasted_iota(jnp.int32, (128, B_STATE), 0)
        rowc = lax.broadcasted_iota(jnp.int32, (CHUNK, 128), 0)
        cb_mat = [_nt(q["cmb"][k], q["bmb"][k]) for k in range(2)]
        d_cb = [jnp.zeros((CHUNK, CHUNK), f32) for _ in range(2)]
        d_b = [jnp.zeros((CHUNK, B_STATE), f32) for _ in range(2)]
        d_c = [jnp.zeros((CHUNK, B_STATE), f32) for _ in range(2)]
        for p in range(N_PAIR):
            cols = slice(128 * p, 128 * (p + 1))
            g_lo, g_hi = _pair_groups(p)
            lo, hi = lane < HEAD, lane >= HEAD
            cs_p, xdt_p, dy_p, ds_p, g1_p = q["cs"][:, cols], xdt[:, cols], dy[:, cols], dsd[:, cols], g1[:, cols]
            s_p = sp_ref[0, cols, :]
            s_pb = s_p.astype(bf16)
            dsn = ds_ref[cols, :]
            dsnb = dsn.astype(bf16)
            g1b = g1_p.astype(bf16)
            off_lo = _nt(q["cmb"][g_lo], s_pb)
            off = off_lo if g_lo == g_hi else jnp.where(lo, off_lo, _nt(q["cmb"][g_hi], s_pb))
            dcs_p = dy_p * off * ecs[:, cols]
            dsp_lo = _tn(g1b, q["cmb"][g_lo])
            dsp = dsp_lo if g_lo == g_hi else jnp.where(rowi < HEAD, dsp_lo, _tn(g1b, q["cmb"][g_hi]))
            dx_lo = _nt(q["bmb"][g_lo], dsnb)
            dxd = dx_lo if g_lo == g_hi else jnp.where(lo, dx_lo, _nt(q["bmb"][g_hi], dsnb))
            xd_p = xdt_p * ds_p
            if g_lo == g_hi:
                d_c[g_lo] = d_c[g_lo] + _nn(g1b, s_pb)
                d_b[g_lo] = d_b[g_lo] + _nn(xd_p.astype(bf16), dsnb)
            else:
                d_c[g_lo] = d_c[g_lo] + _nn(jnp.where(lo, g1_p, 0.0).astype(bf16), s_pb)
                d_c[g_hi] = d_c[g_hi] + _nn(jnp.where(hi, g1_p, 0.0).astype(bf16), s_pb)
                d_b[g_lo] = d_b[g_lo] + _nn(jnp.where(lo, xd_p, 0.0).astype(bf16), dsnb)
                d_b[g_hi] = d_b[g_hi] + _nn(jnp.where(hi, xd_p, 0.0).astype(bf16), dsnb)
            dxdt_p = dxd * ds_p
            t2 = dxd * xdt_p * ds_p
            dcs_p = dcs_p - t2
            dlast = jnp.sum(t2, axis=0, keepdims=True)
            cst = None
            for half, grp in ((0, g_lo), (1, g_hi)):
                sel = lo if half == 0 else hi
                lm, cst = _ssd_decay(cs_p, half)
                m = cb_mat[grp] * lm
                dyh = jnp.where(sel, dy_p, 0.0).astype(bf16)
                xdh = jnp.where(sel, xdt_p, 0.0).astype(bf16)
                dm = _nt(dyh, xdh)
                pm = dm * m
                col = jnp.sum(pm, axis=1, keepdims=True) - jnp.sum(pm.T, axis=1, keepdims=True)
                dcs_p = dcs_p + jnp.where(lane == HEAD * half, col, 0.0)
                d_cb[grp] = d_cb[grp] + dm * lm
                dxdt_p = dxdt_p + _tn(m.astype(bf16), dyh)
            cdcol = jnp.exp(jnp.broadcast_to(cst[:, CHUNK - 1:CHUNK], (128, B_STATE)))
            ds_ref[cols, :] = cdcol * dsn + dsp
            dcd_row = jnp.sum((dsn * s_p).T, axis=0, keepdims=True)
            dlast = dlast + dcd_row * ecs[CHUNK - 1:CHUNK, cols]
            dcs_ref[:, cols] = dcs_p + jnp.where(rowc == CHUNK - 1, dlast, 0.0)
            dxdt_ref[:, cols] = dxdt_p
        for k in range(2):
            dcbb = d_cb[k].astype(bf16)
            d_c[k] = d_c[k] + _nn(dcbb, q["bmb"][k])
            d_b[k] = d_b[k] + _tn(dcbb, q["cmb"][k])
            dxbc_ref[:, B_W + k * B_STATE:B_W + (k + 1) * B_STATE] = d_b[k]
            dxbc_ref[:, B_W + (2 + k) * B_STATE:B_W + (3 + k) * B_STATE] = d_c[k]
        dxdt = dxdt_ref[...]
        dxbc_ref[:, 0:B_W] = dy * d_ref[...] + dxdt * dt
        dcs = dcs_ref[...]
        dad = jnp.sum(dcs, axis=0, keepdims=True) - _cumsum_rows(dcs) + dcs
        ddt = dxdt * xs + dad * a
        dalog_ref[...] += jnp.sum(dad * dt, axis=0, keepdims=True) * a
        dtraw = ddt * _sigmoid(q["dtin"])
        ddtb_ref[...] += jnp.sum(dtraw, axis=0, keepdims=True)
        dpb_ref[:, B_W + XBC_W:PB_W] = dtraw.astype(bf16)
        dpre = dxbc_ref[...] * _silu_grad(q["pre"], q["sg"])
        dx, dw, db = _conv_bwd(buf_ref, dbuf_ref, dpre, dnext_ref[...], cw, CHUNK)
        dnext_ref[...] = dpre[0:8, :]
        dcw_ref[...] += dw
        dcb_ref[...] += db
        dpb_ref[:, B_W:B_W + XBC_W] = dx.astype(bf16)

    vec = pl.BlockSpec((1, B_W), lambda i: (0, 0))
    cwspec = pl.BlockSpec((4, XBC_W), lambda i: (0, 0))
    cbspec = pl.BlockSpec((1, XBC_W), lambda i: (0, 0))

    def rev(w):
        return pl.BlockSpec((CHUNK, w), lambda i: (nc - 1 - i, 0))

    vshape = jax.ShapeDtypeStruct((1, B_W), f32)
    return pl.pallas_call(
        body, name="ssd_bwd", grid=(nc,),
        in_specs=[rev(PB_W), pl.BlockSpec((8, PB_W), lambda i: (jnp.maximum((nc - 1 - i) * (CHUNK // 8) - 1, 0), 0)),
                  rev(B_W), pl.BlockSpec((1, B_W, B_STATE), lambda i: (nc - 1 - i, 0, 0)), rev(B_W),
                  *[_lspec(a, l) for a in (conv_w, conv_b, dtb, alog, dskip, norm_g)]],
        out_specs=[rev(PB_W), cwspec, cbspec, vec, vec, vec, vec],
        out_shape=[jax.ShapeDtypeStruct((t, PB_W), bf16), jax.ShapeDtypeStruct((4, XBC_W), f32),
                   jax.ShapeDtypeStruct((1, XBC_W), f32), vshape, vshape, vshape, vshape],
        scratch_shapes=[pltpu.VMEM((8 + CHUNK, XBC_W), f32), pltpu.VMEM((CHUNK + 8, XBC_W), f32),
                        pltpu.VMEM((B_W, B_STATE), f32), pltpu.VMEM((8, XBC_W), f32),
                        pltpu.VMEM((CHUNK, XBC_W), f32), pltpu.VMEM((CHUNK, B_W), f32), pltpu.VMEM((CHUNK, B_W), f32)],
        compiler_params=_params(("arbitrary",)),
    )(pb, pb, yp, sprev, dyb, conv_w, conv_b, dtb, alog, dskip, norm_g)


def _loss_fwd(y, target):
    t, d = y.shape
    tm = _tile(t, 512)

    def body(y_ref, t_ref, dy_ref, loss_ref):
        @pl.when(pl.program_id(0) == 0)
        def _():
            loss_ref[...] = jnp.zeros_like(loss_ref)

        e = y_ref[...] - t_ref[...]
        dy_ref[...] = e * (1.0 / d)
        per_tok = jnp.mean(e * e, axis=-1, keepdims=True)
        loss_ref[...] += 0.5 * jnp.sum(per_tok, axis=0, keepdims=True)

    row = pl.BlockSpec((tm, d), lambda i: (i, 0))
    return pl.pallas_call(
        body, name="loss_fwd", grid=(t // tm,), in_specs=[row, row],
        out_specs=[row, pl.BlockSpec((1, 128), lambda i: (0, 0))],
        out_shape=[jax.ShapeDtypeStruct((t, d), f32), jax.ShapeDtypeStruct((1, 128), f32)],
        compiler_params=_params(("arbitrary",)),
    )(y, target)


def _row_tile(r):
    return 512 if r % 512 == 0 else r


def _pair_add(g, r, c_dev):
    _, nl, rows, cols = g.shape
    tr = _row_tile(rows)

    def body(c_ref, g_ref, r_ref, o_ref):
        o_ref[...] = (g_ref[...].astype(f32) + r_ref[...].astype(f32)).astype(bf16)

    blk = (None, None, tr, cols)
    return pl.pallas_call(
        body, name="pair_add",
        grid_spec=pltpu.PrefetchScalarGridSpec(
            num_scalar_prefetch=1, grid=(4, nl, rows // tr),
            in_specs=[pl.BlockSpec(blk, lambda b, l, i, c: (2 * b + c[0], l, i, 0)),
                      pl.BlockSpec(blk, lambda b, l, i, c: (b, l, i, 0))],
            out_specs=pl.BlockSpec(blk, lambda b, l, i, c: (b, l, i, 0))),
        out_shape=jax.ShapeDtypeStruct(r.shape, bf16),
        compiler_params=_params(("arbitrary", "arbitrary", "arbitrary")),
    )(c_dev, g, r)


def _grad_sum(s, q, b_dev):
    _, nl, rows, cols = s.shape
    tr = _row_tile(rows)

    def body(b_ref, s_ref, q0_ref, q1_ref, q2_ref, o_ref):
        o_ref[...] = ((s_ref[...].astype(f32) + q0_ref[...].astype(f32)) + q1_ref[...].astype(f32)) + q2_ref[...].astype(f32)

    blk = (None, None, tr, cols)

    def qspec(k):
        return pl.BlockSpec(blk, lambda l, i, b: (k, l, i, 0))

    return pl.pallas_call(
        body, name="grad_sum",
        grid_spec=pltpu.PrefetchScalarGridSpec(
            num_scalar_prefetch=1, grid=(nl, rows // tr),
            in_specs=[pl.BlockSpec(blk, lambda l, i, b: (b[0], l, i, 0)), qspec(0), qspec(1), qspec(2)],
            out_specs=pl.BlockSpec((None, tr, cols), lambda l, i, b: (l, i, 0))),
        out_shape=jax.ShapeDtypeStruct(s.shape[1:], f32),
        compiler_params=_params(("arbitrary", "arbitrary")),
    )(b_dev, s, q, q, q)


def _sum_devices(parts):
    n, rows, cols = parts.shape
    tr = _row_tile(rows)

    def body(p_ref, o_ref):
        acc = p_ref[0]
        for k in range(1, n):
            acc = acc + p_ref[k]
        o_ref[...] = acc

    return pl.pallas_call(
        body, name="sum_devices", grid=(rows // tr,),
        in_specs=[pl.BlockSpec((n, tr, cols), lambda i: (0, i, 0))],
        out_specs=pl.BlockSpec((tr, cols), lambda i: (i, 0)),
        out_shape=jax.ShapeDtypeStruct((rows, cols), f32),
        compiler_params=_params(("arbitrary",)),
    )(parts)


def _adamw(w, m, v, g):
    nl, rows, cols = w.shape
    tr = _row_tile(rows)

    def body(w_ref, m_ref, v_ref, g_ref, d_ref, nm_ref, nv_ref):
        d_ref[...], nm_ref[...], nv_ref[...] = _adamw_math(w_ref[...], m_ref[...], v_ref[...], g_ref[...])

    blk = pl.BlockSpec((None, tr, cols), lambda l, i: (l, i, 0))
    shape = jax.ShapeDtypeStruct(w.shape, f32)
    return pl.pallas_call(
        body, name="adamw", grid=(nl, rows // tr), in_specs=[blk] * 4, out_specs=[blk] * 3,
        out_shape=[shape] * 3, compiler_params=_params(("arbitrary", "arbitrary")),
    )(w, m, v, g)


def _adamw_math(w, m, v, g):
    nm = ADAM_B1 * m + (1.0 - ADAM_B1) * g
    nv = ADAM_B2 * v + (1.0 - ADAM_B2) * (g * g)
    m_hat = nm / (1.0 - ADAM_B1 ** ADAM_STEP)
    v_hat = nv / (1.0 - ADAM_B2 ** ADAM_STEP)
    return -ADAM_LR * (m_hat / (jnp.sqrt(v_hat) + ADAM_EPS) + ADAM_WD * w), nm, nv


def _adamw_layer(w, m, v, s, q, b_dev, outs, l, deps=()):
    _, rows, cols = w.shape
    tr = _row_tile(rows)

    def body(b_ref, w_ref, m_ref, v_ref, s_ref, q0_ref, q1_ref, q2_ref, o0, o1, o2, o3, g_ref, d_ref, nm_ref, nv_ref):
        g = ((s_ref[...].astype(f32) + q0_ref[...].astype(f32)) + q1_ref[...].astype(f32)) + q2_ref[...].astype(f32)
        g_ref[...] = g
        d_ref[...], nm_ref[...], nv_ref[...] = _adamw_math(w_ref[...], m_ref[...], v_ref[...], g)

    wspec = pl.BlockSpec((None, tr, cols), lambda i, b: (l, i, 0))
    blk = (None, None, tr, cols)

    def qspec(k):
        return pl.BlockSpec(blk, lambda i, b: (k, 0, i, 0))

    shape = jax.ShapeDtypeStruct(w.shape, f32)
    return pl.pallas_call(
        _after(body, 12, deps), name="adamw_layer",
        grid_spec=pltpu.PrefetchScalarGridSpec(
            num_scalar_prefetch=1, grid=(rows // tr,),
            in_specs=[wspec] * 3 + [pl.BlockSpec(blk, lambda i, b: (b[0], 0, i, 0)), qspec(0), qspec(1), qspec(2)]
            + [ANY] * (4 + len(deps)),
            out_specs=[wspec] * 4),
        out_shape=[shape] * 4, input_output_aliases={8 + k: k for k in range(4)},
        compiler_params=_params(("arbitrary",)),
    )(b_dev, w, m, v, s, q, q, q, *outs, *deps)


def _place():
    return lax.axis_index("x"), lax.axis_index("y"), lax.axis_index("c")


def _all_gather(shards, deps=()):
    n = len(shards)
    nd = len(deps)

    def body(*refs):
        src, dst = refs[:n], refs[n:2 * n]
        send_sems, recv_sems, local_sems = refs[2 * n:]
        x, y, c = _place()
        me, sibling = (x, y, c), (x, y, 1 - c)
        chips = [(1 - x, y), (x, 1 - y), (1 - x, 1 - y)]

        def copy(a, k, block, to, from_shard=False):
            px, py, pc = block
            rows = dst[a].at[4 * px + 2 * py + pc]
            return pltpu.make_async_remote_copy(
                src_ref=src[a] if from_shard else rows, dst_ref=rows,
                send_sem=send_sems.at[a, k], recv_sem=recv_sems.at[a, k], device_id=to, device_id_type=MESH)

        mine = [pltpu.make_async_copy(src[a], dst[a].at[4 * x + 2 * y + c], local_sems.at[a]) for a in range(n)]
        for cp in mine:
            cp.start()
        first = []
        for a in range(n):
            first.append(copy(a, 0, me, sibling, True))
            first += [copy(a, 1 + j, me, (*chip, c), True) for j, chip in enumerate(chips)]
        for cp in first:
            cp.start()
        passed = []
        for j, chip in enumerate(chips):
            for a in range(n):
                copy(a, 1 + j, (*chip, c), me).wait_recv()
                fwd = copy(a, 4 + j, (*chip, c), sibling)
                fwd.start()
                passed.append(fwd)
        for a in range(n):
            copy(a, 0, sibling, me).wait_recv()
            for j, chip in enumerate(chips):
                copy(a, 4 + j, (*chip, 1 - c), me).wait_recv()
        for cp in first + passed:
            cp.wait_send()
        for cp in mine:
            cp.wait()

    return pl.pallas_call(
        _after(body, n, deps), name="all_gather", in_specs=[ANY] * (n + nd), out_specs=[ANY] * n,
        out_shape=[jax.ShapeDtypeStruct((N_DEV,) + s.shape, s.dtype) for s in shards],
        scratch_shapes=[pltpu.SemaphoreType.DMA((n, 7)), pltpu.SemaphoreType.DMA((n, 7)), pltpu.SemaphoreType.DMA((n,))],
    )(*shards, *deps)


HBM = pl.BlockSpec(memory_space=pltpu.HBM)
SEM = pl.BlockSpec(memory_space=pltpu.SEMAPHORE)
_EFFECT = pltpu.SideEffectType.DATAFLOW_SIDE_EFFECTING


def _split_start(name, srcs, dsts, sem_shape, plan):
    ns, nb = len(srcs), len(srcs) + len(dsts)

    def body(*refs):
        send_sems, recv_sems = refs[nb], refs[nb + 1]
        for cp in plan(refs[:ns], refs[ns:nb], send_sems, recv_sems):
            cp.start()
        refs[-1][...] = jnp.zeros_like(refs[-1])

    bufs = list(srcs) + list(dsts)
    return pl.pallas_call(
        body, name=name,
        out_shape=(pltpu.SemaphoreType.DMA(sem_shape), pltpu.SemaphoreType.DMA(sem_shape),
                   *[pltpu.HBM(a.shape, a.dtype) for a in bufs], jax.ShapeDtypeStruct((8, 128), f32)),
        in_specs=[HBM] * nb, out_specs=(SEM, SEM, *[HBM] * nb, pl.BlockSpec(memory_space=pltpu.VMEM)),
        input_output_aliases={i: 2 + i for i in range(nb)},
        compiler_params=pltpu.CompilerParams(has_side_effects=_EFFECT),
    )(*[pltpu.with_memory_space_constraint(a, pltpu.HBM) for a in bufs])


def _split_wait(name, started, ns, plan, after):
    send_sems, recv_sems = started[0], started[1]
    bufs = list(started[2:-1])
    nb = len(bufs)
    after = list(after) if isinstance(after, (list, tuple)) else [after]

    def body(*refs):
        for cp in plan(refs[:ns], refs[ns:nb], refs[nb], refs[nb + 1]):
            cp.wait_send()
            cp.wait_recv()

    return pl.pallas_call(
        body, name=name, out_shape=tuple(pltpu.HBM(a.shape, a.dtype) for a in bufs),
        in_specs=[HBM] * nb + [SEM, SEM] + [ANY] * len(after), out_specs=tuple([HBM] * nb),
        input_output_aliases={i: i for i in range(nb)},
        compiler_params=pltpu.CompilerParams(has_side_effects=_EFFECT),
    )(*bufs, send_sems, recv_sems, *after)


def _remote(src, dst, send_sem, recv_sem, to):
    return pltpu.make_async_remote_copy(src_ref=src, dst_ref=dst, send_sem=send_sem, recv_sem=recv_sem,
                                        device_id=to, device_id_type=MESH)


def _gather_plan(src, dst, send_sems, recv_sems):
    x, y, c = _place()
    peers = [(x, y, 1 - c), (1 - x, y, c), (x, 1 - y, c), (1 - x, 1 - y, c)]
    copies = []
    for a in range(len(dst)):
        rows = dst[a].at[4 * x + 2 * y + c]
        copies += [_remote(rows, rows, send_sems.at[4 * a + k], recv_sems.at[4 * a + k], peer) for k, peer in enumerate(peers)]
    return copies


def _pair_plan(src, dst, send_sems, recv_sems):
    x, y, c = _place()
    return [_remote(src[a].at[2 * b + (1 - c)], dst[a].at[b], send_sems.at[4 * a + b], recv_sems.at[4 * a + b], (x, y, 1 - c))
            for a in range(len(src)) for b in range(4)]


def _chips_plan(src, dst, send_sems, recv_sems):
    x, y, c = _place()
    chips = [(1 - x, y), (x, 1 - y), (1 - x, 1 - y)]
    return [_remote(src[a].at[2 * px + py], dst[a].at[j], send_sems.at[3 * a + j], recv_sems.at[3 * a + j], (px, py, c))
            for a in range(len(src)) for j, (px, py) in enumerate(chips)]


def _forward_plan(src, dst, send_sems, recv_sems):
    x, y, c = _place()
    copies = []
    for a in range(len(dst)):
        for j, (px, py) in enumerate([(1 - x, y), (x, 1 - y), (1 - x, 1 - y)]):
            rows = dst[a].at[4 * px + 2 * py + c]
            copies.append(_remote(rows, rows, send_sems.at[3 * a + j], recv_sems.at[3 * a + j], (x, y, 1 - c)))
    return copies


def _gather_finish(bufs):
    n = len(bufs)

    def body(*refs):
        dst = refs[n:2 * n]
        send_sems, recv_sems = refs[2 * n:]
        x, y, c = _place()
        chips = [(1 - x, y), (x, 1 - y), (1 - x, 1 - y)]
        passed = []
        for a in range(n):
            for j, (px, py) in enumerate(chips):
                rows = dst[a].at[4 * px + 2 * py + c]
                passed.append(_remote(rows, rows, send_sems.at[a, j], recv_sems.at[a, j], (x, y, 1 - c)))
        for cp in passed:
            cp.start()
        for cp in passed:
            cp.wait_send()
        for a in range(n):
            for j, (px, py) in enumerate(chips):
                rows = dst[a].at[4 * px + 2 * py + (1 - c)]
                _remote(rows, rows, send_sems.at[a, j], recv_sems.at[a, j], (x, y, 1 - c)).wait_recv()

    return pl.pallas_call(
        body, name="gather_finish", in_specs=[ANY] * n, out_specs=[ANY] * n,
        out_shape=[jax.ShapeDtypeStruct(b.shape, b.dtype) for b in bufs],
        input_output_aliases={a: a for a in range(n)},
        scratch_shapes=[pltpu.SemaphoreType.DMA((n, 3)), pltpu.SemaphoreType.DMA((n, 3))],
    )(*bufs)


def _place_shards(mats, l, dev):
    n = len(mats)

    def body(dev_ref, *refs):
        for a in range(n):
            refs[n + a][...] = refs[a][...].astype(bf16)

    return pl.pallas_call(
        body, name="place_shards",
        grid_spec=pltpu.PrefetchScalarGridSpec(
            num_scalar_prefetch=1, grid=(1,),
            in_specs=[pl.BlockSpec((None,) + m.shape[1:], lambda i, dv: (l, 0, 0)) for m in mats],
            out_specs=[pl.BlockSpec((None, None) + m.shape[1:], lambda i, dv: (dv[0], 0, 0, 0)) for m in mats]),
        out_shape=[jax.ShapeDtypeStruct((N_DEV, 1) + m.shape[1:], bf16) for m in mats],
        compiler_params=_params(("arbitrary",)),
    )(dev, *mats)


BIG = ("ffn1_w_gu", "ffn1_w_down", "mix_w_in", "mix_w_out", "ffn2_w_gu", "ffn2_w_down")
SHARDED_CONV = ("lru_conv_w", "ssd_conv_w")
REPLICATED = ("ffn1_pre_g", "ffn1_post_g", "mix_pre_g", "mix_post_g", "lru_conv_b", "lru_w_r", "lru_b_r", "lru_w_i",
              "lru_b_i", "lru_lambda", "ssd_conv_b", "ssd_dt_bias", "ssd_a_log", "ssd_d", "ssd_norm_g", "sgu_ln_g",
              "sgu_ln_b", "sgu_w_s", "sgu_b_s", "ffn2_pre_g", "ffn2_post_g")
WEIGHTS = ("ffn1_pre_g", "ffn1_post_g", "ffn1_w_gu", "ffn1_w_down", "mix_pre_g", "mix_post_g", "mix_w_in", "mix_w_out",
           "lru_conv_w", "lru_conv_b", "lru_w_r", "lru_b_r", "lru_w_i", "lru_b_i", "lru_lambda", "ssd_conv_w",
           "ssd_conv_b", "ssd_dt_bias", "ssd_a_log", "ssd_d", "ssd_norm_g", "sgu_ln_g", "sgu_ln_b", "sgu_w_s", "sgu_b_s",
           "ffn2_pre_g", "ffn2_post_g", "ffn2_w_gu", "ffn2_w_down")
DT_LO = PA_W + B_W + XBC_W
N_HEADS = B_W // HEAD
PACK_COLS = 1024


def _pack(arrays):
    flat = jnp.concatenate([a.reshape(-1) for a in arrays])
    rows = -(-flat.shape[0] // (8 * PACK_COLS)) * 8
    return jnp.pad(flat, (0, rows * PACK_COLS - flat.shape[0])).reshape(rows, PACK_COLS)


def _unpack(packed, shapes):
    flat = packed.reshape(-1)
    out, off = [], 0
    for s in shapes:
        size = 1
        for dim in s:
            size *= dim
        out.append(flat[off:off + size].reshape(s))
        off += size
    return out


def _widen_w_in(w):
    return jnp.concatenate([w[..., :DT_LO], jnp.repeat(w[..., DT_LO:DT_LO + N_HEADS], HEAD, axis=-1),
                            w[..., DT_LO + N_HEADS:]], axis=-1)


def _narrow_w_in_grad(g):
    dt = g[..., DT_LO:DT_LO + B_W]
    dt = dt.reshape(dt.shape[:-1] + (N_HEADS, HEAD)).sum(-1)
    return jnp.concatenate([g[..., :DT_LO], dt, g[..., DT_LO + B_W:]], axis=-1)


def _per_head(a):
    return a.reshape(a.shape[:-1] + (N_HEADS, HEAD)).sum(-1)


def kernel(x, ffn1_pre_g, ffn1_post_g, ffn1_w_gu, ffn1_w_down, mix_pre_g, mix_post_g, mix_w_in, mix_w_out, lru_conv_w, lru_conv_b, lru_w_r, lru_b_r, lru_w_i, lru_b_i, lru_lambda, ssd_conv_w, ssd_conv_b, ssd_dt_bias, ssd_a_log, ssd_d, ssd_norm_g, sgu_ln_g, sgu_ln_b, sgu_w_s, sgu_b_s, ffn2_pre_g, ffn2_post_g, ffn2_w_gu, ffn2_w_down, loss_target, m_ffn1_pre_g, m_ffn1_post_g, m_ffn1_w_gu, m_ffn1_w_down, m_mix_pre_g, m_mix_post_g, m_mix_w_in, m_mix_w_out, m_lru_conv_w, m_lru_conv_b, m_lru_w_r, m_lru_b_r, m_lru_w_i, m_lru_b_i, m_lru_lambda, m_ssd_conv_w, m_ssd_conv_b, m_ssd_dt_bias, m_ssd_a_log, m_ssd_d, m_ssd_norm_g, m_sgu_ln_g, m_sgu_ln_b, m_sgu_w_s, m_sgu_b_s, m_ffn2_pre_g, m_ffn2_post_g, m_ffn2_w_gu, m_ffn2_w_down, v_ffn1_pre_g, v_ffn1_post_g, v_ffn1_w_gu, v_ffn1_w_down, v_mix_pre_g, v_mix_post_g, v_mix_w_in, v_mix_w_out, v_lru_conv_w, v_lru_conv_b, v_lru_w_r, v_lru_b_r, v_lru_w_i, v_lru_b_i, v_lru_lambda, v_ssd_conv_w, v_ssd_conv_b, v_ssd_dt_bias, v_ssd_a_log, v_ssd_d, v_ssd_norm_g, v_sgu_ln_g, v_sgu_ln_b, v_sgu_w_s, v_sgu_b_s, v_ffn2_pre_g, v_ffn2_post_g, v_ffn2_w_gu, v_ffn2_w_down):
    given = dict(locals())
    w = {n: given[n] for n in WEIGHTS}
    mom = {n: given["m_" + n] for n in WEIGHTS}
    var = {n: given["v_" + n] for n in WEIGHTS}
    nl = ffn1_pre_g.shape[0]
    _, t, d = x.shape
    xi, yi, ci = _place()
    dev = 4 * xi + 2 * yi + ci
    c_dev = jnp.reshape(ci, (1,)).astype(jnp.int32)
    b_dev = jnp.reshape(2 * xi + yi, (1,)).astype(jnp.int32)

    conv_shapes = [lru_conv_w.shape, ssd_conv_w.shape]
    shards = [ffn1_w_gu, ffn1_w_down, _widen_w_in(mix_w_in), mix_w_out, ffn2_w_gu, ffn2_w_down]
    nbig = len(shards)
    dev_arr = jnp.reshape(dev, (1,)).astype(jnp.int32)
    conv_pack = _pack([lru_conv_w, ssd_conv_w])
    conv_buf = lax.dynamic_update_slice_in_dim(jnp.zeros((N_DEV,) + conv_pack.shape, f32), conv_pack[None], dev, axis=0)
    def gather_groups(l):
        return [(0, 1), (2, 3), (4, 5)] if l == 0 else [tuple(range(nbig))]

    gather_started = {}
    for l in range(nl):
        for gi, idx in enumerate(gather_groups(l)):
            bufs = list(_place_shards([shards[i] for i in idx], l, dev_arr)) + ([conv_buf] if (l, gi) == (0, 1) else [])
            gather_started[l, gi] = _split_start(f"gather_start_{l}_{gi}", [], bufs, (4 * len(bufs),), _gather_plan)

    def finish_gather(l, gi, after):
        waited = _split_wait(f"gather_wait_{l}_{gi}", gather_started[l, gi], 0, _gather_plan, after)
        return _gather_finish(list(waited))

    def conv_taps(conv_all):
        full = []
        for k, shape in enumerate(conv_shapes):
            per_dev = jnp.stack([_unpack(conv_all[s], conv_shapes)[k] for s in range(N_DEV)], axis=2)
            full.append(per_dev.reshape(shape[0], shape[1], N_DEV * shape[2]))
        return full

    def vec(a):
        return a.reshape(nl, 1, -1)

    def per_channel(a):
        return jnp.repeat(a, HEAD, axis=-1).reshape(nl, 1, B_W)

    eye = jnp.eye(A_W // HEAD, dtype=f32)

    def block_diag(a):
        return jnp.einsum("lhij,hg->lhigj", a, eye).reshape(nl, A_W, A_W).astype(bf16)

    causal = jnp.tril(jnp.ones((CHUNK, CHUNK), dtype=bool))
    p = dict(
        ffn1_pre=vec(ffn1_pre_g), ffn1_post=vec(ffn1_post_g), mix_pre=vec(mix_pre_g), mix_post=vec(mix_post_g),
        ffn2_pre=vec(ffn2_pre_g), ffn2_post=vec(ffn2_post_g),
        lru=(vec(lru_conv_b), block_diag(lru_w_r), block_diag(lru_w_i), vec(lru_b_r), vec(lru_b_i), vec(lru_lambda)),
        ssd=(vec(ssd_conv_b), per_channel(ssd_dt_bias), per_channel(ssd_a_log), per_channel(ssd_d), vec(ssd_norm_g)),
    )
    wm = jnp.where(causal, sgu_w_s, 0.0).astype(bf16)
    sgu_bias = jnp.repeat(jnp.swapaxes(sgu_b_s, 1, 2), HEAD, axis=2)
    sgu_f = (vec(sgu_ln_g), vec(sgu_ln_b), wm, sgu_bias)
    sgu_b = (vec(sgu_ln_g), vec(sgu_ln_b), wm, jnp.swapaxes(wm, 2, 3), sgu_bias)

    small_names = REPLICATED + SHARDED_CONV
    small_state = [_pack([src[n] for n in small_names])[None] for src in (w, mom, var)]
    prepared = [a for v in p.values() for a in (v if isinstance(v, tuple) else (v,))] + list(sgu_b) + small_state

    xs = x.reshape(t, d)
    saved, gathered, early_forward = [], [], {}
    for l in range(nl):
        x0 = xs
        if l == 0:
            wgu1, wd1 = finish_gather(0, 0, [x0] + prepared)
            deps = tuple(started[-1] for key, started in gather_started.items() if key != (0, 0))
        elif l in early_forward:
            wgu1, wd1, win, wout, wgu2, wd2 = _split_wait(f"forward_wait_{l}", early_forward[l], 0, _forward_plan, x0)
            deps = ()
        else:
            wgu1, wd1, win, wout, wgu2, wd2 = finish_gather(l, 0, x0)
            deps = ()
        x1, hb1, g1, u1, f1 = _ffn_fwd(x0, p["ffn1_pre"], p["ffn1_post"], wgu1, wd1, l, deps)
        if l == 0:
            win, wout, conv_all = finish_gather(0, 1, x1)
            lru_cw, ssd_cw = conv_taps(conv_all)
            p["lru"], p["ssd"] = (lru_cw,) + p["lru"], (ssd_cw,) + p["ssd"]
        hbm, pa, pb, pc = _mix_in_fwd(x1, p["mix_pre"], win, l)
        ya, h, gates = _lru_fwd(pa, *p["lru"], l)
        yb, yp, sp = _ssd_fwd(pb, *p["ssd"], l)
        yc = _sgu_fwd(pc, *sgu_f, l)
        x2, cat, m = _mix_out_fwd(x1, ya, yb, yc, p["mix_post"], wout, l)
        deps = ()
        if l == 0:
            wgu2, wd2 = finish_gather(0, 2, x2)
        elif l + 1 < nl:
            waited = _split_wait(f"gather_wait_{l + 1}_0", gather_started[l + 1, 0], 0, _gather_plan, x2)
            early_forward[l + 1] = _split_start(f"forward_start_{l + 1}", [], list(waited), (3 * nbig,), _forward_plan)
            deps = (early_forward[l + 1][-1],)
        xs, hb2, g2, u2, f2 = _ffn_fwd(x2, p["ffn2_pre"], p["ffn2_post"], wgu2, wd2, l, deps)
        gathered.append((wgu1, wd1, win, wout, wgu2, wd2))
        saved.append((x0, hb1, g1, u1, f1, x1, hbm, pa, pb, pc, h, gates, yp, sp, cat, m, x2, hb2, g2, u2, f2))
    dy, loss_part = _loss_fwd(xs, loss_target.reshape(t, d))
    loss = lax.psum(loss_part[0, 0], ("x", "y", "c"))

    small = {n: [None] * nl for n in REPLICATED + SHARDED_CONV}
    grads, delta, new_m, new_v = {}, {}, {}, {}
    fused = [n for n in BIG if n != "mix_w_in"]

    def oriented(a, n):
        return jnp.swapaxes(a, 1, 2) if n.endswith("w_gu") else a

    opt_in = {n: tuple(oriented(src[n], n) for src in (w, mom, var)) for n in fused}
    opt_out = {n: tuple(lax.empty(opt_in[n][0].shape, f32) for _ in range(4)) for n in fused}
    w_in_grads = [None] * nl
    grad_shapes = {n: (s.shape[2], s.shape[1]) if n.endswith("w_gu") else s.shape[1:] for n, s in zip(BIG, shards)}

    def start_pair(tag, lp, names, gbuf):
        landing = [lax.empty((4, 1) + grad_shapes[n], bf16) for n in names]
        started = _split_start(f"pair_start_{tag}", [gbuf[n] for n in names], landing, (4 * len(names),), _pair_plan)
        return tag, lp, names, started

    def finish_pair(pending, after):
        tag, lp, names, started = pending
        k = len(names)
        done = _split_wait(f"pair_wait_{tag}", started, k, _pair_plan, after)
        sums = [_pair_add(g, r, c_dev) for g, r in zip(done[:k], done[k:])]
        landing = [lax.empty((3,) + s.shape[1:], bf16) for s in sums]
        return tag, lp, names, _split_start(f"chips_start_{tag}", sums, landing, (3 * k,), _chips_plan)

    def finish_chips(pending, after, deps=()):
        tag, lp, names, started = pending
        k = len(names)
        done = _split_wait(f"chips_wait_{tag}", started, k, _chips_plan, after)
        last = None
        for n, s, q in zip(names, done[:k], done[k:]):
            if n == "mix_w_in":
                w_in_grads[lp] = last = _grad_sum(s, q, b_dev)
            else:
                opt_out[n] = tuple(_adamw_layer(*opt_in[n], s, q, b_dev, opt_out[n], lp, deps))
                last = opt_out[n][0]
        return last

    early = ("ffn2_w_gu", "ffn2_w_down", "mix_w_out")
    late = ("mix_w_in", "ffn1_w_gu", "ffn1_w_down")
    pending_pair = pending_chips = early_pair = early_chips = upper_started = None
    deferred = []
    names = REPLICATED + SHARDED_CONV
    assert nl > 1
    for l in reversed(range(nl)):
        x0, hb1, g1, u1, f1, x1, hbm, pa, pb, pc, h, gates, yp, sp, cat, m, x2, hb2, g2, u2, f2 = saved[l]
        wgu1, wd1, win, wout, wgu2, wd2 = gathered[l][:nbig]
        gbuf ={n: lax.empty((N_DEV, 1) + grad_shapes[n], bf16) for n in BIG}
        deps = () if pending_pair is None else (pending_pair[3][-1],)
        if l == 0:
            deps += (upper_started[-1],)
        dx2, dfb, act, dg, du, dpre, dpost = _ffn_bwd(x2, dy, f2, p["ffn2_pre"], p["ffn2_post"], g2, u2, wgu2, wd2, l, deps)
        small["ffn2_pre_g"][l], small["ffn2_post_g"][l] = dpre[0], dpost[0]
        gbuf["ffn2_w_gu"] = _wgrad_cols(hb2, dg, gbuf["ffn2_w_gu"], 0, 0)
        gbuf["ffn2_w_gu"] = _wgrad_cols(hb2, du, gbuf["ffn2_w_gu"], 0, dg.shape[0])
        gbuf["ffn2_w_down"] = _wgrad_rows(act, dfb, gbuf["ffn2_w_down"], 0)
        deps = ()
        if pending_pair is not None:
            pending_chips = finish_pair(pending_pair, dx2)
            deps = (pending_chips[3][-1],)

        dm, dya, dyb, dyc, dpost = _mix_out_bwd(dx2, m, p["mix_post"], wout, l, deps)
        small["mix_post_g"][l] = dpost[0]
        gbuf["mix_w_out"] = _wgrad_kblocks(cat, [dm], gbuf["mix_w_out"], 0)
        deps = ()
        if l == 0:
            early_pair = start_pair("0a", 0, early, gbuf)
            deps = (early_pair[3][-1],)
        dpc, dws, dbias, dlg, dlb = _sgu_bwd(pc, dyc, *sgu_b, l, deps)
        small["sgu_w_s"][l] = jnp.where(causal, dws, 0.0)
        small["sgu_b_s"][l] = dbias.reshape(CHUNK, C_W // HEAD, HEAD).sum(-1).T
        small["sgu_ln_g"][l], small["sgu_ln_b"][l] = dlg[0], dlb[0]
        dpb, dcw, dcb, ddtb, dalog, ddsk, dng = _ssd_bwd(pb, yp, sp, dyb, *p["ssd"], l)
        small["ssd_conv_w"][l], small["ssd_conv_b"][l], small["ssd_norm_g"][l] = dcw, dcb[0], dng[0]
        small["ssd_dt_bias"][l], small["ssd_a_log"][l], small["ssd_d"][l] = _per_head(ddtb[0]), _per_head(dalog[0]), _per_head(ddsk[0])
        deps = ()
        if l == 0:
            early_chips = finish_pair(early_pair, dpb)
            deps = (early_chips[3][-1],)
        dpa, dcw, dcb, dwr, dwi, dbr, dbi, dlam = _lru_bwd(pa, h, gates, dya, *p["lru"], l, deps)
        small["lru_conv_w"][l], small["lru_conv_b"][l], small["lru_lambda"][l] = dcw, dcb[0], dlam[0]
        small["lru_b_r"][l], small["lru_b_i"][l] = dbr[0], dbi[0]
        heads = range(A_W // HEAD)
        small["lru_w_r"][l] = jnp.stack([dwr[HEAD * i:HEAD * (i + 1), HEAD * i:HEAD * (i + 1)] for i in heads])
        small["lru_w_i"][l] = jnp.stack([dwi[HEAD * i:HEAD * (i + 1), HEAD * i:HEAD * (i + 1)] for i in heads])
        dx1, dpre = _mix_in_bwd(x1, dx2, p["mix_pre"], dpa, dpb, dpc, win, l)
        small["mix_pre_g"][l] = dpre[0]
        gbuf["mix_w_in"] = _wgrad_kblocks(hbm, [dpa, dpb, dpc], gbuf["mix_w_in"], 0)

        dy, dfb, act, dg, du, dpre, dpost = _ffn_bwd(x0, dx1, f1, p["ffn1_pre"], p["ffn1_post"], g1, u1, wgu1, wd1, l)
        small["ffn1_pre_g"][l], small["ffn1_post_g"][l] = dpre[0], dpost[0]
        gbuf["ffn1_w_gu"] = _wgrad_cols(hb1, dg, gbuf["ffn1_w_gu"], 0, 0)
        gbuf["ffn1_w_gu"] = _wgrad_cols(hb1, du, gbuf["ffn1_w_gu"], 0, dg.shape[0])
        gbuf["ffn1_w_down"] = _wgrad_rows(act, dfb, gbuf["ffn1_w_down"], 0)
        if pending_chips is not None:
            deferred.append(pending_chips)
            pending_chips = None
        pending_pair = start_pair(f"{l}", l, late if l == 0 else BIG, gbuf)
        if l == 1:
            upper = [jnp.stack(small[n][1:]) for n in names]
            upper_pack = _pack(upper)
            upper_buf = lax.dynamic_update_slice_in_dim(
                jnp.zeros((N_DEV,) + upper_pack.shape, f32), upper_pack[None], dev, axis=0)
            upper_started = _split_start("small_start", [], [upper_buf], (4,), _gather_plan)
    grad_x = dy.reshape(x.shape)

    lower = [jnp.stack(small[n][:1]) for n in names]
    lower_total = _sum_devices(_all_gather([_pack(lower)], (pending_pair[3][-1],))[0])
    late_chips = finish_pair(pending_pair, lower_total)
    order = lower_total
    for pending in deferred + [early_chips]:
        order = finish_chips(pending, order, (late_chips[3][-1],))
    upper_all = _gather_finish(list(_split_wait("small_wait", upper_started, 0, _gather_plan, order)))[0]
    upper_total = _sum_devices(upper_all)
    finish_chips(late_chips, [upper_total] + [opt_out[n][0] for n in fused] + [g for g in w_in_grads if g is not None])
    full = {n: jnp.concatenate([lo, up], axis=0) for n, lo, up in zip(
        names, _unpack(lower_total, [a.shape for a in lower]), _unpack(upper_total, [a.shape for a in upper]))}

    for n in fused:
        grads[n], delta[n], new_m[n], new_v[n] = (oriented(a, n) for a in opt_out[n])
    grads["mix_w_in"] = _narrow_w_in_grad(jnp.concatenate(w_in_grads, axis=0))
    delta["mix_w_in"], new_m["mix_w_in"], new_v["mix_w_in"] = _adamw(
        w["mix_w_in"], mom["mix_w_in"], var["mix_w_in"], grads["mix_w_in"])
    for n in REPLICATED:
        grads[n] = full[n]
    for n in SHARDED_CONV:
        cols = w[n].shape[2]
        grads[n] = lax.dynamic_slice_in_dim(full[n], dev * cols, cols, axis=2)
    shapes = [w[n].shape for n in names]
    packs = small_state + [_pack([grads[n] for n in names])[None]]
    for dst, packed in zip((delta, new_m, new_v), _adamw(*packs)):
        dst.update(zip(names, _unpack(packed[0], shapes)))

    return (loss, grad_x, *[grads[n] for n in WEIGHTS], *[delta[n] for n in WEIGHTS],
            *[new_m[n] for n in WEIGHTS], *[new_v[n] for n in WEIGHTS])
```

```python
import functools

import jax
import jax.numpy as jnp
from jax import lax
from jax.experimental import pallas as pl
from jax.experimental.pallas import tpu as pltpu

f32, bf16 = jnp.float32, jnp.bfloat16
MESH = pl.DeviceIdType.MESH
ANY = pl.BlockSpec(memory_space=pl.ANY)

N_DEV = 8
NORM_EPS = 1e-6
LRU_C = 8.0
CHUNK = 128
HEAD = 64
A_W, B_W, C_W = 384, 384, 256
B_STATE = 128
XBC_W = B_W + 4 * B_STATE
PA_W, PB_W, PC_W = 2 * A_W, B_W + XBC_W + B_W, 2 * C_W
IN_PAD = PA_W + PB_W + PC_W
ADAM_LR, ADAM_B1, ADAM_B2, ADAM_EPS, ADAM_WD, ADAM_STEP = 0.001, 0.9, 0.999, 1e-08, 0.01, 10
VMEM_LIMIT_BYTES = 56 * 1024 * 1024
FFN_BWD_SPLIT = 2
NEG_BIG = -1e30


def _params(sem=None):
    return pltpu.CompilerParams(dimension_semantics=sem, vmem_limit_bytes=VMEM_LIMIT_BYTES)


def _nn(a, b):
    return jnp.dot(a, b, preferred_element_type=f32)


def _nt(a, b):
    return lax.dot_general(a, b, (((1,), (1,)), ((), ())), preferred_element_type=f32)


def _tn(a, b):
    return lax.dot_general(a, b, (((0,), (0,)), ((), ())), preferred_element_type=f32)


def _sigmoid(x):
    return 0.5 * jnp.tanh(0.5 * x) + 0.5


def _softplus(x):
    return jnp.maximum(x, 0.0) + jnp.log(1.0 + jnp.exp(-jnp.abs(x)))


_GELU_C0, _GELU_C1 = 0.7978845608028654, 0.044715


def _gelu(x):
    t = jnp.tanh(_GELU_C0 * (x + _GELU_C1 * x * x * x))
    return 0.5 * x * (1.0 + t)


def _gelu_grad(x):
    t = jnp.tanh(_GELU_C0 * (x + _GELU_C1 * x * x * x))
    return 0.5 * (1.0 + t) + 0.5 * x * (1.0 - t * t) * _GELU_C0 * (1.0 + 3.0 * _GELU_C1 * x * x)


def _silu_grad(x, s):
    return s * (1.0 + x * (1.0 - s))


def _rms_fwd(x, g):
    r = lax.rsqrt(jnp.mean(x * x, axis=-1, keepdims=True) + NORM_EPS)
    return x * r * g


def _rms_bwd(x, g, dy):
    r = lax.rsqrt(jnp.mean(x * x, axis=-1, keepdims=True) + NORM_EPS)
    xh = x * r
    dxh = dy * g
    dx = r * (dxh - xh * jnp.mean(dxh * xh, axis=-1, keepdims=True))
    return dx, jnp.sum(dy * xh, axis=0, keepdims=True)


def _one_minus_exp(x):
    series = -x * (1.0 + x * (0.5 + x * (1.0 / 6.0 + x * (1.0 / 24.0))))
    return jnp.where(x > -0.01, series, 1.0 - jnp.exp(x))


def _cumsum_rows(x):
    row = lax.broadcasted_iota(jnp.int32, x.shape, 0)
    d = 1
    while d < x.shape[0]:
        x = x + jnp.where(row >= d, pltpu.roll(x, d, 0), 0.0)
        d *= 2
    return x


def _tile(t, cap):
    tm = min(cap, t)
    assert t % tm == 0
    return tm


def _after(body, n_in, deps):
    def wrapped(*refs):
        return body(*refs[:n_in], *refs[n_in + len(deps):])
    return wrapped


def _lspec(a, l):
    return pl.BlockSpec((None,) + a.shape[1:], lambda *_: (l,) + (0,) * (a.ndim - 1))


def _wd_rows(wd_ref):
    return wd_ref[:, 0].reshape(2 * wd_ref.shape[2], wd_ref.shape[3])


def _ffn_fwd(x, pre_g, post_g, wgu, wd, l, deps=()):
    t, d = x.shape
    nb, _, _, h = wgu.shape
    nj = nb // 2
    tm = _tile(t, 512)

    def body(x_ref, pg_ref, qg_ref, wg_ref, wu_ref, wd_ref, y_ref, hb_ref, g_ref, u_ref, f_ref, acc_ref):
        j = pl.program_id(1)

        @pl.when(j == 0)
        def _():
            hb_ref[...] = _rms_fwd(x_ref[...], pg_ref[...]).astype(bf16)

        hb = hb_ref[...]
        g = _nn(hb, wg_ref[0, 0])
        u = _nn(hb, wu_ref[0, 0])
        g_ref[0] = g.astype(bf16)
        u_ref[0] = u.astype(bf16)
        a = (g * _sigmoid(g) * u).astype(bf16)
        part = _nn(a, _wd_rows(wd_ref))

        @pl.when(j == 0)
        def _():
            acc_ref[...] = part

        @pl.when(j > 0)
        def _():
            acc_ref[...] += part

        @pl.when(j == nj - 1)
        def _():
            f = acc_ref[...]
            f_ref[...] = f
            y_ref[...] = x_ref[...] + 0.5 * _rms_fwd(f, qg_ref[...])

    row = pl.BlockSpec((tm, d), lambda i, j: (i, 0))
    vec = pl.BlockSpec((1, d), lambda i, j: (0, 0))
    act = pl.BlockSpec((1, tm, h), lambda i, j: (j, i, 0))
    return pl.pallas_call(
        _after(body, 6, deps), name="ffn_fwd", grid=(t // tm, nj),
        in_specs=[row, _lspec(pre_g, l), _lspec(post_g, l),
                  pl.BlockSpec((1, 1, d, h), lambda i, j: (j, 0, 0, 0)),
                  pl.BlockSpec((1, 1, d, h), lambda i, j: (j + nj, 0, 0, 0)),
                  pl.BlockSpec((2, 1, h // 2, d), lambda i, j: (j, 0, 0, 0))] + [ANY] * len(deps),
        out_specs=[row, row, act, act, row],
        out_shape=[jax.ShapeDtypeStruct((t, d), f32), jax.ShapeDtypeStruct((t, d), bf16),
                   jax.ShapeDtypeStruct((nj, t, h), bf16), jax.ShapeDtypeStruct((nj, t, h), bf16),
                   jax.ShapeDtypeStruct((t, d), f32)],
        scratch_shapes=[pltpu.VMEM((tm, d), f32)],
        compiler_params=_params(("arbitrary", "arbitrary")),
    )(x, pre_g, post_g, wgu, wgu, wd, *deps)


def _ffn_bwd(x, dy, f, pre_g, post_g, g, u, wgu, wd, l, deps=()):
    t, d = x.shape
    nj, _, h = g.shape
    tm = _tile(t, 512)

    def body(x_ref, dy_ref, f_ref, pg_ref, qg_ref, g_ref, u_ref, wg_ref, wu_ref, wd_ref,
             dx_ref, dfb_ref, a_ref, dg_ref, du_ref, dpg_ref, dqg_ref, dh_ref):
        i, j = pl.program_id(0), pl.program_id(1)

        @pl.when((i == 0) & (j == 0))
        def _():
            dpg_ref[...] = jnp.zeros_like(dpg_ref)
            dqg_ref[...] = jnp.zeros_like(dqg_ref)

        @pl.when(j == 0)
        def _():
            df, dq = _rms_bwd(f_ref[...], qg_ref[...], 0.5 * dy_ref[...])
            dfb_ref[...] = df.astype(bf16)
            dqg_ref[...] += dq
            dh_ref[...] = jnp.zeros_like(dh_ref)

        wdm, wg, wu = _wd_rows(wd_ref), wg_ref[0, 0], wu_ref[0, 0]
        sub = tm // FFN_BWD_SPLIT
        das = [_nt(dfb_ref[pl.ds(half * sub, sub), :], wdm) for half in range(FFN_BWD_SPLIT)]
        for half in range(FFN_BWD_SPLIT):
            rows = pl.ds(half * sub, sub)
            da = das[half]
            gv = g_ref[0, rows, :].astype(f32)
            uv = u_ref[0, rows, :].astype(f32)
            s = _sigmoid(gv)
            sg = gv * s
            a_ref[0, rows, :] = (sg * uv).astype(bf16)
            dg = (da * uv * _silu_grad(gv, s)).astype(bf16)
            du = (da * sg).astype(bf16)
            dg_ref[0, rows, :] = dg
            du_ref[0, rows, :] = du
            dh_ref[rows, :] += _nt(dg, wg) + _nt(du, wu)

        @pl.when(j == nj - 1)
        def _():
            dxn, dp = _rms_bwd(x_ref[...], pg_ref[...], dh_ref[...])
            dx_ref[...] = dy_ref[...] + dxn
            dpg_ref[...] += dp

    row = pl.BlockSpec((tm, d), lambda i, j: (i, 0))
    vec = pl.BlockSpec((1, d), lambda i, j: (0, 0))
    act = pl.BlockSpec((1, tm, h), lambda i, j: (j, i, 0))
    act_shape = jax.ShapeDtypeStruct((nj, t, h), bf16)
    return pl.pallas_call(
        _after(body, 10, deps), name="ffn_bwd", grid=(t // tm, nj),
        in_specs=[row, row, row, _lspec(pre_g, l), _lspec(post_g, l), act, act,
                  pl.BlockSpec((1, 1, d, h), lambda i, j: (j, 0, 0, 0)),
                  pl.BlockSpec((1, 1, d, h), lambda i, j: (j + nj, 0, 0, 0)),
                  pl.BlockSpec((2, 1, h // 2, d), lambda i, j: (j, 0, 0, 0))] + [ANY] * len(deps),
        out_specs=[row, row, act, act, act, vec, vec],
        out_shape=[jax.ShapeDtypeStruct((t, d), f32), jax.ShapeDtypeStruct((t, d), bf16),
                   act_shape, act_shape, act_shape,
                   jax.ShapeDtypeStruct((1, d), f32), jax.ShapeDtypeStruct((1, d), f32)],
        scratch_shapes=[pltpu.VMEM((tm, d), f32)],
        compiler_params=_params(("arbitrary", "arbitrary")),
    )(x, dy, f, pre_g, post_g, g, u, wgu, wgu, wd, *deps)


def _wgrad_cols(x, dy, buf, l, slot0):
    (t, k), (nj, _, n) = x.shape, dy.shape

    def body(x_ref, dy_ref, buf_ref, o_ref):
        o_ref[0, 0] = _tn(dy_ref[0], x_ref[...]).astype(bf16)

    return pl.pallas_call(
        body, name="wgrad_cols", grid=(nj,),
        in_specs=[pl.BlockSpec((t, k), lambda b: (0, 0)), pl.BlockSpec((1, t, n), lambda b: (b, 0, 0)), ANY],
        out_specs=pl.BlockSpec((1, 1, n, k), lambda b: (b + slot0, l, 0, 0)),
        out_shape=jax.ShapeDtypeStruct(buf.shape, bf16), input_output_aliases={2: 0},
        compiler_params=_params(("arbitrary",)),
    )(x, dy, buf)


def _wgrad_rows(x, dy, buf, l):
    (nj, t, k), (_, n) = x.shape, dy.shape

    def body(x_ref, dy_ref, buf_ref, o_ref):
        o_ref[:, 0] = _tn(x_ref[0], dy_ref[...]).astype(bf16).reshape(2, k // 2, n)

    return pl.pallas_call(
        body, name="wgrad_rows", grid=(nj,),
        in_specs=[pl.BlockSpec((1, t, k), lambda b: (b, 0, 0)), pl.BlockSpec((t, n), lambda b: (0, 0)), ANY],
        out_specs=pl.BlockSpec((2, 1, k // 2, n), lambda b: (b, l, 0, 0)),
        out_shape=jax.ShapeDtypeStruct(buf.shape, bf16), input_output_aliases={2: 0},
        compiler_params=_params(("arbitrary",)),
    )(x, dy, buf)


def _wgrad_kblocks(x, dys, buf, l):
    t, k = x.shape
    kb = k // N_DEV
    widths = [dy.shape[1] for dy in dys]
    n = sum(widths)
    nd = len(dys)

    def body(x_ref, *refs):
        dy_hbm, o_ref, dy_vmem = refs[:nd], refs[nd + 1], refs[nd + 2:]

        @pl.when(pl.program_id(0) == 0)
        def _():
            for src, dst in zip(dy_hbm, dy_vmem):
                pltpu.sync_copy(src, dst)

        off = 0
        for dst, w in zip(dy_vmem, widths):
            o_ref[0, 0, :, off:off + w] = _tn(x_ref[...], dst[...]).astype(bf16)
            off += w

    return pl.pallas_call(
        body, name="wgrad_kblocks", grid=(N_DEV,),
        in_specs=[pl.BlockSpec((t, kb), lambda s: (0, s))] + [ANY] * (nd + 1),
        out_specs=pl.BlockSpec((1, 1, kb, n), lambda s: (s, l, 0, 0)),
        out_shape=jax.ShapeDtypeStruct(buf.shape, bf16), input_output_aliases={nd + 1: 0},
        scratch_shapes=[pltpu.VMEM((t, w), bf16) for w in widths],
        compiler_params=_params(("arbitrary",)),
    )(x, *dys, buf)


def _gathered_rows(w_ref, lo, hi):
    return w_ref[:, 0, :, lo:hi].reshape(N_DEV * w_ref.shape[2], hi - lo)


def _gathered_spec(w):
    return pl.BlockSpec((N_DEV, 1) + w.shape[2:], lambda i: (0, 0, 0, 0))


def _mix_in_fwd(x, pre_g, w_in, l):
    t, d = x.shape
    tm = _tile(t, 512)

    def body(x_ref, g_ref, w_ref, hb_ref, pa_ref, pb_ref, pc_ref):
        hb = _rms_fwd(x_ref[...], g_ref[...]).astype(bf16)
        hb_ref[...] = hb
        pa_ref[...] = _nn(hb, _gathered_rows(w_ref, 0, PA_W))
        pb_ref[...] = _nn(hb, _gathered_rows(w_ref, PA_W, PA_W + PB_W))
        pc_ref[...] = _nn(hb, _gathered_rows(w_ref, PA_W + PB_W, IN_PAD))

    def row(w):
        return pl.BlockSpec((tm, w), lambda i: (i, 0))

    return pl.pallas_call(
        body, name="mix_in_fwd", grid=(t // tm,),
        in_specs=[row(d), _lspec(pre_g, l), _gathered_spec(w_in)],
        out_specs=[row(d), row(PA_W), row(PB_W), row(PC_W)],
        out_shape=[jax.ShapeDtypeStruct((t, d), bf16), jax.ShapeDtypeStruct((t, PA_W), f32),
                   jax.ShapeDtypeStruct((t, PB_W), f32), jax.ShapeDtypeStruct((t, PC_W), f32)],
        compiler_params=_params(("arbitrary",)),
    )(x, pre_g, w_in)


def _mix_in_bwd(x, dy, pre_g, dpa, dpb, dpc, w_in, l):
    t, d = x.shape
    tm = _tile(t, 512)

    def body(x_ref, dy_ref, g_ref, dpa_ref, dpb_ref, dpc_ref, w_ref, dx_ref, dg_ref):
        @pl.when(pl.program_id(0) == 0)
        def _():
            dg_ref[...] = jnp.zeros_like(dg_ref)

        wa, wb, wc = (_gathered_rows(w_ref, 0, PA_W), _gathered_rows(w_ref, PA_W, PA_W + PB_W),
                      _gathered_rows(w_ref, PA_W + PB_W, IN_PAD))
        halves = [pl.ds(k * (tm // 2), tm // 2) for k in range(2)]
        dhs = [_nt(dpa_ref[rows, :], wa) + _nt(dpb_ref[rows, :], wb) + _nt(dpc_ref[rows, :], wc) for rows in halves]
        for rows, dh in zip(halves, dhs):
            dxn, dg = _rms_bwd(x_ref[rows, :], g_ref[...], dh)
            dx_ref[rows, :] = dy_ref[rows, :] + dxn
            dg_ref[...] += dg

    def row(w):
        return pl.BlockSpec((tm, w), lambda i: (i, 0))

    vec = pl.BlockSpec((1, d), lambda i: (0, 0))
    return pl.pallas_call(
        body, name="mix_in_bwd", grid=(t // tm,),
        in_specs=[row(d), row(d), _lspec(pre_g, l), row(PA_W), row(PB_W), row(PC_W), _gathered_spec(w_in)],
        out_specs=[row(d), vec],
        out_shape=[jax.ShapeDtypeStruct((t, d), f32), jax.ShapeDtypeStruct((1, d), f32)],
        compiler_params=_params(("arbitrary",)),
    )(x, dy, pre_g, dpa, dpb, dpc, w_in)


def _mix_out_fwd(x, ya, yb, yc, post_g, w_out, l):
    t, d = x.shape
    tm = _tile(t, 512)

    def body(x_ref, ya_ref, yb_ref, yc_ref, g_ref, w_ref, y_ref, cat_ref, m_ref):
        cat_ref[:, 0:A_W] = ya_ref[...].astype(bf16)
        cat_ref[:, A_W:A_W + B_W] = yb_ref[...].astype(bf16)
        cat_ref[:, A_W + B_W:d] = yc_ref[...].astype(bf16)
        m = _nn(cat_ref[...], _gathered_rows(w_ref, 0, d))
        m_ref[...] = m
        y_ref[...] = x_ref[...] + _rms_fwd(m, g_ref[...])

    def row(w):
        return pl.BlockSpec((tm, w), lambda i: (i, 0))

    return pl.pallas_call(
        body, name="mix_out_fwd", grid=(t // tm,),
        in_specs=[row(d), row(A_W), row(B_W), row(C_W), _lspec(post_g, l), _gathered_spec(w_out)],
        out_specs=[row(d), row(d), row(d)],
        out_shape=[jax.ShapeDtypeStruct((t, d), f32), jax.ShapeDtypeStruct((t, d), bf16), jax.ShapeDtypeStruct((t, d), f32)],
        compiler_params=_params(("arbitrary",)),
    )(x, ya, yb, yc, post_g, w_out)


def _mix_out_bwd(dy, m, post_g, w_out, l, deps=()):
    t, d = m.shape
    tm = _tile(t, 512)

    def body(dy_ref, m_ref, g_ref, w_ref, dm_ref, dya_ref, dyb_ref, dyc_ref, dg_ref):
        @pl.when(pl.program_id(0) == 0)
        def _():
            dg_ref[...] = jnp.zeros_like(dg_ref)

        dm, dg = _rms_bwd(m_ref[...], g_ref[...], dy_ref[...])
        dmb = dm.astype(bf16)
        dm_ref[...] = dmb
        dg_ref[...] += dg
        dcat = _nt(dmb, _gathered_rows(w_ref, 0, d))
        dya_ref[...] = dcat[:, 0:A_W]
        dyb_ref[...] = dcat[:, A_W:A_W + B_W]
        dyc_ref[...] = dcat[:, A_W + B_W:d]

    def row(w):
        return pl.BlockSpec((tm, w), lambda i: (i, 0))

    vec = pl.BlockSpec((1, d), lambda i: (0, 0))
    return pl.pallas_call(
        _after(body, 4, deps), name="mix_out_bwd", grid=(t // tm,),
        in_specs=[row(d), row(d), _lspec(post_g, l), _gathered_spec(w_out)] + [ANY] * len(deps),
        out_specs=[row(d), row(A_W), row(B_W), row(C_W), vec],
        out_shape=[jax.ShapeDtypeStruct((t, d), bf16), jax.ShapeDtypeStruct((t, A_W), f32),
                   jax.ShapeDtypeStruct((t, B_W), f32), jax.ShapeDtypeStruct((t, C_W), f32),
                   jax.ShapeDtypeStruct((1, d), f32)],
        compiler_params=_params(("arbitrary",)),
    )(dy, m, post_g, w_out, *deps)


def _conv_fwd(buf_ref, halo, x, w, b, n):
    buf_ref[0:8, :] = halo
    buf_ref[8:8 + n, :] = x
    out = b + w[3:4, :] * x
    for k in range(3):
        out = out + w[k:k + 1, :] * buf_ref[pl.ds(5 + k, n), :]
    return out


def _conv_bwd(buf_ref, dbuf_ref, dout, dnext, w, n):
    dbuf_ref[0:n, :] = dout
    dbuf_ref[n:n + 8, :] = dnext
    dx = w[3:4, :] * dout
    dws = []
    for k in range(3):
        dx = dx + w[k:k + 1, :] * dbuf_ref[pl.ds(3 - k, n), :]
        dws.append(jnp.sum(dout * buf_ref[pl.ds(5 + k, n), :], axis=0, keepdims=True))
    dws.append(jnp.sum(dout * buf_ref[pl.ds(8, n), :], axis=0, keepdims=True))
    return dx, jnp.concatenate(dws, axis=0), jnp.sum(dout, axis=0, keepdims=True)


def _lru_gates(rec, wr, wi, br, bi, lam):
    rb = rec.astype(bf16)
    r = _sigmoid(_nn(rb, wr) + br)
    ig = _sigmoid(_nn(rb, wi) + bi)
    sp = _softplus(-lam)
    la = -LRU_C * r * sp
    a = jnp.exp(la)
    mult = jnp.sqrt(_one_minus_exp(2.0 * la))
    return rb, r, ig, sp, a, mult


def _scan_rows(a_ref, b_ref, o_ref, carry, n, reverse):
    row = lax.broadcasted_iota(jnp.int32, (8, a_ref.shape[1]), 0)
    nb = n // 8

    def step(k, carry):
        blk = (nb - 1 - k) if reverse else k
        rows = pl.ds(pl.multiple_of(blk * 8, 8), 8)
        a, b = a_ref[rows, :], b_ref[rows, :]
        for d in (1, 2, 4):
            shift = 8 - d if reverse else d
            keep = (row < 8 - d) if reverse else (row >= d)
            b = a * jnp.where(keep, pltpu.roll(b, shift, 0), 0.0) + b
            a = a * jnp.where(keep, pltpu.roll(a, shift, 0), 1.0)
        o = a * carry + b
        o_ref[rows, :] = o
        return o[0:1, :] if reverse else o[7:8, :]

    return lax.fori_loop(0, nb, step, carry, unroll=2)


N_GATES = 5


def _lru_fwd(pa, conv_w, conv_b, wr, wi, br, bi, lam, l):
    t = pa.shape[0]
    tc = _tile(t, 512)

    def body(pa_ref, halo_ref, cw_ref, cb_ref, wr_ref, wi_ref, br_ref, bi_ref, lam_ref,
             ya_ref, h_ref, gates_ref, buf_ref, u_ref, carry_ref):
        i = pl.program_id(0)

        @pl.when(i == 0)
        def _():
            carry_ref[...] = jnp.zeros_like(carry_ref)

        halo = jnp.where(i > 0, halo_ref[:, A_W:PA_W], 0.0)
        rec = _conv_fwd(buf_ref, halo, pa_ref[:, A_W:PA_W], cw_ref[...], cb_ref[...], tc)
        _, r, ig, _, a, mult = _lru_gates(rec, wr_ref[...], wi_ref[...], br_ref[...], bi_ref[...], lam_ref[...])
        for k, val in enumerate((rec, r, ig, a, mult)):
            gates_ref[k] = val
        u_ref[...] = mult * (ig * rec)

        carry_ref[...] = _scan_rows(gates_ref.at[3], u_ref, h_ref, carry_ref[...], tc, reverse=False)
        ya_ref[...] = h_ref[...] * _gelu(pa_ref[:, 0:A_W])

    vec = pl.BlockSpec((1, A_W), lambda i: (0, 0))
    mat = pl.BlockSpec((A_W, A_W), lambda i: (0, 0))
    row = pl.BlockSpec((tc, A_W), lambda i: (i, 0))
    return pl.pallas_call(
        body, name="lru_fwd", grid=(t // tc,),
        in_specs=[pl.BlockSpec((tc, PA_W), lambda i: (i, 0)),
                  pl.BlockSpec((8, PA_W), lambda i: (jnp.maximum(i * (tc // 8) - 1, 0), 0)),
                  *[_lspec(a, l) for a in (conv_w, conv_b, wr, wi, br, bi, lam)]],
        out_specs=[row, row, pl.BlockSpec((N_GATES, tc, A_W), lambda i: (0, i, 0))],
        out_shape=[jax.ShapeDtypeStruct((t, A_W), f32), jax.ShapeDtypeStruct((t, A_W), f32),
                   jax.ShapeDtypeStruct((N_GATES, t, A_W), f32)],
        scratch_shapes=[pltpu.VMEM((8 + tc, A_W), f32), pltpu.VMEM((tc, A_W), f32), pltpu.VMEM((1, A_W), f32)],
        compiler_params=_params(("arbitrary",)),
    )(pa, pa, conv_w, conv_b, wr, wi, br, bi, lam)


def _lru_bwd(pa, h, gates, dya, conv_w, conv_b, wr, wi, br, bi, lam, l, deps=()):
    t = pa.shape[0]
    tc = _tile(t, 512)
    nc = t // tc

    def body(pa_ref, halo_ref, h_ref, hhalo_ref, gates_ref, dya_ref, cw_ref, cb_ref, wr_ref, wi_ref, br_ref, bi_ref,
             lam_ref, dpa_ref, dcw_ref, dcb_ref, dwr_ref, dwi_ref, dbr_ref, dbi_ref, dlam_ref,
             buf_ref, dbuf_ref, hbuf_ref, g_ref, dh_ref, carry_ref, dnext_ref, dhbuf_ref):
        i = pl.program_id(0)
        c = nc - 1 - i

        @pl.when(i == 0)
        def _():
            carry_ref[...] = jnp.zeros_like(carry_ref)
            dnext_ref[...] = jnp.zeros_like(dnext_ref)
            for ref in (dcw_ref, dcb_ref, dwr_ref, dwi_ref, dbr_ref, dbi_ref, dlam_ref):
                ref[...] = jnp.zeros_like(ref)

        halo = jnp.where(c > 0, halo_ref[:, A_W:PA_W], 0.0)
        cw = cw_ref[...]
        buf_ref[0:8, :] = halo
        buf_ref[8:8 + tc, :] = pa_ref[:, A_W:PA_W]
        rec, r, ig, a, mult = (gates_ref[k] for k in range(N_GATES))
        rb = rec.astype(bf16)
        lam = lam_ref[...]
        sp = _softplus(-lam)
        hbuf_ref[0:8, :] = jnp.where(c > 0, hhalo_ref[...], 0.0)
        hbuf_ref[8:8 + tc, :] = h_ref[...]
        h_prev = hbuf_ref[pl.ds(7, tc), :]
        gate = pa_ref[:, 0:A_W]
        dya = dya_ref[...]
        dpa_ref[:, 0:A_W] = (dya * h_ref[...] * _gelu_grad(gate)).astype(bf16)
        gg = dya * _gelu(gate)
        g_ref[...] = a * gg
        carry_in = carry_ref[...]
        carry_ref[...] = _scan_rows(gates_ref.at[3], g_ref, dh_ref, carry_in, tc, reverse=True)
        dhbuf_ref[0:tc, :] = dh_ref[...]
        dhbuf_ref[tc:tc + 8, :] = jnp.broadcast_to(carry_in, (8, A_W))
        dh = gg + dhbuf_ref[pl.ds(1, tc), :]
        da = dh * h_prev
        dmult = dh * ig * rec
        dig = dh * mult * rec
        drec = dh * mult * ig
        dla = da * a - dmult * (a * a) / mult
        dr = dla * (-LRU_C * sp)
        dsp = jnp.sum(dla * (-LRU_C * r), axis=0, keepdims=True)
        dlam_ref[...] += dsp * (-_sigmoid(-lam))
        dpr = (dr * r * (1.0 - r))
        dpi = (dig * ig * (1.0 - ig))
        dprb, dpib = dpr.astype(bf16), dpi.astype(bf16)
        drec = drec + _nt(dprb, wr_ref[...]) + _nt(dpib, wi_ref[...])
        dwr_ref[...] += _tn(rb, dprb)
        dwi_ref[...] += _tn(rb, dpib)
        dbr_ref[...] += jnp.sum(dpr, axis=0, keepdims=True)
        dbi_ref[...] += jnp.sum(dpi, axis=0, keepdims=True)
        dx, dw, db = _conv_bwd(buf_ref, dbuf_ref, drec, dnext_ref[...], cw, tc)
        dnext_ref[...] = drec[0:8, :]
        dcw_ref[...] += dw
        dcb_ref[...] += db
        dpa_ref[:, A_W:PA_W] = dx.astype(bf16)

    vec = pl.BlockSpec((1, A_W), lambda i: (0, 0))
    mat = pl.BlockSpec((A_W, A_W), lambda i: (0, 0))
    cwspec = pl.BlockSpec((4, A_W), lambda i: (0, 0))

    def rev(w):
        return pl.BlockSpec((tc, w), lambda i: (nc - 1 - i, 0))

    def halo(w):
        return pl.BlockSpec((8, w), lambda i: (jnp.maximum((nc - 1 - i) * (tc // 8) - 1, 0), 0))

    chunk = pltpu.VMEM((tc, A_W), f32)
    return pl.pallas_call(
        _after(body, 13, deps), name="lru_bwd", grid=(nc,),
        in_specs=[rev(PA_W), halo(PA_W), rev(A_W), halo(A_W),
                  pl.BlockSpec((N_GATES, tc, A_W), lambda i: (0, nc - 1 - i, 0)), rev(A_W),
                  *[_lspec(a, l) for a in (conv_w, conv_b, wr, wi, br, bi, lam)]] + [ANY] * len(deps),
        out_specs=[rev(PA_W), cwspec, vec, mat, mat, vec, vec, vec],
        out_shape=[jax.ShapeDtypeStruct((t, PA_W), bf16), jax.ShapeDtypeStruct((4, A_W), f32),
                   jax.ShapeDtypeStruct((1, A_W), f32), jax.ShapeDtypeStruct((A_W, A_W), f32),
                   jax.ShapeDtypeStruct((A_W, A_W), f32), jax.ShapeDtypeStruct((1, A_W), f32),
                   jax.ShapeDtypeStruct((1, A_W), f32), jax.ShapeDtypeStruct((1, A_W), f32)],
        scratch_shapes=[pltpu.VMEM((8 + tc, A_W), f32), pltpu.VMEM((tc + 8, A_W), f32), pltpu.VMEM((8 + tc, A_W), f32),
                        chunk, chunk, pltpu.VMEM((1, A_W), f32), pltpu.VMEM((8, A_W), f32),
                        pltpu.VMEM((tc + 8, A_W), f32)],
        compiler_params=_params(("arbitrary",)),
    )(pa, pa, h, h, gates, dya, conv_w, conv_b, wr, wi, br, bi, lam, *deps)


def _sgu_norm(v, g, b):
    mu = jnp.mean(v, axis=-1, keepdims=True)
    vc = v - mu
    rstd = lax.rsqrt(jnp.mean(vc * vc, axis=-1, keepdims=True) + NORM_EPS)
    vh = vc * rstd
    return vh, rstd, vh * g + b


def _sgu_mix(w_ref, vb, bias):
    grp = lax.broadcasted_iota(jnp.int32, (CHUNK, C_W), 1) // HEAD
    out = bias
    for gi in range(C_W // HEAD):
        out = out + jnp.where(grp == gi, _nn(w_ref[gi], vb), 0.0)
    return out


def _sgu_fwd(pc, ln_g, ln_b, wm, bias, l):
    t = pc.shape[0]
    tm = _tile(t, 512)

    def body(pc_ref, g_ref, b_ref, w_ref, bias_ref, yc_ref):
        for ci in range(tm // CHUNK):
            rows = pl.ds(ci * CHUNK, CHUNK)
            ge = _gelu(pc_ref[rows, :])
            _, _, vn = _sgu_norm(ge[:, C_W:PC_W], g_ref[...], b_ref[...])
            yc_ref[rows, :] = ge[:, 0:C_W] * _sgu_mix(w_ref, vn.astype(bf16), bias_ref[...])

    vec = pl.BlockSpec((1, C_W), lambda i: (0, 0))
    return pl.pallas_call(
        body, name="sgu_fwd", grid=(t // tm,),
        in_specs=[pl.BlockSpec((tm, PC_W), lambda i: (i, 0)), *[_lspec(a, l) for a in (ln_g, ln_b, wm, bias)]],
        out_specs=pl.BlockSpec((tm, C_W), lambda i: (i, 0)),
        out_shape=jax.ShapeDtypeStruct((t, C_W), f32),
        compiler_params=_params(("arbitrary",)),
    )(pc, ln_g, ln_b, wm, bias)


def _sgu_bwd(pc, dyc, ln_g, ln_b, wm, wmt, bias, l, deps=()):
    t = pc.shape[0]
    tm = _tile(t, 512)

    def body(pc_ref, dyc_ref, g_ref, b_ref, w_ref, wt_ref, bias_ref, dpc_ref, dw_ref, dbias_ref, dg_ref, db_ref):
        @pl.when(pl.program_id(0) == 0)
        def _():
            for ref in (dw_ref, dbias_ref, dg_ref, db_ref):
                ref[...] = jnp.zeros_like(ref)

        grp = lax.broadcasted_iota(jnp.int32, (CHUNK, C_W), 1) // HEAD
        for ci in range(tm // CHUNK):
            rows = pl.ds(ci * CHUNK, CHUNK)
            x = pc_ref[rows, :]
            ge = _gelu(x)
            gv = g_ref[...]
            vh, rstd, vn = _sgu_norm(ge[:, C_W:PC_W], gv, b_ref[...])
            vb = vn.astype(bf16)
            mixed = _sgu_mix(w_ref, vb, bias_ref[...])
            dyc = dyc_ref[rows, :]
            du = dyc * mixed
            dmix = dyc * ge[:, 0:C_W]
            dmb = dmix.astype(bf16)
            dvn = jnp.zeros((CHUNK, C_W), f32)
            for gi in range(C_W // HEAD):
                dvn = dvn + jnp.where(grp == gi, _nn(wt_ref[gi], dmb), 0.0)
                dw_ref[gi] += _nt(jnp.where(grp == gi, dmix, 0.0).astype(bf16), vb)
            dbias_ref[...] += dmix
            dg_ref[...] += jnp.sum(dvn * vh, axis=0, keepdims=True)
            db_ref[...] += jnp.sum(dvn, axis=0, keepdims=True)
            dvh = dvn * gv
            dv = rstd * (dvh - jnp.mean(dvh, axis=-1, keepdims=True) - vh * jnp.mean(dvh * vh, axis=-1, keepdims=True))
            gg = _gelu_grad(x)
            dpc_ref[rows, 0:C_W] = (du * gg[:, 0:C_W]).astype(bf16)
            dpc_ref[rows, C_W:PC_W] = (dv * gg[:, C_W:PC_W]).astype(bf16)

    vec = pl.BlockSpec((1, C_W), lambda i: (0, 0))
    wspec = pl.BlockSpec((4, CHUNK, CHUNK), lambda i: (0, 0, 0))
    bspec = pl.BlockSpec((CHUNK, C_W), lambda i: (0, 0))
    return pl.pallas_call(
        _after(body, 7, deps), name="sgu_bwd", grid=(t // tm,),
        in_specs=[pl.BlockSpec((tm, PC_W), lambda i: (i, 0)), pl.BlockSpec((tm, C_W), lambda i: (i, 0)),
                  *[_lspec(a, l) for a in (ln_g, ln_b, wm, wmt, bias)]] + [ANY] * len(deps),
        out_specs=[pl.BlockSpec((tm, PC_W), lambda i: (i, 0)), wspec, bspec, vec, vec],
        out_shape=[jax.ShapeDtypeStruct((t, PC_W), bf16), jax.ShapeDtypeStruct((4, CHUNK, CHUNK), f32),
                   jax.ShapeDtypeStruct((CHUNK, C_W), f32), jax.ShapeDtypeStruct((1, C_W), f32),
                   jax.ShapeDtypeStruct((1, C_W), f32)],
        compiler_params=_params(("arbitrary",)),
    )(pc, dyc, ln_g, ln_b, wm, wmt, bias, *deps)


N_PAIR = B_W // 128
HEADS_PER_GROUP = 3


def _pair_groups(p):
    return (2 * p) // HEADS_PER_GROUP, (2 * p + 1) // HEADS_PER_GROUP


def _ssd_chunk(pb_ref, halo, buf_ref, cw, cb, dtb, alog):
    z = pb_ref[:, 0:B_W]
    pre = _conv_fwd(buf_ref, halo, pb_ref[:, B_W:B_W + XBC_W], cw, cb, CHUNK)
    sg = _sigmoid(pre)
    xbc = pre * sg
    xs = xbc[:, 0:B_W]
    bm = [xbc[:, B_W + k * B_STATE:B_W + (k + 1) * B_STATE] for k in range(2)]
    cm = [xbc[:, B_W + (2 + k) * B_STATE:B_W + (3 + k) * B_STATE] for k in range(2)]
    dtin = pb_ref[:, B_W + XBC_W:PB_W] + dtb
    dt = _softplus(dtin)
    a = -jnp.exp(alog)
    cs = _cumsum_rows(dt * a)
    return dict(z=z, pre=pre, sg=sg, xs=xs, bm=bm, cm=cm, dtin=dtin, dt=dt, a=a, cs=cs,
                ecs=jnp.exp(cs), ds=jnp.exp(cs[CHUNK - 1:CHUNK, :] - cs), xdt=xs * dt,
                bmb=[v.astype(bf16) for v in bm], cmb=[v.astype(bf16) for v in cm])


def _ssd_decay(cs_pair, half):
    cst = cs_pair.T
    lane0 = HEAD * half
    csc = jnp.broadcast_to(cs_pair[:, lane0:lane0 + 1], (CHUNK, CHUNK))
    csr = cst[lane0:lane0 + 1, :]
    tri = lax.broadcasted_iota(jnp.int32, (CHUNK, CHUNK), 0) >= lax.broadcasted_iota(jnp.int32, (CHUNK, CHUNK), 1)
    return jnp.exp(jnp.where(tri, csc - csr, NEG_BIG)), cst


def _ssd_fwd(pb, conv_w, conv_b, dtb, alog, dskip, norm_g, l):
    t = pb.shape[0]
    nc = t // CHUNK

    def body(pb_ref, halo_ref, cw_ref, cb_ref, dtb_ref, alog_ref, d_ref, ng_ref, yb_ref, yp_ref, sp_ref, buf_ref, s_ref):
        i = pl.program_id(0)

        @pl.when(i == 0)
        def _():
            s_ref[...] = jnp.zeros_like(s_ref)

        halo = jnp.where(i > 0, halo_ref[:, B_W:B_W + XBC_W], 0.0)
        q = _ssd_chunk(pb_ref, halo, buf_ref, cw_ref[...], cb_ref[...], dtb_ref[...], alog_ref[...])
        sp_ref[0] = s_ref[...]
        lane = lax.broadcasted_iota(jnp.int32, (CHUNK, 128), 1)
        rowi = lax.broadcasted_iota(jnp.int32, (128, B_STATE), 0)
        cb_mat = [_nt(q["cmb"][k], q["bmb"][k]) for k in range(2)]
        xd = q["xdt"] * q["ds"]
        for p in range(N_PAIR):
            cols = slice(128 * p, 128 * (p + 1))
            g_lo, g_hi = _pair_groups(p)
            cs_p, xdt_p = q["cs"][:, cols], q["xdt"][:, cols]
            s_p = s_ref[cols, :]
            s_pb = s_p.astype(bf16)
            y_p = jnp.zeros((CHUNK, 128), f32)
            for half, grp in ((0, g_lo), (1, g_hi)):
                lm, cst = _ssd_decay(cs_p, half)
                mb = (cb_mat[grp] * lm).astype(bf16)
                sel = (lane < HEAD) if half == 0 else (lane >= HEAD)
                y_p = y_p + _nn(mb, jnp.where(sel, xdt_p, 0.0).astype(bf16))
            off_lo = _nt(q["cmb"][g_lo], s_pb)
            off = off_lo if g_lo == g_hi else jnp.where(lane < HEAD, off_lo, _nt(q["cmb"][g_hi], s_pb))
            y_p = y_p + off * q["ecs"][:, cols] + q["xs"][:, cols] * d_ref[:, cols]
            yp_ref[:, cols] = y_p
            xd_pb = xd[:, cols].astype(bf16)
            upd_lo = _tn(xd_pb, q["bmb"][g_lo])
            upd = upd_lo if g_lo == g_hi else jnp.where(rowi < HEAD, upd_lo, _tn(xd_pb, q["bmb"][g_hi]))
            cd = jnp.exp(jnp.broadcast_to(cst[:, CHUNK - 1:CHUNK], (128, B_STATE)))
            s_ref[cols, :] = cd * s_p + upd
        z = q["z"]
        yg = yp_ref[...] * (z * _sigmoid(z))
        yb_ref[...] = _rms_fwd(yg, ng_ref[...])

    vec = pl.BlockSpec((1, B_W), lambda i: (0, 0))
    row = pl.BlockSpec((CHUNK, B_W), lambda i: (i, 0))
    return pl.pallas_call(
        body, name="ssd_fwd", grid=(nc,),
        in_specs=[pl.BlockSpec((CHUNK, PB_W), lambda i: (i, 0)),
                  pl.BlockSpec((8, PB_W), lambda i: (jnp.maximum(i * (CHUNK // 8) - 1, 0), 0)),
                  *[_lspec(a, l) for a in (conv_w, conv_b, dtb, alog, dskip, norm_g)]],
        out_specs=[row, row, pl.BlockSpec((1, B_W, B_STATE), lambda i: (i, 0, 0))],
        out_shape=[jax.ShapeDtypeStruct((t, B_W), f32), jax.ShapeDtypeStruct((t, B_W), f32),
                   jax.ShapeDtypeStruct((nc, B_W, B_STATE), f32)],
        scratch_shapes=[pltpu.VMEM((8 + CHUNK, XBC_W), f32), pltpu.VMEM((B_W, B_STATE), f32)],
        compiler_params=_params(("arbitrary",)),
    )(pb, pb, conv_w, conv_b, dtb, alog, dskip, norm_g)


def _ssd_bwd(pb, yp, sprev, dyb, conv_w, conv_b, dtb, alog, dskip, norm_g, l):
    t = pb.shape[0]
    nc = t // CHUNK

    def body(pb_ref, halo_ref, yp_ref, sp_ref, dyb_ref, cw_ref, cb_ref, dtb_ref, alog_ref, d_ref, ng_ref,
             dpb_ref, dcw_ref, dcb_ref, ddtb_ref, dalog_ref, dd_ref, dng_ref,
             buf_ref, dbuf_ref, ds_ref, dnext_ref, dxbc_ref, dcs_ref, dxdt_ref):
        i = pl.program_id(0)
        c = nc - 1 - i

        @pl.when(i == 0)
        def _():
            ds_ref[...] = jnp.zeros_like(ds_ref)
            dnext_ref[...] = jnp.zeros_like(dnext_ref)
            for ref in (dcw_ref, dcb_ref, ddtb_ref, dalog_ref, dd_ref, dng_ref):
                ref[...] = jnp.zeros_like(ref)

        halo = jnp.where(c > 0, halo_ref[:, B_W:B_W + XBC_W], 0.0)
        cw = cw_ref[...]
        q = _ssd_chunk(pb_ref, halo, buf_ref, cw, cb_ref[...], dtb_ref[...], alog_ref[...])
        z, xs, dt, a, ecs, dsd, xdt =q["z"], q["xs"], q["dt"], q["a"], q["ecs"], q["ds"], q["xdt"]
        sz = _sigmoid(z)
        siluz = z * sz
        yp = yp_ref[...]
        dyg, dng = _rms_bwd(yp * siluz, ng_ref[...], dyb_ref[...])
        dng_ref[...] += dng
        dy = dyg * siluz
        dpb_ref[:, 0:B_W] = (dyg * yp * _silu_grad(z, sz)).astype(bf16)
        dd_ref[...] += jnp.sum(dy * xs, axis=0, keepdims=True)
        g1 = dy * ecs
        lane = lax.broadcasted_iota(jnp.int32, (CHUNK, 128), 1)
        rowi = lax.broadcasted_iota(jnp.int32, (128, B_STATE), 0)
        rowc = lax.broadcasted_iota(jnp.int32, (CHUNK, 128), 0)
        cb_mat = [_nt(q["cmb"][k], q["bmb"][k]) for k in range(2)]
        d_cb = [jnp.zeros((CHUNK, CHUNK), f32) for _ in range(2)]
        d_b = [jnp.zeros((CHUNK, B_STATE), f32) for _ in range(2)]
        d_c = [jnp.zeros((CHUNK, B_STATE), f32) for _ in range(2)]
        for p in range(N_PAIR):
            cols = slice(128 * p, 128 * (p + 1))
            g_lo, g_hi = _pair_groups(p)
            lo, hi = lane < HEAD, lane >= HEAD
            cs_p, xdt_p, dy_p, ds_p, g1_p = q["cs"][:, cols], xdt[:, cols], dy[:, cols], dsd[:, cols], g1[:, cols]
            s_p = sp_ref[0, cols, :]
            s_pb = s_p.astype(bf16)
            dsn = ds_ref[cols, :]
            dsnb = dsn.astype(bf16)
            g1b = g1_p.astype(bf16)
            off_lo = _nt(q["cmb"][g_lo], s_pb)
            off = off_lo if g_lo == g_hi else jnp.where(lo, off_lo, _nt(q["cmb"][g_hi], s_pb))
            dcs_p = dy_p * off * ecs[:, cols]
            dsp_lo = _tn(g1b, q["cmb"][g_lo])
            dsp = dsp_lo if g_lo == g_hi else jnp.where(rowi < HEAD, dsp_lo, _tn(g1b, q["cmb"][g_hi]))
            dx_lo = _nt(q["bmb"][g_lo], dsnb)
            dxd = dx_lo if g_lo == g_hi else jnp.where(lo, dx_lo, _nt(q["bmb"][g_hi], dsnb))
            xd_p = xdt_p * ds_p
            if g_lo == g_hi:
                d_c[g_lo] = d_c[g_lo] + _nn(g1b, s_pb)
                d_b[g_lo] = d_b[g_lo] + _nn(xd_p.astype(bf16), dsnb)
            else:
                d_c[g_lo] = d_c[g_lo] + _nn(jnp.where(lo, g1_p, 0.0).astype(bf16), s_pb)
                d_c[g_hi] = d_c[g_hi] + _nn(jnp.where(hi, g1_p, 0.0).astype(bf16), s_pb)
                d_b[g_lo] = d_b[g_lo] + _nn(jnp.where(lo, xd_p, 0.0).astype(bf16), dsnb)
                d_b[g_hi] = d_b[g_hi] + _nn(jnp.where(hi, xd_p, 0.0).astype(bf16), dsnb)
            dxdt_p = dxd * ds_p
            t2 = dxd * xdt_p * ds_p
            dcs_p = dcs_p - t2
            dlast = jnp.sum(t2, axis=0, keepdims=True)
            cst = None
            for half, grp in ((0, g_lo), (1, g_hi)):
                sel = lo if half == 0 else hi
                lm, cst = _ssd_decay(cs_p, half)
                m = cb_mat[grp] * lm
                dyh = jnp.where(sel, dy_p, 0.0).astype(bf16)
                xdh = jnp.where(sel, xdt_p, 0.0).astype(bf16)
                dm = _nt(dyh, xdh)
                pm = dm * m
                col = jnp.sum(pm, axis=1, keepdims=True) - jnp.sum(pm.T, axis=1, keepdims=True)
                dcs_p = dcs_p + jnp.where(lane == HEAD * half, col, 0.0)
                d_cb[grp] = d_cb[grp] + dm * lm
                dxdt_p = dxdt_p + _tn(m.astype(bf16), dyh)
            cdcol = jnp.exp(jnp.broadcast_to(cst[:, CHUNK - 1:CHUNK], (128, B_STATE)))
            ds_ref[cols, :] = cdcol * dsn + dsp
            dcd_row = jnp.sum((dsn * s_p).T, axis=0, keepdims=True)
            dlast = dlast + dcd_row * ecs[CHUNK - 1:CHUNK, cols]
            dcs_ref[:, cols] = dcs_p + jnp.where(rowc == CHUNK - 1, dlast, 0.0)
            dxdt_ref[:, cols] = dxdt_p
        for k in range(2):
            dcbb = d_cb[k].astype(bf16)
            d_c[k] = d_c[k] + _nn(dcbb, q["bmb"][k])
            d_b[k] = d_b[k] + _tn(dcbb, q["cmb"][k])
            dxbc_ref[:, B_W + k * B_STATE:B_W + (k + 1) * B_STATE] = d_b[k]
            dxbc_ref[:, B_W + (2 + k) * B_STATE:B_W + (3 + k) * B_STATE] = d_c[k]
        dxdt = dxdt_ref[...]
        dxbc_ref[:, 0:B_W] = dy * d_ref[...] + dxdt * dt
        dcs = dcs_ref[...]
        dad = jnp.sum(dcs, axis=0, keepdims=True) - _cumsum_rows(dcs) + dcs
        ddt = dxdt * xs + dad * a
        dalog_ref[...] += jnp.sum(dad * dt, axis=0, keepdims=True) * a
        dtraw = ddt * _sigmoid(q["dtin"])
        ddtb_ref[...] += jnp.sum(dtraw, axis=0, keepdims=True)
        dpb_ref[:, B_W + XBC_W:PB_W] = dtraw.astype(bf16)
        dpre = dxbc_ref[...] * _silu_grad(q["pre"], q["sg"])
        dx, dw, db = _conv_bwd(buf_ref, dbuf_ref, dpre, dnext_ref[...], cw, CHUNK)
        dnext_ref[...] = dpre[0:8, :]
        dcw_ref[...] += dw
        dcb_ref[...] += db
        dpb_ref[:, B_W:B_W + XBC_W] = dx.astype(bf16)

    vec = pl.BlockSpec((1, B_W), lambda i: (0, 0))
    cwspec = pl.BlockSpec((4, XBC_W), lambda i: (0, 0))
    cbspec = pl.BlockSpec((1, XBC_W), lambda i: (0, 0))

    def rev(w):
        return pl.BlockSpec((CHUNK, w), lambda i: (nc - 1 - i, 0))

    vshape = jax.ShapeDtypeStruct((1, B_W), f32)
    return pl.pallas_call(
        body, name="ssd_bwd", grid=(nc,),
        in_specs=[rev(PB_W), pl.BlockSpec((8, PB_W), lambda i: (jnp.maximum((nc - 1 - i) * (CHUNK // 8) - 1, 0), 0)),
                  rev(B_W), pl.BlockSpec((1, B_W, B_STATE), lambda i: (nc - 1 - i, 0, 0)), rev(B_W),
                  *[_lspec(a, l) for a in (conv_w, conv_b, dtb, alog, dskip, norm_g)]],
        out_specs=[rev(PB_W), cwspec, cbspec, vec, vec, vec, vec],
        out_shape=[jax.ShapeDtypeStruct((t, PB_W), bf16), jax.ShapeDtypeStruct((4, XBC_W), f32),
                   jax.ShapeDtypeStruct((1, XBC_W), f32), vshape, vshape, vshape, vshape],
        scratch_shapes=[pltpu.VMEM((8 + CHUNK, XBC_W), f32), pltpu.VMEM((CHUNK + 8, XBC_W), f32),
                        pltpu.VMEM((B_W, B_STATE), f32), pltpu.VMEM((8, XBC_W), f32),
                        pltpu.VMEM((CHUNK, XBC_W), f32), pltpu.VMEM((CHUNK, B_W), f32), pltpu.VMEM((CHUNK, B_W), f32)],
        compiler_params=_params(("arbitrary",)),
    )(pb, pb, yp, sprev, dyb, conv_w, conv_b, dtb, alog, dskip, norm_g)


def _loss_fwd(y, target):
    t, d = y.shape
    tm = _tile(t, 512)

    def body(y_ref, t_ref, dy_ref, loss_ref):
        @pl.when(pl.program_id(0) == 0)
        def _():
            loss_ref[...] = jnp.zeros_like(loss_ref)

        e = y_ref[...] - t_ref[...]
        dy_ref[...] = e * (1.0 / d)
        per_tok = jnp.mean(e * e, axis=-1, keepdims=True)
        loss_ref[...] += 0.5 * jnp.sum(per_tok, axis=0, keepdims=True)

    row = pl.BlockSpec((tm, d), lambda i: (i, 0))
    return pl.pallas_call(
        body, name="loss_fwd", grid=(t // tm,), in_specs=[row, row],
        out_specs=[row, pl.BlockSpec((1, 128), lambda i: (0, 0))],
        out_shape=[jax.ShapeDtypeStruct((t, d), f32), jax.ShapeDtypeStruct((1, 128), f32)],
        compiler_params=_params(("arbitrary",)),
    )(y, target)


def _row_tile(r):
    return 512 if r % 512 == 0 else r


def _pair_add(g, r, c_dev):
    _, nl, rows, cols = g.shape
    tr = _row_tile(rows)

    def body(c_ref, g_ref, r_ref, o_ref):
        o_ref[...] = (g_ref[...].astype(f32) + r_ref[...].astype(f32)).astype(bf16)

    blk = (None, None, tr, cols)
    return pl.pallas_call(
        body, name="pair_add",
        grid_spec=pltpu.PrefetchScalarGridSpec(
            num_scalar_prefetch=1, grid=(4, nl, rows // tr),
            in_specs=[pl.BlockSpec(blk, lambda b, l, i, c: (2 * b + c[0], l, i, 0)),
                      pl.BlockSpec(blk, lambda b, l, i, c: (b, l, i, 0))],
            out_specs=pl.BlockSpec(blk, lambda b, l, i, c: (b, l, i, 0))),
        out_shape=jax.ShapeDtypeStruct(r.shape, bf16),
        compiler_params=_params(("arbitrary", "arbitrary", "arbitrary")),
    )(c_dev, g, r)


def _grad_sum(s, q, b_dev):
    _, nl, rows, cols = s.shape
    tr = _row_tile(rows)

    def body(b_ref, s_ref, q0_ref, q1_ref, q2_ref, o_ref):
        o_ref[...] = ((s_ref[...].astype(f32) + q0_ref[...].astype(f32)) + q1_ref[...].astype(f32)) + q2_ref[...].astype(f32)

    blk = (None, None, tr, cols)

    def qspec(k):
        return pl.BlockSpec(blk, lambda l, i, b: (k, l, i, 0))

    return pl.pallas_call(
        body, name="grad_sum",
        grid_spec=pltpu.PrefetchScalarGridSpec(
            num_scalar_prefetch=1, grid=(nl, rows // tr),
            in_specs=[pl.BlockSpec(blk, lambda l, i, b: (b[0], l, i, 0)), qspec(0), qspec(1), qspec(2)],
            out_specs=pl.BlockSpec((None, tr, cols), lambda l, i, b: (l, i, 0))),
        out_shape=jax.ShapeDtypeStruct(s.shape[1:], f32),
        compiler_params=_params(("arbitrary", "arbitrary")),
    )(b_dev, s, q, q, q)


def _sum_devices(parts):
    n, rows, cols = parts.shape
    tr = _row_tile(rows)

    def body(p_ref, o_ref):
        acc = p_ref[0]
        for k in range(1, n):
            acc = acc + p_ref[k]
        o_ref[...] = acc

    return pl.pallas_call(
        body, name="sum_devices", grid=(rows // tr,),
        in_specs=[pl.BlockSpec((n, tr, cols), lambda i: (0, i, 0))],
        out_specs=pl.BlockSpec((tr, cols), lambda i: (i, 0)),
        out_shape=jax.ShapeDtypeStruct((rows, cols), f32),
        compiler_params=_params(("arbitrary",)),
    )(parts)


def _adamw(w, m, v, g):
    nl, rows, cols = w.shape
    tr = _row_tile(rows)

    def body(w_ref, m_ref, v_ref, g_ref, d_ref, nm_ref, nv_ref):
        d_ref[...], nm_ref[...], nv_ref[...] = _adamw_math(w_ref[...], m_ref[...], v_ref[...], g_ref[...])

    blk = pl.BlockSpec((None, tr, cols), lambda l, i: (l, i, 0))
    shape = jax.ShapeDtypeStruct(w.shape, f32)
    return pl.pallas_call(
        body, name="adamw", grid=(nl, rows // tr), in_specs=[blk] * 4, out_specs=[blk] * 3,
        out_shape=[shape] * 3, compiler_params=_params(("arbitrary", "arbitrary")),
    )(w, m, v, g)


def _adamw_math(w, m, v, g):
    nm = ADAM_B1 * m + (1.0 - ADAM_B1) * g
    nv = ADAM_B2 * v + (1.0 - ADAM_B2) * (g * g)
    m_hat = nm / (1.0 - ADAM_B1 ** ADAM_STEP)
    v_hat = nv / (1.0 - ADAM_B2 ** ADAM_STEP)
    return -ADAM_LR * (m_hat / (jnp.sqrt(v_hat) + ADAM_EPS) + ADAM_WD * w), nm, nv


def _adamw_layer(w, m, v, s, q, b_dev, outs, l, deps=()):
    _, rows, cols = w.shape
    tr = _row_tile(rows)

    def body(b_ref, w_ref, m_ref, v_ref, s_ref, q0_ref, q1_ref, q2_ref, o0, o1, o2, o3, g_ref, d_ref, nm_ref, nv_ref):
        g = ((s_ref[...].astype(f32) + q0_ref[...].astype(f32)) + q1_ref[...].astype(f32)) + q2_ref[...].astype(f32)
        g_ref[...] = g
        d_ref[...], nm_ref[...], nv_ref[...] = _adamw_math(w_ref[...], m_ref[...], v_ref[...], g)

    wspec = pl.BlockSpec((None, tr, cols), lambda i, b: (l, i, 0))
    blk = (None, None, tr, cols)

    def qspec(k):
        return pl.BlockSpec(blk, lambda i, b: (k, 0, i, 0))

    shape = jax.ShapeDtypeStruct(w.shape, f32)
    return pl.pallas_call(
        _after(body, 12, deps), name="adamw_layer",
        grid_spec=pltpu.PrefetchScalarGridSpec(
            num_scalar_prefetch=1, grid=(rows // tr,),
            in_specs=[wspec] * 3 + [pl.BlockSpec(blk, lambda i, b: (b[0], 0, i, 0)), qspec(0), qspec(1), qspec(2)]
            + [ANY] * (4 + len(deps)),
            out_specs=[wspec] * 4),
        out_shape=[shape] * 4, input_output_aliases={8 + k: k for k in range(4)},
        compiler_params=_params(("arbitrary",)),
    )(b_dev, w, m, v, s, q, q, q, *outs, *deps)


def _place():
    return lax.axis_index("x"), lax.axis_index("y"), lax.axis_index("c")


def _all_gather(shards, deps=()):
    n = len(shards)
    nd = len(deps)

    def body(*refs):
        src, dst = refs[:n], refs[n:2 * n]
        send_sems, recv_sems, local_sems = refs[2 * n:]
        x, y, c = _place()
        me, sibling = (x, y, c), (x, y, 1 - c)
        chips = [(1 - x, y), (x, 1 - y), (1 - x, 1 - y)]

        def copy(a, k, block, to, from_shard=False):
            px, py, pc = block
            rows = dst[a].at[4 * px + 2 * py + pc]
            return pltpu.make_async_remote_copy(
                src_ref=src[a] if from_shard else rows, dst_ref=rows,
                send_sem=send_sems.at[a, k], recv_sem=recv_sems.at[a, k], device_id=to, device_id_type=MESH)

        mine = [pltpu.make_async_copy(src[a], dst[a].at[4 * x + 2 * y + c], local_sems.at[a]) for a in range(n)]
        for cp in mine:
            cp.start()
        first = []
        for a in range(n):
            first.append(copy(a, 0, me, sibling, True))
            first += [copy(a, 1 + j, me, (*chip, c), True) for j, chip in enumerate(chips)]
        for cp in first:
            cp.start()
        passed = []
        for j, chip in enumerate(chips):
            for a in range(n):
                copy(a, 1 + j, (*chip, c), me).wait_recv()
                fwd = copy(a, 4 + j, (*chip, c), sibling)
                fwd.start()
                passed.append(fwd)
        for a in range(n):
            copy(a, 0, sibling, me).wait_recv()
            for j, chip in enumerate(chips):
                copy(a, 4 + j, (*chip, 1 - c), me).wait_recv()
        for cp in first + passed:
            cp.wait_send()
        for cp in mine:
            cp.wait()

    return pl.pallas_call(
        _after(body, n, deps), name="all_gather", in_specs=[ANY] * (n + nd), out_specs=[ANY] * n,
        out_shape=[jax.ShapeDtypeStruct((N_DEV,) + s.shape, s.dtype) for s in shards],
        scratch_shapes=[pltpu.SemaphoreType.DMA((n, 7)), pltpu.SemaphoreType.DMA((n, 7)), pltpu.SemaphoreType.DMA((n,))],
    )(*shards, *deps)


HBM = pl.BlockSpec(memory_space=pltpu.HBM)
SEM = pl.BlockSpec(memory_space=pltpu.SEMAPHORE)
_EFFECT = pltpu.SideEffectType.DATAFLOW_SIDE_EFFECTING


def _split_start(name, srcs, dsts, sem_shape, plan):
    ns, nb = len(srcs), len(srcs) + len(dsts)

    def body(*refs):
        send_sems, recv_sems = refs[nb], refs[nb + 1]
        for cp in plan(refs[:ns], refs[ns:nb], send_sems, recv_sems):
            cp.start()
        refs[-1][...] = jnp.zeros_like(refs[-1])

    bufs = list(srcs) + list(dsts)
    return pl.pallas_call(
        body, name=name,
        out_shape=(pltpu.SemaphoreType.DMA(sem_shape), pltpu.SemaphoreType.DMA(sem_shape),
                   *[pltpu.HBM(a.shape, a.dtype) for a in bufs], jax.ShapeDtypeStruct((8, 128), f32)),
        in_specs=[HBM] * nb, out_specs=(SEM, SEM, *[HBM] * nb, pl.BlockSpec(memory_space=pltpu.VMEM)),
        input_output_aliases={i: 2 + i for i in range(nb)},
        compiler_params=pltpu.CompilerParams(has_side_effects=_EFFECT),
    )(*[pltpu.with_memory_space_constraint(a, pltpu.HBM) for a in bufs])


def _split_wait(name, started, ns, plan, after):
    send_sems, recv_sems = started[0], started[1]
    bufs = list(started[2:-1])
    nb = len(bufs)
    after = list(after) if isinstance(after, (list, tuple)) else [after]

    def body(*refs):
        for cp in plan(refs[:ns], refs[ns:nb], refs[nb], refs[nb + 1]):
            cp.wait_send()
            cp.wait_recv()

    return pl.pallas_call(
        body, name=name, out_shape=tuple(pltpu.HBM(a.shape, a.dtype) for a in bufs),
        in_specs=[HBM] * nb + [SEM, SEM] + [ANY] * len(after), out_specs=tuple([HBM] * nb),
        input_output_aliases={i: i for i in range(nb)},
        compiler_params=pltpu.CompilerParams(has_side_effects=_EFFECT),
    )(*bufs, send_sems, recv_sems, *after)


def _remote(src, dst, send_sem, recv_sem, to):
    return pltpu.make_async_remote_copy(src_ref=src, dst_ref=dst, send_sem=send_sem, recv_sem=recv_sem,
                                        device_id=to, device_id_type=MESH)


def _gather_plan(src, dst, send_sems, recv_sems):
    x, y, c = _place()
    peers = [(x, y, 1 - c), (1 - x, y, c), (x, 1 - y, c), (1 - x, 1 - y, c)]
    copies = []
    for a in range(len(dst)):
        rows = dst[a].at[4 * x + 2 * y + c]
        copies += [_remote(rows, rows, send_sems.at[4 * a + k], recv_sems.at[4 * a + k], peer) for k, peer in enumerate(peers)]
    return copies


def _pair_plan(src, dst, send_sems, recv_sems):
    x, y, c = _place()
    return [_remote(src[a].at[2 * b + (1 - c)], dst[a].at[b], send_sems.at[4 * a + b], recv_sems.at[4 * a + b], (x, y, 1 - c))
            for a in range(len(src)) for b in range(4)]


def _chips_plan(src, dst, send_sems, recv_sems):
    x, y, c = _place()
    chips = [(1 - x, y), (x, 1 - y), (1 - x, 1 - y)]
    return [_remote(src[a].at[2 * px + py], dst[a].at[j], send_sems.at[3 * a + j], recv_sems.at[3 * a + j], (px, py, c))
            for a in range(len(src)) for j, (px, py) in enumerate(chips)]


def _forward_plan(src, dst, send_sems, recv_sems):
    x, y, c = _place()
    copies = []
    for a in range(len(dst)):
        for j, (px, py) in enumerate([(1 - x, y), (x, 1 - y), (1 - x, 1 - y)]):
            rows = dst[a].at[4 * px + 2 * py + c]
            copies.append(_remote(rows, rows, send_sems.at[3 * a + j], recv_sems.at[3 * a + j], (x, y, 1 - c)))
    return copies


def _gather_finish(bufs):
    n = len(bufs)

    def body(*refs):
        dst = refs[n:2 * n]
        send_sems, recv_sems = refs[2 * n:]
        x, y, c = _place()
        chips = [(1 - x, y), (x, 1 - y), (1 - x, 1 - y)]
        passed = []
        for a in range(n):
            for j, (px, py) in enumerate(chips):
                rows = dst[a].at[4 * px + 2 * py + c]
                passed.append(_remote(rows, rows, send_sems.at[a, j], recv_sems.at[a, j], (x, y, 1 - c)))
        for cp in passed:
            cp.start()
        for cp in passed:
            cp.wait_send()
        for a in range(n):
            for j, (px, py) in enumerate(chips):
                rows = dst[a].at[4 * px + 2 * py + (1 - c)]
                _remote(rows, rows, send_sems.at[a, j], recv_sems.at[a, j], (x, y, 1 - c)).wait_recv()

    return pl.pallas_call(
        body, name="gather_finish", in_specs=[ANY] * n, out_specs=[ANY] * n,
        out_shape=[jax.ShapeDtypeStruct(b.shape, b.dtype) for b in bufs],
        input_output_aliases={a: a for a in range(n)},
        scratch_shapes=[pltpu.SemaphoreType.DMA((n, 3)), pltpu.SemaphoreType.DMA((n, 3))],
    )(*bufs)


def _place_shards(mats, l, dev):
    n = len(mats)

    def body(dev_ref, *refs):
        for a in range(n):
            refs[n + a][...] = refs[a][...].astype(bf16)

    return pl.pallas_call(
        body, name="place_shards",
        grid_spec=pltpu.PrefetchScalarGridSpec(
            num_scalar_prefetch=1, grid=(1,),
            in_specs=[pl.BlockSpec((None,) + m.shape[1:], lambda i, dv: (l, 0, 0)) for m in mats],
            out_specs=[pl.BlockSpec((None, None) + m.shape[1:], lambda i, dv: (dv[0], 0, 0, 0)) for m in mats]),
        out_shape=[jax.ShapeDtypeStruct((N_DEV, 1) + m.shape[1:], bf16) for m in mats],
        compiler_params=_params(("arbitrary",)),
    )(dev, *mats)


BIG = ("ffn1_w_gu", "ffn1_w_down", "mix_w_in", "mix_w_out", "ffn2_w_gu", "ffn2_w_down")
SHARDED_CONV = ("lru_conv_w", "ssd_conv_w")
REPLICATED = ("ffn1_pre_g", "ffn1_post_g", "mix_pre_g", "mix_post_g", "lru_conv_b", "lru_w_r", "lru_b_r", "lru_w_i",
              "lru_b_i", "lru_lambda", "ssd_conv_b", "ssd_dt_bias", "ssd_a_log", "ssd_d", "ssd_norm_g", "sgu_ln_g",
              "sgu_ln_b", "sgu_w_s", "sgu_b_s", "ffn2_pre_g", "ffn2_post_g")
WEIGHTS = ("ffn1_pre_g", "ffn1_post_g", "ffn1_w_gu", "ffn1_w_down", "mix_pre_g", "mix_post_g", "mix_w_in", "mix_w_out",
           "lru_conv_w", "lru_conv_b", "lru_w_r", "lru_b_r", "lru_w_i", "lru_b_i", "lru_lambda", "ssd_conv_w",
           "ssd_conv_b", "ssd_dt_bias", "ssd_a_log", "ssd_d", "ssd_norm_g", "sgu_ln_g", "sgu_ln_b", "sgu_w_s", "sgu_b_s",
           "ffn2_pre_g", "ffn2_post_g", "ffn2_w_gu", "ffn2_w_down")
DT_LO = PA_W + B_W + XBC_W
N_HEADS = B_W // HEAD
PACK_COLS = 1024


def _size(shape):
    size = 1
    for dim in shape:
        size *= dim
    return size


def _pack_rows(shape):
    return -(-_size(shape) // PACK_COLS)


def _pack(arrays):
    pieces = [jnp.pad(a.reshape(-1), (0, _pack_rows(a.shape) * PACK_COLS - _size(a.shape))) for a in arrays]
    rows = sum(_pack_rows(a.shape) for a in arrays)
    if rows % 8:
        pieces.append(jnp.zeros(((8 - rows % 8) * PACK_COLS,), f32))
    return jnp.concatenate(pieces).reshape(-1, PACK_COLS)


def _unpack(packed, shapes):
    out, row = [], 0
    for s in shapes:
        nr = _pack_rows(s)
        out.append(packed[row:row + nr].reshape(-1)[:_size(s)].reshape(s))
        row += nr
    return out


def _widen_w_in(w):
    return jnp.concatenate([w[..., :DT_LO], jnp.repeat(w[..., DT_LO:DT_LO + N_HEADS], HEAD, axis=-1),
                            w[..., DT_LO + N_HEADS:]], axis=-1)


def _narrow_w_in_grad(g):
    dt = g[..., DT_LO:DT_LO + B_W]
    dt = dt.reshape(dt.shape[:-1] + (N_HEADS, HEAD)).sum(-1)
    return jnp.concatenate([g[..., :DT_LO], dt, g[..., DT_LO + B_W:]], axis=-1)


def _per_head(a):
    return a.reshape(a.shape[:-1] + (N_HEADS, HEAD)).sum(-1)


def kernel(x, ffn1_pre_g, ffn1_post_g, ffn1_w_gu, ffn1_w_down, mix_pre_g, mix_post_g, mix_w_in, mix_w_out, lru_conv_w, lru_conv_b, lru_w_r, lru_b_r, lru_w_i, lru_b_i, lru_lambda, ssd_conv_w, ssd_conv_b, ssd_dt_bias, ssd_a_log, ssd_d, ssd_norm_g, sgu_ln_g, sgu_ln_b, sgu_w_s, sgu_b_s, ffn2_pre_g, ffn2_post_g, ffn2_w_gu, ffn2_w_down, loss_target, m_ffn1_pre_g, m_ffn1_post_g, m_ffn1_w_gu, m_ffn1_w_down, m_mix_pre_g, m_mix_post_g, m_mix_w_in, m_mix_w_out, m_lru_conv_w, m_lru_conv_b, m_lru_w_r, m_lru_b_r, m_lru_w_i, m_lru_b_i, m_lru_lambda, m_ssd_conv_w, m_ssd_conv_b, m_ssd_dt_bias, m_ssd_a_log, m_ssd_d, m_ssd_norm_g, m_sgu_ln_g, m_sgu_ln_b, m_sgu_w_s, m_sgu_b_s, m_ffn2_pre_g, m_ffn2_post_g, m_ffn2_w_gu, m_ffn2_w_down, v_ffn1_pre_g, v_ffn1_post_g, v_ffn1_w_gu, v_ffn1_w_down, v_mix_pre_g, v_mix_post_g, v_mix_w_in, v_mix_w_out, v_lru_conv_w, v_lru_conv_b, v_lru_w_r, v_lru_b_r, v_lru_w_i, v_lru_b_i, v_lru_lambda, v_ssd_conv_w, v_ssd_conv_b, v_ssd_dt_bias, v_ssd_a_log, v_ssd_d, v_ssd_norm_g, v_sgu_ln_g, v_sgu_ln_b, v_sgu_w_s, v_sgu_b_s, v_ffn2_pre_g, v_ffn2_post_g, v_ffn2_w_gu, v_ffn2_w_down):
    given = dict(locals())
    w = {n: given[n] for n in WEIGHTS}
    mom = {n: given["m_" + n] for n in WEIGHTS}
    var = {n: given["v_" + n] for n in WEIGHTS}
    nl = ffn1_pre_g.shape[0]
    _, t, d = x.shape
    xi, yi, ci = _place()
    dev = 4 * xi + 2 * yi + ci
    c_dev = jnp.reshape(ci, (1,)).astype(jnp.int32)
    b_dev = jnp.reshape(2 * xi + yi, (1,)).astype(jnp.int32)

    conv_shapes = [lru_conv_w.shape, ssd_conv_w.shape]
    shards = [ffn1_w_gu, ffn1_w_down, _widen_w_in(mix_w_in), mix_w_out, ffn2_w_gu, ffn2_w_down]
    nbig = len(shards)
    dev_arr = jnp.reshape(dev, (1,)).astype(jnp.int32)
    conv_pack = _pack([lru_conv_w, ssd_conv_w])
    conv_buf = lax.dynamic_update_slice_in_dim(jnp.zeros((N_DEV,) + conv_pack.shape, f32), conv_pack[None], dev, axis=0)
    def gather_groups(l):
        return [(0, 1), (2, 3), (4, 5)] if l == 0 else [tuple(range(nbig))]

    gather_started = {}
    for l in range(nl):
        for gi, idx in enumerate(gather_groups(l)):
            bufs = list(_place_shards([shards[i] for i in idx], l, dev_arr)) + ([conv_buf] if (l, gi) == (0, 1) else [])
            gather_started[l, gi] = _split_start(f"gather_start_{l}_{gi}", [], bufs, (4 * len(bufs),), _gather_plan)

    def finish_gather(l, gi, after):
        waited = _split_wait(f"gather_wait_{l}_{gi}", gather_started[l, gi], 0, _gather_plan, after)
        return _gather_finish(list(waited))

    def conv_taps(conv_all):
        full = []
        for k, shape in enumerate(conv_shapes):
            per_dev = jnp.stack([_unpack(conv_all[s], conv_shapes)[k] for s in range(N_DEV)], axis=2)
            full.append(per_dev.reshape(shape[0], shape[1], N_DEV * shape[2]))
        return full

    def vec(a):
        return a.reshape(nl, 1, -1)

    def per_channel(a):
        return jnp.repeat(a, HEAD, axis=-1).reshape(nl, 1, B_W)

    eye = jnp.eye(A_W // HEAD, dtype=f32)

    def block_diag(a):
        return jnp.einsum("lhij,hg->lhigj", a, eye).reshape(nl, A_W, A_W).astype(bf16)

    causal = jnp.tril(jnp.ones((CHUNK, CHUNK), dtype=bool))
    p = dict(
        ffn1_pre=vec(ffn1_pre_g), ffn1_post=vec(ffn1_post_g), mix_pre=vec(mix_pre_g), mix_post=vec(mix_post_g),
        ffn2_pre=vec(ffn2_pre_g), ffn2_post=vec(ffn2_post_g),
        lru=(vec(lru_conv_b), block_diag(lru_w_r), block_diag(lru_w_i), vec(lru_b_r), vec(lru_b_i), vec(lru_lambda)),
        ssd=(vec(ssd_conv_b), per_channel(ssd_dt_bias), per_channel(ssd_a_log), per_channel(ssd_d), vec(ssd_norm_g)),
    )
    wm = jnp.where(causal, sgu_w_s, 0.0).astype(bf16)
    sgu_bias = jnp.repeat(jnp.swapaxes(sgu_b_s, 1, 2), HEAD, axis=2)
    sgu_f = (vec(sgu_ln_g), vec(sgu_ln_b), wm, sgu_bias)
    sgu_b = (vec(sgu_ln_g), vec(sgu_ln_b), wm, jnp.swapaxes(wm, 2, 3), sgu_bias)

    small_names = REPLICATED + SHARDED_CONV
    small_state = [_pack([src[n] for n in small_names])[None] for src in (w, mom, var)]
    prepared = [a for v in p.values() for a in (v if isinstance(v, tuple) else (v,))] + list(sgu_b) + small_state

    xs = x.reshape(t, d)
    saved, gathered, early_forward = [], [], {}
    for l in range(nl):
        x0 = xs
        if l == 0:
            wgu1, wd1 = finish_gather(0, 0, [x0] + prepared)
            deps = tuple(started[-1] for key, started in gather_started.items() if key != (0, 0))
        elif l in early_forward:
            wgu1, wd1, win, wout, wgu2, wd2 = _split_wait(f"forward_wait_{l}", early_forward[l], 0, _forward_plan, x0)
            deps = ()
        else:
            wgu1, wd1, win, wout, wgu2, wd2 = finish_gather(l, 0, x0)
            deps = ()
        x1, hb1, g1, u1, f1 = _ffn_fwd(x0, p["ffn1_pre"], p["ffn1_post"], wgu1, wd1, l, deps)
        if l == 0:
            win, wout, conv_all = finish_gather(0, 1, x1)
            lru_cw, ssd_cw = conv_taps(conv_all)
            p["lru"], p["ssd"] = (lru_cw,) + p["lru"], (ssd_cw,) + p["ssd"]
        hbm, pa, pb, pc = _mix_in_fwd(x1, p["mix_pre"], win, l)
        ya, h, gates = _lru_fwd(pa, *p["lru"], l)
        yb, yp, sp = _ssd_fwd(pb, *p["ssd"], l)
        yc = _sgu_fwd(pc, *sgu_f, l)
        x2, cat, m = _mix_out_fwd(x1, ya, yb, yc, p["mix_post"], wout, l)
        deps = ()
        if l == 0:
            wgu2, wd2 = finish_gather(0, 2, x2)
        elif l + 1 < nl:
            waited = _split_wait(f"gather_wait_{l + 1}_0", gather_started[l + 1, 0], 0, _gather_plan, x2)
            early_forward[l + 1] = _split_start(f"forward_start_{l + 1}", [], list(waited), (3 * nbig,), _forward_plan)
            deps = (early_forward[l + 1][-1],)
        xs, hb2, g2, u2, f2 = _ffn_fwd(x2, p["ffn2_pre"], p["ffn2_post"], wgu2, wd2, l, deps)
        gathered.append((wgu1, wd1, win, wout, wgu2, wd2))
        saved.append((x0, hb1, g1, u1, f1, x1, hbm, pa, pb, pc, h, gates, yp, sp, cat, m, x2, hb2, g2, u2, f2))
    dy, loss_part = _loss_fwd(xs, loss_target.reshape(t, d))
    loss = lax.psum(loss_part[0, 0], ("x", "y", "c"))

    small = {n: [None] * nl for n in REPLICATED + SHARDED_CONV}
    grads, delta, new_m, new_v = {}, {}, {}, {}
    fused = [n for n in BIG if n != "mix_w_in"]

    def oriented(a, n):
        return jnp.swapaxes(a, 1, 2) if n.endswith("w_gu") else a

    opt_in = {n: tuple(oriented(src[n], n) for src in (w, mom, var)) for n in fused}
    opt_out = {n: tuple(lax.empty(opt_in[n][0].shape, f32) for _ in range(4)) for n in fused}
    w_in_grads = [None] * nl
    grad_shapes = {n: (s.shape[2], s.shape[1]) if n.endswith("w_gu") else s.shape[1:] for n, s in zip(BIG, shards)}

    def start_pair(tag, lp, names, gbuf):
        landing = [lax.empty((4, 1) + grad_shapes[n], bf16) for n in names]
        started = _split_start(f"pair_start_{tag}", [gbuf[n] for n in names], landing, (4 * len(names),), _pair_plan)
        return tag, lp, names, started

    def finish_pair(pending, after):
        tag, lp, names, started = pending
        k = len(names)
        done = _split_wait(f"pair_wait_{tag}", started, k, _pair_plan, after)
        sums = [_pair_add(g, r, c_dev) for g, r in zip(done[:k], done[k:])]
        landing = [lax.empty((3,) + s.shape[1:], bf16) for s in sums]
        return tag, lp, names, _split_start(f"chips_start_{tag}", sums, landing, (3 * k,), _chips_plan)

    def finish_chips(pending, after, deps=()):
        tag, lp, names, started = pending
        k = len(names)
        done = _split_wait(f"chips_wait_{tag}", started, k, _chips_plan, after)
        last = None
        for n, s, q in zip(names, done[:k], done[k:]):
            if n == "mix_w_in":
                w_in_grads[lp] = last = _grad_sum(s, q, b_dev)
            else:
                opt_out[n] = tuple(_adamw_layer(*opt_in[n], s, q, b_dev, opt_out[n], lp, deps))
                last = opt_out[n][0]
        return last

    early = ("ffn2_w_gu", "ffn2_w_down", "mix_w_out")
    late = ("mix_w_in", "ffn1_w_gu", "ffn1_w_down")
    pending_pair = pending_chips = early_pair = early_chips = upper_started = None
    deferred = []
    names = REPLICATED + SHARDED_CONV
    assert nl > 1
    for l in reversed(range(nl)):
        x0, hb1, g1, u1, f1, x1, hbm, pa, pb, pc, h, gates, yp, sp, cat, m, x2, hb2, g2, u2, f2 = saved[l]
        wgu1, wd1, win, wout, wgu2, wd2 = gathered[l][:nbig]
        gbuf ={n: lax.empty((N_DEV, 1) + grad_shapes[n], bf16) for n in BIG}
        deps = () if pending_pair is None else (pending_pair[3][-1],)
        if l == 0:
            deps += (upper_started[-1],)
        dx2, dfb, act, dg, du, dpre, dpost = _ffn_bwd(x2, dy, f2, p["ffn2_pre"], p["ffn2_post"], g2, u2, wgu2, wd2, l, deps)
        small["ffn2_pre_g"][l], small["ffn2_post_g"][l] = dpre[0], dpost[0]
        gbuf["ffn2_w_gu"] = _wgrad_cols(hb2, dg, gbuf["ffn2_w_gu"], 0, 0)
        gbuf["ffn2_w_gu"] = _wgrad_cols(hb2, du, gbuf["ffn2_w_gu"], 0, dg.shape[0])
        gbuf["ffn2_w_down"] = _wgrad_rows(act, dfb, gbuf["ffn2_w_down"], 0)
        deps = ()
        if pending_pair is not None:
            pending_chips = finish_pair(pending_pair, dx2)
            deps = (pending_chips[3][-1],)

        dm, dya, dyb, dyc, dpost = _mix_out_bwd(dx2, m, p["mix_post"], wout, l, deps)
        small["mix_post_g"][l] = dpost[0]
        gbuf["mix_w_out"] = _wgrad_kblocks(cat, [dm], gbuf["mix_w_out"], 0)
        deps = ()
        if l == 0:
            early_pair = start_pair("0a", 0, early, gbuf)
            deps = (early_pair[3][-1],)
        dpc, dws, dbias, dlg, dlb = _sgu_bwd(pc, dyc, *sgu_b, l, deps)
        small["sgu_w_s"][l] = jnp.where(causal, dws, 0.0)
        small["sgu_b_s"][l] = dbias.reshape(CHUNK, C_W // HEAD, HEAD).sum(-1).T
        small["sgu_ln_g"][l], small["sgu_ln_b"][l] = dlg[0], dlb[0]
        dpb, dcw, dcb, ddtb, dalog, ddsk, dng = _ssd_bwd(pb, yp, sp, dyb, *p["ssd"], l)
        small["ssd_conv_w"][l], small["ssd_conv_b"][l], small["ssd_norm_g"][l] = dcw, dcb[0], dng[0]
        small["ssd_dt_bias"][l], small["ssd_a_log"][l], small["ssd_d"][l] = _per_head(ddtb[0]), _per_head(dalog[0]), _per_head(ddsk[0])
        deps = ()
        if l == 0:
            early_chips = finish_pair(early_pair, dpb)
            deps = (early_chips[3][-1],)
        dpa, dcw, dcb, dwr, dwi, dbr, dbi, dlam = _lru_bwd(pa, h, gates, dya, *p["lru"], l, deps)
        small["lru_conv_w"][l], small["lru_conv_b"][l], small["lru_lambda"][l] = dcw, dcb[0], dlam[0]
        small["lru_b_r"][l], small["lru_b_i"][l] = dbr[0], dbi[0]
        heads = range(A_W // HEAD)
        small["lru_w_r"][l] = jnp.stack([dwr[HEAD * i:HEAD * (i + 1), HEAD * i:HEAD * (i + 1)] for i in heads])
        small["lru_w_i"][l] = jnp.stack([dwi[HEAD * i:HEAD * (i + 1), HEAD * i:HEAD * (i + 1)] for i in heads])
        dx1, dpre = _mix_in_bwd(x1, dx2, p["mix_pre"], dpa, dpb, dpc, win, l)
        small["mix_pre_g"][l] = dpre[0]
        gbuf["mix_w_in"] = _wgrad_kblocks(hbm, [dpa, dpb, dpc], gbuf["mix_w_in"], 0)

        dy, dfb, act, dg, du, dpre, dpost = _ffn_bwd(x0, dx1, f1, p["ffn1_pre"], p["ffn1_post"], g1, u1, wgu1, wd1, l)
        small["ffn1_pre_g"][l], small["ffn1_post_g"][l] = dpre[0], dpost[0]
        gbuf["ffn1_w_gu"] = _wgrad_cols(hb1, dg, gbuf["ffn1_w_gu"], 0, 0)
        gbuf["ffn1_w_gu"] = _wgrad_cols(hb1, du, gbuf["ffn1_w_gu"], 0, dg.shape[0])
        gbuf["ffn1_w_down"] = _wgrad_rows(act, dfb, gbuf["ffn1_w_down"], 0)
        if pending_chips is not None:
            deferred.append(pending_chips)
            pending_chips = None
        pending_pair = start_pair(f"{l}", l, late if l == 0 else BIG, gbuf)
        if l == 1:
            upper = [jnp.stack(small[n][1:]) for n in names]
            upper_pack = _pack(upper)
            upper_buf = lax.dynamic_update_slice_in_dim(
                jnp.zeros((N_DEV,) + upper_pack.shape, f32), upper_pack[None], dev, axis=0)
            upper_started = _split_start("small_start", [], [upper_buf], (4,), _gather_plan)
    grad_x = dy.reshape(x.shape)

    lower = [jnp.stack(small[n][:1]) for n in names]
    lower_total = _sum_devices(_all_gather([_pack(lower)], (pending_pair[3][-1],))[0])
    late_chips = finish_pair(pending_pair, lower_total)
    order = lower_total
    for pending in deferred + [early_chips]:
        order = finish_chips(pending, order, (late_chips[3][-1],))
    upper_all = _gather_finish(list(_split_wait("small_wait", upper_started, 0, _gather_plan, order)))[0]
    upper_total = _sum_devices(upper_all)
    finish_chips(late_chips, [upper_total] + [opt_out[n][0] for n in fused] + [g for g in w_in_grads if g is not None])
    full = {n: jnp.concatenate([lo, up], axis=0) for n, lo, up in zip(
        names, _unpack(lower_total, [a.shape for a in lower]), _unpack(upper_total, [a.shape for a in upper]))}

    for n in fused:
        grads[n], delta[n], new_m[n], new_v[n] = (oriented(a, n) for a in opt_out[n])
    grads["mix_w_in"] = _narrow_w_in_grad(jnp.concatenate(w_in_grads, axis=0))
    delta["mix_w_in"], new_m["mix_w_in"], new_v["mix_w_in"] = _adamw(
        w["mix_w_in"], mom["mix_w_in"], var["mix_w_in"], grads["mix_w_in"])
    for n in REPLICATED:
        grads[n] = full[n]
    for n in SHARDED_CONV:
        cols = w[n].shape[2]
        grads[n] = lax.dynamic_slice_in_dim(full[n], dev * cols, cols, axis=2)
    shapes = [w[n].shape for n in names]
    packs = small_state + [_pack([grads[n] for n in names])[None]]
    for dst, packed in zip((delta, new_m, new_v), _adamw(*packs)):
        dst.update(zip(names, _unpack(packed[0], shapes)))

    return (loss, grad_x, *[grads[n] for n in WEIGHTS], *[delta[n] for n in WEIGHTS],
            *[new_m[n] for n in WEIGHTS], *[new_v[n] for n in WEIGHTS])
```

```python
import functools

import jax
import jax.numpy as jnp
from jax import lax
from jax.experimental import pallas as pl
from jax.experimental.pallas import tpu as pltpu

f32, bf16 = jnp.float32, jnp.bfloat16
MESH = pl.DeviceIdType.MESH
ANY = pl.BlockSpec(memory_space=pl.ANY)

N_DEV = 8
NORM_EPS = 1e-6
LRU_C = 8.0
CHUNK = 128
HEAD = 64
A_W, B_W, C_W = 384, 384, 256
B_STATE = 128
XBC_W = B_W + 4 * B_STATE
PA_W, PB_W, PC_W = 2 * A_W, B_W + XBC_W + B_W, 2 * C_W
IN_PAD = PA_W + PB_W + PC_W
ADAM_LR, ADAM_B1, ADAM_B2, ADAM_EPS, ADAM_WD, ADAM_STEP = 0.001, 0.9, 0.999, 1e-08, 0.01, 10
VMEM_LIMIT_BYTES = 56 * 1024 * 1024
FFN_BWD_SPLIT = 2
ADAMW_BLOCK_BYTES = 2 * 1024 * 1024
NEG_BIG = -1e30


def _params(sem=None):
    return pltpu.CompilerParams(dimension_semantics=sem, vmem_limit_bytes=VMEM_LIMIT_BYTES)


def _nn(a, b):
    return jnp.dot(a, b, preferred_element_type=f32)


def _nt(a, b):
    return lax.dot_general(a, b, (((1,), (1,)), ((), ())), preferred_element_type=f32)


def _tn(a, b):
    return lax.dot_general(a, b, (((0,), (0,)), ((), ())), preferred_element_type=f32)


def _sigmoid(x):
    return 0.5 * jnp.tanh(0.5 * x) + 0.5


def _softplus(x):
    return jnp.maximum(x, 0.0) + jnp.log(1.0 + jnp.exp(-jnp.abs(x)))


_GELU_C0, _GELU_C1 = 0.7978845608028654, 0.044715


def _gelu(x):
    t = jnp.tanh(_GELU_C0 * (x + _GELU_C1 * x * x * x))
    return 0.5 * x * (1.0 + t)


def _gelu_grad(x):
    t = jnp.tanh(_GELU_C0 * (x + _GELU_C1 * x * x * x))
    return 0.5 * (1.0 + t) + 0.5 * x * (1.0 - t * t) * _GELU_C0 * (1.0 + 3.0 * _GELU_C1 * x * x)


def _silu_grad(x, s):
    return s * (1.0 + x * (1.0 - s))


def _rms_fwd(x, g):
    r = lax.rsqrt(jnp.mean(x * x, axis=-1, keepdims=True) + NORM_EPS)
    return x * r * g


def _rms_bwd(x, g, dy):
    r = lax.rsqrt(jnp.mean(x * x, axis=-1, keepdims=True) + NORM_EPS)
    xh = x * r
    dxh = dy * g
    dx = r * (dxh - xh * jnp.mean(dxh * xh, axis=-1, keepdims=True))
    return dx, jnp.sum(dy * xh, axis=0, keepdims=True)


def _one_minus_exp(x):
    series = -x * (1.0 + x * (0.5 + x * (1.0 / 6.0 + x * (1.0 / 24.0))))
    return jnp.where(x > -0.01, series, 1.0 - jnp.exp(x))


def _cumsum_rows(x):
    row = lax.broadcasted_iota(jnp.int32, x.shape, 0)
    d = 1
    while d < x.shape[0]:
        x = x + jnp.where(row >= d, pltpu.roll(x, d, 0), 0.0)
        d *= 2
    return x


def _tile(t, cap):
    tm = min(cap, t)
    assert t % tm == 0
    return tm


def _after(body, n_in, deps):
    def wrapped(*refs):
        return body(*refs[:n_in], *refs[n_in + len(deps):])
    return wrapped


def _lspec(a, l):
    return pl.BlockSpec((None,) + a.shape[1:], lambda *_: (l,) + (0,) * (a.ndim - 1))


def _wd_rows(wd_ref):
    return wd_ref[:, 0].reshape(2 * wd_ref.shape[2], wd_ref.shape[3])


def _ffn_fwd(x, pre_g, post_g, wgu, wd, l, deps=()):
    t, d = x.shape
    nb, _, _, h = wgu.shape
    nj = nb // 2
    tm = _tile(t, 512)

    def body(x_ref, pg_ref, qg_ref, wg_ref, wu_ref, wd_ref, y_ref, hb_ref, g_ref, u_ref, f_ref, acc_ref):
        j = pl.program_id(1)

        @pl.when(j == 0)
        def _():
            hb_ref[...] = _rms_fwd(x_ref[...], pg_ref[...]).astype(bf16)
            acc_ref[...] = jnp.zeros_like(acc_ref)

        hb = hb_ref[...]
        g = _nn(hb, wg_ref[0, 0])
        u = _nn(hb, wu_ref[0, 0])
        g_ref[0] = g.astype(bf16)
        u_ref[0] = u.astype(bf16)
        a = (g * _sigmoid(g) * u).astype(bf16)
        acc_ref[...] += _nn(a, _wd_rows(wd_ref))

        @pl.when(j == nj - 1)
        def _():
            f = acc_ref[...]
            f_ref[...] = f
            y_ref[...] = x_ref[...] + 0.5 * _rms_fwd(f, qg_ref[...])

    row = pl.BlockSpec((tm, d), lambda i, j: (i, 0))
    vec = pl.BlockSpec((1, d), lambda i, j: (0, 0))
    act = pl.BlockSpec((1, tm, h), lambda i, j: (j, i, 0))
    return pl.pallas_call(
        _after(body, 6, deps), name="ffn_fwd", grid=(t // tm, nj),
        in_specs=[row, _lspec(pre_g, l), _lspec(post_g, l),
                  pl.BlockSpec((1, 1, d, h), lambda i, j: (j, 0, 0, 0)),
                  pl.BlockSpec((1, 1, d, h), lambda i, j: (j + nj, 0, 0, 0)),
                  pl.BlockSpec((2, 1, h // 2, d), lambda i, j: (j, 0, 0, 0))] + [ANY] * len(deps),
        out_specs=[row, row, act, act, row],
        out_shape=[jax.ShapeDtypeStruct((t, d), f32), jax.ShapeDtypeStruct((t, d), bf16),
                   jax.ShapeDtypeStruct((nj, t, h), bf16), jax.ShapeDtypeStruct((nj, t, h), bf16),
                   jax.ShapeDtypeStruct((t, d), f32)],
        scratch_shapes=[pltpu.VMEM((tm, d), f32)],
        compiler_params=_params(("arbitrary", "arbitrary")),
    )(x, pre_g, post_g, wgu, wgu, wd, *deps)


def _ffn_bwd(x, dy, f, pre_g, post_g, g, u, wgu, wd, l, deps=()):
    t, d = x.shape
    nj, _, h = g.shape
    tm = _tile(t, 512)

    def body(x_ref, dy_ref, f_ref, pg_ref, qg_ref, g_ref, u_ref, wg_ref, wu_ref, wd_ref,
             dx_ref, dfb_ref, a_ref, dg_ref, du_ref, dpg_ref, dqg_ref, dh_ref):
        i, j = pl.program_id(0), pl.program_id(1)

        @pl.when((i == 0) & (j == 0))
        def _():
            dpg_ref[...] = jnp.zeros_like(dpg_ref)
            dqg_ref[...] = jnp.zeros_like(dqg_ref)

        @pl.when(j == 0)
        def _():
            df, dq = _rms_bwd(f_ref[...], qg_ref[...], 0.5 * dy_ref[...])
            dfb_ref[...] = df.astype(bf16)
            dqg_ref[...] += dq
            dh_ref[...] = jnp.zeros_like(dh_ref)

        wdm, wg, wu = _wd_rows(wd_ref), wg_ref[0, 0], wu_ref[0, 0]
        sub = tm // FFN_BWD_SPLIT
        das = [_nt(dfb_ref[pl.ds(half * sub, sub), :], wdm) for half in range(FFN_BWD_SPLIT)]
        for half in range(FFN_BWD_SPLIT):
            rows = pl.ds(half * sub, sub)
            da = das[half]
            gv = g_ref[0, rows, :].astype(f32)
            uv = u_ref[0, rows, :].astype(f32)
            s = _sigmoid(gv)
            sg = gv * s
            a_ref[0, rows, :] = (sg * uv).astype(bf16)
            dg = (da * uv * _silu_grad(gv, s)).astype(bf16)
            du = (da * sg).astype(bf16)
            dg_ref[0, rows, :] = dg
            du_ref[0, rows, :] = du
            dh_ref[rows, :] += _nt(dg, wg) + _nt(du, wu)

        @pl.when(j == nj - 1)
        def _():
            dxn, dp = _rms_bwd(x_ref[...], pg_ref[...], dh_ref[...])
            dx_ref[...] = dy_ref[...] + dxn
            dpg_ref[...] += dp

    row = pl.BlockSpec((tm, d), lambda i, j: (i, 0))
    vec = pl.BlockSpec((1, d), lambda i, j: (0, 0))
    act = pl.BlockSpec((1, tm, h), lambda i, j: (j, i, 0))
    act_shape = jax.ShapeDtypeStruct((nj, t, h), bf16)
    return pl.pallas_call(
        _after(body, 10, deps), name="ffn_bwd", grid=(t // tm, nj),
        in_specs=[row, row, row, _lspec(pre_g, l), _lspec(post_g, l), act, act,
                  pl.BlockSpec((1, 1, d, h), lambda i, j: (j, 0, 0, 0)),
                  pl.BlockSpec((1, 1, d, h), lambda i, j: (j + nj, 0, 0, 0)),
                  pl.BlockSpec((2, 1, h // 2, d), lambda i, j: (j, 0, 0, 0))] + [ANY] * len(deps),
        out_specs=[row, row, act, act, act, vec, vec],
        out_shape=[jax.ShapeDtypeStruct((t, d), f32), jax.ShapeDtypeStruct((t, d), bf16),
                   act_shape, act_shape, act_shape,
                   jax.ShapeDtypeStruct((1, d), f32), jax.ShapeDtypeStruct((1, d), f32)],
        scratch_shapes=[pltpu.VMEM((tm, d), f32)],
        compiler_params=_params(("arbitrary", "arbitrary")),
    )(x, dy, f, pre_g, post_g, g, u, wgu, wgu, wd, *deps)


def _wgrad_cols(x, dy, buf, l, slot0):
    (t, k), (nj, _, n) = x.shape, dy.shape

    def body(x_ref, dy_ref, buf_ref, o_ref):
        o_ref[0, 0] = _tn(dy_ref[0], x_ref[...]).astype(bf16)

    return pl.pallas_call(
        body, name="wgrad_cols", grid=(nj,),
        in_specs=[pl.BlockSpec((t, k), lambda b: (0, 0)), pl.BlockSpec((1, t, n), lambda b: (b, 0, 0)), ANY],
        out_specs=pl.BlockSpec((1, 1, n, k), lambda b: (b + slot0, l, 0, 0)),
        out_shape=jax.ShapeDtypeStruct(buf.shape, bf16), input_output_aliases={2: 0},
        compiler_params=_params(("arbitrary",)),
    )(x, dy, buf)


def _wgrad_rows(x, dy, buf, l):
    (nj, t, k), (_, n) = x.shape, dy.shape

    def body(x_ref, dy_ref, buf_ref, o_ref):
        o_ref[:, 0] = _tn(x_ref[0], dy_ref[...]).astype(bf16).reshape(2, k // 2, n)

    return pl.pallas_call(
        body, name="wgrad_rows", grid=(nj,),
        in_specs=[pl.BlockSpec((1, t, k), lambda b: (b, 0, 0)), pl.BlockSpec((t, n), lambda b: (0, 0)), ANY],
        out_specs=pl.BlockSpec((2, 1, k // 2, n), lambda b: (b, l, 0, 0)),
        out_shape=jax.ShapeDtypeStruct(buf.shape, bf16), input_output_aliases={2: 0},
        compiler_params=_params(("arbitrary",)),
    )(x, dy, buf)


def _wgrad_kblocks(x, dys, buf, l):
    t, k = x.shape
    kb = k // N_DEV
    widths = [dy.shape[1] for dy in dys]
    n = sum(widths)
    nd = len(dys)

    def body(x_ref, *refs):
        dy_hbm, o_ref, dy_vmem = refs[:nd], refs[nd + 1], refs[nd + 2:]

        @pl.when(pl.program_id(0) == 0)
        def _():
            for src, dst in zip(dy_hbm, dy_vmem):
                pltpu.sync_copy(src, dst)

        off = 0
        for dst, w in zip(dy_vmem, widths):
            o_ref[0, 0, :, off:off + w] = _tn(x_ref[...], dst[...]).astype(bf16)
            off += w

    return pl.pallas_call(
        body, name="wgrad_kblocks", grid=(N_DEV,),
        in_specs=[pl.BlockSpec((t, kb), lambda s: (0, s))] + [ANY] * (nd + 1),
        out_specs=pl.BlockSpec((1, 1, kb, n), lambda s: (s, l, 0, 0)),
        out_shape=jax.ShapeDtypeStruct(buf.shape, bf16), input_output_aliases={nd + 1: 0},
        scratch_shapes=[pltpu.VMEM((t, w), bf16) for w in widths],
        compiler_params=_params(("arbitrary",)),
    )(x, *dys, buf)


def _gathered_rows(w_ref, lo, hi):
    return w_ref[:, 0, :, lo:hi].reshape(N_DEV * w_ref.shape[2], hi - lo)


def _gathered_spec(w):
    return pl.BlockSpec((N_DEV, 1) + w.shape[2:], lambda i: (0, 0, 0, 0))


def _mix_in_fwd(x, pre_g, w_in, l):
    t, d = x.shape
    tm = _tile(t, 512)

    def body(x_ref, g_ref, w_ref, hb_ref, pa_ref, pb_ref, pc_ref):
        hb = _rms_fwd(x_ref[...], g_ref[...]).astype(bf16)
        hb_ref[...] = hb
        pa_ref[...] = _nn(hb, _gathered_rows(w_ref, 0, PA_W))
        pb_ref[...] = _nn(hb, _gathered_rows(w_ref, PA_W, PA_W + PB_W))
        pc_ref[...] = _nn(hb, _gathered_rows(w_ref, PA_W + PB_W, IN_PAD))

    def row(w):
        return pl.BlockSpec((tm, w), lambda i: (i, 0))

    return pl.pallas_call(
        body, name="mix_in_fwd", grid=(t // tm,),
        in_specs=[row(d), _lspec(pre_g, l), _gathered_spec(w_in)],
        out_specs=[row(d), row(PA_W), row(PB_W), row(PC_W)],
        out_shape=[jax.ShapeDtypeStruct((t, d), bf16), jax.ShapeDtypeStruct((t, PA_W), f32),
                   jax.ShapeDtypeStruct((t, PB_W), f32), jax.ShapeDtypeStruct((t, PC_W), f32)],
        compiler_params=_params(("arbitrary",)),
    )(x, pre_g, w_in)


def _mix_in_bwd(x, dy, pre_g, dpa, dpb, dpc, w_in, l):
    t, d = x.shape
    tm = _tile(t, 512)

    def body(x_ref, dy_ref, g_ref, dpa_ref, dpb_ref, dpc_ref, w_ref, dx_ref, dg_ref):
        @pl.when(pl.program_id(0) == 0)
        def _():
            dg_ref[...] = jnp.zeros_like(dg_ref)

        wa, wb, wc = (_gathered_rows(w_ref, 0, PA_W), _gathered_rows(w_ref, PA_W, PA_W + PB_W),
                      _gathered_rows(w_ref, PA_W + PB_W, IN_PAD))
        halves = [pl.ds(k * (tm // 2), tm // 2) for k in range(2)]
        dhs = [_nt(dpa_ref[rows, :], wa) + _nt(dpb_ref[rows, :], wb) + _nt(dpc_ref[rows, :], wc) for rows in halves]
        for rows, dh in zip(halves, dhs):
            dxn, dg = _rms_bwd(x_ref[rows, :], g_ref[...], dh)
            dx_ref[rows, :] = dy_ref[rows, :] + dxn
            dg_ref[...] += dg

    def row(w):
        return pl.BlockSpec((tm, w), lambda i: (i, 0))

    vec = pl.BlockSpec((1, d), lambda i: (0, 0))
    return pl.pallas_call(
        body, name="mix_in_bwd", grid=(t // tm,),
        in_specs=[row(d), row(d), _lspec(pre_g, l), row(PA_W), row(PB_W), row(PC_W), _gathered_spec(w_in)],
        out_specs=[row(d), vec],
        out_shape=[jax.ShapeDtypeStruct((t, d), f32), jax.ShapeDtypeStruct((1, d), f32)],
        compiler_params=_params(("arbitrary",)),
    )(x, dy, pre_g, dpa, dpb, dpc, w_in)


def _mix_out_fwd(x, ya, yb, yc, post_g, w_out, l):
    t, d = x.shape
    tm = _tile(t, 512)

    def body(x_ref, ya_ref, yb_ref, yc_ref, g_ref, w_ref, y_ref, cat_ref, m_ref):
        cat_ref[:, 0:A_W] = ya_ref[...].astype(bf16)
        cat_ref[:, A_W:A_W + B_W] = yb_ref[...].astype(bf16)
        cat_ref[:, A_W + B_W:d] = yc_ref[...].astype(bf16)
        m = _nn(cat_ref[...], _gathered_rows(w_ref, 0, d))
        m_ref[...] = m
        y_ref[...] = x_ref[...] + _rms_fwd(m, g_ref[...])

    def row(w):
        return pl.BlockSpec((tm, w), lambda i: (i, 0))

    return pl.pallas_call(
        body, name="mix_out_fwd", grid=(t // tm,),
        in_specs=[row(d), row(A_W), row(B_W), row(C_W), _lspec(post_g, l), _gathered_spec(w_out)],
        out_specs=[row(d), row(d), row(d)],
        out_shape=[jax.ShapeDtypeStruct((t, d), f32), jax.ShapeDtypeStruct((t, d), bf16), jax.ShapeDtypeStruct((t, d), f32)],
        compiler_params=_params(("arbitrary",)),
    )(x, ya, yb, yc, post_g, w_out)


def _mix_out_bwd(dy, m, post_g, w_out, l, deps=()):
    t, d = m.shape
    tm = _tile(t, 512)

    def body(dy_ref, m_ref, g_ref, w_ref, dm_ref, dya_ref, dyb_ref, dyc_ref, dg_ref):
        @pl.when(pl.program_id(0) == 0)
        def _():
            dg_ref[...] = jnp.zeros_like(dg_ref)

        dm, dg = _rms_bwd(m_ref[...], g_ref[...], dy_ref[...])
        dmb = dm.astype(bf16)
        dm_ref[...] = dmb
        dg_ref[...] += dg
        dcat = _nt(dmb, _gathered_rows(w_ref, 0, d))
        dya_ref[...] = dcat[:, 0:A_W]
        dyb_ref[...] = dcat[:, A_W:A_W + B_W]
        dyc_ref[...] = dcat[:, A_W + B_W:d]

    def row(w):
        return pl.BlockSpec((tm, w), lambda i: (i, 0))

    vec = pl.BlockSpec((1, d), lambda i: (0, 0))
    return pl.pallas_call(
        _after(body, 4, deps), name="mix_out_bwd", grid=(t // tm,),
        in_specs=[row(d), row(d), _lspec(post_g, l), _gathered_spec(w_out)] + [ANY] * len(deps),
        out_specs=[row(d), row(A_W), row(B_W), row(C_W), vec],
        out_shape=[jax.ShapeDtypeStruct((t, d), bf16), jax.ShapeDtypeStruct((t, A_W), f32),
                   jax.ShapeDtypeStruct((t, B_W), f32), jax.ShapeDtypeStruct((t, C_W), f32),
                   jax.ShapeDtypeStruct((1, d), f32)],
        compiler_params=_params(("arbitrary",)),
    )(dy, m, post_g, w_out, *deps)


def _conv_fwd(buf_ref, halo, x, w, b, n):
    buf_ref[0:8, :] = halo
    buf_ref[8:8 + n, :] = x
    out = b + w[3:4, :] * x
    for k in range(3):
        out = out + w[k:k + 1, :] * buf_ref[pl.ds(5 + k, n), :]
    return out


def _conv_bwd(buf_ref, dbuf_ref, dout, dnext, w, n):
    dbuf_ref[0:n, :] = dout
    dbuf_ref[n:n + 8, :] = dnext
    dx = w[3:4, :] * dout
    dws = []
    for k in range(3):
        dx = dx + w[k:k + 1, :] * dbuf_ref[pl.ds(3 - k, n), :]
        dws.append(jnp.sum(dout * buf_ref[pl.ds(5 + k, n), :], axis=0, keepdims=True))
    dws.append(jnp.sum(dout * buf_ref[pl.ds(8, n), :], axis=0, keepdims=True))
    return dx, jnp.concatenate(dws, axis=0), jnp.sum(dout, axis=0, keepdims=True)


def _lru_gates(rec, wr, wi, br, bi, lam):
    rb = rec.astype(bf16)
    r = _sigmoid(_nn(rb, wr) + br)
    ig = _sigmoid(_nn(rb, wi) + bi)
    sp = _softplus(-lam)
    la = -LRU_C * r * sp
    a = jnp.exp(la)
    mult = jnp.sqrt(_one_minus_exp(2.0 * la))
    return rb, r, ig, sp, a, mult


def _scan_rows(a_ref, b_ref, o_ref, carry, n, reverse):
    row = lax.broadcasted_iota(jnp.int32, (8, a_ref.shape[1]), 0)
    nb = n // 8

    def step(k, carry):
        blk = (nb - 1 - k) if reverse else k
        rows = pl.ds(pl.multiple_of(blk * 8, 8), 8)
        a, b = a_ref[rows, :], b_ref[rows, :]
        for d in (1, 2, 4):
            shift = 8 - d if reverse else d
            keep = (row < 8 - d) if reverse else (row >= d)
            b = a * jnp.where(keep, pltpu.roll(b, shift, 0), 0.0) + b
            a = a * jnp.where(keep, pltpu.roll(a, shift, 0), 1.0)
        o = a * carry + b
        o_ref[rows, :] = o
        return o[0:1, :] if reverse else o[7:8, :]

    return lax.fori_loop(0, nb, step, carry, unroll=2)


N_GATES = 5
LRU_SUB = 128


def _lru_fwd(pa, conv_w, conv_b, wr, wi, br, bi, lam, l):
    t = pa.shape[0]
    tc = _tile(t, 512)

    def body(pa_ref, halo_ref, cw_ref, cb_ref, wr_ref, wi_ref, br_ref, bi_ref, lam_ref,
             ya_ref, h_ref, gates_ref, buf_ref, u_ref, carry_ref):
        i = pl.program_id(0)

        @pl.when(i == 0)
        def _():
            carry_ref[...] = jnp.zeros_like(carry_ref)

        halo = jnp.where(i > 0, halo_ref[:, A_W:PA_W], 0.0)
        rec = _conv_fwd(buf_ref, halo, pa_ref[:, A_W:PA_W], cw_ref[...], cb_ref[...], tc)
        _, r, ig, _, a, mult = _lru_gates(rec, wr_ref[...], wi_ref[...], br_ref[...], bi_ref[...], lam_ref[...])
        for k, val in enumerate((rec, r, ig, a, mult)):
            gates_ref[k] = val
        u_ref[...] = mult * (ig * rec)

        carry_ref[...] = _scan_rows(gates_ref.at[3], u_ref, h_ref, carry_ref[...], tc, reverse=False)
        ya_ref[...] = h_ref[...] * _gelu(pa_ref[:, 0:A_W])

    vec = pl.BlockSpec((1, A_W), lambda i: (0, 0))
    mat = pl.BlockSpec((A_W, A_W), lambda i: (0, 0))
    row = pl.BlockSpec((tc, A_W), lambda i: (i, 0))
    return pl.pallas_call(
        body, name="lru_fwd", grid=(t // tc,),
        in_specs=[pl.BlockSpec((tc, PA_W), lambda i: (i, 0)),
                  pl.BlockSpec((8, PA_W), lambda i: (jnp.maximum(i * (tc // 8) - 1, 0), 0)),
                  *[_lspec(a, l) for a in (conv_w, conv_b, wr, wi, br, bi, lam)]],
        out_specs=[row, row, pl.BlockSpec((N_GATES, tc, A_W), lambda i: (0, i, 0))],
        out_shape=[jax.ShapeDtypeStruct((t, A_W), f32), jax.ShapeDtypeStruct((t, A_W), f32),
                   jax.ShapeDtypeStruct((N_GATES, t, A_W), f32)],
        scratch_shapes=[pltpu.VMEM((8 + tc, A_W), f32), pltpu.VMEM((tc, A_W), f32), pltpu.VMEM((1, A_W), f32)],
        compiler_params=_params(("arbitrary",)),
    )(pa, pa, conv_w, conv_b, wr, wi, br, bi, lam)


def _lru_bwd(pa, h, gates, dya, conv_w, conv_b, wr, wi, br, bi, lam, l, deps=()):
    t = pa.shape[0]
    tc = _tile(t, 512)
    nc = t // tc

    def body(pa_ref, halo_ref, h_ref, hhalo_ref, gates_ref, dya_ref, cw_ref, cb_ref, wr_ref, wi_ref, br_ref, bi_ref,
             lam_ref, dpa_ref, dcw_ref, dcb_ref, dwr_ref, dwi_ref, dbr_ref, dbi_ref, dlam_ref,
             buf_ref, dbuf_ref, hbuf_ref, g_ref, dh_ref, carry_ref, dnext_ref, dhbuf_ref, gg_ref):
        i = pl.program_id(0)
        c = nc - 1 - i

        @pl.when(i == 0)
        def _():
            carry_ref[...] = jnp.zeros_like(carry_ref)
            dnext_ref[...] = jnp.zeros_like(dnext_ref)
            for ref in (dcw_ref, dcb_ref, dwr_ref, dwi_ref, dbr_ref, dbi_ref, dlam_ref):
                ref[...] = jnp.zeros_like(ref)

        halo = jnp.where(c > 0, halo_ref[:, A_W:PA_W], 0.0)
        cw = cw_ref[...]
        buf_ref[0:8, :] = halo
        buf_ref[8:8 + tc, :] = pa_ref[:, A_W:PA_W]
        lam = lam_ref[...]
        sp = _softplus(-lam)
        hbuf_ref[0:8, :] = jnp.where(c > 0, hhalo_ref[...], 0.0)
        hbuf_ref[8:8 + tc, :] = h_ref[...]
        gate = pa_ref[:, 0:A_W]
        dya = dya_ref[...]
        dpa_ref[:, 0:A_W] = (dya * h_ref[...] * _gelu_grad(gate)).astype(bf16)
        gg = dya * _gelu(gate)
        gg_ref[...] = gg
        g_ref[...] = gates_ref[3] * gg
        carry_in = carry_ref[...]
        carry_ref[...] = _scan_rows(gates_ref.at[3], g_ref, dh_ref, carry_in, tc, reverse=True)
        dhbuf_ref[0:tc, :] = dh_ref[...]
        dhbuf_ref[tc:tc + 8, :] = jnp.broadcast_to(carry_in, (8, A_W))
        for sb in range(tc // LRU_SUB):
            lo = sb * LRU_SUB
            rows = pl.ds(lo, LRU_SUB)
            rec, r, ig, a, mult = (gates_ref[k, rows, :] for k in range(N_GATES))
            rb = rec.astype(bf16)
            dh = gg_ref[rows, :] + dhbuf_ref[pl.ds(lo + 1, LRU_SUB), :]
            da = dh * hbuf_ref[pl.ds(lo + 7, LRU_SUB), :]
            dmult = dh * ig * rec
            dig = dh * mult * rec
            dla = da * a - dmult * (a * a) / mult
            dr = dla * (-LRU_C * sp)
            dlam_ref[...] += jnp.sum(dla * (-LRU_C * r), axis=0, keepdims=True) * (-_sigmoid(-lam))
            dpr = (dr * r * (1.0 - r))
            dpi = (dig * ig * (1.0 - ig))
            dprb, dpib = dpr.astype(bf16), dpi.astype(bf16)
            g_ref[rows, :] = dh * mult * ig + _nt(dprb, wr_ref[...]) + _nt(dpib, wi_ref[...])
            dwr_ref[...] += _tn(rb, dprb)
            dwi_ref[...] += _tn(rb, dpib)
            dbr_ref[...] += jnp.sum(dpr, axis=0, keepdims=True)
            dbi_ref[...] += jnp.sum(dpi, axis=0, keepdims=True)
        drec = g_ref[...]
        dx, dw, db = _conv_bwd(buf_ref, dbuf_ref, drec, dnext_ref[...], cw, tc)
        dnext_ref[...] = drec[0:8, :]
        dcw_ref[...] += dw
        dcb_ref[...] += db
        dpa_ref[:, A_W:PA_W] = dx.astype(bf16)

    vec = pl.BlockSpec((1, A_W), lambda i: (0, 0))
    mat = pl.BlockSpec((A_W, A_W), lambda i: (0, 0))
    cwspec = pl.BlockSpec((4, A_W), lambda i: (0, 0))

    def rev(w):
        return pl.BlockSpec((tc, w), lambda i: (nc - 1 - i, 0))

    def halo(w):
        return pl.BlockSpec((8, w), lambda i: (jnp.maximum((nc - 1 - i) * (tc // 8) - 1, 0), 0))

    chunk = pltpu.VMEM((tc, A_W), f32)
    return pl.pallas_call(
        _after(body, 13, deps), name="lru_bwd", grid=(nc,),
        in_specs=[rev(PA_W), halo(PA_W), rev(A_W), halo(A_W),
                  pl.BlockSpec((N_GATES, tc, A_W), lambda i: (0, nc - 1 - i, 0)), rev(A_W),
                  *[_lspec(a, l) for a in (conv_w, conv_b, wr, wi, br, bi, lam)]] + [ANY] * len(deps),
        out_specs=[rev(PA_W), cwspec, vec, mat, mat, vec, vec, vec],
        out_shape=[jax.ShapeDtypeStruct((t, PA_W), bf16), jax.ShapeDtypeStruct((4, A_W), f32),
                   jax.ShapeDtypeStruct((1, A_W), f32), jax.ShapeDtypeStruct((A_W, A_W), f32),
                   jax.ShapeDtypeStruct((A_W, A_W), f32), jax.ShapeDtypeStruct((1, A_W), f32),
                   jax.ShapeDtypeStruct((1, A_W), f32), jax.ShapeDtypeStruct((1, A_W), f32)],
        scratch_shapes=[pltpu.VMEM((8 + tc, A_W), f32), pltpu.VMEM((tc + 8, A_W), f32), pltpu.VMEM((8 + tc, A_W), f32),
                        chunk, chunk, pltpu.VMEM((1, A_W), f32), pltpu.VMEM((8, A_W), f32),
                        pltpu.VMEM((tc + 8, A_W), f32), chunk],
        compiler_params=_params(("arbitrary",)),
    )(pa, pa, h, h, gates, dya, conv_w, conv_b, wr, wi, br, bi, lam, *deps)


def _sgu_norm(v, g, b):
    mu = jnp.mean(v, axis=-1, keepdims=True)
    vc = v - mu
    rstd = lax.rsqrt(jnp.mean(vc * vc, axis=-1, keepdims=True) + NORM_EPS)
    vh = vc * rstd
    return vh, rstd, vh * g + b


def _sgu_mix(w_ref, vb, bias):
    grp = lax.broadcasted_iota(jnp.int32, (CHUNK, C_W), 1) // HEAD
    out = bias
    for gi in range(C_W // HEAD):
        out = out + jnp.where(grp == gi, _nn(w_ref[gi], vb), 0.0)
    return out


def _sgu_fwd(pc, ln_g, ln_b, wm, bias, l):
    t = pc.shape[0]
    tm = _tile(t, 512)

    def body(pc_ref, g_ref, b_ref, w_ref, bias_ref, yc_ref):
        for ci in range(tm // CHUNK):
            rows = pl.ds(ci * CHUNK, CHUNK)
            ge = _gelu(pc_ref[rows, :])
            _, _, vn = _sgu_norm(ge[:, C_W:PC_W], g_ref[...], b_ref[...])
            yc_ref[rows, :] = ge[:, 0:C_W] * _sgu_mix(w_ref, vn.astype(bf16), bias_ref[...])

    vec = pl.BlockSpec((1, C_W), lambda i: (0, 0))
    return pl.pallas_call(
        body, name="sgu_fwd", grid=(t // tm,),
        in_specs=[pl.BlockSpec((tm, PC_W), lambda i: (i, 0)), *[_lspec(a, l) for a in (ln_g, ln_b, wm, bias)]],
        out_specs=pl.BlockSpec((tm, C_W), lambda i: (i, 0)),
        out_shape=jax.ShapeDtypeStruct((t, C_W), f32),
        compiler_params=_params(("arbitrary",)),
    )(pc, ln_g, ln_b, wm, bias)


def _sgu_bwd(pc, dyc, ln_g, ln_b, wm, wmt, bias, l, deps=()):
    t = pc.shape[0]
    tm = _tile(t, 512)

    def body(pc_ref, dyc_ref, g_ref, b_ref, w_ref, wt_ref, bias_ref, dpc_ref, dw_ref, dbias_ref, dg_ref, db_ref):
        @pl.when(pl.program_id(0) == 0)
        def _():
            for ref in (dw_ref, dbias_ref, dg_ref, db_ref):
                ref[...] = jnp.zeros_like(ref)

        grp = lax.broadcasted_iota(jnp.int32, (CHUNK, C_W), 1) // HEAD
        for ci in range(tm // CHUNK):
            rows = pl.ds(ci * CHUNK, CHUNK)
            x = pc_ref[rows, :]
            ge = _gelu(x)
            gv = g_ref[...]
            vh, rstd, vn = _sgu_norm(ge[:, C_W:PC_W], gv, b_ref[...])
            vb = vn.astype(bf16)
            mixed = _sgu_mix(w_ref, vb, bias_ref[...])
            dyc = dyc_ref[rows, :]
            du = dyc * mixed
            dmix = dyc * ge[:, 0:C_W]
            dmb = dmix.astype(bf16)
            dvn = jnp.zeros((CHUNK, C_W), f32)
            for gi in range(C_W // HEAD):
                dvn = dvn + jnp.where(grp == gi, _nn(wt_ref[gi], dmb), 0.0)
                dw_ref[gi] += _nt(jnp.where(grp == gi, dmix, 0.0).astype(bf16), vb)
            dbias_ref[...] += dmix
            dg_ref[...] += jnp.sum(dvn * vh, axis=0, keepdims=True)
            db_ref[...] += jnp.sum(dvn, axis=0, keepdims=True)
            dvh = dvn * gv
            dv = rstd * (dvh - jnp.mean(dvh, axis=-1, keepdims=True) - vh * jnp.mean(dvh * vh, axis=-1, keepdims=True))
            gg = _gelu_grad(x)
            dpc_ref[rows, 0:C_W] = (du * gg[:, 0:C_W]).astype(bf16)
            dpc_ref[rows, C_W:PC_W] = (dv * gg[:, C_W:PC_W]).astype(bf16)

    vec = pl.BlockSpec((1, C_W), lambda i: (0, 0))
    wspec = pl.BlockSpec((4, CHUNK, CHUNK), lambda i: (0, 0, 0))
    bspec = pl.BlockSpec((CHUNK, C_W), lambda i: (0, 0))
    return pl.pallas_call(
        _after(body, 7, deps), name="sgu_bwd", grid=(t // tm,),
        in_specs=[pl.BlockSpec((tm, PC_W), lambda i: (i, 0)), pl.BlockSpec((tm, C_W), lambda i: (i, 0)),
                  *[_lspec(a, l) for a in (ln_g, ln_b, wm, wmt, bias)]] + [ANY] * len(deps),
        out_specs=[pl.BlockSpec((tm, PC_W), lambda i: (i, 0)), wspec, bspec, vec, vec],
        out_shape=[jax.ShapeDtypeStruct((t, PC_W), bf16), jax.ShapeDtypeStruct((4, CHUNK, CHUNK), f32),
                   jax.ShapeDtypeStruct((CHUNK, C_W), f32), jax.ShapeDtypeStruct((1, C_W), f32),
                   jax.ShapeDtypeStruct((1, C_W), f32)],
        compiler_params=_params(("arbitrary",)),
    )(pc, dyc, ln_g, ln_b, wm, wmt, bias, *deps)


N_PAIR = B_W // 128
HEADS_PER_GROUP = 3


def _pair_groups(p):
    return (2 * p) // HEADS_PER_GROUP, (2 * p + 1) // HEADS_PER_GROUP


def _ssd_chunk(pb_ref, halo, buf_ref, cw, cb, dtb, alog):
    z = pb_ref[:, 0:B_W]
    pre = _conv_fwd(buf_ref, halo, pb_ref[:, B_W:B_W + XBC_W], cw, cb, CHUNK)
    sg = _sigmoid(pre)
    xbc = pre * sg
    xs = xbc[:, 0:B_W]
    bm = [xbc[:, B_W + k * B_STATE:B_W + (k + 1) * B_STATE] for k in range(2)]
    cm = [xbc[:, B_W + (2 + k) * B_STATE:B_W + (3 + k) * B_STATE] for k in range(2)]
    dtin = pb_ref[:, B_W + XBC_W:PB_W] + dtb
    dt = _softplus(dtin)
    a = -jnp.exp(alog)
    cs = _cumsum_rows(dt * a)
    return dict(z=z, pre=pre, sg=sg, xs=xs, bm=bm, cm=cm, dtin=dtin, dt=dt, a=a, cs=cs,
                ecs=jnp.exp(cs), ds=jnp.exp(cs[CHUNK - 1:CHUNK, :] - cs), xdt=xs * dt,
                bmb=[v.astype(bf16) for v in bm], cmb=[v.astype(bf16) for v in cm])


def _ssd_decay(cs_pair, half):
    cst = cs_pair.T
    lane0 = HEAD * half
    csc = jnp.broadcast_to(cs_pair[:, lane0:lane0 + 1], (CHUNK, CHUNK))
    csr = cst[lane0:lane0 + 1, :]
    tri = lax.broadcasted_iota(jnp.int32, (CHUNK, CHUNK), 0) >= lax.broadcasted_iota(jnp.int32, (CHUNK, CHUNK), 1)
    return jnp.exp(jnp.where(tri, csc - csr, NEG_BIG)), cst


def _ssd_fwd(pb, conv_w, conv_b, dtb, alog, dskip, norm_g, l):
    t = pb.shape[0]
    nc = t // CHUNK

    def body(pb_ref, halo_ref, cw_ref, cb_ref, dtb_ref, alog_ref, d_ref, ng_ref, yb_ref, yp_ref, sp_ref, buf_ref, s_ref):
        i = pl.program_id(0)

        @pl.when(i == 0)
        def _():
            s_ref[...] = jnp.zeros_like(s_ref)

        halo = jnp.where(i > 0, halo_ref[:, B_W:B_W + XBC_W], 0.0)
        q = _ssd_chunk(pb_ref, halo, buf_ref, cw_ref[...], cb_ref[...], dtb_ref[...], alog_ref[...])
        sp_ref[0] = s_ref[...]
        lane = lax.broadcasted_iota(jnp.int32, (CHUNK, 128), 1)
        rowi = lax.broadcasted_iota(jnp.int32, (128, B_STATE), 0)
        cb_mat = [_nt(q["cmb"][k], q["bmb"][k]) for k in range(2)]
        xd = q["xdt"] * q["ds"]
        for p in range(N_PAIR):
            cols = slice(128 * p, 128 * (p + 1))
            g_lo, g_hi = _pair_groups(p)
            cs_p, xdt_p = q["cs"][:, cols], q["xdt"][:, cols]
            s_p = s_ref[cols, :]
            s_pb = s_p.astype(bf16)
            y_p = jnp.zeros((CHUNK, 128), f32)
            for half, grp in ((0, g_lo), (1, g_hi)):
                lm, cst = _ssd_decay(cs_p, half)
                mb = (cb_mat[grp] * lm).astype(bf16)
                sel = (lane < HEAD) if half == 0 else (lane >= HEAD)
                y_p = y_p + _nn(mb, jnp.where(sel, xdt_p, 0.0).astype(bf16))
            off_lo = _nt(q["cmb"][g_lo], s_pb)
            off = off_lo if g_lo == g_hi else jnp.where(lane < HEAD, off_lo, _nt(q["cmb"][g_hi], s_pb))
            y_p = y_p + off * q["ecs"][:, cols] + q["xs"][:, cols] * d_ref[:, cols]
            yp_ref[:, cols] = y_p
            xd_pb = xd[:, cols].astype(bf16)
            upd_lo = _tn(xd_pb, q["bmb"][g_lo])
            upd = upd_lo if g_lo == g_hi else jnp.where(rowi < HEAD, upd_lo, _tn(xd_pb, q["bmb"][g_hi]))
            cd = jnp.exp(jnp.broadcast_to(cst[:, CHUNK - 1:CHUNK], (128, B_STATE)))
            s_ref[cols, :] = cd * s_p + upd
        z = q["z"]
        yg = yp_ref[...] * (z * _sigmoid(z))
        yb_ref[...] = _rms_fwd(yg, ng_ref[...])

    vec = pl.BlockSpec((1, B_W), lambda i: (0, 0))
    row = pl.BlockSpec((CHUNK, B_W), lambda i: (i, 0))
    return pl.pallas_call(
        body, name="ssd_fwd", grid=(nc,),
        in_specs=[pl.BlockSpec((CHUNK, PB_W), lambda i: (i, 0)),
                  pl.BlockSpec((8, PB_W), lambda i: (jnp.maximum(i * (CHUNK // 8) - 1, 0), 0)),
                  *[_lspec(a, l) for a in (conv_w, conv_b, dtb, alog, dskip, norm_g)]],
        out_specs=[row, row, pl.BlockSpec((1, B_W, B_STATE), lambda i: (i, 0, 0))],
        out_shape=[jax.ShapeDtypeStruct((t, B_W), f32), jax.ShapeDtypeStruct((t, B_W), f32),
                   jax.ShapeDtypeStruct((nc, B_W, B_STATE), f32)],
        scratch_shapes=[pltpu.VMEM((8 + CHUNK, XBC_W), f32), pltpu.VMEM((B_W, B_STATE), f32)],
        compiler_params=_params(("arbitrary",)),
    )(pb, pb, conv_w, conv_b, dtb, alog, dskip, norm_g)


def _ssd_bwd(pb, yp, sprev, dyb, conv_w, conv_b, dtb, alog, dskip, norm_g, l):
    t = pb.shape[0]
    nc = t // CHUNK

    def body(pb_ref, halo_ref, yp_ref, sp_ref, dyb_ref, cw_ref, cb_ref, dtb_ref, alog_ref, d_ref, ng_ref,
             dpb_ref, dcw_ref, dcb_ref, ddtb_ref, dalog_ref, dd_ref, dng_ref,
             buf_ref, dbuf_ref, ds_ref, dnext_ref, dxbc_ref, dcs_ref, dxdt_ref):
        i = pl.program_id(0)
        c = nc - 1 - i

        @pl.when(i == 0)
        def _():
            ds_ref[...] = jnp.zeros_like(ds_ref)
            dnext_ref[...] = jnp.zeros_like(dnext_ref)
            for ref in (dcw_ref, dcb_ref, ddtb_ref, dalog_ref, dd_ref, dng_ref):
                ref[...] = jnp.zeros_like(ref)

        halo = jnp.where(c > 0, halo_ref[:, B_W:B_W + XBC_W], 0.0)
        cw = cw_ref[...]
        q = _ssd_chunk(pb_ref, halo, buf_ref, cw, cb_ref[...], dtb_ref[...], alog_ref[...])
        z, xs, dt, a, ecs, dsd, xdt =q["z"], q["xs"], q["dt"], q["a"], q["ecs"], q["ds"], q["xdt"]
        sz = _sigmoid(z)
        siluz = z * sz
        yp = yp_ref[...]
        dyg, dng = _rms_bwd(yp * siluz, ng_ref[...], dyb_ref[...])
        dng_ref[...] += dng
        dy = dyg * siluz
        dpb_ref[:, 0:B_W] = (dyg * yp * _silu_grad(z, sz)).astype(bf16)
        dd_ref[...] += jnp.sum(dy * xs, axis=0, keepdims=True)
        g1 = dy * ecs
        lane = lax.broadcasted_iota(jnp.int32, (CHUNK, 128), 1)
        rowi = lax.broadcasted_iota(jnp.int32, (128, B_STATE), 0)
        rowc = lax.broadcasted_iota(jnp.int32, (CHUNK, 128), 0)
        cb_mat = [_nt(q["cmb"][k], q["bmb"][k]) for k in range(2)]
        d_cb = [jnp.zeros((CHUNK, CHUNK), f32) for _ in range(2)]
        d_b = [jnp.zeros((CHUNK, B_STATE), f32) for _ in range(2)]
        d_c = [jnp.zeros((CHUNK, B_STATE), f32) for _ in range(2)]
        for p in range(N_PAIR):
            cols = slice(128 * p, 128 * (p + 1))
            g_lo, g_hi = _pair_groups(p)
            lo, hi = lane < HEAD, lane >= HEAD
            cs_p, xdt_p, dy_p, ds_p, g1_p = q["cs"][:, cols], xdt[:, cols], dy[:, cols], dsd[:, cols], g1[:, cols]
            s_p = sp_ref[0, cols, :]
            s_pb = s_p.astype(bf16)
            dsn = ds_ref[cols, :]
            dsnb = dsn.astype(bf16)
            g1b = g1_p.astype(bf16)
            off_lo = _nt(q["cmb"][g_lo], s_pb)
            off = off_lo if g_lo == g_hi else jnp.where(lo, off_lo, _nt(q["cmb"][g_hi], s_pb))
            dcs_p = dy_p * off * ecs[:, cols]
            dsp_lo = _tn(g1b, q["cmb"][g_lo])
            dsp = dsp_lo if g_lo == g_hi else jnp.where(rowi < HEAD, dsp_lo, _tn(g1b, q["cmb"][g_hi]))
            dx_lo = _nt(q["bmb"][g_lo], dsnb)
            dxd = dx_lo if g_lo == g_hi else jnp.where(lo, dx_lo, _nt(q["bmb"][g_hi], dsnb))
            xd_p = xdt_p * ds_p
            if g_lo == g_hi:
                d_c[g_lo] = d_c[g_lo] + _nn(g1b, s_pb)
                d_b[g_lo] = d_b[g_lo] + _nn(xd_p.astype(bf16), dsnb)
            else:
                d_c[g_lo] = d_c[g_lo] + _nn(jnp.where(lo, g1_p, 0.0).astype(bf16), s_pb)
                d_c[g_hi] = d_c[g_hi] + _nn(jnp.where(hi, g1_p, 0.0).astype(bf16), s_pb)
                d_b[g_lo] = d_b[g_lo] + _nn(jnp.where(lo, xd_p, 0.0).astype(bf16), dsnb)
                d_b[g_hi] = d_b[g_hi] + _nn(jnp.where(hi, xd_p, 0.0).astype(bf16), dsnb)
            dxdt_p = dxd * ds_p
            t2 = dxd * xdt_p * ds_p
            dcs_p = dcs_p - t2
            dlast = jnp.sum(t2, axis=0, keepdims=True)
            cst = None
            for half, grp in ((0, g_lo), (1, g_hi)):
                sel = lo if half == 0 else hi
                lm, cst = _ssd_decay(cs_p, half)
                m = cb_mat[grp] * lm
                dyh = jnp.where(sel, dy_p, 0.0).astype(bf16)
                xdh = jnp.where(sel, xdt_p, 0.0).astype(bf16)
                dm = _nt(dyh, xdh)
                pm = dm * m
                col = jnp.sum(pm, axis=1, keepdims=True) - jnp.sum(pm.T, axis=1, keepdims=True)
                dcs_p = dcs_p + jnp.where(lane == HEAD * half, col, 0.0)
                d_cb[grp] = d_cb[grp] + dm * lm
                dxdt_p = dxdt_p + _tn(m.astype(bf16), dyh)
            cdcol = jnp.exp(jnp.broadcast_to(cst[:, CHUNK - 1:CHUNK], (128, B_STATE)))
            ds_ref[cols, :] = cdcol * dsn + dsp
            dcd_row = jnp.sum((dsn * s_p).T, axis=0, keepdims=True)
            dlast = dlast + dcd_row * ecs[CHUNK - 1:CHUNK, cols]
            dcs_ref[:, cols] = dcs_p + jnp.where(rowc == CHUNK - 1, dlast, 0.0)
            dxdt_ref[:, cols] = dxdt_p
        for k in range(2):
            dcbb = d_cb[k].astype(bf16)
            d_c[k] = d_c[k] + _nn(dcbb, q["bmb"][k])
            d_b[k] = d_b[k] + _tn(dcbb, q["cmb"][k])
            dxbc_ref[:, B_W + k * B_STATE:B_W + (k + 1) * B_STATE] = d_b[k]
            dxbc_ref[:, B_W + (2 + k) * B_STATE:B_W + (3 + k) * B_STATE] = d_c[k]
        dxdt = dxdt_ref[...]
        dxbc_ref[:, 0:B_W] = dy * d_ref[...] + dxdt * dt
        dcs = dcs_ref[...]
        dad = jnp.sum(dcs, axis=0, keepdims=True) - _cumsum_rows(dcs) + dcs
        ddt = dxdt * xs + dad * a
        dalog_ref[...] += jnp.sum(dad * dt, axis=0, keepdims=True) * a
        dtraw = ddt * _sigmoid(q["dtin"])
        ddtb_ref[...] += jnp.sum(dtraw, axis=0, keepdims=True)
        dpb_ref[:, B_W + XBC_W:PB_W] = dtraw.astype(bf16)
        dpre = dxbc_ref[...] * _silu_grad(q["pre"], q["sg"])
        dx, dw, db = _conv_bwd(buf_ref, dbuf_ref, dpre, dnext_ref[...], cw, CHUNK)
        dnext_ref[...] = dpre[0:8, :]
        dcw_ref[...] += dw
        dcb_ref[...] += db
        dpb_ref[:, B_W:B_W + XBC_W] = dx.astype(bf16)

    vec = pl.BlockSpec((1, B_W), lambda i: (0, 0))
    cwspec = pl.BlockSpec((4, XBC_W), lambda i: (0, 0))
    cbspec = pl.BlockSpec((1, XBC_W), lambda i: (0, 0))

    def rev(w):
        return pl.BlockSpec((CHUNK, w), lambda i: (nc - 1 - i, 0))

    vshape = jax.ShapeDtypeStruct((1, B_W), f32)
    return pl.pallas_call(
        body, name="ssd_bwd", grid=(nc,),
        in_specs=[rev(PB_W), pl.BlockSpec((8, PB_W), lambda i: (jnp.maximum((nc - 1 - i) * (CHUNK // 8) - 1, 0), 0)),
                  rev(B_W), pl.BlockSpec((1, B_W, B_STATE), lambda i: (nc - 1 - i, 0, 0)), rev(B_W),
                  *[_lspec(a, l) for a in (conv_w, conv_b, dtb, alog, dskip, norm_g)]],
        out_specs=[rev(PB_W), cwspec, cbspec, vec, vec, vec, vec],
        out_shape=[jax.ShapeDtypeStruct((t, PB_W), bf16), jax.ShapeDtypeStruct((4, XBC_W), f32),
                   jax.ShapeDtypeStruct((1, XBC_W), f32), vshape, vshape, vshape, vshape],
        scratch_shapes=[pltpu.VMEM((8 + CHUNK, XBC_W), f32), pltpu.VMEM((CHUNK + 8, XBC_W), f32),
                        pltpu.VMEM((B_W, B_STATE), f32), pltpu.VMEM((8, XBC_W), f32),
                        pltpu.VMEM((CHUNK, XBC_W), f32), pltpu.VMEM((CHUNK, B_W), f32), pltpu.VMEM((CHUNK, B_W), f32)],
        compiler_params=_params(("arbitrary",)),
    )(pb, pb, yp, sprev, dyb, conv_w, conv_b, dtb, alog, dskip, norm_g)


def _loss_fwd(y, target):
    t, d = y.shape
    tm = _tile(t, 512)

    def body(y_ref, t_ref, dy_ref, loss_ref):
        @pl.when(pl.program_id(0) == 0)
        def _():
            loss_ref[...] = jnp.zeros_like(loss_ref)

        e = y_ref[...] - t_ref[...]
        dy_ref[...] = e * (1.0 / d)
        per_tok = jnp.mean(e * e, axis=-1, keepdims=True)
        loss_ref[...] += 0.5 * jnp.sum(per_tok, axis=0, keepdims=True)

    row = pl.BlockSpec((tm, d), lambda i: (i, 0))
    return pl.pallas_call(
        body, name="loss_fwd", grid=(t // tm,), in_specs=[row, row],
        out_specs=[row, pl.BlockSpec((1, 128), lambda i: (0, 0))],
        out_shape=[jax.ShapeDtypeStruct((t, d), f32), jax.ShapeDtypeStruct((1, 128), f32)],
        compiler_params=_params(("arbitrary",)),
    )(y, target)


def _row_tile(r):
    return 512 if r % 512 == 0 else r


def _pair_add(g, r, c_dev):
    _, nl, rows, cols = g.shape
    tr = _row_tile(rows)

    def body(c_ref, g_ref, r_ref, o_ref):
        o_ref[...] = (g_ref[...].astype(f32) + r_ref[...].astype(f32)).astype(bf16)

    blk = (None, None, tr, cols)
    return pl.pallas_call(
        body, name="pair_add",
        grid_spec=pltpu.PrefetchScalarGridSpec(
            num_scalar_prefetch=1, grid=(4, nl, rows // tr),
            in_specs=[pl.BlockSpec(blk, lambda b, l, i, c: (2 * b + c[0], l, i, 0)),
                      pl.BlockSpec(blk, lambda b, l, i, c: (b, l, i, 0))],
            out_specs=pl.BlockSpec(blk, lambda b, l, i, c: (b, l, i, 0))),
        out_shape=jax.ShapeDtypeStruct(r.shape, bf16),
        compiler_params=_params(("arbitrary", "arbitrary", "arbitrary")),
    )(c_dev, g, r)


def _grad_sum(s, q, b_dev):
    _, nl, rows, cols = s.shape
    tr = _row_tile(rows)

    def body(b_ref, s_ref, q0_ref, q1_ref, q2_ref, o_ref):
        o_ref[...] = ((s_ref[...].astype(f32) + q0_ref[...].astype(f32)) + q1_ref[...].astype(f32)) + q2_ref[...].astype(f32)

    blk = (None, None, tr, cols)

    def qspec(k):
        return pl.BlockSpec(blk, lambda l, i, b: (k, l, i, 0))

    return pl.pallas_call(
        body, name="grad_sum",
        grid_spec=pltpu.PrefetchScalarGridSpec(
            num_scalar_prefetch=1, grid=(nl, rows // tr),
            in_specs=[pl.BlockSpec(blk, lambda l, i, b: (b[0], l, i, 0)), qspec(0), qspec(1), qspec(2)],
            out_specs=pl.BlockSpec((None, tr, cols), lambda l, i, b: (l, i, 0))),
        out_shape=jax.ShapeDtypeStruct(s.shape[1:], f32),
        compiler_params=_params(("arbitrary", "arbitrary")),
    )(b_dev, s, q, q, q)


def _sum_devices(parts):
    n, rows, cols = parts.shape
    tr = _row_tile(rows)

    def body(p_ref, o_ref):
        acc = p_ref[0]
        for k in range(1, n):
            acc = acc + p_ref[k]
        o_ref[...] = acc

    return pl.pallas_call(
        body, name="sum_devices", grid=(rows // tr,),
        in_specs=[pl.BlockSpec((n, tr, cols), lambda i: (0, i, 0))],
        out_specs=pl.BlockSpec((tr, cols), lambda i: (i, 0)),
        out_shape=jax.ShapeDtypeStruct((rows, cols), f32),
        compiler_params=_params(("arbitrary",)),
    )(parts)


def _adamw(w, m, v, g):
    nl, rows, cols = w.shape
    tr = _row_tile(rows)
    tc = 128 if (tr == rows and rows * cols * 4 > ADAMW_BLOCK_BYTES and cols % 128 == 0) else cols

    def body(w_ref, m_ref, v_ref, g_ref, d_ref, nm_ref, nv_ref):
        d_ref[...], nm_ref[...], nv_ref[...] = _adamw_math(w_ref[...], m_ref[...], v_ref[...], g_ref[...])

    blk = pl.BlockSpec((None, tr, tc), lambda l, i, c: (l, i, c))
    shape = jax.ShapeDtypeStruct(w.shape, f32)
    return pl.pallas_call(
        body, name="adamw", grid=(nl, rows // tr, cols // tc), in_specs=[blk] * 4, out_specs=[blk] * 3,
        out_shape=[shape] * 3, compiler_params=_params(("arbitrary", "arbitrary", "arbitrary")),
    )(w, m, v, g)


def _adamw_math(w, m, v, g):
    nm = ADAM_B1 * m + (1.0 - ADAM_B1) * g
    nv = ADAM_B2 * v + (1.0 - ADAM_B2) * (g * g)
    m_hat = nm / (1.0 - ADAM_B1 ** ADAM_STEP)
    v_hat = nv / (1.0 - ADAM_B2 ** ADAM_STEP)
    return -ADAM_LR * (m_hat / (jnp.sqrt(v_hat) + ADAM_EPS) + ADAM_WD * w), nm, nv


def _adamw_layer(w, m, v, s, q, b_dev, outs, l, deps=()):
    _, rows, cols = w.shape
    tr = _row_tile(rows)

    def body(b_ref, w_ref, m_ref, v_ref, s_ref, q0_ref, q1_ref, q2_ref, o0, o1, o2, o3, g_ref, d_ref, nm_ref, nv_ref):
        g = ((s_ref[...].astype(f32) + q0_ref[...].astype(f32)) + q1_ref[...].astype(f32)) + q2_ref[...].astype(f32)
        g_ref[...] = g
        d_ref[...], nm_ref[...], nv_ref[...] = _adamw_math(w_ref[...], m_ref[...], v_ref[...], g)

    wspec = pl.BlockSpec((None, tr, cols), lambda i, b: (l, i, 0))
    blk = (None, None, tr, cols)

    def qspec(k):
        return pl.BlockSpec(blk, lambda i, b: (k, 0, i, 0))

    shape = jax.ShapeDtypeStruct(w.shape, f32)
    return pl.pallas_call(
        _after(body, 12, deps), name="adamw_layer",
        grid_spec=pltpu.PrefetchScalarGridSpec(
            num_scalar_prefetch=1, grid=(rows // tr,),
            in_specs=[wspec] * 3 + [pl.BlockSpec(blk, lambda i, b: (b[0], 0, i, 0)), qspec(0), qspec(1), qspec(2)]
            + [ANY] * (4 + len(deps)),
            out_specs=[wspec] * 4),
        out_shape=[shape] * 4, input_output_aliases={8 + k: k for k in range(4)},
        compiler_params=_params(("arbitrary",)),
    )(b_dev, w, m, v, s, q, q, q, *outs, *deps)


def _place():
    return lax.axis_index("x"), lax.axis_index("y"), lax.axis_index("c")


def _all_gather(shards, deps=()):
    n = len(shards)
    nd = len(deps)

    def body(*refs):
        src, dst = refs[:n], refs[n:2 * n]
        send_sems, recv_sems, local_sems = refs[2 * n:]
        x, y, c = _place()
        me, sibling = (x, y, c), (x, y, 1 - c)
        chips = [(1 - x, y), (x, 1 - y), (1 - x, 1 - y)]

        def copy(a, k, block, to, from_shard=False):
            px, py, pc = block
            rows = dst[a].at[4 * px + 2 * py + pc]
            return pltpu.make_async_remote_copy(
                src_ref=src[a] if from_shard else rows, dst_ref=rows,
                send_sem=send_sems.at[a, k], recv_sem=recv_sems.at[a, k], device_id=to, device_id_type=MESH)

        mine = [pltpu.make_async_copy(src[a], dst[a].at[4 * x + 2 * y + c], local_sems.at[a]) for a in range(n)]
        for cp in mine:
            cp.start()
        first = []
        for a in range(n):
            first.append(copy(a, 0, me, sibling, True))
            first += [copy(a, 1 + j, me, (*chip, c), True) for j, chip in enumerate(chips)]
        for cp in first:
            cp.start()
        passed = []
        for j, chip in enumerate(chips):
            for a in range(n):
                copy(a, 1 + j, (*chip, c), me).wait_recv()
                fwd = copy(a, 4 + j, (*chip, c), sibling)
                fwd.start()
                passed.append(fwd)
        for a in range(n):
            copy(a, 0, sibling, me).wait_recv()
            for j, chip in enumerate(chips):
                copy(a, 4 + j, (*chip, 1 - c), me).wait_recv()
        for cp in first + passed:
            cp.wait_send()
        for cp in mine:
            cp.wait()

    return pl.pallas_call(
        _after(body, n, deps), name="all_gather", in_specs=[ANY] * (n + nd), out_specs=[ANY] * n,
        out_shape=[jax.ShapeDtypeStruct((N_DEV,) + s.shape, s.dtype) for s in shards],
        scratch_shapes=[pltpu.SemaphoreType.DMA((n, 7)), pltpu.SemaphoreType.DMA((n, 7)), pltpu.SemaphoreType.DMA((n,))],
    )(*shards, *deps)


HBM = pl.BlockSpec(memory_space=pltpu.HBM)
SEM = pl.BlockSpec(memory_space=pltpu.SEMAPHORE)
_EFFECT = pltpu.SideEffectType.DATAFLOW_SIDE_EFFECTING


def _split_start(name, srcs, dsts, sem_shape, plan):
    ns, nb = len(srcs), len(srcs) + len(dsts)

    def body(*refs):
        send_sems, recv_sems = refs[nb], refs[nb + 1]
        for cp in plan(refs[:ns], refs[ns:nb], send_sems, recv_sems):
            cp.start()
        refs[-1][...] = jnp.zeros_like(refs[-1])

    bufs = list(srcs) + list(dsts)
    return pl.pallas_call(
        body, name=name,
        out_shape=(pltpu.SemaphoreType.DMA(sem_shape), pltpu.SemaphoreType.DMA(sem_shape),
                   *[pltpu.HBM(a.shape, a.dtype) for a in bufs], jax.ShapeDtypeStruct((8, 128), f32)),
        in_specs=[HBM] * nb, out_specs=(SEM, SEM, *[HBM] * nb, pl.BlockSpec(memory_space=pltpu.VMEM)),
        input_output_aliases={i: 2 + i for i in range(nb)},
        compiler_params=pltpu.CompilerParams(has_side_effects=_EFFECT),
    )(*[pltpu.with_memory_space_constraint(a, pltpu.HBM) for a in bufs])


def _split_wait(name, started, ns, plan, after):
    send_sems, recv_sems = started[0], started[1]
    bufs = list(started[2:-1])
    nb = len(bufs)
    after = list(after) if isinstance(after, (list, tuple)) else [after]

    def body(*refs):
        for cp in plan(refs[:ns], refs[ns:nb], refs[nb], refs[nb + 1]):
            cp.wait_send()
            cp.wait_recv()

    return pl.pallas_call(
        body, name=name, out_shape=tuple(pltpu.HBM(a.shape, a.dtype) for a in bufs),
        in_specs=[HBM] * nb + [SEM, SEM] + [ANY] * len(after), out_specs=tuple([HBM] * nb),
        input_output_aliases={i: i for i in range(nb)},
        compiler_params=pltpu.CompilerParams(has_side_effects=_EFFECT),
    )(*bufs, send_sems, recv_sems, *after)


def _remote(src, dst, send_sem, recv_sem, to):
    return pltpu.make_async_remote_copy(src_ref=src, dst_ref=dst, send_sem=send_sem, recv_sem=recv_sem,
                                        device_id=to, device_id_type=MESH)


def _gather_plan(src, dst, send_sems, recv_sems):
    x, y, c = _place()
    peers = [(x, y, 1 - c), (1 - x, y, c), (x, 1 - y, c), (1 - x, 1 - y, c)]
    copies = []
    for a in range(len(dst)):
        rows = dst[a].at[4 * x + 2 * y + c]
        copies += [_remote(rows, rows, send_sems.at[4 * a + k], recv_sems.at[4 * a + k], peer) for k, peer in enumerate(peers)]
    return copies


def _pair_plan(src, dst, send_sems, recv_sems):
    x, y, c = _place()
    return [_remote(src[a].at[2 * b + (1 - c)], dst[a].at[b], send_sems.at[4 * a + b], recv_sems.at[4 * a + b], (x, y, 1 - c))
            for a in range(len(src)) for b in range(4)]


def _chips_plan(src, dst, send_sems, recv_sems):
    x, y, c = _place()
    chips = [(1 - x, y), (x, 1 - y), (1 - x, 1 - y)]
    return [_remote(src[a].at[2 * px + py], dst[a].at[j], send_sems.at[3 * a + j], recv_sems.at[3 * a + j], (px, py, c))
            for a in range(len(src)) for j, (px, py) in enumerate(chips)]


def _forward_plan(src, dst, send_sems, recv_sems):
    x, y, c = _place()
    copies = []
    for a in range(len(dst)):
        for j, (px, py) in enumerate([(1 - x, y), (x, 1 - y), (1 - x, 1 - y)]):
            rows = dst[a].at[4 * px + 2 * py + c]
            copies.append(_remote(rows, rows, send_sems.at[3 * a + j], recv_sems.at[3 * a + j], (x, y, 1 - c)))
    return copies


def _gather_finish(bufs):
    n = len(bufs)

    def body(*refs):
        dst = refs[n:2 * n]
        send_sems, recv_sems = refs[2 * n:]
        x, y, c = _place()
        chips = [(1 - x, y), (x, 1 - y), (1 - x, 1 - y)]
        passed = []
        for a in range(n):
            for j, (px, py) in enumerate(chips):
                rows = dst[a].at[4 * px + 2 * py + c]
                passed.append(_remote(rows, rows, send_sems.at[a, j], recv_sems.at[a, j], (x, y, 1 - c)))
        for cp in passed:
            cp.start()
        for cp in passed:
            cp.wait_send()
        for a in range(n):
            for j, (px, py) in enumerate(chips):
                rows = dst[a].at[4 * px + 2 * py + (1 - c)]
                _remote(rows, rows, send_sems.at[a, j], recv_sems.at[a, j], (x, y, 1 - c)).wait_recv()

    return pl.pallas_call(
        body, name="gather_finish", in_specs=[ANY] * n, out_specs=[ANY] * n,
        out_shape=[jax.ShapeDtypeStruct(b.shape, b.dtype) for b in bufs],
        input_output_aliases={a: a for a in range(n)},
        scratch_shapes=[pltpu.SemaphoreType.DMA((n, 3)), pltpu.SemaphoreType.DMA((n, 3))],
    )(*bufs)


def _place_shards(mats, l, dev):
    n = len(mats)

    def body(dev_ref, *refs):
        for a in range(n):
            refs[n + a][...] = refs[a][...].astype(bf16)

    return pl.pallas_call(
        body, name="place_shards",
        grid_spec=pltpu.PrefetchScalarGridSpec(
            num_scalar_prefetch=1, grid=(1,),
            in_specs=[pl.BlockSpec((None,) + m.shape[1:], lambda i, dv: (l, 0, 0)) for m in mats],
            out_specs=[pl.BlockSpec((None, None) + m.shape[1:], lambda i, dv: (dv[0], 0, 0, 0)) for m in mats]),
        out_shape=[jax.ShapeDtypeStruct((N_DEV, 1) + m.shape[1:], bf16) for m in mats],
        compiler_params=_params(("arbitrary",)),
    )(dev, *mats)


BIG = ("ffn1_w_gu", "ffn1_w_down", "mix_w_in", "mix_w_out", "ffn2_w_gu", "ffn2_w_down")
SHARDED_CONV = ("lru_conv_w", "ssd_conv_w")
REPLICATED = ("ffn1_pre_g", "ffn1_post_g", "mix_pre_g", "mix_post_g", "lru_conv_b", "lru_w_r", "lru_b_r", "lru_w_i",
              "lru_b_i", "lru_lambda", "ssd_conv_b", "ssd_dt_bias", "ssd_a_log", "ssd_d", "ssd_norm_g", "sgu_ln_g",
              "sgu_ln_b", "sgu_w_s", "sgu_b_s", "ffn2_pre_g", "ffn2_post_g")
WEIGHTS = ("ffn1_pre_g", "ffn1_post_g", "ffn1_w_gu", "ffn1_w_down", "mix_pre_g", "mix_post_g", "mix_w_in", "mix_w_out",
           "lru_conv_w", "lru_conv_b", "lru_w_r", "lru_b_r", "lru_w_i", "lru_b_i", "lru_lambda", "ssd_conv_w",
           "ssd_conv_b", "ssd_dt_bias", "ssd_a_log", "ssd_d", "ssd_norm_g", "sgu_ln_g", "sgu_ln_b", "sgu_w_s", "sgu_b_s",
           "ffn2_pre_g", "ffn2_post_g", "ffn2_w_gu", "ffn2_w_down")
DT_LO = PA_W + B_W + XBC_W
N_HEADS = B_W // HEAD
PACK_COLS = 1024


def _size(shape):
    size = 1
    for dim in shape:
        size *= dim
    return size


def _pack_rows(shape):
    return -(-_size(shape) // PACK_COLS)


def _pack(arrays):
    pieces = [jnp.pad(a.reshape(-1), (0, _pack_rows(a.shape) * PACK_COLS - _size(a.shape))) for a in arrays]
    rows = sum(_pack_rows(a.shape) for a in arrays)
    if rows % 8:
        pieces.append(jnp.zeros(((8 - rows % 8) * PACK_COLS,), f32))
    return jnp.concatenate(pieces).reshape(-1, PACK_COLS)


def _unpack(packed, shapes):
    out, row = [], 0
    for s in shapes:
        nr = _pack_rows(s)
        out.append(packed[row:row + nr].reshape(-1)[:_size(s)].reshape(s))
        row += nr
    return out


def _widen_w_in(w):
    return jnp.concatenate([w[..., :DT_LO], jnp.repeat(w[..., DT_LO:DT_LO + N_HEADS], HEAD, axis=-1),
                            w[..., DT_LO + N_HEADS:]], axis=-1)


def _narrow_w_in_grad(g):
    dt = g[..., DT_LO:DT_LO + B_W]
    dt = dt.reshape(dt.shape[:-1] + (N_HEADS, HEAD)).sum(-1)
    return jnp.concatenate([g[..., :DT_LO], dt, g[..., DT_LO + B_W:]], axis=-1)


def _per_head(a):
    return a.reshape(a.shape[:-1] + (N_HEADS, HEAD)).sum(-1)


def kernel(x, ffn1_pre_g, ffn1_post_g, ffn1_w_gu, ffn1_w_down, mix_pre_g, mix_post_g, mix_w_in, mix_w_out, lru_conv_w, lru_conv_b, lru_w_r, lru_b_r, lru_w_i, lru_b_i, lru_lambda, ssd_conv_w, ssd_conv_b, ssd_dt_bias, ssd_a_log, ssd_d, ssd_norm_g, sgu_ln_g, sgu_ln_b, sgu_w_s, sgu_b_s, ffn2_pre_g, ffn2_post_g, ffn2_w_gu, ffn2_w_down, loss_target, m_ffn1_pre_g, m_ffn1_post_g, m_ffn1_w_gu, m_ffn1_w_down, m_mix_pre_g, m_mix_post_g, m_mix_w_in, m_mix_w_out, m_lru_conv_w, m_lru_conv_b, m_lru_w_r, m_lru_b_r, m_lru_w_i, m_lru_b_i, m_lru_lambda, m_ssd_conv_w, m_ssd_conv_b, m_ssd_dt_bias, m_ssd_a_log, m_ssd_d, m_ssd_norm_g, m_sgu_ln_g, m_sgu_ln_b, m_sgu_w_s, m_sgu_b_s, m_ffn2_pre_g, m_ffn2_post_g, m_ffn2_w_gu, m_ffn2_w_down, v_ffn1_pre_g, v_ffn1_post_g, v_ffn1_w_gu, v_ffn1_w_down, v_mix_pre_g, v_mix_post_g, v_mix_w_in, v_mix_w_out, v_lru_conv_w, v_lru_conv_b, v_lru_w_r, v_lru_b_r, v_lru_w_i, v_lru_b_i, v_lru_lambda, v_ssd_conv_w, v_ssd_conv_b, v_ssd_dt_bias, v_ssd_a_log, v_ssd_d, v_ssd_norm_g, v_sgu_ln_g, v_sgu_ln_b, v_sgu_w_s, v_sgu_b_s, v_ffn2_pre_g, v_ffn2_post_g, v_ffn2_w_gu, v_ffn2_w_down):
    given = dict(locals())
    w = {n: given[n] for n in WEIGHTS}
    mom = {n: given["m_" + n] for n in WEIGHTS}
    var = {n: given["v_" + n] for n in WEIGHTS}
    nl = ffn1_pre_g.shape[0]
    _, t, d = x.shape
    xi, yi, ci = _place()
    dev = 4 * xi + 2 * yi + ci
    c_dev = jnp.reshape(ci, (1,)).astype(jnp.int32)
    b_dev = jnp.reshape(2 * xi + yi, (1,)).astype(jnp.int32)

    conv_shapes = [lru_conv_w.shape, ssd_conv_w.shape]
    shards = [ffn1_w_gu, ffn1_w_down, _widen_w_in(mix_w_in), mix_w_out, ffn2_w_gu, ffn2_w_down]
    nbig = len(shards)
    dev_arr = jnp.reshape(dev, (1,)).astype(jnp.int32)
    conv_pack = _pack([lru_conv_w, ssd_conv_w])
    conv_buf = lax.dynamic_update_slice_in_dim(jnp.zeros((N_DEV,) + conv_pack.shape, f32), conv_pack[None], dev, axis=0)
    def gather_groups(l):
        return [(0, 1), (2, 3), (4, 5)] if l == 0 else [tuple(range(nbig))]

    gather_started = {}
    for l in range(nl):
        for gi, idx in enumerate(gather_groups(l)):
            bufs = list(_place_shards([shards[i] for i in idx], l, dev_arr)) + ([conv_buf] if (l, gi) == (0, 1) else [])
            gather_started[l, gi] = _split_start(f"gather_start_{l}_{gi}", [], bufs, (4 * len(bufs),), _gather_plan)

    def finish_gather(l, gi, after):
        waited = _split_wait(f"gather_wait_{l}_{gi}", gather_started[l, gi], 0, _gather_plan, after)
        return _gather_finish(list(waited))

    def conv_taps(conv_all):
        full = []
        for k, shape in enumerate(conv_shapes):
            per_dev = jnp.stack([_unpack(conv_all[s], conv_shapes)[k] for s in range(N_DEV)], axis=2)
            full.append(per_dev.reshape(shape[0], shape[1], N_DEV * shape[2]))
        return full

    def vec(a):
        return a.reshape(nl, 1, -1)

    def per_channel(a):
        return jnp.repeat(a, HEAD, axis=-1).reshape(nl, 1, B_W)

    eye = jnp.eye(A_W // HEAD, dtype=f32)

    def block_diag(a):
        return jnp.einsum("lhij,hg->lhigj", a, eye).reshape(nl, A_W, A_W).astype(bf16)

    causal = jnp.tril(jnp.ones((CHUNK, CHUNK), dtype=bool))
    p = dict(
        ffn1_pre=vec(ffn1_pre_g), ffn1_post=vec(ffn1_post_g), mix_pre=vec(mix_pre_g), mix_post=vec(mix_post_g),
        ffn2_pre=vec(ffn2_pre_g), ffn2_post=vec(ffn2_post_g),
        lru=(vec(lru_conv_b), block_diag(lru_w_r), block_diag(lru_w_i), vec(lru_b_r), vec(lru_b_i), vec(lru_lambda)),
        ssd=(vec(ssd_conv_b), per_channel(ssd_dt_bias), per_channel(ssd_a_log), per_channel(ssd_d), vec(ssd_norm_g)),
    )
    wm = jnp.where(causal, sgu_w_s, 0.0).astype(bf16)
    sgu_bias = jnp.repeat(jnp.swapaxes(sgu_b_s, 1, 2), HEAD, axis=2)
    sgu_f = (vec(sgu_ln_g), vec(sgu_ln_b), wm, sgu_bias)
    sgu_b = (vec(sgu_ln_g), vec(sgu_ln_b), wm, jnp.swapaxes(wm, 2, 3), sgu_bias)

    small_names = REPLICATED + SHARDED_CONV
    small_state = [_pack([src[n] for n in small_names])[None] for src in (w, mom, var)]
    prepared = [a for v in p.values() for a in (v if isinstance(v, tuple) else (v,))] + list(sgu_b) + small_state

    xs = x.reshape(t, d)
    saved, gathered, early_forward = [], [], {}
    for l in range(nl):
        x0 = xs
        if l == 0:
            wgu1, wd1 = finish_gather(0, 0, [x0] + prepared)
            deps = tuple(started[-1] for key, started in gather_started.items() if key != (0, 0))
        elif l in early_forward:
            wgu1, wd1, win, wout, wgu2, wd2 = _split_wait(f"forward_wait_{l}", early_forward[l], 0, _forward_plan, x0)
            deps = ()
        else:
            wgu1, wd1, win, wout, wgu2, wd2 = finish_gather(l, 0, x0)
            deps = ()
        x1, hb1, g1, u1, f1 = _ffn_fwd(x0, p["ffn1_pre"], p["ffn1_post"], wgu1, wd1, l, deps)
        if l == 0:
            win, wout, conv_all = finish_gather(0, 1, x1)
            lru_cw, ssd_cw = conv_taps(conv_all)
            p["lru"], p["ssd"] = (lru_cw,) + p["lru"], (ssd_cw,) + p["ssd"]
        hbm, pa, pb, pc = _mix_in_fwd(x1, p["mix_pre"], win, l)
        ya, h, gates = _lru_fwd(pa, *p["lru"], l)
        yb, yp, sp = _ssd_fwd(pb, *p["ssd"], l)
        yc = _sgu_fwd(pc, *sgu_f, l)
        x2, cat, m = _mix_out_fwd(x1, ya, yb, yc, p["mix_post"], wout, l)
        deps = ()
        if l == 0:
            wgu2, wd2 = finish_gather(0, 2, x2)
        elif l + 1 < nl:
            waited = _split_wait(f"gather_wait_{l + 1}_0", gather_started[l + 1, 0], 0, _gather_plan, x2)
            early_forward[l + 1] = _split_start(f"forward_start_{l + 1}", [], list(waited), (3 * nbig,), _forward_plan)
            deps = (early_forward[l + 1][-1],)
        xs, hb2, g2, u2, f2 = _ffn_fwd(x2, p["ffn2_pre"], p["ffn2_post"], wgu2, wd2, l, deps)
        gathered.append((wgu1, wd1, win, wout, wgu2, wd2))
        saved.append((x0, hb1, g1, u1, f1, x1, hbm, pa, pb, pc, h, gates, yp, sp, cat, m, x2, hb2, g2, u2, f2))
    dy, loss_part = _loss_fwd(xs, loss_target.reshape(t, d))
    loss = lax.psum(loss_part[0, 0], ("x", "y", "c"))

    small = {n: [None] * nl for n in REPLICATED + SHARDED_CONV}
    grads, delta, new_m, new_v = {}, {}, {}, {}
    fused = [n for n in BIG if n != "mix_w_in"]

    def oriented(a, n):
        return jnp.swapaxes(a, 1, 2) if n.endswith("w_gu") else a

    opt_in = {n: tuple(oriented(src[n], n) for src in (w, mom, var)) for n in fused}
    opt_out = {n: tuple(lax.empty(opt_in[n][0].shape, f32) for _ in range(4)) for n in fused}
    w_in_grads = [None] * nl
    grad_shapes = {n: (s.shape[2], s.shape[1]) if n.endswith("w_gu") else s.shape[1:] for n, s in zip(BIG, shards)}

    def start_pair(tag, lp, names, gbuf):
        landing = [lax.empty((4, 1) + grad_shapes[n], bf16) for n in names]
        started = _split_start(f"pair_start_{tag}", [gbuf[n] for n in names], landing, (4 * len(names),), _pair_plan)
        return tag, lp, names, started

    def finish_pair(pending, after):
        tag, lp, names, started = pending
        k = len(names)
        done = _split_wait(f"pair_wait_{tag}", started, k, _pair_plan, after)
        sums = [_pair_add(g, r, c_dev) for g, r in zip(done[:k], done[k:])]
        landing = [lax.empty((3,) + s.shape[1:], bf16) for s in sums]
        return tag, lp, names, _split_start(f"chips_start_{tag}", sums, landing, (3 * k,), _chips_plan)

    def finish_chips(pending, after, deps=()):
        tag, lp, names, started = pending
        k = len(names)
        done = _split_wait(f"chips_wait_{tag}", started, k, _chips_plan, after)
        last = None
        for n, s, q in zip(names, done[:k], done[k:]):
            if n == "mix_w_in":
                w_in_grads[lp] = last = _grad_sum(s, q, b_dev)
            else:
                opt_out[n] = tuple(_adamw_layer(*opt_in[n], s, q, b_dev, opt_out[n], lp, deps))
                last = opt_out[n][0]
        return last

    early = ("ffn2_w_gu", "ffn2_w_down", "mix_w_out")
    late = ("mix_w_in", "ffn1_w_gu", "ffn1_w_down")
    pending_pair = pending_chips = early_pair = early_chips = upper_started = None
    deferred = []
    names = REPLICATED + SHARDED_CONV
    assert nl > 1
    for l in reversed(range(nl)):
        x0, hb1, g1, u1, f1, x1, hbm, pa, pb, pc, h, gates, yp, sp, cat, m, x2, hb2, g2, u2, f2 = saved[l]
        wgu1, wd1, win, wout, wgu2, wd2 = gathered[l][:nbig]
        gbuf ={n: lax.empty((N_DEV, 1) + grad_shapes[n], bf16) for n in BIG}
        deps = () if pending_pair is None else (pending_pair[3][-1],)
        if l == 0:
            deps += (upper_started[-1],)
        dx2, dfb, act, dg, du, dpre, dpost = _ffn_bwd(x2, dy, f2, p["ffn2_pre"], p["ffn2_post"], g2, u2, wgu2, wd2, l, deps)
        small["ffn2_pre_g"][l], small["ffn2_post_g"][l] = dpre[0], dpost[0]
        gbuf["ffn2_w_gu"] = _wgrad_cols(hb2, dg, gbuf["ffn2_w_gu"], 0, 0)
        gbuf["ffn2_w_gu"] = _wgrad_cols(hb2, du, gbuf["ffn2_w_gu"], 0, dg.shape[0])
        gbuf["ffn2_w_down"] = _wgrad_rows(act, dfb, gbuf["ffn2_w_down"], 0)
        deps = ()
        if pending_pair is not None:
            pending_chips = finish_pair(pending_pair, dx2)
            deps = (pending_chips[3][-1],)

        dm, dya, dyb, dyc, dpost = _mix_out_bwd(dx2, m, p["mix_post"], wout, l, deps)
        small["mix_post_g"][l] = dpost[0]
        gbuf["mix_w_out"] = _wgrad_kblocks(cat, [dm], gbuf["mix_w_out"], 0)
        deps = ()
        if l == 0:
            early_pair = start_pair("0a", 0, early, gbuf)
            deps = (early_pair[3][-1],)
        dpc, dws, dbias, dlg, dlb = _sgu_bwd(pc, dyc, *sgu_b, l, deps)
        small["sgu_w_s"][l] = jnp.where(causal, dws, 0.0)
        small["sgu_b_s"][l] = dbias.reshape(CHUNK, C_W // HEAD, HEAD).sum(-1).T
        small["sgu_ln_g"][l], small["sgu_ln_b"][l] = dlg[0], dlb[0]
        dpb, dcw, dcb, ddtb, dalog, ddsk, dng = _ssd_bwd(pb, yp, sp, dyb, *p["ssd"], l)
        small["ssd_conv_w"][l], small["ssd_conv_b"][l], small["ssd_norm_g"][l] = dcw, dcb[0], dng[0]
        small["ssd_dt_bias"][l], small["ssd_a_log"][l], small["ssd_d"][l] = _per_head(ddtb[0]), _per_head(dalog[0]), _per_head(ddsk[0])
        deps = ()
        if l == 0:
            early_chips = finish_pair(early_pair, dpb)
            deps = (early_chips[3][-1],)
        dpa, dcw, dcb, dwr, dwi, dbr, dbi, dlam = _lru_bwd(pa, h, gates, dya, *p["lru"], l, deps)
        small["lru_conv_w"][l], small["lru_conv_b"][l], small["lru_lambda"][l] = dcw, dcb[0], dlam[0]
        small["lru_b_r"][l], small["lru_b_i"][l] = dbr[0], dbi[0]
        heads = range(A_W // HEAD)
        small["lru_w_r"][l] = jnp.stack([dwr[HEAD * i:HEAD * (i + 1), HEAD * i:HEAD * (i + 1)] for i in heads])
        small["lru_w_i"][l] = jnp.stack([dwi[HEAD * i:HEAD * (i + 1), HEAD * i:HEAD * (i + 1)] for i in heads])
        dx1, dpre = _mix_in_bwd(x1, dx2, p["mix_pre"], dpa, dpb, dpc, win, l)
        small["mix_pre_g"][l] = dpre[0]
        gbuf["mix_w_in"] = _wgrad_kblocks(hbm, [dpa, dpb, dpc], gbuf["mix_w_in"], 0)

        dy, dfb, act, dg, du, dpre, dpost = _ffn_bwd(x0, dx1, f1, p["ffn1_pre"], p["ffn1_post"], g1, u1, wgu1, wd1, l)
        small["ffn1_pre_g"][l], small["ffn1_post_g"][l] = dpre[0], dpost[0]
        gbuf["ffn1_w_gu"] = _wgrad_cols(hb1, dg, gbuf["ffn1_w_gu"], 0, 0)
        gbuf["ffn1_w_gu"] = _wgrad_cols(hb1, du, gbuf["ffn1_w_gu"], 0, dg.shape[0])
        gbuf["ffn1_w_down"] = _wgrad_rows(act, dfb, gbuf["ffn1_w_down"], 0)
        if pending_chips is not None:
            deferred.append(pending_chips)
            pending_chips = None
        pending_pair = start_pair(f"{l}", l, late if l == 0 else BIG, gbuf)
        if l == 1:
            upper = [jnp.stack(small[n][1:]) for n in names]
            upper_pack = _pack(upper)
            upper_buf = lax.dynamic_update_slice_in_dim(
                jnp.zeros((N_DEV,) + upper_pack.shape, f32), upper_pack[None], dev, axis=0)
            upper_started = _split_start("small_start", [], [upper_buf], (4,), _gather_plan)
    grad_x = dy.reshape(x.shape)

    lower = [jnp.stack(small[n][:1]) for n in names]
    lower_total = _sum_devices(_all_gather([_pack(lower)], (pending_pair[3][-1],))[0])
    late_chips = finish_pair(pending_pair, lower_total)
    order = lower_total
    for pending in deferred + [early_chips]:
        order = finish_chips(pending, order, (late_chips[3][-1],))
    upper_all = _gather_finish(list(_split_wait("small_wait", upper_started, 0, _gather_plan, order)))[0]
    upper_total = _sum_devices(upper_all)
    finish_chips(late_chips, [upper_total] + [opt_out[n][0] for n in fused] + [g for g in w_in_grads if g is not None])
    full = {n: jnp.concatenate([lo, up], axis=0) for n, lo, up in zip(
        names, _unpack(lower_total, [a.shape for a in lower]), _unpack(upper_total, [a.shape for a in upper]))}

    for n in fused:
        grads[n], delta[n], new_m[n], new_v[n] = (oriented(a, n) for a in opt_out[n])
    grads["mix_w_in"] = _narrow_w_in_grad(jnp.concatenate(w_in_grads, axis=0))

    def minor_rows(a):
        return jnp.transpose(a, (2, 0, 1)).reshape(1, -1, PACK_COLS)

    def major_rows(a):
        return jnp.transpose(a.reshape(mix_w_in.shape[2], nl, mix_w_in.shape[1]), (1, 2, 0))

    updated = _adamw(*[minor_rows(src["mix_w_in"]) for src in (w, mom, var, grads)])
    delta["mix_w_in"], new_m["mix_w_in"], new_v["mix_w_in"] = (major_rows(a) for a in updated)
    for n in REPLICATED:
        grads[n] = full[n]
    for n in SHARDED_CONV:
        cols = w[n].shape[2]
        grads[n] = lax.dynamic_slice_in_dim(full[n], dev * cols, cols, axis=2)
    shapes = [w[n].shape for n in names]
    packs = small_state + [_pack([grads[n] for n in names])[None]]
    for dst, packed in zip((delta, new_m, new_v), _adamw(*packs)):
        dst.update(zip(names, _unpack(packed[0], shapes)))

    return (loss, grad_x, *[grads[n] for n in WEIGHTS], *[delta[n] for n in WEIGHTS],
            *[new_m[n] for n in WEIGHTS], *[new_v[n] for n in WEIGHTS])
```

```python
import functools

import jax
import jax.numpy as jnp
from jax import lax
from jax.experimental import pallas as pl
from jax.experimental.pallas import tpu as pltpu

f32, bf16 = jnp.float32, jnp.bfloat16
MESH = pl.DeviceIdType.MESH
ANY = pl.BlockSpec(memory_space=pl.ANY)

N_DEV = 8
NORM_EPS = 1e-6
LRU_C = 8.0
CHUNK = 128
HEAD = 64
A_W, B_W, C_W = 384, 384, 256
B_STATE = 128
XBC_W = B_W + 4 * B_STATE
PA_W, PB_W, PC_W = 2 * A_W, B_W + XBC_W + B_W, 2 * C_W
IN_PAD = PA_W + PB_W + PC_W
ADAM_LR, ADAM_B1, ADAM_B2, ADAM_EPS, ADAM_WD, ADAM_STEP = 0.001, 0.9, 0.999, 1e-08, 0.01, 10
VMEM_LIMIT_BYTES = 56 * 1024 * 1024
FFN_BWD_SPLIT = 2
ADAMW_BLOCK_BYTES = 2 * 1024 * 1024
NEG_BIG = -1e30


def _params(sem=None):
    return pltpu.CompilerParams(dimension_semantics=sem, vmem_limit_bytes=VMEM_LIMIT_BYTES)


def _nn(a, b):
    return jnp.dot(a, b, preferred_element_type=f32)


def _nt(a, b):
    return lax.dot_general(a, b, (((1,), (1,)), ((), ())), preferred_element_type=f32)


def _tn(a, b):
    return lax.dot_general(a, b, (((0,), (0,)), ((), ())), preferred_element_type=f32)


def _sigmoid(x):
    return 0.5 * jnp.tanh(0.5 * x) + 0.5


def _softplus(x):
    return jnp.maximum(x, 0.0) + jnp.log(1.0 + jnp.exp(-jnp.abs(x)))


_GELU_C0, _GELU_C1 = 0.7978845608028654, 0.044715


def _gelu(x):
    t = jnp.tanh(_GELU_C0 * (x + _GELU_C1 * x * x * x))
    return 0.5 * x * (1.0 + t)


def _gelu_grad(x):
    t = jnp.tanh(_GELU_C0 * (x + _GELU_C1 * x * x * x))
    return 0.5 * (1.0 + t) + 0.5 * x * (1.0 - t * t) * _GELU_C0 * (1.0 + 3.0 * _GELU_C1 * x * x)


def _silu_grad(x, s):
    return s * (1.0 + x * (1.0 - s))


def _rms_fwd(x, g):
    r = lax.rsqrt(jnp.mean(x * x, axis=-1, keepdims=True) + NORM_EPS)
    return x * r * g


def _rms_bwd(x, g, dy):
    r = lax.rsqrt(jnp.mean(x * x, axis=-1, keepdims=True) + NORM_EPS)
    xh = x * r
    dxh = dy * g
    dx = r * (dxh - xh * jnp.mean(dxh * xh, axis=-1, keepdims=True))
    return dx, jnp.sum(dy * xh, axis=0, keepdims=True)


def _one_minus_exp(x):
    series = -x * (1.0 + x * (0.5 + x * (1.0 / 6.0 + x * (1.0 / 24.0))))
    return jnp.where(x > -0.01, series, 1.0 - jnp.exp(x))


def _cumsum_rows(x):
    row = lax.broadcasted_iota(jnp.int32, x.shape, 0)
    d = 1
    while d < x.shape[0]:
        x = x + jnp.where(row >= d, pltpu.roll(x, d, 0), 0.0)
        d *= 2
    return x


def _tile(t, cap):
    tm = min(cap, t)
    assert t % tm == 0
    return tm


def _after(body, n_in, deps):
    def wrapped(*refs):
        return body(*refs[:n_in], *refs[n_in + len(deps):])
    return wrapped


def _lspec(a, l):
    return pl.BlockSpec((None,) + a.shape[1:], lambda *_: (l,) + (0,) * (a.ndim - 1))


def _wd_rows(wd_ref):
    return wd_ref[:, 0].reshape(2 * wd_ref.shape[2], wd_ref.shape[3])


def _ffn_fwd(x, pre_g, post_g, wgu, wd, l, deps=()):
    t, d = x.shape
    nb, _, _, h = wgu.shape
    nj = nb // 2
    tm = _tile(t, 512)

    def body(x_ref, pg_ref, qg_ref, wg_ref, wu_ref, wd_ref, y_ref, hb_ref, g_ref, u_ref, f_ref, acc_ref):
        j = pl.program_id(1)

        @pl.when(j == 0)
        def _():
            hb_ref[...] = _rms_fwd(x_ref[...], pg_ref[...]).astype(bf16)
            acc_ref[...] = jnp.zeros_like(acc_ref)

        hb = hb_ref[...]
        g = _nn(hb, wg_ref[0, 0])
        u = _nn(hb, wu_ref[0, 0])
        g_ref[0] = g.astype(bf16)
        u_ref[0] = u.astype(bf16)
        a = (g * _sigmoid(g) * u).astype(bf16)
        acc_ref[...] += _nn(a, _wd_rows(wd_ref))

        @pl.when(j == nj - 1)
        def _():
            f = acc_ref[...]
            f_ref[...] = f
            y_ref[...] = x_ref[...] + 0.5 * _rms_fwd(f, qg_ref[...])

    row = pl.BlockSpec((tm, d), lambda i, j: (i, 0))
    vec = pl.BlockSpec((1, d), lambda i, j: (0, 0))
    act = pl.BlockSpec((1, tm, h), lambda i, j: (j, i, 0))
    return pl.pallas_call(
        _after(body, 6, deps), name="ffn_fwd", grid=(t // tm, nj),
        in_specs=[row, _lspec(pre_g, l), _lspec(post_g, l),
                  pl.BlockSpec((1, 1, d, h), lambda i, j: (j, 0, 0, 0)),
                  pl.BlockSpec((1, 1, d, h), lambda i, j: (j + nj, 0, 0, 0)),
                  pl.BlockSpec((2, 1, h // 2, d), lambda i, j: (j, 0, 0, 0))] + [ANY] * len(deps),
        out_specs=[row, row, act, act, row],
        out_shape=[jax.ShapeDtypeStruct((t, d), f32), jax.ShapeDtypeStruct((t, d), bf16),
                   jax.ShapeDtypeStruct((nj, t, h), bf16), jax.ShapeDtypeStruct((nj, t, h), bf16),
                   jax.ShapeDtypeStruct((t, d), f32)],
        scratch_shapes=[pltpu.VMEM((tm, d), f32)],
        compiler_params=_params(("arbitrary", "arbitrary")),
    )(x, pre_g, post_g, wgu, wgu, wd, *deps)


def _ffn_bwd(x, dy, f, pre_g, post_g, g, u, wgu, wd, l, deps=()):
    t, d = x.shape
    nj, _, h = g.shape
    tm = _tile(t, 512)

    def body(x_ref, dy_ref, f_ref, pg_ref, qg_ref, g_ref, u_ref, wg_ref, wu_ref, wd_ref,
             dx_ref, dfb_ref, a_ref, dg_ref, du_ref, dpg_ref, dqg_ref, dh_ref):
        i, j = pl.program_id(0), pl.program_id(1)

        @pl.when((i == 0) & (j == 0))
        def _():
            dpg_ref[...] = jnp.zeros_like(dpg_ref)
            dqg_ref[...] = jnp.zeros_like(dqg_ref)

        @pl.when(j == 0)
        def _():
            df, dq = _rms_bwd(f_ref[...], qg_ref[...], 0.5 * dy_ref[...])
            dfb_ref[...] = df.astype(bf16)
            dqg_ref[...] += dq
            dh_ref[...] = jnp.zeros_like(dh_ref)

        wdm, wg, wu = _wd_rows(wd_ref), wg_ref[0, 0], wu_ref[0, 0]
        sub = tm // FFN_BWD_SPLIT
        das = [_nt(dfb_ref[pl.ds(half * sub, sub), :], wdm) for half in range(FFN_BWD_SPLIT)]
        for half in range(FFN_BWD_SPLIT):
            rows = pl.ds(half * sub, sub)
            da = das[half]
            gv = g_ref[0, rows, :].astype(f32)
            uv = u_ref[0, rows, :].astype(f32)
            s = _sigmoid(gv)
            sg = gv * s
            a_ref[0, rows, :] = (sg * uv).astype(bf16)
            dg = (da * uv * _silu_grad(gv, s)).astype(bf16)
            du = (da * sg).astype(bf16)
            dg_ref[0, rows, :] = dg
            du_ref[0, rows, :] = du
            dh_ref[rows, :] += _nt(dg, wg) + _nt(du, wu)

        @pl.when(j == nj - 1)
        def _():
            dxn, dp = _rms_bwd(x_ref[...], pg_ref[...], dh_ref[...])
            dx_ref[...] = dy_ref[...] + dxn
            dpg_ref[...] += dp

    row = pl.BlockSpec((tm, d), lambda i, j: (i, 0))
    vec = pl.BlockSpec((1, d), lambda i, j: (0, 0))
    act = pl.BlockSpec((1, tm, h), lambda i, j: (j, i, 0))
    act_shape = jax.ShapeDtypeStruct((nj, t, h), bf16)
    return pl.pallas_call(
        _after(body, 10, deps), name="ffn_bwd", grid=(t // tm, nj),
        in_specs=[row, row, row, _lspec(pre_g, l), _lspec(post_g, l), act, act,
                  pl.BlockSpec((1, 1, d, h), lambda i, j: (j, 0, 0, 0)),
                  pl.BlockSpec((1, 1, d, h), lambda i, j: (j + nj, 0, 0, 0)),
                  pl.BlockSpec((2, 1, h // 2, d), lambda i, j: (j, 0, 0, 0))] + [ANY] * len(deps),
        out_specs=[row, row, act, act, act, vec, vec],
        out_shape=[jax.ShapeDtypeStruct((t, d), f32), jax.ShapeDtypeStruct((t, d), bf16),
                   act_shape, act_shape, act_shape,
                   jax.ShapeDtypeStruct((1, d), f32), jax.ShapeDtypeStruct((1, d), f32)],
        scratch_shapes=[pltpu.VMEM((tm, d), f32)],
        compiler_params=_params(("arbitrary", "arbitrary")),
    )(x, dy, f, pre_g, post_g, g, u, wgu, wgu, wd, *deps)


def _wgrad_cols(x, dy, buf, l, slot0):
    (t, k), (nj, _, n) = x.shape, dy.shape

    def body(x_ref, dy_ref, buf_ref, o_ref):
        o_ref[0, 0] = _tn(dy_ref[0], x_ref[...]).astype(bf16)

    return pl.pallas_call(
        body, name="wgrad_cols", grid=(nj,),
        in_specs=[pl.BlockSpec((t, k), lambda b: (0, 0)), pl.BlockSpec((1, t, n), lambda b: (b, 0, 0)), ANY],
        out_specs=pl.BlockSpec((1, 1, n, k), lambda b: (b + slot0, l, 0, 0)),
        out_shape=jax.ShapeDtypeStruct(buf.shape, bf16), input_output_aliases={2: 0},
        compiler_params=_params(("arbitrary",)),
    )(x, dy, buf)


def _wgrad_rows(x, dy, buf, l):
    (nj, t, k), (_, n) = x.shape, dy.shape

    def body(x_ref, dy_ref, buf_ref, o_ref):
        o_ref[:, 0] = _tn(x_ref[0], dy_ref[...]).astype(bf16).reshape(2, k // 2, n)

    return pl.pallas_call(
        body, name="wgrad_rows", grid=(nj,),
        in_specs=[pl.BlockSpec((1, t, k), lambda b: (b, 0, 0)), pl.BlockSpec((t, n), lambda b: (0, 0)), ANY],
        out_specs=pl.BlockSpec((2, 1, k // 2, n), lambda b: (b, l, 0, 0)),
        out_shape=jax.ShapeDtypeStruct(buf.shape, bf16), input_output_aliases={2: 0},
        compiler_params=_params(("arbitrary",)),
    )(x, dy, buf)


def _wgrad_kblocks(x, dys, buf, l):
    t, k = x.shape
    kb = k // N_DEV
    widths = [dy.shape[1] for dy in dys]
    n = sum(widths)
    nd = len(dys)

    def body(x_ref, *refs):
        dy_hbm, o_ref, dy_vmem = refs[:nd], refs[nd + 1], refs[nd + 2:]

        @pl.when(pl.program_id(0) == 0)
        def _():
            for src, dst in zip(dy_hbm, dy_vmem):
                pltpu.sync_copy(src, dst)

        off = 0
        for dst, w in zip(dy_vmem, widths):
            o_ref[0, 0, :, off:off + w] = _tn(x_ref[...], dst[...]).astype(bf16)
            off += w

    return pl.pallas_call(
        body, name="wgrad_kblocks", grid=(N_DEV,),
        in_specs=[pl.BlockSpec((t, kb), lambda s: (0, s))] + [ANY] * (nd + 1),
        out_specs=pl.BlockSpec((1, 1, kb, n), lambda s: (s, l, 0, 0)),
        out_shape=jax.ShapeDtypeStruct(buf.shape, bf16), input_output_aliases={nd + 1: 0},
        scratch_shapes=[pltpu.VMEM((t, w), bf16) for w in widths],
        compiler_params=_params(("arbitrary",)),
    )(x, *dys, buf)


def _gathered_rows(w_ref, lo, hi):
    return w_ref[:, 0, :, lo:hi].reshape(N_DEV * w_ref.shape[2], hi - lo)


def _gathered_spec(w):
    return pl.BlockSpec((N_DEV, 1) + w.shape[2:], lambda i: (0, 0, 0, 0))


def _mix_in_fwd(x, pre_g, w_in, l):
    t, d = x.shape
    tm = _tile(t, 512)

    def body(x_ref, g_ref, w_ref, hb_ref, pa_ref, pb_ref, pc_ref):
        hb = _rms_fwd(x_ref[...], g_ref[...]).astype(bf16)
        hb_ref[...] = hb
        pa_ref[...] = _nn(hb, _gathered_rows(w_ref, 0, PA_W))
        pb_ref[...] = _nn(hb, _gathered_rows(w_ref, PA_W, PA_W + PB_W))
        pc_ref[...] = _nn(hb, _gathered_rows(w_ref, PA_W + PB_W, IN_PAD))

    def row(w):
        return pl.BlockSpec((tm, w), lambda i: (i, 0))

    return pl.pallas_call(
        body, name="mix_in_fwd", grid=(t // tm,),
        in_specs=[row(d), _lspec(pre_g, l), _gathered_spec(w_in)],
        out_specs=[row(d), row(PA_W), row(PB_W), row(PC_W)],
        out_shape=[jax.ShapeDtypeStruct((t, d), bf16), jax.ShapeDtypeStruct((t, PA_W), f32),
                   jax.ShapeDtypeStruct((t, PB_W), f32), jax.ShapeDtypeStruct((t, PC_W), f32)],
        compiler_params=_params(("arbitrary",)),
    )(x, pre_g, w_in)


def _mix_in_bwd(x, dy, pre_g, dpa, dpb, dpc, w_in, l):
    t, d = x.shape
    tm = _tile(t, 512)

    def body(x_ref, dy_ref, g_ref, dpa_ref, dpb_ref, dpc_ref, w_ref, dx_ref, dg_ref):
        @pl.when(pl.program_id(0) == 0)
        def _():
            dg_ref[...] = jnp.zeros_like(dg_ref)

        wa, wb, wc = (_gathered_rows(w_ref, 0, PA_W), _gathered_rows(w_ref, PA_W, PA_W + PB_W),
                      _gathered_rows(w_ref, PA_W + PB_W, IN_PAD))
        halves = [pl.ds(k * (tm // 2), tm // 2) for k in range(2)]
        dhs = [_nt(dpa_ref[rows, :], wa) + _nt(dpb_ref[rows, :], wb) + _nt(dpc_ref[rows, :], wc) for rows in halves]
        for rows, dh in zip(halves, dhs):
            dxn, dg = _rms_bwd(x_ref[rows, :], g_ref[...], dh)
            dx_ref[rows, :] = dy_ref[rows, :] + dxn
            dg_ref[...] += dg

    def row(w):
        return pl.BlockSpec((tm, w), lambda i: (i, 0))

    vec = pl.BlockSpec((1, d), lambda i: (0, 0))
    return pl.pallas_call(
        body, name="mix_in_bwd", grid=(t // tm,),
        in_specs=[row(d), row(d), _lspec(pre_g, l), row(PA_W), row(PB_W), row(PC_W), _gathered_spec(w_in)],
        out_specs=[row(d), vec],
        out_shape=[jax.ShapeDtypeStruct((t, d), f32), jax.ShapeDtypeStruct((1, d), f32)],
        compiler_params=_params(("arbitrary",)),
    )(x, dy, pre_g, dpa, dpb, dpc, w_in)


def _mix_out_fwd(x, ya, yb, yc, post_g, w_out, l):
    t, d = x.shape
    tm = _tile(t, 512)

    def body(x_ref, ya_ref, yb_ref, yc_ref, g_ref, w_ref, y_ref, cat_ref, m_ref):
        cat_ref[:, 0:A_W] = ya_ref[...].astype(bf16)
        cat_ref[:, A_W:A_W + B_W] = yb_ref[...].astype(bf16)
        cat_ref[:, A_W + B_W:d] = yc_ref[...].astype(bf16)
        m = _nn(cat_ref[...], _gathered_rows(w_ref, 0, d))
        m_ref[...] = m
        y_ref[...] = x_ref[...] + _rms_fwd(m, g_ref[...])

    def row(w):
        return pl.BlockSpec((tm, w), lambda i: (i, 0))

    return pl.pallas_call(
        body, name="mix_out_fwd", grid=(t // tm,),
        in_specs=[row(d), row(A_W), row(B_W), row(C_W), _lspec(post_g, l), _gathered_spec(w_out)],
        out_specs=[row(d), row(d), row(d)],
        out_shape=[jax.ShapeDtypeStruct((t, d), f32), jax.ShapeDtypeStruct((t, d), bf16), jax.ShapeDtypeStruct((t, d), f32)],
        compiler_params=_params(("arbitrary",)),
    )(x, ya, yb, yc, post_g, w_out)


def _mix_out_bwd(dy, m, post_g, w_out, l, deps=()):
    t, d = m.shape
    tm = _tile(t, 512)

    def body(dy_ref, m_ref, g_ref, w_ref, dm_ref, dya_ref, dyb_ref, dyc_ref, dg_ref):
        @pl.when(pl.program_id(0) == 0)
        def _():
            dg_ref[...] = jnp.zeros_like(dg_ref)

        dm, dg = _rms_bwd(m_ref[...], g_ref[...], dy_ref[...])
        dmb = dm.astype(bf16)
        dm_ref[...] = dmb
        dg_ref[...] += dg
        dcat = _nt(dmb, _gathered_rows(w_ref, 0, d))
        dya_ref[...] = dcat[:, 0:A_W]
        dyb_ref[...] = dcat[:, A_W:A_W + B_W]
        dyc_ref[...] = dcat[:, A_W + B_W:d]

    def row(w):
        return pl.BlockSpec((tm, w), lambda i: (i, 0))

    vec = pl.BlockSpec((1, d), lambda i: (0, 0))
    return pl.pallas_call(
        _after(body, 4, deps), name="mix_out_bwd", grid=(t // tm,),
        in_specs=[row(d), row(d), _lspec(post_g, l), _gathered_spec(w_out)] + [ANY] * len(deps),
        out_specs=[row(d), row(A_W), row(B_W), row(C_W), vec],
        out_shape=[jax.ShapeDtypeStruct((t, d), bf16), jax.ShapeDtypeStruct((t, A_W), f32),
                   jax.ShapeDtypeStruct((t, B_W), f32), jax.ShapeDtypeStruct((t, C_W), f32),
                   jax.ShapeDtypeStruct((1, d), f32)],
        compiler_params=_params(("arbitrary",)),
    )(dy, m, post_g, w_out, *deps)


def _conv_fwd(buf_ref, halo, x, w, b, n):
    buf_ref[0:8, :] = halo
    buf_ref[8:8 + n, :] = x
    out = b + w[3:4, :] * x
    for k in range(3):
        out = out + w[k:k + 1, :] * buf_ref[pl.ds(5 + k, n), :]
    return out


def _conv_bwd(buf_ref, dbuf_ref, dout, dnext, w, n):
    dbuf_ref[0:n, :] = dout
    dbuf_ref[n:n + 8, :] = dnext
    dx = w[3:4, :] * dout
    dws = []
    for k in range(3):
        dx = dx + w[k:k + 1, :] * dbuf_ref[pl.ds(3 - k, n), :]
        dws.append(jnp.sum(dout * buf_ref[pl.ds(5 + k, n), :], axis=0, keepdims=True))
    dws.append(jnp.sum(dout * buf_ref[pl.ds(8, n), :], axis=0, keepdims=True))
    return dx, jnp.concatenate(dws, axis=0), jnp.sum(dout, axis=0, keepdims=True)


def _lru_gates(rec, wr, wi, br, bi, lam):
    rb = rec.astype(bf16)
    r = _sigmoid(_nn(rb, wr) + br)
    ig = _sigmoid(_nn(rb, wi) + bi)
    sp = _softplus(-lam)
    la = -LRU_C * r * sp
    a = jnp.exp(la)
    mult = jnp.sqrt(_one_minus_exp(2.0 * la))
    return rb, r, ig, sp, a, mult


def _scan_rows(a_ref, b_ref, o_ref, carry, n, reverse):
    row = lax.broadcasted_iota(jnp.int32, (8, a_ref.shape[1]), 0)
    nb = n // 8

    def step(k, carry):
        blk = (nb - 1 - k) if reverse else k
        rows = pl.ds(pl.multiple_of(blk * 8, 8), 8)
        a, b = a_ref[rows, :], b_ref[rows, :]
        for d in (1, 2, 4):
            shift = 8 - d if reverse else d
            keep = (row < 8 - d) if reverse else (row >= d)
            b = a * jnp.where(keep, pltpu.roll(b, shift, 0), 0.0) + b
            a = a * jnp.where(keep, pltpu.roll(a, shift, 0), 1.0)
        o = a * carry + b
        o_ref[rows, :] = o
        return o[0:1, :] if reverse else o[7:8, :]

    return lax.fori_loop(0, nb, step, carry, unroll=2)


N_GATES = 5
LRU_SUB = 128


def _lru_fwd(pa, conv_w, conv_b, wr, wi, br, bi, lam, l):
    t = pa.shape[0]
    tc = _tile(t, 512)

    def body(pa_ref, halo_ref, cw_ref, cb_ref, wr_ref, wi_ref, br_ref, bi_ref, lam_ref,
             ya_ref, h_ref, gates_ref, buf_ref, u_ref, carry_ref):
        i = pl.program_id(0)

        @pl.when(i == 0)
        def _():
            carry_ref[...] = jnp.zeros_like(carry_ref)

        halo = jnp.where(i > 0, halo_ref[:, A_W:PA_W], 0.0)
        rec = _conv_fwd(buf_ref, halo, pa_ref[:, A_W:PA_W], cw_ref[...], cb_ref[...], tc)
        _, r, ig, _, a, mult = _lru_gates(rec, wr_ref[...], wi_ref[...], br_ref[...], bi_ref[...], lam_ref[...])
        for k, val in enumerate((rec, r, ig, a, mult)):
            gates_ref[k] = val
        u_ref[...] = mult * (ig * rec)

        carry_ref[...] = _scan_rows(gates_ref.at[3], u_ref, h_ref, carry_ref[...], tc, reverse=False)
        ya_ref[...] = h_ref[...] * _gelu(pa_ref[:, 0:A_W])

    vec = pl.BlockSpec((1, A_W), lambda i: (0, 0))
    mat = pl.BlockSpec((A_W, A_W), lambda i: (0, 0))
    row = pl.BlockSpec((tc, A_W), lambda i: (i, 0))
    return pl.pallas_call(
        body, name="lru_fwd", grid=(t // tc,),
        in_specs=[pl.BlockSpec((tc, PA_W), lambda i: (i, 0)),
                  pl.BlockSpec((8, PA_W), lambda i: (jnp.maximum(i * (tc // 8) - 1, 0), 0)),
                  *[_lspec(a, l) for a in (conv_w, conv_b, wr, wi, br, bi, lam)]],
        out_specs=[row, row, pl.BlockSpec((N_GATES, tc, A_W), lambda i: (0, i, 0))],
        out_shape=[jax.ShapeDtypeStruct((t, A_W), f32), jax.ShapeDtypeStruct((t, A_W), f32),
                   jax.ShapeDtypeStruct((N_GATES, t, A_W), f32)],
        scratch_shapes=[pltpu.VMEM((8 + tc, A_W), f32), pltpu.VMEM((tc, A_W), f32), pltpu.VMEM((1, A_W), f32)],
        compiler_params=_params(("arbitrary",)),
    )(pa, pa, conv_w, conv_b, wr, wi, br, bi, lam)


def _lru_bwd(pa, h, gates, dya, conv_w, conv_b, wr, wi, br, bi, lam, l, deps=()):
    t = pa.shape[0]
    tc = _tile(t, 512)
    nc = t // tc

    def body(pa_ref, halo_ref, h_ref, hhalo_ref, gates_ref, dya_ref, cw_ref, cb_ref, wr_ref, wi_ref, br_ref, bi_ref,
             lam_ref, dpa_ref, dcw_ref, dcb_ref, dwr_ref, dwi_ref, dbr_ref, dbi_ref, dlam_ref,
             buf_ref, dbuf_ref, hbuf_ref, g_ref, dh_ref, carry_ref, dnext_ref, dhbuf_ref, gg_ref):
        i = pl.program_id(0)
        c = nc - 1 - i

        @pl.when(i == 0)
        def _():
            carry_ref[...] = jnp.zeros_like(carry_ref)
            dnext_ref[...] = jnp.zeros_like(dnext_ref)
            for ref in (dcw_ref, dcb_ref, dwr_ref, dwi_ref, dbr_ref, dbi_ref, dlam_ref):
                ref[...] = jnp.zeros_like(ref)

        halo = jnp.where(c > 0, halo_ref[:, A_W:PA_W], 0.0)
        cw = cw_ref[...]
        buf_ref[0:8, :] = halo
        buf_ref[8:8 + tc, :] = pa_ref[:, A_W:PA_W]
        lam = lam_ref[...]
        sp = _softplus(-lam)
        hbuf_ref[0:8, :] = jnp.where(c > 0, hhalo_ref[...], 0.0)
        hbuf_ref[8:8 + tc, :] = h_ref[...]
        gate = pa_ref[:, 0:A_W]
        dya = dya_ref[...]
        dpa_ref[:, 0:A_W] = (dya * h_ref[...] * _gelu_grad(gate)).astype(bf16)
        gg = dya * _gelu(gate)
        gg_ref[...] = gg
        g_ref[...] = gates_ref[3] * gg
        carry_in = carry_ref[...]
        carry_ref[...] = _scan_rows(gates_ref.at[3], g_ref, dh_ref, carry_in, tc, reverse=True)
        dhbuf_ref[0:tc, :] = dh_ref[...]
        dhbuf_ref[tc:tc + 8, :] = jnp.broadcast_to(carry_in, (8, A_W))
        for sb in range(tc // LRU_SUB):
            lo = sb * LRU_SUB
            rows = pl.ds(lo, LRU_SUB)
            rec, r, ig, a, mult = (gates_ref[k, rows, :] for k in range(N_GATES))
            rb = rec.astype(bf16)
            dh = gg_ref[rows, :] + dhbuf_ref[pl.ds(lo + 1, LRU_SUB), :]
            da = dh * hbuf_ref[pl.ds(lo + 7, LRU_SUB), :]
            dmult = dh * ig * rec
            dig = dh * mult * rec
            dla = da * a - dmult * (a * a) / mult
            dr = dla * (-LRU_C * sp)
            dlam_ref[...] += jnp.sum(dla * (-LRU_C * r), axis=0, keepdims=True) * (-_sigmoid(-lam))
            dpr = (dr * r * (1.0 - r))
            dpi = (dig * ig * (1.0 - ig))
            dprb, dpib = dpr.astype(bf16), dpi.astype(bf16)
            g_ref[rows, :] = dh * mult * ig + _nt(dprb, wr_ref[...]) + _nt(dpib, wi_ref[...])
            dwr_ref[...] += _tn(rb, dprb)
            dwi_ref[...] += _tn(rb, dpib)
            dbr_ref[...] += jnp.sum(dpr, axis=0, keepdims=True)
            dbi_ref[...] += jnp.sum(dpi, axis=0, keepdims=True)
        drec = g_ref[...]
        dx, dw, db = _conv_bwd(buf_ref, dbuf_ref, drec, dnext_ref[...], cw, tc)
        dnext_ref[...] = drec[0:8, :]
        dcw_ref[...] += dw
        dcb_ref[...] += db
        dpa_ref[:, A_W:PA_W] = dx.astype(bf16)

    vec = pl.BlockSpec((1, A_W), lambda i: (0, 0))
    mat = pl.BlockSpec((A_W, A_W), lambda i: (0, 0))
    cwspec = pl.BlockSpec((4, A_W), lambda i: (0, 0))

    def rev(w):
        return pl.BlockSpec((tc, w), lambda i: (nc - 1 - i, 0))

    def halo(w):
        return pl.BlockSpec((8, w), lambda i: (jnp.maximum((nc - 1 - i) * (tc // 8) - 1, 0), 0))

    chunk = pltpu.VMEM((tc, A_W), f32)
    return pl.pallas_call(
        _after(body, 13, deps), name="lru_bwd", grid=(nc,),
        in_specs=[rev(PA_W), halo(PA_W), rev(A_W), halo(A_W),
                  pl.BlockSpec((N_GATES, tc, A_W), lambda i: (0, nc - 1 - i, 0)), rev(A_W),
                  *[_lspec(a, l) for a in (conv_w, conv_b, wr, wi, br, bi, lam)]] + [ANY] * len(deps),
        out_specs=[rev(PA_W), cwspec, vec, mat, mat, vec, vec, vec],
        out_shape=[jax.ShapeDtypeStruct((t, PA_W), bf16), jax.ShapeDtypeStruct((4, A_W), f32),
                   jax.ShapeDtypeStruct((1, A_W), f32), jax.ShapeDtypeStruct((A_W, A_W), f32),
                   jax.ShapeDtypeStruct((A_W, A_W), f32), jax.ShapeDtypeStruct((1, A_W), f32),
                   jax.ShapeDtypeStruct((1, A_W), f32), jax.ShapeDtypeStruct((1, A_W), f32)],
        scratch_shapes=[pltpu.VMEM((8 + tc, A_W), f32), pltpu.VMEM((tc + 8, A_W), f32), pltpu.VMEM((8 + tc, A_W), f32),
                        chunk, chunk, pltpu.VMEM((1, A_W), f32), pltpu.VMEM((8, A_W), f32),
                        pltpu.VMEM((tc + 8, A_W), f32), chunk],
        compiler_params=_params(("arbitrary",)),
    )(pa, pa, h, h, gates, dya, conv_w, conv_b, wr, wi, br, bi, lam, *deps)


def _sgu_norm(v, g, b):
    mu = jnp.mean(v, axis=-1, keepdims=True)
    vc = v - mu
    rstd = lax.rsqrt(jnp.mean(vc * vc, axis=-1, keepdims=True) + NORM_EPS)
    vh = vc * rstd
    return vh, rstd, vh * g + b


def _sgu_mix(w_ref, vb, bias):
    grp = lax.broadcasted_iota(jnp.int32, (CHUNK, C_W), 1) // HEAD
    out = bias
    for gi in range(C_W // HEAD):
        out = out + jnp.where(grp == gi, _nn(w_ref[gi], vb), 0.0)
    return out


def _sgu_fwd(pc, ln_g, ln_b, wm, bias, l):
    t = pc.shape[0]
    tm = _tile(t, 512)

    def body(pc_ref, g_ref, b_ref, w_ref, bias_ref, yc_ref):
        for ci in range(tm // CHUNK):
            rows = pl.ds(ci * CHUNK, CHUNK)
            ge = _gelu(pc_ref[rows, :])
            _, _, vn = _sgu_norm(ge[:, C_W:PC_W], g_ref[...], b_ref[...])
            yc_ref[rows, :] = ge[:, 0:C_W] * _sgu_mix(w_ref, vn.astype(bf16), bias_ref[...])

    vec = pl.BlockSpec((1, C_W), lambda i: (0, 0))
    return pl.pallas_call(
        body, name="sgu_fwd", grid=(t // tm,),
        in_specs=[pl.BlockSpec((tm, PC_W), lambda i: (i, 0)), *[_lspec(a, l) for a in (ln_g, ln_b, wm, bias)]],
        out_specs=pl.BlockSpec((tm, C_W), lambda i: (i, 0)),
        out_shape=jax.ShapeDtypeStruct((t, C_W), f32),
        compiler_params=_params(("arbitrary",)),
    )(pc, ln_g, ln_b, wm, bias)


def _sgu_bwd(pc, dyc, ln_g, ln_b, wm, wmt, bias, l, deps=()):
    t = pc.shape[0]
    tm = _tile(t, 512)

    def body(pc_ref, dyc_ref, g_ref, b_ref, w_ref, wt_ref, bias_ref, dpc_ref, dw_ref, dbias_ref, dg_ref, db_ref):
        @pl.when(pl.program_id(0) == 0)
        def _():
            for ref in (dw_ref, dbias_ref, dg_ref, db_ref):
                ref[...] = jnp.zeros_like(ref)

        grp = lax.broadcasted_iota(jnp.int32, (CHUNK, C_W), 1) // HEAD
        for ci in range(tm // CHUNK):
            rows = pl.ds(ci * CHUNK, CHUNK)
            x = pc_ref[rows, :]
            ge = _gelu(x)
            gv = g_ref[...]
            vh, rstd, vn = _sgu_norm(ge[:, C_W:PC_W], gv, b_ref[...])
            vb = vn.astype(bf16)
            mixed = _sgu_mix(w_ref, vb, bias_ref[...])
            dyc = dyc_ref[rows, :]
            du = dyc * mixed
            dmix = dyc * ge[:, 0:C_W]
            dmb = dmix.astype(bf16)
            dvn = jnp.zeros((CHUNK, C_W), f32)
            for gi in range(C_W // HEAD):
                dvn = dvn + jnp.where(grp == gi, _nn(wt_ref[gi], dmb), 0.0)
                dw_ref[gi] += _nt(jnp.where(grp == gi, dmix, 0.0).astype(bf16), vb)
            dbias_ref[...] += dmix
            dg_ref[...] += jnp.sum(dvn * vh, axis=0, keepdims=True)
            db_ref[...] += jnp.sum(dvn, axis=0, keepdims=True)
            dvh = dvn * gv
            dv = rstd * (dvh - jnp.mean(dvh, axis=-1, keepdims=True) - vh * jnp.mean(dvh * vh, axis=-1, keepdims=True))
            gg = _gelu_grad(x)
            dpc_ref[rows, 0:C_W] = (du * gg[:, 0:C_W]).astype(bf16)
            dpc_ref[rows, C_W:PC_W] = (dv * gg[:, C_W:PC_W]).astype(bf16)

    vec = pl.BlockSpec((1, C_W), lambda i: (0, 0))
    wspec = pl.BlockSpec((4, CHUNK, CHUNK), lambda i: (0, 0, 0))
    bspec = pl.BlockSpec((CHUNK, C_W), lambda i: (0, 0))
    return pl.pallas_call(
        _after(body, 7, deps), name="sgu_bwd", grid=(t // tm,),
        in_specs=[pl.BlockSpec((tm, PC_W), lambda i: (i, 0)), pl.BlockSpec((tm, C_W), lambda i: (i, 0)),
                  *[_lspec(a, l) for a in (ln_g, ln_b, wm, wmt, bias)]] + [ANY] * len(deps),
        out_specs=[pl.BlockSpec((tm, PC_W), lambda i: (i, 0)), wspec, bspec, vec, vec],
        out_shape=[jax.ShapeDtypeStruct((t, PC_W), bf16), jax.ShapeDtypeStruct((4, CHUNK, CHUNK), f32),
                   jax.ShapeDtypeStruct((CHUNK, C_W), f32), jax.ShapeDtypeStruct((1, C_W), f32),
                   jax.ShapeDtypeStruct((1, C_W), f32)],
        compiler_params=_params(("arbitrary",)),
    )(pc, dyc, ln_g, ln_b, wm, wmt, bias, *deps)


N_PAIR = B_W // 128
HEADS_PER_GROUP = 3


def _pair_groups(p):
    return (2 * p) // HEADS_PER_GROUP, (2 * p + 1) // HEADS_PER_GROUP


def _ssd_chunk(pb_ref, halo, buf_ref, cw, cb, dtb, alog):
    z = pb_ref[:, 0:B_W]
    pre = _conv_fwd(buf_ref, halo, pb_ref[:, B_W:B_W + XBC_W], cw, cb, CHUNK)
    sg = _sigmoid(pre)
    xbc = pre * sg
    xs = xbc[:, 0:B_W]
    bm = [xbc[:, B_W + k * B_STATE:B_W + (k + 1) * B_STATE] for k in range(2)]
    cm = [xbc[:, B_W + (2 + k) * B_STATE:B_W + (3 + k) * B_STATE] for k in range(2)]
    dtin = pb_ref[:, B_W + XBC_W:PB_W] + dtb
    dt = _softplus(dtin)
    a = -jnp.exp(alog)
    cs = _cumsum_rows(dt * a)
    return dict(z=z, pre=pre, sg=sg, xs=xs, bm=bm, cm=cm, dtin=dtin, dt=dt, a=a, cs=cs,
                ecs=jnp.exp(cs), ds=jnp.exp(cs[CHUNK - 1:CHUNK, :] - cs), xdt=xs * dt,
                bmb=[v.astype(bf16) for v in bm], cmb=[v.astype(bf16) for v in cm])


def _ssd_decay(cs_pair, half):
    cst = cs_pair.T
    lane0 = HEAD * half
    csc = jnp.broadcast_to(cs_pair[:, lane0:lane0 + 1], (CHUNK, CHUNK))
    csr = cst[lane0:lane0 + 1, :]
    tri = lax.broadcasted_iota(jnp.int32, (CHUNK, CHUNK), 0) >= lax.broadcasted_iota(jnp.int32, (CHUNK, CHUNK), 1)
    return jnp.exp(jnp.where(tri, csc - csr, NEG_BIG)), cst


def _ssd_fwd(pb, conv_w, conv_b, dtb, alog, dskip, norm_g, l):
    t = pb.shape[0]
    nc = t // CHUNK

    def body(pb_ref, halo_ref, cw_ref, cb_ref, dtb_ref, alog_ref, d_ref, ng_ref, yb_ref, yp_ref, sp_ref, buf_ref, s_ref):
        i = pl.program_id(0)

        @pl.when(i == 0)
        def _():
            s_ref[...] = jnp.zeros_like(s_ref)

        halo = jnp.where(i > 0, halo_ref[:, B_W:B_W + XBC_W], 0.0)
        q = _ssd_chunk(pb_ref, halo, buf_ref, cw_ref[...], cb_ref[...], dtb_ref[...], alog_ref[...])
        sp_ref[0] = s_ref[...]
        lane = lax.broadcasted_iota(jnp.int32, (CHUNK, 128), 1)
        rowi = lax.broadcasted_iota(jnp.int32, (128, B_STATE), 0)
        cb_mat = [_nt(q["cmb"][k], q["bmb"][k]) for k in range(2)]
        xd = q["xdt"] * q["ds"]
        for p in range(N_PAIR):
            cols = slice(128 * p, 128 * (p + 1))
            g_lo, g_hi = _pair_groups(p)
            cs_p, xdt_p = q["cs"][:, cols], q["xdt"][:, cols]
            s_p = s_ref[cols, :]
            s_pb = s_p.astype(bf16)
            y_p = jnp.zeros((CHUNK, 128), f32)
            for half, grp in ((0, g_lo), (1, g_hi)):
                lm, cst = _ssd_decay(cs_p, half)
                mb = (cb_mat[grp] * lm).astype(bf16)
                sel = (lane < HEAD) if half == 0 else (lane >= HEAD)
                y_p = y_p + _nn(mb, jnp.where(sel, xdt_p, 0.0).astype(bf16))
            off_lo = _nt(q["cmb"][g_lo], s_pb)
            off = off_lo if g_lo == g_hi else jnp.where(lane < HEAD, off_lo, _nt(q["cmb"][g_hi], s_pb))
            y_p = y_p + off * q["ecs"][:, cols] + q["xs"][:, cols] * d_ref[:, cols]
            yp_ref[:, cols] = y_p
            xd_pb = xd[:, cols].astype(bf16)
            upd_lo = _tn(xd_pb, q["bmb"][g_lo])
            upd = upd_lo if g_lo == g_hi else jnp.where(rowi < HEAD, upd_lo, _tn(xd_pb, q["bmb"][g_hi]))
            cd = jnp.exp(jnp.broadcast_to(cst[:, CHUNK - 1:CHUNK], (128, B_STATE)))
            s_ref[cols, :] = cd * s_p + upd
        z = q["z"]
        yg = yp_ref[...] * (z * _sigmoid(z))
        yb_ref[...] = _rms_fwd(yg, ng_ref[...])

    vec = pl.BlockSpec((1, B_W), lambda i: (0, 0))
    row = pl.BlockSpec((CHUNK, B_W), lambda i: (i, 0))
    return pl.pallas_call(
        body, name="ssd_fwd", grid=(nc,),
        in_specs=[pl.BlockSpec((CHUNK, PB_W), lambda i: (i, 0)),
                  pl.BlockSpec((8, PB_W), lambda i: (jnp.maximum(i * (CHUNK // 8) - 1, 0), 0)),
                  *[_lspec(a, l) for a in (conv_w, conv_b, dtb, alog, dskip, norm_g)]],
        out_specs=[row, row, pl.BlockSpec((1, B_W, B_STATE), lambda i: (i, 0, 0))],
        out_shape=[jax.ShapeDtypeStruct((t, B_W), f32), jax.ShapeDtypeStruct((t, B_W), f32),
                   jax.ShapeDtypeStruct((nc, B_W, B_STATE), f32)],
        scratch_shapes=[pltpu.VMEM((8 + CHUNK, XBC_W), f32), pltpu.VMEM((B_W, B_STATE), f32)],
        compiler_params=_params(("arbitrary",)),
    )(pb, pb, conv_w, conv_b, dtb, alog, dskip, norm_g)


def _ssd_bwd(pb, yp, sprev, dyb, conv_w, conv_b, dtb, alog, dskip, norm_g, l):
    t = pb.shape[0]
    nc = t // CHUNK

    def body(pb_ref, halo_ref, yp_ref, sp_ref, dyb_ref, cw_ref, cb_ref, dtb_ref, alog_ref, d_ref, ng_ref,
             dpb_ref, dcw_ref, dcb_ref, ddtb_ref, dalog_ref, dd_ref, dng_ref,
             buf_ref, dbuf_ref, ds_ref, dnext_ref, dxbc_ref, dcs_ref, dxdt_ref):
        i = pl.program_id(0)
        c = nc - 1 - i

        @pl.when(i == 0)
        def _():
            ds_ref[...] = jnp.zeros_like(ds_ref)
            dnext_ref[...] = jnp.zeros_like(dnext_ref)
            for ref in (dcw_ref, dcb_ref, ddtb_ref, dalog_ref, dd_ref, dng_ref):
                ref[...] = jnp.zeros_like(ref)

        halo = jnp.where(c > 0, halo_ref[:, B_W:B_W + XBC_W], 0.0)
        cw = cw_ref[...]
        q = _ssd_chunk(pb_ref, halo, buf_ref, cw, cb_ref[...], dtb_ref[...], alog_ref[...])
        z, xs, dt, a, ecs, dsd, xdt =q["z"], q["xs"], q["dt"], q["a"], q["ecs"], q["ds"], q["xdt"]
        sz = _sigmoid(z)
        siluz = z * sz
        yp = yp_ref[...]
        dyg, dng = _rms_bwd(yp * siluz, ng_ref[...], dyb_ref[...])
        dng_ref[...] += dng
        dy = dyg * siluz
        dpb_ref[:, 0:B_W] = (dyg * yp * _silu_grad(z, sz)).astype(bf16)
        dd_ref[...] += jnp.sum(dy * xs, axis=0, keepdims=True)
        g1 = dy * ecs
        lane = lax.broadcasted_iota(jnp.int32, (CHUNK, 128), 1)
        rowi = lax.broadcasted_iota(jnp.int32, (128, B_STATE), 0)
        rowc = lax.broadcasted_iota(jnp.int32, (CHUNK, 128), 0)
        cb_mat = [_nt(q["cmb"][k], q["bmb"][k]) for k in range(2)]
        d_cb = [jnp.zeros((CHUNK, CHUNK), f32) for _ in range(2)]
        d_b = [jnp.zeros((CHUNK, B_STATE), f32) for _ in range(2)]
        d_c = [jnp.zeros((CHUNK, B_STATE), f32) for _ in range(2)]
        for p in range(N_PAIR):
            cols = slice(128 * p, 128 * (p + 1))
            g_lo, g_hi = _pair_groups(p)
            lo, hi = lane < HEAD, lane >= HEAD
            cs_p, xdt_p, dy_p, ds_p, g1_p = q["cs"][:, cols], xdt[:, cols], dy[:, cols], dsd[:, cols], g1[:, cols]
            s_p = sp_ref[0, cols, :]
            s_pb = s_p.astype(bf16)
            dsn = ds_ref[cols, :]
            dsnb = dsn.astype(bf16)
            g1b = g1_p.astype(bf16)
            off_lo = _nt(q["cmb"][g_lo], s_pb)
            off = off_lo if g_lo == g_hi else jnp.where(lo, off_lo, _nt(q["cmb"][g_hi], s_pb))
            dcs_p = dy_p * off * ecs[:, cols]
            dsp_lo = _tn(g1b, q["cmb"][g_lo])
            dsp = dsp_lo if g_lo == g_hi else jnp.where(rowi < HEAD, dsp_lo, _tn(g1b, q["cmb"][g_hi]))
            dx_lo = _nt(q["bmb"][g_lo], dsnb)
            dxd = dx_lo if g_lo == g_hi else jnp.where(lo, dx_lo, _nt(q["bmb"][g_hi], dsnb))
            xd_p = xdt_p * ds_p
            if g_lo == g_hi:
                d_c[g_lo] = d_c[g_lo] + _nn(g1b, s_pb)
                d_b[g_lo] = d_b[g_lo] + _nn(xd_p.astype(bf16), dsnb)
            else:
                d_c[g_lo] = d_c[g_lo] + _nn(jnp.where(lo, g1_p, 0.0).astype(bf16), s_pb)
                d_c[g_hi] = d_c[g_hi] + _nn(jnp.where(hi, g1_p, 0.0).astype(bf16), s_pb)
                d_b[g_lo] = d_b[g_lo] + _nn(jnp.where(lo, xd_p, 0.0).astype(bf16), dsnb)
                d_b[g_hi] = d_b[g_hi] + _nn(jnp.where(hi, xd_p, 0.0).astype(bf16), dsnb)
            dxdt_p = dxd * ds_p
            t2 = dxd * xdt_p * ds_p
            dcs_p = dcs_p - t2
            dlast = jnp.sum(t2, axis=0, keepdims=True)
            cst = None
            for half, grp in ((0, g_lo), (1, g_hi)):
                sel = lo if half == 0 else hi
                lm, cst = _ssd_decay(cs_p, half)
                m = cb_mat[grp] * lm
                dyh = jnp.where(sel, dy_p, 0.0).astype(bf16)
                xdh = jnp.where(sel, xdt_p, 0.0).astype(bf16)
                dm = _nt(dyh, xdh)
                pm = dm * m
                col = jnp.sum(pm, axis=1, keepdims=True) - jnp.sum(pm.T, axis=1, keepdims=True)
                dcs_p = dcs_p + jnp.where(lane == HEAD * half, col, 0.0)
                d_cb[grp] = d_cb[grp] + dm * lm
                dxdt_p = dxdt_p + _tn(m.astype(bf16), dyh)
            cdcol = jnp.exp(jnp.broadcast_to(cst[:, CHUNK - 1:CHUNK], (128, B_STATE)))
            ds_ref[cols, :] = cdcol * dsn + dsp
            dcd_row = jnp.sum((dsn * s_p).T, axis=0, keepdims=True)
            dlast = dlast + dcd_row * ecs[CHUNK - 1:CHUNK, cols]
            dcs_ref[:, cols] = dcs_p + jnp.where(rowc == CHUNK - 1, dlast, 0.0)
            dxdt_ref[:, cols] = dxdt_p
        for k in range(2):
            dcbb = d_cb[k].astype(bf16)
            d_c[k] = d_c[k] + _nn(dcbb, q["bmb"][k])
            d_b[k] = d_b[k] + _tn(dcbb, q["cmb"][k])
            dxbc_ref[:, B_W + k * B_STATE:B_W + (k + 1) * B_STATE] = d_b[k]
            dxbc_ref[:, B_W + (2 + k) * B_STATE:B_W + (3 + k) * B_STATE] = d_c[k]
        dxdt = dxdt_ref[...]
        dxbc_ref[:, 0:B_W] = dy * d_ref[...] + dxdt * dt
        dcs = dcs_ref[...]
        dad = jnp.sum(dcs, axis=0, keepdims=True) - _cumsum_rows(dcs) + dcs
        ddt = dxdt * xs + dad * a
        dalog_ref[...] += jnp.sum(dad * dt, axis=0, keepdims=True) * a
        dtraw = ddt * _sigmoid(q["dtin"])
        ddtb_ref[...] += jnp.sum(dtraw, axis=0, keepdims=True)
        dpb_ref[:, B_W + XBC_W:PB_W] = dtraw.astype(bf16)
        dpre = dxbc_ref[...] * _silu_grad(q["pre"], q["sg"])
        dx, dw, db = _conv_bwd(buf_ref, dbuf_ref, dpre, dnext_ref[...], cw, CHUNK)
        dnext_ref[...] = dpre[0:8, :]
        dcw_ref[...] += dw
        dcb_ref[...] += db
        dpb_ref[:, B_W:B_W + XBC_W] = dx.astype(bf16)

    vec = pl.BlockSpec((1, B_W), lambda i: (0, 0))
    cwspec = pl.BlockSpec((4, XBC_W), lambda i: (0, 0))
    cbspec = pl.BlockSpec((1, XBC_W), lambda i: (0, 0))

    def rev(w):
        return pl.BlockSpec((CHUNK, w), lambda i: (nc - 1 - i, 0))

    vshape = jax.ShapeDtypeStruct((1, B_W), f32)
    return pl.pallas_call(
        body, name="ssd_bwd", grid=(nc,),
        in_specs=[rev(PB_W), pl.BlockSpec((8, PB_W), lambda i: (jnp.maximum((nc - 1 - i) * (CHUNK // 8) - 1, 0), 0)),
                  rev(B_W), pl.BlockSpec((1, B_W, B_STATE), lambda i: (nc - 1 - i, 0, 0)), rev(B_W),
                  *[_lspec(a, l) for a in (conv_w, conv_b, dtb, alog, dskip, norm_g)]],
        out_specs=[rev(PB_W), cwspec, cbspec, vec, vec, vec, vec],
        out_shape=[jax.ShapeDtypeStruct((t, PB_W), bf16), jax.ShapeDtypeStruct((4, XBC_W), f32),
                   jax.ShapeDtypeStruct((1, XBC_W), f32), vshape, vshape, vshape, vshape],
        scratch_shapes=[pltpu.VMEM((8 + CHUNK, XBC_W), f32), pltpu.VMEM((CHUNK + 8, XBC_W), f32),
                        pltpu.VMEM((B_W, B_STATE), f32), pltpu.VMEM((8, XBC_W), f32),
                        pltpu.VMEM((CHUNK, XBC_W), f32), pltpu.VMEM((CHUNK, B_W), f32), pltpu.VMEM((CHUNK, B_W), f32)],
        compiler_params=_params(("arbitrary",)),
    )(pb, pb, yp, sprev, dyb, conv_w, conv_b, dtb, alog, dskip, norm_g)


def _loss_fwd(y, target):
    t, d = y.shape
    tm = _tile(t, 512)

    def body(y_ref, t_ref, dy_ref, loss_ref):
        @pl.when(pl.program_id(0) == 0)
        def _():
            loss_ref[...] = jnp.zeros_like(loss_ref)

        e = y_ref[...] - t_ref[...]
        dy_ref[...] = e * (1.0 / d)
        per_tok = jnp.mean(e * e, axis=-1, keepdims=True)
        loss_ref[...] += 0.5 * jnp.sum(per_tok, axis=0, keepdims=True)

    row = pl.BlockSpec((tm, d), lambda i: (i, 0))
    return pl.pallas_call(
        body, name="loss_fwd", grid=(t // tm,), in_specs=[row, row],
        out_specs=[row, pl.BlockSpec((1, 128), lambda i: (0, 0))],
        out_shape=[jax.ShapeDtypeStruct((t, d), f32), jax.ShapeDtypeStruct((1, 128), f32)],
        compiler_params=_params(("arbitrary",)),
    )(y, target)


def _row_tile(r):
    return 512 if r % 512 == 0 else r


def _pair_add(g, r, c_dev):
    _, nl, rows, cols = g.shape
    tr = _row_tile(rows)

    def body(c_ref, g_ref, r_ref, o_ref):
        o_ref[...] = (g_ref[...].astype(f32) + r_ref[...].astype(f32)).astype(bf16)

    blk = (None, None, tr, cols)
    return pl.pallas_call(
        body, name="pair_add",
        grid_spec=pltpu.PrefetchScalarGridSpec(
            num_scalar_prefetch=1, grid=(4, nl, rows // tr),
            in_specs=[pl.BlockSpec(blk, lambda b, l, i, c: (2 * b + c[0], l, i, 0)),
                      pl.BlockSpec(blk, lambda b, l, i, c: (b, l, i, 0))],
            out_specs=pl.BlockSpec(blk, lambda b, l, i, c: (b, l, i, 0))),
        out_shape=jax.ShapeDtypeStruct(r.shape, bf16),
        compiler_params=_params(("arbitrary", "arbitrary", "arbitrary")),
    )(c_dev, g, r)


def _grad_sum(s, q, b_dev):
    _, nl, rows, cols = s.shape
    tr = _row_tile(rows)

    def body(b_ref, s_ref, q0_ref, q1_ref, q2_ref, o_ref):
        o_ref[...] = ((s_ref[...].astype(f32) + q0_ref[...].astype(f32)) + q1_ref[...].astype(f32)) + q2_ref[...].astype(f32)

    blk = (None, None, tr, cols)

    def qspec(k):
        return pl.BlockSpec(blk, lambda l, i, b: (k, l, i, 0))

    return pl.pallas_call(
        body, name="grad_sum",
        grid_spec=pltpu.PrefetchScalarGridSpec(
            num_scalar_prefetch=1, grid=(nl, rows // tr),
            in_specs=[pl.BlockSpec(blk, lambda l, i, b: (b[0], l, i, 0)), qspec(0), qspec(1), qspec(2)],
            out_specs=pl.BlockSpec((None, tr, cols), lambda l, i, b: (l, i, 0))),
        out_shape=jax.ShapeDtypeStruct(s.shape[1:], f32),
        compiler_params=_params(("arbitrary", "arbitrary")),
    )(b_dev, s, q, q, q)


def _sum_devices(parts):
    n, rows, cols = parts.shape
    tr = _row_tile(rows)

    def body(p_ref, o_ref):
        acc = p_ref[0]
        for k in range(1, n):
            acc = acc + p_ref[k]
        o_ref[...] = acc

    return pl.pallas_call(
        body, name="sum_devices", grid=(rows // tr,),
        in_specs=[pl.BlockSpec((n, tr, cols), lambda i: (0, i, 0))],
        out_specs=pl.BlockSpec((tr, cols), lambda i: (i, 0)),
        out_shape=jax.ShapeDtypeStruct((rows, cols), f32),
        compiler_params=_params(("arbitrary",)),
    )(parts)


def _adamw(w, m, v, g):
    nl, rows, cols = w.shape
    tr = _row_tile(rows)
    tc = 128 if (tr == rows and rows * cols * 4 > ADAMW_BLOCK_BYTES and cols % 128 == 0) else cols

    def body(w_ref, m_ref, v_ref, g_ref, d_ref, nm_ref, nv_ref):
        d_ref[...], nm_ref[...], nv_ref[...] = _adamw_math(w_ref[...], m_ref[...], v_ref[...], g_ref[...])

    blk = pl.BlockSpec((None, tr, tc), lambda l, i, c: (l, i, c))
    shape = jax.ShapeDtypeStruct(w.shape, f32)
    return pl.pallas_call(
        body, name="adamw", grid=(nl, rows // tr, cols // tc), in_specs=[blk] * 4, out_specs=[blk] * 3,
        out_shape=[shape] * 3, compiler_params=_params(("arbitrary", "arbitrary", "arbitrary")),
    )(w, m, v, g)


def _adamw_math(w, m, v, g):
    nm = ADAM_B1 * m + (1.0 - ADAM_B1) * g
    nv = ADAM_B2 * v + (1.0 - ADAM_B2) * (g * g)
    m_hat = nm / (1.0 - ADAM_B1 ** ADAM_STEP)
    v_hat = nv / (1.0 - ADAM_B2 ** ADAM_STEP)
    return -ADAM_LR * (m_hat / (jnp.sqrt(v_hat) + ADAM_EPS) + ADAM_WD * w), nm, nv


def _adamw_layer(w, m, v, s, q, b_dev, outs, l, deps=()):
    _, rows, cols = w.shape
    tr = _row_tile(rows)

    def body(b_ref, w_ref, m_ref, v_ref, s_ref, q0_ref, q1_ref, q2_ref, o0, o1, o2, o3, g_ref, d_ref, nm_ref, nv_ref):
        g = ((s_ref[...].astype(f32) + q0_ref[...].astype(f32)) + q1_ref[...].astype(f32)) + q2_ref[...].astype(f32)
        g_ref[...] = g
        d_ref[...], nm_ref[...], nv_ref[...] = _adamw_math(w_ref[...], m_ref[...], v_ref[...], g)

    wspec = pl.BlockSpec((None, tr, cols), lambda i, b: (l, i, 0))
    blk = (None, None, tr, cols)

    def qspec(k):
        return pl.BlockSpec(blk, lambda i, b: (k, 0, i, 0))

    shape = jax.ShapeDtypeStruct(w.shape, f32)
    return pl.pallas_call(
        _after(body, 12, deps), name="adamw_layer",
        grid_spec=pltpu.PrefetchScalarGridSpec(
            num_scalar_prefetch=1, grid=(rows // tr,),
            in_specs=[wspec] * 3 + [pl.BlockSpec(blk, lambda i, b: (b[0], 0, i, 0)), qspec(0), qspec(1), qspec(2)]
            + [ANY] * (4 + len(deps)),
            out_specs=[wspec] * 4),
        out_shape=[shape] * 4, input_output_aliases={8 + k: k for k in range(4)},
        compiler_params=_params(("arbitrary",)),
    )(b_dev, w, m, v, s, q, q, q, *outs, *deps)


def _place():
    return lax.axis_index("x"), lax.axis_index("y"), lax.axis_index("c")


def _all_gather(shards, deps=()):
    n = len(shards)
    nd = len(deps)

    def body(*refs):
        src, dst = refs[:n], refs[n:2 * n]
        send_sems, recv_sems, local_sems = refs[2 * n:]
        x, y, c = _place()
        me, sibling = (x, y, c), (x, y, 1 - c)
        chips = [(1 - x, y), (x, 1 - y), (1 - x, 1 - y)]

        def copy(a, k, block, to, from_shard=False):
            px, py, pc = block
            rows = dst[a].at[4 * px + 2 * py + pc]
            return pltpu.make_async_remote_copy(
                src_ref=src[a] if from_shard else rows, dst_ref=rows,
                send_sem=send_sems.at[a, k], recv_sem=recv_sems.at[a, k], device_id=to, device_id_type=MESH)

        mine = [pltpu.make_async_copy(src[a], dst[a].at[4 * x + 2 * y + c], local_sems.at[a]) for a in range(n)]
        for cp in mine:
            cp.start()
        first = []
        for a in range(n):
            first.append(copy(a, 0, me, sibling, True))
            first += [copy(a, 1 + j, me, (*chip, c), True) for j, chip in enumerate(chips)]
        for cp in first:
            cp.start()
        passed = []
        for j, chip in enumerate(chips):
            for a in range(n):
                copy(a, 1 + j, (*chip, c), me).wait_recv()
                fwd = copy(a, 4 + j, (*chip, c), sibling)
                fwd.start()
                passed.append(fwd)
        for a in range(n):
            copy(a, 0, sibling, me).wait_recv()
            for j, chip in enumerate(chips):
                copy(a, 4 + j, (*chip, 1 - c), me).wait_recv()
        for cp in first + passed:
            cp.wait_send()
        for cp in mine:
            cp.wait()

    return pl.pallas_call(
        _after(body, n, deps), name="all_gather", in_specs=[ANY] * (n + nd), out_specs=[ANY] * n,
        out_shape=[jax.ShapeDtypeStruct((N_DEV,) + s.shape, s.dtype) for s in shards],
        scratch_shapes=[pltpu.SemaphoreType.DMA((n, 7)), pltpu.SemaphoreType.DMA((n, 7)), pltpu.SemaphoreType.DMA((n,))],
    )(*shards, *deps)


HBM = pl.BlockSpec(memory_space=pltpu.HBM)
SEM = pl.BlockSpec(memory_space=pltpu.SEMAPHORE)
_EFFECT = pltpu.SideEffectType.DATAFLOW_SIDE_EFFECTING


def _split_start(name, srcs, dsts, sem_shape, plan):
    ns, nb = len(srcs), len(srcs) + len(dsts)

    def body(*refs):
        send_sems, recv_sems = refs[nb], refs[nb + 1]
        for cp in plan(refs[:ns], refs[ns:nb], send_sems, recv_sems):
            cp.start()
        refs[-1][...] = jnp.zeros_like(refs[-1])

    bufs = list(srcs) + list(dsts)
    return pl.pallas_call(
        body, name=name,
        out_shape=(pltpu.SemaphoreType.DMA(sem_shape), pltpu.SemaphoreType.DMA(sem_shape),
                   *[pltpu.HBM(a.shape, a.dtype) for a in bufs], jax.ShapeDtypeStruct((8, 128), f32)),
        in_specs=[HBM] * nb, out_specs=(SEM, SEM, *[HBM] * nb, pl.BlockSpec(memory_space=pltpu.VMEM)),
        input_output_aliases={i: 2 + i for i in range(nb)},
        compiler_params=pltpu.CompilerParams(has_side_effects=_EFFECT),
    )(*[pltpu.with_memory_space_constraint(a, pltpu.HBM) for a in bufs])


def _split_wait(name, started, ns, plan, after):
    send_sems, recv_sems = started[0], started[1]
    bufs = list(started[2:-1])
    nb = len(bufs)
    after = list(after) if isinstance(after, (list, tuple)) else [after]

    def body(*refs):
        for cp in plan(refs[:ns], refs[ns:nb], refs[nb], refs[nb + 1]):
            cp.wait_send()
            cp.wait_recv()

    return pl.pallas_call(
        body, name=name, out_shape=tuple(pltpu.HBM(a.shape, a.dtype) for a in bufs),
        in_specs=[HBM] * nb + [SEM, SEM] + [ANY] * len(after), out_specs=tuple([HBM] * nb),
        input_output_aliases={i: i for i in range(nb)},
        compiler_params=pltpu.CompilerParams(has_side_effects=_EFFECT),
    )(*bufs, send_sems, recv_sems, *after)


def _remote(src, dst, send_sem, recv_sem, to):
    return pltpu.make_async_remote_copy(src_ref=src, dst_ref=dst, send_sem=send_sem, recv_sem=recv_sem,
                                        device_id=to, device_id_type=MESH)


def _gather_plan(src, dst, send_sems, recv_sems):
    x, y, c = _place()
    peers = [(x, y, 1 - c), (1 - x, y, c), (x, 1 - y, c), (1 - x, 1 - y, c)]
    copies = []
    for a in range(len(dst)):
        rows = dst[a].at[4 * x + 2 * y + c]
        copies += [_remote(rows, rows, send_sems.at[4 * a + k], recv_sems.at[4 * a + k], peer) for k, peer in enumerate(peers)]
    return copies


def _pair_plan(src, dst, send_sems, recv_sems):
    x, y, c = _place()
    return [_remote(src[a].at[2 * b + (1 - c)], dst[a].at[b], send_sems.at[4 * a + b], recv_sems.at[4 * a + b], (x, y, 1 - c))
            for a in range(len(src)) for b in range(4)]


def _chips_plan(src, dst, send_sems, recv_sems):
    x, y, c = _place()
    chips = [(1 - x, y), (x, 1 - y), (1 - x, 1 - y)]
    return [_remote(src[a].at[2 * px + py], dst[a].at[j], send_sems.at[3 * a + j], recv_sems.at[3 * a + j], (px, py, c))
            for a in range(len(src)) for j, (px, py) in enumerate(chips)]


def _forward_plan(src, dst, send_sems, recv_sems):
    x, y, c = _place()
    copies = []
    for a in range(len(dst)):
        for j, (px, py) in enumerate([(1 - x, y), (x, 1 - y), (1 - x, 1 - y)]):
            rows = dst[a].at[4 * px + 2 * py + c]
            copies.append(_remote(rows, rows, send_sems.at[3 * a + j], recv_sems.at[3 * a + j], (x, y, 1 - c)))
    return copies


def _gather_finish(bufs):
    n = len(bufs)

    def body(*refs):
        dst = refs[n:2 * n]
        send_sems, recv_sems = refs[2 * n:]
        x, y, c = _place()
        chips = [(1 - x, y), (x, 1 - y), (1 - x, 1 - y)]
        passed = []
        for a in range(n):
            for j, (px, py) in enumerate(chips):
                rows = dst[a].at[4 * px + 2 * py + c]
                passed.append(_remote(rows, rows, send_sems.at[a, j], recv_sems.at[a, j], (x, y, 1 - c)))
        for cp in passed:
            cp.start()
        for cp in passed:
            cp.wait_send()
        for a in range(n):
            for j, (px, py) in enumerate(chips):
                rows = dst[a].at[4 * px + 2 * py + (1 - c)]
                _remote(rows, rows, send_sems.at[a, j], recv_sems.at[a, j], (x, y, 1 - c)).wait_recv()

    return pl.pallas_call(
        body, name="gather_finish", in_specs=[ANY] * n, out_specs=[ANY] * n,
        out_shape=[jax.ShapeDtypeStruct(b.shape, b.dtype) for b in bufs],
        input_output_aliases={a: a for a in range(n)},
        scratch_shapes=[pltpu.SemaphoreType.DMA((n, 3)), pltpu.SemaphoreType.DMA((n, 3))],
    )(*bufs)


def _place_shards(mats, l, dev):
    n = len(mats)

    def body(dev_ref, *refs):
        for a in range(n):
            refs[n + a][...] = refs[a][...].astype(bf16)

    return pl.pallas_call(
        body, name="place_shards",
        grid_spec=pltpu.PrefetchScalarGridSpec(
            num_scalar_prefetch=1, grid=(1,),
            in_specs=[pl.BlockSpec((None,) + m.shape[1:], lambda i, dv: (l, 0, 0)) for m in mats],
            out_specs=[pl.BlockSpec((None, None) + m.shape[1:], lambda i, dv: (dv[0], 0, 0, 0)) for m in mats]),
        out_shape=[jax.ShapeDtypeStruct((N_DEV, 1) + m.shape[1:], bf16) for m in mats],
        compiler_params=_params(("arbitrary",)),
    )(dev, *mats)


BIG = ("ffn1_w_gu", "ffn1_w_down", "mix_w_in", "mix_w_out", "ffn2_w_gu", "ffn2_w_down")
SHARDED_CONV = ("lru_conv_w", "ssd_conv_w")
REPLICATED = ("ffn1_pre_g", "ffn1_post_g", "mix_pre_g", "mix_post_g", "lru_conv_b", "lru_w_r", "lru_b_r", "lru_w_i",
              "lru_b_i", "lru_lambda", "ssd_conv_b", "ssd_dt_bias", "ssd_a_log", "ssd_d", "ssd_norm_g", "sgu_ln_g",
              "sgu_ln_b", "sgu_w_s", "sgu_b_s", "ffn2_pre_g", "ffn2_post_g")
WEIGHTS = ("ffn1_pre_g", "ffn1_post_g", "ffn1_w_gu", "ffn1_w_down", "mix_pre_g", "mix_post_g", "mix_w_in", "mix_w_out",
           "lru_conv_w", "lru_conv_b", "lru_w_r", "lru_b_r", "lru_w_i", "lru_b_i", "lru_lambda", "ssd_conv_w",
           "ssd_conv_b", "ssd_dt_bias", "ssd_a_log", "ssd_d", "ssd_norm_g", "sgu_ln_g", "sgu_ln_b", "sgu_w_s", "sgu_b_s",
           "ffn2_pre_g", "ffn2_post_g", "ffn2_w_gu", "ffn2_w_down")
DT_LO = PA_W + B_W + XBC_W
N_HEADS = B_W // HEAD
PACK_COLS = 1024


def _size(shape):
    size = 1
    for dim in shape:
        size *= dim
    return size


def _pack_rows(shape):
    return -(-_size(shape) // PACK_COLS)


def _pack(arrays):
    pieces = [jnp.pad(a.reshape(-1), (0, _pack_rows(a.shape) * PACK_COLS - _size(a.shape))) for a in arrays]
    rows = sum(_pack_rows(a.shape) for a in arrays)
    if rows % 8:
        pieces.append(jnp.zeros(((8 - rows % 8) * PACK_COLS,), f32))
    return jnp.concatenate(pieces).reshape(-1, PACK_COLS)


def _unpack(packed, shapes):
    out, row = [], 0
    for s in shapes:
        nr = _pack_rows(s)
        out.append(packed[row:row + nr].reshape(-1)[:_size(s)].reshape(s))
        row += nr
    return out


def _widen_w_in(w):
    return jnp.concatenate([w[..., :DT_LO], jnp.repeat(w[..., DT_LO:DT_LO + N_HEADS], HEAD, axis=-1),
                            w[..., DT_LO + N_HEADS:]], axis=-1)


def _narrow_w_in_grad(g):
    dt = g[..., DT_LO:DT_LO + B_W]
    dt = dt.reshape(dt.shape[:-1] + (N_HEADS, HEAD)).sum(-1)
    return jnp.concatenate([g[..., :DT_LO], dt, g[..., DT_LO + B_W:]], axis=-1)


def _per_head(a):
    return a.reshape(a.shape[:-1] + (N_HEADS, HEAD)).sum(-1)


def kernel(x, ffn1_pre_g, ffn1_post_g, ffn1_w_gu, ffn1_w_down, mix_pre_g, mix_post_g, mix_w_in, mix_w_out, lru_conv_w, lru_conv_b, lru_w_r, lru_b_r, lru_w_i, lru_b_i, lru_lambda, ssd_conv_w, ssd_conv_b, ssd_dt_bias, ssd_a_log, ssd_d, ssd_norm_g, sgu_ln_g, sgu_ln_b, sgu_w_s, sgu_b_s, ffn2_pre_g, ffn2_post_g, ffn2_w_gu, ffn2_w_down, loss_target, m_ffn1_pre_g, m_ffn1_post_g, m_ffn1_w_gu, m_ffn1_w_down, m_mix_pre_g, m_mix_post_g, m_mix_w_in, m_mix_w_out, m_lru_conv_w, m_lru_conv_b, m_lru_w_r, m_lru_b_r, m_lru_w_i, m_lru_b_i, m_lru_lambda, m_ssd_conv_w, m_ssd_conv_b, m_ssd_dt_bias, m_ssd_a_log, m_ssd_d, m_ssd_norm_g, m_sgu_ln_g, m_sgu_ln_b, m_sgu_w_s, m_sgu_b_s, m_ffn2_pre_g, m_ffn2_post_g, m_ffn2_w_gu, m_ffn2_w_down, v_ffn1_pre_g, v_ffn1_post_g, v_ffn1_w_gu, v_ffn1_w_down, v_mix_pre_g, v_mix_post_g, v_mix_w_in, v_mix_w_out, v_lru_conv_w, v_lru_conv_b, v_lru_w_r, v_lru_b_r, v_lru_w_i, v_lru_b_i, v_lru_lambda, v_ssd_conv_w, v_ssd_conv_b, v_ssd_dt_bias, v_ssd_a_log, v_ssd_d, v_ssd_norm_g, v_sgu_ln_g, v_sgu_ln_b, v_sgu_w_s, v_sgu_b_s, v_ffn2_pre_g, v_ffn2_post_g, v_ffn2_w_gu, v_ffn2_w_down):
    given = dict(locals())
    w = {n: given[n] for n in WEIGHTS}
    mom = {n: given["m_" + n] for n in WEIGHTS}
    var = {n: given["v_" + n] for n in WEIGHTS}
    nl = ffn1_pre_g.shape[0]
    _, t, d = x.shape
    xi, yi, ci = _place()
    dev = 4 * xi + 2 * yi + ci
    c_dev = jnp.reshape(ci, (1,)).astype(jnp.int32)
    b_dev = jnp.reshape(2 * xi + yi, (1,)).astype(jnp.int32)

    conv_shapes = [lru_conv_w.shape, ssd_conv_w.shape]
    shards = [ffn1_w_gu, ffn1_w_down, _widen_w_in(mix_w_in), mix_w_out, ffn2_w_gu, ffn2_w_down]
    nbig = len(shards)
    dev_arr = jnp.reshape(dev, (1,)).astype(jnp.int32)
    conv_pack = _pack([lru_conv_w, ssd_conv_w])
    conv_buf = lax.dynamic_update_slice_in_dim(jnp.zeros((N_DEV,) + conv_pack.shape, f32), conv_pack[None], dev, axis=0)
    def gather_groups(l):
        return [(0, 1), (2, 3), (4, 5)] if l == 0 else [tuple(range(nbig))]

    gather_started = {}
    for l in range(nl):
        for gi, idx in enumerate(gather_groups(l)):
            bufs = list(_place_shards([shards[i] for i in idx], l, dev_arr)) + ([conv_buf] if (l, gi) == (0, 1) else [])
            gather_started[l, gi] = _split_start(f"gather_start_{l}_{gi}", [], bufs, (4 * len(bufs),), _gather_plan)

    def finish_gather(l, gi, after):
        waited = _split_wait(f"gather_wait_{l}_{gi}", gather_started[l, gi], 0, _gather_plan, after)
        return _gather_finish(list(waited))

    def conv_taps(conv_all):
        full = []
        for k, shape in enumerate(conv_shapes):
            per_dev = jnp.stack([_unpack(conv_all[s], conv_shapes)[k] for s in range(N_DEV)], axis=2)
            full.append(per_dev.reshape(shape[0], shape[1], N_DEV * shape[2]))
        return full

    def vec(a):
        return a.reshape(nl, 1, -1)

    def per_channel(a):
        return jnp.repeat(a, HEAD, axis=-1).reshape(nl, 1, B_W)

    eye = jnp.eye(A_W // HEAD, dtype=f32)

    def block_diag(a):
        return jnp.einsum("lhij,hg->lhigj", a, eye).reshape(nl, A_W, A_W).astype(bf16)

    causal = jnp.tril(jnp.ones((CHUNK, CHUNK), dtype=bool))
    p = dict(
        ffn1_pre=vec(ffn1_pre_g), ffn1_post=vec(ffn1_post_g), mix_pre=vec(mix_pre_g), mix_post=vec(mix_post_g),
        ffn2_pre=vec(ffn2_pre_g), ffn2_post=vec(ffn2_post_g),
        lru=(vec(lru_conv_b), block_diag(lru_w_r), block_diag(lru_w_i), vec(lru_b_r), vec(lru_b_i), vec(lru_lambda)),
        ssd=(vec(ssd_conv_b), per_channel(ssd_dt_bias), per_channel(ssd_a_log), per_channel(ssd_d), vec(ssd_norm_g)),
    )
    wm = jnp.where(causal, sgu_w_s, 0.0).astype(bf16)
    sgu_bias = jnp.repeat(jnp.swapaxes(sgu_b_s, 1, 2), HEAD, axis=2)
    sgu_f = (vec(sgu_ln_g), vec(sgu_ln_b), wm, sgu_bias)
    sgu_b = (vec(sgu_ln_g), vec(sgu_ln_b), wm, jnp.swapaxes(wm, 2, 3), sgu_bias)

    small_names = REPLICATED + SHARDED_CONV
    small_state = [_pack([src[n] for n in small_names])[None] for src in (w, mom, var)]
    prepared = [a for v in p.values() for a in (v if isinstance(v, tuple) else (v,))] + list(sgu_b) + small_state

    xs = x.reshape(t, d)
    saved, gathered, early_forward = [], [], {}
    for l in range(nl):
        x0 = xs
        if l == 0:
            wgu1, wd1 = finish_gather(0, 0, [x0] + prepared)
            deps = tuple(started[-1] for key, started in gather_started.items() if key != (0, 0))
        elif l in early_forward:
            wgu1, wd1, win, wout, wgu2, wd2 = _split_wait(f"forward_wait_{l}", early_forward[l], 0, _forward_plan, x0)
            deps = ()
        else:
            wgu1, wd1, win, wout, wgu2, wd2 = finish_gather(l, 0, x0)
            deps = ()
        x1, hb1, g1, u1, f1 = _ffn_fwd(x0, p["ffn1_pre"], p["ffn1_post"], wgu1, wd1, l, deps)
        if l == 0:
            win, wout, conv_all = finish_gather(0, 1, x1)
            lru_cw, ssd_cw = conv_taps(conv_all)
            p["lru"], p["ssd"] = (lru_cw,) + p["lru"], (ssd_cw,) + p["ssd"]
        hbm, pa, pb, pc = _mix_in_fwd(x1, p["mix_pre"], win, l)
        ya, h, gates = _lru_fwd(pa, *p["lru"], l)
        yb, yp, sp = _ssd_fwd(pb, *p["ssd"], l)
        yc = _sgu_fwd(pc, *sgu_f, l)
        x2, cat, m = _mix_out_fwd(x1, ya, yb, yc, p["mix_post"], wout, l)
        deps = ()
        if l == 0:
            wgu2, wd2 = finish_gather(0, 2, x2)
        elif l + 1 < nl:
            waited = _split_wait(f"gather_wait_{l + 1}_0", gather_started[l + 1, 0], 0, _gather_plan, x2)
            early_forward[l + 1] = _split_start(f"forward_start_{l + 1}", [], list(waited), (3 * nbig,), _forward_plan)
            deps = (early_forward[l + 1][-1],)
        xs, hb2, g2, u2, f2 = _ffn_fwd(x2, p["ffn2_pre"], p["ffn2_post"], wgu2, wd2, l, deps)
        gathered.append((wgu1, wd1, win, wout, wgu2, wd2))
        saved.append((x0, hb1, g1, u1, f1, x1, hbm, pa, pb, pc, h, gates, yp, sp, cat, m, x2, hb2, g2, u2, f2))
    dy, loss_part = _loss_fwd(xs, loss_target.reshape(t, d))
    loss = lax.psum(loss_part[0, 0], ("x", "y", "c"))

    small = {n: [None] * nl for n in REPLICATED + SHARDED_CONV}
    grads, delta, new_m, new_v = {}, {}, {}, {}
    fused = [n for n in BIG if n != "mix_w_in"]

    def oriented(a, n):
        return jnp.swapaxes(a, 1, 2) if n.endswith("w_gu") else a

    opt_in = {n: tuple(oriented(src[n], n) for src in (w, mom, var)) for n in fused}
    opt_out = {n: tuple(lax.empty(opt_in[n][0].shape, f32) for _ in range(4)) for n in fused}
    w_in_grads = [None] * nl
    grad_shapes = {n: (s.shape[2], s.shape[1]) if n.endswith("w_gu") else s.shape[1:] for n, s in zip(BIG, shards)}

    def start_pair(tag, lp, names, gbuf):
        landing = [lax.empty((4, 1) + grad_shapes[n], bf16) for n in names]
        started = _split_start(f"pair_start_{tag}", [gbuf[n] for n in names], landing, (4 * len(names),), _pair_plan)
        return tag, lp, names, started

    def finish_pair(pending, after):
        tag, lp, names, started = pending
        k = len(names)
        done = _split_wait(f"pair_wait_{tag}", started, k, _pair_plan, after)
        sums = [_pair_add(g, r, c_dev) for g, r in zip(done[:k], done[k:])]
        landing = [lax.empty((3,) + s.shape[1:], bf16) for s in sums]
        return tag, lp, names, _split_start(f"chips_start_{tag}", sums, landing, (3 * k,), _chips_plan)

    def finish_chips(pending, after, deps=()):
        tag, lp, names, started = pending
        k = len(names)
        done = _split_wait(f"chips_wait_{tag}", started, k, _chips_plan, after)
        last = None
        for n, s, q in zip(names, done[:k], done[k:]):
            if n == "mix_w_in":
                w_in_grads[lp] = last = _grad_sum(s, q, b_dev)
            else:
                opt_out[n] = tuple(_adamw_layer(*opt_in[n], s, q, b_dev, opt_out[n], lp, deps))
                last = opt_out[n][0]
        return last

    early = ("ffn2_w_gu", "ffn2_w_down", "mix_w_out")
    late = ("mix_w_in", "ffn1_w_gu", "ffn1_w_down")
    pending_pair = pending_chips = early_pair = early_chips = upper_started = None
    deferred = []
    names = REPLICATED + SHARDED_CONV
    assert nl > 1
    for l in reversed(range(nl)):
        x0, hb1, g1, u1, f1, x1, hbm, pa, pb, pc, h, gates, yp, sp, cat, m, x2, hb2, g2, u2, f2 = saved[l]
        wgu1, wd1, win, wout, wgu2, wd2 = gathered[l][:nbig]
        gbuf ={n: lax.empty((N_DEV, 1) + grad_shapes[n], bf16) for n in BIG}
        deps = () if pending_pair is None else (pending_pair[3][-1],)
        if l == 0:
            deps += (upper_started[-1],)
        dx2, dfb, act, dg, du, dpre, dpost = _ffn_bwd(x2, dy, f2, p["ffn2_pre"], p["ffn2_post"], g2, u2, wgu2, wd2, l, deps)
        small["ffn2_pre_g"][l], small["ffn2_post_g"][l] = dpre[0], dpost[0]
        gbuf["ffn2_w_gu"] = _wgrad_cols(hb2, dg, gbuf["ffn2_w_gu"], 0, 0)
        gbuf["ffn2_w_gu"] = _wgrad_cols(hb2, du, gbuf["ffn2_w_gu"], 0, dg.shape[0])
        gbuf["ffn2_w_down"] = _wgrad_rows(act, dfb, gbuf["ffn2_w_down"], 0)
        deps = ()
        if pending_pair is not None:
            pending_chips = finish_pair(pending_pair, dx2)
            deps = (pending_chips[3][-1],)

        dm, dya, dyb, dyc, dpost = _mix_out_bwd(dx2, m, p["mix_post"], wout, l, deps)
        small["mix_post_g"][l] = dpost[0]
        gbuf["mix_w_out"] = _wgrad_kblocks(cat, [dm], gbuf["mix_w_out"], 0)
        deps = ()
        if l == 0:
            early_pair = start_pair("0a", 0, early, gbuf)
            deps = (early_pair[3][-1],)
        dpc, dws, dbias, dlg, dlb = _sgu_bwd(pc, dyc, *sgu_b, l, deps)
        small["sgu_w_s"][l] = jnp.where(causal, dws, 0.0)
        small["sgu_b_s"][l] = dbias.reshape(CHUNK, C_W // HEAD, HEAD).sum(-1).T
        small["sgu_ln_g"][l], small["sgu_ln_b"][l] = dlg[0], dlb[0]
        dpb, dcw, dcb, ddtb, dalog, ddsk, dng = _ssd_bwd(pb, yp, sp, dyb, *p["ssd"], l)
        small["ssd_conv_w"][l], small["ssd_conv_b"][l], small["ssd_norm_g"][l] = dcw, dcb[0], dng[0]
        small["ssd_dt_bias"][l], small["ssd_a_log"][l], small["ssd_d"][l] = _per_head(ddtb[0]), _per_head(dalog[0]), _per_head(ddsk[0])
        deps = ()
        if l == 0:
            early_chips = finish_pair(early_pair, dpb)
            deps = (early_chips[3][-1],)
        dpa, dcw, dcb, dwr, dwi, dbr, dbi, dlam = _lru_bwd(pa, h, gates, dya, *p["lru"], l, deps)
        small["lru_conv_w"][l], small["lru_conv_b"][l], small["lru_lambda"][l] = dcw, dcb[0], dlam[0]
        small["lru_b_r"][l], small["lru_b_i"][l] = dbr[0], dbi[0]
        heads = range(A_W // HEAD)
        small["lru_w_r"][l] = jnp.stack([dwr[HEAD * i:HEAD * (i + 1), HEAD * i:HEAD * (i + 1)] for i in heads])
        small["lru_w_i"][l] = jnp.stack([dwi[HEAD * i:HEAD * (i + 1), HEAD * i:HEAD * (i + 1)] for i in heads])
        dx1, dpre = _mix_in_bwd(x1, dx2, p["mix_pre"], dpa, dpb, dpc, win, l)
        small["mix_pre_g"][l] = dpre[0]
        gbuf["mix_w_in"] = _wgrad_kblocks(hbm, [dpa, dpb, dpc], gbuf["mix_w_in"], 0)

        dy, dfb, act, dg, du, dpre, dpost = _ffn_bwd(x0, dx1, f1, p["ffn1_pre"], p["ffn1_post"], g1, u1, wgu1, wd1, l)
        small["ffn1_pre_g"][l], small["ffn1_post_g"][l] = dpre[0], dpost[0]
        gbuf["ffn1_w_gu"] = _wgrad_cols(hb1, dg, gbuf["ffn1_w_gu"], 0, 0)
        gbuf["ffn1_w_gu"] = _wgrad_cols(hb1, du, gbuf["ffn1_w_gu"], 0, dg.shape[0])
        gbuf["ffn1_w_down"] = _wgrad_rows(act, dfb, gbuf["ffn1_w_down"], 0)
        if pending_chips is not None:
            deferred.append(pending_chips)
            pending_chips = None
        pending_pair = start_pair(f"{l}", l, late if l == 0 else BIG, gbuf)
        if l == 1:
            upper = [jnp.stack(small[n][1:]) for n in names]
            upper_pack = _pack(upper)
            upper_buf = lax.dynamic_update_slice_in_dim(
                jnp.zeros((N_DEV,) + upper_pack.shape, f32), upper_pack[None], dev, axis=0)
            upper_started = _split_start("small_start", [], [upper_buf], (4,), _gather_plan)
    grad_x = dy.reshape(x.shape)

    lower = [jnp.stack(small[n][:1]) for n in names]
    lower_total = _sum_devices(_all_gather([_pack(lower)], (pending_pair[3][-1],))[0])
    late_chips = finish_pair(pending_pair, lower_total)
    order = lower_total
    for pending in deferred + [early_chips]:
        order = finish_chips(pending, order, (late_chips[3][-1],))
    upper_all = _gather_finish(list(_split_wait("small_wait", upper_started, 0, _gather_plan, order)))[0]
    upper_total = _sum_devices(upper_all)
    finish_chips(late_chips, [upper_total] + [opt_out[n][0] for n in fused] + [g for g in w_in_grads if g is not None])
    full = {n: jnp.concatenate([lo, up], axis=0) for n, lo, up in zip(
        names, _unpack(lower_total, [a.shape for a in lower]), _unpack(upper_total, [a.shape for a in upper]))}

    for n in fused:
        grads[n], delta[n], new_m[n], new_v[n] = (oriented(a, n) for a in opt_out[n])
    grads["mix_w_in"] = _narrow_w_in_grad(jnp.concatenate(w_in_grads, axis=0))
    delta["mix_w_in"], new_m["mix_w_in"], new_v["mix_w_in"] = _adamw(
        w["mix_w_in"], mom["mix_w_in"], var["mix_w_in"], grads["mix_w_in"])
    for n in REPLICATED:
        grads[n] = full[n]
    for n in SHARDED_CONV:
        cols = w[n].shape[2]
        grads[n] = lax.dynamic_slice_in_dim(full[n], dev * cols, cols, axis=2)
    shapes = [w[n].shape for n in names]
    packs = small_state + [_pack([grads[n] for n in names])[None]]
    for dst, packed in zip((delta, new_m, new_v), _adamw(*packs)):
        dst.update(zip(names, _unpack(packed[0], shapes)))

    return (loss, grad_x, *[grads[n] for n in WEIGHTS], *[delta[n] for n in WEIGHTS],
            *[new_m[n] for n in WEIGHTS], *[new_v[n] for n in WEIGHTS])
```

```python
import functools

import jax
import jax.numpy as jnp
from jax import lax
from jax.experimental import pallas as pl
from jax.experimental.pallas import tpu as pltpu

f32, bf16 = jnp.float32, jnp.bfloat16
MESH = pl.DeviceIdType.MESH
ANY = pl.BlockSpec(memory_space=pl.ANY)

N_DEV = 8
NORM_EPS = 1e-6
LRU_C = 8.0
CHUNK = 128
HEAD = 64
A_W, B_W, C_W = 384, 384, 256
B_STATE = 128
XBC_W = B_W + 4 * B_STATE
PA_W, PB_W, PC_W = 2 * A_W, B_W + XBC_W + B_W, 2 * C_W
IN_PAD = PA_W + PB_W + PC_W
ADAM_LR, ADAM_B1, ADAM_B2, ADAM_EPS, ADAM_WD, ADAM_STEP = 0.001, 0.9, 0.999, 1e-08, 0.01, 10
VMEM_LIMIT_BYTES = 56 * 1024 * 1024
FFN_BWD_SPLIT = 2
ADAMW_BLOCK_BYTES = 2 * 1024 * 1024
NEG_BIG = -1e30


def _params(sem=None):
    return pltpu.CompilerParams(dimension_semantics=sem, vmem_limit_bytes=VMEM_LIMIT_BYTES)


def _nn(a, b):
    return jnp.dot(a, b, preferred_element_type=f32)


def _nt(a, b):
    return lax.dot_general(a, b, (((1,), (1,)), ((), ())), preferred_element_type=f32)


def _tn(a, b):
    return lax.dot_general(a, b, (((0,), (0,)), ((), ())), preferred_element_type=f32)


def _sigmoid(x):
    return 0.5 * jnp.tanh(0.5 * x) + 0.5


def _softplus(x):
    return jnp.maximum(x, 0.0) + jnp.log(1.0 + jnp.exp(-jnp.abs(x)))


_GELU_C0, _GELU_C1 = 0.7978845608028654, 0.044715


def _gelu(x):
    t = jnp.tanh(_GELU_C0 * (x + _GELU_C1 * x * x * x))
    return 0.5 * x * (1.0 + t)


def _gelu_grad(x):
    t = jnp.tanh(_GELU_C0 * (x + _GELU_C1 * x * x * x))
    return 0.5 * (1.0 + t) + 0.5 * x * (1.0 - t * t) * _GELU_C0 * (1.0 + 3.0 * _GELU_C1 * x * x)


def _silu_grad(x, s):
    return s * (1.0 + x * (1.0 - s))


def _rms_fwd(x, g):
    r = lax.rsqrt(jnp.mean(x * x, axis=-1, keepdims=True) + NORM_EPS)
    return x * r * g


def _rms_bwd(x, g, dy):
    r = lax.rsqrt(jnp.mean(x * x, axis=-1, keepdims=True) + NORM_EPS)
    xh = x * r
    dxh = dy * g
    dx = r * (dxh - xh * jnp.mean(dxh * xh, axis=-1, keepdims=True))
    return dx, jnp.sum(dy * xh, axis=0, keepdims=True)


def _one_minus_exp(x):
    series = -x * (1.0 + x * (0.5 + x * (1.0 / 6.0 + x * (1.0 / 24.0))))
    return jnp.where(x > -0.01, series, 1.0 - jnp.exp(x))


def _cumsum_rows(x):
    row = lax.broadcasted_iota(jnp.int32, x.shape, 0)
    d = 1
    while d < x.shape[0]:
        x = x + jnp.where(row >= d, pltpu.roll(x, d, 0), 0.0)
        d *= 2
    return x


def _tile(t, cap):
    tm = min(cap, t)
    assert t % tm == 0
    return tm


def _after(body, n_in, deps):
    def wrapped(*refs):
        return body(*refs[:n_in], *refs[n_in + len(deps):])
    return wrapped


def _lspec(a, l):
    return pl.BlockSpec((None,) + a.shape[1:], lambda *_: (l,) + (0,) * (a.ndim - 1))


def _wd_rows(wd_ref):
    return wd_ref[:, 0].reshape(2 * wd_ref.shape[2], wd_ref.shape[3])


def _ffn_fwd(x, pre_g, post_g, wgu, wd, l, deps=()):
    t, d = x.shape
    nb, _, _, h = wgu.shape
    nj = nb // 2
    tm = _tile(t, 512)

    def body(x_ref, pg_ref, qg_ref, wg_ref, wu_ref, wd_ref, y_ref, hb_ref, g_ref, u_ref, f_ref, acc_ref):
        j = pl.program_id(1)

        @pl.when(j == 0)
        def _():
            hb_ref[...] = _rms_fwd(x_ref[...], pg_ref[...]).astype(bf16)
            acc_ref[...] = jnp.zeros_like(acc_ref)

        hb = hb_ref[...]
        g = _nn(hb, wg_ref[0, 0])
        u = _nn(hb, wu_ref[0, 0])
        g_ref[0] = g.astype(bf16)
        u_ref[0] = u.astype(bf16)
        a = (g * _sigmoid(g) * u).astype(bf16)
        acc_ref[...] += _nn(a, _wd_rows(wd_ref))

        @pl.when(j == nj - 1)
        def _():
            f = acc_ref[...]
            f_ref[...] = f
            y_ref[...] = x_ref[...] + 0.5 * _rms_fwd(f, qg_ref[...])

    row = pl.BlockSpec((tm, d), lambda i, j: (i, 0))
    vec = pl.BlockSpec((1, d), lambda i, j: (0, 0))
    act = pl.BlockSpec((1, tm, h), lambda i, j: (j, i, 0))
    return pl.pallas_call(
        _after(body, 6, deps), name="ffn_fwd", grid=(t // tm, nj),
        in_specs=[row, _lspec(pre_g, l), _lspec(post_g, l),
                  pl.BlockSpec((1, 1, d, h), lambda i, j: (j, 0, 0, 0)),
                  pl.BlockSpec((1, 1, d, h), lambda i, j: (j + nj, 0, 0, 0)),
                  pl.BlockSpec((2, 1, h // 2, d), lambda i, j: (j, 0, 0, 0))] + [ANY] * len(deps),
        out_specs=[row, row, act, act, row],
        out_shape=[jax.ShapeDtypeStruct((t, d), f32), jax.ShapeDtypeStruct((t, d), bf16),
                   jax.ShapeDtypeStruct((nj, t, h), bf16), jax.ShapeDtypeStruct((nj, t, h), bf16),
                   jax.ShapeDtypeStruct((t, d), f32)],
        scratch_shapes=[pltpu.VMEM((tm, d), f32)],
        compiler_params=_params(("arbitrary", "arbitrary")),
    )(x, pre_g, post_g, wgu, wgu, wd, *deps)


def _ffn_bwd(x, dy, f, pre_g, post_g, g, u, wgu, wd, l, deps=()):
    t, d = x.shape
    nj, _, h = g.shape
    tm = _tile(t, 512)

    def body(x_ref, dy_ref, f_ref, pg_ref, qg_ref, g_ref, u_ref, wg_ref, wu_ref, wd_ref,
             dx_ref, dfb_ref, a_ref, dg_ref, du_ref, dpg_ref, dqg_ref, dh_ref):
        i, j = pl.program_id(0), pl.program_id(1)

        @pl.when((i == 0) & (j == 0))
        def _():
            dpg_ref[...] = jnp.zeros_like(dpg_ref)
            dqg_ref[...] = jnp.zeros_like(dqg_ref)

        @pl.when(j == 0)
        def _():
            df, dq = _rms_bwd(f_ref[...], qg_ref[...], 0.5 * dy_ref[...])
            dfb_ref[...] = df.astype(bf16)
            dqg_ref[...] += dq
            dh_ref[...] = jnp.zeros_like(dh_ref)

        wdm, wg, wu = _wd_rows(wd_ref), wg_ref[0, 0], wu_ref[0, 0]
        sub = tm // FFN_BWD_SPLIT
        das = [_nt(dfb_ref[pl.ds(half * sub, sub), :], wdm) for half in range(FFN_BWD_SPLIT)]
        for half in range(FFN_BWD_SPLIT):
            rows = pl.ds(half * sub, sub)
            da = das[half]
            gv = g_ref[0, rows, :].astype(f32)
            uv = u_ref[0, rows, :].astype(f32)
            s = _sigmoid(gv)
            sg = gv * s
            a_ref[0, rows, :] = (sg * uv).astype(bf16)
            dg = (da * uv * _silu_grad(gv, s)).astype(bf16)
            du = (da * sg).astype(bf16)
            dg_ref[0, rows, :] = dg
            du_ref[0, rows, :] = du
            dh_ref[rows, :] += _nt(dg, wg) + _nt(du, wu)

        @pl.when(j == nj - 1)
        def _():
            dxn, dp = _rms_bwd(x_ref[...], pg_ref[...], dh_ref[...])
            dx_ref[...] = dy_ref[...] + dxn
            dpg_ref[...] += dp

    row = pl.BlockSpec((tm, d), lambda i, j: (i, 0))
    vec = pl.BlockSpec((1, d), lambda i, j: (0, 0))
    act = pl.BlockSpec((1, tm, h), lambda i, j: (j, i, 0))
    act_shape = jax.ShapeDtypeStruct((nj, t, h), bf16)
    return pl.pallas_call(
        _after(body, 10, deps), name="ffn_bwd", grid=(t // tm, nj),
        in_specs=[row, row, row, _lspec(pre_g, l), _lspec(post_g, l), act, act,
                  pl.BlockSpec((1, 1, d, h), lambda i, j: (j, 0, 0, 0)),
                  pl.BlockSpec((1, 1, d, h), lambda i, j: (j + nj, 0, 0, 0)),
                  pl.BlockSpec((2, 1, h // 2, d), lambda i, j: (j, 0, 0, 0))] + [ANY] * len(deps),
        out_specs=[row, row, act, act, act, vec, vec],
        out_shape=[jax.ShapeDtypeStruct((t, d), f32), jax.ShapeDtypeStruct((t, d), bf16),
                   act_shape, act_shape, act_shape,
                   jax.ShapeDtypeStruct((1, d), f32), jax.ShapeDtypeStruct((1, d), f32)],
        scratch_shapes=[pltpu.VMEM((tm, d), f32)],
        compiler_params=_params(("arbitrary", "arbitrary")),
    )(x, dy, f, pre_g, post_g, g, u, wgu, wgu, wd, *deps)


def _wgrad_cols(x, dy, buf, l, slot0):
    (t, k), (nj, _, n) = x.shape, dy.shape

    def body(x_ref, dy_ref, buf_ref, o_ref):
        o_ref[0, 0] = _tn(dy_ref[0], x_ref[...]).astype(bf16)

    return pl.pallas_call(
        body, name="wgrad_cols", grid=(nj,),
        in_specs=[pl.BlockSpec((t, k), lambda b: (0, 0)), pl.BlockSpec((1, t, n), lambda b: (b, 0, 0)), ANY],
        out_specs=pl.BlockSpec((1, 1, n, k), lambda b: (b + slot0, l, 0, 0)),
        out_shape=jax.ShapeDtypeStruct(buf.shape, bf16), input_output_aliases={2: 0},
        compiler_params=_params(("arbitrary",)),
    )(x, dy, buf)


def _wgrad_rows(x, dy, buf, l):
    (nj, t, k), (_, n) = x.shape, dy.shape

    def body(x_ref, dy_ref, buf_ref, o_ref):
        o_ref[:, 0] = _tn(x_ref[0], dy_ref[...]).astype(bf16).reshape(2, k // 2, n)

    return pl.pallas_call(
        body, name="wgrad_rows", grid=(nj,),
        in_specs=[pl.BlockSpec((1, t, k), lambda b: (b, 0, 0)), pl.BlockSpec((t, n), lambda b: (0, 0)), ANY],
        out_specs=pl.BlockSpec((2, 1, k // 2, n), lambda b: (b, l, 0, 0)),
        out_shape=jax.ShapeDtypeStruct(buf.shape, bf16), input_output_aliases={2: 0},
        compiler_params=_params(("arbitrary",)),
    )(x, dy, buf)


def _wgrad_kblocks(x, dys, buf, l):
    t, k = x.shape
    kb = k // N_DEV
    widths = [dy.shape[1] for dy in dys]
    n = sum(widths)
    nd = len(dys)

    def body(x_ref, *refs):
        dy_hbm, o_ref, dy_vmem, sems = refs[:nd], refs[nd + 1], refs[nd + 2:2 * nd + 2], refs[2 * nd + 2]
        first = pl.program_id(0) == 0
        loads = [pltpu.make_async_copy(src, dst, sems.at[p]) for p, (src, dst) in enumerate(zip(dy_hbm, dy_vmem))]

        @pl.when(first)
        def _():
            for cp in loads:
                cp.start()

        off = 0
        for cp, dst, w in zip(loads, dy_vmem, widths):
            @pl.when(first)
            def _():
                cp.wait()

            o_ref[0, 0, :, off:off + w] = _tn(x_ref[...], dst[...]).astype(bf16)
            off += w

    return pl.pallas_call(
        body, name="wgrad_kblocks", grid=(N_DEV,),
        in_specs=[pl.BlockSpec((t, kb), lambda s: (0, s))] + [ANY] * (nd + 1),
        out_specs=pl.BlockSpec((1, 1, kb, n), lambda s: (s, l, 0, 0)),
        out_shape=jax.ShapeDtypeStruct(buf.shape, bf16), input_output_aliases={nd + 1: 0},
        scratch_shapes=[pltpu.VMEM((t, w), bf16) for w in widths] + [pltpu.SemaphoreType.DMA((nd,))],
        compiler_params=_params(("arbitrary",)),
    )(x, *dys, buf)


def _gathered_rows(w_ref, lo, hi):
    return w_ref[:, 0, :, lo:hi].reshape(N_DEV * w_ref.shape[2], hi - lo)


def _gathered_spec(w):
    return pl.BlockSpec((N_DEV, 1) + w.shape[2:], lambda i: (0, 0, 0, 0))


def _mix_in_fwd(x, pre_g, w_in, l):
    t, d = x.shape
    tm = _tile(t, 512)

    def body(x_ref, g_ref, w_ref, hb_ref, pa_ref, pb_ref, pc_ref):
        hb = _rms_fwd(x_ref[...], g_ref[...]).astype(bf16)
        hb_ref[...] = hb
        pa_ref[...] = _nn(hb, _gathered_rows(w_ref, 0, PA_W))
        pb_ref[...] = _nn(hb, _gathered_rows(w_ref, PA_W, PA_W + PB_W))
        pc_ref[...] = _nn(hb, _gathered_rows(w_ref, PA_W + PB_W, IN_PAD))

    def row(w):
        return pl.BlockSpec((tm, w), lambda i: (i, 0))

    return pl.pallas_call(
        body, name="mix_in_fwd", grid=(t // tm,),
        in_specs=[row(d), _lspec(pre_g, l), _gathered_spec(w_in)],
        out_specs=[row(d), row(PA_W), row(PB_W), row(PC_W)],
        out_shape=[jax.ShapeDtypeStruct((t, d), bf16), jax.ShapeDtypeStruct((t, PA_W), f32),
                   jax.ShapeDtypeStruct((t, PB_W), f32), jax.ShapeDtypeStruct((t, PC_W), f32)],
        compiler_params=_params(("arbitrary",)),
    )(x, pre_g, w_in)


def _mix_in_bwd(x, dy, pre_g, dpa, dpb, dpc, w_in, l):
    t, d = x.shape
    tm = _tile(t, 512)

    def body(x_ref, dy_ref, g_ref, dpa_ref, dpb_ref, dpc_ref, w_ref, dx_ref, dg_ref):
        @pl.when(pl.program_id(0) == 0)
        def _():
            dg_ref[...] = jnp.zeros_like(dg_ref)

        wa, wb, wc = (_gathered_rows(w_ref, 0, PA_W), _gathered_rows(w_ref, PA_W, PA_W + PB_W),
                      _gathered_rows(w_ref, PA_W + PB_W, IN_PAD))
        halves = [pl.ds(k * (tm // 2), tm // 2) for k in range(2)]
        dhs = [_nt(dpa_ref[rows, :], wa) + _nt(dpb_ref[rows, :], wb) + _nt(dpc_ref[rows, :], wc) for rows in halves]
        for rows, dh in zip(halves, dhs):
            dxn, dg = _rms_bwd(x_ref[rows, :], g_ref[...], dh)
            dx_ref[rows, :] = dy_ref[rows, :] + dxn
            dg_ref[...] += dg

    def row(w):
        return pl.BlockSpec((tm, w), lambda i: (i, 0))

    vec = pl.BlockSpec((1, d), lambda i: (0, 0))
    return pl.pallas_call(
        body, name="mix_in_bwd", grid=(t // tm,),
        in_specs=[row(d), row(d), _lspec(pre_g, l), row(PA_W), row(PB_W), row(PC_W), _gathered_spec(w_in)],
        out_specs=[row(d), vec],
        out_shape=[jax.ShapeDtypeStruct((t, d), f32), jax.ShapeDtypeStruct((1, d), f32)],
        compiler_params=_params(("arbitrary",)),
    )(x, dy, pre_g, dpa, dpb, dpc, w_in)


def _mix_out_fwd(x, ya, yb, yc, post_g, w_out, l):
    t, d = x.shape
    tm = _tile(t, 512)

    def body(x_ref, ya_ref, yb_ref, yc_ref, g_ref, w_ref, y_ref, cat_ref, m_ref):
        cat_ref[:, 0:A_W] = ya_ref[...].astype(bf16)
        cat_ref[:, A_W:A_W + B_W] = yb_ref[...].astype(bf16)
        cat_ref[:, A_W + B_W:d] = yc_ref[...].astype(bf16)
        m = _nn(cat_ref[...], _gathered_rows(w_ref, 0, d))
        m_ref[...] = m
        y_ref[...] = x_ref[...] + _rms_fwd(m, g_ref[...])

    def row(w):
        return pl.BlockSpec((tm, w), lambda i: (i, 0))

    return pl.pallas_call(
        body, name="mix_out_fwd", grid=(t // tm,),
        in_specs=[row(d), row(A_W), row(B_W), row(C_W), _lspec(post_g, l), _gathered_spec(w_out)],
        out_specs=[row(d), row(d), row(d)],
        out_shape=[jax.ShapeDtypeStruct((t, d), f32), jax.ShapeDtypeStruct((t, d), bf16), jax.ShapeDtypeStruct((t, d), f32)],
        compiler_params=_params(("arbitrary",)),
    )(x, ya, yb, yc, post_g, w_out)


def _mix_out_bwd(dy, m, post_g, w_out, l, deps=()):
    t, d = m.shape
    tm = _tile(t, 512)

    def body(dy_ref, m_ref, g_ref, w_ref, dm_ref, dya_ref, dyb_ref, dyc_ref, dg_ref):
        @pl.when(pl.program_id(0) == 0)
        def _():
            dg_ref[...] = jnp.zeros_like(dg_ref)

        dm, dg = _rms_bwd(m_ref[...], g_ref[...], dy_ref[...])
        dmb = dm.astype(bf16)
        dm_ref[...] = dmb
        dg_ref[...] += dg
        dcat = _nt(dmb, _gathered_rows(w_ref, 0, d))
        dya_ref[...] = dcat[:, 0:A_W]
        dyb_ref[...] = dcat[:, A_W:A_W + B_W]
        dyc_ref[...] = dcat[:, A_W + B_W:d]

    def row(w):
        return pl.BlockSpec((tm, w), lambda i: (i, 0))

    vec = pl.BlockSpec((1, d), lambda i: (0, 0))
    return pl.pallas_call(
        _after(body, 4, deps), name="mix_out_bwd", grid=(t // tm,),
        in_specs=[row(d), row(d), _lspec(post_g, l), _gathered_spec(w_out)] + [ANY] * len(deps),
        out_specs=[row(d), row(A_W), row(B_W), row(C_W), vec],
        out_shape=[jax.ShapeDtypeStruct((t, d), bf16), jax.ShapeDtypeStruct((t, A_W), f32),
                   jax.ShapeDtypeStruct((t, B_W), f32), jax.ShapeDtypeStruct((t, C_W), f32),
                   jax.ShapeDtypeStruct((1, d), f32)],
        compiler_params=_params(("arbitrary",)),
    )(dy, m, post_g, w_out, *deps)


def _conv_fwd(buf_ref, halo, x, w, b, n):
    buf_ref[0:8, :] = halo
    buf_ref[8:8 + n, :] = x
    out = b + w[3:4, :] * x
    for k in range(3):
        out = out + w[k:k + 1, :] * buf_ref[pl.ds(5 + k, n), :]
    return out


def _conv_bwd(buf_ref, dbuf_ref, dout, dnext, w, n):
    dbuf_ref[0:n, :] = dout
    dbuf_ref[n:n + 8, :] = dnext
    dx = w[3:4, :] * dout
    dws = []
    for k in range(3):
        dx = dx + w[k:k + 1, :] * dbuf_ref[pl.ds(3 - k, n), :]
        dws.append(jnp.sum(dout * buf_ref[pl.ds(5 + k, n), :], axis=0, keepdims=True))
    dws.append(jnp.sum(dout * buf_ref[pl.ds(8, n), :], axis=0, keepdims=True))
    return dx, jnp.concatenate(dws, axis=0), jnp.sum(dout, axis=0, keepdims=True)


def _lru_gates(rec, wr, wi, br, bi, lam):
    rb = rec.astype(bf16)
    r = _sigmoid(_nn(rb, wr) + br)
    ig = _sigmoid(_nn(rb, wi) + bi)
    sp = _softplus(-lam)
    la = -LRU_C * r * sp
    a = jnp.exp(la)
    mult = jnp.sqrt(_one_minus_exp(2.0 * la))
    return rb, r, ig, sp, a, mult


def _scan_rows(a_ref, b_ref, o_ref, carry, n, reverse):
    row = lax.broadcasted_iota(jnp.int32, (8, a_ref.shape[1]), 0)
    nb = n // 8

    def step(k, carry):
        blk = (nb - 1 - k) if reverse else k
        rows = pl.ds(pl.multiple_of(blk * 8, 8), 8)
        a, b = a_ref[rows, :], b_ref[rows, :]
        for d in (1, 2, 4):
            shift = 8 - d if reverse else d
            keep = (row < 8 - d) if reverse else (row >= d)
            b = a * jnp.where(keep, pltpu.roll(b, shift, 0), 0.0) + b
            a = a * jnp.where(keep, pltpu.roll(a, shift, 0), 1.0)
        o = a * carry + b
        o_ref[rows, :] = o
        return o[0:1, :] if reverse else o[7:8, :]

    return lax.fori_loop(0, nb, step, carry, unroll=2)


N_GATES = 5
LRU_SUB = 128


def _lru_fwd(pa, conv_w, conv_b, wr, wi, br, bi, lam, l):
    t = pa.shape[0]
    tc = _tile(t, 512)

    def body(pa_ref, halo_ref, cw_ref, cb_ref, wr_ref, wi_ref, br_ref, bi_ref, lam_ref,
             ya_ref, h_ref, gates_ref, buf_ref, u_ref, carry_ref):
        i = pl.program_id(0)

        @pl.when(i == 0)
        def _():
            carry_ref[...] = jnp.zeros_like(carry_ref)

        halo = jnp.where(i > 0, halo_ref[:, A_W:PA_W], 0.0)
        rec = _conv_fwd(buf_ref, halo, pa_ref[:, A_W:PA_W], cw_ref[...], cb_ref[...], tc)
        _, r, ig, _, a, mult = _lru_gates(rec, wr_ref[...], wi_ref[...], br_ref[...], bi_ref[...], lam_ref[...])
        for k, val in enumerate((rec, r, ig, a, mult)):
            gates_ref[k] = val
        u_ref[...] = mult * (ig * rec)

        carry_ref[...] = _scan_rows(gates_ref.at[3], u_ref, h_ref, carry_ref[...], tc, reverse=False)
        ya_ref[...] = h_ref[...] * _gelu(pa_ref[:, 0:A_W])

    vec = pl.BlockSpec((1, A_W), lambda i: (0, 0))
    mat = pl.BlockSpec((A_W, A_W), lambda i: (0, 0))
    row = pl.BlockSpec((tc, A_W), lambda i: (i, 0))
    return pl.pallas_call(
        body, name="lru_fwd", grid=(t // tc,),
        in_specs=[pl.BlockSpec((tc, PA_W), lambda i: (i, 0)),
                  pl.BlockSpec((8, PA_W), lambda i: (jnp.maximum(i * (tc // 8) - 1, 0), 0)),
                  *[_lspec(a, l) for a in (conv_w, conv_b, wr, wi, br, bi, lam)]],
        out_specs=[row, row, pl.BlockSpec((N_GATES, tc, A_W), lambda i: (0, i, 0))],
        out_shape=[jax.ShapeDtypeStruct((t, A_W), f32), jax.ShapeDtypeStruct((t, A_W), f32),
                   jax.ShapeDtypeStruct((N_GATES, t, A_W), f32)],
        scratch_shapes=[pltpu.VMEM((8 + tc, A_W), f32), pltpu.VMEM((tc, A_W), f32), pltpu.VMEM((1, A_W), f32)],
        compiler_params=_params(("arbitrary",)),
    )(pa, pa, conv_w, conv_b, wr, wi, br, bi, lam)


def _lru_bwd(pa, h, gates, dya, conv_w, conv_b, wr, wi, br, bi, lam, l, deps=()):
    t = pa.shape[0]
    tc = _tile(t, 512)
    nc = t // tc

    def body(pa_ref, halo_ref, h_ref, hhalo_ref, gates_ref, dya_ref, cw_ref, cb_ref, wr_ref, wi_ref, br_ref, bi_ref,
             lam_ref, dpa_ref, dcw_ref, dcb_ref, dwr_ref, dwi_ref, dbr_ref, dbi_ref, dlam_ref,
             buf_ref, dbuf_ref, hbuf_ref, g_ref, dh_ref, carry_ref, dnext_ref, dhbuf_ref, gg_ref):
        i = pl.program_id(0)
        c = nc - 1 - i

        @pl.when(i == 0)
        def _():
            carry_ref[...] = jnp.zeros_like(carry_ref)
            dnext_ref[...] = jnp.zeros_like(dnext_ref)
            for ref in (dcw_ref, dcb_ref, dwr_ref, dwi_ref, dbr_ref, dbi_ref, dlam_ref):
                ref[...] = jnp.zeros_like(ref)

        halo = jnp.where(c > 0, halo_ref[:, A_W:PA_W], 0.0)
        cw = cw_ref[...]
        buf_ref[0:8, :] = halo
        buf_ref[8:8 + tc, :] = pa_ref[:, A_W:PA_W]
        lam = lam_ref[...]
        sp = _softplus(-lam)
        hbuf_ref[0:8, :] = jnp.where(c > 0, hhalo_ref[...], 0.0)
        hbuf_ref[8:8 + tc, :] = h_ref[...]
        gate = pa_ref[:, 0:A_W]
        dya = dya_ref[...]
        dpa_ref[:, 0:A_W] = (dya * h_ref[...] * _gelu_grad(gate)).astype(bf16)
        gg = dya * _gelu(gate)
        gg_ref[...] = gg
        g_ref[...] = gates_ref[3] * gg
        carry_in = carry_ref[...]
        carry_ref[...] = _scan_rows(gates_ref.at[3], g_ref, dh_ref, carry_in, tc, reverse=True)
        dhbuf_ref[0:tc, :] = dh_ref[...]
        dhbuf_ref[tc:tc + 8, :] = jnp.broadcast_to(carry_in, (8, A_W))
        for sb in range(tc // LRU_SUB):
            lo = sb * LRU_SUB
            rows = pl.ds(lo, LRU_SUB)
            rec, r, ig, a, mult = (gates_ref[k, rows, :] for k in range(N_GATES))
            rb = rec.astype(bf16)
            dh = gg_ref[rows, :] + dhbuf_ref[pl.ds(lo + 1, LRU_SUB), :]
            da = dh * hbuf_ref[pl.ds(lo + 7, LRU_SUB), :]
            dmult = dh * ig * rec
            dig = dh * mult * rec
            dla = da * a - dmult * (a * a) / mult
            dr = dla * (-LRU_C * sp)
            dlam_ref[...] += jnp.sum(dla * (-LRU_C * r), axis=0, keepdims=True) * (-_sigmoid(-lam))
            dpr = (dr * r * (1.0 - r))
            dpi = (dig * ig * (1.0 - ig))
            dprb, dpib = dpr.astype(bf16), dpi.astype(bf16)
            g_ref[rows, :] = dh * mult * ig + _nt(dprb, wr_ref[...]) + _nt(dpib, wi_ref[...])
            dwr_ref[...] += _tn(rb, dprb)
            dwi_ref[...] += _tn(rb, dpib)
            dbr_ref[...] += jnp.sum(dpr, axis=0, keepdims=True)
            dbi_ref[...] += jnp.sum(dpi, axis=0, keepdims=True)
        drec = g_ref[...]
        dx, dw, db = _conv_bwd(buf_ref, dbuf_ref, drec, dnext_ref[...], cw, tc)
        dnext_ref[...] = drec[0:8, :]
        dcw_ref[...] += dw
        dcb_ref[...] += db
        dpa_ref[:, A_W:PA_W] = dx.astype(bf16)

    vec = pl.BlockSpec((1, A_W), lambda i: (0, 0))
    mat = pl.BlockSpec((A_W, A_W), lambda i: (0, 0))
    cwspec = pl.BlockSpec((4, A_W), lambda i: (0, 0))

    def rev(w):
        return pl.BlockSpec((tc, w), lambda i: (nc - 1 - i, 0))

    def halo(w):
        return pl.BlockSpec((8, w), lambda i: (jnp.maximum((nc - 1 - i) * (tc // 8) - 1, 0), 0))

    chunk = pltpu.VMEM((tc, A_W), f32)
    return pl.pallas_call(
        _after(body, 13, deps), name="lru_bwd", grid=(nc,),
        in_specs=[rev(PA_W), halo(PA_W), rev(A_W), halo(A_W),
                  pl.BlockSpec((N_GATES, tc, A_W), lambda i: (0, nc - 1 - i, 0)), rev(A_W),
                  *[_lspec(a, l) for a in (conv_w, conv_b, wr, wi, br, bi, lam)]] + [ANY] * len(deps),
        out_specs=[rev(PA_W), cwspec, vec, mat, mat, vec, vec, vec],
        out_shape=[jax.ShapeDtypeStruct((t, PA_W), bf16), jax.ShapeDtypeStruct((4, A_W), f32),
                   jax.ShapeDtypeStruct((1, A_W), f32), jax.ShapeDtypeStruct((A_W, A_W), f32),
                   jax.ShapeDtypeStruct((A_W, A_W), f32), jax.ShapeDtypeStruct((1, A_W), f32),
                   jax.ShapeDtypeStruct((1, A_W), f32), jax.ShapeDtypeStruct((1, A_W), f32)],
        scratch_shapes=[pltpu.VMEM((8 + tc, A_W), f32), pltpu.VMEM((tc + 8, A_W), f32), pltpu.VMEM((8 + tc, A_W), f32),
                        chunk, chunk, pltpu.VMEM((1, A_W), f32), pltpu.VMEM((8, A_W), f32),
                        pltpu.VMEM((tc + 8, A_W), f32), chunk],
        compiler_params=_params(("arbitrary",)),
    )(pa, pa, h, h, gates, dya, conv_w, conv_b, wr, wi, br, bi, lam, *deps)


def _sgu_norm(v, g, b):
    mu = jnp.mean(v, axis=-1, keepdims=True)
    vc = v - mu
    rstd = lax.rsqrt(jnp.mean(vc * vc, axis=-1, keepdims=True) + NORM_EPS)
    vh = vc * rstd
    return vh, rstd, vh * g + b


def _sgu_mix(w_ref, vb, bias):
    grp = lax.broadcasted_iota(jnp.int32, (CHUNK, C_W), 1) // HEAD
    out = bias
    for gi in range(C_W // HEAD):
        out = out + jnp.where(grp == gi, _nn(w_ref[gi], vb), 0.0)
    return out


def _sgu_fwd(pc, ln_g, ln_b, wm, bias, l):
    t = pc.shape[0]
    tm = _tile(t, 512)

    def body(pc_ref, g_ref, b_ref, w_ref, bias_ref, yc_ref):
        for ci in range(tm // CHUNK):
            rows = pl.ds(ci * CHUNK, CHUNK)
            ge = _gelu(pc_ref[rows, :])
            _, _, vn = _sgu_norm(ge[:, C_W:PC_W], g_ref[...], b_ref[...])
            yc_ref[rows, :] = ge[:, 0:C_W] * _sgu_mix(w_ref, vn.astype(bf16), bias_ref[...])

    vec = pl.BlockSpec((1, C_W), lambda i: (0, 0))
    return pl.pallas_call(
        body, name="sgu_fwd", grid=(t // tm,),
        in_specs=[pl.BlockSpec((tm, PC_W), lambda i: (i, 0)), *[_lspec(a, l) for a in (ln_g, ln_b, wm, bias)]],
        out_specs=pl.BlockSpec((tm, C_W), lambda i: (i, 0)),
        out_shape=jax.ShapeDtypeStruct((t, C_W), f32),
        compiler_params=_params(("arbitrary",)),
    )(pc, ln_g, ln_b, wm, bias)


def _sgu_bwd(pc, dyc, ln_g, ln_b, wm, wmt, bias, l, deps=()):
    t = pc.shape[0]
    tm = _tile(t, 512)

    def body(pc_ref, dyc_ref, g_ref, b_ref, w_ref, wt_ref, bias_ref, dpc_ref, dw_ref, dbias_ref, dg_ref, db_ref):
        @pl.when(pl.program_id(0) == 0)
        def _():
            for ref in (dw_ref, dbias_ref, dg_ref, db_ref):
                ref[...] = jnp.zeros_like(ref)

        grp = lax.broadcasted_iota(jnp.int32, (CHUNK, C_W), 1) // HEAD
        for ci in range(tm // CHUNK):
            rows = pl.ds(ci * CHUNK, CHUNK)
            x = pc_ref[rows, :]
            ge = _gelu(x)
            gv = g_ref[...]
            vh, rstd, vn = _sgu_norm(ge[:, C_W:PC_W], gv, b_ref[...])
            vb = vn.astype(bf16)
            mixed = _sgu_mix(w_ref, vb, bias_ref[...])
            dyc = dyc_ref[rows, :]
            du = dyc * mixed
            dmix = dyc * ge[:, 0:C_W]
            dmb = dmix.astype(bf16)
            dvn = jnp.zeros((CHUNK, C_W), f32)
            for gi in range(C_W // HEAD):
                dvn = dvn + jnp.where(grp == gi, _nn(wt_ref[gi], dmb), 0.0)
                dw_ref[gi] += _nt(jnp.where(grp == gi, dmix, 0.0).astype(bf16), vb)
            dbias_ref[...] += dmix
            dg_ref[...] += jnp.sum(dvn * vh, axis=0, keepdims=True)
            db_ref[...] += jnp.sum(dvn, axis=0, keepdims=True)
            dvh = dvn * gv
            dv = rstd * (dvh - jnp.mean(dvh, axis=-1, keepdims=True) - vh * jnp.mean(dvh * vh, axis=-1, keepdims=True))
            gg = _gelu_grad(x)
            dpc_ref[rows, 0:C_W] = (du * gg[:, 0:C_W]).astype(bf16)
            dpc_ref[rows, C_W:PC_W] = (dv * gg[:, C_W:PC_W]).astype(bf16)

    vec = pl.BlockSpec((1, C_W), lambda i: (0, 0))
    wspec = pl.BlockSpec((4, CHUNK, CHUNK), lambda i: (0, 0, 0))
    bspec = pl.BlockSpec((CHUNK, C_W), lambda i: (0, 0))
    return pl.pallas_call(
        _after(body, 7, deps), name="sgu_bwd", grid=(t // tm,),
        in_specs=[pl.BlockSpec((tm, PC_W), lambda i: (i, 0)), pl.BlockSpec((tm, C_W), lambda i: (i, 0)),
                  *[_lspec(a, l) for a in (ln_g, ln_b, wm, wmt, bias)]] + [ANY] * len(deps),
        out_specs=[pl.BlockSpec((tm, PC_W), lambda i: (i, 0)), wspec, bspec, vec, vec],
        out_shape=[jax.ShapeDtypeStruct((t, PC_W), bf16), jax.ShapeDtypeStruct((4, CHUNK, CHUNK), f32),
                   jax.ShapeDtypeStruct((CHUNK, C_W), f32), jax.ShapeDtypeStruct((1, C_W), f32),
                   jax.ShapeDtypeStruct((1, C_W), f32)],
        compiler_params=_params(("arbitrary",)),
    )(pc, dyc, ln_g, ln_b, wm, wmt, bias, *deps)


N_PAIR = B_W // 128
HEADS_PER_GROUP = 3


def _pair_groups(p):
    return (2 * p) // HEADS_PER_GROUP, (2 * p + 1) // HEADS_PER_GROUP


def _ssd_chunk(pb_ref, halo, buf_ref, cw, cb, dtb, alog):
    z = pb_ref[:, 0:B_W]
    pre = _conv_fwd(buf_ref, halo, pb_ref[:, B_W:B_W + XBC_W], cw, cb, CHUNK)
    sg = _sigmoid(pre)
    xbc = pre * sg
    xs = xbc[:, 0:B_W]
    bm = [xbc[:, B_W + k * B_STATE:B_W + (k + 1) * B_STATE] for k in range(2)]
    cm = [xbc[:, B_W + (2 + k) * B_STATE:B_W + (3 + k) * B_STATE] for k in range(2)]
    dtin = pb_ref[:, B_W + XBC_W:PB_W] + dtb
    dt = _softplus(dtin)
    a = -jnp.exp(alog)
    cs = _cumsum_rows(dt * a)
    return dict(z=z, pre=pre, sg=sg, xs=xs, bm=bm, cm=cm, dtin=dtin, dt=dt, a=a, cs=cs,
                ecs=jnp.exp(cs), ds=jnp.exp(cs[CHUNK - 1:CHUNK, :] - cs), xdt=xs * dt,
                bmb=[v.astype(bf16) for v in bm], cmb=[v.astype(bf16) for v in cm])


def _ssd_decay(cs_pair, half):
    cst = cs_pair.T
    lane0 = HEAD * half
    csc = jnp.broadcast_to(cs_pair[:, lane0:lane0 + 1], (CHUNK, CHUNK))
    csr = cst[lane0:lane0 + 1, :]
    tri = lax.broadcasted_iota(jnp.int32, (CHUNK, CHUNK), 0) >= lax.broadcasted_iota(jnp.int32, (CHUNK, CHUNK), 1)
    return jnp.exp(jnp.where(tri, csc - csr, NEG_BIG)), cst


def _ssd_fwd(pb, conv_w, conv_b, dtb, alog, dskip, norm_g, l):
    t = pb.shape[0]
    nc = t // CHUNK

    def body(pb_ref, halo_ref, cw_ref, cb_ref, dtb_ref, alog_ref, d_ref, ng_ref, yb_ref, yp_ref, sp_ref, buf_ref, s_ref):
        i = pl.program_id(0)

        @pl.when(i == 0)
        def _():
            s_ref[...] = jnp.zeros_like(s_ref)

        halo = jnp.where(i > 0, halo_ref[:, B_W:B_W + XBC_W], 0.0)
        q = _ssd_chunk(pb_ref, halo, buf_ref, cw_ref[...], cb_ref[...], dtb_ref[...], alog_ref[...])
        sp_ref[0] = s_ref[...]
        lane = lax.broadcasted_iota(jnp.int32, (CHUNK, 128), 1)
        rowi = lax.broadcasted_iota(jnp.int32, (128, B_STATE), 0)
        cb_mat = [_nt(q["cmb"][k], q["bmb"][k]) for k in range(2)]
        xd = q["xdt"] * q["ds"]
        for p in range(N_PAIR):
            cols = slice(128 * p, 128 * (p + 1))
            g_lo, g_hi = _pair_groups(p)
            cs_p, xdt_p = q["cs"][:, cols], q["xdt"][:, cols]
            s_p = s_ref[cols, :]
            s_pb = s_p.astype(bf16)
            y_p = jnp.zeros((CHUNK, 128), f32)
            for half, grp in ((0, g_lo), (1, g_hi)):
                lm, cst = _ssd_decay(cs_p, half)
                mb = (cb_mat[grp] * lm).astype(bf16)
                sel = (lane < HEAD) if half == 0 else (lane >= HEAD)
                y_p = y_p + _nn(mb, jnp.where(sel, xdt_p, 0.0).astype(bf16))
            off_lo = _nt(q["cmb"][g_lo], s_pb)
            off = off_lo if g_lo == g_hi else jnp.where(lane < HEAD, off_lo, _nt(q["cmb"][g_hi], s_pb))
            y_p = y_p + off * q["ecs"][:, cols] + q["xs"][:, cols] * d_ref[:, cols]
            yp_ref[:, cols] = y_p
            xd_pb = xd[:, cols].astype(bf16)
            upd_lo = _tn(xd_pb, q["bmb"][g_lo])
            upd = upd_lo if g_lo == g_hi else jnp.where(rowi < HEAD, upd_lo, _tn(xd_pb, q["bmb"][g_hi]))
            cd = jnp.exp(jnp.broadcast_to(cst[:, CHUNK - 1:CHUNK], (128, B_STATE)))
            s_ref[cols, :] = cd * s_p + upd
        z = q["z"]
        yg = yp_ref[...] * (z * _sigmoid(z))
        yb_ref[...] = _rms_fwd(yg, ng_ref[...])

    vec = pl.BlockSpec((1, B_W), lambda i: (0, 0))
    row = pl.BlockSpec((CHUNK, B_W), lambda i: (i, 0))
    return pl.pallas_call(
        body, name="ssd_fwd", grid=(nc,),
        in_specs=[pl.BlockSpec((CHUNK, PB_W), lambda i: (i, 0)),
                  pl.BlockSpec((8, PB_W), lambda i: (jnp.maximum(i * (CHUNK // 8) - 1, 0), 0)),
                  *[_lspec(a, l) for a in (conv_w, conv_b, dtb, alog, dskip, norm_g)]],
        out_specs=[row, row, pl.BlockSpec((1, B_W, B_STATE), lambda i: (i, 0, 0))],
        out_shape=[jax.ShapeDtypeStruct((t, B_W), f32), jax.ShapeDtypeStruct((t, B_W), f32),
                   jax.ShapeDtypeStruct((nc, B_W, B_STATE), f32)],
        scratch_shapes=[pltpu.VMEM((8 + CHUNK, XBC_W), f32), pltpu.VMEM((B_W, B_STATE), f32)],
        compiler_params=_params(("arbitrary",)),
    )(pb, pb, conv_w, conv_b, dtb, alog, dskip, norm_g)


def _ssd_bwd(pb, yp, sprev, dyb, conv_w, conv_b, dtb, alog, dskip, norm_g, l):
    t = pb.shape[0]
    nc = t // CHUNK

    def body(pb_ref, halo_ref, yp_ref, sp_ref, dyb_ref, cw_ref, cb_ref, dtb_ref, alog_ref, d_ref, ng_ref,
             dpb_ref, dcw_ref, dcb_ref, ddtb_ref, dalog_ref, dd_ref, dng_ref,
             buf_ref, dbuf_ref, ds_ref, dnext_ref, dxbc_ref, dcs_ref, dxdt_ref):
        i = pl.program_id(0)
        c = nc - 1 - i

        @pl.when(i == 0)
        def _():
            ds_ref[...] = jnp.zeros_like(ds_ref)
            dnext_ref[...] = jnp.zeros_like(dnext_ref)
            for ref in (dcw_ref, dcb_ref, ddtb_ref, dalog_ref, dd_ref, dng_ref):
                ref[...] = jnp.zeros_like(ref)

        halo = jnp.where(c > 0, halo_ref[:, B_W:B_W + XBC_W], 0.0)
        cw = cw_ref[...]
        q = _ssd_chunk(pb_ref, halo, buf_ref, cw, cb_ref[...], dtb_ref[...], alog_ref[...])
        z, xs, dt, a, ecs, dsd, xdt =q["z"], q["xs"], q["dt"], q["a"], q["ecs"], q["ds"], q["xdt"]
        sz = _sigmoid(z)
        siluz = z * sz
        yp = yp_ref[...]
        dyg, dng = _rms_bwd(yp * siluz, ng_ref[...], dyb_ref[...])
        dng_ref[...] += dng
        dy = dyg * siluz
        dpb_ref[:, 0:B_W] = (dyg * yp * _silu_grad(z, sz)).astype(bf16)
        dd_ref[...] += jnp.sum(dy * xs, axis=0, keepdims=True)
        g1 = dy * ecs
        lane = lax.broadcasted_iota(jnp.int32, (CHUNK, 128), 1)
        rowi = lax.broadcasted_iota(jnp.int32, (128, B_STATE), 0)
        rowc = lax.broadcasted_iota(jnp.int32, (CHUNK, 128), 0)
        cb_mat = [_nt(q["cmb"][k], q["bmb"][k]) for k in range(2)]
        d_cb = [jnp.zeros((CHUNK, CHUNK), f32) for _ in range(2)]
        d_b = [jnp.zeros((CHUNK, B_STATE), f32) for _ in range(2)]
        d_c = [jnp.zeros((CHUNK, B_STATE), f32) for _ in range(2)]
        for p in range(N_PAIR):
            cols = slice(128 * p, 128 * (p + 1))
            g_lo, g_hi = _pair_groups(p)
            lo, hi = lane < HEAD, lane >= HEAD
            cs_p, xdt_p, dy_p, ds_p, g1_p = q["cs"][:, cols], xdt[:, cols], dy[:, cols], dsd[:, cols], g1[:, cols]
            s_p = sp_ref[0, cols, :]
            s_pb = s_p.astype(bf16)
            dsn = ds_ref[cols, :]
            dsnb = dsn.astype(bf16)
            g1b = g1_p.astype(bf16)
            off_lo = _nt(q["cmb"][g_lo], s_pb)
            off = off_lo if g_lo == g_hi else jnp.where(lo, off_lo, _nt(q["cmb"][g_hi], s_pb))
            dcs_p = dy_p * off * ecs[:, cols]
            dsp_lo = _tn(g1b, q["cmb"][g_lo])
            dsp = dsp_lo if g_lo == g_hi else jnp.where(rowi < HEAD, dsp_lo, _tn(g1b, q["cmb"][g_hi]))
            dx_lo = _nt(q["bmb"][g_lo], dsnb)
            dxd = dx_lo if g_lo == g_hi else jnp.where(lo, dx_lo, _nt(q["bmb"][g_hi], dsnb))
            xd_p = xdt_p * ds_p
            if g_lo == g_hi:
                d_c[g_lo] = d_c[g_lo] + _nn(g1b, s_pb)
                d_b[g_lo] = d_b[g_lo] + _nn(xd_p.astype(bf16), dsnb)
            else:
                d_c[g_lo] = d_c[g_lo] + _nn(jnp.where(lo, g1_p, 0.0).astype(bf16), s_pb)
                d_c[g_hi] = d_c[g_hi] + _nn(jnp.where(hi, g1_p, 0.0).astype(bf16), s_pb)
                d_b[g_lo] = d_b[g_lo] + _nn(jnp.where(lo, xd_p, 0.0).astype(bf16), dsnb)
                d_b[g_hi] = d_b[g_hi] + _nn(jnp.where(hi, xd_p, 0.0).astype(bf16), dsnb)
            dxdt_p = dxd * ds_p
            t2 = dxd * xdt_p * ds_p
            dcs_p = dcs_p - t2
            dlast = jnp.sum(t2, axis=0, keepdims=True)
            cst = None
            for half, grp in ((0, g_lo), (1, g_hi)):
                sel = lo if half == 0 else hi
                lm, cst = _ssd_decay(cs_p, half)
                m = cb_mat[grp] * lm
                dyh = jnp.where(sel, dy_p, 0.0).astype(bf16)
                xdh = jnp.where(sel, xdt_p, 0.0).astype(bf16)
                dm = _nt(dyh, xdh)
                pm = dm * m
                col = jnp.sum(pm, axis=1, keepdims=True) - jnp.sum(pm.T, axis=1, keepdims=True)
                dcs_p = dcs_p + jnp.where(lane == HEAD * half, col, 0.0)
                d_cb[grp] = d_cb[grp] + dm * lm
                dxdt_p = dxdt_p + _tn(m.astype(bf16), dyh)
            cdcol = jnp.exp(jnp.broadcast_to(cst[:, CHUNK - 1:CHUNK], (128, B_STATE)))
            ds_ref[cols, :] = cdcol * dsn + dsp
            dcd_row = jnp.sum((dsn * s_p).T, axis=0, keepdims=True)
            dlast = dlast + dcd_row * ecs[CHUNK - 1:CHUNK, cols]
            dcs_ref[:, cols] = dcs_p + jnp.where(rowc == CHUNK - 1, dlast, 0.0)
            dxdt_ref[:, cols] = dxdt_p
        for k in range(2):
            dcbb = d_cb[k].astype(bf16)
            d_c[k] = d_c[k] + _nn(dcbb, q["bmb"][k])
            d_b[k] = d_b[k] + _tn(dcbb, q["cmb"][k])
            dxbc_ref[:, B_W + k * B_STATE:B_W + (k + 1) * B_STATE] = d_b[k]
            dxbc_ref[:, B_W + (2 + k) * B_STATE:B_W + (3 + k) * B_STATE] = d_c[k]
        dxdt = dxdt_ref[...]
        dxbc_ref[:, 0:B_W] = dy * d_ref[...] + dxdt * dt
        dcs = dcs_ref[...]
        dad = jnp.sum(dcs, axis=0, keepdims=True) - _cumsum_rows(dcs) + dcs
        ddt = dxdt * xs + dad * a
        dalog_ref[...] += jnp.sum(dad * dt, axis=0, keepdims=True) * a
        dtraw = ddt * _sigmoid(q["dtin"])
        ddtb_ref[...] += jnp.sum(dtraw, axis=0, keepdims=True)
        dpb_ref[:, B_W + XBC_W:PB_W] = dtraw.astype(bf16)
        dpre = dxbc_ref[...] * _silu_grad(q["pre"], q["sg"])
        dx, dw, db = _conv_bwd(buf_ref, dbuf_ref, dpre, dnext_ref[...], cw, CHUNK)
        dnext_ref[...] = dpre[0:8, :]
        dcw_ref[...] += dw
        dcb_ref[...] += db
        dpb_ref[:, B_W:B_W + XBC_W] = dx.astype(bf16)

    vec = pl.BlockSpec((1, B_W), lambda i: (0, 0))
    cwspec = pl.BlockSpec((4, XBC_W), lambda i: (0, 0))
    cbspec = pl.BlockSpec((1, XBC_W), lambda i: (0, 0))

    def rev(w):
        return pl.BlockSpec((CHUNK, w), lambda i: (nc - 1 - i, 0))

    vshape = jax.ShapeDtypeStruct((1, B_W), f32)
    return pl.pallas_call(
        body, name="ssd_bwd", grid=(nc,),
        in_specs=[rev(PB_W), pl.BlockSpec((8, PB_W), lambda i: (jnp.maximum((nc - 1 - i) * (CHUNK // 8) - 1, 0), 0)),
                  rev(B_W), pl.BlockSpec((1, B_W, B_STATE), lambda i: (nc - 1 - i, 0, 0)), rev(B_W),
                  *[_lspec(a, l) for a in (conv_w, conv_b, dtb, alog, dskip, norm_g)]],
        out_specs=[rev(PB_W), cwspec, cbspec, vec, vec, vec, vec],
        out_shape=[jax.ShapeDtypeStruct((t, PB_W), bf16), jax.ShapeDtypeStruct((4, XBC_W), f32),
                   jax.ShapeDtypeStruct((1, XBC_W), f32), vshape, vshape, vshape, vshape],
        scratch_shapes=[pltpu.VMEM((8 + CHUNK, XBC_W), f32), pltpu.VMEM((CHUNK + 8, XBC_W), f32),
                        pltpu.VMEM((B_W, B_STATE), f32), pltpu.VMEM((8, XBC_W), f32),
                        pltpu.VMEM((CHUNK, XBC_W), f32), pltpu.VMEM((CHUNK, B_W), f32), pltpu.VMEM((CHUNK, B_W), f32)],
        compiler_params=_params(("arbitrary",)),
    )(pb, pb, yp, sprev, dyb, conv_w, conv_b, dtb, alog, dskip, norm_g)


def _loss_fwd(y, target):
    t, d = y.shape
    tm = _tile(t, 512)

    def body(y_ref, t_ref, dy_ref, loss_ref):
        @pl.when(pl.program_id(0) == 0)
        def _():
            loss_ref[...] = jnp.zeros_like(loss_ref)

        e = y_ref[...] - t_ref[...]
        dy_ref[...] = e * (1.0 / d)
        per_tok = jnp.mean(e * e, axis=-1, keepdims=True)
        loss_ref[...] += 0.5 * jnp.sum(per_tok, axis=0, keepdims=True)

    row = pl.BlockSpec((tm, d), lambda i: (i, 0))
    return pl.pallas_call(
        body, name="loss_fwd", grid=(t // tm,), in_specs=[row, row],
        out_specs=[row, pl.BlockSpec((1, 128), lambda i: (0, 0))],
        out_shape=[jax.ShapeDtypeStruct((t, d), f32), jax.ShapeDtypeStruct((1, 128), f32)],
        compiler_params=_params(("arbitrary",)),
    )(y, target)


def _row_tile(r):
    return 512 if r % 512 == 0 else r


def _pair_add(g, r, c_dev):
    _, nl, rows, cols = g.shape
    tr = _row_tile(rows)

    def body(c_ref, g_ref, r_ref, o_ref):
        o_ref[...] = (g_ref[...].astype(f32) + r_ref[...].astype(f32)).astype(bf16)

    blk = (None, None, tr, cols)
    return pl.pallas_call(
        body, name="pair_add",
        grid_spec=pltpu.PrefetchScalarGridSpec(
            num_scalar_prefetch=1, grid=(4, nl, rows // tr),
            in_specs=[pl.BlockSpec(blk, lambda b, l, i, c: (2 * b + c[0], l, i, 0)),
                      pl.BlockSpec(blk, lambda b, l, i, c: (b, l, i, 0))],
            out_specs=pl.BlockSpec(blk, lambda b, l, i, c: (b, l, i, 0))),
        out_shape=jax.ShapeDtypeStruct(r.shape, bf16),
        compiler_params=_params(("arbitrary", "arbitrary", "arbitrary")),
    )(c_dev, g, r)


def _grad_sum(s, q, b_dev):
    _, nl, rows, cols = s.shape
    tr = _row_tile(rows)

    def body(b_ref, s_ref, q0_ref, q1_ref, q2_ref, o_ref):
        o_ref[...] = ((s_ref[...].astype(f32) + q0_ref[...].astype(f32)) + q1_ref[...].astype(f32)) + q2_ref[...].astype(f32)

    blk = (None, None, tr, cols)

    def qspec(k):
        return pl.BlockSpec(blk, lambda l, i, b: (k, l, i, 0))

    return pl.pallas_call(
        body, name="grad_sum",
        grid_spec=pltpu.PrefetchScalarGridSpec(
            num_scalar_prefetch=1, grid=(nl, rows // tr),
            in_specs=[pl.BlockSpec(blk, lambda l, i, b: (b[0], l, i, 0)), qspec(0), qspec(1), qspec(2)],
            out_specs=pl.BlockSpec((None, tr, cols), lambda l, i, b: (l, i, 0))),
        out_shape=jax.ShapeDtypeStruct(s.shape[1:], f32),
        compiler_params=_params(("arbitrary", "arbitrary")),
    )(b_dev, s, q, q, q)


def _sum_devices(parts):
    n, rows, cols = parts.shape
    tr = _row_tile(rows)

    def body(p_ref, o_ref):
        acc = p_ref[0]
        for k in range(1, n):
            acc = acc + p_ref[k]
        o_ref[...] = acc

    return pl.pallas_call(
        body, name="sum_devices", grid=(rows // tr,),
        in_specs=[pl.BlockSpec((n, tr, cols), lambda i: (0, i, 0))],
        out_specs=pl.BlockSpec((tr, cols), lambda i: (i, 0)),
        out_shape=jax.ShapeDtypeStruct((rows, cols), f32),
        compiler_params=_params(("arbitrary",)),
    )(parts)


def _adamw(w, m, v, g):
    nl, rows, cols = w.shape
    tr = _row_tile(rows)
    tc = 128 if (tr == rows and rows * cols * 4 > ADAMW_BLOCK_BYTES and cols % 128 == 0) else cols

    def body(w_ref, m_ref, v_ref, g_ref, d_ref, nm_ref, nv_ref):
        d_ref[...], nm_ref[...], nv_ref[...] = _adamw_math(w_ref[...], m_ref[...], v_ref[...], g_ref[...])

    blk = pl.BlockSpec((None, tr, tc), lambda l, i, c: (l, i, c))
    shape = jax.ShapeDtypeStruct(w.shape, f32)
    return pl.pallas_call(
        body, name="adamw", grid=(nl, rows // tr, cols // tc), in_specs=[blk] * 4, out_specs=[blk] * 3,
        out_shape=[shape] * 3, compiler_params=_params(("arbitrary", "arbitrary", "arbitrary")),
    )(w, m, v, g)


def _adamw_math(w, m, v, g):
    nm = ADAM_B1 * m + (1.0 - ADAM_B1) * g
    nv = ADAM_B2 * v + (1.0 - ADAM_B2) * (g * g)
    m_hat = nm / (1.0 - ADAM_B1 ** ADAM_STEP)
    v_hat = nv / (1.0 - ADAM_B2 ** ADAM_STEP)
    return -ADAM_LR * (m_hat / (jnp.sqrt(v_hat) + ADAM_EPS) + ADAM_WD * w), nm, nv


def _adamw_layer(w, m, v, s, q, b_dev, outs, l, deps=()):
    _, rows, cols = w.shape
    tr = _row_tile(rows)

    def body(b_ref, w_ref, m_ref, v_ref, s_ref, q0_ref, q1_ref, q2_ref, o0, o1, o2, o3, g_ref, d_ref, nm_ref, nv_ref):
        g = ((s_ref[...].astype(f32) + q0_ref[...].astype(f32)) + q1_ref[...].astype(f32)) + q2_ref[...].astype(f32)
        g_ref[...] = g
        d_ref[...], nm_ref[...], nv_ref[...] = _adamw_math(w_ref[...], m_ref[...], v_ref[...], g)

    wspec = pl.BlockSpec((None, tr, cols), lambda i, b: (l, i, 0))
    blk = (None, None, tr, cols)

    def qspec(k):
        return pl.BlockSpec(blk, lambda i, b: (k, 0, i, 0))

    shape = jax.ShapeDtypeStruct(w.shape, f32)
    return pl.pallas_call(
        _after(body, 12, deps), name="adamw_layer",
        grid_spec=pltpu.PrefetchScalarGridSpec(
            num_scalar_prefetch=1, grid=(rows // tr,),
            in_specs=[wspec] * 3 + [pl.BlockSpec(blk, lambda i, b: (b[0], 0, i, 0)), qspec(0), qspec(1), qspec(2)]
            + [ANY] * (4 + len(deps)),
            out_specs=[wspec] * 4),
        out_shape=[shape] * 4, input_output_aliases={8 + k: k for k in range(4)},
        compiler_params=_params(("arbitrary",)),
    )(b_dev, w, m, v, s, q, q, q, *outs, *deps)


def _place():
    return lax.axis_index("x"), lax.axis_index("y"), lax.axis_index("c")


def _all_gather(shards, deps=()):
    n = len(shards)
    nd = len(deps)

    def body(*refs):
        src, dst = refs[:n], refs[n:2 * n]
        send_sems, recv_sems, local_sems = refs[2 * n:]
        x, y, c = _place()
        me, sibling = (x, y, c), (x, y, 1 - c)
        chips = [(1 - x, y), (x, 1 - y), (1 - x, 1 - y)]

        def copy(a, k, block, to, from_shard=False):
            px, py, pc = block
            rows = dst[a].at[4 * px + 2 * py + pc]
            return pltpu.make_async_remote_copy(
                src_ref=src[a] if from_shard else rows, dst_ref=rows,
                send_sem=send_sems.at[a, k], recv_sem=recv_sems.at[a, k], device_id=to, device_id_type=MESH)

        mine = [pltpu.make_async_copy(src[a], dst[a].at[4 * x + 2 * y + c], local_sems.at[a]) for a in range(n)]
        for cp in mine:
            cp.start()
        first = []
        for a in range(n):
            first.append(copy(a, 0, me, sibling, True))
            first += [copy(a, 1 + j, me, (*chip, c), True) for j, chip in enumerate(chips)]
        for cp in first:
            cp.start()
        passed = []
        for j, chip in enumerate(chips):
            for a in range(n):
                copy(a, 1 + j, (*chip, c), me).wait_recv()
                fwd = copy(a, 4 + j, (*chip, c), sibling)
                fwd.start()
                passed.append(fwd)
        for a in range(n):
            copy(a, 0, sibling, me).wait_recv()
            for j, chip in enumerate(chips):
                copy(a, 4 + j, (*chip, 1 - c), me).wait_recv()
        for cp in first + passed:
            cp.wait_send()
        for cp in mine:
            cp.wait()

    return pl.pallas_call(
        _after(body, n, deps), name="all_gather", in_specs=[ANY] * (n + nd), out_specs=[ANY] * n,
        out_shape=[jax.ShapeDtypeStruct((N_DEV,) + s.shape, s.dtype) for s in shards],
        scratch_shapes=[pltpu.SemaphoreType.DMA((n, 7)), pltpu.SemaphoreType.DMA((n, 7)), pltpu.SemaphoreType.DMA((n,))],
    )(*shards, *deps)


HBM = pl.BlockSpec(memory_space=pltpu.HBM)
SEM = pl.BlockSpec(memory_space=pltpu.SEMAPHORE)
_EFFECT = pltpu.SideEffectType.DATAFLOW_SIDE_EFFECTING


def _split_start(name, srcs, dsts, sem_shape, plan):
    ns, nb = len(srcs), len(srcs) + len(dsts)

    def body(*refs):
        send_sems, recv_sems = refs[nb], refs[nb + 1]
        for cp in plan(refs[:ns], refs[ns:nb], send_sems, recv_sems):
            cp.start()
        refs[-1][...] = jnp.zeros_like(refs[-1])

    bufs = list(srcs) + list(dsts)
    return pl.pallas_call(
        body, name=name,
        out_shape=(pltpu.SemaphoreType.DMA(sem_shape), pltpu.SemaphoreType.DMA(sem_shape),
                   *[pltpu.HBM(a.shape, a.dtype) for a in bufs], jax.ShapeDtypeStruct((8, 128), f32)),
        in_specs=[HBM] * nb, out_specs=(SEM, SEM, *[HBM] * nb, pl.BlockSpec(memory_space=pltpu.VMEM)),
        input_output_aliases={i: 2 + i for i in range(nb)},
        compiler_params=pltpu.CompilerParams(has_side_effects=_EFFECT),
    )(*[pltpu.with_memory_space_constraint(a, pltpu.HBM) for a in bufs])


def _split_wait(name, started, ns, plan, after):
    send_sems, recv_sems = started[0], started[1]
    bufs = list(started[2:-1])
    nb = len(bufs)
    after = list(after) if isinstance(after, (list, tuple)) else [after]

    def body(*refs):
        for cp in plan(refs[:ns], refs[ns:nb], refs[nb], refs[nb + 1]):
            cp.wait_send()
            cp.wait_recv()

    return pl.pallas_call(
        body, name=name, out_shape=tuple(pltpu.HBM(a.shape, a.dtype) for a in bufs),
        in_specs=[HBM] * nb + [SEM, SEM] + [ANY] * len(after), out_specs=tuple([HBM] * nb),
        input_output_aliases={i: i for i in range(nb)},
        compiler_params=pltpu.CompilerParams(has_side_effects=_EFFECT),
    )(*bufs, send_sems, recv_sems, *after)


def _remote(src, dst, send_sem, recv_sem, to):
    return pltpu.make_async_remote_copy(src_ref=src, dst_ref=dst, send_sem=send_sem, recv_sem=recv_sem,
                                        device_id=to, device_id_type=MESH)


def _gather_plan(src, dst, send_sems, recv_sems):
    x, y, c = _place()
    peers = [(x, y, 1 - c), (1 - x, y, c), (x, 1 - y, c), (1 - x, 1 - y, c)]
    copies = []
    for a in range(len(dst)):
        rows = dst[a].at[4 * x + 2 * y + c]
        copies += [_remote(rows, rows, send_sems.at[4 * a + k], recv_sems.at[4 * a + k], peer) for k, peer in enumerate(peers)]
    return copies


def _pair_plan(src, dst, send_sems, recv_sems):
    x, y, c = _place()
    return [_remote(src[a].at[2 * b + (1 - c)], dst[a].at[b], send_sems.at[4 * a + b], recv_sems.at[4 * a + b], (x, y, 1 - c))
            for a in range(len(src)) for b in range(4)]


def _chips_plan(src, dst, send_sems, recv_sems):
    x, y, c = _place()
    chips = [(1 - x, y), (x, 1 - y), (1 - x, 1 - y)]
    return [_remote(src[a].at[2 * px + py], dst[a].at[j], send_sems.at[3 * a + j], recv_sems.at[3 * a + j], (px, py, c))
            for a in range(len(src)) for j, (px, py) in enumerate(chips)]


def _forward_plan(src, dst, send_sems, recv_sems):
    x, y, c = _place()
    copies = []
    for a in range(len(dst)):
        for j, (px, py) in enumerate([(1 - x, y), (x, 1 - y), (1 - x, 1 - y)]):
            rows = dst[a].at[4 * px + 2 * py + c]
            copies.append(_remote(rows, rows, send_sems.at[3 * a + j], recv_sems.at[3 * a + j], (x, y, 1 - c)))
    return copies


def _gather_finish(bufs):
    n = len(bufs)

    def body(*refs):
        dst = refs[n:2 * n]
        send_sems, recv_sems = refs[2 * n:]
        x, y, c = _place()
        chips = [(1 - x, y), (x, 1 - y), (1 - x, 1 - y)]
        passed = []
        for a in range(n):
            for j, (px, py) in enumerate(chips):
                rows = dst[a].at[4 * px + 2 * py + c]
                passed.append(_remote(rows, rows, send_sems.at[a, j], recv_sems.at[a, j], (x, y, 1 - c)))
        for cp in passed:
            cp.start()
        for cp in passed:
            cp.wait_send()
        for a in range(n):
            for j, (px, py) in enumerate(chips):
                rows = dst[a].at[4 * px + 2 * py + (1 - c)]
                _remote(rows, rows, send_sems.at[a, j], recv_sems.at[a, j], (x, y, 1 - c)).wait_recv()

    return pl.pallas_call(
        body, name="gather_finish", in_specs=[ANY] * n, out_specs=[ANY] * n,
        out_shape=[jax.ShapeDtypeStruct(b.shape, b.dtype) for b in bufs],
        input_output_aliases={a: a for a in range(n)},
        scratch_shapes=[pltpu.SemaphoreType.DMA((n, 3)), pltpu.SemaphoreType.DMA((n, 3))],
    )(*bufs)


def _place_shards(mats, l, dev):
    n = len(mats)

    def body(dev_ref, *refs):
        for a in range(n):
            refs[n + a][...] = refs[a][...].astype(bf16)

    return pl.pallas_call(
        body, name="place_shards",
        grid_spec=pltpu.PrefetchScalarGridSpec(
            num_scalar_prefetch=1, grid=(1,),
            in_specs=[pl.BlockSpec((None,) + m.shape[1:], lambda i, dv: (l, 0, 0)) for m in mats],
            out_specs=[pl.BlockSpec((None, None) + m.shape[1:], lambda i, dv: (dv[0], 0, 0, 0)) for m in mats]),
        out_shape=[jax.ShapeDtypeStruct((N_DEV, 1) + m.shape[1:], bf16) for m in mats],
        compiler_params=_params(("arbitrary",)),
    )(dev, *mats)


BIG = ("ffn1_w_gu", "ffn1_w_down", "mix_w_in", "mix_w_out", "ffn2_w_gu", "ffn2_w_down")
SHARDED_CONV = ("lru_conv_w", "ssd_conv_w")
REPLICATED = ("ffn1_pre_g", "ffn1_post_g", "mix_pre_g", "mix_post_g", "lru_conv_b", "lru_w_r", "lru_b_r", "lru_w_i",
              "lru_b_i", "lru_lambda", "ssd_conv_b", "ssd_dt_bias", "ssd_a_log", "ssd_d", "ssd_norm_g", "sgu_ln_g",
              "sgu_ln_b", "sgu_w_s", "sgu_b_s", "ffn2_pre_g", "ffn2_post_g")
WEIGHTS = ("ffn1_pre_g", "ffn1_post_g", "ffn1_w_gu", "ffn1_w_down", "mix_pre_g", "mix_post_g", "mix_w_in", "mix_w_out",
           "lru_conv_w", "lru_conv_b", "lru_w_r", "lru_b_r", "lru_w_i", "lru_b_i", "lru_lambda", "ssd_conv_w",
           "ssd_conv_b", "ssd_dt_bias", "ssd_a_log", "ssd_d", "ssd_norm_g", "sgu_ln_g", "sgu_ln_b", "sgu_w_s", "sgu_b_s",
           "ffn2_pre_g", "ffn2_post_g", "ffn2_w_gu", "ffn2_w_down")
DT_LO = PA_W + B_W + XBC_W
N_HEADS = B_W // HEAD
PACK_COLS = 1024


def _size(shape):
    size = 1
    for dim in shape:
        size *= dim
    return size


def _pack_rows(shape):
    return -(-_size(shape) // PACK_COLS)


def _pack(arrays):
    pieces = [jnp.pad(a.reshape(-1), (0, _pack_rows(a.shape) * PACK_COLS - _size(a.shape))) for a in arrays]
    rows = sum(_pack_rows(a.shape) for a in arrays)
    if rows % 8:
        pieces.append(jnp.zeros(((8 - rows % 8) * PACK_COLS,), f32))
    return jnp.concatenate(pieces).reshape(-1, PACK_COLS)


def _unpack(packed, shapes):
    out, row = [], 0
    for s in shapes:
        nr = _pack_rows(s)
        out.append(packed[row:row + nr].reshape(-1)[:_size(s)].reshape(s))
        row += nr
    return out


def _widen_w_in(w):
    return jnp.concatenate([w[..., :DT_LO], jnp.repeat(w[..., DT_LO:DT_LO + N_HEADS], HEAD, axis=-1),
                            w[..., DT_LO + N_HEADS:]], axis=-1)


def _narrow_w_in_grad(g):
    dt = g[..., DT_LO:DT_LO + B_W]
    dt = dt.reshape(dt.shape[:-1] + (N_HEADS, HEAD)).sum(-1)
    return jnp.concatenate([g[..., :DT_LO], dt, g[..., DT_LO + B_W:]], axis=-1)


def _per_head(a):
    return a.reshape(a.shape[:-1] + (N_HEADS, HEAD)).sum(-1)


def kernel(x, ffn1_pre_g, ffn1_post_g, ffn1_w_gu, ffn1_w_down, mix_pre_g, mix_post_g, mix_w_in, mix_w_out, lru_conv_w, lru_conv_b, lru_w_r, lru_b_r, lru_w_i, lru_b_i, lru_lambda, ssd_conv_w, ssd_conv_b, ssd_dt_bias, ssd_a_log, ssd_d, ssd_norm_g, sgu_ln_g, sgu_ln_b, sgu_w_s, sgu_b_s, ffn2_pre_g, ffn2_post_g, ffn2_w_gu, ffn2_w_down, loss_target, m_ffn1_pre_g, m_ffn1_post_g, m_ffn1_w_gu, m_ffn1_w_down, m_mix_pre_g, m_mix_post_g, m_mix_w_in, m_mix_w_out, m_lru_conv_w, m_lru_conv_b, m_lru_w_r, m_lru_b_r, m_lru_w_i, m_lru_b_i, m_lru_lambda, m_ssd_conv_w, m_ssd_conv_b, m_ssd_dt_bias, m_ssd_a_log, m_ssd_d, m_ssd_norm_g, m_sgu_ln_g, m_sgu_ln_b, m_sgu_w_s, m_sgu_b_s, m_ffn2_pre_g, m_ffn2_post_g, m_ffn2_w_gu, m_ffn2_w_down, v_ffn1_pre_g, v_ffn1_post_g, v_ffn1_w_gu, v_ffn1_w_down, v_mix_pre_g, v_mix_post_g, v_mix_w_in, v_mix_w_out, v_lru_conv_w, v_lru_conv_b, v_lru_w_r, v_lru_b_r, v_lru_w_i, v_lru_b_i, v_lru_lambda, v_ssd_conv_w, v_ssd_conv_b, v_ssd_dt_bias, v_ssd_a_log, v_ssd_d, v_ssd_norm_g, v_sgu_ln_g, v_sgu_ln_b, v_sgu_w_s, v_sgu_b_s, v_ffn2_pre_g, v_ffn2_post_g, v_ffn2_w_gu, v_ffn2_w_down):
    given = dict(locals())
    w = {n: given[n] for n in WEIGHTS}
    mom = {n: given["m_" + n] for n in WEIGHTS}
    var = {n: given["v_" + n] for n in WEIGHTS}
    nl = ffn1_pre_g.shape[0]
    _, t, d = x.shape
    xi, yi, ci = _place()
    dev = 4 * xi + 2 * yi + ci
    c_dev = jnp.reshape(ci, (1,)).astype(jnp.int32)
    b_dev = jnp.reshape(2 * xi + yi, (1,)).astype(jnp.int32)

    conv_shapes = [lru_conv_w.shape, ssd_conv_w.shape]
    shards = [ffn1_w_gu, ffn1_w_down, _widen_w_in(mix_w_in), mix_w_out, ffn2_w_gu, ffn2_w_down]
    nbig = len(shards)
    dev_arr = jnp.reshape(dev, (1,)).astype(jnp.int32)
    conv_pack = _pack([lru_conv_w, ssd_conv_w])
    conv_buf = lax.dynamic_update_slice_in_dim(jnp.zeros((N_DEV,) + conv_pack.shape, f32), conv_pack[None], dev, axis=0)
    def gather_groups(l):
        return [(0, 1), (2, 3), (4, 5)] if l == 0 else [tuple(range(nbig))]

    gather_started = {}
    for l in range(nl):
        for gi, idx in enumerate(gather_groups(l)):
            bufs = list(_place_shards([shards[i] for i in idx], l, dev_arr)) + ([conv_buf] if (l, gi) == (0, 1) else [])
            gather_started[l, gi] = _split_start(f"gather_start_{l}_{gi}", [], bufs, (4 * len(bufs),), _gather_plan)

    def finish_gather(l, gi, after):
        waited = _split_wait(f"gather_wait_{l}_{gi}", gather_started[l, gi], 0, _gather_plan, after)
        return _gather_finish(list(waited))

    def conv_taps(conv_all):
        full = []
        for k, shape in enumerate(conv_shapes):
            per_dev = jnp.stack([_unpack(conv_all[s], conv_shapes)[k] for s in range(N_DEV)], axis=2)
            full.append(per_dev.reshape(shape[0], shape[1], N_DEV * shape[2]))
        return full

    def vec(a):
        return a.reshape(nl, 1, -1)

    def per_channel(a):
        return jnp.repeat(a, HEAD, axis=-1).reshape(nl, 1, B_W)

    eye = jnp.eye(A_W // HEAD, dtype=f32)

    def block_diag(a):
        return jnp.einsum("lhij,hg->lhigj", a, eye).reshape(nl, A_W, A_W).astype(bf16)

    causal = jnp.tril(jnp.ones((CHUNK, CHUNK), dtype=bool))
    p = dict(
        ffn1_pre=vec(ffn1_pre_g), ffn1_post=vec(ffn1_post_g), mix_pre=vec(mix_pre_g), mix_post=vec(mix_post_g),
        ffn2_pre=vec(ffn2_pre_g), ffn2_post=vec(ffn2_post_g),
        lru=(vec(lru_conv_b), block_diag(lru_w_r), block_diag(lru_w_i), vec(lru_b_r), vec(lru_b_i), vec(lru_lambda)),
        ssd=(vec(ssd_conv_b), per_channel(ssd_dt_bias), per_channel(ssd_a_log), per_channel(ssd_d), vec(ssd_norm_g)),
    )
    wm = jnp.where(causal, sgu_w_s, 0.0).astype(bf16)
    sgu_bias = jnp.repeat(jnp.swapaxes(sgu_b_s, 1, 2), HEAD, axis=2)
    sgu_f = (vec(sgu_ln_g), vec(sgu_ln_b), wm, sgu_bias)
    sgu_b = (vec(sgu_ln_g), vec(sgu_ln_b), wm, jnp.swapaxes(wm, 2, 3), sgu_bias)

    small_names = REPLICATED + SHARDED_CONV
    small_state = [_pack([src[n] for n in small_names])[None] for src in (w, mom, var)]
    prepared = [a for v in p.values() for a in (v if isinstance(v, tuple) else (v,))] + list(sgu_b) + small_state

    xs = x.reshape(t, d)
    saved, gathered, early_forward = [], [], {}
    for l in range(nl):
        x0 = xs
        if l == 0:
            wgu1, wd1 = finish_gather(0, 0, [x0] + prepared)
            deps = tuple(started[-1] for key, started in gather_started.items() if key != (0, 0))
        elif l in early_forward:
            wgu1, wd1, win, wout, wgu2, wd2 = _split_wait(f"forward_wait_{l}", early_forward[l], 0, _forward_plan, x0)
            deps = ()
        else:
            wgu1, wd1, win, wout, wgu2, wd2 = finish_gather(l, 0, x0)
            deps = ()
        x1, hb1, g1, u1, f1 = _ffn_fwd(x0, p["ffn1_pre"], p["ffn1_post"], wgu1, wd1, l, deps)
        if l == 0:
            win, wout, conv_all = finish_gather(0, 1, x1)
            lru_cw, ssd_cw = conv_taps(conv_all)
            p["lru"], p["ssd"] = (lru_cw,) + p["lru"], (ssd_cw,) + p["ssd"]
        hbm, pa, pb, pc = _mix_in_fwd(x1, p["mix_pre"], win, l)
        ya, h, gates = _lru_fwd(pa, *p["lru"], l)
        yb, yp, sp = _ssd_fwd(pb, *p["ssd"], l)
        yc = _sgu_fwd(pc, *sgu_f, l)
        x2, cat, m = _mix_out_fwd(x1, ya, yb, yc, p["mix_post"], wout, l)
        deps = ()
        if l == 0:
            wgu2, wd2 = finish_gather(0, 2, x2)
        elif l + 1 < nl:
            waited = _split_wait(f"gather_wait_{l + 1}_0", gather_started[l + 1, 0], 0, _gather_plan, x2)
            early_forward[l + 1] = _split_start(f"forward_start_{l + 1}", [], list(waited), (3 * nbig,), _forward_plan)
            deps = (early_forward[l + 1][-1],)
        xs, hb2, g2, u2, f2 = _ffn_fwd(x2, p["ffn2_pre"], p["ffn2_post"], wgu2, wd2, l, deps)
        gathered.append((wgu1, wd1, win, wout, wgu2, wd2))
        saved.append((x0, hb1, g1, u1, f1, x1, hbm, pa, pb, pc, h, gates, yp, sp, cat, m, x2, hb2, g2, u2, f2))
    dy, loss_part = _loss_fwd(xs, loss_target.reshape(t, d))
    loss = lax.psum(loss_part[0, 0], ("x", "y", "c"))

    small = {n: [None] * nl for n in REPLICATED + SHARDED_CONV}
    grads, delta, new_m, new_v = {}, {}, {}, {}
    fused = [n for n in BIG if n != "mix_w_in"]

    def oriented(a, n):
        return jnp.swapaxes(a, 1, 2) if n.endswith("w_gu") else a

    opt_in = {n: tuple(oriented(src[n], n) for src in (w, mom, var)) for n in fused}
    opt_out = {n: tuple(lax.empty(opt_in[n][0].shape, f32) for _ in range(4)) for n in fused}
    w_in_grads = [None] * nl
    grad_shapes = {n: (s.shape[2], s.shape[1]) if n.endswith("w_gu") else s.shape[1:] for n, s in zip(BIG, shards)}

    def start_pair(tag, lp, names, gbuf):
        landing = [lax.empty((4, 1) + grad_shapes[n], bf16) for n in names]
        started = _split_start(f"pair_start_{tag}", [gbuf[n] for n in names], landing, (4 * len(names),), _pair_plan)
        return tag, lp, names, started

    def finish_pair(pending, after):
        tag, lp, names, started = pending
        k = len(names)
        done = _split_wait(f"pair_wait_{tag}", started, k, _pair_plan, after)
        sums = [_pair_add(g, r, c_dev) for g, r in zip(done[:k], done[k:])]
        landing = [lax.empty((3,) + s.shape[1:], bf16) for s in sums]
        return tag, lp, names, _split_start(f"chips_start_{tag}", sums, landing, (3 * k,), _chips_plan)

    def finish_chips(pending, after, deps=()):
        tag, lp, names, started = pending
        k = len(names)
        done = _split_wait(f"chips_wait_{tag}", started, k, _chips_plan, after)
        last = None
        for n, s, q in zip(names, done[:k], done[k:]):
            if n == "mix_w_in":
                w_in_grads[lp] = last = _grad_sum(s, q, b_dev)
            else:
                opt_out[n] = tuple(_adamw_layer(*opt_in[n], s, q, b_dev, opt_out[n], lp, deps))
                last = opt_out[n][0]
        return last

    early = ("ffn2_w_gu", "ffn2_w_down", "mix_w_out")
    late = ("mix_w_in", "ffn1_w_gu", "ffn1_w_down")
    pending_pair = pending_chips = early_pair = early_chips = upper_started = None
    deferred = []
    names = REPLICATED + SHARDED_CONV
    assert nl > 1
    for l in reversed(range(nl)):
        x0, hb1, g1, u1, f1, x1, hbm, pa, pb, pc, h, gates, yp, sp, cat, m, x2, hb2, g2, u2, f2 = saved[l]
        wgu1, wd1, win, wout, wgu2, wd2 = gathered[l][:nbig]
        gbuf ={n: lax.empty((N_DEV, 1) + grad_shapes[n], bf16) for n in BIG}
        deps = () if pending_pair is None else (pending_pair[3][-1],)
        if l == 0:
            deps += (upper_started[-1],)
        dx2, dfb, act, dg, du, dpre, dpost = _ffn_bwd(x2, dy, f2, p["ffn2_pre"], p["ffn2_post"], g2, u2, wgu2, wd2, l, deps)
        small["ffn2_pre_g"][l], small["ffn2_post_g"][l] = dpre[0], dpost[0]
        gbuf["ffn2_w_gu"] = _wgrad_cols(hb2, dg, gbuf["ffn2_w_gu"], 0, 0)
        gbuf["ffn2_w_gu"] = _wgrad_cols(hb2, du, gbuf["ffn2_w_gu"], 0, dg.shape[0])
        gbuf["ffn2_w_down"] = _wgrad_rows(act, dfb, gbuf["ffn2_w_down"], 0)
        deps = ()
        if pending_pair is not None:
            pending_chips = finish_pair(pending_pair, dx2)
            deps = (pending_chips[3][-1],)

        dm, dya, dyb, dyc, dpost = _mix_out_bwd(dx2, m, p["mix_post"], wout, l, deps)
        small["mix_post_g"][l] = dpost[0]
        gbuf["mix_w_out"] = _wgrad_kblocks(cat, [dm], gbuf["mix_w_out"], 0)
        deps = ()
        if l == 0:
            early_pair = start_pair("0a", 0, early, gbuf)
            deps = (early_pair[3][-1],)
        dpc, dws, dbias, dlg, dlb = _sgu_bwd(pc, dyc, *sgu_b, l, deps)
        small["sgu_w_s"][l] = jnp.where(causal, dws, 0.0)
        small["sgu_b_s"][l] = dbias.reshape(CHUNK, C_W // HEAD, HEAD).sum(-1).T
        small["sgu_ln_g"][l], small["sgu_ln_b"][l] = dlg[0], dlb[0]
        dpb, dcw, dcb, ddtb, dalog, ddsk, dng = _ssd_bwd(pb, yp, sp, dyb, *p["ssd"], l)
        small["ssd_conv_w"][l], small["ssd_conv_b"][l], small["ssd_norm_g"][l] = dcw, dcb[0], dng[0]
        small["ssd_dt_bias"][l], small["ssd_a_log"][l], small["ssd_d"][l] = _per_head(ddtb[0]), _per_head(dalog[0]), _per_head(ddsk[0])
        deps = ()
        if l == 0:
            early_chips = finish_pair(early_pair, dpb)
            deps = (early_chips[3][-1],)
        dpa, dcw, dcb, dwr, dwi, dbr, dbi, dlam = _lru_bwd(pa, h, gates, dya, *p["lru"], l, deps)
        small["lru_conv_w"][l], small["lru_conv_b"][l], small["lru_lambda"][l] = dcw, dcb[0], dlam[0]
        small["lru_b_r"][l], small["lru_b_i"][l] = dbr[0], dbi[0]
        heads = range(A_W // HEAD)
        small["lru_w_r"][l] = jnp.stack([dwr[HEAD * i:HEAD * (i + 1), HEAD * i:HEAD * (i + 1)] for i in heads])
        small["lru_w_i"][l] = jnp.stack([dwi[HEAD * i:HEAD * (i + 1), HEAD * i:HEAD * (i + 1)] for i in heads])
        dx1, dpre = _mix_in_bwd(x1, dx2, p["mix_pre"], dpa, dpb, dpc, win, l)
        small["mix_pre_g"][l] = dpre[0]
        gbuf["mix_w_in"] = _wgrad_kblocks(hbm, [dpa, dpb, dpc], gbuf["mix_w_in"], 0)

        dy, dfb, act, dg, du, dpre, dpost = _ffn_bwd(x0, dx1, f1, p["ffn1_pre"], p["ffn1_post"], g1, u1, wgu1, wd1, l)
        small["ffn1_pre_g"][l], small["ffn1_post_g"][l] = dpre[0], dpost[0]
        gbuf["ffn1_w_gu"] = _wgrad_cols(hb1, dg, gbuf["ffn1_w_gu"], 0, 0)
        gbuf["ffn1_w_gu"] = _wgrad_cols(hb1, du, gbuf["ffn1_w_gu"], 0, dg.shape[0])
        gbuf["ffn1_w_down"] = _wgrad_rows(act, dfb, gbuf["ffn1_w_down"], 0)
        if pending_chips is not None:
            deferred.append(pending_chips)
            pending_chips = None
        pending_pair = start_pair(f"{l}", l, late if l == 0 else BIG, gbuf)
        if l == 1:
            upper = [jnp.stack(small[n][1:]) for n in names]
            upper_pack = _pack(upper)
            upper_buf = lax.dynamic_update_slice_in_dim(
                jnp.zeros((N_DEV,) + upper_pack.shape, f32), upper_pack[None], dev, axis=0)
            upper_started = _split_start("small_start", [], [upper_buf], (4,), _gather_plan)
    grad_x = dy.reshape(x.shape)

    lower = [jnp.stack(small[n][:1]) for n in names]
    lower_total = _sum_devices(_all_gather([_pack(lower)], (pending_pair[3][-1],))[0])
    late_chips = finish_pair(pending_pair, lower_total)
    order = lower_total
    for pending in deferred + [early_chips]:
        order = finish_chips(pending, order, (late_chips[3][-1],))
    upper_all = _gather_finish(list(_split_wait("small_wait", upper_started, 0, _gather_plan, order)))[0]
    upper_total = _sum_devices(upper_all)
    finish_chips(late_chips, [upper_total] + [opt_out[n][0] for n in fused] + [g for g in w_in_grads if g is not None])
    full = {n: jnp.concatenate([lo, up], axis=0) for n, lo, up in zip(
        names, _unpack(lower_total, [a.shape for a in lower]), _unpack(upper_total, [a.shape for a in upper]))}

    for n in fused:
        grads[n], delta[n], new_m[n], new_v[n] = (oriented(a, n) for a in opt_out[n])
    grads["mix_w_in"] = _narrow_w_in_grad(jnp.concatenate(w_in_grads, axis=0))
    delta["mix_w_in"], new_m["mix_w_in"], new_v["mix_w_in"] = _adamw(
        w["mix_w_in"], mom["mix_w_in"], var["mix_w_in"], grads["mix_w_in"])
    for n in REPLICATED:
        grads[n] = full[n]
    for n in SHARDED_CONV:
        cols = w[n].shape[2]
        grads[n] = lax.dynamic_slice_in_dim(full[n], dev * cols, cols, axis=2)
    shapes = [w[n].shape for n in names]
    packs = small_state + [_pack([grads[n] for n in names])[None]]
    for dst, packed in zip((delta, new_m, new_v), _adamw(*packs)):
        dst.update(zip(names, _unpack(packed[0], shapes)))

    return (loss, grad_x, *[grads[n] for n in WEIGHTS], *[delta[n] for n in WEIGHTS],
            *[new_m[n] for n in WEIGHTS], *[new_v[n] for n in WEIGHTS])
```

```python
import functools

import jax
import jax.numpy as jnp
from jax import lax
from jax.experimental import pallas as pl
from jax.experimental.pallas import tpu as pltpu

f32, bf16 = jnp.float32, jnp.bfloat16
MESH = pl.DeviceIdType.MESH
ANY = pl.BlockSpec(memory_space=pl.ANY)

N_DEV = 8
NORM_EPS = 1e-6
LRU_C = 8.0
CHUNK = 128
HEAD = 64
A_W, B_W, C_W = 384, 384, 256
B_STATE = 128
XBC_W = B_W + 4 * B_STATE
PA_W, PB_W, PC_W = 2 * A_W, B_W + XBC_W + B_W, 2 * C_W
IN_PAD = PA_W + PB_W + PC_W
ADAM_LR, ADAM_B1, ADAM_B2, ADAM_EPS, ADAM_WD, ADAM_STEP = 0.001, 0.9, 0.999, 1e-08, 0.01, 10
VMEM_LIMIT_BYTES = 56 * 1024 * 1024
FFN_BWD_SPLIT = 2
ADAMW_BLOCK_BYTES = 2 * 1024 * 1024
NEG_BIG = -1e30


def _params(sem=None):
    return pltpu.CompilerParams(dimension_semantics=sem, vmem_limit_bytes=VMEM_LIMIT_BYTES)


def _nn(a, b):
    return jnp.dot(a, b, preferred_element_type=f32)


def _nt(a, b):
    return lax.dot_general(a, b, (((1,), (1,)), ((), ())), preferred_element_type=f32)


def _tn(a, b):
    return lax.dot_general(a, b, (((0,), (0,)), ((), ())), preferred_element_type=f32)


def _sigmoid(x):
    return 0.5 * jnp.tanh(0.5 * x) + 0.5


def _softplus(x):
    return jnp.maximum(x, 0.0) + jnp.log(1.0 + jnp.exp(-jnp.abs(x)))


_GELU_C0, _GELU_C1 = 0.7978845608028654, 0.044715


def _gelu(x):
    t = jnp.tanh(_GELU_C0 * (x + _GELU_C1 * x * x * x))
    return 0.5 * x * (1.0 + t)


def _gelu_grad(x):
    t = jnp.tanh(_GELU_C0 * (x + _GELU_C1 * x * x * x))
    return 0.5 * (1.0 + t) + 0.5 * x * (1.0 - t * t) * _GELU_C0 * (1.0 + 3.0 * _GELU_C1 * x * x)


def _silu_grad(x, s):
    return s * (1.0 + x * (1.0 - s))


def _rms_fwd(x, g):
    r = lax.rsqrt(jnp.mean(x * x, axis=-1, keepdims=True) + NORM_EPS)
    return x * r * g


def _rms_bwd(x, g, dy):
    r = lax.rsqrt(jnp.mean(x * x, axis=-1, keepdims=True) + NORM_EPS)
    xh = x * r
    dxh = dy * g
    dx = r * (dxh - xh * jnp.mean(dxh * xh, axis=-1, keepdims=True))
    return dx, jnp.sum(dy * xh, axis=0, keepdims=True)


def _one_minus_exp(x):
    series = -x * (1.0 + x * (0.5 + x * (1.0 / 6.0 + x * (1.0 / 24.0))))
    return jnp.where(x > -0.01, series, 1.0 - jnp.exp(x))


def _cumsum_rows(x):
    row = lax.broadcasted_iota(jnp.int32, x.shape, 0)
    d = 1
    while d < x.shape[0]:
        x = x + jnp.where(row >= d, pltpu.roll(x, d, 0), 0.0)
        d *= 2
    return x


def _tile(t, cap):
    tm = min(cap, t)
    assert t % tm == 0
    return tm


def _after(body, n_in, deps):
    def wrapped(*refs):
        return body(*refs[:n_in], *refs[n_in + len(deps):])
    return wrapped


def _lspec(a, l):
    return pl.BlockSpec((None,) + a.shape[1:], lambda *_: (l,) + (0,) * (a.ndim - 1))


def _wd_rows(wd_ref):
    return wd_ref[:, 0].reshape(2 * wd_ref.shape[2], wd_ref.shape[3])


def _ffn_fwd(x, pre_g, post_g, wgu, wd, l, deps=()):
    t, d = x.shape
    nb, _, _, h = wgu.shape
    nj = nb // 2
    tm = _tile(t, 512)

    def body(x_ref, pg_ref, qg_ref, wg_ref, wu_ref, wd_ref, y_ref, hb_ref, g_ref, u_ref, f_ref, acc_ref):
        j = pl.program_id(1)

        @pl.when(j == 0)
        def _():
            hb_ref[...] = _rms_fwd(x_ref[...], pg_ref[...]).astype(bf16)
            acc_ref[...] = jnp.zeros_like(acc_ref)

        hb = hb_ref[...]
        g = _nn(hb, wg_ref[0, 0])
        u = _nn(hb, wu_ref[0, 0])
        g_ref[0] = g.astype(bf16)
        u_ref[0] = u.astype(bf16)
        a = (g * _sigmoid(g) * u).astype(bf16)
        acc_ref[...] += _nn(a, _wd_rows(wd_ref))

        @pl.when(j == nj - 1)
        def _():
            f = acc_ref[...]
            f_ref[...] = f
            y_ref[...] = x_ref[...] + 0.5 * _rms_fwd(f, qg_ref[...])

    row = pl.BlockSpec((tm, d), lambda i, j: (i, 0))
    vec = pl.BlockSpec((1, d), lambda i, j: (0, 0))
    act = pl.BlockSpec((1, tm, h), lambda i, j: (j, i, 0))
    return pl.pallas_call(
        _after(body, 6, deps), name="ffn_fwd", grid=(t // tm, nj),
        in_specs=[row, _lspec(pre_g, l), _lspec(post_g, l),
                  pl.BlockSpec((1, 1, d, h), lambda i, j: (j, 0, 0, 0)),
                  pl.BlockSpec((1, 1, d, h), lambda i, j: (j + nj, 0, 0, 0)),
                  pl.BlockSpec((2, 1, h // 2, d), lambda i, j: (j, 0, 0, 0))] + [ANY] * len(deps),
        out_specs=[row, row, act, act, row],
        out_shape=[jax.ShapeDtypeStruct((t, d), f32), jax.ShapeDtypeStruct((t, d), bf16),
                   jax.ShapeDtypeStruct((nj, t, h), bf16), jax.ShapeDtypeStruct((nj, t, h), bf16),
                   jax.ShapeDtypeStruct((t, d), f32)],
        scratch_shapes=[pltpu.VMEM((tm, d), f32)],
        compiler_params=_params(("arbitrary", "arbitrary")),
    )(x, pre_g, post_g, wgu, wgu, wd, *deps)


def _ffn_bwd(x, dy, f, pre_g, post_g, g, u, wgu, wd, l, deps=()):
    t, d = x.shape
    nj, _, h = g.shape
    tm = _tile(t, 512)

    def body(x_ref, dy_ref, f_ref, pg_ref, qg_ref, g_ref, u_ref, wg_ref, wu_ref, wd_ref,
             dx_ref, dfb_ref, a_ref, dg_ref, du_ref, dpg_ref, dqg_ref, dh_ref):
        i, j = pl.program_id(0), pl.program_id(1)

        @pl.when((i == 0) & (j == 0))
        def _():
            dpg_ref[...] = jnp.zeros_like(dpg_ref)
            dqg_ref[...] = jnp.zeros_like(dqg_ref)

        @pl.when(j == 0)
        def _():
            df, dq = _rms_bwd(f_ref[...], qg_ref[...], 0.5 * dy_ref[...])
            dfb_ref[...] = df.astype(bf16)
            dqg_ref[...] += dq
            dh_ref[...] = jnp.zeros_like(dh_ref)

        wdm, wg, wu = _wd_rows(wd_ref), wg_ref[0, 0], wu_ref[0, 0]
        sub = tm // FFN_BWD_SPLIT
        das = [_nt(dfb_ref[pl.ds(half * sub, sub), :], wdm) for half in range(FFN_BWD_SPLIT)]
        for half in range(FFN_BWD_SPLIT):
            rows = pl.ds(half * sub, sub)
            da = das[half]
            gv = g_ref[0, rows, :].astype(f32)
            uv = u_ref[0, rows, :].astype(f32)
            s = _sigmoid(gv)
            sg = gv * s
            a_ref[0, rows, :] = (sg * uv).astype(bf16)
            dg = (da * uv * _silu_grad(gv, s)).astype(bf16)
            du = (da * sg).astype(bf16)
            dg_ref[0, rows, :] = dg
            du_ref[0, rows, :] = du
            dh_ref[rows, :] += _nt(dg, wg) + _nt(du, wu)

        @pl.when(j == nj - 1)
        def _():
            dxn, dp = _rms_bwd(x_ref[...], pg_ref[...], dh_ref[...])
            dx_ref[...] = dy_ref[...] + dxn
            dpg_ref[...] += dp

    row = pl.BlockSpec((tm, d), lambda i, j: (i, 0))
    vec = pl.BlockSpec((1, d), lambda i, j: (0, 0))
    act = pl.BlockSpec((1, tm, h), lambda i, j: (j, i, 0))
    act_shape = jax.ShapeDtypeStruct((nj, t, h), bf16)
    return pl.pallas_call(
        _after(body, 10, deps), name="ffn_bwd", grid=(t // tm, nj),
        in_specs=[row, row, row, _lspec(pre_g, l), _lspec(post_g, l), act, act,
                  pl.BlockSpec((1, 1, d, h), lambda i, j: (j, 0, 0, 0)),
                  pl.BlockSpec((1, 1, d, h), lambda i, j: (j + nj, 0, 0, 0)),
                  pl.BlockSpec((2, 1, h // 2, d), lambda i, j: (j, 0, 0, 0))] + [ANY] * len(deps),
        out_specs=[row, row, act, act, act, vec, vec],
        out_shape=[jax.ShapeDtypeStruct((t, d), f32), jax.ShapeDtypeStruct((t, d), bf16),
                   act_shape, act_shape, act_shape,
                   jax.ShapeDtypeStruct((1, d), f32), jax.ShapeDtypeStruct((1, d), f32)],
        scratch_shapes=[pltpu.VMEM((tm, d), f32)],
        compiler_params=_params(("arbitrary", "arbitrary")),
    )(x, dy, f, pre_g, post_g, g, u, wgu, wgu, wd, *deps)


def _wgrad_cols(x, dy, buf, l, slot0):
    (t, k), (nj, _, n) = x.shape, dy.shape

    def body(x_ref, dy_ref, buf_ref, o_ref):
        o_ref[0, 0] = _tn(dy_ref[0], x_ref[...]).astype(bf16)

    return pl.pallas_call(
        body, name="wgrad_cols", grid=(nj,),
        in_specs=[pl.BlockSpec((t, k), lambda b: (0, 0)), pl.BlockSpec((1, t, n), lambda b: (b, 0, 0)), ANY],
        out_specs=pl.BlockSpec((1, 1, n, k), lambda b: (b + slot0, l, 0, 0)),
        out_shape=jax.ShapeDtypeStruct(buf.shape, bf16), input_output_aliases={2: 0},
        compiler_params=_params(("arbitrary",)),
    )(x, dy, buf)


def _wgrad_rows(x, dy, buf, l):
    (nj, t, k), (_, n) = x.shape, dy.shape

    def body(x_ref, dy_ref, buf_ref, o_ref):
        o_ref[:, 0] = _tn(x_ref[0], dy_ref[...]).astype(bf16).reshape(2, k // 2, n)

    return pl.pallas_call(
        body, name="wgrad_rows", grid=(nj,),
        in_specs=[pl.BlockSpec((1, t, k), lambda b: (b, 0, 0)), pl.BlockSpec((t, n), lambda b: (0, 0)), ANY],
        out_specs=pl.BlockSpec((2, 1, k // 2, n), lambda b: (b, l, 0, 0)),
        out_shape=jax.ShapeDtypeStruct(buf.shape, bf16), input_output_aliases={2: 0},
        compiler_params=_params(("arbitrary",)),
    )(x, dy, buf)


def _wgrad_kblocks(x, dys, buf, l):
    t, k = x.shape
    kb = k // N_DEV
    widths = [dy.shape[1] for dy in dys]
    n = sum(widths)
    nd = len(dys)

    def body(x_ref, *refs):
        dy_hbm, o_ref, dy_vmem = refs[:nd], refs[nd + 1], refs[nd + 2:]

        @pl.when(pl.program_id(0) == 0)
        def _():
            for src, dst in zip(dy_hbm, dy_vmem):
                pltpu.sync_copy(src, dst)

        off = 0
        for dst, w in zip(dy_vmem, widths):
            o_ref[0, 0, :, off:off + w] = _tn(x_ref[...], dst[...]).astype(bf16)
            off += w

    return pl.pallas_call(
        body, name="wgrad_kblocks", grid=(N_DEV,),
        in_specs=[pl.BlockSpec((t, kb), lambda s: (0, s))] + [ANY] * (nd + 1),
        out_specs=pl.BlockSpec((1, 1, kb, n), lambda s: (s, l, 0, 0)),
        out_shape=jax.ShapeDtypeStruct(buf.shape, bf16), input_output_aliases={nd + 1: 0},
        scratch_shapes=[pltpu.VMEM((t, w), bf16) for w in widths],
        compiler_params=_params(("arbitrary",)),
    )(x, *dys, buf)


def _gathered_rows(w_ref, lo, hi):
    return w_ref[:, 0, :, lo:hi].reshape(N_DEV * w_ref.shape[2], hi - lo)


def _gathered_spec(w):
    return pl.BlockSpec((N_DEV, 1) + w.shape[2:], lambda i: (0, 0, 0, 0))


def _mix_in_fwd(x, pre_g, w_in, l):
    t, d = x.shape
    tm = _tile(t, 512)

    def body(x_ref, g_ref, w_ref, hb_ref, pa_ref, pb_ref, pc_ref):
        hb = _rms_fwd(x_ref[...], g_ref[...]).astype(bf16)
        hb_ref[...] = hb
        pa_ref[...] = _nn(hb, _gathered_rows(w_ref, 0, PA_W))
        pb_ref[...] = _nn(hb, _gathered_rows(w_ref, PA_W, PA_W + PB_W))
        pc_ref[...] = _nn(hb, _gathered_rows(w_ref, PA_W + PB_W, IN_PAD))

    def row(w):
        return pl.BlockSpec((tm, w), lambda i: (i, 0))

    return pl.pallas_call(
        body, name="mix_in_fwd", grid=(t // tm,),
        in_specs=[row(d), _lspec(pre_g, l), _gathered_spec(w_in)],
        out_specs=[row(d), row(PA_W), row(PB_W), row(PC_W)],
        out_shape=[jax.ShapeDtypeStruct((t, d), bf16), jax.ShapeDtypeStruct((t, PA_W), f32),
                   jax.ShapeDtypeStruct((t, PB_W), f32), jax.ShapeDtypeStruct((t, PC_W), f32)],
        compiler_params=_params(("arbitrary",)),
    )(x, pre_g, w_in)


def _mix_in_bwd(x, dy, pre_g, dpa, dpb, dpc, w_in, l):
    t, d = x.shape
    tm = _tile(t, 512)

    def body(x_ref, dy_ref, g_ref, dpa_ref, dpb_ref, dpc_ref, w_ref, dx_ref, dg_ref):
        @pl.when(pl.program_id(0) == 0)
        def _():
            dg_ref[...] = jnp.zeros_like(dg_ref)

        wa, wb, wc = (_gathered_rows(w_ref, 0, PA_W), _gathered_rows(w_ref, PA_W, PA_W + PB_W),
                      _gathered_rows(w_ref, PA_W + PB_W, IN_PAD))
        halves = [pl.ds(k * (tm // 2), tm // 2) for k in range(2)]
        dhs = [_nt(dpa_ref[rows, :], wa) + _nt(dpb_ref[rows, :], wb) + _nt(dpc_ref[rows, :], wc) for rows in halves]
        for rows, dh in zip(halves, dhs):
            dxn, dg = _rms_bwd(x_ref[rows, :], g_ref[...], dh)
            dx_ref[rows, :] = dy_ref[rows, :] + dxn
            dg_ref[...] += dg

    def row(w):
        return pl.BlockSpec((tm, w), lambda i: (i, 0))

    vec = pl.BlockSpec((1, d), lambda i: (0, 0))
    return pl.pallas_call(
        body, name="mix_in_bwd", grid=(t // tm,),
        in_specs=[row(d), row(d), _lspec(pre_g, l), row(PA_W), row(PB_W), row(PC_W), _gathered_spec(w_in)],
        out_specs=[row(d), vec],
        out_shape=[jax.ShapeDtypeStruct((t, d), f32), jax.ShapeDtypeStruct((1, d), f32)],
        compiler_params=_params(("arbitrary",)),
    )(x, dy, pre_g, dpa, dpb, dpc, w_in)


def _mix_out_fwd(x, ya, yb, yc, post_g, w_out, l):
    t, d = x.shape
    tm = _tile(t, 512)

    def body(x_ref, ya_ref, yb_ref, yc_ref, g_ref, w_ref, y_ref, cat_ref, m_ref):
        cat_ref[:, 0:A_W] = ya_ref[...].astype(bf16)
        cat_ref[:, A_W:A_W + B_W] = yb_ref[...].astype(bf16)
        cat_ref[:, A_W + B_W:d] = yc_ref[...].astype(bf16)
        m = _nn(cat_ref[...], _gathered_rows(w_ref, 0, d))
        m_ref[...] = m
        y_ref[...] = x_ref[...] + _rms_fwd(m, g_ref[...])

    def row(w):
        return pl.BlockSpec((tm, w), lambda i: (i, 0))

    return pl.pallas_call(
        body, name="mix_out_fwd", grid=(t // tm,),
        in_specs=[row(d), row(A_W), row(B_W), row(C_W), _lspec(post_g, l), _gathered_spec(w_out)],
        out_specs=[row(d), row(d), row(d)],
        out_shape=[jax.ShapeDtypeStruct((t, d), f32), jax.ShapeDtypeStruct((t, d), bf16), jax.ShapeDtypeStruct((t, d), f32)],
        compiler_params=_params(("arbitrary",)),
    )(x, ya, yb, yc, post_g, w_out)


def _mix_out_bwd(dy, m, post_g, w_out, l, deps=()):
    t, d = m.shape
    tm = _tile(t, 512)

    def body(dy_ref, m_ref, g_ref, w_ref, dm_ref, dya_ref, dyb_ref, dyc_ref, dg_ref):
        @pl.when(pl.program_id(0) == 0)
        def _():
            dg_ref[...] = jnp.zeros_like(dg_ref)

        dm, dg = _rms_bwd(m_ref[...], g_ref[...], dy_ref[...])
        dmb = dm.astype(bf16)
        dm_ref[...] = dmb
        dg_ref[...] += dg
        dcat = _nt(dmb, _gathered_rows(w_ref, 0, d))
        dya_ref[...] = dcat[:, 0:A_W]
        dyb_ref[...] = dcat[:, A_W:A_W + B_W]
        dyc_ref[...] = dcat[:, A_W + B_W:d]

    def row(w):
        return pl.BlockSpec((tm, w), lambda i: (i, 0))

    vec = pl.BlockSpec((1, d), lambda i: (0, 0))
    return pl.pallas_call(
        _after(body, 4, deps), name="mix_out_bwd", grid=(t // tm,),
        in_specs=[row(d), row(d), _lspec(post_g, l), _gathered_spec(w_out)] + [ANY] * len(deps),
        out_specs=[row(d), row(A_W), row(B_W), row(C_W), vec],
        out_shape=[jax.ShapeDtypeStruct((t, d), bf16), jax.ShapeDtypeStruct((t, A_W), f32),
                   jax.ShapeDtypeStruct((t, B_W), f32), jax.ShapeDtypeStruct((t, C_W), f32),
                   jax.ShapeDtypeStruct((1, d), f32)],
        compiler_params=_params(("arbitrary",)),
    )(dy, m, post_g, w_out, *deps)


def _conv_fwd(buf_ref, halo, x, w, b, n):
    buf_ref[0:8, :] = halo
    buf_ref[8:8 + n, :] = x
    out = b + w[3:4, :] * x
    for k in range(3):
        out = out + w[k:k + 1, :] * buf_ref[pl.ds(5 + k, n), :]
    return out


def _conv_bwd(buf_ref, dbuf_ref, dout, dnext, w, n):
    dbuf_ref[0:n, :] = dout
    dbuf_ref[n:n + 8, :] = dnext
    dx = w[3:4, :] * dout
    dws = []
    for k in range(3):
        dx = dx + w[k:k + 1, :] * dbuf_ref[pl.ds(3 - k, n), :]
        dws.append(jnp.sum(dout * buf_ref[pl.ds(5 + k, n), :], axis=0, keepdims=True))
    dws.append(jnp.sum(dout * buf_ref[pl.ds(8, n), :], axis=0, keepdims=True))
    return dx, jnp.concatenate(dws, axis=0), jnp.sum(dout, axis=0, keepdims=True)


def _lru_gates(rec, wr, wi, br, bi, lam):
    rb = rec.astype(bf16)
    r = _sigmoid(_nn(rb, wr) + br)
    ig = _sigmoid(_nn(rb, wi) + bi)
    sp = _softplus(-lam)
    la = -LRU_C * r * sp
    a = jnp.exp(la)
    mult = jnp.sqrt(_one_minus_exp(2.0 * la))
    return rb, r, ig, sp, a, mult


def _scan_rows(a_ref, b_ref, o_ref, carry, n, reverse):
    row = lax.broadcasted_iota(jnp.int32, (8, a_ref.shape[1]), 0)
    nb = n // 8

    def step(k, carry):
        blk = (nb - 1 - k) if reverse else k
        rows = pl.ds(pl.multiple_of(blk * 8, 8), 8)
        a, b = a_ref[rows, :], b_ref[rows, :]
        for d in (1, 2, 4):
            shift = 8 - d if reverse else d
            keep = (row < 8 - d) if reverse else (row >= d)
            b = a * jnp.where(keep, pltpu.roll(b, shift, 0), 0.0) + b
            a = a * jnp.where(keep, pltpu.roll(a, shift, 0), 1.0)
        o = a * carry + b
        o_ref[rows, :] = o
        return o[0:1, :] if reverse else o[7:8, :]

    return lax.fori_loop(0, nb, step, carry, unroll=2)


N_GATES = 5
LRU_SUB = 128


def _lru_fwd(pa, conv_w, conv_b, wr, wi, br, bi, lam, l):
    t = pa.shape[0]
    tc = _tile(t, 512)

    def body(pa_ref, halo_ref, cw_ref, cb_ref, wr_ref, wi_ref, br_ref, bi_ref, lam_ref,
             ya_ref, h_ref, gates_ref, buf_ref, u_ref, carry_ref):
        i = pl.program_id(0)

        @pl.when(i == 0)
        def _():
            carry_ref[...] = jnp.zeros_like(carry_ref)

        halo = jnp.where(i > 0, halo_ref[:, A_W:PA_W], 0.0)
        rec = _conv_fwd(buf_ref, halo, pa_ref[:, A_W:PA_W], cw_ref[...], cb_ref[...], tc)
        _, r, ig, _, a, mult = _lru_gates(rec, wr_ref[...], wi_ref[...], br_ref[...], bi_ref[...], lam_ref[...])
        for k, val in enumerate((rec, r, ig, a, mult)):
            gates_ref[k] = val
        u_ref[...] = mult * (ig * rec)

        carry_ref[...] = _scan_rows(gates_ref.at[3], u_ref, h_ref, carry_ref[...], tc, reverse=False)
        ya_ref[...] = h_ref[...] * _gelu(pa_ref[:, 0:A_W])

    vec = pl.BlockSpec((1, A_W), lambda i: (0, 0))
    mat = pl.BlockSpec((A_W, A_W), lambda i: (0, 0))
    row = pl.BlockSpec((tc, A_W), lambda i: (i, 0))
    return pl.pallas_call(
        body, name="lru_fwd", grid=(t // tc,),
        in_specs=[pl.BlockSpec((tc, PA_W), lambda i: (i, 0)),
                  pl.BlockSpec((8, PA_W), lambda i: (jnp.maximum(i * (tc // 8) - 1, 0), 0)),
                  *[_lspec(a, l) for a in (conv_w, conv_b, wr, wi, br, bi, lam)]],
        out_specs=[row, row, pl.BlockSpec((N_GATES, tc, A_W), lambda i: (0, i, 0))],
        out_shape=[jax.ShapeDtypeStruct((t, A_W), f32), jax.ShapeDtypeStruct((t, A_W), f32),
                   jax.ShapeDtypeStruct((N_GATES, t, A_W), f32)],
        scratch_shapes=[pltpu.VMEM((8 + tc, A_W), f32), pltpu.VMEM((tc, A_W), f32), pltpu.VMEM((1, A_W), f32)],
        compiler_params=_params(("arbitrary",)),
    )(pa, pa, conv_w, conv_b, wr, wi, br, bi, lam)


def _lru_bwd(pa, h, gates, dya, conv_w, conv_b, wr, wi, br, bi, lam, l, deps=()):
    t = pa.shape[0]
    tc = _tile(t, 512)
    nc = t // tc

    def body(pa_ref, halo_ref, h_ref, hhalo_ref, gates_ref, dya_ref, cw_ref, cb_ref, wr_ref, wi_ref, br_ref, bi_ref,
             lam_ref, dpa_ref, dcw_ref, dcb_ref, dwr_ref, dwi_ref, dbr_ref, dbi_ref, dlam_ref,
             buf_ref, dbuf_ref, hbuf_ref, g_ref, dh_ref, carry_ref, dnext_ref, dhbuf_ref, gg_ref):
        i = pl.program_id(0)
        c = nc - 1 - i

        @pl.when(i == 0)
        def _():
            carry_ref[...] = jnp.zeros_like(carry_ref)
            dnext_ref[...] = jnp.zeros_like(dnext_ref)
            for ref in (dcw_ref, dcb_ref, dwr_ref, dwi_ref, dbr_ref, dbi_ref, dlam_ref):
                ref[...] = jnp.zeros_like(ref)

        halo = jnp.where(c > 0, halo_ref[:, A_W:PA_W], 0.0)
        cw = cw_ref[...]
        buf_ref[0:8, :] = halo
        buf_ref[8:8 + tc, :] = pa_ref[:, A_W:PA_W]
        lam = lam_ref[...]
        sp = _softplus(-lam)
        hbuf_ref[0:8, :] = jnp.where(c > 0, hhalo_ref[...], 0.0)
        hbuf_ref[8:8 + tc, :] = h_ref[...]
        gate = pa_ref[:, 0:A_W]
        dya = dya_ref[...]
        dpa_ref[:, 0:A_W] = (dya * h_ref[...] * _gelu_grad(gate)).astype(bf16)
        gg = dya * _gelu(gate)
        gg_ref[...] = gg
        g_ref[...] = gates_ref[3] * gg
        carry_in = carry_ref[...]
        carry_ref[...] = _scan_rows(gates_ref.at[3], g_ref, dh_ref, carry_in, tc, reverse=True)
        dhbuf_ref[0:tc, :] = dh_ref[...]
        dhbuf_ref[tc:tc + 8, :] = jnp.broadcast_to(carry_in, (8, A_W))
        for sb in range(tc // LRU_SUB):
            lo = sb * LRU_SUB
            rows = pl.ds(lo, LRU_SUB)
            rec, r, ig, a, mult = (gates_ref[k, rows, :] for k in range(N_GATES))
            rb = rec.astype(bf16)
            dh = gg_ref[rows, :] + dhbuf_ref[pl.ds(lo + 1, LRU_SUB), :]
            da = dh * hbuf_ref[pl.ds(lo + 7, LRU_SUB), :]
            dmult = dh * ig * rec
            dig = dh * mult * rec
            dla = da * a - dmult * (a * a) / mult
            dr = dla * (-LRU_C * sp)
            dlam_ref[...] += jnp.sum(dla * (-LRU_C * r), axis=0, keepdims=True) * (-_sigmoid(-lam))
            dpr = (dr * r * (1.0 - r))
            dpi = (dig * ig * (1.0 - ig))
            dprb, dpib = dpr.astype(bf16), dpi.astype(bf16)
            g_ref[rows, :] = dh * mult * ig + _nt(dprb, wr_ref[...]) + _nt(dpib, wi_ref[...])
            dwr_ref[...] += _tn(rb, dprb)
            dwi_ref[...] += _tn(rb, dpib)
            dbr_ref[...] += jnp.sum(dpr, axis=0, keepdims=True)
            dbi_ref[...] += jnp.sum(dpi, axis=0, keepdims=True)
        drec = g_ref[...]
        dx, dw, db = _conv_bwd(buf_ref, dbuf_ref, drec, dnext_ref[...], cw, tc)
        dnext_ref[...] = drec[0:8, :]
        dcw_ref[...] += dw
        dcb_ref[...] += db
        dpa_ref[:, A_W:PA_W] = dx.astype(bf16)

    vec = pl.BlockSpec((1, A_W), lambda i: (0, 0))
    mat = pl.BlockSpec((A_W, A_W), lambda i: (0, 0))
    cwspec = pl.BlockSpec((4, A_W), lambda i: (0, 0))

    def rev(w):
        return pl.BlockSpec((tc, w), lambda i: (nc - 1 - i, 0))

    def halo(w):
        return pl.BlockSpec((8, w), lambda i: (jnp.maximum((nc - 1 - i) * (tc // 8) - 1, 0), 0))

    chunk = pltpu.VMEM((tc, A_W), f32)
    return pl.pallas_call(
        _after(body, 13, deps), name="lru_bwd", grid=(nc,),
        in_specs=[rev(PA_W), halo(PA_W), rev(A_W), halo(A_W),
                  pl.BlockSpec((N_GATES, tc, A_W), lambda i: (0, nc - 1 - i, 0)), rev(A_W),
                  *[_lspec(a, l) for a in (conv_w, conv_b, wr, wi, br, bi, lam)]] + [ANY] * len(deps),
        out_specs=[rev(PA_W), cwspec, vec, mat, mat, vec, vec, vec],
        out_shape=[jax.ShapeDtypeStruct((t, PA_W), bf16), jax.ShapeDtypeStruct((4, A_W), f32),
                   jax.ShapeDtypeStruct((1, A_W), f32), jax.ShapeDtypeStruct((A_W, A_W), f32),
                   jax.ShapeDtypeStruct((A_W, A_W), f32), jax.ShapeDtypeStruct((1, A_W), f32),
                   jax.ShapeDtypeStruct((1, A_W), f32), jax.ShapeDtypeStruct((1, A_W), f32)],
        scratch_shapes=[pltpu.VMEM((8 + tc, A_W), f32), pltpu.VMEM((tc + 8, A_W), f32), pltpu.VMEM((8 + tc, A_W), f32),
                        chunk, chunk, pltpu.VMEM((1, A_W), f32), pltpu.VMEM((8, A_W), f32),
                        pltpu.VMEM((tc + 8, A_W), f32), chunk],
        compiler_params=_params(("arbitrary",)),
    )(pa, pa, h, h, gates, dya, conv_w, conv_b, wr, wi, br, bi, lam, *deps)


def _sgu_norm(v, g, b):
    mu = jnp.mean(v, axis=-1, keepdims=True)
    vc = v - mu
    rstd = lax.rsqrt(jnp.mean(vc * vc, axis=-1, keepdims=True) + NORM_EPS)
    vh = vc * rstd
    return vh, rstd, vh * g + b


def _sgu_mix(w_ref, vb, bias):
    grp = lax.broadcasted_iota(jnp.int32, (CHUNK, C_W), 1) // HEAD
    out = bias
    for gi in range(C_W // HEAD):
        out = out + jnp.where(grp == gi, _nn(w_ref[gi], vb), 0.0)
    return out


def _sgu_fwd(pc, ln_g, ln_b, wm, bias, l):
    t = pc.shape[0]
    tm = _tile(t, 512)

    def body(pc_ref, g_ref, b_ref, w_ref, bias_ref, yc_ref):
        for ci in range(tm // CHUNK):
            rows = pl.ds(ci * CHUNK, CHUNK)
            ge = _gelu(pc_ref[rows, :])
            _, _, vn = _sgu_norm(ge[:, C_W:PC_W], g_ref[...], b_ref[...])
            yc_ref[rows, :] = ge[:, 0:C_W] * _sgu_mix(w_ref, vn.astype(bf16), bias_ref[...])

    vec = pl.BlockSpec((1, C_W), lambda i: (0, 0))
    return pl.pallas_call(
        body, name="sgu_fwd", grid=(t // tm,),
        in_specs=[pl.BlockSpec((tm, PC_W), lambda i: (i, 0)), *[_lspec(a, l) for a in (ln_g, ln_b, wm, bias)]],
        out_specs=pl.BlockSpec((tm, C_W), lambda i: (i, 0)),
        out_shape=jax.ShapeDtypeStruct((t, C_W), f32),
        compiler_params=_params(("arbitrary",)),
    )(pc, ln_g, ln_b, wm, bias)


def _sgu_bwd(pc, dyc, ln_g, ln_b, wm, wmt, bias, l, deps=()):
    t = pc.shape[0]
    tm = _tile(t, 512)

    def body(pc_ref, dyc_ref, g_ref, b_ref, w_ref, wt_ref, bias_ref, dpc_ref, dw_ref, dbias_ref, dg_ref, db_ref):
        @pl.when(pl.program_id(0) == 0)
        def _():
            for ref in (dw_ref, dbias_ref, dg_ref, db_ref):
                ref[...] = jnp.zeros_like(ref)

        grp = lax.broadcasted_iota(jnp.int32, (CHUNK, C_W), 1) // HEAD
        for ci in range(tm // CHUNK):
            rows = pl.ds(ci * CHUNK, CHUNK)
            x = pc_ref[rows, :]
            ge = _gelu(x)
            gv = g_ref[...]
            vh, rstd, vn = _sgu_norm(ge[:, C_W:PC_W], gv, b_ref[...])
            vb = vn.astype(bf16)
            mixed = _sgu_mix(w_ref, vb, bias_ref[...])
            dyc = dyc_ref[rows, :]
            du = dyc * mixed
            dmix = dyc * ge[:, 0:C_W]
            dmb = dmix.astype(bf16)
            dvn = jnp.zeros((CHUNK, C_W), f32)
            for gi in range(C_W // HEAD):
                dvn = dvn + jnp.where(grp == gi, _nn(wt_ref[gi], dmb), 0.0)
                dw_ref[gi] += _nt(jnp.where(grp == gi, dmix, 0.0).astype(bf16), vb)
            dbias_ref[...] += dmix
            dg_ref[...] += jnp.sum(dvn * vh, axis=0, keepdims=True)
            db_ref[...] += jnp.sum(dvn, axis=0, keepdims=True)
            dvh = dvn * gv
            dv = rstd * (dvh - jnp.mean(dvh, axis=-1, keepdims=True) - vh * jnp.mean(dvh * vh, axis=-1, keepdims=True))
            gg = _gelu_grad(x)
            dpc_ref[rows, 0:C_W] = (du * gg[:, 0:C_W]).astype(bf16)
            dpc_ref[rows, C_W:PC_W] = (dv * gg[:, C_W:PC_W]).astype(bf16)

    vec = pl.BlockSpec((1, C_W), lambda i: (0, 0))
    wspec = pl.BlockSpec((4, CHUNK, CHUNK), lambda i: (0, 0, 0))
    bspec = pl.BlockSpec((CHUNK, C_W), lambda i: (0, 0))
    return pl.pallas_call(
        _after(body, 7, deps), name="sgu_bwd", grid=(t // tm,),
        in_specs=[pl.BlockSpec((tm, PC_W), lambda i: (i, 0)), pl.BlockSpec((tm, C_W), lambda i: (i, 0)),
                  *[_lspec(a, l) for a in (ln_g, ln_b, wm, wmt, bias)]] + [ANY] * len(deps),
        out_specs=[pl.BlockSpec((tm, PC_W), lambda i: (i, 0)), wspec, bspec, vec, vec],
        out_shape=[jax.ShapeDtypeStruct((t, PC_W), bf16), jax.ShapeDtypeStruct((4, CHUNK, CHUNK), f32),
                   jax.ShapeDtypeStruct((CHUNK, C_W), f32), jax.ShapeDtypeStruct((1, C_W), f32),
                   jax.ShapeDtypeStruct((1, C_W), f32)],
        compiler_params=_params(("arbitrary",)),
    )(pc, dyc, ln_g, ln_b, wm, wmt, bias, *deps)


N_PAIR = B_W // 128
HEADS_PER_GROUP = 3


def _pair_groups(p):
    return (2 * p) // HEADS_PER_GROUP, (2 * p + 1) // HEADS_PER_GROUP


def _ssd_chunk(pb_ref, halo, buf_ref, cw, cb, dtb, alog):
    z = pb_ref[:, 0:B_W]
    pre = _conv_fwd(buf_ref, halo, pb_ref[:, B_W:B_W + XBC_W], cw, cb, CHUNK)
    sg = _sigmoid(pre)
    xbc = pre * sg
    xs = xbc[:, 0:B_W]
    bm = [xbc[:, B_W + k * B_STATE:B_W + (k + 1) * B_STATE] for k in range(2)]
    cm = [xbc[:, B_W + (2 + k) * B_STATE:B_W + (3 + k) * B_STATE] for k in range(2)]
    dtin = pb_ref[:, B_W + XBC_W:PB_W] + dtb
    dt = _softplus(dtin)
    a = -jnp.exp(alog)
    cs = _cumsum_rows(dt * a)
    return dict(z=z, pre=pre, sg=sg, xs=xs, bm=bm, cm=cm, dtin=dtin, dt=dt, a=a, cs=cs,
                ecs=jnp.exp(cs), ds=jnp.exp(cs[CHUNK - 1:CHUNK, :] - cs), xdt=xs * dt,
                bmb=[v.astype(bf16) for v in bm], cmb=[v.astype(bf16) for v in cm])


def _ssd_decay(cs_pair, half):
    cst = cs_pair.T
    lane0 = HEAD * half
    csc = jnp.broadcast_to(cs_pair[:, lane0:lane0 + 1], (CHUNK, CHUNK))
    csr = cst[lane0:lane0 + 1, :]
    tri = lax.broadcasted_iota(jnp.int32, (CHUNK, CHUNK), 0) >= lax.broadcasted_iota(jnp.int32, (CHUNK, CHUNK), 1)
    return jnp.exp(jnp.where(tri, csc - csr, NEG_BIG)), cst


def _ssd_fwd(pb, conv_w, conv_b, dtb, alog, dskip, norm_g, l):
    t = pb.shape[0]
    nc = t // CHUNK

    def body(pb_ref, halo_ref, cw_ref, cb_ref, dtb_ref, alog_ref, d_ref, ng_ref, yb_ref, yp_ref, sp_ref, buf_ref, s_ref):
        i = pl.program_id(0)

        @pl.when(i == 0)
        def _():
            s_ref[...] = jnp.zeros_like(s_ref)

        halo = jnp.where(i > 0, halo_ref[:, B_W:B_W + XBC_W], 0.0)
        q = _ssd_chunk(pb_ref, halo, buf_ref, cw_ref[...], cb_ref[...], dtb_ref[...], alog_ref[...])
        sp_ref[0] = s_ref[...]
        lane = lax.broadcasted_iota(jnp.int32, (CHUNK, 128), 1)
        rowi = lax.broadcasted_iota(jnp.int32, (128, B_STATE), 0)
        cb_mat = [_nt(q["cmb"][k], q["bmb"][k]) for k in range(2)]
        xd = q["xdt"] * q["ds"]
        for p in range(N_PAIR):
            cols = slice(128 * p, 128 * (p + 1))
            g_lo, g_hi = _pair_groups(p)
            cs_p, xdt_p = q["cs"][:, cols], q["xdt"][:, cols]
            s_p = s_ref[cols, :]
            s_pb = s_p.astype(bf16)
            y_p = jnp.zeros((CHUNK, 128), f32)
            for half, grp in ((0, g_lo), (1, g_hi)):
                lm, cst = _ssd_decay(cs_p, half)
                mb = (cb_mat[grp] * lm).astype(bf16)
                sel = (lane < HEAD) if half == 0 else (lane >= HEAD)
                y_p = y_p + _nn(mb, jnp.where(sel, xdt_p, 0.0).astype(bf16))
            off_lo = _nt(q["cmb"][g_lo], s_pb)
            off = off_lo if g_lo == g_hi else jnp.where(lane < HEAD, off_lo, _nt(q["cmb"][g_hi], s_pb))
            y_p = y_p + off * q["ecs"][:, cols] + q["xs"][:, cols] * d_ref[:, cols]
            yp_ref[:, cols] = y_p
            xd_pb = xd[:, cols].astype(bf16)
            upd_lo = _tn(xd_pb, q["bmb"][g_lo])
            upd = upd_lo if g_lo == g_hi else jnp.where(rowi < HEAD, upd_lo, _tn(xd_pb, q["bmb"][g_hi]))
            cd = jnp.exp(jnp.broadcast_to(cst[:, CHUNK - 1:CHUNK], (128, B_STATE)))
            s_ref[cols, :] = cd * s_p + upd
        z = q["z"]
        yg = yp_ref[...] * (z * _sigmoid(z))
        yb_ref[...] = _rms_fwd(yg, ng_ref[...])

    vec = pl.BlockSpec((1, B_W), lambda i: (0, 0))
    row = pl.BlockSpec((CHUNK, B_W), lambda i: (i, 0))
    return pl.pallas_call(
        body, name="ssd_fwd", grid=(nc,),
        in_specs=[pl.BlockSpec((CHUNK, PB_W), lambda i: (i, 0)),
                  pl.BlockSpec((8, PB_W), lambda i: (jnp.maximum(i * (CHUNK // 8) - 1, 0), 0)),
                  *[_lspec(a, l) for a in (conv_w, conv_b, dtb, alog, dskip, norm_g)]],
        out_specs=[row, row, pl.BlockSpec((1, B_W, B_STATE), lambda i: (i, 0, 0))],
        out_shape=[jax.ShapeDtypeStruct((t, B_W), f32), jax.ShapeDtypeStruct((t, B_W), f32),
                   jax.ShapeDtypeStruct((nc, B_W, B_STATE), f32)],
        scratch_shapes=[pltpu.VMEM((8 + CHUNK, XBC_W), f32), pltpu.VMEM((B_W, B_STATE), f32)],
        compiler_params=_params(("arbitrary",)),
    )(pb, pb, conv_w, conv_b, dtb, alog, dskip, norm_g)


def _ssd_bwd(pb, yp, sprev, dyb, conv_w, conv_b, dtb, alog, dskip, norm_g, l):
    t = pb.shape[0]
    nc = t // CHUNK

    def body(pb_ref, halo_ref, yp_ref, sp_ref, dyb_ref, cw_ref, cb_ref, dtb_ref, alog_ref, d_ref, ng_ref,
             dpb_ref, dcw_ref, dcb_ref, ddtb_ref, dalog_ref, dd_ref, dng_ref,
             buf_ref, dbuf_ref, ds_ref, dnext_ref, dxbc_ref, dcs_ref, dxdt_ref):
        i = pl.program_id(0)
        c = nc - 1 - i

        @pl.when(i == 0)
        def _():
            ds_ref[...] = jnp.zeros_like(ds_ref)
            dnext_ref[...] = jnp.zeros_like(dnext_ref)
            for ref in (dcw_ref, dcb_ref, ddtb_ref, dalog_ref, dd_ref, dng_ref):
                ref[...] = jnp.zeros_like(ref)

        halo = jnp.where(c > 0, halo_ref[:, B_W:B_W + XBC_W], 0.0)
        cw = cw_ref[...]
        q = _ssd_chunk(pb_ref, halo, buf_ref, cw, cb_ref[...], dtb_ref[...], alog_ref[...])
        z, xs, dt, a, ecs, dsd, xdt =q["z"], q["xs"], q["dt"], q["a"], q["ecs"], q["ds"], q["xdt"]
        sz = _sigmoid(z)
        siluz = z * sz
        yp = yp_ref[...]
        dyg, dng = _rms_bwd(yp * siluz, ng_ref[...], dyb_ref[...])
        dng_ref[...] += dng
        dy = dyg * siluz
        dpb_ref[:, 0:B_W] = (dyg * yp * _silu_grad(z, sz)).astype(bf16)
        dd_ref[...] += jnp.sum(dy * xs, axis=0, keepdims=True)
        g1 = dy * ecs
        lane = lax.broadcasted_iota(jnp.int32, (CHUNK, 128), 1)
        rowi = lax.broadcasted_iota(jnp.int32, (128, B_STATE), 0)
        rowc = lax.broadcasted_iota(jnp.int32, (CHUNK, 128), 0)
        cb_mat = [_nt(q["cmb"][k], q["bmb"][k]) for k in range(2)]
        d_cb = [jnp.zeros((CHUNK, CHUNK), f32) for _ in range(2)]
        d_b = [jnp.zeros((CHUNK, B_STATE), f32) for _ in range(2)]
        d_c = [jnp.zeros((CHUNK, B_STATE), f32) for _ in range(2)]
        for p in range(N_PAIR):
            cols = slice(128 * p, 128 * (p + 1))
            g_lo, g_hi = _pair_groups(p)
            lo, hi = lane < HEAD, lane >= HEAD
            cs_p, xdt_p, dy_p, ds_p, g1_p = q["cs"][:, cols], xdt[:, cols], dy[:, cols], dsd[:, cols], g1[:, cols]
            s_p = sp_ref[0, cols, :]
            s_pb = s_p.astype(bf16)
            dsn = ds_ref[cols, :]
            dsnb = dsn.astype(bf16)
            g1b = g1_p.astype(bf16)
            off_lo = _nt(q["cmb"][g_lo], s_pb)
            off = off_lo if g_lo == g_hi else jnp.where(lo, off_lo, _nt(q["cmb"][g_hi], s_pb))
            dcs_p = dy_p * off * ecs[:, cols]
            dsp_lo = _tn(g1b, q["cmb"][g_lo])
            dsp = dsp_lo if g_lo == g_hi else jnp.where(rowi < HEAD, dsp_lo, _tn(g1b, q["cmb"][g_hi]))
            dx_lo = _nt(q["bmb"][g_lo], dsnb)
            dxd = dx_lo if g_lo == g_hi else jnp.where(lo, dx_lo, _nt(q["bmb"][g_hi], dsnb))
            xd_p = xdt_p * ds_p
            if g_lo == g_hi:
                d_c[g_lo] = d_c[g_lo] + _nn(g1b, s_pb)
                d_b[g_lo] = d_b[g_lo] + _nn(xd_p.astype(bf16), dsnb)
            else:
                d_c[g_lo] = d_c[g_lo] + _nn(jnp.where(lo, g1_p, 0.0).astype(bf16), s_pb)
                d_c[g_hi] = d_c[g_hi] + _nn(jnp.where(hi, g1_p, 0.0).astype(bf16), s_pb)
                d_b[g_lo] = d_b[g_lo] + _nn(jnp.where(lo, xd_p, 0.0).astype(bf16), dsnb)
                d_b[g_hi] = d_b[g_hi] + _nn(jnp.where(hi, xd_p, 0.0).astype(bf16), dsnb)
            dxdt_p = dxd * ds_p
            t2 = dxd * xdt_p * ds_p
            dcs_p = dcs_p - t2
            dlast = jnp.sum(t2, axis=0, keepdims=True)
            cst = None
            for half, grp in ((0, g_lo), (1, g_hi)):
                sel = lo if half == 0 else hi
                lm, cst = _ssd_decay(cs_p, half)
                m = cb_mat[grp] * lm
                dyh = jnp.where(sel, dy_p, 0.0).astype(bf16)
                xdh = jnp.where(sel, xdt_p, 0.0).astype(bf16)
                dm = _nt(dyh, xdh)
                pm = dm * m
                col = jnp.sum(pm, axis=1, keepdims=True) - jnp.sum(pm.T, axis=1, keepdims=True)
                dcs_p = dcs_p + jnp.where(lane == HEAD * half, col, 0.0)
                d_cb[grp] = d_cb[grp] + dm * lm
                dxdt_p = dxdt_p + _tn(m.astype(bf16), dyh)
            cdcol = jnp.exp(jnp.broadcast_to(cst[:, CHUNK - 1:CHUNK], (128, B_STATE)))
            ds_ref[cols, :] = cdcol * dsn + dsp
            dcd_row = jnp.sum((dsn * s_p).T, axis=0, keepdims=True)
            dlast = dlast + dcd_row * ecs[CHUNK - 1:CHUNK, cols]
            dcs_ref[:, cols] = dcs_p + jnp.where(rowc == CHUNK - 1, dlast, 0.0)
            dxdt_ref[:, cols] = dxdt_p
        for k in range(2):
            dcbb = d_cb[k].astype(bf16)
            d_c[k] = d_c[k] + _nn(dcbb, q["bmb"][k])
            d_b[k] = d_b[k] + _tn(dcbb, q["cmb"][k])
            dxbc_ref[:, B_W + k * B_STATE:B_W + (k + 1) * B_STATE] = d_b[k]
            dxbc_ref[:, B_W + (2 + k) * B_STATE:B_W + (3 + k) * B_STATE] = d_c[k]
        dxdt = dxdt_ref[...]
        dxbc_ref[:, 0:B_W] = dy * d_ref[...] + dxdt * dt
        dcs = dcs_ref[...]
        dad = jnp.sum(dcs, axis=0, keepdims=True) - _cumsum_rows(dcs) + dcs
        ddt = dxdt * xs + dad * a
        dalog_ref[...] += jnp.sum(dad * dt, axis=0, keepdims=True) * a
        dtraw = ddt * _sigmoid(q["dtin"])
        ddtb_ref[...] += jnp.sum(dtraw, axis=0, keepdims=True)
        dpb_ref[:, B_W + XBC_W:PB_W] = dtraw.astype(bf16)
        dpre = dxbc_ref[...] * _silu_grad(q["pre"], q["sg"])
        dx, dw, db = _conv_bwd(buf_ref, dbuf_ref, dpre, dnext_ref[...], cw, CHUNK)
        dnext_ref[...] = dpre[0:8, :]
        dcw_ref[...] += dw
        dcb_ref[...] += db
        dpb_ref[:, B_W:B_W + XBC_W] = dx.astype(bf16)

    vec = pl.BlockSpec((1, B_W), lambda i: (0, 0))
    cwspec = pl.BlockSpec((4, XBC_W), lambda i: (0, 0))
    cbspec = pl.BlockSpec((1, XBC_W), lambda i: (0, 0))

    def rev(w):
        return pl.BlockSpec((CHUNK, w), lambda i: (nc - 1 - i, 0))

    vshape = jax.ShapeDtypeStruct((1, B_W), f32)
    return pl.pallas_call(
        body, name="ssd_bwd", grid=(nc,),
        in_specs=[rev(PB_W), pl.BlockSpec((8, PB_W), lambda i: (jnp.maximum((nc - 1 - i) * (CHUNK // 8) - 1, 0), 0)),
                  rev(B_W), pl.BlockSpec((1, B_W, B_STATE), lambda i: (nc - 1 - i, 0, 0)), rev(B_W),
                  *[_lspec(a, l) for a in (conv_w, conv_b, dtb, alog, dskip, norm_g)]],
        out_specs=[rev(PB_W), cwspec, cbspec, vec, vec, vec, vec],
        out_shape=[jax.ShapeDtypeStruct((t, PB_W), bf16), jax.ShapeDtypeStruct((4, XBC_W), f32),
                   jax.ShapeDtypeStruct((1, XBC_W), f32), vshape, vshape, vshape, vshape],
        scratch_shapes=[pltpu.VMEM((8 + CHUNK, XBC_W), f32), pltpu.VMEM((CHUNK + 8, XBC_W), f32),
                        pltpu.VMEM((B_W, B_STATE), f32), pltpu.VMEM((8, XBC_W), f32),
                        pltpu.VMEM((CHUNK, XBC_W), f32), pltpu.VMEM((CHUNK, B_W), f32), pltpu.VMEM((CHUNK, B_W), f32)],
        compiler_params=_params(("arbitrary",)),
    )(pb, pb, yp, sprev, dyb, conv_w, conv_b, dtb, alog, dskip, norm_g)


def _loss_fwd(y, target):
    t, d = y.shape
    tm = _tile(t, 512)

    def body(y_ref, t_ref, dy_ref, loss_ref):
        @pl.when(pl.program_id(0) == 0)
        def _():
            loss_ref[...] = jnp.zeros_like(loss_ref)

        e = y_ref[...] - t_ref[...]
        dy_ref[...] = e * (1.0 / d)
        per_tok = jnp.mean(e * e, axis=-1, keepdims=True)
        loss_ref[...] += 0.5 * jnp.sum(per_tok, axis=0, keepdims=True)

    row = pl.BlockSpec((tm, d), lambda i: (i, 0))
    return pl.pallas_call(
        body, name="loss_fwd", grid=(t // tm,), in_specs=[row, row],
        out_specs=[row, pl.BlockSpec((1, 128), lambda i: (0, 0))],
        out_shape=[jax.ShapeDtypeStruct((t, d), f32), jax.ShapeDtypeStruct((1, 128), f32)],
        compiler_params=_params(("arbitrary",)),
    )(y, target)


def _row_tile(r):
    return 512 if r % 512 == 0 else r


def _pair_add(g, r, c_dev):
    _, nl, rows, cols = g.shape
    tr = _row_tile(rows)

    def body(c_ref, g_ref, r_ref, o_ref):
        o_ref[...] = (g_ref[...].astype(f32) + r_ref[...].astype(f32)).astype(bf16)

    blk = (None, None, tr, cols)
    return pl.pallas_call(
        body, name="pair_add",
        grid_spec=pltpu.PrefetchScalarGridSpec(
            num_scalar_prefetch=1, grid=(4, nl, rows // tr),
            in_specs=[pl.BlockSpec(blk, lambda b, l, i, c: (2 * b + c[0], l, i, 0)),
                      pl.BlockSpec(blk, lambda b, l, i, c: (b, l, i, 0))],
            out_specs=pl.BlockSpec(blk, lambda b, l, i, c: (b, l, i, 0))),
        out_shape=jax.ShapeDtypeStruct(r.shape, bf16),
        compiler_params=_params(("arbitrary", "arbitrary", "arbitrary")),
    )(c_dev, g, r)


def _grad_sum(s, q, b_dev):
    _, nl, rows, cols = s.shape
    tr = _row_tile(rows)

    def body(b_ref, s_ref, q0_ref, q1_ref, q2_ref, o_ref):
        o_ref[...] = ((s_ref[...].astype(f32) + q0_ref[...].astype(f32)) + q1_ref[...].astype(f32)) + q2_ref[...].astype(f32)

    blk = (None, None, tr, cols)

    def qspec(k):
        return pl.BlockSpec(blk, lambda l, i, b: (k, l, i, 0))

    return pl.pallas_call(
        body, name="grad_sum",
        grid_spec=pltpu.PrefetchScalarGridSpec(
            num_scalar_prefetch=1, grid=(nl, rows // tr),
            in_specs=[pl.BlockSpec(blk, lambda l, i, b: (b[0], l, i, 0)), qspec(0), qspec(1), qspec(2)],
            out_specs=pl.BlockSpec((None, tr, cols), lambda l, i, b: (l, i, 0))),
        out_shape=jax.ShapeDtypeStruct(s.shape[1:], f32),
        compiler_params=_params(("arbitrary", "arbitrary")),
    )(b_dev, s, q, q, q)


def _sum_devices(parts):
    n, rows, cols = parts.shape
    tr = _row_tile(rows)

    def body(p_ref, o_ref):
        acc = p_ref[0]
        for k in range(1, n):
            acc = acc + p_ref[k]
        o_ref[...] = acc

    return pl.pallas_call(
        body, name="sum_devices", grid=(rows // tr,),
        in_specs=[pl.BlockSpec((n, tr, cols), lambda i: (0, i, 0))],
        out_specs=pl.BlockSpec((tr, cols), lambda i: (i, 0)),
        out_shape=jax.ShapeDtypeStruct((rows, cols), f32),
        compiler_params=_params(("arbitrary",)),
    )(parts)


def _adamw(w, m, v, g):
    nl, rows, cols = w.shape
    tr = _row_tile(rows)
    tc = 128 if (tr == rows and rows * cols * 4 > ADAMW_BLOCK_BYTES and cols % 128 == 0) else cols

    def body(w_ref, m_ref, v_ref, g_ref, d_ref, nm_ref, nv_ref):
        d_ref[...], nm_ref[...], nv_ref[...] = _adamw_math(w_ref[...], m_ref[...], v_ref[...], g_ref[...])

    blk = pl.BlockSpec((None, tr, tc), lambda l, i, c: (l, i, c))
    shape = jax.ShapeDtypeStruct(w.shape, f32)
    return pl.pallas_call(
        body, name="adamw", grid=(nl, rows // tr, cols // tc), in_specs=[blk] * 4, out_specs=[blk] * 3,
        out_shape=[shape] * 3, compiler_params=_params(("arbitrary", "arbitrary", "arbitrary")),
    )(w, m, v, g)


def _adamw_math(w, m, v, g):
    nm = ADAM_B1 * m + (1.0 - ADAM_B1) * g
    nv = ADAM_B2 * v + (1.0 - ADAM_B2) * (g * g)
    m_hat = nm / (1.0 - ADAM_B1 ** ADAM_STEP)
    v_hat = nv / (1.0 - ADAM_B2 ** ADAM_STEP)
    return -ADAM_LR * (m_hat / (jnp.sqrt(v_hat) + ADAM_EPS) + ADAM_WD * w), nm, nv


def _adamw_layer(w, m, v, s, q, b_dev, outs, l, deps=()):
    _, rows, cols = w.shape
    tr = _row_tile(rows)

    def body(b_ref, w_ref, m_ref, v_ref, s_ref, q0_ref, q1_ref, q2_ref, o0, o1, o2, o3, g_ref, d_ref, nm_ref, nv_ref):
        g = ((s_ref[...].astype(f32) + q0_ref[...].astype(f32)) + q1_ref[...].astype(f32)) + q2_ref[...].astype(f32)
        g_ref[...] = g
        d_ref[...], nm_ref[...], nv_ref[...] = _adamw_math(w_ref[...], m_ref[...], v_ref[...], g)

    wspec = pl.BlockSpec((None, tr, cols), lambda i, b: (l, i, 0))
    blk = (None, None, tr, cols)

    def qspec(k):
        return pl.BlockSpec(blk, lambda i, b: (k, 0, i, 0))

    shape = jax.ShapeDtypeStruct(w.shape, f32)
    return pl.pallas_call(
        _after(body, 12, deps), name="adamw_layer",
        grid_spec=pltpu.PrefetchScalarGridSpec(
            num_scalar_prefetch=1, grid=(rows // tr,),
            in_specs=[wspec] * 3 + [pl.BlockSpec(blk, lambda i, b: (b[0], 0, i, 0)), qspec(0), qspec(1), qspec(2)]
            + [ANY] * (4 + len(deps)),
            out_specs=[wspec] * 4),
        out_shape=[shape] * 4, input_output_aliases={8 + k: k for k in range(4)},
        compiler_params=_params(("arbitrary",)),
    )(b_dev, w, m, v, s, q, q, q, *outs, *deps)


def _place():
    return lax.axis_index("x"), lax.axis_index("y"), lax.axis_index("c")


def _all_gather(shards, deps=()):
    n = len(shards)
    nd = len(deps)

    def body(*refs):
        src, dst = refs[:n], refs[n:2 * n]
        send_sems, recv_sems, local_sems = refs[2 * n:]
        x, y, c = _place()
        me, sibling = (x, y, c), (x, y, 1 - c)
        chips = [(1 - x, y), (x, 1 - y), (1 - x, 1 - y)]

        def copy(a, k, block, to, from_shard=False):
            px, py, pc = block
            rows = dst[a].at[4 * px + 2 * py + pc]
            return pltpu.make_async_remote_copy(
                src_ref=src[a] if from_shard else rows, dst_ref=rows,
                send_sem=send_sems.at[a, k], recv_sem=recv_sems.at[a, k], device_id=to, device_id_type=MESH)

        mine = [pltpu.make_async_copy(src[a], dst[a].at[4 * x + 2 * y + c], local_sems.at[a]) for a in range(n)]
        for cp in mine:
            cp.start()
        first = []
        for a in range(n):
            first.append(copy(a, 0, me, sibling, True))
            first += [copy(a, 1 + j, me, (*chip, c), True) for j, chip in enumerate(chips)]
        for cp in first:
            cp.start()
        passed = []
        for j, chip in enumerate(chips):
            for a in range(n):
                copy(a, 1 + j, (*chip, c), me).wait_recv()
                fwd = copy(a, 4 + j, (*chip, c), sibling)
                fwd.start()
                passed.append(fwd)
        for a in range(n):
            copy(a, 0, sibling, me).wait_recv()
            for j, chip in enumerate(chips):
                copy(a, 4 + j, (*chip, 1 - c), me).wait_recv()
        for cp in first + passed:
            cp.wait_send()
        for cp in mine:
            cp.wait()

    return pl.pallas_call(
        _after(body, n, deps), name="all_gather", in_specs=[ANY] * (n + nd), out_specs=[ANY] * n,
        out_shape=[jax.ShapeDtypeStruct((N_DEV,) + s.shape, s.dtype) for s in shards],
        scratch_shapes=[pltpu.SemaphoreType.DMA((n, 7)), pltpu.SemaphoreType.DMA((n, 7)), pltpu.SemaphoreType.DMA((n,))],
    )(*shards, *deps)


HBM = pl.BlockSpec(memory_space=pltpu.HBM)
SEM = pl.BlockSpec(memory_space=pltpu.SEMAPHORE)
_EFFECT = pltpu.SideEffectType.DATAFLOW_SIDE_EFFECTING


def _split_start(name, srcs, dsts, sem_shape, plan):
    ns, nb = len(srcs), len(srcs) + len(dsts)

    def body(*refs):
        send_sems, recv_sems = refs[nb], refs[nb + 1]
        for cp in plan(refs[:ns], refs[ns:nb], send_sems, recv_sems):
            cp.start()
        refs[-1][...] = jnp.zeros_like(refs[-1])

    bufs = list(srcs) + list(dsts)
    return pl.pallas_call(
        body, name=name,
        out_shape=(pltpu.SemaphoreType.DMA(sem_shape), pltpu.SemaphoreType.DMA(sem_shape),
                   *[pltpu.HBM(a.shape, a.dtype) for a in bufs], jax.ShapeDtypeStruct((8, 128), f32)),
        in_specs=[HBM] * nb, out_specs=(SEM, SEM, *[HBM] * nb, pl.BlockSpec(memory_space=pltpu.VMEM)),
        input_output_aliases={i: 2 + i for i in range(nb)},
        compiler_params=pltpu.CompilerParams(has_side_effects=_EFFECT),
    )(*[pltpu.with_memory_space_constraint(a, pltpu.HBM) for a in bufs])


def _split_wait(name, started, ns, plan, after):
    send_sems, recv_sems = started[0], started[1]
    bufs = list(started[2:-1])
    nb = len(bufs)
    after = list(after) if isinstance(after, (list, tuple)) else [after]

    def body(*refs):
        for cp in plan(refs[:ns], refs[ns:nb], refs[nb], refs[nb + 1]):
            cp.wait_send()
            cp.wait_recv()

    return pl.pallas_call(
        body, name=name, out_shape=tuple(pltpu.HBM(a.shape, a.dtype) for a in bufs),
        in_specs=[HBM] * nb + [SEM, SEM] + [ANY] * len(after), out_specs=tuple([HBM] * nb),
        input_output_aliases={i: i for i in range(nb)},
        compiler_params=pltpu.CompilerParams(has_side_effects=_EFFECT),
    )(*bufs, send_sems, recv_sems, *after)


def _remote(src, dst, send_sem, recv_sem, to):
    return pltpu.make_async_remote_copy(src_ref=src, dst_ref=dst, send_sem=send_sem, recv_sem=recv_sem,
                                        device_id=to, device_id_type=MESH)


def _gather_plan(src, dst, send_sems, recv_sems):
    x, y, c = _place()
    peers = [(x, y, 1 - c), (1 - x, y, c), (x, 1 - y, c), (1 - x, 1 - y, c)]
    copies = []
    for a in range(len(dst)):
        rows = dst[a].at[4 * x + 2 * y + c]
        copies += [_remote(rows, rows, send_sems.at[4 * a + k], recv_sems.at[4 * a + k], peer) for k, peer in enumerate(peers)]
    return copies


def _pair_plan(src, dst, send_sems, recv_sems):
    x, y, c = _place()
    return [_remote(src[a].at[2 * b + (1 - c)], dst[a].at[b], send_sems.at[4 * a + b], recv_sems.at[4 * a + b], (x, y, 1 - c))
            for a in range(len(src)) for b in range(4)]


def _chips_plan(src, dst, send_sems, recv_sems):
    x, y, c = _place()
    chips = [(1 - x, y), (x, 1 - y), (1 - x, 1 - y)]
    return [_remote(src[a].at[2 * px + py], dst[a].at[j], send_sems.at[3 * a + j], recv_sems.at[3 * a + j], (px, py, c))
            for a in range(len(src)) for j, (px, py) in enumerate(chips)]


def _forward_plan(src, dst, send_sems, recv_sems):
    x, y, c = _place()
    copies = []
    for a in range(len(dst)):
        for j, (px, py) in enumerate([(1 - x, y), (x, 1 - y), (1 - x, 1 - y)]):
            rows = dst[a].at[4 * px + 2 * py + c]
            copies.append(_remote(rows, rows, send_sems.at[3 * a + j], recv_sems.at[3 * a + j], (x, y, 1 - c)))
    return copies


def _gather_finish(bufs):
    n = len(bufs)

    def body(*refs):
        dst = refs[n:2 * n]
        send_sems, recv_sems = refs[2 * n:]
        x, y, c = _place()
        chips = [(1 - x, y), (x, 1 - y), (1 - x, 1 - y)]
        passed = []
        for a in range(n):
            for j, (px, py) in enumerate(chips):
                rows = dst[a].at[4 * px + 2 * py + c]
                passed.append(_remote(rows, rows, send_sems.at[a, j], recv_sems.at[a, j], (x, y, 1 - c)))
        for cp in passed:
            cp.start()
        for cp in passed:
            cp.wait_send()
        for a in range(n):
            for j, (px, py) in enumerate(chips):
                rows = dst[a].at[4 * px + 2 * py + (1 - c)]
                _remote(rows, rows, send_sems.at[a, j], recv_sems.at[a, j], (x, y, 1 - c)).wait_recv()

    return pl.pallas_call(
        body, name="gather_finish", in_specs=[ANY] * n, out_specs=[ANY] * n,
        out_shape=[jax.ShapeDtypeStruct(b.shape, b.dtype) for b in bufs],
        input_output_aliases={a: a for a in range(n)},
        scratch_shapes=[pltpu.SemaphoreType.DMA((n, 3)), pltpu.SemaphoreType.DMA((n, 3))],
    )(*bufs)


def _place_shards(mats, l, dev):
    n = len(mats)

    def body(dev_ref, *refs):
        for a in range(n):
            refs[n + a][...] = refs[a][...].astype(bf16)

    return pl.pallas_call(
        body, name="place_shards",
        grid_spec=pltpu.PrefetchScalarGridSpec(
            num_scalar_prefetch=1, grid=(1,),
            in_specs=[pl.BlockSpec((None,) + m.shape[1:], lambda i, dv: (l, 0, 0)) for m in mats],
            out_specs=[pl.BlockSpec((None, None) + m.shape[1:], lambda i, dv: (dv[0], 0, 0, 0)) for m in mats]),
        out_shape=[jax.ShapeDtypeStruct((N_DEV, 1) + m.shape[1:], bf16) for m in mats],
        compiler_params=_params(("arbitrary",)),
    )(dev, *mats)


BIG = ("ffn1_w_gu", "ffn1_w_down", "mix_w_in", "mix_w_out", "ffn2_w_gu", "ffn2_w_down")
SHARDED_CONV = ("lru_conv_w", "ssd_conv_w")
REPLICATED = ("ffn1_pre_g", "ffn1_post_g", "mix_pre_g", "mix_post_g", "lru_conv_b", "lru_w_r", "lru_b_r", "lru_w_i",
              "lru_b_i", "lru_lambda", "ssd_conv_b", "ssd_dt_bias", "ssd_a_log", "ssd_d", "ssd_norm_g", "sgu_ln_g",
              "sgu_ln_b", "sgu_w_s", "sgu_b_s", "ffn2_pre_g", "ffn2_post_g")
WEIGHTS = ("ffn1_pre_g", "ffn1_post_g", "ffn1_w_gu", "ffn1_w_down", "mix_pre_g", "mix_post_g", "mix_w_in", "mix_w_out",
           "lru_conv_w", "lru_conv_b", "lru_w_r", "lru_b_r", "lru_w_i", "lru_b_i", "lru_lambda", "ssd_conv_w",
           "ssd_conv_b", "ssd_dt_bias", "ssd_a_log", "ssd_d", "ssd_norm_g", "sgu_ln_g", "sgu_ln_b", "sgu_w_s", "sgu_b_s",
           "ffn2_pre_g", "ffn2_post_g", "ffn2_w_gu", "ffn2_w_down")
DT_LO = PA_W + B_W + XBC_W
N_HEADS = B_W // HEAD
PACK_COLS = 1024


def _size(shape):
    size = 1
    for dim in shape:
        size *= dim
    return size


def _pack_rows(shape):
    return -(-_size(shape) // PACK_COLS)


def _pack(arrays):
    pieces = [jnp.pad(a.reshape(-1), (0, _pack_rows(a.shape) * PACK_COLS - _size(a.shape))) for a in arrays]
    rows = sum(_pack_rows(a.shape) for a in arrays)
    if rows % 8:
        pieces.append(jnp.zeros(((8 - rows % 8) * PACK_COLS,), f32))
    return jnp.concatenate(pieces).reshape(-1, PACK_COLS)


def _unpack(packed, shapes):
    out, row = [], 0
    for s in shapes:
        nr = _pack_rows(s)
        out.append(packed[row:row + nr].reshape(-1)[:_size(s)].reshape(s))
        row += nr
    return out


def _widen_w_in(w):
    return jnp.concatenate([w[..., :DT_LO], jnp.repeat(w[..., DT_LO:DT_LO + N_HEADS], HEAD, axis=-1),
                            w[..., DT_LO + N_HEADS:]], axis=-1)


def _narrow_w_in_grad(g):
    dt = g[..., DT_LO:DT_LO + B_W]
    dt = dt.reshape(dt.shape[:-1] + (N_HEADS, HEAD)).sum(-1)
    return jnp.concatenate([g[..., :DT_LO], dt, g[..., DT_LO + B_W:]], axis=-1)


def _per_head(a):
    return a.reshape(a.shape[:-1] + (N_HEADS, HEAD)).sum(-1)


def kernel(x, ffn1_pre_g, ffn1_post_g, ffn1_w_gu, ffn1_w_down, mix_pre_g, mix_post_g, mix_w_in, mix_w_out, lru_conv_w, lru_conv_b, lru_w_r, lru_b_r, lru_w_i, lru_b_i, lru_lambda, ssd_conv_w, ssd_conv_b, ssd_dt_bias, ssd_a_log, ssd_d, ssd_norm_g, sgu_ln_g, sgu_ln_b, sgu_w_s, sgu_b_s, ffn2_pre_g, ffn2_post_g, ffn2_w_gu, ffn2_w_down, loss_target, m_ffn1_pre_g, m_ffn1_post_g, m_ffn1_w_gu, m_ffn1_w_down, m_mix_pre_g, m_mix_post_g, m_mix_w_in, m_mix_w_out, m_lru_conv_w, m_lru_conv_b, m_lru_w_r, m_lru_b_r, m_lru_w_i, m_lru_b_i, m_lru_lambda, m_ssd_conv_w, m_ssd_conv_b, m_ssd_dt_bias, m_ssd_a_log, m_ssd_d, m_ssd_norm_g, m_sgu_ln_g, m_sgu_ln_b, m_sgu_w_s, m_sgu_b_s, m_ffn2_pre_g, m_ffn2_post_g, m_ffn2_w_gu, m_ffn2_w_down, v_ffn1_pre_g, v_ffn1_post_g, v_ffn1_w_gu, v_ffn1_w_down, v_mix_pre_g, v_mix_post_g, v_mix_w_in, v_mix_w_out, v_lru_conv_w, v_lru_conv_b, v_lru_w_r, v_lru_b_r, v_lru_w_i, v_lru_b_i, v_lru_lambda, v_ssd_conv_w, v_ssd_conv_b, v_ssd_dt_bias, v_ssd_a_log, v_ssd_d, v_ssd_norm_g, v_sgu_ln_g, v_sgu_ln_b, v_sgu_w_s, v_sgu_b_s, v_ffn2_pre_g, v_ffn2_post_g, v_ffn2_w_gu, v_ffn2_w_down):
    given = dict(locals())
    w = {n: given[n] for n in WEIGHTS}
    mom = {n: given["m_" + n] for n in WEIGHTS}
    var = {n: given["v_" + n] for n in WEIGHTS}
    nl = ffn1_pre_g.shape[0]
    _, t, d = x.shape
    xi, yi, ci = _place()
    dev = 4 * xi + 2 * yi + ci
    c_dev = jnp.reshape(ci, (1,)).astype(jnp.int32)
    b_dev = jnp.reshape(2 * xi + yi, (1,)).astype(jnp.int32)

    conv_shapes = [lru_conv_w.shape, ssd_conv_w.shape]
    shards = [ffn1_w_gu, ffn1_w_down, _widen_w_in(mix_w_in), mix_w_out, ffn2_w_gu, ffn2_w_down]
    nbig = len(shards)
    dev_arr = jnp.reshape(dev, (1,)).astype(jnp.int32)
    conv_pack = _pack([lru_conv_w, ssd_conv_w])
    conv_buf = lax.dynamic_update_slice_in_dim(jnp.zeros((N_DEV,) + conv_pack.shape, f32), conv_pack[None], dev, axis=0)
    def gather_groups(l):
        return [(0, 1), (2, 3), (4, 5)] if l == 0 else [tuple(range(nbig))]

    placed, gather_started = {}, {}
    for l in range(nl):
        for gi, idx in enumerate(gather_groups(l)):
            placed[l, gi] = list(_place_shards([shards[i] for i in idx], l, dev_arr)) + (
                [conv_buf] if (l, gi) == (0, 1) else [])

    def start_gather(key):
        gather_started[key] = _split_start(f"gather_start_{key[0]}_{key[1]}", [], placed[key], (4 * len(placed[key]),),
                                           _gather_plan)

    start_gather((0, 0))

    def finish_gather(l, gi, after):
        waited = _split_wait(f"gather_wait_{l}_{gi}", gather_started[l, gi], 0, _gather_plan, after)
        return _gather_finish(list(waited))

    def conv_taps(conv_all):
        full = []
        for k, shape in enumerate(conv_shapes):
            per_dev = jnp.stack([_unpack(conv_all[s], conv_shapes)[k] for s in range(N_DEV)], axis=2)
            full.append(per_dev.reshape(shape[0], shape[1], N_DEV * shape[2]))
        return full

    def vec(a):
        return a.reshape(nl, 1, -1)

    def per_channel(a):
        return jnp.repeat(a, HEAD, axis=-1).reshape(nl, 1, B_W)

    eye = jnp.eye(A_W // HEAD, dtype=f32)

    def block_diag(a):
        return jnp.einsum("lhij,hg->lhigj", a, eye).reshape(nl, A_W, A_W).astype(bf16)

    causal = jnp.tril(jnp.ones((CHUNK, CHUNK), dtype=bool))
    p = dict(
        ffn1_pre=vec(ffn1_pre_g), ffn1_post=vec(ffn1_post_g), mix_pre=vec(mix_pre_g), mix_post=vec(mix_post_g),
        ffn2_pre=vec(ffn2_pre_g), ffn2_post=vec(ffn2_post_g),
        lru=(vec(lru_conv_b), block_diag(lru_w_r), block_diag(lru_w_i), vec(lru_b_r), vec(lru_b_i), vec(lru_lambda)),
        ssd=(vec(ssd_conv_b), per_channel(ssd_dt_bias), per_channel(ssd_a_log), per_channel(ssd_d), vec(ssd_norm_g)),
    )
    wm = jnp.where(causal, sgu_w_s, 0.0).astype(bf16)
    sgu_bias = jnp.repeat(jnp.swapaxes(sgu_b_s, 1, 2), HEAD, axis=2)
    sgu_f = (vec(sgu_ln_g), vec(sgu_ln_b), wm, sgu_bias)
    sgu_b = (vec(sgu_ln_g), vec(sgu_ln_b), wm, jnp.swapaxes(wm, 2, 3), sgu_bias)

    small_names = REPLICATED + SHARDED_CONV
    small_state = [_pack([src[n] for n in small_names])[None] for src in (w, mom, var)]
    prepared = [a for v in p.values() for a in (v if isinstance(v, tuple) else (v,))] + list(sgu_b) + small_state

    xs = x.reshape(t, d)
    saved, gathered, early_forward = [], [], {}
    for l in range(nl):
        x0 = xs
        if l == 0:
            later = [key for key in placed if key != (0, 0)]
            wgu1, wd1 = finish_gather(0, 0, [x0] + prepared + [b for key in later for b in placed[key]])
            for key in later:
                start_gather(key)
            deps = tuple(started[-1] for key, started in gather_started.items() if key != (0, 0))
        elif l in early_forward:
            wgu1, wd1, win, wout, wgu2, wd2 = _split_wait(f"forward_wait_{l}", early_forward[l], 0, _forward_plan, x0)
            deps = ()
        else:
            wgu1, wd1, win, wout, wgu2, wd2 = finish_gather(l, 0, x0)
            deps = ()
        x1, hb1, g1, u1, f1 = _ffn_fwd(x0, p["ffn1_pre"], p["ffn1_post"], wgu1, wd1, l, deps)
        if l == 0:
            win, wout, conv_all = finish_gather(0, 1, x1)
            lru_cw, ssd_cw = conv_taps(conv_all)
            p["lru"], p["ssd"] = (lru_cw,) + p["lru"], (ssd_cw,) + p["ssd"]
        hbm, pa, pb, pc = _mix_in_fwd(x1, p["mix_pre"], win, l)
        ya, h, gates = _lru_fwd(pa, *p["lru"], l)
        yb, yp, sp = _ssd_fwd(pb, *p["ssd"], l)
        yc = _sgu_fwd(pc, *sgu_f, l)
        x2, cat, m = _mix_out_fwd(x1, ya, yb, yc, p["mix_post"], wout, l)
        deps = ()
        if l == 0:
            wgu2, wd2 = finish_gather(0, 2, x2)
        elif l + 1 < nl:
            waited = _split_wait(f"gather_wait_{l + 1}_0", gather_started[l + 1, 0], 0, _gather_plan, x2)
            early_forward[l + 1] = _split_start(f"forward_start_{l + 1}", [], list(waited), (3 * nbig,), _forward_plan)
            deps = (early_forward[l + 1][-1],)
        xs, hb2, g2, u2, f2 = _ffn_fwd(x2, p["ffn2_pre"], p["ffn2_post"], wgu2, wd2, l, deps)
        gathered.append((wgu1, wd1, win, wout, wgu2, wd2))
        saved.append((x0, hb1, g1, u1, f1, x1, hbm, pa, pb, pc, h, gates, yp, sp, cat, m, x2, hb2, g2, u2, f2))
    dy, loss_part = _loss_fwd(xs, loss_target.reshape(t, d))
    loss = lax.psum(loss_part[0, 0], ("x", "y", "c"))

    small = {n: [None] * nl for n in REPLICATED + SHARDED_CONV}
    grads, delta, new_m, new_v = {}, {}, {}, {}
    fused = [n for n in BIG if n != "mix_w_in"]

    def oriented(a, n):
        return jnp.swapaxes(a, 1, 2) if n.endswith("w_gu") else a

    opt_in = {n: tuple(oriented(src[n], n) for src in (w, mom, var)) for n in fused}
    opt_out = {n: tuple(lax.empty(opt_in[n][0].shape, f32) for _ in range(4)) for n in fused}
    w_in_grads = [None] * nl
    grad_shapes = {n: (s.shape[2], s.shape[1]) if n.endswith("w_gu") else s.shape[1:] for n, s in zip(BIG, shards)}

    def start_pair(tag, lp, names, gbuf):
        landing = [lax.empty((4, 1) + grad_shapes[n], bf16) for n in names]
        started = _split_start(f"pair_start_{tag}", [gbuf[n] for n in names], landing, (4 * len(names),), _pair_plan)
        return tag, lp, names, started

    def finish_pair(pending, after):
        tag, lp, names, started = pending
        k = len(names)
        done = _split_wait(f"pair_wait_{tag}", started, k, _pair_plan, after)
        sums = [_pair_add(g, r, c_dev) for g, r in zip(done[:k], done[k:])]
        landing = [lax.empty((3,) + s.shape[1:], bf16) for s in sums]
        return tag, lp, names, _split_start(f"chips_start_{tag}", sums, landing, (3 * k,), _chips_plan)

    def finish_chips(pending, after, deps=()):
        tag, lp, names, started = pending
        k = len(names)
        done = _split_wait(f"chips_wait_{tag}", started, k, _chips_plan, after)
        last = None
        for n, s, q in zip(names, done[:k], done[k:]):
            if n == "mix_w_in":
                w_in_grads[lp] = last = _grad_sum(s, q, b_dev)
            else:
                opt_out[n] = tuple(_adamw_layer(*opt_in[n], s, q, b_dev, opt_out[n], lp, deps))
                last = opt_out[n][0]
        return last

    early = ("ffn2_w_gu", "ffn2_w_down", "mix_w_out")
    late = ("mix_w_in", "ffn1_w_gu", "ffn1_w_down")
    pending_pair = pending_chips = early_pair = early_chips = upper_started = None
    deferred = []
    names = REPLICATED + SHARDED_CONV
    assert nl > 1
    for l in reversed(range(nl)):
        x0, hb1, g1, u1, f1, x1, hbm, pa, pb, pc, h, gates, yp, sp, cat, m, x2, hb2, g2, u2, f2 = saved[l]
        wgu1, wd1, win, wout, wgu2, wd2 = gathered[l][:nbig]
        gbuf ={n: lax.empty((N_DEV, 1) + grad_shapes[n], bf16) for n in BIG}
        deps = () if pending_pair is None else (pending_pair[3][-1],)
        if l == 0:
            deps += (upper_started[-1],)
        dx2, dfb, act, dg, du, dpre, dpost = _ffn_bwd(x2, dy, f2, p["ffn2_pre"], p["ffn2_post"], g2, u2, wgu2, wd2, l, deps)
        small["ffn2_pre_g"][l], small["ffn2_post_g"][l] = dpre[0], dpost[0]
        gbuf["ffn2_w_gu"] = _wgrad_cols(hb2, dg, gbuf["ffn2_w_gu"], 0, 0)
        gbuf["ffn2_w_gu"] = _wgrad_cols(hb2, du, gbuf["ffn2_w_gu"], 0, dg.shape[0])
        gbuf["ffn2_w_down"] = _wgrad_rows(act, dfb, gbuf["ffn2_w_down"], 0)
        deps = ()
        if pending_pair is not None:
            pending_chips = finish_pair(pending_pair, dx2)
            deps = (pending_chips[3][-1],)

        dm, dya, dyb, dyc, dpost = _mix_out_bwd(dx2, m, p["mix_post"], wout, l, deps)
        small["mix_post_g"][l] = dpost[0]
        gbuf["mix_w_out"] = _wgrad_kblocks(cat, [dm], gbuf["mix_w_out"], 0)
        deps = ()
        if l == 0:
            early_pair = start_pair("0a", 0, early, gbuf)
            deps = (early_pair[3][-1],)
        dpc, dws, dbias, dlg, dlb = _sgu_bwd(pc, dyc, *sgu_b, l, deps)
        small["sgu_w_s"][l] = jnp.where(causal, dws, 0.0)
        small["sgu_b_s"][l] = dbias.reshape(CHUNK, C_W // HEAD, HEAD).sum(-1).T
        small["sgu_ln_g"][l], small["sgu_ln_b"][l] = dlg[0], dlb[0]
        dpb, dcw, dcb, ddtb, dalog, ddsk, dng = _ssd_bwd(pb, yp, sp, dyb, *p["ssd"], l)
        small["ssd_conv_w"][l], small["ssd_conv_b"][l], small["ssd_norm_g"][l] = dcw, dcb[0], dng[0]
        small["ssd_dt_bias"][l], small["ssd_a_log"][l], small["ssd_d"][l] = _per_head(ddtb[0]), _per_head(dalog[0]), _per_head(ddsk[0])
        deps = ()
        if l == 0:
            early_chips = finish_pair(early_pair, dpb)
            deps = (early_chips[3][-1],)
        dpa, dcw, dcb, dwr, dwi, dbr, dbi, dlam = _lru_bwd(pa, h, gates, dya, *p["lru"], l, deps)
        small["lru_conv_w"][l], small["lru_conv_b"][l], small["lru_lambda"][l] = dcw, dcb[0], dlam[0]
        small["lru_b_r"][l], small["lru_b_i"][l] = dbr[0], dbi[0]
        heads = range(A_W // HEAD)
        small["lru_w_r"][l] = jnp.stack([dwr[HEAD * i:HEAD * (i + 1), HEAD * i:HEAD * (i + 1)] for i in heads])
        small["lru_w_i"][l] = jnp.stack([dwi[HEAD * i:HEAD * (i + 1), HEAD * i:HEAD * (i + 1)] for i in heads])
        dx1, dpre = _mix_in_bwd(x1, dx2, p["mix_pre"], dpa, dpb, dpc, win, l)
        small["mix_pre_g"][l] = dpre[0]
        gbuf["mix_w_in"] = _wgrad_kblocks(hbm, [dpa, dpb, dpc], gbuf["mix_w_in"], 0)

        dy, dfb, act, dg, du, dpre, dpost = _ffn_bwd(x0, dx1, f1, p["ffn1_pre"], p["ffn1_post"], g1, u1, wgu1, wd1, l)
        small["ffn1_pre_g"][l], small["ffn1_post_g"][l] = dpre[0], dpost[0]
        gbuf["ffn1_w_gu"] = _wgrad_cols(hb1, dg, gbuf["ffn1_w_gu"], 0, 0)
        gbuf["ffn1_w_gu"] = _wgrad_cols(hb1, du, gbuf["ffn1_w_gu"], 0, dg.shape[0])
        gbuf["ffn1_w_down"] = _wgrad_rows(act, dfb, gbuf["ffn1_w_down"], 0)
        if pending_chips is not None:
            deferred.append(pending_chips)
            pending_chips = None
        pending_pair = start_pair(f"{l}", l, late if l == 0 else BIG, gbuf)
        if l == 1:
            upper = [jnp.stack(small[n][1:]) for n in names]
            upper_pack = _pack(upper)
            upper_buf = lax.dynamic_update_slice_in_dim(
                jnp.zeros((N_DEV,) + upper_pack.shape, f32), upper_pack[None], dev, axis=0)
            upper_started = _split_start("small_start", [], [upper_buf], (4,), _gather_plan)
    grad_x = dy.reshape(x.shape)

    lower = [jnp.stack(small[n][:1]) for n in names]
    lower_total = _sum_devices(_all_gather([_pack(lower)], (pending_pair[3][-1],))[0])
    late_chips = finish_pair(pending_pair, lower_total)
    order = lower_total
    for pending in deferred + [early_chips]:
        order = finish_chips(pending, order, (late_chips[3][-1],))
    upper_all = _gather_finish(list(_split_wait("small_wait", upper_started, 0, _gather_plan, order)))[0]
    upper_total = _sum_devices(upper_all)
    finish_chips(late_chips, [upper_total] + [opt_out[n][0] for n in fused] + [g for g in w_in_grads if g is not None])
    full = {n: jnp.concatenate([lo, up], axis=0) for n, lo, up in zip(
        names, _unpack(lower_total, [a.shape for a in lower]), _unpack(upper_total, [a.shape for a in upper]))}

    for n in fused:
        grads[n], delta[n], new_m[n], new_v[n] = (oriented(a, n) for a in opt_out[n])
    grads["mix_w_in"] = _narrow_w_in_grad(jnp.concatenate(w_in_grads, axis=0))
    delta["mix_w_in"], new_m["mix_w_in"], new_v["mix_w_in"] = _adamw(
        w["mix_w_in"], mom["mix_w_in"], var["mix_w_in"], grads["mix_w_in"])
    for n in REPLICATED:
        grads[n] = full[n]
    for n in SHARDED_CONV:
        cols = w[n].shape[2]
        grads[n] = lax.dynamic_slice_in_dim(full[n], dev * cols, cols, axis=2)
    shapes = [w[n].shape for n in names]
    packs = small_state + [_pack([grads[n] for n in names])[None]]
    for dst, packed in zip((delta, new_m, new_v), _adamw(*packs)):
        dst.update(zip(names, _unpack(packed[0], shapes)))

    return (loss, grad_x, *[grads[n] for n in WEIGHTS], *[delta[n] for n in WEIGHTS],
            *[new_m[n] for n in WEIGHTS], *[new_v[n] for n in WEIGHTS])
```

```python
import functools

import jax
import jax.numpy as jnp
from jax import lax
from jax.experimental import pallas as pl
from jax.experimental.pallas import tpu as pltpu

f32, bf16 = jnp.float32, jnp.bfloat16
MESH = pl.DeviceIdType.MESH
ANY = pl.BlockSpec(memory_space=pl.ANY)

N_DEV = 8
NORM_EPS = 1e-6
LRU_C = 8.0
CHUNK = 128
HEAD = 64
A_W, B_W, C_W = 384, 384, 256
B_STATE = 128
XBC_W = B_W + 4 * B_STATE
PA_W, PB_W, PC_W = 2 * A_W, B_W + XBC_W + B_W, 2 * C_W
IN_PAD = PA_W + PB_W + PC_W
ADAM_LR, ADAM_B1, ADAM_B2, ADAM_EPS, ADAM_WD, ADAM_STEP = 0.001, 0.9, 0.999, 1e-08, 0.01, 10
VMEM_LIMIT_BYTES = 56 * 1024 * 1024
FFN_BWD_SPLIT = 2
ADAMW_BLOCK_BYTES = 2 * 1024 * 1024
NEG_BIG = -1e30


def _params(sem=None):
    return pltpu.CompilerParams(dimension_semantics=sem, vmem_limit_bytes=VMEM_LIMIT_BYTES)


def _nn(a, b):
    return jnp.dot(a, b, preferred_element_type=f32)


def _nt(a, b):
    return lax.dot_general(a, b, (((1,), (1,)), ((), ())), preferred_element_type=f32)


def _tn(a, b):
    return lax.dot_general(a, b, (((0,), (0,)), ((), ())), preferred_element_type=f32)


def _sigmoid(x):
    return 0.5 * jnp.tanh(0.5 * x) + 0.5


def _softplus(x):
    return jnp.maximum(x, 0.0) + jnp.log(1.0 + jnp.exp(-jnp.abs(x)))


_GELU_C0, _GELU_C1 = 0.7978845608028654, 0.044715


def _gelu(x):
    t = jnp.tanh(_GELU_C0 * (x + _GELU_C1 * x * x * x))
    return 0.5 * x * (1.0 + t)


def _gelu_grad(x):
    t = jnp.tanh(_GELU_C0 * (x + _GELU_C1 * x * x * x))
    return 0.5 * (1.0 + t) + 0.5 * x * (1.0 - t * t) * _GELU_C0 * (1.0 + 3.0 * _GELU_C1 * x * x)


def _silu_grad(x, s):
    return s * (1.0 + x * (1.0 - s))


def _rms_fwd(x, g):
    r = lax.rsqrt(jnp.mean(x * x, axis=-1, keepdims=True) + NORM_EPS)
    return x * r * g


def _rms_bwd(x, g, dy):
    r = lax.rsqrt(jnp.mean(x * x, axis=-1, keepdims=True) + NORM_EPS)
    xh = x * r
    dxh = dy * g
    dx = r * (dxh - xh * jnp.mean(dxh * xh, axis=-1, keepdims=True))
    return dx, jnp.sum(dy * xh, axis=0, keepdims=True)


def _one_minus_exp(x):
    series = -x * (1.0 + x * (0.5 + x * (1.0 / 6.0 + x * (1.0 / 24.0))))
    return jnp.where(x > -0.01, series, 1.0 - jnp.exp(x))


def _cumsum_rows(x):
    row = lax.broadcasted_iota(jnp.int32, x.shape, 0)
    d = 1
    while d < x.shape[0]:
        x = x + jnp.where(row >= d, pltpu.roll(x, d, 0), 0.0)
        d *= 2
    return x


def _tile(t, cap):
    tm = min(cap, t)
    assert t % tm == 0
    return tm


def _after(body, n_in, deps):
    def wrapped(*refs):
        return body(*refs[:n_in], *refs[n_in + len(deps):])
    return wrapped


def _lspec(a, l):
    return pl.BlockSpec((None,) + a.shape[1:], lambda *_: (l,) + (0,) * (a.ndim - 1))


def _wd_rows(wd_ref):
    return wd_ref[:, 0].reshape(2 * wd_ref.shape[2], wd_ref.shape[3])


def _ffn_fwd(x, pre_g, post_g, wgu, wd, l, deps=()):
    t, d = x.shape
    nb, _, _, h = wgu.shape
    nj = nb // 2
    tm = _tile(t, 512)

    def body(x_ref, pg_ref, qg_ref, wg_ref, wu_ref, wd_ref, y_ref, hb_ref, g_ref, u_ref, f_ref, acc_ref):
        j = pl.program_id(1)

        @pl.when(j == 0)
        def _():
            hb_ref[...] = _rms_fwd(x_ref[...], pg_ref[...]).astype(bf16)
            acc_ref[...] = jnp.zeros_like(acc_ref)

        hb = hb_ref[...]
        g = _nn(hb, wg_ref[0, 0])
        u = _nn(hb, wu_ref[0, 0])
        g_ref[0] = g.astype(bf16)
        u_ref[0] = u.astype(bf16)
        a = (g * _sigmoid(g) * u).astype(bf16)
        acc_ref[...] += _nn(a, _wd_rows(wd_ref))

        @pl.when(j == nj - 1)
        def _():
            f = acc_ref[...]
            f_ref[...] = f
            y_ref[...] = x_ref[...] + 0.5 * _rms_fwd(f, qg_ref[...])

    row = pl.BlockSpec((tm, d), lambda i, j: (i, 0))
    vec = pl.BlockSpec((1, d), lambda i, j: (0, 0))
    act = pl.BlockSpec((1, tm, h), lambda i, j: (j, i, 0))
    return pl.pallas_call(
        _after(body, 6, deps), name="ffn_fwd", grid=(t // tm, nj),
        in_specs=[row, _lspec(pre_g, l), _lspec(post_g, l),
                  pl.BlockSpec((1, 1, d, h), lambda i, j: (j, 0, 0, 0)),
                  pl.BlockSpec((1, 1, d, h), lambda i, j: (j + nj, 0, 0, 0)),
                  pl.BlockSpec((2, 1, h // 2, d), lambda i, j: (j, 0, 0, 0))] + [ANY] * len(deps),
        out_specs=[row, row, act, act, row],
        out_shape=[jax.ShapeDtypeStruct((t, d), f32), jax.ShapeDtypeStruct((t, d), bf16),
                   jax.ShapeDtypeStruct((nj, t, h), bf16), jax.ShapeDtypeStruct((nj, t, h), bf16),
                   jax.ShapeDtypeStruct((t, d), f32)],
        scratch_shapes=[pltpu.VMEM((tm, d), f32)],
        compiler_params=_params(("arbitrary", "arbitrary")),
    )(x, pre_g, post_g, wgu, wgu, wd, *deps)


def _ffn_bwd(x, dy, f, pre_g, post_g, g, u, wgu, wd, l, deps=()):
    t, d = x.shape
    nj, _, h = g.shape
    tm = _tile(t, 512)

    def body(x_ref, dy_ref, f_ref, pg_ref, qg_ref, g_ref, u_ref, wg_ref, wu_ref, wd_ref,
             dx_ref, dfb_ref, a_ref, dg_ref, du_ref, dpg_ref, dqg_ref, dh_ref):
        i, j = pl.program_id(0), pl.program_id(1)

        @pl.when((i == 0) & (j == 0))
        def _():
            dpg_ref[...] = jnp.zeros_like(dpg_ref)
            dqg_ref[...] = jnp.zeros_like(dqg_ref)

        @pl.when(j == 0)
        def _():
            df, dq = _rms_bwd(f_ref[...], qg_ref[...], 0.5 * dy_ref[...])
            dfb_ref[...] = df.astype(bf16)
            dqg_ref[...] += dq
            dh_ref[...] = jnp.zeros_like(dh_ref)

        wdm, wg, wu = _wd_rows(wd_ref), wg_ref[0, 0], wu_ref[0, 0]
        sub = tm // FFN_BWD_SPLIT
        das = [_nt(dfb_ref[pl.ds(half * sub, sub), :], wdm) for half in range(FFN_BWD_SPLIT)]
        for half in range(FFN_BWD_SPLIT):
            rows = pl.ds(half * sub, sub)
            da = das[half]
            gv = g_ref[0, rows, :].astype(f32)
            uv = u_ref[0, rows, :].astype(f32)
            s = _sigmoid(gv)
            sg = gv * s
            a_ref[0, rows, :] = (sg * uv).astype(bf16)
            dg = (da * uv * _silu_grad(gv, s)).astype(bf16)
            du = (da * sg).astype(bf16)
            dg_ref[0, rows, :] = dg
            du_ref[0, rows, :] = du
            dh_ref[rows, :] += _nt(dg, wg) + _nt(du, wu)

        @pl.when(j == nj - 1)
        def _():
            dxn, dp = _rms_bwd(x_ref[...], pg_ref[...], dh_ref[...])
            dx_ref[...] = dy_ref[...] + dxn
            dpg_ref[...] += dp

    row = pl.BlockSpec((tm, d), lambda i, j: (i, 0))
    vec = pl.BlockSpec((1, d), lambda i, j: (0, 0))
    act = pl.BlockSpec((1, tm, h), lambda i, j: (j, i, 0))
    act_shape = jax.ShapeDtypeStruct((nj, t, h), bf16)
    return pl.pallas_call(
        _after(body, 10, deps), name="ffn_bwd", grid=(t // tm, nj),
        in_specs=[row, row, row, _lspec(pre_g, l), _lspec(post_g, l), act, act,
                  pl.BlockSpec((1, 1, d, h), lambda i, j: (j, 0, 0, 0)),
                  pl.BlockSpec((1, 1, d, h), lambda i, j: (j + nj, 0, 0, 0)),
                  pl.BlockSpec((2, 1, h // 2, d), lambda i, j: (j, 0, 0, 0))] + [ANY] * len(deps),
        out_specs=[row, row, act, act, act, vec, vec],
        out_shape=[jax.ShapeDtypeStruct((t, d), f32), jax.ShapeDtypeStruct((t, d), bf16),
                   act_shape, act_shape, act_shape,
                   jax.ShapeDtypeStruct((1, d), f32), jax.ShapeDtypeStruct((1, d), f32)],
        scratch_shapes=[pltpu.VMEM((tm, d), f32)],
        compiler_params=_params(("arbitrary", "arbitrary")),
    )(x, dy, f, pre_g, post_g, g, u, wgu, wgu, wd, *deps)


def _wgrad_cols(x, dy, buf, l, slot0):
    (t, k), (nj, _, n) = x.shape, dy.shape

    def body(x_ref, dy_ref, buf_ref, o_ref):
        o_ref[0, 0] = _tn(dy_ref[0], x_ref[...]).astype(bf16)

    return pl.pallas_call(
        body, name="wgrad_cols", grid=(nj,),
        in_specs=[pl.BlockSpec((t, k), lambda b: (0, 0)), pl.BlockSpec((1, t, n), lambda b: (b, 0, 0)), ANY],
        out_specs=pl.BlockSpec((1, 1, n, k), lambda b: (b + slot0, l, 0, 0)),
        out_shape=jax.ShapeDtypeStruct(buf.shape, bf16), input_output_aliases={2: 0},
        compiler_params=_params(("arbitrary",)),
    )(x, dy, buf)


def _wgrad_rows(x, dy, buf, l):
    (nj, t, k), (_, n) = x.shape, dy.shape

    def body(x_ref, dy_ref, buf_ref, o_ref):
        o_ref[:, 0] = _tn(x_ref[0], dy_ref[...]).astype(bf16).reshape(2, k // 2, n)

    return pl.pallas_call(
        body, name="wgrad_rows", grid=(nj,),
        in_specs=[pl.BlockSpec((1, t, k), lambda b: (b, 0, 0)), pl.BlockSpec((t, n), lambda b: (0, 0)), ANY],
        out_specs=pl.BlockSpec((2, 1, k // 2, n), lambda b: (b, l, 0, 0)),
        out_shape=jax.ShapeDtypeStruct(buf.shape, bf16), input_output_aliases={2: 0},
        compiler_params=_params(("arbitrary",)),
    )(x, dy, buf)


def _wgrad_kblocks(x, dys, buf, l):
    t, k = x.shape
    kb = k // N_DEV
    widths = [dy.shape[1] for dy in dys]
    n = sum(widths)
    nd = len(dys)

    def body(x_ref, *refs):
        dy_hbm, o_ref, dy_vmem = refs[:nd], refs[nd + 1], refs[nd + 2:]

        @pl.when(pl.program_id(0) == 0)
        def _():
            for src, dst in zip(dy_hbm, dy_vmem):
                pltpu.sync_copy(src, dst)

        off = 0
        for dst, w in zip(dy_vmem, widths):
            o_ref[0, 0, :, off:off + w] = _tn(x_ref[...], dst[...]).astype(bf16)
            off += w

    return pl.pallas_call(
        body, name="wgrad_kblocks", grid=(N_DEV,),
        in_specs=[pl.BlockSpec((t, kb), lambda s: (0, s))] + [ANY] * (nd + 1),
        out_specs=pl.BlockSpec((1, 1, kb, n), lambda s: (s, l, 0, 0)),
        out_shape=jax.ShapeDtypeStruct(buf.shape, bf16), input_output_aliases={nd + 1: 0},
        scratch_shapes=[pltpu.VMEM((t, w), bf16) for w in widths],
        compiler_params=_params(("arbitrary",)),
    )(x, *dys, buf)


def _gathered_rows(w_ref, lo, hi):
    return w_ref[:, 0, :, lo:hi].reshape(N_DEV * w_ref.shape[2], hi - lo)


def _gathered_spec(w):
    return pl.BlockSpec((N_DEV, 1) + w.shape[2:], lambda i: (0, 0, 0, 0))


def _mix_in_fwd(x, pre_g, w_in, l):
    t, d = x.shape
    tm = _tile(t, 512)

    def body(x_ref, g_ref, w_ref, hb_ref, pa_ref, pb_ref, pc_ref):
        hb = _rms_fwd(x_ref[...], g_ref[...]).astype(bf16)
        hb_ref[...] = hb
        pa_ref[...] = _nn(hb, _gathered_rows(w_ref, 0, PA_W))
        pb_ref[...] = _nn(hb, _gathered_rows(w_ref, PA_W, PA_W + PB_W))
        pc_ref[...] = _nn(hb, _gathered_rows(w_ref, PA_W + PB_W, IN_PAD))

    def row(w):
        return pl.BlockSpec((tm, w), lambda i: (i, 0))

    return pl.pallas_call(
        body, name="mix_in_fwd", grid=(t // tm,),
        in_specs=[row(d), _lspec(pre_g, l), _gathered_spec(w_in)],
        out_specs=[row(d), row(PA_W), row(PB_W), row(PC_W)],
        out_shape=[jax.ShapeDtypeStruct((t, d), bf16), jax.ShapeDtypeStruct((t, PA_W), f32),
                   jax.ShapeDtypeStruct((t, PB_W), f32), jax.ShapeDtypeStruct((t, PC_W), f32)],
        compiler_params=_params(("arbitrary",)),
    )(x, pre_g, w_in)


def _mix_in_bwd(x, dy, pre_g, dpa, dpb, dpc, w_in, l):
    t, d = x.shape
    tm = _tile(t, 512)

    def body(x_ref, dy_ref, g_ref, dpa_ref, dpb_ref, dpc_ref, w_ref, dx_ref, dg_ref):
        @pl.when(pl.program_id(0) == 0)
        def _():
            dg_ref[...] = jnp.zeros_like(dg_ref)

        wa, wb, wc = (_gathered_rows(w_ref, 0, PA_W), _gathered_rows(w_ref, PA_W, PA_W + PB_W),
                      _gathered_rows(w_ref, PA_W + PB_W, IN_PAD))
        halves = [pl.ds(k * (tm // 2), tm // 2) for k in range(2)]
        dhs = [_nt(dpa_ref[rows, :], wa) + _nt(dpb_ref[rows, :], wb) + _nt(dpc_ref[rows, :], wc) for rows in halves]
        for rows, dh in zip(halves, dhs):
            dxn, dg = _rms_bwd(x_ref[rows, :], g_ref[...], dh)
            dx_ref[rows, :] = dy_ref[rows, :] + dxn
            dg_ref[...] += dg

    def row(w):
        return pl.BlockSpec((tm, w), lambda i: (i, 0))

    vec = pl.BlockSpec((1, d), lambda i: (0, 0))
    return pl.pallas_call(
        body, name="mix_in_bwd", grid=(t // tm,),
        in_specs=[row(d), row(d), _lspec(pre_g, l), row(PA_W), row(PB_W), row(PC_W), _gathered_spec(w_in)],
        out_specs=[row(d), vec],
        out_shape=[jax.ShapeDtypeStruct((t, d), f32), jax.ShapeDtypeStruct((1, d), f32)],
        compiler_params=_params(("arbitrary",)),
    )(x, dy, pre_g, dpa, dpb, dpc, w_in)


def _mix_out_fwd(x, ya, yb, yc, post_g, w_out, l):
    t, d = x.shape
    tm = _tile(t, 512)

    def body(x_ref, ya_ref, yb_ref, yc_ref, g_ref, w_ref, y_ref, cat_ref, m_ref):
        cat_ref[:, 0:A_W] = ya_ref[...]
        cat_ref[:, A_W:A_W + B_W] = yb_ref[...]
        cat_ref[:, A_W + B_W:d] = yc_ref[...]
        m = _nn(cat_ref[...], _gathered_rows(w_ref, 0, d))
        m_ref[...] = m
        y_ref[...] = x_ref[...] + _rms_fwd(m, g_ref[...])

    def row(w):
        return pl.BlockSpec((tm, w), lambda i: (i, 0))

    return pl.pallas_call(
        body, name="mix_out_fwd", grid=(t // tm,),
        in_specs=[row(d), row(A_W), row(B_W), row(C_W), _lspec(post_g, l), _gathered_spec(w_out)],
        out_specs=[row(d), row(d), row(d)],
        out_shape=[jax.ShapeDtypeStruct((t, d), f32), jax.ShapeDtypeStruct((t, d), bf16), jax.ShapeDtypeStruct((t, d), f32)],
        compiler_params=_params(("arbitrary",)),
    )(x, ya, yb, yc, post_g, w_out)


def _mix_out_bwd(dy, m, post_g, w_out, l, deps=()):
    t, d = m.shape
    tm = _tile(t, 512)

    def body(dy_ref, m_ref, g_ref, w_ref, dm_ref, dya_ref, dyb_ref, dyc_ref, dg_ref):
        @pl.when(pl.program_id(0) == 0)
        def _():
            dg_ref[...] = jnp.zeros_like(dg_ref)

        dm, dg = _rms_bwd(m_ref[...], g_ref[...], dy_ref[...])
        dmb = dm.astype(bf16)
        dm_ref[...] = dmb
        dg_ref[...] += dg
        dcat = _nt(dmb, _gathered_rows(w_ref, 0, d))
        dya_ref[...] = dcat[:, 0:A_W]
        dyb_ref[...] = dcat[:, A_W:A_W + B_W]
        dyc_ref[...] = dcat[:, A_W + B_W:d]

    def row(w):
        return pl.BlockSpec((tm, w), lambda i: (i, 0))

    vec = pl.BlockSpec((1, d), lambda i: (0, 0))
    return pl.pallas_call(
        _after(body, 4, deps), name="mix_out_bwd", grid=(t // tm,),
        in_specs=[row(d), row(d), _lspec(post_g, l), _gathered_spec(w_out)] + [ANY] * len(deps),
        out_specs=[row(d), row(A_W), row(B_W), row(C_W), vec],
        out_shape=[jax.ShapeDtypeStruct((t, d), bf16), jax.ShapeDtypeStruct((t, A_W), f32),
                   jax.ShapeDtypeStruct((t, B_W), f32), jax.ShapeDtypeStruct((t, C_W), f32),
                   jax.ShapeDtypeStruct((1, d), f32)],
        compiler_params=_params(("arbitrary",)),
    )(dy, m, post_g, w_out, *deps)


def _conv_fwd(buf_ref, halo, x, w, b, n):
    buf_ref[0:8, :] = halo
    buf_ref[8:8 + n, :] = x
    out = b + w[3:4, :] * x
    for k in range(3):
        out = out + w[k:k + 1, :] * buf_ref[pl.ds(5 + k, n), :]
    return out


def _conv_bwd(buf_ref, dbuf_ref, dout, dnext, w, n):
    dbuf_ref[0:n, :] = dout
    dbuf_ref[n:n + 8, :] = dnext
    dx = w[3:4, :] * dout
    dws = []
    for k in range(3):
        dx = dx + w[k:k + 1, :] * dbuf_ref[pl.ds(3 - k, n), :]
        dws.append(jnp.sum(dout * buf_ref[pl.ds(5 + k, n), :], axis=0, keepdims=True))
    dws.append(jnp.sum(dout * buf_ref[pl.ds(8, n), :], axis=0, keepdims=True))
    return dx, jnp.concatenate(dws, axis=0), jnp.sum(dout, axis=0, keepdims=True)


def _lru_gates(rec, wr, wi, br, bi, lam):
    rb = rec.astype(bf16)
    r = _sigmoid(_nn(rb, wr) + br)
    ig = _sigmoid(_nn(rb, wi) + bi)
    sp = _softplus(-lam)
    la = -LRU_C * r * sp
    a = jnp.exp(la)
    mult = jnp.sqrt(_one_minus_exp(2.0 * la))
    return rb, r, ig, sp, a, mult


def _scan_rows(a_ref, b_ref, o_ref, carry, n, reverse):
    row = lax.broadcasted_iota(jnp.int32, (8, a_ref.shape[1]), 0)
    nb = n // 8

    def step(k, carry):
        blk = (nb - 1 - k) if reverse else k
        rows = pl.ds(pl.multiple_of(blk * 8, 8), 8)
        a, b = a_ref[rows, :], b_ref[rows, :]
        for d in (1, 2, 4):
            shift = 8 - d if reverse else d
            keep = (row < 8 - d) if reverse else (row >= d)
            b = a * jnp.where(keep, pltpu.roll(b, shift, 0), 0.0) + b
            a = a * jnp.where(keep, pltpu.roll(a, shift, 0), 1.0)
        o = a * carry + b
        o_ref[rows, :] = o
        return o[0:1, :] if reverse else o[7:8, :]

    return lax.fori_loop(0, nb, step, carry, unroll=2)


N_GATES = 5
LRU_SUB = 128


def _lru_fwd(pa, conv_w, conv_b, wr, wi, br, bi, lam, l):
    t = pa.shape[0]
    tc = _tile(t, 512)

    def body(pa_ref, halo_ref, cw_ref, cb_ref, wr_ref, wi_ref, br_ref, bi_ref, lam_ref,
             ya_ref, h_ref, gates_ref, buf_ref, u_ref, carry_ref):
        i = pl.program_id(0)

        @pl.when(i == 0)
        def _():
            carry_ref[...] = jnp.zeros_like(carry_ref)

        halo = jnp.where(i > 0, halo_ref[:, A_W:PA_W], 0.0)
        rec = _conv_fwd(buf_ref, halo, pa_ref[:, A_W:PA_W], cw_ref[...], cb_ref[...], tc)
        _, r, ig, _, a, mult = _lru_gates(rec, wr_ref[...], wi_ref[...], br_ref[...], bi_ref[...], lam_ref[...])
        for k, val in enumerate((rec, r, ig, a, mult)):
            gates_ref[k] = val
        u_ref[...] = mult * (ig * rec)

        carry_ref[...] = _scan_rows(gates_ref.at[3], u_ref, h_ref, carry_ref[...], tc, reverse=False)
        ya_ref[...] = (h_ref[...] * _gelu(pa_ref[:, 0:A_W])).astype(bf16)

    vec = pl.BlockSpec((1, A_W), lambda i: (0, 0))
    mat = pl.BlockSpec((A_W, A_W), lambda i: (0, 0))
    row = pl.BlockSpec((tc, A_W), lambda i: (i, 0))
    return pl.pallas_call(
        body, name="lru_fwd", grid=(t // tc,),
        in_specs=[pl.BlockSpec((tc, PA_W), lambda i: (i, 0)),
                  pl.BlockSpec((8, PA_W), lambda i: (jnp.maximum(i * (tc // 8) - 1, 0), 0)),
                  *[_lspec(a, l) for a in (conv_w, conv_b, wr, wi, br, bi, lam)]],
        out_specs=[row, row, pl.BlockSpec((N_GATES, tc, A_W), lambda i: (0, i, 0))],
        out_shape=[jax.ShapeDtypeStruct((t, A_W), bf16), jax.ShapeDtypeStruct((t, A_W), f32),
                   jax.ShapeDtypeStruct((N_GATES, t, A_W), f32)],
        scratch_shapes=[pltpu.VMEM((8 + tc, A_W), f32), pltpu.VMEM((tc, A_W), f32), pltpu.VMEM((1, A_W), f32)],
        compiler_params=_params(("arbitrary",)),
    )(pa, pa, conv_w, conv_b, wr, wi, br, bi, lam)


def _lru_bwd(pa, h, gates, dya, conv_w, conv_b, wr, wi, br, bi, lam, l, deps=()):
    t = pa.shape[0]
    tc = _tile(t, 512)
    nc = t // tc

    def body(pa_ref, halo_ref, h_ref, hhalo_ref, gates_ref, dya_ref, cw_ref, cb_ref, wr_ref, wi_ref, br_ref, bi_ref,
             lam_ref, dpa_ref, dcw_ref, dcb_ref, dwr_ref, dwi_ref, dbr_ref, dbi_ref, dlam_ref,
             buf_ref, dbuf_ref, hbuf_ref, g_ref, dh_ref, carry_ref, dnext_ref, dhbuf_ref, gg_ref):
        i = pl.program_id(0)
        c = nc - 1 - i

        @pl.when(i == 0)
        def _():
            carry_ref[...] = jnp.zeros_like(carry_ref)
            dnext_ref[...] = jnp.zeros_like(dnext_ref)
            for ref in (dcw_ref, dcb_ref, dwr_ref, dwi_ref, dbr_ref, dbi_ref, dlam_ref):
                ref[...] = jnp.zeros_like(ref)

        halo = jnp.where(c > 0, halo_ref[:, A_W:PA_W], 0.0)
        cw = cw_ref[...]
        buf_ref[0:8, :] = halo
        buf_ref[8:8 + tc, :] = pa_ref[:, A_W:PA_W]
        lam = lam_ref[...]
        sp = _softplus(-lam)
        hbuf_ref[0:8, :] = jnp.where(c > 0, hhalo_ref[...], 0.0)
        hbuf_ref[8:8 + tc, :] = h_ref[...]
        gate = pa_ref[:, 0:A_W]
        dya = dya_ref[...]
        dpa_ref[:, 0:A_W] = (dya * h_ref[...] * _gelu_grad(gate)).astype(bf16)
        gg = dya * _gelu(gate)
        gg_ref[...] = gg
        g_ref[...] = gates_ref[3] * gg
        carry_in = carry_ref[...]
        carry_ref[...] = _scan_rows(gates_ref.at[3], g_ref, dh_ref, carry_in, tc, reverse=True)
        dhbuf_ref[0:tc, :] = dh_ref[...]
        dhbuf_ref[tc:tc + 8, :] = jnp.broadcast_to(carry_in, (8, A_W))
        for sb in range(tc // LRU_SUB):
            lo = sb * LRU_SUB
            rows = pl.ds(lo, LRU_SUB)
            rec, r, ig, a, mult = (gates_ref[k, rows, :] for k in range(N_GATES))
            rb = rec.astype(bf16)
            dh = gg_ref[rows, :] + dhbuf_ref[pl.ds(lo + 1, LRU_SUB), :]
            da = dh * hbuf_ref[pl.ds(lo + 7, LRU_SUB), :]
            dmult = dh * ig * rec
            dig = dh * mult * rec
            dla = da * a - dmult * (a * a) / mult
            dr = dla * (-LRU_C * sp)
            dlam_ref[...] += jnp.sum(dla * (-LRU_C * r), axis=0, keepdims=True) * (-_sigmoid(-lam))
            dpr = (dr * r * (1.0 - r))
            dpi = (dig * ig * (1.0 - ig))
            dprb, dpib = dpr.astype(bf16), dpi.astype(bf16)
            g_ref[rows, :] = dh * mult * ig + _nt(dprb, wr_ref[...]) + _nt(dpib, wi_ref[...])
            dwr_ref[...] += _tn(rb, dprb)
            dwi_ref[...] += _tn(rb, dpib)
            dbr_ref[...] += jnp.sum(dpr, axis=0, keepdims=True)
            dbi_ref[...] += jnp.sum(dpi, axis=0, keepdims=True)
        drec = g_ref[...]
        dx, dw, db = _conv_bwd(buf_ref, dbuf_ref, drec, dnext_ref[...], cw, tc)
        dnext_ref[...] = drec[0:8, :]
        dcw_ref[...] += dw
        dcb_ref[...] += db
        dpa_ref[:, A_W:PA_W] = dx.astype(bf16)

    vec = pl.BlockSpec((1, A_W), lambda i: (0, 0))
    mat = pl.BlockSpec((A_W, A_W), lambda i: (0, 0))
    cwspec = pl.BlockSpec((4, A_W), lambda i: (0, 0))

    def rev(w):
        return pl.BlockSpec((tc, w), lambda i: (nc - 1 - i, 0))

    def halo(w):
        return pl.BlockSpec((8, w), lambda i: (jnp.maximum((nc - 1 - i) * (tc // 8) - 1, 0), 0))

    chunk = pltpu.VMEM((tc, A_W), f32)
    return pl.pallas_call(
        _after(body, 13, deps), name="lru_bwd", grid=(nc,),
        in_specs=[rev(PA_W), halo(PA_W), rev(A_W), halo(A_W),
                  pl.BlockSpec((N_GATES, tc, A_W), lambda i: (0, nc - 1 - i, 0)), rev(A_W),
                  *[_lspec(a, l) for a in (conv_w, conv_b, wr, wi, br, bi, lam)]] + [ANY] * len(deps),
        out_specs=[rev(PA_W), cwspec, vec, mat, mat, vec, vec, vec],
        out_shape=[jax.ShapeDtypeStruct((t, PA_W), bf16), jax.ShapeDtypeStruct((4, A_W), f32),
                   jax.ShapeDtypeStruct((1, A_W), f32), jax.ShapeDtypeStruct((A_W, A_W), f32),
                   jax.ShapeDtypeStruct((A_W, A_W), f32), jax.ShapeDtypeStruct((1, A_W), f32),
                   jax.ShapeDtypeStruct((1, A_W), f32), jax.ShapeDtypeStruct((1, A_W), f32)],
        scratch_shapes=[pltpu.VMEM((8 + tc, A_W), f32), pltpu.VMEM((tc + 8, A_W), f32), pltpu.VMEM((8 + tc, A_W), f32),
                        chunk, chunk, pltpu.VMEM((1, A_W), f32), pltpu.VMEM((8, A_W), f32),
                        pltpu.VMEM((tc + 8, A_W), f32), chunk],
        compiler_params=_params(("arbitrary",)),
    )(pa, pa, h, h, gates, dya, conv_w, conv_b, wr, wi, br, bi, lam, *deps)


def _sgu_norm(v, g, b):
    mu = jnp.mean(v, axis=-1, keepdims=True)
    vc = v - mu
    rstd = lax.rsqrt(jnp.mean(vc * vc, axis=-1, keepdims=True) + NORM_EPS)
    vh = vc * rstd
    return vh, rstd, vh * g + b


def _sgu_mix(w_ref, vb, bias):
    grp = lax.broadcasted_iota(jnp.int32, (CHUNK, C_W), 1) // HEAD
    out = bias
    for gi in range(C_W // HEAD):
        out = out + jnp.where(grp == gi, _nn(w_ref[gi], vb), 0.0)
    return out


def _sgu_fwd(pc, ln_g, ln_b, wm, bias, l):
    t = pc.shape[0]
    tm = _tile(t, 512)

    def body(pc_ref, g_ref, b_ref, w_ref, bias_ref, yc_ref):
        for ci in range(tm // CHUNK):
            rows = pl.ds(ci * CHUNK, CHUNK)
            ge = _gelu(pc_ref[rows, :])
            _, _, vn = _sgu_norm(ge[:, C_W:PC_W], g_ref[...], b_ref[...])
            yc_ref[rows, :] = (ge[:, 0:C_W] * _sgu_mix(w_ref, vn.astype(bf16), bias_ref[...])).astype(bf16)

    vec = pl.BlockSpec((1, C_W), lambda i: (0, 0))
    return pl.pallas_call(
        body, name="sgu_fwd", grid=(t // tm,),
        in_specs=[pl.BlockSpec((tm, PC_W), lambda i: (i, 0)), *[_lspec(a, l) for a in (ln_g, ln_b, wm, bias)]],
        out_specs=pl.BlockSpec((tm, C_W), lambda i: (i, 0)),
        out_shape=jax.ShapeDtypeStruct((t, C_W), bf16),
        compiler_params=_params(("arbitrary",)),
    )(pc, ln_g, ln_b, wm, bias)


def _sgu_bwd(pc, dyc, ln_g, ln_b, wm, wmt, bias, l, deps=()):
    t = pc.shape[0]
    tm = _tile(t, 512)

    def body(pc_ref, dyc_ref, g_ref, b_ref, w_ref, wt_ref, bias_ref, dpc_ref, dw_ref, dbias_ref, dg_ref, db_ref):
        @pl.when(pl.program_id(0) == 0)
        def _():
            for ref in (dw_ref, dbias_ref, dg_ref, db_ref):
                ref[...] = jnp.zeros_like(ref)

        grp = lax.broadcasted_iota(jnp.int32, (CHUNK, C_W), 1) // HEAD
        for ci in range(tm // CHUNK):
            rows = pl.ds(ci * CHUNK, CHUNK)
            x = pc_ref[rows, :]
            ge = _gelu(x)
            gv = g_ref[...]
            vh, rstd, vn = _sgu_norm(ge[:, C_W:PC_W], gv, b_ref[...])
            vb = vn.astype(bf16)
            mixed = _sgu_mix(w_ref, vb, bias_ref[...])
            dyc = dyc_ref[rows, :]
            du = dyc * mixed
            dmix = dyc * ge[:, 0:C_W]
            dmb = dmix.astype(bf16)
            dvn = jnp.zeros((CHUNK, C_W), f32)
            for gi in range(C_W // HEAD):
                dvn = dvn + jnp.where(grp == gi, _nn(wt_ref[gi], dmb), 0.0)
                dw_ref[gi] += _nt(jnp.where(grp == gi, dmix, 0.0).astype(bf16), vb)
            dbias_ref[...] += dmix
            dg_ref[...] += jnp.sum(dvn * vh, axis=0, keepdims=True)
            db_ref[...] += jnp.sum(dvn, axis=0, keepdims=True)
            dvh = dvn * gv
            dv = rstd * (dvh - jnp.mean(dvh, axis=-1, keepdims=True) - vh * jnp.mean(dvh * vh, axis=-1, keepdims=True))
            gg = _gelu_grad(x)
            dpc_ref[rows, 0:C_W] = (du * gg[:, 0:C_W]).astype(bf16)
            dpc_ref[rows, C_W:PC_W] = (dv * gg[:, C_W:PC_W]).astype(bf16)

    vec = pl.BlockSpec((1, C_W), lambda i: (0, 0))
    wspec = pl.BlockSpec((4, CHUNK, CHUNK), lambda i: (0, 0, 0))
    bspec = pl.BlockSpec((CHUNK, C_W), lambda i: (0, 0))
    return pl.pallas_call(
        _after(body, 7, deps), name="sgu_bwd", grid=(t // tm,),
        in_specs=[pl.BlockSpec((tm, PC_W), lambda i: (i, 0)), pl.BlockSpec((tm, C_W), lambda i: (i, 0)),
                  *[_lspec(a, l) for a in (ln_g, ln_b, wm, wmt, bias)]] + [ANY] * len(deps),
        out_specs=[pl.BlockSpec((tm, PC_W), lambda i: (i, 0)), wspec, bspec, vec, vec],
        out_shape=[jax.ShapeDtypeStruct((t, PC_W), bf16), jax.ShapeDtypeStruct((4, CHUNK, CHUNK), f32),
                   jax.ShapeDtypeStruct((CHUNK, C_W), f32), jax.ShapeDtypeStruct((1, C_W), f32),
                   jax.ShapeDtypeStruct((1, C_W), f32)],
        compiler_params=_params(("arbitrary",)),
    )(pc, dyc, ln_g, ln_b, wm, wmt, bias, *deps)


N_PAIR = B_W // 128
HEADS_PER_GROUP = 3


def _pair_groups(p):
    return (2 * p) // HEADS_PER_GROUP, (2 * p + 1) // HEADS_PER_GROUP


def _ssd_chunk(pb_ref, halo, buf_ref, cw, cb, dtb, alog):
    z = pb_ref[:, 0:B_W]
    pre = _conv_fwd(buf_ref, halo, pb_ref[:, B_W:B_W + XBC_W], cw, cb, CHUNK)
    sg = _sigmoid(pre)
    xbc = pre * sg
    xs = xbc[:, 0:B_W]
    bm = [xbc[:, B_W + k * B_STATE:B_W + (k + 1) * B_STATE] for k in range(2)]
    cm = [xbc[:, B_W + (2 + k) * B_STATE:B_W + (3 + k) * B_STATE] for k in range(2)]
    dtin = pb_ref[:, B_W + XBC_W:PB_W] + dtb
    dt = _softplus(dtin)
    a = -jnp.exp(alog)
    cs = _cumsum_rows(dt * a)
    return dict(z=z, pre=pre, sg=sg, xs=xs, bm=bm, cm=cm, dtin=dtin, dt=dt, a=a, cs=cs,
                ecs=jnp.exp(cs), ds=jnp.exp(cs[CHUNK - 1:CHUNK, :] - cs), xdt=xs * dt,
                bmb=[v.astype(bf16) for v in bm], cmb=[v.astype(bf16) for v in cm])


def _ssd_decay(cs_pair, half):
    cst = cs_pair.T
    lane0 = HEAD * half
    csc = jnp.broadcast_to(cs_pair[:, lane0:lane0 + 1], (CHUNK, CHUNK))
    csr = cst[lane0:lane0 + 1, :]
    tri = lax.broadcasted_iota(jnp.int32, (CHUNK, CHUNK), 0) >= lax.broadcasted_iota(jnp.int32, (CHUNK, CHUNK), 1)
    return jnp.exp(jnp.where(tri, csc - csr, NEG_BIG)), cst


def _ssd_fwd(pb, conv_w, conv_b, dtb, alog, dskip, norm_g, l):
    t = pb.shape[0]
    nc = t // CHUNK

    def body(pb_ref, halo_ref, cw_ref, cb_ref, dtb_ref, alog_ref, d_ref, ng_ref, yb_ref, yp_ref, sp_ref, buf_ref, s_ref):
        i = pl.program_id(0)

        @pl.when(i == 0)
        def _():
            s_ref[...] = jnp.zeros_like(s_ref)

        halo = jnp.where(i > 0, halo_ref[:, B_W:B_W + XBC_W], 0.0)
        q = _ssd_chunk(pb_ref, halo, buf_ref, cw_ref[...], cb_ref[...], dtb_ref[...], alog_ref[...])
        sp_ref[0] = s_ref[...]
        lane = lax.broadcasted_iota(jnp.int32, (CHUNK, 128), 1)
        rowi = lax.broadcasted_iota(jnp.int32, (128, B_STATE), 0)
        cb_mat = [_nt(q["cmb"][k], q["bmb"][k]) for k in range(2)]
        xd = q["xdt"] * q["ds"]
        for p in range(N_PAIR):
            cols = slice(128 * p, 128 * (p + 1))
            g_lo, g_hi = _pair_groups(p)
            cs_p, xdt_p = q["cs"][:, cols], q["xdt"][:, cols]
            s_p = s_ref[cols, :]
            s_pb = s_p.astype(bf16)
            y_p = jnp.zeros((CHUNK, 128), f32)
            for half, grp in ((0, g_lo), (1, g_hi)):
                lm, cst = _ssd_decay(cs_p, half)
                mb = (cb_mat[grp] * lm).astype(bf16)
                sel = (lane < HEAD) if half == 0 else (lane >= HEAD)
                y_p = y_p + _nn(mb, jnp.where(sel, xdt_p, 0.0).astype(bf16))
            off_lo = _nt(q["cmb"][g_lo], s_pb)
            off = off_lo if g_lo == g_hi else jnp.where(lane < HEAD, off_lo, _nt(q["cmb"][g_hi], s_pb))
            y_p = y_p + off * q["ecs"][:, cols] + q["xs"][:, cols] * d_ref[:, cols]
            yp_ref[:, cols] = y_p
            xd_pb = xd[:, cols].astype(bf16)
            upd_lo = _tn(xd_pb, q["bmb"][g_lo])
            upd = upd_lo if g_lo == g_hi else jnp.where(rowi < HEAD, upd_lo, _tn(xd_pb, q["bmb"][g_hi]))
            cd = jnp.exp(jnp.broadcast_to(cst[:, CHUNK - 1:CHUNK], (128, B_STATE)))
            s_ref[cols, :] = cd * s_p + upd
        z = q["z"]
        yg = yp_ref[...] * (z * _sigmoid(z))
        yb_ref[...] = _rms_fwd(yg, ng_ref[...]).astype(bf16)

    vec = pl.BlockSpec((1, B_W), lambda i: (0, 0))
    row = pl.BlockSpec((CHUNK, B_W), lambda i: (i, 0))
    return pl.pallas_call(
        body, name="ssd_fwd", grid=(nc,),
        in_specs=[pl.BlockSpec((CHUNK, PB_W), lambda i: (i, 0)),
                  pl.BlockSpec((8, PB_W), lambda i: (jnp.maximum(i * (CHUNK // 8) - 1, 0), 0)),
                  *[_lspec(a, l) for a in (conv_w, conv_b, dtb, alog, dskip, norm_g)]],
        out_specs=[row, row, pl.BlockSpec((1, B_W, B_STATE), lambda i: (i, 0, 0))],
        out_shape=[jax.ShapeDtypeStruct((t, B_W), bf16), jax.ShapeDtypeStruct((t, B_W), f32),
                   jax.ShapeDtypeStruct((nc, B_W, B_STATE), f32)],
        scratch_shapes=[pltpu.VMEM((8 + CHUNK, XBC_W), f32), pltpu.VMEM((B_W, B_STATE), f32)],
        compiler_params=_params(("arbitrary",)),
    )(pb, pb, conv_w, conv_b, dtb, alog, dskip, norm_g)


def _ssd_bwd(pb, yp, sprev, dyb, conv_w, conv_b, dtb, alog, dskip, norm_g, l):
    t = pb.shape[0]
    nc = t // CHUNK

    def body(pb_ref, halo_ref, yp_ref, sp_ref, dyb_ref, cw_ref, cb_ref, dtb_ref, alog_ref, d_ref, ng_ref,
             dpb_ref, dcw_ref, dcb_ref, ddtb_ref, dalog_ref, dd_ref, dng_ref,
             buf_ref, dbuf_ref, ds_ref, dnext_ref, dxbc_ref, dcs_ref, dxdt_ref):
        i = pl.program_id(0)
        c = nc - 1 - i

        @pl.when(i == 0)
        def _():
            ds_ref[...] = jnp.zeros_like(ds_ref)
            dnext_ref[...] = jnp.zeros_like(dnext_ref)
            for ref in (dcw_ref, dcb_ref, ddtb_ref, dalog_ref, dd_ref, dng_ref):
                ref[...] = jnp.zeros_like(ref)

        halo = jnp.where(c > 0, halo_ref[:, B_W:B_W + XBC_W], 0.0)
        cw = cw_ref[...]
        q = _ssd_chunk(pb_ref, halo, buf_ref, cw, cb_ref[...], dtb_ref[...], alog_ref[...])
        z, xs, dt, a, ecs, dsd, xdt =q["z"], q["xs"], q["dt"], q["a"], q["ecs"], q["ds"], q["xdt"]
        sz = _sigmoid(z)
        siluz = z * sz
        yp = yp_ref[...]
        dyg, dng = _rms_bwd(yp * siluz, ng_ref[...], dyb_ref[...])
        dng_ref[...] += dng
        dy = dyg * siluz
        dpb_ref[:, 0:B_W] = (dyg * yp * _silu_grad(z, sz)).astype(bf16)
        dd_ref[...] += jnp.sum(dy * xs, axis=0, keepdims=True)
        g1 = dy * ecs
        lane = lax.broadcasted_iota(jnp.int32, (CHUNK, 128), 1)
        rowi = lax.broadcasted_iota(jnp.int32, (128, B_STATE), 0)
        rowc = lax.broadcasted_iota(jnp.int32, (CHUNK, 128), 0)
        cb_mat = [_nt(q["cmb"][k], q["bmb"][k]) for k in range(2)]
        d_cb = [jnp.zeros((CHUNK, CHUNK), f32) for _ in range(2)]
        d_b = [jnp.zeros((CHUNK, B_STATE), f32) for _ in range(2)]
        d_c = [jnp.zeros((CHUNK, B_STATE), f32) for _ in range(2)]
        for p in range(N_PAIR):
            cols = slice(128 * p, 128 * (p + 1))
            g_lo, g_hi = _pair_groups(p)
            lo, hi = lane < HEAD, lane >= HEAD
            cs_p, xdt_p, dy_p, ds_p, g1_p = q["cs"][:, cols], xdt[:, cols], dy[:, cols], dsd[:, cols], g1[:, cols]
            s_p = sp_ref[0, cols, :]
            s_pb = s_p.astype(bf16)
            dsn = ds_ref[cols, :]
            dsnb = dsn.astype(bf16)
            g1b = g1_p.astype(bf16)
            off_lo = _nt(q["cmb"][g_lo], s_pb)
            off = off_lo if g_lo == g_hi else jnp.where(lo, off_lo, _nt(q["cmb"][g_hi], s_pb))
            dcs_p = dy_p * off * ecs[:, cols]
            dsp_lo = _tn(g1b, q["cmb"][g_lo])
            dsp = dsp_lo if g_lo == g_hi else jnp.where(rowi < HEAD, dsp_lo, _tn(g1b, q["cmb"][g_hi]))
            dx_lo = _nt(q["bmb"][g_lo], dsnb)
            dxd = dx_lo if g_lo == g_hi else jnp.where(lo, dx_lo, _nt(q["bmb"][g_hi], dsnb))
            xd_p = xdt_p * ds_p
            if g_lo == g_hi:
                d_c[g_lo] = d_c[g_lo] + _nn(g1b, s_pb)
                d_b[g_lo] = d_b[g_lo] + _nn(xd_p.astype(bf16), dsnb)
            else:
                d_c[g_lo] = d_c[g_lo] + _nn(jnp.where(lo, g1_p, 0.0).astype(bf16), s_pb)
                d_c[g_hi] = d_c[g_hi] + _nn(jnp.where(hi, g1_p, 0.0).astype(bf16), s_pb)
                d_b[g_lo] = d_b[g_lo] + _nn(jnp.where(lo, xd_p, 0.0).astype(bf16), dsnb)
                d_b[g_hi] = d_b[g_hi] + _nn(jnp.where(hi, xd_p, 0.0).astype(bf16), dsnb)
            dxdt_p = dxd * ds_p
            t2 = dxd * xdt_p * ds_p
            dcs_p = dcs_p - t2
            dlast = jnp.sum(t2, axis=0, keepdims=True)
            cst = None
            for half, grp in ((0, g_lo), (1, g_hi)):
                sel = lo if half == 0 else hi
                lm, cst = _ssd_decay(cs_p, half)
                m = cb_mat[grp] * lm
                dyh = jnp.where(sel, dy_p, 0.0).astype(bf16)
                xdh = jnp.where(sel, xdt_p, 0.0).astype(bf16)
                dm = _nt(dyh, xdh)
                pm = dm * m
                col = jnp.sum(pm, axis=1, keepdims=True) - jnp.sum(pm.T, axis=1, keepdims=True)
                dcs_p = dcs_p + jnp.where(lane == HEAD * half, col, 0.0)
                d_cb[grp] = d_cb[grp] + dm * lm
                dxdt_p = dxdt_p + _tn(m.astype(bf16), dyh)
            cdcol = jnp.exp(jnp.broadcast_to(cst[:, CHUNK - 1:CHUNK], (128, B_STATE)))
            ds_ref[cols, :] = cdcol * dsn + dsp
            dcd_row = jnp.sum((dsn * s_p).T, axis=0, keepdims=True)
            dlast = dlast + dcd_row * ecs[CHUNK - 1:CHUNK, cols]
            dcs_ref[:, cols] = dcs_p + jnp.where(rowc == CHUNK - 1, dlast, 0.0)
            dxdt_ref[:, cols] = dxdt_p
        for k in range(2):
            dcbb = d_cb[k].astype(bf16)
            d_c[k] = d_c[k] + _nn(dcbb, q["bmb"][k])
            d_b[k] = d_b[k] + _tn(dcbb, q["cmb"][k])
            dxbc_ref[:, B_W + k * B_STATE:B_W + (k + 1) * B_STATE] = d_b[k]
            dxbc_ref[:, B_W + (2 + k) * B_STATE:B_W + (3 + k) * B_STATE] = d_c[k]
        dxdt = dxdt_ref[...]
        dxbc_ref[:, 0:B_W] = dy * d_ref[...] + dxdt * dt
        dcs = dcs_ref[...]
        dad = jnp.sum(dcs, axis=0, keepdims=True) - _cumsum_rows(dcs) + dcs
        ddt = dxdt * xs + dad * a
        dalog_ref[...] += jnp.sum(dad * dt, axis=0, keepdims=True) * a
        dtraw = ddt * _sigmoid(q["dtin"])
        ddtb_ref[...] += jnp.sum(dtraw, axis=0, keepdims=True)
        dpb_ref[:, B_W + XBC_W:PB_W] = dtraw.astype(bf16)
        dpre = dxbc_ref[...] * _silu_grad(q["pre"], q["sg"])
        dx, dw, db = _conv_bwd(buf_ref, dbuf_ref, dpre, dnext_ref[...], cw, CHUNK)
        dnext_ref[...] = dpre[0:8, :]
        dcw_ref[...] += dw
        dcb_ref[...] += db
        dpb_ref[:, B_W:B_W + XBC_W] = dx.astype(bf16)

    vec = pl.BlockSpec((1, B_W), lambda i: (0, 0))
    cwspec = pl.BlockSpec((4, XBC_W), lambda i: (0, 0))
    cbspec = pl.BlockSpec((1, XBC_W), lambda i: (0, 0))

    def rev(w):
        return pl.BlockSpec((CHUNK, w), lambda i: (nc - 1 - i, 0))

    vshape = jax.ShapeDtypeStruct((1, B_W), f32)
    return pl.pallas_call(
        body, name="ssd_bwd", grid=(nc,),
        in_specs=[rev(PB_W), pl.BlockSpec((8, PB_W), lambda i: (jnp.maximum((nc - 1 - i) * (CHUNK // 8) - 1, 0), 0)),
                  rev(B_W), pl.BlockSpec((1, B_W, B_STATE), lambda i: (nc - 1 - i, 0, 0)), rev(B_W),
                  *[_lspec(a, l) for a in (conv_w, conv_b, dtb, alog, dskip, norm_g)]],
        out_specs=[rev(PB_W), cwspec, cbspec, vec, vec, vec, vec],
        out_shape=[jax.ShapeDtypeStruct((t, PB_W), bf16), jax.ShapeDtypeStruct((4, XBC_W), f32),
                   jax.ShapeDtypeStruct((1, XBC_W), f32), vshape, vshape, vshape, vshape],
        scratch_shapes=[pltpu.VMEM((8 + CHUNK, XBC_W), f32), pltpu.VMEM((CHUNK + 8, XBC_W), f32),
                        pltpu.VMEM((B_W, B_STATE), f32), pltpu.VMEM((8, XBC_W), f32),
                        pltpu.VMEM((CHUNK, XBC_W), f32), pltpu.VMEM((CHUNK, B_W), f32), pltpu.VMEM((CHUNK, B_W), f32)],
        compiler_params=_params(("arbitrary",)),
    )(pb, pb, yp, sprev, dyb, conv_w, conv_b, dtb, alog, dskip, norm_g)


def _loss_fwd(y, target):
    t, d = y.shape
    tm = _tile(t, 512)

    def body(y_ref, t_ref, dy_ref, loss_ref):
        @pl.when(pl.program_id(0) == 0)
        def _():
            loss_ref[...] = jnp.zeros_like(loss_ref)

        e = y_ref[...] - t_ref[...]
        dy_ref[...] = e * (1.0 / d)
        per_tok = jnp.mean(e * e, axis=-1, keepdims=True)
        loss_ref[...] += 0.5 * jnp.sum(per_tok, axis=0, keepdims=True)

    row = pl.BlockSpec((tm, d), lambda i: (i, 0))
    return pl.pallas_call(
        body, name="loss_fwd", grid=(t // tm,), in_specs=[row, row],
        out_specs=[row, pl.BlockSpec((1, 128), lambda i: (0, 0))],
        out_shape=[jax.ShapeDtypeStruct((t, d), f32), jax.ShapeDtypeStruct((1, 128), f32)],
        compiler_params=_params(("arbitrary",)),
    )(y, target)


def _row_tile(r):
    return 512 if r % 512 == 0 else r


def _pair_add(g, r, c_dev):
    _, nl, rows, cols = g.shape
    tr = _row_tile(rows)

    def body(c_ref, g_ref, r_ref, o_ref):
        o_ref[...] = (g_ref[...].astype(f32) + r_ref[...].astype(f32)).astype(bf16)

    blk = (None, None, tr, cols)
    return pl.pallas_call(
        body, name="pair_add",
        grid_spec=pltpu.PrefetchScalarGridSpec(
            num_scalar_prefetch=1, grid=(4, nl, rows // tr),
            in_specs=[pl.BlockSpec(blk, lambda b, l, i, c: (2 * b + c[0], l, i, 0)),
                      pl.BlockSpec(blk, lambda b, l, i, c: (b, l, i, 0))],
            out_specs=pl.BlockSpec(blk, lambda b, l, i, c: (b, l, i, 0))),
        out_shape=jax.ShapeDtypeStruct(r.shape, bf16),
        compiler_params=_params(("arbitrary", "arbitrary", "arbitrary")),
    )(c_dev, g, r)


def _grad_sum(s, q, b_dev):
    _, nl, rows, cols = s.shape
    tr = _row_tile(rows)

    def body(b_ref, s_ref, q0_ref, q1_ref, q2_ref, o_ref):
        o_ref[...] = ((s_ref[...].astype(f32) + q0_ref[...].astype(f32)) + q1_ref[...].astype(f32)) + q2_ref[...].astype(f32)

    blk = (None, None, tr, cols)

    def qspec(k):
        return pl.BlockSpec(blk, lambda l, i, b: (k, l, i, 0))

    return pl.pallas_call(
        body, name="grad_sum",
        grid_spec=pltpu.PrefetchScalarGridSpec(
            num_scalar_prefetch=1, grid=(nl, rows // tr),
            in_specs=[pl.BlockSpec(blk, lambda l, i, b: (b[0], l, i, 0)), qspec(0), qspec(1), qspec(2)],
            out_specs=pl.BlockSpec((None, tr, cols), lambda l, i, b: (l, i, 0))),
        out_shape=jax.ShapeDtypeStruct(s.shape[1:], f32),
        compiler_params=_params(("arbitrary", "arbitrary")),
    )(b_dev, s, q, q, q)


def _sum_devices(parts):
    n, rows, cols = parts.shape
    tr = _row_tile(rows)

    def body(p_ref, o_ref):
        acc = p_ref[0]
        for k in range(1, n):
            acc = acc + p_ref[k]
        o_ref[...] = acc

    return pl.pallas_call(
        body, name="sum_devices", grid=(rows // tr,),
        in_specs=[pl.BlockSpec((n, tr, cols), lambda i: (0, i, 0))],
        out_specs=pl.BlockSpec((tr, cols), lambda i: (i, 0)),
        out_shape=jax.ShapeDtypeStruct((rows, cols), f32),
        compiler_params=_params(("arbitrary",)),
    )(parts)


def _adamw(w, m, v, g):
    nl, rows, cols = w.shape
    tr = _row_tile(rows)
    tc = 128 if (tr == rows and rows * cols * 4 > ADAMW_BLOCK_BYTES and cols % 128 == 0) else cols

    def body(w_ref, m_ref, v_ref, g_ref, d_ref, nm_ref, nv_ref):
        d_ref[...], nm_ref[...], nv_ref[...] = _adamw_math(w_ref[...], m_ref[...], v_ref[...], g_ref[...])

    blk = pl.BlockSpec((None, tr, tc), lambda l, i, c: (l, i, c))
    shape = jax.ShapeDtypeStruct(w.shape, f32)
    return pl.pallas_call(
        body, name="adamw", grid=(nl, rows // tr, cols // tc), in_specs=[blk] * 4, out_specs=[blk] * 3,
        out_shape=[shape] * 3, compiler_params=_params(("arbitrary", "arbitrary", "arbitrary")),
    )(w, m, v, g)


def _adamw_math(w, m, v, g):
    nm = ADAM_B1 * m + (1.0 - ADAM_B1) * g
    nv = ADAM_B2 * v + (1.0 - ADAM_B2) * (g * g)
    m_hat = nm / (1.0 - ADAM_B1 ** ADAM_STEP)
    v_hat = nv / (1.0 - ADAM_B2 ** ADAM_STEP)
    return -ADAM_LR * (m_hat / (jnp.sqrt(v_hat) + ADAM_EPS) + ADAM_WD * w), nm, nv


def _adamw_layer(w, m, v, s, q, b_dev, outs, l, deps=()):
    _, rows, cols = w.shape
    tr = _row_tile(rows)

    def body(b_ref, w_ref, m_ref, v_ref, s_ref, q0_ref, q1_ref, q2_ref, o0, o1, o2, o3, g_ref, d_ref, nm_ref, nv_ref):
        g = ((s_ref[...].astype(f32) + q0_ref[...].astype(f32)) + q1_ref[...].astype(f32)) + q2_ref[...].astype(f32)
        g_ref[...] = g
        d_ref[...], nm_ref[...], nv_ref[...] = _adamw_math(w_ref[...], m_ref[...], v_ref[...], g)

    wspec = pl.BlockSpec((None, tr, cols), lambda i, b: (l, i, 0))
    blk = (None, None, tr, cols)

    def qspec(k):
        return pl.BlockSpec(blk, lambda i, b: (k, 0, i, 0))

    shape = jax.ShapeDtypeStruct(w.shape, f32)
    return pl.pallas_call(
        _after(body, 12, deps), name="adamw_layer",
        grid_spec=pltpu.PrefetchScalarGridSpec(
            num_scalar_prefetch=1, grid=(rows // tr,),
            in_specs=[wspec] * 3 + [pl.BlockSpec(blk, lambda i, b: (b[0], 0, i, 0)), qspec(0), qspec(1), qspec(2)]
            + [ANY] * (4 + len(deps)),
            out_specs=[wspec] * 4),
        out_shape=[shape] * 4, input_output_aliases={8 + k: k for k in range(4)},
        compiler_params=_params(("arbitrary",)),
    )(b_dev, w, m, v, s, q, q, q, *outs, *deps)


def _place():
    return lax.axis_index("x"), lax.axis_index("y"), lax.axis_index("c")


def _all_gather(shards, deps=()):
    n = len(shards)
    nd = len(deps)

    def body(*refs):
        src, dst = refs[:n], refs[n:2 * n]
        send_sems, recv_sems, local_sems = refs[2 * n:]
        x, y, c = _place()
        me, sibling = (x, y, c), (x, y, 1 - c)
        chips = [(1 - x, y), (x, 1 - y), (1 - x, 1 - y)]

        def copy(a, k, block, to, from_shard=False):
            px, py, pc = block
            rows = dst[a].at[4 * px + 2 * py + pc]
            return pltpu.make_async_remote_copy(
                src_ref=src[a] if from_shard else rows, dst_ref=rows,
                send_sem=send_sems.at[a, k], recv_sem=recv_sems.at[a, k], device_id=to, device_id_type=MESH)

        mine = [pltpu.make_async_copy(src[a], dst[a].at[4 * x + 2 * y + c], local_sems.at[a]) for a in range(n)]
        for cp in mine:
            cp.start()
        first = []
        for a in range(n):
            first.append(copy(a, 0, me, sibling, True))
            first += [copy(a, 1 + j, me, (*chip, c), True) for j, chip in enumerate(chips)]
        for cp in first:
            cp.start()
        passed = []
        for j, chip in enumerate(chips):
            for a in range(n):
                copy(a, 1 + j, (*chip, c), me).wait_recv()
                fwd = copy(a, 4 + j, (*chip, c), sibling)
                fwd.start()
                passed.append(fwd)
        for a in range(n):
            copy(a, 0, sibling, me).wait_recv()
            for j, chip in enumerate(chips):
                copy(a, 4 + j, (*chip, 1 - c), me).wait_recv()
        for cp in first + passed:
            cp.wait_send()
        for cp in mine:
            cp.wait()

    return pl.pallas_call(
        _after(body, n, deps), name="all_gather", in_specs=[ANY] * (n + nd), out_specs=[ANY] * n,
        out_shape=[jax.ShapeDtypeStruct((N_DEV,) + s.shape, s.dtype) for s in shards],
        scratch_shapes=[pltpu.SemaphoreType.DMA((n, 7)), pltpu.SemaphoreType.DMA((n, 7)), pltpu.SemaphoreType.DMA((n,))],
    )(*shards, *deps)


HBM = pl.BlockSpec(memory_space=pltpu.HBM)
SEM = pl.BlockSpec(memory_space=pltpu.SEMAPHORE)
_EFFECT = pltpu.SideEffectType.DATAFLOW_SIDE_EFFECTING


def _split_start(name, srcs, dsts, sem_shape, plan):
    ns, nb = len(srcs), len(srcs) + len(dsts)

    def body(*refs):
        send_sems, recv_sems = refs[nb], refs[nb + 1]
        for cp in plan(refs[:ns], refs[ns:nb], send_sems, recv_sems):
            cp.start()
        refs[-1][...] = jnp.zeros_like(refs[-1])

    bufs = list(srcs) + list(dsts)
    return pl.pallas_call(
        body, name=name,
        out_shape=(pltpu.SemaphoreType.DMA(sem_shape), pltpu.SemaphoreType.DMA(sem_shape),
                   *[pltpu.HBM(a.shape, a.dtype) for a in bufs], jax.ShapeDtypeStruct((8, 128), f32)),
        in_specs=[HBM] * nb, out_specs=(SEM, SEM, *[HBM] * nb, pl.BlockSpec(memory_space=pltpu.VMEM)),
        input_output_aliases={i: 2 + i for i in range(nb)},
        compiler_params=pltpu.CompilerParams(has_side_effects=_EFFECT),
    )(*[pltpu.with_memory_space_constraint(a, pltpu.HBM) for a in bufs])


def _split_wait(name, started, ns, plan, after):
    send_sems, recv_sems = started[0], started[1]
    bufs = list(started[2:-1])
    nb = len(bufs)
    after = list(after) if isinstance(after, (list, tuple)) else [after]

    def body(*refs):
        for cp in plan(refs[:ns], refs[ns:nb], refs[nb], refs[nb + 1]):
            cp.wait_send()
            cp.wait_recv()

    return pl.pallas_call(
        body, name=name, out_shape=tuple(pltpu.HBM(a.shape, a.dtype) for a in bufs),
        in_specs=[HBM] * nb + [SEM, SEM] + [ANY] * len(after), out_specs=tuple([HBM] * nb),
        input_output_aliases={i: i for i in range(nb)},
        compiler_params=pltpu.CompilerParams(has_side_effects=_EFFECT),
    )(*bufs, send_sems, recv_sems, *after)


def _remote(src, dst, send_sem, recv_sem, to):
    return pltpu.make_async_remote_copy(src_ref=src, dst_ref=dst, send_sem=send_sem, recv_sem=recv_sem,
                                        device_id=to, device_id_type=MESH)


def _gather_plan(src, dst, send_sems, recv_sems):
    x, y, c = _place()
    peers = [(x, y, 1 - c), (1 - x, y, c), (x, 1 - y, c), (1 - x, 1 - y, c)]
    copies = []
    for a in range(len(dst)):
        rows = dst[a].at[4 * x + 2 * y + c]
        copies += [_remote(rows, rows, send_sems.at[4 * a + k], recv_sems.at[4 * a + k], peer) for k, peer in enumerate(peers)]
    return copies


def _pair_plan(src, dst, send_sems, recv_sems):
    x, y, c = _place()
    return [_remote(src[a].at[2 * b + (1 - c)], dst[a].at[b], send_sems.at[4 * a + b], recv_sems.at[4 * a + b], (x, y, 1 - c))
            for a in range(len(src)) for b in range(4)]


def _chips_plan(src, dst, send_sems, recv_sems):
    x, y, c = _place()
    chips = [(1 - x, y), (x, 1 - y), (1 - x, 1 - y)]
    return [_remote(src[a].at[2 * px + py], dst[a].at[j], send_sems.at[3 * a + j], recv_sems.at[3 * a + j], (px, py, c))
            for a in range(len(src)) for j, (px, py) in enumerate(chips)]


def _forward_plan(src, dst, send_sems, recv_sems):
    x, y, c = _place()
    copies = []
    for a in range(len(dst)):
        for j, (px, py) in enumerate([(1 - x, y), (x, 1 - y), (1 - x, 1 - y)]):
            rows = dst[a].at[4 * px + 2 * py + c]
            copies.append(_remote(rows, rows, send_sems.at[3 * a + j], recv_sems.at[3 * a + j], (x, y, 1 - c)))
    return copies


def _gather_finish(bufs):
    n = len(bufs)

    def body(*refs):
        dst = refs[n:2 * n]
        send_sems, recv_sems = refs[2 * n:]
        x, y, c = _place()
        chips = [(1 - x, y), (x, 1 - y), (1 - x, 1 - y)]
        passed = []
        for a in range(n):
            for j, (px, py) in enumerate(chips):
                rows = dst[a].at[4 * px + 2 * py + c]
                passed.append(_remote(rows, rows, send_sems.at[a, j], recv_sems.at[a, j], (x, y, 1 - c)))
        for cp in passed:
            cp.start()
        for cp in passed:
            cp.wait_send()
        for a in range(n):
            for j, (px, py) in enumerate(chips):
                rows = dst[a].at[4 * px + 2 * py + (1 - c)]
                _remote(rows, rows, send_sems.at[a, j], recv_sems.at[a, j], (x, y, 1 - c)).wait_recv()

    return pl.pallas_call(
        body, name="gather_finish", in_specs=[ANY] * n, out_specs=[ANY] * n,
        out_shape=[jax.ShapeDtypeStruct(b.shape, b.dtype) for b in bufs],
        input_output_aliases={a: a for a in range(n)},
        scratch_shapes=[pltpu.SemaphoreType.DMA((n, 3)), pltpu.SemaphoreType.DMA((n, 3))],
    )(*bufs)


def _place_shards(mats, l, dev):
    n = len(mats)

    def body(dev_ref, *refs):
        for a in range(n):
            refs[n + a][...] = refs[a][...].astype(bf16)

    return pl.pallas_call(
        body, name="place_shards",
        grid_spec=pltpu.PrefetchScalarGridSpec(
            num_scalar_prefetch=1, grid=(1,),
            in_specs=[pl.BlockSpec((None,) + m.shape[1:], lambda i, dv: (l, 0, 0)) for m in mats],
            out_specs=[pl.BlockSpec((None, None) + m.shape[1:], lambda i, dv: (dv[0], 0, 0, 0)) for m in mats]),
        out_shape=[jax.ShapeDtypeStruct((N_DEV, 1) + m.shape[1:], bf16) for m in mats],
        compiler_params=_params(("arbitrary",)),
    )(dev, *mats)


BIG = ("ffn1_w_gu", "ffn1_w_down", "mix_w_in", "mix_w_out", "ffn2_w_gu", "ffn2_w_down")
SHARDED_CONV = ("lru_conv_w", "ssd_conv_w")
REPLICATED = ("ffn1_pre_g", "ffn1_post_g", "mix_pre_g", "mix_post_g", "lru_conv_b", "lru_w_r", "lru_b_r", "lru_w_i",
              "lru_b_i", "lru_lambda", "ssd_conv_b", "ssd_dt_bias", "ssd_a_log", "ssd_d", "ssd_norm_g", "sgu_ln_g",
              "sgu_ln_b", "sgu_w_s", "sgu_b_s", "ffn2_pre_g", "ffn2_post_g")
WEIGHTS = ("ffn1_pre_g", "ffn1_post_g", "ffn1_w_gu", "ffn1_w_down", "mix_pre_g", "mix_post_g", "mix_w_in", "mix_w_out",
           "lru_conv_w", "lru_conv_b", "lru_w_r", "lru_b_r", "lru_w_i", "lru_b_i", "lru_lambda", "ssd_conv_w",
           "ssd_conv_b", "ssd_dt_bias", "ssd_a_log", "ssd_d", "ssd_norm_g", "sgu_ln_g", "sgu_ln_b", "sgu_w_s", "sgu_b_s",
           "ffn2_pre_g", "ffn2_post_g", "ffn2_w_gu", "ffn2_w_down")
DT_LO = PA_W + B_W + XBC_W
N_HEADS = B_W // HEAD
PACK_COLS = 1024


def _size(shape):
    size = 1
    for dim in shape:
        size *= dim
    return size


def _pack_rows(shape):
    return -(-_size(shape) // PACK_COLS)


def _pack(arrays):
    pieces = [jnp.pad(a.reshape(-1), (0, _pack_rows(a.shape) * PACK_COLS - _size(a.shape))) for a in arrays]
    rows = sum(_pack_rows(a.shape) for a in arrays)
    if rows % 8:
        pieces.append(jnp.zeros(((8 - rows % 8) * PACK_COLS,), f32))
    return jnp.concatenate(pieces).reshape(-1, PACK_COLS)


def _unpack(packed, shapes):
    out, row = [], 0
    for s in shapes:
        nr = _pack_rows(s)
        out.append(packed[row:row + nr].reshape(-1)[:_size(s)].reshape(s))
        row += nr
    return out


def _widen_w_in(w):
    return jnp.concatenate([w[..., :DT_LO], jnp.repeat(w[..., DT_LO:DT_LO + N_HEADS], HEAD, axis=-1),
                            w[..., DT_LO + N_HEADS:]], axis=-1)


def _narrow_w_in_grad(g):
    dt = g[..., DT_LO:DT_LO + B_W]
    dt = dt.reshape(dt.shape[:-1] + (N_HEADS, HEAD)).sum(-1)
    return jnp.concatenate([g[..., :DT_LO], dt, g[..., DT_LO + B_W:]], axis=-1)


def _per_head(a):
    return a.reshape(a.shape[:-1] + (N_HEADS, HEAD)).sum(-1)


def kernel(x, ffn1_pre_g, ffn1_post_g, ffn1_w_gu, ffn1_w_down, mix_pre_g, mix_post_g, mix_w_in, mix_w_out, lru_conv_w, lru_conv_b, lru_w_r, lru_b_r, lru_w_i, lru_b_i, lru_lambda, ssd_conv_w, ssd_conv_b, ssd_dt_bias, ssd_a_log, ssd_d, ssd_norm_g, sgu_ln_g, sgu_ln_b, sgu_w_s, sgu_b_s, ffn2_pre_g, ffn2_post_g, ffn2_w_gu, ffn2_w_down, loss_target, m_ffn1_pre_g, m_ffn1_post_g, m_ffn1_w_gu, m_ffn1_w_down, m_mix_pre_g, m_mix_post_g, m_mix_w_in, m_mix_w_out, m_lru_conv_w, m_lru_conv_b, m_lru_w_r, m_lru_b_r, m_lru_w_i, m_lru_b_i, m_lru_lambda, m_ssd_conv_w, m_ssd_conv_b, m_ssd_dt_bias, m_ssd_a_log, m_ssd_d, m_ssd_norm_g, m_sgu_ln_g, m_sgu_ln_b, m_sgu_w_s, m_sgu_b_s, m_ffn2_pre_g, m_ffn2_post_g, m_ffn2_w_gu, m_ffn2_w_down, v_ffn1_pre_g, v_ffn1_post_g, v_ffn1_w_gu, v_ffn1_w_down, v_mix_pre_g, v_mix_post_g, v_mix_w_in, v_mix_w_out, v_lru_conv_w, v_lru_conv_b, v_lru_w_r, v_lru_b_r, v_lru_w_i, v_lru_b_i, v_lru_lambda, v_ssd_conv_w, v_ssd_conv_b, v_ssd_dt_bias, v_ssd_a_log, v_ssd_d, v_ssd_norm_g, v_sgu_ln_g, v_sgu_ln_b, v_sgu_w_s, v_sgu_b_s, v_ffn2_pre_g, v_ffn2_post_g, v_ffn2_w_gu, v_ffn2_w_down):
    given = dict(locals())
    w = {n: given[n] for n in WEIGHTS}
    mom = {n: given["m_" + n] for n in WEIGHTS}
    var = {n: given["v_" + n] for n in WEIGHTS}
    nl = ffn1_pre_g.shape[0]
    _, t, d = x.shape
    xi, yi, ci = _place()
    dev = 4 * xi + 2 * yi + ci
    c_dev = jnp.reshape(ci, (1,)).astype(jnp.int32)
    b_dev = jnp.reshape(2 * xi + yi, (1,)).astype(jnp.int32)

    conv_shapes = [lru_conv_w.shape, ssd_conv_w.shape]
    shards = [ffn1_w_gu, ffn1_w_down, _widen_w_in(mix_w_in), mix_w_out, ffn2_w_gu, ffn2_w_down]
    nbig = len(shards)
    dev_arr = jnp.reshape(dev, (1,)).astype(jnp.int32)
    conv_pack = _pack([lru_conv_w, ssd_conv_w])
    conv_buf = lax.dynamic_update_slice_in_dim(jnp.zeros((N_DEV,) + conv_pack.shape, f32), conv_pack[None], dev, axis=0)
    def gather_groups(l):
        return [(0, 1), (2, 3), (4, 5)] if l == 0 else [tuple(range(nbig))]

    placed, gather_started = {}, {}
    for l in range(nl):
        for gi, idx in enumerate(gather_groups(l)):
            placed[l, gi] = list(_place_shards([shards[i] for i in idx], l, dev_arr)) + (
                [conv_buf] if (l, gi) == (0, 1) else [])

    def start_gather(key):
        gather_started[key] = _split_start(f"gather_start_{key[0]}_{key[1]}", [], placed[key], (4 * len(placed[key]),),
                                           _gather_plan)

    start_gather((0, 0))

    def finish_gather(l, gi, after):
        waited = _split_wait(f"gather_wait_{l}_{gi}", gather_started[l, gi], 0, _gather_plan, after)
        return _gather_finish(list(waited))

    def conv_taps(conv_all):
        full = []
        for k, shape in enumerate(conv_shapes):
            per_dev = jnp.stack([_unpack(conv_all[s], conv_shapes)[k] for s in range(N_DEV)], axis=2)
            full.append(per_dev.reshape(shape[0], shape[1], N_DEV * shape[2]))
        return full

    def vec(a):
        return a.reshape(nl, 1, -1)

    def per_channel(a):
        return jnp.repeat(a, HEAD, axis=-1).reshape(nl, 1, B_W)

    eye = jnp.eye(A_W // HEAD, dtype=f32)

    def block_diag(a):
        return jnp.einsum("lhij,hg->lhigj", a, eye).reshape(nl, A_W, A_W).astype(bf16)

    causal = jnp.tril(jnp.ones((CHUNK, CHUNK), dtype=bool))
    p = dict(
        ffn1_pre=vec(ffn1_pre_g), ffn1_post=vec(ffn1_post_g), mix_pre=vec(mix_pre_g), mix_post=vec(mix_post_g),
        ffn2_pre=vec(ffn2_pre_g), ffn2_post=vec(ffn2_post_g),
        lru=(vec(lru_conv_b), block_diag(lru_w_r), block_diag(lru_w_i), vec(lru_b_r), vec(lru_b_i), vec(lru_lambda)),
        ssd=(vec(ssd_conv_b), per_channel(ssd_dt_bias), per_channel(ssd_a_log), per_channel(ssd_d), vec(ssd_norm_g)),
    )
    wm = jnp.where(causal, sgu_w_s, 0.0).astype(bf16)
    sgu_bias = jnp.repeat(jnp.swapaxes(sgu_b_s, 1, 2), HEAD, axis=2)
    sgu_f = (vec(sgu_ln_g), vec(sgu_ln_b), wm, sgu_bias)
    sgu_b = (vec(sgu_ln_g), vec(sgu_ln_b), wm, jnp.swapaxes(wm, 2, 3), sgu_bias)

    small_names = REPLICATED + SHARDED_CONV
    small_state = [_pack([src[n] for n in small_names])[None] for src in (w, mom, var)]
    prepared = [a for v in p.values() for a in (v if isinstance(v, tuple) else (v,))] + list(sgu_b) + small_state

    xs = x.reshape(t, d)
    saved, gathered, early_forward = [], [], {}
    for l in range(nl):
        x0 = xs
        if l == 0:
            later = [key for key in placed if key != (0, 0)]
            wgu1, wd1 = finish_gather(0, 0, [x0] + prepared + [b for key in later for b in placed[key]])
            for key in later:
                start_gather(key)
            deps = tuple(started[-1] for key, started in gather_started.items() if key != (0, 0))
        elif l in early_forward:
            wgu1, wd1, win, wout, wgu2, wd2 = _split_wait(f"forward_wait_{l}", early_forward[l], 0, _forward_plan, x0)
            deps = ()
        else:
            wgu1, wd1, win, wout, wgu2, wd2 = finish_gather(l, 0, x0)
            deps = ()
        x1, hb1, g1, u1, f1 = _ffn_fwd(x0, p["ffn1_pre"], p["ffn1_post"], wgu1, wd1, l, deps)
        if l == 0:
            win, wout, conv_all = finish_gather(0, 1, x1)
            lru_cw, ssd_cw = conv_taps(conv_all)
            p["lru"], p["ssd"] = (lru_cw,) + p["lru"], (ssd_cw,) + p["ssd"]
        hbm, pa, pb, pc = _mix_in_fwd(x1, p["mix_pre"], win, l)
        ya, h, gates = _lru_fwd(pa, *p["lru"], l)
        yb, yp, sp = _ssd_fwd(pb, *p["ssd"], l)
        yc = _sgu_fwd(pc, *sgu_f, l)
        x2, cat, m = _mix_out_fwd(x1, ya, yb, yc, p["mix_post"], wout, l)
        deps = ()
        if l == 0:
            wgu2, wd2 = finish_gather(0, 2, x2)
        elif l + 1 < nl:
            waited = _split_wait(f"gather_wait_{l + 1}_0", gather_started[l + 1, 0], 0, _gather_plan, x2)
            early_forward[l + 1] = _split_start(f"forward_start_{l + 1}", [], list(waited), (3 * nbig,), _forward_plan)
            deps = (early_forward[l + 1][-1],)
        xs, hb2, g2, u2, f2 = _ffn_fwd(x2, p["ffn2_pre"], p["ffn2_post"], wgu2, wd2, l, deps)
        gathered.append((wgu1, wd1, win, wout, wgu2, wd2))
        saved.append((x0, hb1, g1, u1, f1, x1, hbm, pa, pb, pc, h, gates, yp, sp, cat, m, x2, hb2, g2, u2, f2))
    dy, loss_part = _loss_fwd(xs, loss_target.reshape(t, d))
    loss = lax.psum(loss_part[0, 0], ("x", "y", "c"))

    small = {n: [None] * nl for n in REPLICATED + SHARDED_CONV}
    grads, delta, new_m, new_v = {}, {}, {}, {}
    fused = [n for n in BIG if n != "mix_w_in"]

    def oriented(a, n):
        return jnp.swapaxes(a, 1, 2) if n.endswith("w_gu") else a

    opt_in = {n: tuple(oriented(src[n], n) for src in (w, mom, var)) for n in fused}
    opt_out = {n: tuple(lax.empty(opt_in[n][0].shape, f32) for _ in range(4)) for n in fused}
    w_in_grads = [None] * nl
    grad_shapes = {n: (s.shape[2], s.shape[1]) if n.endswith("w_gu") else s.shape[1:] for n, s in zip(BIG, shards)}

    def start_pair(tag, lp, names, gbuf):
        landing = [lax.empty((4, 1) + grad_shapes[n], bf16) for n in names]
        started = _split_start(f"pair_start_{tag}", [gbuf[n] for n in names], landing, (4 * len(names),), _pair_plan)
        return tag, lp, names, started

    def finish_pair(pending, after):
        tag, lp, names, started = pending
        k = len(names)
        done = _split_wait(f"pair_wait_{tag}", started, k, _pair_plan, after)
        sums = [_pair_add(g, r, c_dev) for g, r in zip(done[:k], done[k:])]
        landing = [lax.empty((3,) + s.shape[1:], bf16) for s in sums]
        return tag, lp, names, _split_start(f"chips_start_{tag}", sums, landing, (3 * k,), _chips_plan)

    def finish_chips(pending, after, deps=()):
        tag, lp, names, started = pending
        k = len(names)
        done = _split_wait(f"chips_wait_{tag}", started, k, _chips_plan, after)
        last = None
        for n, s, q in zip(names, done[:k], done[k:]):
            if n == "mix_w_in":
                w_in_grads[lp] = last = _grad_sum(s, q, b_dev)
            else:
                opt_out[n] = tuple(_adamw_layer(*opt_in[n], s, q, b_dev, opt_out[n], lp, deps))
                last = opt_out[n][0]
        return last

    early = ("ffn2_w_gu", "ffn2_w_down", "mix_w_out")
    late = ("mix_w_in", "ffn1_w_gu", "ffn1_w_down")
    pending_pair = pending_chips = early_pair = early_chips = upper_started = None
    deferred = []
    names = REPLICATED + SHARDED_CONV
    assert nl > 1
    for l in reversed(range(nl)):
        x0, hb1, g1, u1, f1, x1, hbm, pa, pb, pc, h, gates, yp, sp, cat, m, x2, hb2, g2, u2, f2 = saved[l]
        wgu1, wd1, win, wout, wgu2, wd2 = gathered[l][:nbig]
        gbuf ={n: lax.empty((N_DEV, 1) + grad_shapes[n], bf16) for n in BIG}
        deps = () if pending_pair is None else (pending_pair[3][-1],)
        if l == 0:
            deps += (upper_started[-1],)
        dx2, dfb, act, dg, du, dpre, dpost = _ffn_bwd(x2, dy, f2, p["ffn2_pre"], p["ffn2_post"], g2, u2, wgu2, wd2, l, deps)
        small["ffn2_pre_g"][l], small["ffn2_post_g"][l] = dpre[0], dpost[0]
        gbuf["ffn2_w_gu"] = _wgrad_cols(hb2, dg, gbuf["ffn2_w_gu"], 0, 0)
        gbuf["ffn2_w_gu"] = _wgrad_cols(hb2, du, gbuf["ffn2_w_gu"], 0, dg.shape[0])
        gbuf["ffn2_w_down"] = _wgrad_rows(act, dfb, gbuf["ffn2_w_down"], 0)
        deps = ()
        if pending_pair is not None:
            pending_chips = finish_pair(pending_pair, dx2)
            deps = (pending_chips[3][-1],)

        dm, dya, dyb, dyc, dpost = _mix_out_bwd(dx2, m, p["mix_post"], wout, l, deps)
        small["mix_post_g"][l] = dpost[0]
        gbuf["mix_w_out"] = _wgrad_kblocks(cat, [dm], gbuf["mix_w_out"], 0)
        deps = ()
        if l == 0:
            early_pair = start_pair("0a", 0, early, gbuf)
            deps = (early_pair[3][-1],)
        dpc, dws, dbias, dlg, dlb = _sgu_bwd(pc, dyc, *sgu_b, l, deps)
        small["sgu_w_s"][l] = jnp.where(causal, dws, 0.0)
        small["sgu_b_s"][l] = dbias.reshape(CHUNK, C_W // HEAD, HEAD).sum(-1).T
        small["sgu_ln_g"][l], small["sgu_ln_b"][l] = dlg[0], dlb[0]
        dpb, dcw, dcb, ddtb, dalog, ddsk, dng = _ssd_bwd(pb, yp, sp, dyb, *p["ssd"], l)
        small["ssd_conv_w"][l], small["ssd_conv_b"][l], small["ssd_norm_g"][l] = dcw, dcb[0], dng[0]
        small["ssd_dt_bias"][l], small["ssd_a_log"][l], small["ssd_d"][l] = _per_head(ddtb[0]), _per_head(dalog[0]), _per_head(ddsk[0])
        deps = ()
        if l == 0:
            early_chips = finish_pair(early_pair, dpb)
            deps = (early_chips[3][-1],)
        dpa, dcw, dcb, dwr, dwi, dbr, dbi, dlam = _lru_bwd(pa, h, gates, dya, *p["lru"], l, deps)
        small["lru_conv_w"][l], small["lru_conv_b"][l], small["lru_lambda"][l] = dcw, dcb[0], dlam[0]
        small["lru_b_r"][l], small["lru_b_i"][l] = dbr[0], dbi[0]
        heads = range(A_W // HEAD)
        small["lru_w_r"][l] = jnp.stack([dwr[HEAD * i:HEAD * (i + 1), HEAD * i:HEAD * (i + 1)] for i in heads])
        small["lru_w_i"][l] = jnp.stack([dwi[HEAD * i:HEAD * (i + 1), HEAD * i:HEAD * (i + 1)] for i in heads])
        dx1, dpre = _mix_in_bwd(x1, dx2, p["mix_pre"], dpa, dpb, dpc, win, l)
        small["mix_pre_g"][l] = dpre[0]
        gbuf["mix_w_in"] = _wgrad_kblocks(hbm, [dpa, dpb, dpc], gbuf["mix_w_in"], 0)

        dy, dfb, act, dg, du, dpre, dpost = _ffn_bwd(x0, dx1, f1, p["ffn1_pre"], p["ffn1_post"], g1, u1, wgu1, wd1, l)
        small["ffn1_pre_g"][l], small["ffn1_post_g"][l] = dpre[0], dpost[0]
        gbuf["ffn1_w_gu"] = _wgrad_cols(hb1, dg, gbuf["ffn1_w_gu"], 0, 0)
        gbuf["ffn1_w_gu"] = _wgrad_cols(hb1, du, gbuf["ffn1_w_gu"], 0, dg.shape[0])
        gbuf["ffn1_w_down"] = _wgrad_rows(act, dfb, gbuf["ffn1_w_down"], 0)
        if pending_chips is not None:
            deferred.append(pending_chips)
            pending_chips = None
        pending_pair = start_pair(f"{l}", l, late if l == 0 else BIG, gbuf)
        if l == 1:
            upper = [jnp.stack(small[n][1:]) for n in names]
            upper_pack = _pack(upper)
            upper_buf = lax.dynamic_update_slice_in_dim(
                jnp.zeros((N_DEV,) + upper_pack.shape, f32), upper_pack[None], dev, axis=0)
            upper_started = _split_start("small_start", [], [upper_buf], (4,), _gather_plan)
    grad_x = dy.reshape(x.shape)

    lower = [jnp.stack(small[n][:1]) for n in names]
    lower_total = _sum_devices(_all_gather([_pack(lower)], (pending_pair[3][-1],))[0])
    late_chips = finish_pair(pending_pair, lower_total)
    order = lower_total
    for pending in deferred + [early_chips]:
        order = finish_chips(pending, order, (late_chips[3][-1],))
    upper_all = _gather_finish(list(_split_wait("small_wait", upper_started, 0, _gather_plan, order)))[0]
    upper_total = _sum_devices(upper_all)
    finish_chips(late_chips, [upper_total] + [opt_out[n][0] for n in fused] + [g for g in w_in_grads if g is not None])
    full = {n: jnp.concatenate([lo, up], axis=0) for n, lo, up in zip(
        names, _unpack(lower_total, [a.shape for a in lower]), _unpack(upper_total, [a.shape for a in upper]))}

    for n in fused:
        grads[n], delta[n], new_m[n], new_v[n] = (oriented(a, n) for a in opt_out[n])
    grads["mix_w_in"] = _narrow_w_in_grad(jnp.concatenate(w_in_grads, axis=0))
    delta["mix_w_in"], new_m["mix_w_in"], new_v["mix_w_in"] = _adamw(
        w["mix_w_in"], mom["mix_w_in"], var["mix_w_in"], grads["mix_w_in"])
    for n in REPLICATED:
        grads[n] = full[n]
    for n in SHARDED_CONV:
        cols = w[n].shape[2]
        grads[n] = lax.dynamic_slice_in_dim(full[n], dev * cols, cols, axis=2)
    shapes = [w[n].shape for n in names]
    packs = small_state + [_pack([grads[n] for n in names])[None]]
    for dst, packed in zip((delta, new_m, new_v), _adamw(*packs)):
        dst.update(zip(names, _unpack(packed[0], shapes)))

    return (loss, grad_x, *[grads[n] for n in WEIGHTS], *[delta[n] for n in WEIGHTS],
            *[new_m[n] for n in WEIGHTS], *[new_v[n] for n in WEIGHTS])
```

```python
import functools

import jax
import jax.numpy as jnp
from jax import lax
from jax.experimental import pallas as pl
from jax.experimental.pallas import tpu as pltpu

f32, bf16 = jnp.float32, jnp.bfloat16
MESH = pl.DeviceIdType.MESH
ANY = pl.BlockSpec(memory_space=pl.ANY)

N_DEV = 8
NORM_EPS = 1e-6
LRU_C = 8.0
CHUNK = 128
HEAD = 64
A_W, B_W, C_W = 384, 384, 256
B_STATE = 128
XBC_W = B_W + 4 * B_STATE
PA_W, PB_W, PC_W = 2 * A_W, B_W + XBC_W + B_W, 2 * C_W
IN_PAD = PA_W + PB_W + PC_W
ADAM_LR, ADAM_B1, ADAM_B2, ADAM_EPS, ADAM_WD, ADAM_STEP = 0.001, 0.9, 0.999, 1e-08, 0.01, 10
VMEM_LIMIT_BYTES = 56 * 1024 * 1024
FFN_BWD_SPLIT = 2
ADAMW_BLOCK_BYTES = 2 * 1024 * 1024
NEG_BIG = -1e30


def _params(sem=None):
    return pltpu.CompilerParams(dimension_semantics=sem, vmem_limit_bytes=VMEM_LIMIT_BYTES)


def _nn(a, b):
    return jnp.dot(a, b, preferred_element_type=f32)


def _nt(a, b):
    return lax.dot_general(a, b, (((1,), (1,)), ((), ())), preferred_element_type=f32)


def _tn(a, b):
    return lax.dot_general(a, b, (((0,), (0,)), ((), ())), preferred_element_type=f32)


def _sigmoid(x):
    return 0.5 * jnp.tanh(0.5 * x) + 0.5


def _softplus(x):
    return jnp.maximum(x, 0.0) + jnp.log(1.0 + jnp.exp(-jnp.abs(x)))


_GELU_C0, _GELU_C1 = 0.7978845608028654, 0.044715


def _gelu(x):
    t = jnp.tanh(_GELU_C0 * (x + _GELU_C1 * x * x * x))
    return 0.5 * x * (1.0 + t)


def _gelu_grad(x):
    t = jnp.tanh(_GELU_C0 * (x + _GELU_C1 * x * x * x))
    return 0.5 * (1.0 + t) + 0.5 * x * (1.0 - t * t) * _GELU_C0 * (1.0 + 3.0 * _GELU_C1 * x * x)


def _silu_grad(x, s):
    return s * (1.0 + x * (1.0 - s))


def _rms_fwd(x, g):
    r = lax.rsqrt(jnp.mean(x * x, axis=-1, keepdims=True) + NORM_EPS)
    return x * r * g


def _rms_bwd(x, g, dy):
    r = lax.rsqrt(jnp.mean(x * x, axis=-1, keepdims=True) + NORM_EPS)
    xh = x * r
    dxh = dy * g
    dx = r * (dxh - xh * jnp.mean(dxh * xh, axis=-1, keepdims=True))
    return dx, jnp.sum(dy * xh, axis=0, keepdims=True)


def _one_minus_exp(x):
    series = -x * (1.0 + x * (0.5 + x * (1.0 / 6.0 + x * (1.0 / 24.0))))
    return jnp.where(x > -0.01, series, 1.0 - jnp.exp(x))


def _cumsum_rows(x):
    row = lax.broadcasted_iota(jnp.int32, x.shape, 0)
    d = 1
    while d < x.shape[0]:
        x = x + jnp.where(row >= d, pltpu.roll(x, d, 0), 0.0)
        d *= 2
    return x


def _tile(t, cap):
    tm = min(cap, t)
    assert t % tm == 0
    return tm


def _after(body, n_in, deps):
    def wrapped(*refs):
        return body(*refs[:n_in], *refs[n_in + len(deps):])
    return wrapped


def _lspec(a, l):
    return pl.BlockSpec((None,) + a.shape[1:], lambda *_: (l,) + (0,) * (a.ndim - 1))


def _wd_rows(wd_ref):
    return wd_ref[:, 0].reshape(2 * wd_ref.shape[2], wd_ref.shape[3])


def _ffn_fwd(x, pre_g, post_g, wgu, wd, l, deps=()):
    t, d = x.shape
    nb, _, _, h = wgu.shape
    nj = nb // 2
    tm = _tile(t, 512)

    def body(x_ref, pg_ref, qg_ref, wg_ref, wu_ref, wd_ref, y_ref, hb_ref, g_ref, u_ref, f_ref, acc_ref):
        j = pl.program_id(1)

        @pl.when(j == 0)
        def _():
            hb_ref[...] = _rms_fwd(x_ref[...], pg_ref[...]).astype(bf16)
            acc_ref[...] = jnp.zeros_like(acc_ref)

        hb = hb_ref[...]
        g = _nn(hb, wg_ref[0, 0])
        u = _nn(hb, wu_ref[0, 0])
        g_ref[0] = g.astype(bf16)
        u_ref[0] = u.astype(bf16)
        a = (g * _sigmoid(g) * u).astype(bf16)
        acc_ref[...] += _nn(a, _wd_rows(wd_ref))

        @pl.when(j == nj - 1)
        def _():
            f = acc_ref[...]
            f_ref[...] = f
            y_ref[...] = x_ref[...] + 0.5 * _rms_fwd(f, qg_ref[...])

    row = pl.BlockSpec((tm, d), lambda i, j: (i, 0))
    vec = pl.BlockSpec((1, d), lambda i, j: (0, 0))
    act = pl.BlockSpec((1, tm, h), lambda i, j: (j, i, 0))
    return pl.pallas_call(
        _after(body, 6, deps), name="ffn_fwd", grid=(t // tm, nj),
        in_specs=[row, _lspec(pre_g, l), _lspec(post_g, l),
                  pl.BlockSpec((1, 1, d, h), lambda i, j: (j, 0, 0, 0)),
                  pl.BlockSpec((1, 1, d, h), lambda i, j: (j + nj, 0, 0, 0)),
                  pl.BlockSpec((2, 1, h // 2, d), lambda i, j: (j, 0, 0, 0))] + [ANY] * len(deps),
        out_specs=[row, row, act, act, row],
        out_shape=[jax.ShapeDtypeStruct((t, d), f32), jax.ShapeDtypeStruct((t, d), bf16),
                   jax.ShapeDtypeStruct((nj, t, h), bf16), jax.ShapeDtypeStruct((nj, t, h), bf16),
                   jax.ShapeDtypeStruct((t, d), f32)],
        scratch_shapes=[pltpu.VMEM((tm, d), f32)],
        compiler_params=_params(("arbitrary", "arbitrary")),
    )(x, pre_g, post_g, wgu, wgu, wd, *deps)


def _ffn_bwd(x, dy, f, pre_g, post_g, g, u, wgu, wd, l, deps=()):
    t, d = x.shape
    nj, _, h = g.shape
    tm = _tile(t, 512)

    def body(x_ref, dy_ref, f_ref, pg_ref, qg_ref, g_ref, u_ref, wg_ref, wu_ref, wd_ref,
             dx_ref, dfb_ref, a_ref, dg_ref, du_ref, dpg_ref, dqg_ref, dh_ref):
        i, j = pl.program_id(0), pl.program_id(1)

        @pl.when((i == 0) & (j == 0))
        def _():
            dpg_ref[...] = jnp.zeros_like(dpg_ref)
            dqg_ref[...] = jnp.zeros_like(dqg_ref)

        @pl.when(j == 0)
        def _():
            df, dq = _rms_bwd(f_ref[...], qg_ref[...], 0.5 * dy_ref[...])
            dfb_ref[...] = df.astype(bf16)
            dqg_ref[...] += dq
            dh_ref[...] = jnp.zeros_like(dh_ref)

        wdm, wg, wu = _wd_rows(wd_ref), wg_ref[0, 0], wu_ref[0, 0]
        sub = tm // FFN_BWD_SPLIT
        das = [_nt(dfb_ref[pl.ds(half * sub, sub), :], wdm) for half in range(FFN_BWD_SPLIT)]
        for half in range(FFN_BWD_SPLIT):
            rows = pl.ds(half * sub, sub)
            da = das[half]
            gv = g_ref[0, rows, :].astype(f32)
            uv = u_ref[0, rows, :].astype(f32)
            s = _sigmoid(gv)
            sg = gv * s
            a_ref[0, rows, :] = (sg * uv).astype(bf16)
            dg = (da * uv * _silu_grad(gv, s)).astype(bf16)
            du = (da * sg).astype(bf16)
            dg_ref[0, rows, :] = dg
            du_ref[0, rows, :] = du
            dh_ref[rows, :] += _nt(dg, wg) + _nt(du, wu)

        @pl.when(j == nj - 1)
        def _():
            dxn, dp = _rms_bwd(x_ref[...], pg_ref[...], dh_ref[...])
            dx_ref[...] = dy_ref[...] + dxn
            dpg_ref[...] += dp

    row = pl.BlockSpec((tm, d), lambda i, j: (i, 0))
    vec = pl.BlockSpec((1, d), lambda i, j: (0, 0))
    act = pl.BlockSpec((1, tm, h), lambda i, j: (j, i, 0))
    act_shape = jax.ShapeDtypeStruct((nj, t, h), bf16)
    return pl.pallas_call(
        _after(body, 10, deps), name="ffn_bwd", grid=(t // tm, nj),
        in_specs=[row, row, row, _lspec(pre_g, l), _lspec(post_g, l), act, act,
                  pl.BlockSpec((1, 1, d, h), lambda i, j: (j, 0, 0, 0)),
                  pl.BlockSpec((1, 1, d, h), lambda i, j: (j + nj, 0, 0, 0)),
                  pl.BlockSpec((2, 1, h // 2, d), lambda i, j: (j, 0, 0, 0))] + [ANY] * len(deps),
        out_specs=[row, row, act, act, act, vec, vec],
        out_shape=[jax.ShapeDtypeStruct((t, d), f32), jax.ShapeDtypeStruct((t, d), bf16),
                   act_shape, act_shape, act_shape,
                   jax.ShapeDtypeStruct((1, d), f32), jax.ShapeDtypeStruct((1, d), f32)],
        scratch_shapes=[pltpu.VMEM((tm, d), f32)],
        compiler_params=_params(("arbitrary", "arbitrary")),
    )(x, dy, f, pre_g, post_g, g, u, wgu, wgu, wd, *deps)


def _wgrad_cols(x, dy, buf, l, slot0):
    (t, k), (nj, _, n) = x.shape, dy.shape

    def body(x_ref, dy_ref, buf_ref, o_ref):
        o_ref[0, 0] = _tn(dy_ref[0], x_ref[...]).astype(bf16)

    return pl.pallas_call(
        body, name="wgrad_cols", grid=(nj,),
        in_specs=[pl.BlockSpec((t, k), lambda b: (0, 0)), pl.BlockSpec((1, t, n), lambda b: (b, 0, 0)), ANY],
        out_specs=pl.BlockSpec((1, 1, n, k), lambda b: (b + slot0, l, 0, 0)),
        out_shape=jax.ShapeDtypeStruct(buf.shape, bf16), input_output_aliases={2: 0},
        compiler_params=_params(("arbitrary",)),
    )(x, dy, buf)


def _wgrad_rows(x, dy, buf, l):
    (nj, t, k), (_, n) = x.shape, dy.shape

    def body(x_ref, dy_ref, buf_ref, o_ref):
        o_ref[:, 0] = _tn(x_ref[0], dy_ref[...]).astype(bf16).reshape(2, k // 2, n)

    return pl.pallas_call(
        body, name="wgrad_rows", grid=(nj,),
        in_specs=[pl.BlockSpec((1, t, k), lambda b: (b, 0, 0)), pl.BlockSpec((t, n), lambda b: (0, 0)), ANY],
        out_specs=pl.BlockSpec((2, 1, k // 2, n), lambda b: (b, l, 0, 0)),
        out_shape=jax.ShapeDtypeStruct(buf.shape, bf16), input_output_aliases={2: 0},
        compiler_params=_params(("arbitrary",)),
    )(x, dy, buf)


def _wgrad_kblocks(x, dys, buf, l):
    t, k = x.shape
    kb = k // N_DEV
    widths = [dy.shape[1] for dy in dys]
    n = sum(widths)
    nd = len(dys)

    def body(x_ref, *refs):
        dy_hbm, o_ref, dy_vmem = refs[:nd], refs[nd + 1], refs[nd + 2:]

        @pl.when(pl.program_id(0) == 0)
        def _():
            for src, dst in zip(dy_hbm, dy_vmem):
                pltpu.sync_copy(src, dst)

        off = 0
        for dst, w in zip(dy_vmem, widths):
            o_ref[0, 0, :, off:off + w] = _tn(x_ref[...], dst[...]).astype(bf16)
            off += w

    return pl.pallas_call(
        body, name="wgrad_kblocks", grid=(N_DEV,),
        in_specs=[pl.BlockSpec((t, kb), lambda s: (0, s))] + [ANY] * (nd + 1),
        out_specs=pl.BlockSpec((1, 1, kb, n), lambda s: (s, l, 0, 0)),
        out_shape=jax.ShapeDtypeStruct(buf.shape, bf16), input_output_aliases={nd + 1: 0},
        scratch_shapes=[pltpu.VMEM((t, w), bf16) for w in widths],
        compiler_params=_params(("arbitrary",)),
    )(x, *dys, buf)


def _gathered_rows(w_ref, lo, hi):
    return w_ref[:, 0, :, lo:hi].reshape(N_DEV * w_ref.shape[2], hi - lo)


def _gathered_spec(w):
    return pl.BlockSpec((N_DEV, 1) + w.shape[2:], lambda i: (0, 0, 0, 0))


def _mix_in_fwd(x, pre_g, w_in, l):
    t, d = x.shape
    tm = _tile(t, 512)

    def body(x_ref, g_ref, w_ref, hb_ref, pa_ref, pb_ref, pc_ref):
        hb = _rms_fwd(x_ref[...], g_ref[...]).astype(bf16)
        hb_ref[...] = hb
        pa_ref[...] = _nn(hb, _gathered_rows(w_ref, 0, PA_W))
        pb_ref[...] = _nn(hb, _gathered_rows(w_ref, PA_W, PA_W + PB_W))
        pc_ref[...] = _nn(hb, _gathered_rows(w_ref, PA_W + PB_W, IN_PAD))

    def row(w):
        return pl.BlockSpec((tm, w), lambda i: (i, 0))

    return pl.pallas_call(
        body, name="mix_in_fwd", grid=(t // tm,),
        in_specs=[row(d), _lspec(pre_g, l), _gathered_spec(w_in)],
        out_specs=[row(d), row(PA_W), row(PB_W), row(PC_W)],
        out_shape=[jax.ShapeDtypeStruct((t, d), bf16), jax.ShapeDtypeStruct((t, PA_W), f32),
                   jax.ShapeDtypeStruct((t, PB_W), f32), jax.ShapeDtypeStruct((t, PC_W), f32)],
        compiler_params=_params(("arbitrary",)),
    )(x, pre_g, w_in)


def _mix_in_bwd(x, dy, pre_g, dpa, dpb, dpc, w_in, l):
    t, d = x.shape
    tm = _tile(t, 512)

    def body(x_ref, dy_ref, g_ref, dpa_ref, dpb_ref, dpc_ref, w_ref, dx_ref, dg_ref):
        @pl.when(pl.program_id(0) == 0)
        def _():
            dg_ref[...] = jnp.zeros_like(dg_ref)

        wa, wb, wc = (_gathered_rows(w_ref, 0, PA_W), _gathered_rows(w_ref, PA_W, PA_W + PB_W),
                      _gathered_rows(w_ref, PA_W + PB_W, IN_PAD))
        halves = [pl.ds(k * (tm // 2), tm // 2) for k in range(2)]
        dhs = [_nt(dpa_ref[rows, :], wa) + _nt(dpb_ref[rows, :], wb) + _nt(dpc_ref[rows, :], wc) for rows in halves]
        for rows, dh in zip(halves, dhs):
            dxn, dg = _rms_bwd(x_ref[rows, :], g_ref[...], dh)
            dx_ref[rows, :] = dy_ref[rows, :] + dxn
            dg_ref[...] += dg

    def row(w):
        return pl.BlockSpec((tm, w), lambda i: (i, 0))

    vec = pl.BlockSpec((1, d), lambda i: (0, 0))
    return pl.pallas_call(
        body, name="mix_in_bwd", grid=(t // tm,),
        in_specs=[row(d), row(d), _lspec(pre_g, l), row(PA_W), row(PB_W), row(PC_W), _gathered_spec(w_in)],
        out_specs=[row(d), vec],
        out_shape=[jax.ShapeDtypeStruct((t, d), f32), jax.ShapeDtypeStruct((1, d), f32)],
        compiler_params=_params(("arbitrary",)),
    )(x, dy, pre_g, dpa, dpb, dpc, w_in)


def _mix_out_fwd(x, ya, yb, yc, post_g, w_out, l):
    t, d = x.shape
    tm = _tile(t, 512)

    def body(x_ref, ya_ref, yb_ref, yc_ref, g_ref, w_ref, y_ref, cat_ref, m_ref):
        cat_ref[:, 0:A_W] = ya_ref[...]
        cat_ref[:, A_W:A_W + B_W] = yb_ref[...]
        cat_ref[:, A_W + B_W:d] = yc_ref[...]
        m = _nn(cat_ref[...], _gathered_rows(w_ref, 0, d))
        m_ref[...] = m
        y_ref[...] = x_ref[...] + _rms_fwd(m, g_ref[...])

    def row(w):
        return pl.BlockSpec((tm, w), lambda i: (i, 0))

    return pl.pallas_call(
        body, name="mix_out_fwd", grid=(t // tm,),
        in_specs=[row(d), row(A_W), row(B_W), row(C_W), _lspec(post_g, l), _gathered_spec(w_out)],
        out_specs=[row(d), row(d), row(d)],
        out_shape=[jax.ShapeDtypeStruct((t, d), f32), jax.ShapeDtypeStruct((t, d), bf16), jax.ShapeDtypeStruct((t, d), f32)],
        compiler_params=_params(("arbitrary",)),
    )(x, ya, yb, yc, post_g, w_out)


def _mix_out_bwd(dy, m, post_g, w_out, l, deps=()):
    t, d = m.shape
    tm = _tile(t, 512)

    def body(dy_ref, m_ref, g_ref, w_ref, dm_ref, dya_ref, dyb_ref, dyc_ref, dg_ref):
        @pl.when(pl.program_id(0) == 0)
        def _():
            dg_ref[...] = jnp.zeros_like(dg_ref)

        dm, dg = _rms_bwd(m_ref[...], g_ref[...], dy_ref[...])
        dmb = dm.astype(bf16)
        dm_ref[...] = dmb
        dg_ref[...] += dg
        dcat = _nt(dmb, _gathered_rows(w_ref, 0, d))
        dya_ref[...] = dcat[:, 0:A_W]
        dyb_ref[...] = dcat[:, A_W:A_W + B_W]
        dyc_ref[...] = dcat[:, A_W + B_W:d]

    def row(w):
        return pl.BlockSpec((tm, w), lambda i: (i, 0))

    vec = pl.BlockSpec((1, d), lambda i: (0, 0))
    return pl.pallas_call(
        _after(body, 4, deps), name="mix_out_bwd", grid=(t // tm,),
        in_specs=[row(d), row(d), _lspec(post_g, l), _gathered_spec(w_out)] + [ANY] * len(deps),
        out_specs=[row(d), row(A_W), row(B_W), row(C_W), vec],
        out_shape=[jax.ShapeDtypeStruct((t, d), bf16), jax.ShapeDtypeStruct((t, A_W), f32),
                   jax.ShapeDtypeStruct((t, B_W), f32), jax.ShapeDtypeStruct((t, C_W), f32),
                   jax.ShapeDtypeStruct((1, d), f32)],
        compiler_params=_params(("arbitrary",)),
    )(dy, m, post_g, w_out, *deps)


def _conv_fwd(buf_ref, halo, x, w, b, n):
    buf_ref[0:8, :] = halo
    buf_ref[8:8 + n, :] = x
    out = b + w[3:4, :] * x
    for k in range(3):
        out = out + w[k:k + 1, :] * buf_ref[pl.ds(5 + k, n), :]
    return out


def _conv_bwd(buf_ref, dbuf_ref, dout, dnext, w, n):
    dbuf_ref[0:n, :] = dout
    dbuf_ref[n:n + 8, :] = dnext
    dx = w[3:4, :] * dout
    dws = []
    for k in range(3):
        dx = dx + w[k:k + 1, :] * dbuf_ref[pl.ds(3 - k, n), :]
        dws.append(jnp.sum(dout * buf_ref[pl.ds(5 + k, n), :], axis=0, keepdims=True))
    dws.append(jnp.sum(dout * buf_ref[pl.ds(8, n), :], axis=0, keepdims=True))
    return dx, jnp.concatenate(dws, axis=0), jnp.sum(dout, axis=0, keepdims=True)


def _lru_gates(rec, wr, wi, br, bi, lam):
    rb = rec.astype(bf16)
    r = _sigmoid(_nn(rb, wr) + br)
    ig = _sigmoid(_nn(rb, wi) + bi)
    sp = _softplus(-lam)
    la = -LRU_C * r * sp
    a = jnp.exp(la)
    mult = jnp.sqrt(_one_minus_exp(2.0 * la))
    return rb, r, ig, sp, a, mult


def _scan_rows(a_ref, b_ref, o_ref, carry, n, reverse):
    row = lax.broadcasted_iota(jnp.int32, (8, a_ref.shape[1]), 0)
    nb = n // 8

    def step(k, carry):
        blk = (nb - 1 - k) if reverse else k
        rows = pl.ds(pl.multiple_of(blk * 8, 8), 8)
        a, b = a_ref[rows, :], b_ref[rows, :]
        for d in (1, 2, 4):
            shift = 8 - d if reverse else d
            keep = (row < 8 - d) if reverse else (row >= d)
            b = a * jnp.where(keep, pltpu.roll(b, shift, 0), 0.0) + b
            a = a * jnp.where(keep, pltpu.roll(a, shift, 0), 1.0)
        o = a * carry + b
        o_ref[rows, :] = o
        return o[0:1, :] if reverse else o[7:8, :]

    return lax.fori_loop(0, nb, step, carry, unroll=2)


N_GATES = 5
LRU_SUB = 128


def _lru_fwd(pa, conv_w, conv_b, wr, wi, br, bi, lam, l):
    t = pa.shape[0]
    tc = _tile(t, 512)

    def body(pa_ref, halo_ref, cw_ref, cb_ref, wr_ref, wi_ref, br_ref, bi_ref, lam_ref,
             ya_ref, h_ref, gates_ref, buf_ref, u_ref, carry_ref):
        i = pl.program_id(0)

        @pl.when(i == 0)
        def _():
            carry_ref[...] = jnp.zeros_like(carry_ref)

        halo = jnp.where(i > 0, halo_ref[:, A_W:PA_W], 0.0)
        rec = _conv_fwd(buf_ref, halo, pa_ref[:, A_W:PA_W], cw_ref[...], cb_ref[...], tc)
        _, r, ig, _, a, mult = _lru_gates(rec, wr_ref[...], wi_ref[...], br_ref[...], bi_ref[...], lam_ref[...])
        for k, val in enumerate((rec, r, ig, a, mult)):
            gates_ref[k] = val
        u_ref[...] = mult * (ig * rec)

        carry_ref[...] = _scan_rows(gates_ref.at[3], u_ref, h_ref, carry_ref[...], tc, reverse=False)
        ya_ref[...] = (h_ref[...] * _gelu(pa_ref[:, 0:A_W])).astype(bf16)

    vec = pl.BlockSpec((1, A_W), lambda i: (0, 0))
    mat = pl.BlockSpec((A_W, A_W), lambda i: (0, 0))
    row = pl.BlockSpec((tc, A_W), lambda i: (i, 0))
    return pl.pallas_call(
        body, name="lru_fwd", grid=(t // tc,),
        in_specs=[pl.BlockSpec((tc, PA_W), lambda i: (i, 0)),
                  pl.BlockSpec((8, PA_W), lambda i: (jnp.maximum(i * (tc // 8) - 1, 0), 0)),
                  *[_lspec(a, l) for a in (conv_w, conv_b, wr, wi, br, bi, lam)]],
        out_specs=[row, row, pl.BlockSpec((N_GATES, tc, A_W), lambda i: (0, i, 0))],
        out_shape=[jax.ShapeDtypeStruct((t, A_W), bf16), jax.ShapeDtypeStruct((t, A_W), f32),
                   jax.ShapeDtypeStruct((N_GATES, t, A_W), f32)],
        scratch_shapes=[pltpu.VMEM((8 + tc, A_W), f32), pltpu.VMEM((tc, A_W), f32), pltpu.VMEM((1, A_W), f32)],
        compiler_params=_params(("arbitrary",)),
    )(pa, pa, conv_w, conv_b, wr, wi, br, bi, lam)


def _lru_bwd(pa, h, gates, dya, conv_w, conv_b, wr, wi, br, bi, lam, l, deps=()):
    t = pa.shape[0]
    tc = _tile(t, 512)
    nc = t // tc

    def body(pa_ref, halo_ref, h_ref, hhalo_ref, gates_ref, dya_ref, cw_ref, cb_ref, wr_ref, wi_ref, br_ref, bi_ref,
             lam_ref, dpa_ref, dcw_ref, dcb_ref, dwr_ref, dwi_ref, dbr_ref, dbi_ref, dlam_ref,
             buf_ref, dbuf_ref, hbuf_ref, g_ref, dh_ref, carry_ref, dnext_ref, dhbuf_ref, gg_ref):
        i = pl.program_id(0)
        c = nc - 1 - i

        @pl.when(i == 0)
        def _():
            carry_ref[...] = jnp.zeros_like(carry_ref)
            dnext_ref[...] = jnp.zeros_like(dnext_ref)
            for ref in (dcw_ref, dcb_ref, dwr_ref, dwi_ref, dbr_ref, dbi_ref, dlam_ref):
                ref[...] = jnp.zeros_like(ref)

        halo = jnp.where(c > 0, halo_ref[:, A_W:PA_W], 0.0)
        cw = cw_ref[...]
        buf_ref[0:8, :] = halo
        buf_ref[8:8 + tc, :] = pa_ref[:, A_W:PA_W]
        lam = lam_ref[...]
        sp = _softplus(-lam)
        hbuf_ref[0:8, :] = jnp.where(c > 0, hhalo_ref[...], 0.0)
        hbuf_ref[8:8 + tc, :] = h_ref[...]
        gate = pa_ref[:, 0:A_W]
        dya = dya_ref[...]
        dpa_ref[:, 0:A_W] = (dya * h_ref[...] * _gelu_grad(gate)).astype(bf16)
        gg = dya * _gelu(gate)
        gg_ref[...] = gg
        g_ref[...] = gates_ref[3] * gg
        carry_in = carry_ref[...]
        carry_ref[...] = _scan_rows(gates_ref.at[3], g_ref, dh_ref, carry_in, tc, reverse=True)
        dhbuf_ref[0:tc, :] = dh_ref[...]
        dhbuf_ref[tc:tc + 8, :] = jnp.broadcast_to(carry_in, (8, A_W))
        for sb in range(tc // LRU_SUB):
            lo = sb * LRU_SUB
            rows = pl.ds(lo, LRU_SUB)
            rec, r, ig, a, mult = (gates_ref[k, rows, :] for k in range(N_GATES))
            rb = rec.astype(bf16)
            dh = gg_ref[rows, :] + dhbuf_ref[pl.ds(lo + 1, LRU_SUB), :]
            da = dh * hbuf_ref[pl.ds(lo + 7, LRU_SUB), :]
            dmult = dh * ig * rec
            dig = dh * mult * rec
            dla = da * a - dmult * (a * a) / mult
            dr = dla * (-LRU_C * sp)
            dlam_ref[...] += jnp.sum(dla * (-LRU_C * r), axis=0, keepdims=True) * (-_sigmoid(-lam))
            dpr = (dr * r * (1.0 - r))
            dpi = (dig * ig * (1.0 - ig))
            dprb, dpib = dpr.astype(bf16), dpi.astype(bf16)
            g_ref[rows, :] = dh * mult * ig + _nt(dprb, wr_ref[...]) + _nt(dpib, wi_ref[...])
            dwr_ref[...] += _tn(rb, dprb)
            dwi_ref[...] += _tn(rb, dpib)
            dbr_ref[...] += jnp.sum(dpr, axis=0, keepdims=True)
            dbi_ref[...] += jnp.sum(dpi, axis=0, keepdims=True)
        drec = g_ref[...]
        dx, dw, db = _conv_bwd(buf_ref, dbuf_ref, drec, dnext_ref[...], cw, tc)
        dnext_ref[...] = drec[0:8, :]
        dcw_ref[...] += dw
        dcb_ref[...] += db
        dpa_ref[:, A_W:PA_W] = dx.astype(bf16)

    vec = pl.BlockSpec((1, A_W), lambda i: (0, 0))
    mat = pl.BlockSpec((A_W, A_W), lambda i: (0, 0))
    cwspec = pl.BlockSpec((4, A_W), lambda i: (0, 0))

    def rev(w):
        return pl.BlockSpec((tc, w), lambda i: (nc - 1 - i, 0))

    def halo(w):
        return pl.BlockSpec((8, w), lambda i: (jnp.maximum((nc - 1 - i) * (tc // 8) - 1, 0), 0))

    chunk = pltpu.VMEM((tc, A_W), f32)
    return pl.pallas_call(
        _after(body, 13, deps), name="lru_bwd", grid=(nc,),
        in_specs=[rev(PA_W), halo(PA_W), rev(A_W), halo(A_W),
                  pl.BlockSpec((N_GATES, tc, A_W), lambda i: (0, nc - 1 - i, 0)), rev(A_W),
                  *[_lspec(a, l) for a in (conv_w, conv_b, wr, wi, br, bi, lam)]] + [ANY] * len(deps),
        out_specs=[rev(PA_W), cwspec, vec, mat, mat, vec, vec, vec],
        out_shape=[jax.ShapeDtypeStruct((t, PA_W), bf16), jax.ShapeDtypeStruct((4, A_W), f32),
                   jax.ShapeDtypeStruct((1, A_W), f32), jax.ShapeDtypeStruct((A_W, A_W), f32),
                   jax.ShapeDtypeStruct((A_W, A_W), f32), jax.ShapeDtypeStruct((1, A_W), f32),
                   jax.ShapeDtypeStruct((1, A_W), f32), jax.ShapeDtypeStruct((1, A_W), f32)],
        scratch_shapes=[pltpu.VMEM((8 + tc, A_W), f32), pltpu.VMEM((tc + 8, A_W), f32), pltpu.VMEM((8 + tc, A_W), f32),
                        chunk, chunk, pltpu.VMEM((1, A_W), f32), pltpu.VMEM((8, A_W), f32),
                        pltpu.VMEM((tc + 8, A_W), f32), chunk],
        compiler_params=_params(("arbitrary",)),
    )(pa, pa, h, h, gates, dya, conv_w, conv_b, wr, wi, br, bi, lam, *deps)


def _sgu_norm(v, g, b):
    mu = jnp.mean(v, axis=-1, keepdims=True)
    vc = v - mu
    rstd = lax.rsqrt(jnp.mean(vc * vc, axis=-1, keepdims=True) + NORM_EPS)
    vh = vc * rstd
    return vh, rstd, vh * g + b


def _sgu_mix(w_ref, vb, bias):
    grp = lax.broadcasted_iota(jnp.int32, (CHUNK, C_W), 1) // HEAD
    out = bias
    for gi in range(C_W // HEAD):
        out = out + jnp.where(grp == gi, _nn(w_ref[gi], vb), 0.0)
    return out


def _sgu_fwd(pc, ln_g, ln_b, wm, bias, l):
    t = pc.shape[0]
    tm = _tile(t, 512)

    def body(pc_ref, g_ref, b_ref, w_ref, bias_ref, yc_ref):
        for ci in range(tm // CHUNK):
            rows = pl.ds(ci * CHUNK, CHUNK)
            ge = _gelu(pc_ref[rows, :])
            _, _, vn = _sgu_norm(ge[:, C_W:PC_W], g_ref[...], b_ref[...])
            yc_ref[rows, :] = (ge[:, 0:C_W] * _sgu_mix(w_ref, vn.astype(bf16), bias_ref[...])).astype(bf16)

    vec = pl.BlockSpec((1, C_W), lambda i: (0, 0))
    return pl.pallas_call(
        body, name="sgu_fwd", grid=(t // tm,),
        in_specs=[pl.BlockSpec((tm, PC_W), lambda i: (i, 0)), *[_lspec(a, l) for a in (ln_g, ln_b, wm, bias)]],
        out_specs=pl.BlockSpec((tm, C_W), lambda i: (i, 0)),
        out_shape=jax.ShapeDtypeStruct((t, C_W), bf16),
        compiler_params=_params(("arbitrary",)),
    )(pc, ln_g, ln_b, wm, bias)


def _sgu_bwd(pc, dyc, ln_g, ln_b, wm, wmt, bias, l, deps=()):
    t = pc.shape[0]
    tm = _tile(t, 512)

    def body(pc_ref, dyc_ref, g_ref, b_ref, w_ref, wt_ref, bias_ref, dpc_ref, dw_ref, dbias_ref, dg_ref, db_ref):
        @pl.when(pl.program_id(0) == 0)
        def _():
            for ref in (dw_ref, dbias_ref, dg_ref, db_ref):
                ref[...] = jnp.zeros_like(ref)

        grp = lax.broadcasted_iota(jnp.int32, (CHUNK, C_W), 1) // HEAD
        for ci in range(tm // CHUNK):
            rows = pl.ds(ci * CHUNK, CHUNK)
            x = pc_ref[rows, :]
            ge = _gelu(x)
            gv = g_ref[...]
            vh, rstd, vn = _sgu_norm(ge[:, C_W:PC_W], gv, b_ref[...])
            vb = vn.astype(bf16)
            mixed = _sgu_mix(w_ref, vb, bias_ref[...])
            dyc = dyc_ref[rows, :]
            du = dyc * mixed
            dmix = dyc * ge[:, 0:C_W]
            dmb = dmix.astype(bf16)
            dvn = jnp.zeros((CHUNK, C_W), f32)
            for gi in range(C_W // HEAD):
                dvn = dvn + jnp.where(grp == gi, _nn(wt_ref[gi], dmb), 0.0)
                dw_ref[gi] += _nt(jnp.where(grp == gi, dmix, 0.0).astype(bf16), vb)
            dbias_ref[...] += dmix
            dg_ref[...] += jnp.sum(dvn * vh, axis=0, keepdims=True)
            db_ref[...] += jnp.sum(dvn, axis=0, keepdims=True)
            dvh = dvn * gv
            dv = rstd * (dvh - jnp.mean(dvh, axis=-1, keepdims=True) - vh * jnp.mean(dvh * vh, axis=-1, keepdims=True))
            gg = _gelu_grad(x)
            dpc_ref[rows, 0:C_W] = (du * gg[:, 0:C_W]).astype(bf16)
            dpc_ref[rows, C_W:PC_W] = (dv * gg[:, C_W:PC_W]).astype(bf16)

    vec = pl.BlockSpec((1, C_W), lambda i: (0, 0))
    wspec = pl.BlockSpec((4, CHUNK, CHUNK), lambda i: (0, 0, 0))
    bspec = pl.BlockSpec((CHUNK, C_W), lambda i: (0, 0))
    return pl.pallas_call(
        _after(body, 7, deps), name="sgu_bwd", grid=(t // tm,),
        in_specs=[pl.BlockSpec((tm, PC_W), lambda i: (i, 0)), pl.BlockSpec((tm, C_W), lambda i: (i, 0)),
                  *[_lspec(a, l) for a in (ln_g, ln_b, wm, wmt, bias)]] + [ANY] * len(deps),
        out_specs=[pl.BlockSpec((tm, PC_W), lambda i: (i, 0)), wspec, bspec, vec, vec],
        out_shape=[jax.ShapeDtypeStruct((t, PC_W), bf16), jax.ShapeDtypeStruct((4, CHUNK, CHUNK), f32),
                   jax.ShapeDtypeStruct((CHUNK, C_W), f32), jax.ShapeDtypeStruct((1, C_W), f32),
                   jax.ShapeDtypeStruct((1, C_W), f32)],
        compiler_params=_params(("arbitrary",)),
    )(pc, dyc, ln_g, ln_b, wm, wmt, bias, *deps)


N_PAIR = B_W // 128
HEADS_PER_GROUP = 3


def _pair_groups(p):
    return (2 * p) // HEADS_PER_GROUP, (2 * p + 1) // HEADS_PER_GROUP


def _ssd_chunk(pb_ref, halo, buf_ref, cw, cb, dtb, alog):
    z = pb_ref[:, 0:B_W]
    pre = _conv_fwd(buf_ref, halo, pb_ref[:, B_W:B_W + XBC_W], cw, cb, CHUNK)
    sg = _sigmoid(pre)
    xbc = pre * sg
    xs = xbc[:, 0:B_W]
    bm = [xbc[:, B_W + k * B_STATE:B_W + (k + 1) * B_STATE] for k in range(2)]
    cm = [xbc[:, B_W + (2 + k) * B_STATE:B_W + (3 + k) * B_STATE] for k in range(2)]
    dtin = pb_ref[:, B_W + XBC_W:PB_W] + dtb
    dt = _softplus(dtin)
    a = -jnp.exp(alog)
    cs = _cumsum_rows(dt * a)
    return dict(z=z, pre=pre, sg=sg, xs=xs, bm=bm, cm=cm, dtin=dtin, dt=dt, a=a, cs=cs,
                ecs=jnp.exp(cs), ds=jnp.exp(cs[CHUNK - 1:CHUNK, :] - cs), xdt=xs * dt,
                bmb=[v.astype(bf16) for v in bm], cmb=[v.astype(bf16) for v in cm])


def _ssd_decay(cs_pair, half):
    cst = cs_pair.T
    lane0 = HEAD * half
    csc = jnp.broadcast_to(cs_pair[:, lane0:lane0 + 1], (CHUNK, CHUNK))
    csr = cst[lane0:lane0 + 1, :]
    tri = lax.broadcasted_iota(jnp.int32, (CHUNK, CHUNK), 0) >= lax.broadcasted_iota(jnp.int32, (CHUNK, CHUNK), 1)
    return jnp.exp(jnp.where(tri, csc - csr, NEG_BIG)), cst


def _ssd_fwd(pb, conv_w, conv_b, dtb, alog, dskip, norm_g, l):
    t = pb.shape[0]
    nc = t // CHUNK

    def body(pb_ref, halo_ref, cw_ref, cb_ref, dtb_ref, alog_ref, d_ref, ng_ref, yb_ref, yp_ref, sp_ref, buf_ref, s_ref):
        i = pl.program_id(0)

        @pl.when(i == 0)
        def _():
            s_ref[...] = jnp.zeros_like(s_ref)

        halo = jnp.where(i > 0, halo_ref[:, B_W:B_W + XBC_W], 0.0)
        q = _ssd_chunk(pb_ref, halo, buf_ref, cw_ref[...], cb_ref[...], dtb_ref[...], alog_ref[...])
        sp_ref[0] = s_ref[...]
        lane = lax.broadcasted_iota(jnp.int32, (CHUNK, 128), 1)
        rowi = lax.broadcasted_iota(jnp.int32, (128, B_STATE), 0)
        cb_mat = [_nt(q["cmb"][k], q["bmb"][k]) for k in range(2)]
        xd = q["xdt"] * q["ds"]
        for p in range(N_PAIR):
            cols = slice(128 * p, 128 * (p + 1))
            g_lo, g_hi = _pair_groups(p)
            cs_p, xdt_p = q["cs"][:, cols], q["xdt"][:, cols]
            s_p = s_ref[cols, :]
            s_pb = s_p.astype(bf16)
            y_p = jnp.zeros((CHUNK, 128), f32)
            for half, grp in ((0, g_lo), (1, g_hi)):
                lm, cst = _ssd_decay(cs_p, half)
                mb = (cb_mat[grp] * lm).astype(bf16)
                sel = (lane < HEAD) if half == 0 else (lane >= HEAD)
                y_p = y_p + _nn(mb, jnp.where(sel, xdt_p, 0.0).astype(bf16))
            off_lo = _nt(q["cmb"][g_lo], s_pb)
            off = off_lo if g_lo == g_hi else jnp.where(lane < HEAD, off_lo, _nt(q["cmb"][g_hi], s_pb))
            y_p = y_p + off * q["ecs"][:, cols] + q["xs"][:, cols] * d_ref[:, cols]
            yp_ref[:, cols] = y_p
            xd_pb = xd[:, cols].astype(bf16)
            upd_lo = _tn(xd_pb, q["bmb"][g_lo])
            upd = upd_lo if g_lo == g_hi else jnp.where(rowi < HEAD, upd_lo, _tn(xd_pb, q["bmb"][g_hi]))
            cd = jnp.exp(jnp.broadcast_to(cst[:, CHUNK - 1:CHUNK], (128, B_STATE)))
            s_ref[cols, :] = cd * s_p + upd
        z = q["z"]
        yg = yp_ref[...] * (z * _sigmoid(z))
        yb_ref[...] = _rms_fwd(yg, ng_ref[...]).astype(bf16)

    vec = pl.BlockSpec((1, B_W), lambda i: (0, 0))
    row = pl.BlockSpec((CHUNK, B_W), lambda i: (i, 0))
    return pl.pallas_call(
        body, name="ssd_fwd", grid=(nc,),
        in_specs=[pl.BlockSpec((CHUNK, PB_W), lambda i: (i, 0)),
                  pl.BlockSpec((8, PB_W), lambda i: (jnp.maximum(i * (CHUNK // 8) - 1, 0), 0)),
                  *[_lspec(a, l) for a in (conv_w, conv_b, dtb, alog, dskip, norm_g)]],
        out_specs=[row, row, pl.BlockSpec((1, B_W, B_STATE), lambda i: (i, 0, 0))],
        out_shape=[jax.ShapeDtypeStruct((t, B_W), bf16), jax.ShapeDtypeStruct((t, B_W), f32),
                   jax.ShapeDtypeStruct((nc, B_W, B_STATE), f32)],
        scratch_shapes=[pltpu.VMEM((8 + CHUNK, XBC_W), f32), pltpu.VMEM((B_W, B_STATE), f32)],
        compiler_params=_params(("arbitrary",)),
    )(pb, pb, conv_w, conv_b, dtb, alog, dskip, norm_g)


def _ssd_bwd(pb, yp, sprev, dyb, conv_w, conv_b, dtb, alog, dskip, norm_g, l):
    t = pb.shape[0]
    nc = t // CHUNK

    def body(pb_ref, halo_ref, yp_ref, sp_ref, dyb_ref, cw_ref, cb_ref, dtb_ref, alog_ref, d_ref, ng_ref,
             dpb_ref, dcw_ref, dcb_ref, ddtb_ref, dalog_ref, dd_ref, dng_ref,
             buf_ref, dbuf_ref, ds_ref, dnext_ref, dxbc_ref, dcs_ref, dxdt_ref):
        i = pl.program_id(0)
        c = nc - 1 - i

        @pl.when(i == 0)
        def _():
            ds_ref[...] = jnp.zeros_like(ds_ref)
            dnext_ref[...] = jnp.zeros_like(dnext_ref)
            for ref in (dcw_ref, dcb_ref, ddtb_ref, dalog_ref, dd_ref, dng_ref):
                ref[...] = jnp.zeros_like(ref)

        halo = jnp.where(c > 0, halo_ref[:, B_W:B_W + XBC_W], 0.0)
        cw = cw_ref[...]
        q = _ssd_chunk(pb_ref, halo, buf_ref, cw, cb_ref[...], dtb_ref[...], alog_ref[...])
        z, xs, dt, a, ecs, dsd, xdt =q["z"], q["xs"], q["dt"], q["a"], q["ecs"], q["ds"], q["xdt"]
        sz = _sigmoid(z)
        siluz = z * sz
        yp = yp_ref[...]
        dyg, dng = _rms_bwd(yp * siluz, ng_ref[...], dyb_ref[...])
        dng_ref[...] += dng
        dy = dyg * siluz
        dpb_ref[:, 0:B_W] = (dyg * yp * _silu_grad(z, sz)).astype(bf16)
        dd_ref[...] += jnp.sum(dy * xs, axis=0, keepdims=True)
        g1 = dy * ecs
        lane = lax.broadcasted_iota(jnp.int32, (CHUNK, 128), 1)
        rowi = lax.broadcasted_iota(jnp.int32, (128, B_STATE), 0)
        rowc = lax.broadcasted_iota(jnp.int32, (CHUNK, 128), 0)
        cb_mat = [_nt(q["cmb"][k], q["bmb"][k]) for k in range(2)]
        d_cb = [jnp.zeros((CHUNK, CHUNK), f32) for _ in range(2)]
        d_b = [jnp.zeros((CHUNK, B_STATE), f32) for _ in range(2)]
        d_c = [jnp.zeros((CHUNK, B_STATE), f32) for _ in range(2)]
        for p in range(N_PAIR):
            cols = slice(128 * p, 128 * (p + 1))
            g_lo, g_hi = _pair_groups(p)
            lo, hi = lane < HEAD, lane >= HEAD
            cs_p, xdt_p, dy_p, ds_p, g1_p = q["cs"][:, cols], xdt[:, cols], dy[:, cols], dsd[:, cols], g1[:, cols]
            s_p = sp_ref[0, cols, :]
            s_pb = s_p.astype(bf16)
            dsn = ds_ref[cols, :]
            dsnb = dsn.astype(bf16)
            g1b = g1_p.astype(bf16)
            off_lo = _nt(q["cmb"][g_lo], s_pb)
            off = off_lo if g_lo == g_hi else jnp.where(lo, off_lo, _nt(q["cmb"][g_hi], s_pb))
            dcs_p = dy_p * off * ecs[:, cols]
            dsp_lo = _tn(g1b, q["cmb"][g_lo])
            dsp = dsp_lo if g_lo == g_hi else jnp.where(rowi < HEAD, dsp_lo, _tn(g1b, q["cmb"][g_hi]))
            dx_lo = _nt(q["bmb"][g_lo], dsnb)
            dxd = dx_lo if g_lo == g_hi else jnp.where(lo, dx_lo, _nt(q["bmb"][g_hi], dsnb))
            xd_p = xdt_p * ds_p
            if g_lo == g_hi:
                d_c[g_lo] = d_c[g_lo] + _nn(g1b, s_pb)
                d_b[g_lo] = d_b[g_lo] + _nn(xd_p.astype(bf16), dsnb)
            else:
                d_c[g_lo] = d_c[g_lo] + _nn(jnp.where(lo, g1_p, 0.0).astype(bf16), s_pb)
                d_c[g_hi] = d_c[g_hi] + _nn(jnp.where(hi, g1_p, 0.0).astype(bf16), s_pb)
                d_b[g_lo] = d_b[g_lo] + _nn(jnp.where(lo, xd_p, 0.0).astype(bf16), dsnb)
                d_b[g_hi] = d_b[g_hi] + _nn(jnp.where(hi, xd_p, 0.0).astype(bf16), dsnb)
            dxdt_p = dxd * ds_p
            t2 = dxd * xdt_p * ds_p
            dcs_p = dcs_p - t2
            dlast = jnp.sum(t2, axis=0, keepdims=True)
            cst = None
            for half, grp in ((0, g_lo), (1, g_hi)):
                sel = lo if half == 0 else hi
                lm, cst = _ssd_decay(cs_p, half)
                m = cb_mat[grp] * lm
                dyh = jnp.where(sel, dy_p, 0.0).astype(bf16)
                xdh = jnp.where(sel, xdt_p, 0.0).astype(bf16)
                dm = _nt(dyh, xdh)
                pm = dm * m
                col = jnp.sum(pm, axis=1, keepdims=True) - jnp.sum(pm.T, axis=1, keepdims=True)
                dcs_p = dcs_p + jnp.where(lane == HEAD * half, col, 0.0)
                d_cb[grp] = d_cb[grp] + dm * lm
                dxdt_p = dxdt_p + _tn(m.astype(bf16), dyh)
            cdcol = jnp.exp(jnp.broadcast_to(cst[:, CHUNK - 1:CHUNK], (128, B_STATE)))
            ds_ref[cols, :] = cdcol * dsn + dsp
            dcd_row = jnp.sum((dsn * s_p).T, axis=0, keepdims=True)
            dlast = dlast + dcd_row * ecs[CHUNK - 1:CHUNK, cols]
            dcs_ref[:, cols] = dcs_p + jnp.where(rowc == CHUNK - 1, dlast, 0.0)
            dxdt_ref[:, cols] = dxdt_p
        for k in range(2):
            dcbb = d_cb[k].astype(bf16)
            d_c[k] = d_c[k] + _nn(dcbb, q["bmb"][k])
            d_b[k] = d_b[k] + _tn(dcbb, q["cmb"][k])
            dxbc_ref[:, B_W + k * B_STATE:B_W + (k + 1) * B_STATE] = d_b[k]
            dxbc_ref[:, B_W + (2 + k) * B_STATE:B_W + (3 + k) * B_STATE] = d_c[k]
        dxdt = dxdt_ref[...]
        dxbc_ref[:, 0:B_W] = dy * d_ref[...] + dxdt * dt
        dcs = dcs_ref[...]
        dad = jnp.sum(dcs, axis=0, keepdims=True) - _cumsum_rows(dcs) + dcs
        ddt = dxdt * xs + dad * a
        dalog_ref[...] += jnp.sum(dad * dt, axis=0, keepdims=True) * a
        dtraw = ddt * _sigmoid(q["dtin"])
        ddtb_ref[...] += jnp.sum(dtraw, axis=0, keepdims=True)
        dpb_ref[:, B_W + XBC_W:PB_W] = dtraw.astype(bf16)
        dpre = dxbc_ref[...] * _silu_grad(q["pre"], q["sg"])
        dx, dw, db = _conv_bwd(buf_ref, dbuf_ref, dpre, dnext_ref[...], cw, CHUNK)
        dnext_ref[...] = dpre[0:8, :]
        dcw_ref[...] += dw
        dcb_ref[...] += db
        dpb_ref[:, B_W:B_W + XBC_W] = dx.astype(bf16)

    vec = pl.BlockSpec((1, B_W), lambda i: (0, 0))
    cwspec = pl.BlockSpec((4, XBC_W), lambda i: (0, 0))
    cbspec = pl.BlockSpec((1, XBC_W), lambda i: (0, 0))

    def rev(w):
        return pl.BlockSpec((CHUNK, w), lambda i: (nc - 1 - i, 0))

    vshape = jax.ShapeDtypeStruct((1, B_W), f32)
    return pl.pallas_call(
        body, name="ssd_bwd", grid=(nc,),
        in_specs=[rev(PB_W), pl.BlockSpec((8, PB_W), lambda i: (jnp.maximum((nc - 1 - i) * (CHUNK // 8) - 1, 0), 0)),
                  rev(B_W), pl.BlockSpec((1, B_W, B_STATE), lambda i: (nc - 1 - i, 0, 0)), rev(B_W),
                  *[_lspec(a, l) for a in (conv_w, conv_b, dtb, alog, dskip, norm_g)]],
        out_specs=[rev(PB_W), cwspec, cbspec, vec, vec, vec, vec],
        out_shape=[jax.ShapeDtypeStruct((t, PB_W), bf16), jax.ShapeDtypeStruct((4, XBC_W), f32),
                   jax.ShapeDtypeStruct((1, XBC_W), f32), vshape, vshape, vshape, vshape],
        scratch_shapes=[pltpu.VMEM((8 + CHUNK, XBC_W), f32), pltpu.VMEM((CHUNK + 8, XBC_W), f32),
                        pltpu.VMEM((B_W, B_STATE), f32), pltpu.VMEM((8, XBC_W), f32),
                        pltpu.VMEM((CHUNK, XBC_W), f32), pltpu.VMEM((CHUNK, B_W), f32), pltpu.VMEM((CHUNK, B_W), f32)],
        compiler_params=_params(("arbitrary",)),
    )(pb, pb, yp, sprev, dyb, conv_w, conv_b, dtb, alog, dskip, norm_g)


def _loss_fwd(y, target):
    t, d = y.shape
    tm = _tile(t, 512)

    def body(y_ref, t_ref, dy_ref, loss_ref):
        @pl.when(pl.program_id(0) == 0)
        def _():
            loss_ref[...] = jnp.zeros_like(loss_ref)

        e = y_ref[...] - t_ref[...]
        dy_ref[...] = e * (1.0 / d)
        per_tok = jnp.mean(e * e, axis=-1, keepdims=True)
        loss_ref[...] += 0.5 * jnp.sum(per_tok, axis=0, keepdims=True)

    row = pl.BlockSpec((tm, d), lambda i: (i, 0))
    return pl.pallas_call(
        body, name="loss_fwd", grid=(t // tm,), in_specs=[row, row],
        out_specs=[row, pl.BlockSpec((1, 128), lambda i: (0, 0))],
        out_shape=[jax.ShapeDtypeStruct((t, d), f32), jax.ShapeDtypeStruct((1, 128), f32)],
        compiler_params=_params(("arbitrary",)),
    )(y, target)


def _row_tile(r):
    return 512 if r % 512 == 0 else r


def _pair_add(g, r, idx):
    _, nl, rows, cols = g.shape
    tr = _row_tile(rows)

    def body(idx_ref, g_ref, r_ref, o_ref):
        o_ref[...] = (g_ref[...].astype(f32) + r_ref[...].astype(f32)).astype(bf16)

    blk = (None, None, tr, cols)
    return pl.pallas_call(
        body, name="pair_add",
        grid_spec=pltpu.PrefetchScalarGridSpec(
            num_scalar_prefetch=1, grid=(3, nl, rows // tr),
            in_specs=[pl.BlockSpec(blk, lambda j, l, i, p: (p[j], l, i, 0)),
                      pl.BlockSpec(blk, lambda j, l, i, p: (p[3 + j], l, i, 0))],
            out_specs=pl.BlockSpec(blk, lambda j, l, i, p: (j, l, i, 0))),
        out_shape=jax.ShapeDtypeStruct((3,) + r.shape[1:], bf16),
        compiler_params=_params(("arbitrary", "arbitrary", "arbitrary")),
    )(idx, g, r)


def _own_sum(g_ref, r_ref, q0_ref, q1_ref, q2_ref):
    own = (g_ref[...].astype(f32) + r_ref[...].astype(f32)).astype(bf16).astype(f32)
    return ((own + q0_ref[...].astype(f32)) + q1_ref[...].astype(f32)) + q2_ref[...].astype(f32)


def _grad_sum(g, r, q, own):
    _, nl, rows, cols = g.shape
    tr = _row_tile(rows)

    def body(own_ref, g_ref, r_ref, q0_ref, q1_ref, q2_ref, o_ref):
        o_ref[...] = _own_sum(g_ref, r_ref, q0_ref, q1_ref, q2_ref)

    blk = (None, None, tr, cols)

    def qspec(k):
        return pl.BlockSpec(blk, lambda l, i, b: (k, l, i, 0))

    return pl.pallas_call(
        body, name="grad_sum",
        grid_spec=pltpu.PrefetchScalarGridSpec(
            num_scalar_prefetch=1, grid=(nl, rows // tr),
            in_specs=[pl.BlockSpec(blk, lambda l, i, b: (b[0], l, i, 0)), pl.BlockSpec(blk, lambda l, i, b: (b[1], l, i, 0)),
                      qspec(0), qspec(1), qspec(2)],
            out_specs=pl.BlockSpec((None, tr, cols), lambda l, i, b: (l, i, 0))),
        out_shape=jax.ShapeDtypeStruct(g.shape[1:], f32),
        compiler_params=_params(("arbitrary", "arbitrary")),
    )(own, g, r, q, q, q)


def _sum_devices(parts):
    n, rows, cols = parts.shape
    tr = _row_tile(rows)

    def body(p_ref, o_ref):
        acc = p_ref[0]
        for k in range(1, n):
            acc = acc + p_ref[k]
        o_ref[...] = acc

    return pl.pallas_call(
        body, name="sum_devices", grid=(rows // tr,),
        in_specs=[pl.BlockSpec((n, tr, cols), lambda i: (0, i, 0))],
        out_specs=pl.BlockSpec((tr, cols), lambda i: (i, 0)),
        out_shape=jax.ShapeDtypeStruct((rows, cols), f32),
        compiler_params=_params(("arbitrary",)),
    )(parts)


def _adamw(w, m, v, g):
    nl, rows, cols = w.shape
    tr = _row_tile(rows)
    tc = 128 if (tr == rows and rows * cols * 4 > ADAMW_BLOCK_BYTES and cols % 128 == 0) else cols

    def body(w_ref, m_ref, v_ref, g_ref, d_ref, nm_ref, nv_ref):
        d_ref[...], nm_ref[...], nv_ref[...] = _adamw_math(w_ref[...], m_ref[...], v_ref[...], g_ref[...])

    blk = pl.BlockSpec((None, tr, tc), lambda l, i, c: (l, i, c))
    shape = jax.ShapeDtypeStruct(w.shape, f32)
    return pl.pallas_call(
        body, name="adamw", grid=(nl, rows // tr, cols // tc), in_specs=[blk] * 4, out_specs=[blk] * 3,
        out_shape=[shape] * 3, compiler_params=_params(("arbitrary", "arbitrary", "arbitrary")),
    )(w, m, v, g)


def _adamw_math(w, m, v, g):
    nm = ADAM_B1 * m + (1.0 - ADAM_B1) * g
    nv = ADAM_B2 * v + (1.0 - ADAM_B2) * (g * g)
    m_hat = nm / (1.0 - ADAM_B1 ** ADAM_STEP)
    v_hat = nv / (1.0 - ADAM_B2 ** ADAM_STEP)
    return -ADAM_LR * (m_hat / (jnp.sqrt(v_hat) + ADAM_EPS) + ADAM_WD * w), nm, nv


def _adamw_layer(w, m, v, gl, r, q, own, outs, l, deps=()):
    _, rows, cols = w.shape
    tr = _row_tile(rows)

    def body(own_ref, w_ref, m_ref, v_ref, gl_ref, r_ref, q0_ref, q1_ref, q2_ref, o0, o1, o2, o3,
             g_ref, d_ref, nm_ref, nv_ref):
        g = _own_sum(gl_ref, r_ref, q0_ref, q1_ref, q2_ref)
        g_ref[...] = g
        d_ref[...], nm_ref[...], nv_ref[...] = _adamw_math(w_ref[...], m_ref[...], v_ref[...], g)

    wspec = pl.BlockSpec((None, tr, cols), lambda i, b: (l, i, 0))
    blk = (None, None, tr, cols)

    def qspec(k):
        return pl.BlockSpec(blk, lambda i, b: (k, 0, i, 0))

    shape = jax.ShapeDtypeStruct(w.shape, f32)
    return pl.pallas_call(
        _after(body, 13, deps), name="adamw_layer",
        grid_spec=pltpu.PrefetchScalarGridSpec(
            num_scalar_prefetch=1, grid=(rows // tr,),
            in_specs=[wspec] * 3 + [pl.BlockSpec(blk, lambda i, b: (b[0], 0, i, 0)),
                                    pl.BlockSpec(blk, lambda i, b: (b[1], 0, i, 0)), qspec(0), qspec(1), qspec(2)]
            + [ANY] * (4 + len(deps)),
            out_specs=[wspec] * 4),
        out_shape=[shape] * 4, input_output_aliases={9 + k: k for k in range(4)},
        compiler_params=_params(("arbitrary",)),
    )(own, w, m, v, gl, r, q, q, q, *outs, *deps)


def _place():
    return lax.axis_index("x"), lax.axis_index("y"), lax.axis_index("c")


def _all_gather(shards, deps=()):
    n = len(shards)
    nd = len(deps)

    def body(*refs):
        src, dst = refs[:n], refs[n:2 * n]
        send_sems, recv_sems, local_sems = refs[2 * n:]
        x, y, c = _place()
        me, sibling = (x, y, c), (x, y, 1 - c)
        chips = [(1 - x, y), (x, 1 - y), (1 - x, 1 - y)]

        def copy(a, k, block, to, from_shard=False):
            px, py, pc = block
            rows = dst[a].at[4 * px + 2 * py + pc]
            return pltpu.make_async_remote_copy(
                src_ref=src[a] if from_shard else rows, dst_ref=rows,
                send_sem=send_sems.at[a, k], recv_sem=recv_sems.at[a, k], device_id=to, device_id_type=MESH)

        mine = [pltpu.make_async_copy(src[a], dst[a].at[4 * x + 2 * y + c], local_sems.at[a]) for a in range(n)]
        for cp in mine:
            cp.start()
        first = []
        for a in range(n):
            first.append(copy(a, 0, me, sibling, True))
            first += [copy(a, 1 + j, me, (*chip, c), True) for j, chip in enumerate(chips)]
        for cp in first:
            cp.start()
        passed = []
        for j, chip in enumerate(chips):
            for a in range(n):
                copy(a, 1 + j, (*chip, c), me).wait_recv()
                fwd = copy(a, 4 + j, (*chip, c), sibling)
                fwd.start()
                passed.append(fwd)
        for a in range(n):
            copy(a, 0, sibling, me).wait_recv()
            for j, chip in enumerate(chips):
                copy(a, 4 + j, (*chip, 1 - c), me).wait_recv()
        for cp in first + passed:
            cp.wait_send()
        for cp in mine:
            cp.wait()

    return pl.pallas_call(
        _after(body, n, deps), name="all_gather", in_specs=[ANY] * (n + nd), out_specs=[ANY] * n,
        out_shape=[jax.ShapeDtypeStruct((N_DEV,) + s.shape, s.dtype) for s in shards],
        scratch_shapes=[pltpu.SemaphoreType.DMA((n, 7)), pltpu.SemaphoreType.DMA((n, 7)), pltpu.SemaphoreType.DMA((n,))],
    )(*shards, *deps)


HBM = pl.BlockSpec(memory_space=pltpu.HBM)
SEM = pl.BlockSpec(memory_space=pltpu.SEMAPHORE)
_EFFECT = pltpu.SideEffectType.DATAFLOW_SIDE_EFFECTING


def _split_start(name, srcs, dsts, sem_shape, plan):
    ns, nb = len(srcs), len(srcs) + len(dsts)

    def body(*refs):
        send_sems, recv_sems = refs[nb], refs[nb + 1]
        for cp in plan(refs[:ns], refs[ns:nb], send_sems, recv_sems):
            cp.start()
        refs[-1][...] = jnp.zeros_like(refs[-1])

    bufs = list(srcs) + list(dsts)
    return pl.pallas_call(
        body, name=name,
        out_shape=(pltpu.SemaphoreType.DMA(sem_shape), pltpu.SemaphoreType.DMA(sem_shape),
                   *[pltpu.HBM(a.shape, a.dtype) for a in bufs], jax.ShapeDtypeStruct((8, 128), f32)),
        in_specs=[HBM] * nb, out_specs=(SEM, SEM, *[HBM] * nb, pl.BlockSpec(memory_space=pltpu.VMEM)),
        input_output_aliases={i: 2 + i for i in range(nb)},
        compiler_params=pltpu.CompilerParams(has_side_effects=_EFFECT),
    )(*[pltpu.with_memory_space_constraint(a, pltpu.HBM) for a in bufs])


def _split_wait(name, started, ns, plan, after):
    send_sems, recv_sems = started[0], started[1]
    bufs = list(started[2:-1])
    nb = len(bufs)
    after = list(after) if isinstance(after, (list, tuple)) else [after]

    def body(*refs):
        for cp in plan(refs[:ns], refs[ns:nb], refs[nb], refs[nb + 1]):
            cp.wait_send()
            cp.wait_recv()

    return pl.pallas_call(
        body, name=name, out_shape=tuple(pltpu.HBM(a.shape, a.dtype) for a in bufs),
        in_specs=[HBM] * nb + [SEM, SEM] + [ANY] * len(after), out_specs=tuple([HBM] * nb),
        input_output_aliases={i: i for i in range(nb)},
        compiler_params=pltpu.CompilerParams(has_side_effects=_EFFECT),
    )(*bufs, send_sems, recv_sems, *after)


def _remote(src, dst, send_sem, recv_sem, to):
    return pltpu.make_async_remote_copy(src_ref=src, dst_ref=dst, send_sem=send_sem, recv_sem=recv_sem,
                                        device_id=to, device_id_type=MESH)


def _gather_plan(src, dst, send_sems, recv_sems):
    x, y, c = _place()
    peers = [(x, y, 1 - c), (1 - x, y, c), (x, 1 - y, c), (1 - x, 1 - y, c)]
    copies = []
    for a in range(len(dst)):
        rows = dst[a].at[4 * x + 2 * y + c]
        copies += [_remote(rows, rows, send_sems.at[4 * a + k], recv_sems.at[4 * a + k], peer) for k, peer in enumerate(peers)]
    return copies


def _pair_plan(src, dst, send_sems, recv_sems):
    x, y, c = _place()
    return [_remote(src[a].at[2 * b + (1 - c)], dst[a].at[b], send_sems.at[4 * a + b], recv_sems.at[4 * a + b], (x, y, 1 - c))
            for a in range(len(src)) for b in range(4)]


def _chips_plan(src, dst, send_sems, recv_sems):
    x, y, c = _place()
    chips = [(1 - x, y), (x, 1 - y), (1 - x, 1 - y)]
    return [_remote(src[a].at[j], dst[a].at[j], send_sems.at[3 * a + j], recv_sems.at[3 * a + j], (px, py, c))
            for a in range(len(src)) for j, (px, py) in enumerate(chips)]


def _forward_plan(src, dst, send_sems, recv_sems):
    x, y, c = _place()
    copies = []
    for a in range(len(dst)):
        for j, (px, py) in enumerate([(1 - x, y), (x, 1 - y), (1 - x, 1 - y)]):
            rows = dst[a].at[4 * px + 2 * py + c]
            copies.append(_remote(rows, rows, send_sems.at[3 * a + j], recv_sems.at[3 * a + j], (x, y, 1 - c)))
    return copies


def _gather_finish(bufs):
    n = len(bufs)

    def body(*refs):
        dst = refs[n:2 * n]
        send_sems, recv_sems = refs[2 * n:]
        x, y, c = _place()
        chips = [(1 - x, y), (x, 1 - y), (1 - x, 1 - y)]
        passed = []
        for a in range(n):
            for j, (px, py) in enumerate(chips):
                rows = dst[a].at[4 * px + 2 * py + c]
                passed.append(_remote(rows, rows, send_sems.at[a, j], recv_sems.at[a, j], (x, y, 1 - c)))
        for cp in passed:
            cp.start()
        for cp in passed:
            cp.wait_send()
        for a in range(n):
            for j, (px, py) in enumerate(chips):
                rows = dst[a].at[4 * px + 2 * py + (1 - c)]
                _remote(rows, rows, send_sems.at[a, j], recv_sems.at[a, j], (x, y, 1 - c)).wait_recv()

    return pl.pallas_call(
        body, name="gather_finish", in_specs=[ANY] * n, out_specs=[ANY] * n,
        out_shape=[jax.ShapeDtypeStruct(b.shape, b.dtype) for b in bufs],
        input_output_aliases={a: a for a in range(n)},
        scratch_shapes=[pltpu.SemaphoreType.DMA((n, 3)), pltpu.SemaphoreType.DMA((n, 3))],
    )(*bufs)


def _place_shards(mats, l, dev):
    n = len(mats)

    def body(dev_ref, *refs):
        for a in range(n):
            refs[n + a][...] = refs[a][...].astype(bf16)

    return pl.pallas_call(
        body, name="place_shards",
        grid_spec=pltpu.PrefetchScalarGridSpec(
            num_scalar_prefetch=1, grid=(1,),
            in_specs=[pl.BlockSpec((None,) + m.shape[1:], lambda i, dv: (l, 0, 0)) for m in mats],
            out_specs=[pl.BlockSpec((None, None) + m.shape[1:], lambda i, dv: (dv[0], 0, 0, 0)) for m in mats]),
        out_shape=[jax.ShapeDtypeStruct((N_DEV, 1) + m.shape[1:], bf16) for m in mats],
        compiler_params=_params(("arbitrary",)),
    )(dev, *mats)


BIG = ("ffn1_w_gu", "ffn1_w_down", "mix_w_in", "mix_w_out", "ffn2_w_gu", "ffn2_w_down")
SHARDED_CONV = ("lru_conv_w", "ssd_conv_w")
REPLICATED = ("ffn1_pre_g", "ffn1_post_g", "mix_pre_g", "mix_post_g", "lru_conv_b", "lru_w_r", "lru_b_r", "lru_w_i",
              "lru_b_i", "lru_lambda", "ssd_conv_b", "ssd_dt_bias", "ssd_a_log", "ssd_d", "ssd_norm_g", "sgu_ln_g",
              "sgu_ln_b", "sgu_w_s", "sgu_b_s", "ffn2_pre_g", "ffn2_post_g")
WEIGHTS = ("ffn1_pre_g", "ffn1_post_g", "ffn1_w_gu", "ffn1_w_down", "mix_pre_g", "mix_post_g", "mix_w_in", "mix_w_out",
           "lru_conv_w", "lru_conv_b", "lru_w_r", "lru_b_r", "lru_w_i", "lru_b_i", "lru_lambda", "ssd_conv_w",
           "ssd_conv_b", "ssd_dt_bias", "ssd_a_log", "ssd_d", "ssd_norm_g", "sgu_ln_g", "sgu_ln_b", "sgu_w_s", "sgu_b_s",
           "ffn2_pre_g", "ffn2_post_g", "ffn2_w_gu", "ffn2_w_down")
DT_LO = PA_W + B_W + XBC_W
N_HEADS = B_W // HEAD
PACK_COLS = 1024


def _size(shape):
    size = 1
    for dim in shape:
        size *= dim
    return size


def _pack_rows(shape):
    return -(-_size(shape) // PACK_COLS)


def _pack(arrays):
    pieces = [jnp.pad(a.reshape(-1), (0, _pack_rows(a.shape) * PACK_COLS - _size(a.shape))) for a in arrays]
    rows = sum(_pack_rows(a.shape) for a in arrays)
    if rows % 8:
        pieces.append(jnp.zeros(((8 - rows % 8) * PACK_COLS,), f32))
    return jnp.concatenate(pieces).reshape(-1, PACK_COLS)


def _unpack(packed, shapes):
    out, row = [], 0
    for s in shapes:
        nr = _pack_rows(s)
        out.append(packed[row:row + nr].reshape(-1)[:_size(s)].reshape(s))
        row += nr
    return out


def _widen_w_in(w):
    return jnp.concatenate([w[..., :DT_LO], jnp.repeat(w[..., DT_LO:DT_LO + N_HEADS], HEAD, axis=-1),
                            w[..., DT_LO + N_HEADS:]], axis=-1)


def _narrow_w_in_grad(g):
    dt = g[..., DT_LO:DT_LO + B_W]
    dt = dt.reshape(dt.shape[:-1] + (N_HEADS, HEAD)).sum(-1)
    return jnp.concatenate([g[..., :DT_LO], dt, g[..., DT_LO + B_W:]], axis=-1)


def _per_head(a):
    return a.reshape(a.shape[:-1] + (N_HEADS, HEAD)).sum(-1)


def kernel(x, ffn1_pre_g, ffn1_post_g, ffn1_w_gu, ffn1_w_down, mix_pre_g, mix_post_g, mix_w_in, mix_w_out, lru_conv_w, lru_conv_b, lru_w_r, lru_b_r, lru_w_i, lru_b_i, lru_lambda, ssd_conv_w, ssd_conv_b, ssd_dt_bias, ssd_a_log, ssd_d, ssd_norm_g, sgu_ln_g, sgu_ln_b, sgu_w_s, sgu_b_s, ffn2_pre_g, ffn2_post_g, ffn2_w_gu, ffn2_w_down, loss_target, m_ffn1_pre_g, m_ffn1_post_g, m_ffn1_w_gu, m_ffn1_w_down, m_mix_pre_g, m_mix_post_g, m_mix_w_in, m_mix_w_out, m_lru_conv_w, m_lru_conv_b, m_lru_w_r, m_lru_b_r, m_lru_w_i, m_lru_b_i, m_lru_lambda, m_ssd_conv_w, m_ssd_conv_b, m_ssd_dt_bias, m_ssd_a_log, m_ssd_d, m_ssd_norm_g, m_sgu_ln_g, m_sgu_ln_b, m_sgu_w_s, m_sgu_b_s, m_ffn2_pre_g, m_ffn2_post_g, m_ffn2_w_gu, m_ffn2_w_down, v_ffn1_pre_g, v_ffn1_post_g, v_ffn1_w_gu, v_ffn1_w_down, v_mix_pre_g, v_mix_post_g, v_mix_w_in, v_mix_w_out, v_lru_conv_w, v_lru_conv_b, v_lru_w_r, v_lru_b_r, v_lru_w_i, v_lru_b_i, v_lru_lambda, v_ssd_conv_w, v_ssd_conv_b, v_ssd_dt_bias, v_ssd_a_log, v_ssd_d, v_ssd_norm_g, v_sgu_ln_g, v_sgu_ln_b, v_sgu_w_s, v_sgu_b_s, v_ffn2_pre_g, v_ffn2_post_g, v_ffn2_w_gu, v_ffn2_w_down):
    given = dict(locals())
    w = {n: given[n] for n in WEIGHTS}
    mom = {n: given["m_" + n] for n in WEIGHTS}
    var = {n: given["v_" + n] for n in WEIGHTS}
    nl = ffn1_pre_g.shape[0]
    _, t, d = x.shape
    xi, yi, ci = _place()
    dev = 4 * xi + 2 * yi + ci
    chip_slots = [2 * (1 - xi) + yi, 2 * xi + (1 - yi), 2 * (1 - xi) + (1 - yi)]
    pair_idx = jnp.stack([2 * b + ci for b in chip_slots] + chip_slots).astype(jnp.int32)
    own_idx = jnp.stack([2 * (2 * xi + yi) + ci, 2 * xi + yi]).astype(jnp.int32)

    conv_shapes = [lru_conv_w.shape, ssd_conv_w.shape]
    shards = [ffn1_w_gu, ffn1_w_down, _widen_w_in(mix_w_in), mix_w_out, ffn2_w_gu, ffn2_w_down]
    nbig = len(shards)
    dev_arr = jnp.reshape(dev, (1,)).astype(jnp.int32)
    conv_pack = _pack([lru_conv_w, ssd_conv_w])
    conv_buf = lax.dynamic_update_slice_in_dim(jnp.zeros((N_DEV,) + conv_pack.shape, f32), conv_pack[None], dev, axis=0)
    def gather_groups(l):
        return [(0, 1), (2, 3), (4, 5)] if l == 0 else [tuple(range(nbig))]

    placed, gather_started = {}, {}
    for l in range(nl):
        for gi, idx in enumerate(gather_groups(l)):
            placed[l, gi] = list(_place_shards([shards[i] for i in idx], l, dev_arr)) + (
                [conv_buf] if (l, gi) == (0, 1) else [])

    def start_gather(key):
        gather_started[key] = _split_start(f"gather_start_{key[0]}_{key[1]}", [], placed[key], (4 * len(placed[key]),),
                                           _gather_plan)

    start_gather((0, 0))

    def finish_gather(l, gi, after):
        waited = _split_wait(f"gather_wait_{l}_{gi}", gather_started[l, gi], 0, _gather_plan, after)
        return _gather_finish(list(waited))

    def conv_taps(conv_all):
        full = []
        for k, shape in enumerate(conv_shapes):
            per_dev = jnp.stack([_unpack(conv_all[s], conv_shapes)[k] for s in range(N_DEV)], axis=2)
            full.append(per_dev.reshape(shape[0], shape[1], N_DEV * shape[2]))
        return full

    def vec(a):
        return a.reshape(nl, 1, -1)

    def per_channel(a):
        return jnp.repeat(a, HEAD, axis=-1).reshape(nl, 1, B_W)

    eye = jnp.eye(A_W // HEAD, dtype=f32)

    def block_diag(a):
        return jnp.einsum("lhij,hg->lhigj", a, eye).reshape(nl, A_W, A_W).astype(bf16)

    causal = jnp.tril(jnp.ones((CHUNK, CHUNK), dtype=bool))
    p = dict(
        ffn1_pre=vec(ffn1_pre_g), ffn1_post=vec(ffn1_post_g), mix_pre=vec(mix_pre_g), mix_post=vec(mix_post_g),
        ffn2_pre=vec(ffn2_pre_g), ffn2_post=vec(ffn2_post_g),
        lru=(vec(lru_conv_b), block_diag(lru_w_r), block_diag(lru_w_i), vec(lru_b_r), vec(lru_b_i), vec(lru_lambda)),
        ssd=(vec(ssd_conv_b), per_channel(ssd_dt_bias), per_channel(ssd_a_log), per_channel(ssd_d), vec(ssd_norm_g)),
    )
    wm = jnp.where(causal, sgu_w_s, 0.0).astype(bf16)
    sgu_bias = jnp.repeat(jnp.swapaxes(sgu_b_s, 1, 2), HEAD, axis=2)
    sgu_f = (vec(sgu_ln_g), vec(sgu_ln_b), wm, sgu_bias)
    sgu_b = (vec(sgu_ln_g), vec(sgu_ln_b), wm, jnp.swapaxes(wm, 2, 3), sgu_bias)

    small_names = REPLICATED + SHARDED_CONV
    small_state = [_pack([src[n] for n in small_names])[None] for src in (w, mom, var)]
    prepared = [a for v in p.values() for a in (v if isinstance(v, tuple) else (v,))] + list(sgu_b) + small_state

    xs = x.reshape(t, d)
    saved, gathered, early_forward = [], [], {}
    for l in range(nl):
        x0 = xs
        if l == 0:
            later = [key for key in placed if key != (0, 0)]
            wgu1, wd1 = finish_gather(0, 0, [x0] + prepared + [b for key in later for b in placed[key]])
            for key in later:
                start_gather(key)
            deps = tuple(started[-1] for key, started in gather_started.items() if key != (0, 0))
        elif l in early_forward:
            wgu1, wd1, win, wout, wgu2, wd2 = _split_wait(f"forward_wait_{l}", early_forward[l], 0, _forward_plan, x0)
            deps = ()
        else:
            wgu1, wd1, win, wout, wgu2, wd2 = finish_gather(l, 0, x0)
            deps = ()
        x1, hb1, g1, u1, f1 = _ffn_fwd(x0, p["ffn1_pre"], p["ffn1_post"], wgu1, wd1, l, deps)
        if l == 0:
            win, wout, conv_all = finish_gather(0, 1, x1)
            lru_cw, ssd_cw = conv_taps(conv_all)
            p["lru"], p["ssd"] = (lru_cw,) + p["lru"], (ssd_cw,) + p["ssd"]
        hbm, pa, pb, pc = _mix_in_fwd(x1, p["mix_pre"], win, l)
        ya, h, gates = _lru_fwd(pa, *p["lru"], l)
        yb, yp, sp = _ssd_fwd(pb, *p["ssd"], l)
        yc = _sgu_fwd(pc, *sgu_f, l)
        x2, cat, m = _mix_out_fwd(x1, ya, yb, yc, p["mix_post"], wout, l)
        deps = ()
        if l == 0:
            wgu2, wd2 = finish_gather(0, 2, x2)
        elif l + 1 < nl:
            waited = _split_wait(f"gather_wait_{l + 1}_0", gather_started[l + 1, 0], 0, _gather_plan, x2)
            early_forward[l + 1] = _split_start(f"forward_start_{l + 1}", [], list(waited), (3 * nbig,), _forward_plan)
            deps = (early_forward[l + 1][-1],)
        xs, hb2, g2, u2, f2 = _ffn_fwd(x2, p["ffn2_pre"], p["ffn2_post"], wgu2, wd2, l, deps)
        gathered.append((wgu1, wd1, win, wout, wgu2, wd2))
        saved.append((x0, hb1, g1, u1, f1, x1, hbm, pa, pb, pc, h, gates, yp, sp, cat, m, x2, hb2, g2, u2, f2))
    dy, loss_part = _loss_fwd(xs, loss_target.reshape(t, d))
    loss = lax.psum(loss_part[0, 0], ("x", "y", "c"))

    small = {n: [None] * nl for n in REPLICATED + SHARDED_CONV}
    grads, delta, new_m, new_v = {}, {}, {}, {}
    fused = [n for n in BIG if n != "mix_w_in"]

    def oriented(a, n):
        return jnp.swapaxes(a, 1, 2) if n.endswith("w_gu") else a

    opt_in = {n: tuple(oriented(src[n], n) for src in (w, mom, var)) for n in fused}
    opt_out = {n: tuple(lax.empty(opt_in[n][0].shape, f32) for _ in range(4)) for n in fused}
    w_in_grads = [None] * nl
    grad_shapes = {n: (s.shape[2], s.shape[1]) if n.endswith("w_gu") else s.shape[1:] for n, s in zip(BIG, shards)}

    def start_pair(tag, lp, names, gbuf):
        landing = [lax.empty((4, 1) + grad_shapes[n], bf16) for n in names]
        started = _split_start(f"pair_start_{tag}", [gbuf[n] for n in names], landing, (4 * len(names),), _pair_plan)
        return tag, lp, names, started

    def finish_pair(pending, after):
        tag, lp, names, started = pending
        k = len(names)
        done = _split_wait(f"pair_wait_{tag}", started, k, _pair_plan, after)
        sums = [_pair_add(g, r, pair_idx) for g, r in zip(done[:k], done[k:])]
        landing = [lax.empty(s.shape, bf16) for s in sums]
        started = _split_start(f"chips_start_{tag}", sums, landing, (3 * k,), _chips_plan)
        return tag, lp, names, started, done[:k], done[k:]

    def finish_chips(pending, after, deps=()):
        tag, lp, names, started, local, from_sibling = pending
        k = len(names)
        done = _split_wait(f"chips_wait_{tag}", started, k, _chips_plan, after)
        last = None
        for n, gl, r, q in zip(names, local, from_sibling, done[k:]):
            if n == "mix_w_in":
                w_in_grads[lp] = last = _grad_sum(gl, r, q, own_idx)
            else:
                opt_out[n] = tuple(_adamw_layer(*opt_in[n], gl, r, q, own_idx, opt_out[n], lp, deps))
                last = opt_out[n][0]
        return last

    early = ("ffn2_w_gu", "ffn2_w_down", "mix_w_out")
    late = ("mix_w_in", "ffn1_w_gu", "ffn1_w_down")
    pending_pair = pending_chips = early_pair = early_chips = upper_started = None
    deferred = []
    names = REPLICATED + SHARDED_CONV
    assert nl > 1
    for l in reversed(range(nl)):
        x0, hb1, g1, u1, f1, x1, hbm, pa, pb, pc, h, gates, yp, sp, cat, m, x2, hb2, g2, u2, f2 = saved[l]
        wgu1, wd1, win, wout, wgu2, wd2 = gathered[l][:nbig]
        gbuf ={n: lax.empty((N_DEV, 1) + grad_shapes[n], bf16) for n in BIG}
        deps = () if pending_pair is None else (pending_pair[3][-1],)
        if l == 0:
            deps += (upper_started[-1],)
        dx2, dfb, act, dg, du, dpre, dpost = _ffn_bwd(x2, dy, f2, p["ffn2_pre"], p["ffn2_post"], g2, u2, wgu2, wd2, l, deps)
        small["ffn2_pre_g"][l], small["ffn2_post_g"][l] = dpre[0], dpost[0]
        gbuf["ffn2_w_gu"] = _wgrad_cols(hb2, dg, gbuf["ffn2_w_gu"], 0, 0)
        gbuf["ffn2_w_gu"] = _wgrad_cols(hb2, du, gbuf["ffn2_w_gu"], 0, dg.shape[0])
        gbuf["ffn2_w_down"] = _wgrad_rows(act, dfb, gbuf["ffn2_w_down"], 0)
        deps = ()
        if pending_pair is not None:
            pending_chips = finish_pair(pending_pair, dx2)
            deps = (pending_chips[3][-1],)

        dm, dya, dyb, dyc, dpost = _mix_out_bwd(dx2, m, p["mix_post"], wout, l, deps)
        small["mix_post_g"][l] = dpost[0]
        gbuf["mix_w_out"] = _wgrad_kblocks(cat, [dm], gbuf["mix_w_out"], 0)
        deps = ()
        if l == 0:
            early_pair = start_pair("0a", 0, early, gbuf)
            deps = (early_pair[3][-1],)
        dpc, dws, dbias, dlg, dlb = _sgu_bwd(pc, dyc, *sgu_b, l, deps)
        small["sgu_w_s"][l] = jnp.where(causal, dws, 0.0)
        small["sgu_b_s"][l] = dbias.reshape(CHUNK, C_W // HEAD, HEAD).sum(-1).T
        small["sgu_ln_g"][l], small["sgu_ln_b"][l] = dlg[0], dlb[0]
        dpb, dcw, dcb, ddtb, dalog, ddsk, dng = _ssd_bwd(pb, yp, sp, dyb, *p["ssd"], l)
        small["ssd_conv_w"][l], small["ssd_conv_b"][l], small["ssd_norm_g"][l] = dcw, dcb[0], dng[0]
        small["ssd_dt_bias"][l], small["ssd_a_log"][l], small["ssd_d"][l] = _per_head(ddtb[0]), _per_head(dalog[0]), _per_head(ddsk[0])
        deps = ()
        if l == 0:
            early_chips = finish_pair(early_pair, dpb)
            deps = (early_chips[3][-1],)
        dpa, dcw, dcb, dwr, dwi, dbr, dbi, dlam = _lru_bwd(pa, h, gates, dya, *p["lru"], l, deps)
        small["lru_conv_w"][l], small["lru_conv_b"][l], small["lru_lambda"][l] = dcw, dcb[0], dlam[0]
        small["lru_b_r"][l], small["lru_b_i"][l] = dbr[0], dbi[0]
        heads = range(A_W // HEAD)
        small["lru_w_r"][l] = jnp.stack([dwr[HEAD * i:HEAD * (i + 1), HEAD * i:HEAD * (i + 1)] for i in heads])
        small["lru_w_i"][l] = jnp.stack([dwi[HEAD * i:HEAD * (i + 1), HEAD * i:HEAD * (i + 1)] for i in heads])
        dx1, dpre = _mix_in_bwd(x1, dx2, p["mix_pre"], dpa, dpb, dpc, win, l)
        small["mix_pre_g"][l] = dpre[0]
        gbuf["mix_w_in"] = _wgrad_kblocks(hbm, [dpa, dpb, dpc], gbuf["mix_w_in"], 0)

        dy, dfb, act, dg, du, dpre, dpost = _ffn_bwd(x0, dx1, f1, p["ffn1_pre"], p["ffn1_post"], g1, u1, wgu1, wd1, l)
        small["ffn1_pre_g"][l], small["ffn1_post_g"][l] = dpre[0], dpost[0]
        gbuf["ffn1_w_gu"] = _wgrad_cols(hb1, dg, gbuf["ffn1_w_gu"], 0, 0)
        gbuf["ffn1_w_gu"] = _wgrad_cols(hb1, du, gbuf["ffn1_w_gu"], 0, dg.shape[0])
        gbuf["ffn1_w_down"] = _wgrad_rows(act, dfb, gbuf["ffn1_w_down"], 0)
        if pending_chips is not None:
            deferred.append(pending_chips)
            pending_chips = None
        pending_pair = start_pair(f"{l}", l, late if l == 0 else BIG, gbuf)
        if l == 1:
            upper = [jnp.stack(small[n][1:]) for n in names]
            upper_pack = _pack(upper)
            upper_buf = lax.dynamic_update_slice_in_dim(
                jnp.zeros((N_DEV,) + upper_pack.shape, f32), upper_pack[None], dev, axis=0)
            upper_started = _split_start("small_start", [], [upper_buf], (4,), _gather_plan)
    grad_x = dy.reshape(x.shape)

    lower = [jnp.stack(small[n][:1]) for n in names]
    lower_total = _sum_devices(_all_gather([_pack(lower)], (pending_pair[3][-1],))[0])
    late_chips = finish_pair(pending_pair, lower_total)
    order = lower_total
    for pending in deferred + [early_chips]:
        order = finish_chips(pending, order, (late_chips[3][-1],))
    upper_all = _gather_finish(list(_split_wait("small_wait", upper_started, 0, _gather_plan, order)))[0]
    upper_total = _sum_devices(upper_all)
    finish_chips(late_chips, [upper_total] + [opt_out[n][0] for n in fused] + [g for g in w_in_grads if g is not None])
    full = {n: jnp.concatenate([lo, up], axis=0) for n, lo, up in zip(
        names, _unpack(lower_total, [a.shape for a in lower]), _unpack(upper_total, [a.shape for a in upper]))}

    for n in fused:
        grads[n], delta[n], new_m[n], new_v[n] = (oriented(a, n) for a in opt_out[n])
    grads["mix_w_in"] = _narrow_w_in_grad(jnp.concatenate(w_in_grads, axis=0))
    delta["mix_w_in"], new_m["mix_w_in"], new_v["mix_w_in"] = _adamw(
        w["mix_w_in"], mom["mix_w_in"], var["mix_w_in"], grads["mix_w_in"])
    for n in REPLICATED:
        grads[n] = full[n]
    for n in SHARDED_CONV:
        cols = w[n].shape[2]
        grads[n] = lax.dynamic_slice_in_dim(full[n], dev * cols, cols, axis=2)
    shapes = [w[n].shape for n in names]
    packs = small_state + [_pack([grads[n] for n in names])[None]]
    for dst, packed in zip((delta, new_m, new_v), _adamw(*packs)):
        dst.update(zip(names, _unpack(packed[0], shapes)))

    return (loss, grad_x, *[grads[n] for n in WEIGHTS], *[delta[n] for n in WEIGHTS],
            *[new_m[n] for n in WEIGHTS], *[new_v[n] for n in WEIGHTS])
```
